```python
import jax, jax.numpy as jnp
from jax import lax
import numpy as np

D_MODEL = 1024
BATCH = 32
SEQ = 2048
DEPTH = 1

CHUNK = 64
Q_BLOCK = 128
N_HEADS = 8
QK_NOPE_DIM = 64
QK_ROPE_DIM = 32
QK_HEAD_DIM = QK_NOPE_DIM + QK_ROPE_DIM
V_HEAD_DIM = 64
Q_LORA_RANK = 256
KV_LORA_RANK = 128
MLA_WIDTH = N_HEADS * V_HEAD_DIM
CONV_CH = 512
CONV_WIDTH = 31
D_FF = 4 * D_MODEL
N_BRANCH = 2
ADA_CHUNKS = 6
ROPE_THETA = 10000.0
EPS = 1e-6

OFF_Q = Q_LORA_RANK
OFF_KV = OFF_Q + KV_LORA_RANK
OFF_KR = OFF_KV + QK_ROPE_DIM
OFF_GLU = OFF_KR + 2 * CONV_CH
D_IN = OFF_GLU + N_BRANCH * D_MODEL

kernel_name = "hybrid_mla_conformer_conv_adaln_block"


def rms_norm(x, g):
    xf = x.astype(jnp.float32)
    y = xf * lax.rsqrt(jnp.mean(jnp.square(xf), axis=-1, keepdims=True) + EPS)
    return (y * g.astype(jnp.float32)).astype(x.dtype)


def layer_norm(x, g, b):
    xf = x.astype(jnp.float32)
    mu = jnp.mean(xf, axis=-1, keepdims=True)
    var = jnp.mean(jnp.square(xf - mu), axis=-1, keepdims=True)
    y = (xf - mu) * lax.rsqrt(var + EPS)
    return (y * g.astype(jnp.float32) + b.astype(jnp.float32)).astype(x.dtype)


def rope_tables(seq, dtype):
    inv_freq = ROPE_THETA ** (-jnp.arange(0, QK_ROPE_DIM, 2, dtype=jnp.float32) / QK_ROPE_DIM)
    ang = jnp.arange(seq, dtype=jnp.float32)[:, None] * inv_freq[None, :]
    return jnp.cos(ang)[:, None, :].astype(dtype), jnp.sin(ang)[:, None, :].astype(dtype)


def apply_rope(x, cos, sin):
    half = x.shape[-1] // 2
    x1, x2 = x[..., :half], x[..., half:]
    return jnp.concatenate([x1 * cos - x2 * sin, x2 * cos + x1 * sin], axis=-1)


def chunk_causal_attention(q, k, v):
    seq = q.shape[1]
    scale = QK_HEAD_DIM ** -0.5
    chunk_id = jnp.arange(seq) // CHUNK
    outs = []
    for q0 in range(0, seq, Q_BLOCK):
        kv_end = q0 + Q_BLOCK
        qb = q[:, q0:kv_end]
        kb = k[:, :kv_end]
        vb = v[:, :kv_end]
        s = jnp.einsum('bqhd,bkhd->bhqk', qb, kb).astype(jnp.float32) * scale
        mask = chunk_id[q0:kv_end][:, None] >= chunk_id[:kv_end][None, :]
        s = jnp.where(mask[None, None], s, jnp.finfo(jnp.float32).min)
        p = jax.nn.softmax(s, axis=-1).astype(v.dtype)
        outs.append(jnp.einsum('bhqk,bkhd->bqhd', p, vb))
    return jnp.concatenate(outs, axis=1)


def causal_depthwise_conv(u, w, b):
    out = lax.conv_general_dilated(
        u, w[:, None, :].astype(u.dtype), window_strides=(1,),
        padding=[(CONV_WIDTH - 1, 0)],
        dimension_numbers=('NWC', 'WIO', 'NWC'),
        feature_group_count=u.shape[-1])
    return out + b


def _fwd_setup_inputs(seed: int = 0) -> dict:
    key = jax.random.key(seed)
    ks = jax.random.split(key, 24)
    f32 = jnp.float32
    L = DEPTH

    def nrm(k, shape, fan_in):
        return jax.random.normal(k, shape, f32) * (fan_in ** -0.5)

    def gain(k, shape):
        return 1.0 + 0.02 * jax.random.normal(k, shape, f32)

    return {
        "x": jax.random.normal(ks[0], (BATCH, SEQ, D_MODEL), f32),
        "c": jax.random.normal(ks[1], (BATCH, D_MODEL), f32),
        "w_ada": nrm(ks[2], (L, D_MODEL, ADA_CHUNKS * D_MODEL), D_MODEL),
        "b_ada": 0.02 * jax.random.normal(ks[3], (L, ADA_CHUNKS * D_MODEL), f32),
        "norm1_g": gain(ks[4], (L, D_MODEL)),
        "w_in": nrm(ks[5], (L, D_MODEL, D_IN), D_MODEL),
        "q_latent_g": gain(ks[6], (L, Q_LORA_RANK)),
        "w_uq": nrm(ks[7], (L, Q_LORA_RANK, N_HEADS * QK_HEAD_DIM), Q_LORA_RANK),
        "kv_latent_g": gain(ks[8], (L, KV_LORA_RANK)),
        "w_ukv": nrm(ks[9], (L, KV_LORA_RANK, N_HEADS * (QK_NOPE_DIM + V_HEAD_DIM)), KV_LORA_RANK),
        "qk_norm_q_g": gain(ks[10], (L, QK_HEAD_DIM)),
        "qk_norm_k_g": gain(ks[11], (L, QK_HEAD_DIM)),
        "w_o_mla": nrm(ks[12], (L, MLA_WIDTH, D_MODEL), MLA_WIDTH),
        "conv_w": nrm(ks[13], (L, CONV_WIDTH, CONV_CH), CONV_WIDTH),
        "conv_b": 0.02 * jax.random.normal(ks[14], (L, CONV_CH), f32),
        "conv_ln_g": gain(ks[15], (L, CONV_CH)),
        "conv_ln_b": 0.02 * jax.random.normal(ks[16], (L, CONV_CH), f32),
        "w_pw_out": nrm(ks[17], (L, CONV_CH, D_MODEL), CONV_CH),
        "w_out": nrm(ks[18], (L, D_MODEL, D_MODEL), D_MODEL),
        "norm2_g": gain(ks[19], (L, D_MODEL)),
        "w_ff1": nrm(ks[20], (L, D_MODEL, D_FF), D_MODEL),
        "w_ff2": nrm(ks[21], (L, D_FF, D_MODEL), D_FF),
    }


def _fwd_reference(x, c, w_ada, b_ada, norm1_g, w_in, q_latent_g, w_uq, kv_latent_g, w_ukv,
              qk_norm_q_g, qk_norm_k_g, w_o_mla, conv_w, conv_b, conv_ln_g, conv_ln_b,
              w_pw_out, w_out, norm2_g, w_ff1, w_ff2):
    B, S, D = x.shape
    cos, sin = rope_tables(S, x.dtype)
    c_act = jax.nn.silu(c)
    for l in range(DEPTH):
        mod = c_act @ w_ada[l] + b_ada[l]
        shift1, scale1, gate1, shift2, scale2, gate2 = jnp.split(mod[:, None, :], ADA_CHUNKS, axis=-1)

        h = rms_norm(x, norm1_g[l]) * (1.0 + scale1) + shift1
        z = h @ w_in[l]
        z_q = z[..., :OFF_Q]
        z_kv = z[..., OFF_Q:OFF_KV]
        z_kr = z[..., OFF_KV:OFF_KR]
        z_glu = z[..., OFF_KR:OFF_GLU]
        z_gate = z[..., OFF_GLU:]

        q = (rms_norm(z_q, q_latent_g[l]) @ w_uq[l]).reshape(B, S, N_HEADS, QK_HEAD_DIM)
        kv = (rms_norm(z_kv, kv_latent_g[l]) @ w_ukv[l]).reshape(B, S, N_HEADS, QK_NOPE_DIM + V_HEAD_DIM)
        k_nope, v = kv[..., :QK_NOPE_DIM], kv[..., QK_NOPE_DIM:]
        k_rope = jnp.broadcast_to(z_kr[:, :, None, :], (B, S, N_HEADS, QK_ROPE_DIM))
        k = jnp.concatenate([k_nope, k_rope], axis=-1)
        q = rms_norm(q, qk_norm_q_g[l])
        k = rms_norm(k, qk_norm_k_g[l])
        q = jnp.concatenate([q[..., :QK_NOPE_DIM], apply_rope(q[..., QK_NOPE_DIM:], cos, sin)], axis=-1)
        k = jnp.concatenate([k[..., :QK_NOPE_DIM], apply_rope(k[..., QK_NOPE_DIM:], cos, sin)], axis=-1)
        attn = chunk_causal_attention(q, k, v).reshape(B, S, MLA_WIDTH)
        y_a = attn @ w_o_mla[l]

        glu_a, glu_b = jnp.split(z_glu, 2, axis=-1)
        u = glu_a * jax.nn.sigmoid(glu_b)
        u = causal_depthwise_conv(u, conv_w[l], conv_b[l])
        u = jax.nn.silu(layer_norm(u, conv_ln_g[l], conv_ln_b[l]))
        y_b = u @ w_pw_out[l]

        g_a, g_b = jnp.split(jax.nn.sigmoid(z_gate), N_BRANCH, axis=-1)
        mixed = (g_a * y_a + g_b * y_b) @ w_out[l]
        x = x + gate1 * mixed

        h2 = rms_norm(x, norm2_g[l]) * (1.0 + scale2) + shift2
        f = jnp.square(jax.nn.relu(h2 @ w_ff1[l])) @ w_ff2[l]
        x = x + gate2 * f
    return x


import jax as _jax
import jax.numpy as _jnp

TWIN_FORMAT = 'train_step'
FWD_PARAMS = ['x', 'c', 'w_ada', 'b_ada', 'norm1_g', 'w_in', 'q_latent_g', 'w_uq', 'kv_latent_g', 'w_ukv', 'qk_norm_q_g', 'qk_norm_k_g', 'w_o_mla', 'conv_w', 'conv_b', 'conv_ln_g', 'conv_ln_b', 'w_pw_out', 'w_out', 'norm2_g', 'w_ff1', 'w_ff2']
TWIN_WEIGHTS = ['w_ada', 'b_ada', 'norm1_g', 'w_in', 'q_latent_g', 'w_uq', 'kv_latent_g', 'w_ukv', 'qk_norm_q_g', 'qk_norm_k_g', 'w_o_mla', 'conv_w', 'conv_b', 'conv_ln_g', 'conv_ln_b', 'w_pw_out', 'w_out', 'norm2_g', 'w_ff1', 'w_ff2']
TWIN_DIFF_INPUT = 'x'
TWIN_INPUTS = ['x', 'c', 'w_ada', 'b_ada', 'norm1_g', 'w_in', 'q_latent_g', 'w_uq', 'kv_latent_g', 'w_ukv', 'qk_norm_q_g', 'qk_norm_k_g', 'w_o_mla', 'conv_w', 'conv_b', 'conv_ln_g', 'conv_ln_b', 'w_pw_out', 'w_out', 'norm2_g', 'w_ff1', 'w_ff2', 'loss_target', 'm_w_ada', 'm_b_ada', 'm_norm1_g', 'm_w_in', 'm_q_latent_g', 'm_w_uq', 'm_kv_latent_g', 'm_w_ukv', 'm_qk_norm_q_g', 'm_qk_norm_k_g', 'm_w_o_mla', 'm_conv_w', 'm_conv_b', 'm_conv_ln_g', 'm_conv_ln_b', 'm_w_pw_out', 'm_w_out', 'm_norm2_g', 'm_w_ff1', 'm_w_ff2', 'v_w_ada', 'v_b_ada', 'v_norm1_g', 'v_w_in', 'v_q_latent_g', 'v_w_uq', 'v_kv_latent_g', 'v_w_ukv', 'v_qk_norm_q_g', 'v_qk_norm_k_g', 'v_w_o_mla', 'v_conv_w', 'v_conv_b', 'v_conv_ln_g', 'v_conv_ln_b', 'v_w_pw_out', 'v_w_out', 'v_norm2_g', 'v_w_ff1', 'v_w_ff2']
TWIN_OUTPUTS = ['loss', 'grad_x', 'grad_w_ada', 'grad_b_ada', 'grad_norm1_g', 'grad_w_in', 'grad_q_latent_g', 'grad_w_uq', 'grad_kv_latent_g', 'grad_w_ukv', 'grad_qk_norm_q_g', 'grad_qk_norm_k_g', 'grad_w_o_mla', 'grad_conv_w', 'grad_conv_b', 'grad_conv_ln_g', 'grad_conv_ln_b', 'grad_w_pw_out', 'grad_w_out', 'grad_norm2_g', 'grad_w_ff1', 'grad_w_ff2', 'delta_w_ada', 'delta_b_ada', 'delta_norm1_g', 'delta_w_in', 'delta_q_latent_g', 'delta_w_uq', 'delta_kv_latent_g', 'delta_w_ukv', 'delta_qk_norm_q_g', 'delta_qk_norm_k_g', 'delta_w_o_mla', 'delta_conv_w', 'delta_conv_b', 'delta_conv_ln_g', 'delta_conv_ln_b', 'delta_w_pw_out', 'delta_w_out', 'delta_norm2_g', 'delta_w_ff1', 'delta_w_ff2', 'new_m_w_ada', 'new_m_b_ada', 'new_m_norm1_g', 'new_m_w_in', 'new_m_q_latent_g', 'new_m_w_uq', 'new_m_kv_latent_g', 'new_m_w_ukv', 'new_m_qk_norm_q_g', 'new_m_qk_norm_k_g', 'new_m_w_o_mla', 'new_m_conv_w', 'new_m_conv_b', 'new_m_conv_ln_g', 'new_m_conv_ln_b', 'new_m_w_pw_out', 'new_m_w_out', 'new_m_norm2_g', 'new_m_w_ff1', 'new_m_w_ff2', 'new_v_w_ada', 'new_v_b_ada', 'new_v_norm1_g', 'new_v_w_in', 'new_v_q_latent_g', 'new_v_w_uq', 'new_v_kv_latent_g', 'new_v_w_ukv', 'new_v_qk_norm_q_g', 'new_v_qk_norm_k_g', 'new_v_w_o_mla', 'new_v_conv_w', 'new_v_conv_b', 'new_v_conv_ln_g', 'new_v_conv_ln_b', 'new_v_w_pw_out', 'new_v_w_out', 'new_v_norm2_g', 'new_v_w_ff1', 'new_v_w_ff2']
TWIN_LEAF_KINDS = {'loss': 'loss', 'grad_x': 'grad_x', 'grad_w_ada': 'grad_w', 'grad_b_ada': 'grad_w', 'grad_norm1_g': 'grad_w', 'grad_w_in': 'grad_w', 'grad_q_latent_g': 'grad_w', 'grad_w_uq': 'grad_w', 'grad_kv_latent_g': 'grad_w', 'grad_w_ukv': 'grad_w', 'grad_qk_norm_q_g': 'grad_w', 'grad_qk_norm_k_g': 'grad_w', 'grad_w_o_mla': 'grad_w', 'grad_conv_w': 'grad_w', 'grad_conv_b': 'grad_w', 'grad_conv_ln_g': 'grad_w', 'grad_conv_ln_b': 'grad_w', 'grad_w_pw_out': 'grad_w', 'grad_w_out': 'grad_w', 'grad_norm2_g': 'grad_w', 'grad_w_ff1': 'grad_w', 'grad_w_ff2': 'grad_w', 'delta_w_ada': 'delta_w', 'delta_b_ada': 'delta_w', 'delta_norm1_g': 'delta_w', 'delta_w_in': 'delta_w', 'delta_q_latent_g': 'delta_w', 'delta_w_uq': 'delta_w', 'delta_kv_latent_g': 'delta_w', 'delta_w_ukv': 'delta_w', 'delta_qk_norm_q_g': 'delta_w', 'delta_qk_norm_k_g': 'delta_w', 'delta_w_o_mla': 'delta_w', 'delta_conv_w': 'delta_w', 'delta_conv_b': 'delta_w', 'delta_conv_ln_g': 'delta_w', 'delta_conv_ln_b': 'delta_w', 'delta_w_pw_out': 'delta_w', 'delta_w_out': 'delta_w', 'delta_norm2_g': 'delta_w', 'delta_w_ff1': 'delta_w', 'delta_w_ff2': 'delta_w', 'new_m_w_ada': 'new_m', 'new_m_b_ada': 'new_m', 'new_m_norm1_g': 'new_m', 'new_m_w_in': 'new_m', 'new_m_q_latent_g': 'new_m', 'new_m_w_uq': 'new_m', 'new_m_kv_latent_g': 'new_m', 'new_m_w_ukv': 'new_m', 'new_m_qk_norm_q_g': 'new_m', 'new_m_qk_norm_k_g': 'new_m', 'new_m_w_o_mla': 'new_m', 'new_m_conv_w': 'new_m', 'new_m_conv_b': 'new_m', 'new_m_conv_ln_g': 'new_m', 'new_m_conv_ln_b': 'new_m', 'new_m_w_pw_out': 'new_m', 'new_m_w_out': 'new_m', 'new_m_norm2_g': 'new_m', 'new_m_w_ff1': 'new_m', 'new_m_w_ff2': 'new_m', 'new_v_w_ada': 'new_v', 'new_v_b_ada': 'new_v', 'new_v_norm1_g': 'new_v', 'new_v_w_in': 'new_v', 'new_v_q_latent_g': 'new_v', 'new_v_w_uq': 'new_v', 'new_v_kv_latent_g': 'new_v', 'new_v_w_ukv': 'new_v', 'new_v_qk_norm_q_g': 'new_v', 'new_v_qk_norm_k_g': 'new_v', 'new_v_w_o_mla': 'new_v', 'new_v_conv_w': 'new_v', 'new_v_conv_b': 'new_v', 'new_v_conv_ln_g': 'new_v', 'new_v_conv_ln_b': 'new_v', 'new_v_w_pw_out': 'new_v', 'new_v_w_out': 'new_v', 'new_v_norm2_g': 'new_v', 'new_v_w_ff1': 'new_v', 'new_v_w_ff2': 'new_v'}


def _forward(args):
    return _fwd_reference(*[args[k] for k in FWD_PARAMS])


def _output_shape():
    out = _jax.eval_shape(lambda: _forward(_fwd_setup_inputs(0)))
    return out.shape, out.dtype

N_MICROBATCH = 1
ADAM_LR = 0.001
ADAM_B1 = 0.9
ADAM_B2 = 0.999
ADAM_EPS = 1e-08
ADAM_WD = 0.01
ADAM_STEP = 10
PER_EXAMPLE_BATCH_AXIS = {'x': 0, 'c': 0, 'loss_target': 0}
SHARED_INPUTS = []
_WEIGHT_DTYPES = {'w_ada': _jnp.float32, 'b_ada': _jnp.float32, 'norm1_g': _jnp.float32, 'w_in': _jnp.float32, 'q_latent_g': _jnp.float32, 'w_uq': _jnp.float32, 'kv_latent_g': _jnp.float32, 'w_ukv': _jnp.float32, 'qk_norm_q_g': _jnp.float32, 'qk_norm_k_g': _jnp.float32, 'w_o_mla': _jnp.float32, 'conv_w': _jnp.float32, 'conv_b': _jnp.float32, 'conv_ln_g': _jnp.float32, 'conv_ln_b': _jnp.float32, 'w_pw_out': _jnp.float32, 'w_out': _jnp.float32, 'norm2_g': _jnp.float32, 'w_ff1': _jnp.float32, 'w_ff2': _jnp.float32}
MOMENT_SCALE = {'w_ada': 3.682327e+01, 'b_ada': 7.917789e+01, 'norm1_g': 8.105602e-01, 'w_in': 6.995120e+00, 'q_latent_g': 5.314571e-01, 'w_uq': 4.676237e-01, 'kv_latent_g': 3.690075e+01, 'w_ukv': 1.126309e+01, 'qk_norm_q_g': 1.305517e+00, 'qk_norm_k_g': 1.221964e+00, 'w_o_mla': 1.103542e+01, 'conv_w': 6.647315e+00, 'conv_b': 2.269694e+01, 'conv_ln_g': 1.509261e+01, 'conv_ln_b': 1.913063e+01, 'w_pw_out': 7.134951e+00, 'w_out': 1.282600e+01, 'norm2_g': 1.983342e+02, 'w_ff1': 1.481307e+01, 'w_ff2': 3.027657e+01}


def _to_microbatches(a, axis):
    t = _jnp.moveaxis(a, axis, 0)
    t = t.reshape((N_MICROBATCH, t.shape[0] // N_MICROBATCH) + t.shape[1:])
    return _jnp.moveaxis(t, 1, axis + 1)


def setup_inputs(seed: int = 0) -> dict:
    inp = _fwd_setup_inputs(seed)
    key = _jax.random.fold_in(_jax.random.key(seed), 7919)
    shape, _ = _output_shape()
    out = dict(inp)
    out["loss_target"] = _jax.random.normal(_jax.random.fold_in(key, 0), shape, _jnp.float32)
    for i, name in enumerate(TWIN_WEIGHTS):
        w = inp[name].astype(_jnp.float32)
        if MOMENT_SCALE is None:
            s = _jnp.sqrt(_jnp.mean(_jnp.square(w)) + 1e-30)
        else:
            s = MOMENT_SCALE[name]
        km, kv = _jax.random.split(_jax.random.fold_in(key, i + 1))
        out[name] = w
        out["m_" + name] = s * _jax.random.normal(km, w.shape, _jnp.float32)
        out["v_" + name] = (s * s) * _jax.random.uniform(kv, w.shape, _jnp.float32, 0.5, 1.5)
    if N_MICROBATCH > 1:
        for name, axis in PER_EXAMPLE_BATCH_AXIS.items():
            out[name] = _to_microbatches(out[name], axis)
    return {'x': out['x'], 'c': out['c'], 'w_ada': out['w_ada'], 'b_ada': out['b_ada'], 'norm1_g': out['norm1_g'], 'w_in': out['w_in'], 'q_latent_g': out['q_latent_g'], 'w_uq': out['w_uq'], 'kv_latent_g': out['kv_latent_g'], 'w_ukv': out['w_ukv'], 'qk_norm_q_g': out['qk_norm_q_g'], 'qk_norm_k_g': out['qk_norm_k_g'], 'w_o_mla': out['w_o_mla'], 'conv_w': out['conv_w'], 'conv_b': out['conv_b'], 'conv_ln_g': out['conv_ln_g'], 'conv_ln_b': out['conv_ln_b'], 'w_pw_out': out['w_pw_out'], 'w_out': out['w_out'], 'norm2_g': out['norm2_g'], 'w_ff1': out['w_ff1'], 'w_ff2': out['w_ff2'], 'loss_target': out['loss_target'], 'm_w_ada': out['m_w_ada'], 'm_b_ada': out['m_b_ada'], 'm_norm1_g': out['m_norm1_g'], 'm_w_in': out['m_w_in'], 'm_q_latent_g': out['m_q_latent_g'], 'm_w_uq': out['m_w_uq'], 'm_kv_latent_g': out['m_kv_latent_g'], 'm_w_ukv': out['m_w_ukv'], 'm_qk_norm_q_g': out['m_qk_norm_q_g'], 'm_qk_norm_k_g': out['m_qk_norm_k_g'], 'm_w_o_mla': out['m_w_o_mla'], 'm_conv_w': out['m_conv_w'], 'm_conv_b': out['m_conv_b'], 'm_conv_ln_g': out['m_conv_ln_g'], 'm_conv_ln_b': out['m_conv_ln_b'], 'm_w_pw_out': out['m_w_pw_out'], 'm_w_out': out['m_w_out'], 'm_norm2_g': out['m_norm2_g'], 'm_w_ff1': out['m_w_ff1'], 'm_w_ff2': out['m_w_ff2'], 'v_w_ada': out['v_w_ada'], 'v_b_ada': out['v_b_ada'], 'v_norm1_g': out['v_norm1_g'], 'v_w_in': out['v_w_in'], 'v_q_latent_g': out['v_q_latent_g'], 'v_w_uq': out['v_w_uq'], 'v_kv_latent_g': out['v_kv_latent_g'], 'v_w_ukv': out['v_w_ukv'], 'v_qk_norm_q_g': out['v_qk_norm_q_g'], 'v_qk_norm_k_g': out['v_qk_norm_k_g'], 'v_w_o_mla': out['v_w_o_mla'], 'v_conv_w': out['v_conv_w'], 'v_conv_b': out['v_conv_b'], 'v_conv_ln_g': out['v_conv_ln_g'], 'v_conv_ln_b': out['v_conv_ln_b'], 'v_w_pw_out': out['v_w_pw_out'], 'v_w_out': out['v_w_out'], 'v_norm2_g': out['v_norm2_g'], 'v_w_ff1': out['v_w_ff1'], 'v_w_ff2': out['v_w_ff2']}


def _loss(weights, diff, rest, loss_target):
    with _jax.named_scope("forward"):
        args = {**rest, TWIN_DIFF_INPUT: diff, **{k: w.astype(_WEIGHT_DTYPES[k]) for k, w in weights.items()}}
        y = _forward(args)
    with _jax.named_scope("loss_head"):
        err = _jnp.square(y.astype(_jnp.float32) - loss_target)
        return 0.5 * _jnp.sum(_jnp.mean(err, axis=-1)) if err.ndim else 0.5 * err


def _adamw(w, g, m, v):
    m = ADAM_B1 * m + (1.0 - ADAM_B1) * g
    v = ADAM_B2 * v + (1.0 - ADAM_B2) * _jnp.square(g)
    m_hat = m / (1.0 - ADAM_B1 ** ADAM_STEP)
    v_hat = v / (1.0 - ADAM_B2 ** ADAM_STEP)
    delta = -ADAM_LR * (m_hat / (_jnp.sqrt(v_hat) + ADAM_EPS) + ADAM_WD * w)
    return delta, m, v


def reference(x, c, w_ada, b_ada, norm1_g, w_in, q_latent_g, w_uq, kv_latent_g, w_ukv, qk_norm_q_g, qk_norm_k_g, w_o_mla, conv_w, conv_b, conv_ln_g, conv_ln_b, w_pw_out, w_out, norm2_g, w_ff1, w_ff2, loss_target, m_w_ada, m_b_ada, m_norm1_g, m_w_in, m_q_latent_g, m_w_uq, m_kv_latent_g, m_w_ukv, m_qk_norm_q_g, m_qk_norm_k_g, m_w_o_mla, m_conv_w, m_conv_b, m_conv_ln_g, m_conv_ln_b, m_w_pw_out, m_w_out, m_norm2_g, m_w_ff1, m_w_ff2, v_w_ada, v_b_ada, v_norm1_g, v_w_in, v_q_latent_g, v_w_uq, v_kv_latent_g, v_w_ukv, v_qk_norm_q_g, v_qk_norm_k_g, v_w_o_mla, v_conv_w, v_conv_b, v_conv_ln_g, v_conv_ln_b, v_w_pw_out, v_w_out, v_norm2_g, v_w_ff1, v_w_ff2):
    given = dict(x=x, c=c, w_ada=w_ada, b_ada=b_ada, norm1_g=norm1_g, w_in=w_in, q_latent_g=q_latent_g, w_uq=w_uq, kv_latent_g=kv_latent_g, w_ukv=w_ukv, qk_norm_q_g=qk_norm_q_g, qk_norm_k_g=qk_norm_k_g, w_o_mla=w_o_mla, conv_w=conv_w, conv_b=conv_b, conv_ln_g=conv_ln_g, conv_ln_b=conv_ln_b, w_pw_out=w_pw_out, w_out=w_out, norm2_g=norm2_g, w_ff1=w_ff1, w_ff2=w_ff2, loss_target=loss_target, m_w_ada=m_w_ada, m_b_ada=m_b_ada, m_norm1_g=m_norm1_g, m_w_in=m_w_in, m_q_latent_g=m_q_latent_g, m_w_uq=m_w_uq, m_kv_latent_g=m_kv_latent_g, m_w_ukv=m_w_ukv, m_qk_norm_q_g=m_qk_norm_q_g, m_qk_norm_k_g=m_qk_norm_k_g, m_w_o_mla=m_w_o_mla, m_conv_w=m_conv_w, m_conv_b=m_conv_b, m_conv_ln_g=m_conv_ln_g, m_conv_ln_b=m_conv_ln_b, m_w_pw_out=m_w_pw_out, m_w_out=m_w_out, m_norm2_g=m_norm2_g, m_w_ff1=m_w_ff1, m_w_ff2=m_w_ff2, v_w_ada=v_w_ada, v_b_ada=v_b_ada, v_norm1_g=v_norm1_g, v_w_in=v_w_in, v_q_latent_g=v_q_latent_g, v_w_uq=v_w_uq, v_kv_latent_g=v_kv_latent_g, v_w_ukv=v_w_ukv, v_qk_norm_q_g=v_qk_norm_q_g, v_qk_norm_k_g=v_qk_norm_k_g, v_w_o_mla=v_w_o_mla, v_conv_w=v_conv_w, v_conv_b=v_conv_b, v_conv_ln_g=v_conv_ln_g, v_conv_ln_b=v_conv_ln_b, v_w_pw_out=v_w_pw_out, v_w_out=v_w_out, v_norm2_g=v_norm2_g, v_w_ff1=v_w_ff1, v_w_ff2=v_w_ff2)
    weights = {n: given[n] for n in TWIN_WEIGHTS}
    shared = {n: given[n] for n in SHARED_INPUTS}
    per_example = {n: given[n] for n in ['x', 'c']}
    grad_fn = _jax.value_and_grad(_loss, argnums=(0, 1))

    def one_microbatch(ex, loss_target):
        ex = dict(ex)
        diff = ex.pop(TWIN_DIFF_INPUT)
        return grad_fn(weights, diff, {**shared, **ex}, loss_target)

    if N_MICROBATCH == 1:
        loss, (grad_w, grad_x) = one_microbatch(per_example, given["loss_target"])
    else:
        def body(carry, xs):
            loss_sum, grad_sum = carry
            l_k, (gw_k, gx_k) = one_microbatch(xs[0], xs[1])
            with _jax.named_scope("update"):
                return (loss_sum + l_k, _jax.tree.map(_jnp.add, grad_sum, gw_k)), gx_k

        init = (_jnp.zeros((), _jnp.float32), _jax.tree.map(_jnp.zeros_like, weights))
        (loss, grad_w), grad_x = _jax.lax.scan(body, init, (per_example, given["loss_target"]))
    with _jax.named_scope("update"):
        delta_w, new_m, new_v = {}, {}, {}
        for n in TWIN_WEIGHTS:
            delta_w[n], new_m[n], new_v[n] = _adamw(weights[n], grad_w[n], given["m_" + n], given["v_" + n])
    return (loss, grad_x, *[grad_w[n] for n in TWIN_WEIGHTS], *[delta_w[n] for n in TWIN_WEIGHTS],
            *[new_m[n] for n in TWIN_WEIGHTS], *[new_v[n] for n in TWIN_WEIGHTS])
```

```python
import functools

import jax
import jax.numpy as jnp
from jax import lax
from jax.experimental import pallas as pl
from jax.experimental.pallas import tpu as pltpu

F32 = jnp.float32
BF16 = jnp.bfloat16

N_DEV = 8
MESH_AXES = ("x", "y", "c")
EPS = 1e-6
N_HEADS = 8
QK_HEAD_DIM = 96
QK_NOPE_DIM = 64
QK_ROPE_DIM = 32
V_HEAD_DIM = 64
HEAD_PAD = 128
Q_LORA = 256
KV_LORA = 128
CONV_CH = 512
CONV_WIDTH = 31
CONV_HALO = 32
CHUNK = 64
ROPE_THETA = 10000.0
OFF_Q = Q_LORA
OFF_KV = OFF_Q + KV_LORA
OFF_KR = OFF_KV + QK_ROPE_DIM
OFF_GLU = OFF_KR + 2 * CONV_CH
ADAM_LR = 0.001
ADAM_B1 = 0.9
ADAM_B2 = 0.999
ADAM_EPS = 1e-08
ADAM_WD = 0.01
ADAM_STEP = 10
LANES = 128
PACK_ALIGN = 16 * LANES
VMEM_LIMIT = 56 * 1024 * 1024
NEG_BIG = -1e30
ROWS_PAD = 16

SHARDED = ("w_ada", "w_in", "w_uq", "w_ukv", "w_o_mla", "conv_w", "w_pw_out", "w_out", "w_ff1", "w_ff2")
ROW_SHARDED = ("w_out", "w_ff2")
REPLICATED = ("b_ada", "norm1_g", "q_latent_g", "kv_latent_g", "qk_norm_q_g", "qk_norm_k_g", "conv_b", "conv_ln_g",
              "conv_ln_b", "norm2_g")
WEIGHTS = ("w_ada", "b_ada", "norm1_g", "w_in", "q_latent_g", "w_uq", "kv_latent_g", "w_ukv", "qk_norm_q_g",
           "qk_norm_k_g", "w_o_mla", "conv_w", "conv_b", "conv_ln_g", "conv_ln_b", "w_pw_out", "w_out", "norm2_g",
           "w_ff1", "w_ff2")


def _tile(dim, pref):
    if dim <= pref:
        return dim
    t = (pref // LANES) * LANES
    while dim % t:
        t -= LANES
    return t


def _params(semantics):
    return pltpu.CompilerParams(dimension_semantics=semantics, vmem_limit_bytes=VMEM_LIMIT)


def _sigmoid(v):
    return 1.0 / (1.0 + jnp.exp(-v))


_DIMS = {"nn": (((1,), (0,)), ((), ())), "nt": (((1,), (1,)), ((), ())), "tn": (((0,), (0,)), ((), ()))}


def _mm(name, a, b, mode, out_dtype, *, a_fn=None, epi=None, epi_in=(), tm=512, tn=1024, tk=1024):
    if mode == "nn":
        (m, k), n = a.shape, b.shape[1]
    elif mode == "nt":
        (m, k), n = a.shape, b.shape[0]
    else:
        (k, m), n = a.shape, b.shape[1]
    tm, tn, tk = _tile(m, tm), _tile(n, tn), _tile(k, tk)
    nk = k // tk
    a_spec = (pl.BlockSpec((tk, tm), lambda i, j, kk: (kk, i)) if mode == "tn"
              else pl.BlockSpec((tm, tk), lambda i, j, kk: (i, kk)))
    b_spec = (pl.BlockSpec((tn, tk), lambda i, j, kk: (j, kk)) if mode == "nt"
              else pl.BlockSpec((tk, tn), lambda i, j, kk: (kk, j)))
    o_spec = pl.BlockSpec((tm, tn), lambda i, j, kk: (i, j))
    n_epi = len(epi_in)

    def body(a_ref, b_ref, *rest):
        epi_refs, o_ref, acc_ref = rest[:n_epi], rest[n_epi], rest[n_epi + 1]
        kk = pl.program_id(2)

        @pl.when(kk == 0)
        def _():
            acc_ref[...] = jnp.zeros_like(acc_ref)

        av = a_ref[...]
        if a_fn is not None:
            av = a_fn(av.astype(F32))
        acc_ref[...] += lax.dot_general(av.astype(BF16), b_ref[...].astype(BF16), _DIMS[mode],
                                        preferred_element_type=F32)

        @pl.when(kk == nk - 1)
        def _():
            acc = acc_ref[...]
            if epi is not None:
                acc = epi(acc, *[r[...].astype(F32) for r in epi_refs])
            o_ref[...] = acc.astype(out_dtype)

    return pl.pallas_call(
        body, name=name, grid=(m // tm, n // tn, nk),
        in_specs=[a_spec, b_spec] + [o_spec] * n_epi, out_specs=o_spec,
        out_shape=jax.ShapeDtypeStruct((m, n), out_dtype),
        scratch_shapes=[pltpu.VMEM((tm, tn), F32)],
        compiler_params=_params(("parallel", "parallel", "arbitrary")),
    )(a, b, *epi_in)


def _rowwise(name, fn, n_rows, seq, rows, bats=(), consts=(), outs=(), bat_outs=(), tot_outs=(), tm=256):
    tm = min(tm, seq)
    per_seq = seq // tm
    n_b = n_rows // seq
    nr, nb, nc, no, nbo, nto = len(rows), len(bats), len(consts), len(outs), len(bat_outs), len(tot_outs)

    def body(*refs):
        i = pl.program_id(0)
        r_in = [r[...] for r in refs[:nr]]
        b_in = [r[0] for r in refs[nr:nr + nb]]
        c_in = [r[...] for r in refs[nr + nb:nr + nb + nc]]
        o_refs = refs[nr + nb + nc:nr + nb + nc + no]
        bo_refs = refs[nr + nb + nc + no:nr + nb + nc + no + nbo]
        to_refs = refs[nr + nb + nc + no + nbo:]
        o_val, bo_val, to_val = fn(r_in, b_in, c_in)
        for r, v in zip(o_refs, o_val):
            r[...] = v.astype(r.dtype)
        if nbo:
            @pl.when(i % per_seq == 0)
            def _():
                for r in bo_refs:
                    r[...] = jnp.zeros_like(r)

            for r, v in zip(bo_refs, bo_val):
                r[0] += v
        if nto:
            @pl.when(i == 0)
            def _():
                for r in to_refs:
                    r[...] = jnp.zeros_like(r)

            for r, v in zip(to_refs, to_val):
                r[...] += v

    in_specs = [pl.BlockSpec((tm, w), functools.partial(lambda cb, i: (i, cb), cb)) for (_, w, cb) in rows]
    in_specs += [pl.BlockSpec((1, 1, bt.shape[2]), lambda i: (i // per_seq, 0, 0)) for bt in bats]
    in_specs += [pl.BlockSpec(ct.shape, lambda i: (0, 0)) for ct in consts]
    out_specs = [pl.BlockSpec((tm, w), lambda i: (i, 0)) for (w, _) in outs]
    out_specs += [pl.BlockSpec((1, 1, w), lambda i: (i // per_seq, 0, 0)) for w in bat_outs]
    out_specs += [pl.BlockSpec(shp, lambda i: (0, 0)) for shp in tot_outs]
    out_shape = [jax.ShapeDtypeStruct((n_rows, w), dt) for (w, dt) in outs]
    out_shape += [jax.ShapeDtypeStruct((n_b, 1, w), F32) for w in bat_outs]
    out_shape += [jax.ShapeDtypeStruct(shp, F32) for shp in tot_outs]
    res = pl.pallas_call(
        body, name=name, grid=(n_rows // tm,), in_specs=in_specs, out_specs=out_specs, out_shape=out_shape,
        compiler_params=_params(("arbitrary",)),
    )(*[r[0] for r in rows], *bats, *consts)
    return res


def _full(arr):
    return (arr, arr.shape[1], 0)


def _norm_mod(x, g, scale, shift):
    r = lax.rsqrt(jnp.mean(x * x, axis=-1, keepdims=True) + EPS)
    xh = x * r
    return xh * g * (1.0 + scale) + shift


def _norm_mod_bwd(x, g, scale, dh):
    r = lax.rsqrt(jnp.mean(x * x, axis=-1, keepdims=True) + EPS)
    xh = x * r
    dn = dh * (1.0 + scale)
    dxh = dn * g
    dx = r * (dxh - xh * jnp.mean(dxh * xh, axis=-1, keepdims=True))
    dscale = jnp.sum(dh * xh * g, axis=0, keepdims=True)
    dshift = jnp.sum(dh, axis=0, keepdims=True)
    dg = jnp.sum(dn * xh, axis=0, keepdims=True)
    return dx, dscale, dshift, dg


def _rms(v, g):
    r = lax.rsqrt(jnp.mean(v * v, axis=-1, keepdims=True) + EPS)
    return v * r * g


def _rms_bwd(v, g, dy):
    r = lax.rsqrt(jnp.mean(v * v, axis=-1, keepdims=True) + EPS)
    vh = v * r
    dvh = dy * g
    dv = r * (dvh - vh * jnp.mean(dvh * vh, axis=-1, keepdims=True))
    return dv, jnp.sum(dy * vh, axis=0, keepdims=True)


def _head_norm(v, g):
    r = lax.rsqrt(jnp.sum(v * v, axis=-1, keepdims=True) * (1.0 / QK_HEAD_DIM) + EPS)
    return v * r * g


def _head_norm_bwd(v, g, dy):
    r = lax.rsqrt(jnp.sum(v * v, axis=-1, keepdims=True) * (1.0 / QK_HEAD_DIM) + EPS)
    vh = v * r
    dvh = dy * g
    dv = r * (dvh - vh * (jnp.sum(dvh * vh, axis=-1, keepdims=True) * (1.0 / QK_HEAD_DIM)))
    return dv, jnp.sum(dy * vh, axis=0, keepdims=True)


def _rope(v, cos, sin_lo, sin_hi):
    return v * cos + pltpu.roll(v, HEAD_PAD - 16, 1) * sin_lo + pltpu.roll(v, 16, 1) * sin_hi


def _rope_bwd(g, cos, sin_lo, sin_hi):
    return g * cos + pltpu.roll(g * sin_lo, 16, 1) + pltpu.roll(g * sin_hi, HEAD_PAD - 16, 1)


def _mla_prep_fwd(zsm, wuq, wuk, wuv, gq, gkv, gqn, gkn, rope, n_b, seq):
    n_rows = n_b * seq
    tm = min(256, seq)
    per_seq = seq // tm

    def body(z_ref, wuq_ref, wuk_ref, wuv_ref, gq_ref, gkv_ref, gqn_ref, gkn_ref, c_ref, s1_ref, s2_ref,
             q_ref, k_ref, v_ref):
        z = z_ref[...]
        qn = _rms(z[:, :Q_LORA], gq_ref[...]).astype(BF16)
        kvn = _rms(z[:, Q_LORA:Q_LORA + KV_LORA], gkv_ref[...]).astype(BF16)
        krp = z[:, Q_LORA + KV_LORA:]
        cos, s1, s2 = c_ref[...], s1_ref[...], s2_ref[...]
        for h in range(N_HEADS):
            qh = jnp.dot(qn, wuq_ref[h], preferred_element_type=F32)
            q_ref[0, h] = _rope(_head_norm(qh, gqn_ref[...]), cos, s1, s2).astype(BF16)
            kh = jnp.dot(kvn, wuk_ref[h], preferred_element_type=F32) + krp
            k_ref[0, h] = _rope(_head_norm(kh, gkn_ref[...]), cos, s1, s2).astype(BF16)
            v_ref[0, h] = jnp.dot(kvn, wuv_ref[h], preferred_element_type=F32).astype(BF16)

    whole3 = lambda arr: pl.BlockSpec(arr.shape, lambda i: (0, 0, 0))
    whole2 = lambda arr: pl.BlockSpec(arr.shape, lambda i: (0, 0))
    rope_spec = pl.BlockSpec((tm, HEAD_PAD), lambda i: (i % per_seq, 0))
    head_spec = pl.BlockSpec((1, N_HEADS, tm, HEAD_PAD), lambda i: (i // per_seq, 0, i % per_seq, 0))
    head_shape = jax.ShapeDtypeStruct((n_b, N_HEADS, seq, HEAD_PAD), BF16)
    return pl.pallas_call(
        body, name="mla_prep_fwd", grid=(n_rows // tm,),
        in_specs=[pl.BlockSpec((tm, 512), lambda i: (i, 0)), whole3(wuq), whole3(wuk), whole3(wuv),
                  whole2(gq), whole2(gkv), whole2(gqn), whole2(gkn), rope_spec, rope_spec, rope_spec],
        out_specs=[head_spec] * 3, out_shape=[head_shape] * 3,
        compiler_params=_params(("parallel",)),
    )(zsm, wuq, wuk, wuv, gq, gkv, gqn, gkn, *rope)


def _mla_prep_bwd(zsm, dq, dk, dv, wuq, wuk, wuv, gq, gkv, gqn, gkn, rope, n_b, seq):
    n_rows = n_b * seq
    tm = min(256, seq)
    per_seq = seq // tm
    tn_dims = _DIMS["tn"]
    nt_dims = _DIMS["nt"]

    def body(z_ref, dq_ref, dk_ref, dv_ref, wuq_ref, wuk_ref, wuv_ref, gq_ref, gkv_ref, gqn_ref, gkn_ref,
             c_ref, s1_ref, s2_ref, dz_ref, dwuq_ref, dwuk_ref, dwuv_ref, dgq_ref, dgkv_ref, dgqn_ref, dgkn_ref):
        @pl.when(pl.program_id(0) == 0)
        def _():
            for r in (dwuq_ref, dwuk_ref, dwuv_ref, dgq_ref, dgkv_ref, dgqn_ref, dgkn_ref):
                r[...] = jnp.zeros_like(r)

        z = z_ref[...]
        zq, zkv, krp = z[:, :Q_LORA], z[:, Q_LORA:Q_LORA + KV_LORA], z[:, Q_LORA + KV_LORA:]
        qn = _rms(zq, gq_ref[...]).astype(BF16)
        kvn = _rms(zkv, gkv_ref[...]).astype(BF16)
        cos, s1, s2 = c_ref[...], s1_ref[...], s2_ref[...]
        lane = lax.broadcasted_iota(jnp.int32, (tm, HEAD_PAD), 1)
        rope_lanes = (lane >= QK_NOPE_DIM) & (lane < QK_HEAD_DIM)
        dqn = jnp.zeros((tm, Q_LORA), F32)
        dkvn = jnp.zeros((tm, KV_LORA), F32)
        dkrp = jnp.zeros((tm, HEAD_PAD), F32)
        dgqn = jnp.zeros((1, HEAD_PAD), F32)
        dgkn = jnp.zeros((1, HEAD_PAD), F32)
        for h in range(N_HEADS):
            qh = jnp.dot(qn, wuq_ref[h], preferred_element_type=F32)
            dqh, dg = _head_norm_bwd(qh, gqn_ref[...], _rope_bwd(dq_ref[0, h].astype(F32), cos, s1, s2))
            dgqn += dg
            dqh = dqh.astype(BF16)
            dwuq_ref[h] += lax.dot_general(qn, dqh, tn_dims, preferred_element_type=F32)
            dqn += lax.dot_general(dqh, wuq_ref[h], nt_dims, preferred_element_type=F32)

            kh = jnp.dot(kvn, wuk_ref[h], preferred_element_type=F32) + krp
            dkh, dg = _head_norm_bwd(kh, gkn_ref[...], _rope_bwd(dk_ref[0, h].astype(F32), cos, s1, s2))
            dgkn += dg
            dkrp += jnp.where(rope_lanes, dkh, 0.0)
            dkh = dkh.astype(BF16)
            dwuk_ref[h] += lax.dot_general(kvn, dkh, tn_dims, preferred_element_type=F32)
            dkvn += lax.dot_general(dkh, wuk_ref[h], nt_dims, preferred_element_type=F32)

            dvh = dv_ref[0, h]
            dwuv_ref[h] += lax.dot_general(kvn, dvh, tn_dims, preferred_element_type=F32)
            dkvn += lax.dot_general(dvh, wuv_ref[h], nt_dims, preferred_element_type=F32)
        dzq, dg = _rms_bwd(zq, gq_ref[...], dqn)
        dgq_ref[...] += dg
        dzkv, dg = _rms_bwd(zkv, gkv_ref[...], dkvn)
        dgkv_ref[...] += dg
        dgqn_ref[...] += dgqn
        dgkn_ref[...] += dgkn
        dz_ref[:, :Q_LORA] = dzq.astype(dz_ref.dtype)
        dz_ref[:, Q_LORA:Q_LORA + KV_LORA] = dzkv.astype(dz_ref.dtype)
        dz_ref[:, Q_LORA + KV_LORA:] = dkrp.astype(dz_ref.dtype)

    whole3 = lambda arr: pl.BlockSpec(arr.shape, lambda i: (0, 0, 0))
    whole2 = lambda arr: pl.BlockSpec(arr.shape, lambda i: (0, 0))
    rope_spec = pl.BlockSpec((tm, HEAD_PAD), lambda i: (i % per_seq, 0))
    head_spec = pl.BlockSpec((1, N_HEADS, tm, HEAD_PAD), lambda i: (i // per_seq, 0, i % per_seq, 0))
    row_spec = pl.BlockSpec((tm, 512), lambda i: (i, 0))
    return pl.pallas_call(
        body, name="mla_prep_bwd", grid=(n_rows // tm,),
        in_specs=[row_spec, head_spec, head_spec, head_spec, whole3(wuq), whole3(wuk), whole3(wuv),
                  whole2(gq), whole2(gkv), whole2(gqn), whole2(gkn), rope_spec, rope_spec, rope_spec],
        out_specs=[row_spec, whole3(wuq), whole3(wuk), whole3(wuv), whole2(gq), whole2(gkv), whole2(gqn), whole2(gkn)],
        out_shape=[jax.ShapeDtypeStruct((n_rows, 512), BF16),
                   jax.ShapeDtypeStruct(wuq.shape, F32), jax.ShapeDtypeStruct(wuk.shape, F32),
                   jax.ShapeDtypeStruct(wuv.shape, F32), jax.ShapeDtypeStruct(gq.shape, F32),
                   jax.ShapeDtypeStruct(gkv.shape, F32), jax.ShapeDtypeStruct(gqn.shape, F32),
                   jax.ShapeDtypeStruct(gkn.shape, F32)],
        compiler_params=_params(("arbitrary",)),
    )(zsm, dq, dk, dv, wuq, wuk, wuv, gq, gkv, gqn, gkn, *rope)


def _chunk_mask(t):
    r = lax.broadcasted_iota(jnp.int32, (t, t), 0) // CHUNK
    c = lax.broadcasted_iota(jnp.int32, (t, t), 1) // CHUNK
    return r >= c


def _attn_fwd(q, k, v, n_b, seq):
    tq = min(256, seq)
    nq = seq // tq
    scale = QK_HEAD_DIM ** -0.5
    nt_dims = _DIMS["nt"]

    def body(q_ref, k_ref, v_ref, o_ref, lse_ref):
        qi = pl.program_id(2)
        mask = _chunk_mask(tq)
        o_acc = jnp.zeros((tq, HEAD_PAD), F32)
        for hh in range(2):
            qh = q_ref[0, hh]

            def step(j, carry, masked, hh=hh, qh=qh):
                m, l, acc = carry
                rows = pl.ds(pl.multiple_of(j * tq, tq), tq)
                s = lax.dot_general(qh, k_ref[0, hh, rows, :], nt_dims, preferred_element_type=F32) * scale
                if masked:
                    s = jnp.where(mask, s, NEG_BIG)
                m_new = jnp.maximum(m, jnp.max(s, axis=-1, keepdims=True))
                alpha = jnp.exp(m - m_new)
                p = jnp.exp(s - m_new)
                l = alpha * l + jnp.sum(p, axis=-1, keepdims=True)
                acc = alpha * acc + jnp.dot(p.astype(BF16), v_ref[0, hh, rows, :], preferred_element_type=F32)
                return m_new, l, acc

            carry = (jnp.full((tq, 1), NEG_BIG, F32), jnp.zeros((tq, 1), F32), jnp.zeros((tq, HEAD_PAD), F32))
            carry = lax.fori_loop(0, qi, functools.partial(step, masked=False), carry)
            m, l, acc = step(qi, carry, True)
            o_acc += acc / l
            lse_ref[0, hh] = jnp.broadcast_to(m + jnp.log(l), (tq, HEAD_PAD))
        o_ref[0] = o_acc.astype(BF16)

    kv_spec = pl.BlockSpec((1, 2, seq, HEAD_PAD), lambda b, hp, i: (b, hp, 0, 0))
    q_spec = pl.BlockSpec((1, 2, tq, HEAD_PAD), lambda b, hp, i: (b, hp, i, 0))
    return pl.pallas_call(
        body, name="attn_fwd", grid=(n_b, N_HEADS // 2, nq),
        in_specs=[q_spec, kv_spec, kv_spec],
        out_specs=[pl.BlockSpec((1, tq, HEAD_PAD), lambda b, hp, i: (b, i, hp)), q_spec],
        out_shape=[jax.ShapeDtypeStruct((n_b, seq, N_HEADS * V_HEAD_DIM), BF16),
                   jax.ShapeDtypeStruct((n_b, N_HEADS, seq, HEAD_PAD), F32)],
        compiler_params=_params(("parallel", "parallel", "arbitrary")),
    )(q, k, v)


def _attn_bwd_dq(q, k, v, do, o, lse, n_b, seq):
    tq = min(256, seq)
    nq = seq // tq
    scale = QK_HEAD_DIM ** -0.5
    nt_dims = _DIMS["nt"]

    def body(q_ref, k_ref, v_ref, do_ref, o_ref, lse_ref, dq_ref, delta_ref):
        qi = pl.program_id(2)
        mask = _chunk_mask(tq)
        do = do_ref[0]
        prod = do.astype(F32) * o_ref[0].astype(F32)
        lane = lax.broadcasted_iota(jnp.int32, (tq, HEAD_PAD), 1)
        for hh in range(2):
            qh = q_ref[0, hh]
            delta = jnp.sum(jnp.where(lane // V_HEAD_DIM == hh, prod, 0.0), axis=-1, keepdims=True)
            lse = lse_ref[0, hh][:, :1]

            def step(j, dq, masked, hh=hh, qh=qh, delta=delta, lse=lse):
                rows = pl.ds(pl.multiple_of(j * tq, tq), tq)
                kj = k_ref[0, hh, rows, :]
                s = lax.dot_general(qh, kj, nt_dims, preferred_element_type=F32) * scale
                p = jnp.exp(s - lse)
                if masked:
                    p = jnp.where(mask, p, 0.0)
                dp = lax.dot_general(do, v_ref[0, hh, rows, :], nt_dims, preferred_element_type=F32)
                ds = p * (dp - delta) * scale
                return dq + jnp.dot(ds.astype(BF16), kj, preferred_element_type=F32)

            dq = lax.fori_loop(0, qi, functools.partial(step, masked=False), jnp.zeros((tq, HEAD_PAD), F32))
            dq = step(qi, dq, True)
            dq_ref[0, hh] = dq.astype(BF16)
            delta_ref[0, hh] = jnp.broadcast_to(delta, (tq, HEAD_PAD))

    kv_spec = pl.BlockSpec((1, 2, seq, HEAD_PAD), lambda b, hp, i: (b, hp, 0, 0))
    q_spec = pl.BlockSpec((1, 2, tq, HEAD_PAD), lambda b, hp, i: (b, hp, i, 0))
    o_spec = pl.BlockSpec((1, tq, HEAD_PAD), lambda b, hp, i: (b, i, hp))
    return pl.pallas_call(
        body, name="attn_bwd_dq", grid=(n_b, N_HEADS // 2, nq),
        in_specs=[q_spec, kv_spec, kv_spec, o_spec, o_spec, q_spec],
        out_specs=[q_spec, q_spec],
        out_shape=[jax.ShapeDtypeStruct((n_b, N_HEADS, seq, HEAD_PAD), BF16),
                   jax.ShapeDtypeStruct((n_b, N_HEADS, seq, HEAD_PAD), F32)],
        compiler_params=_params(("parallel", "parallel", "arbitrary")),
    )(q, k, v, do, o, lse)


def _attn_bwd_dkv(q, k, v, do, lse, delta, n_b, seq):
    tq = min(256, seq)
    nq = seq // tq
    scale = QK_HEAD_DIM ** -0.5
    nt_dims = _DIMS["nt"]
    tn_dims = _DIMS["tn"]

    def body(q_ref, k_ref, v_ref, do_ref, lse_ref, delta_ref, dk_ref, dv_ref):
        kj = pl.program_id(2)
        mask = _chunk_mask(tq)
        for hh in range(2):
            kb = k_ref[0, hh]
            vb = v_ref[0, hh]

            def step(i, carry, masked, hh=hh, kb=kb, vb=vb):
                dk, dv = carry
                rows = pl.ds(pl.multiple_of(i * tq, tq), tq)
                qi = q_ref[0, hh, rows, :]
                doi = do_ref[0, rows, :]
                s = lax.dot_general(qi, kb, nt_dims, preferred_element_type=F32) * scale
                p = jnp.exp(s - lse_ref[0, hh, rows, :][:, :1])
                if masked:
                    p = jnp.where(mask, p, 0.0)
                dv = dv + lax.dot_general(p.astype(BF16), doi, tn_dims, preferred_element_type=F32)
                dp = lax.dot_general(doi, vb, nt_dims, preferred_element_type=F32)
                ds = p * (dp - delta_ref[0, hh, rows, :][:, :1]) * scale
                dk = dk + lax.dot_general(ds.astype(BF16), qi, tn_dims, preferred_element_type=F32)
                return dk, dv

            carry = (jnp.zeros((tq, HEAD_PAD), F32), jnp.zeros((tq, HEAD_PAD), F32))
            carry = step(kj, carry, True)
            dk, dv = lax.fori_loop(kj + 1, nq, functools.partial(step, masked=False), carry)
            dk_ref[0, hh] = dk.astype(BF16)
            dv_ref[0, hh] = dv.astype(BF16)

    full_spec = pl.BlockSpec((1, 2, seq, HEAD_PAD), lambda b, hp, j: (b, hp, 0, 0))
    blk_spec = pl.BlockSpec((1, 2, tq, HEAD_PAD), lambda b, hp, j: (b, hp, j, 0))
    do_spec = pl.BlockSpec((1, seq, HEAD_PAD), lambda b, hp, j: (b, 0, hp))
    head_shape = jax.ShapeDtypeStruct((n_b, N_HEADS, seq, HEAD_PAD), BF16)
    return pl.pallas_call(
        body, name="attn_bwd_dkv", grid=(n_b, N_HEADS // 2, nq),
        in_specs=[full_spec, blk_spec, blk_spec, do_spec, full_spec, full_spec],
        out_specs=[blk_spec, blk_spec], out_shape=[head_shape, head_shape],
        compiler_params=_params(("parallel", "parallel", "arbitrary")),
    )(q, k, v, do, lse, delta)


def _ln_silu(u1, g, b):
    mu = jnp.mean(u1, axis=-1, keepdims=True)
    uc = u1 - mu
    r = lax.rsqrt(jnp.mean(uc * uc, axis=-1, keepdims=True) + EPS)
    y = uc * r * g + b
    return y * _sigmoid(y)


def _conv_fill_glu(z_ref, u0_ref, seq, tile):
    u0_ref[0:CONV_HALO, :] = jnp.zeros((CONV_HALO, CONV_CH), F32)
    for t in range(seq // tile):
        zt = z_ref[0, t * tile:(t + 1) * tile, :].astype(F32)
        u0_ref[CONV_HALO + t * tile:CONV_HALO + (t + 1) * tile, :] = zt[:, :CONV_CH] * _sigmoid(zt[:, CONV_CH:])


def _conv_tile(u0_ref, w_ref, b_ref, t, tile):
    acc = jnp.broadcast_to(b_ref[...], (tile, CONV_CH))
    base = t * tile + CONV_HALO - (CONV_WIDTH - 1)
    for kk in range(CONV_WIDTH):
        acc = acc + w_ref[kk:kk + 1, :] * u0_ref[base + kk:base + kk + tile, :]
    return acc


def _conv_fwd(zglu, conv_w, conv_b, ln_g, ln_b, n_b, seq):
    tile = min(256, seq)

    def body(z_ref, w_ref, b_ref, g_ref, bb_ref, o_ref, u0_ref):
        _conv_fill_glu(z_ref, u0_ref, seq, tile)
        for t in range(seq // tile):
            u1 = _conv_tile(u0_ref, w_ref, b_ref, t, tile)
            o_ref[0, t * tile:(t + 1) * tile, :] = _ln_silu(u1, g_ref[...], bb_ref[...]).astype(BF16)

    whole2 = lambda arr: pl.BlockSpec(arr.shape, lambda b: (0, 0))
    return pl.pallas_call(
        body, name="conv_fwd", grid=(n_b,),
        in_specs=[pl.BlockSpec((1, seq, 2 * CONV_CH), lambda b: (b, 0, 0)), whole2(conv_w), whole2(conv_b),
                  whole2(ln_g), whole2(ln_b)],
        out_specs=pl.BlockSpec((1, seq, CONV_CH), lambda b: (b, 0, 0)),
        out_shape=jax.ShapeDtypeStruct((n_b, seq, CONV_CH), BF16),
        scratch_shapes=[pltpu.VMEM((seq + CONV_HALO, CONV_CH), F32)],
        compiler_params=_params(("parallel",)),
    )(zglu, conv_w, conv_b, ln_g, ln_b)


def _conv_bwd(zglu, du3, conv_w, conv_b, ln_g, ln_b, n_b, seq):
    tile = min(256, seq)
    n_t = seq // tile

    def body(z_ref, du3_ref, w_ref, b_ref, g_ref, bb_ref, dz_ref, dw_ref, db_ref, dg_ref, dbb_ref, u0_ref, du1_ref):
        @pl.when(pl.program_id(0) == 0)
        def _():
            for r in (dw_ref, db_ref, dg_ref, dbb_ref):
                r[...] = jnp.zeros_like(r)

        _conv_fill_glu(z_ref, u0_ref, seq, tile)
        du1_ref[seq:seq + CONV_HALO, :] = jnp.zeros((CONV_HALO, CONV_CH), F32)
        g = g_ref[...]
        for t in range(n_t):
            u1 = _conv_tile(u0_ref, w_ref, b_ref, t, tile)
            mu = jnp.mean(u1, axis=-1, keepdims=True)
            uc = u1 - mu
            r = lax.rsqrt(jnp.mean(uc * uc, axis=-1, keepdims=True) + EPS)
            xh = uc * r
            y = xh * g + bb_ref[...]
            sg = _sigmoid(y)
            dy = du3_ref[0, t * tile:(t + 1) * tile, :].astype(F32) * (sg * (1.0 + y * (1.0 - sg)))
            dg_ref[...] += jnp.sum(dy * xh, axis=0, keepdims=True)
            dbb_ref[...] += jnp.sum(dy, axis=0, keepdims=True)
            dxh = dy * g
            du1 = r * (dxh - jnp.mean(dxh, axis=-1, keepdims=True) - xh * jnp.mean(dxh * xh, axis=-1, keepdims=True))
            db_ref[...] += jnp.sum(du1, axis=0, keepdims=True)
            du1_ref[t * tile:(t + 1) * tile, :] = du1
        for t in range(n_t):
            du1 = du1_ref[t * tile:(t + 1) * tile, :]
            du0 = jnp.zeros((tile, CONV_CH), F32)
            base_u = t * tile + CONV_HALO - (CONV_WIDTH - 1)
            base_d = t * tile + (CONV_WIDTH - 1)
            for kk in range(CONV_WIDTH):
                du0 = du0 + w_ref[kk:kk + 1, :] * du1_ref[base_d - kk:base_d - kk + tile, :]
                dw_ref[kk:kk + 1, :] += jnp.sum(du1 * u0_ref[base_u + kk:base_u + kk + tile, :], axis=0, keepdims=True)
            zt = z_ref[0, t * tile:(t + 1) * tile, :].astype(F32)
            ga, sb = zt[:, :CONV_CH], _sigmoid(zt[:, CONV_CH:])
            dz_ref[0, t * tile:(t + 1) * tile, :CONV_CH] = (du0 * sb).astype(BF16)
            dz_ref[0, t * tile:(t + 1) * tile, CONV_CH:] = (du0 * ga * sb * (1.0 - sb)).astype(BF16)

    whole2 = lambda arr: pl.BlockSpec(arr.shape, lambda b: (0, 0))
    z_spec = pl.BlockSpec((1, seq, 2 * CONV_CH), lambda b: (b, 0, 0))
    return pl.pallas_call(
        body, name="conv_bwd", grid=(n_b,),
        in_specs=[z_spec, pl.BlockSpec((1, seq, CONV_CH), lambda b: (b, 0, 0)), whole2(conv_w), whole2(conv_b),
                  whole2(ln_g), whole2(ln_b)],
        out_specs=[z_spec, whole2(conv_w), whole2(conv_b), whole2(ln_g), whole2(ln_b)],
        out_shape=[jax.ShapeDtypeStruct((n_b, seq, 2 * CONV_CH), BF16), jax.ShapeDtypeStruct(conv_w.shape, F32),
                   jax.ShapeDtypeStruct(conv_b.shape, F32), jax.ShapeDtypeStruct(ln_g.shape, F32),
                   jax.ShapeDtypeStruct(ln_b.shape, F32)],
        scratch_shapes=[pltpu.VMEM((seq + CONV_HALO, CONV_CH), F32), pltpu.VMEM((seq + CONV_HALO, CONV_CH), F32)],
        compiler_params=_params(("arbitrary",)),
    )(zglu, du3, conv_w, conv_b, ln_g, ln_b)


def _rope_tables(seq):
    inv_freq = ROPE_THETA ** (-jnp.arange(0, QK_ROPE_DIM, 2, dtype=F32) / QK_ROPE_DIM)
    ang = jnp.arange(seq, dtype=F32)[:, None] * inv_freq[None, :]
    cos, sin = jnp.cos(ang), jnp.sin(ang)
    half = QK_ROPE_DIM // 2
    z = lambda n: jnp.zeros((seq, n), F32)
    tail = HEAD_PAD - QK_HEAD_DIM
    cos_t = jnp.concatenate([jnp.ones((seq, QK_NOPE_DIM), F32), cos, cos, z(tail)], axis=1)
    sin_lo = jnp.concatenate([z(QK_NOPE_DIM), -sin, z(half), z(tail)], axis=1)
    sin_hi = jnp.concatenate([z(QK_NOPE_DIM), z(half), sin, z(tail)], axis=1)
    return cos_t, sin_lo, sin_hi


def _pad_lanes(v, width=HEAD_PAD):
    return jnp.pad(v, [(0, 0)] * (v.ndim - 1) + [(0, width - v.shape[-1])])


def _local_step(x, c, w, target):
    n_b, seq, d = x.shape
    n_rows = n_b * seq
    x2 = x.reshape(n_rows, d)
    t2 = target.reshape(n_rows, d)
    rw = functools.partial(_rowwise, n_rows=n_rows, seq=seq)

    w_in = w["w_in"]
    zeros = lambda n: jnp.zeros((d, n), BF16)
    w_sm = jnp.concatenate([w_in[:, :OFF_KV], zeros(QK_NOPE_DIM), w_in[:, OFF_KV:OFF_KR], zeros(HEAD_PAD - QK_HEAD_DIM)], axis=1)
    w_glu = w_in[:, OFF_KR:OFF_GLU]
    w_gate = w_in[:, OFF_GLU:]
    wuq = _pad_lanes(w["w_uq"].reshape(Q_LORA, N_HEADS, QK_HEAD_DIM)).transpose(1, 0, 2)
    wukv = w["w_ukv"].reshape(KV_LORA, N_HEADS, QK_NOPE_DIM + V_HEAD_DIM).transpose(1, 0, 2)
    wuk = _pad_lanes(wukv[:, :, :QK_NOPE_DIM])
    wv = wukv[:, :, QK_NOPE_DIM:]
    odd = (jnp.arange(N_HEADS) % 2 == 1)[:, None, None]
    wuv = jnp.where(odd, jnp.pad(wv, ((0, 0), (0, 0), (V_HEAD_DIM, 0))), jnp.pad(wv, ((0, 0), (0, 0), (0, V_HEAD_DIM))))
    gqn = _pad_lanes(w["qk_norm_q_g"])
    gkn = _pad_lanes(w["qk_norm_k_g"])
    conv_w = jnp.pad(w["conv_w"].astype(F32), ((0, 1), (0, 0)))
    rope = _rope_tables(seq)

    c8 = jnp.pad(c, ((0, ROWS_PAD - n_b), (0, 0)))
    silu = lambda v: v * _sigmoid(v)
    b_ada8 = jnp.broadcast_to(w["b_ada"], (ROWS_PAD, w["b_ada"].shape[1]))
    mod = _mm("ada_fwd", c8, w["w_ada"], "nn", F32, a_fn=silu, epi=lambda acc, b: acc + b, epi_in=(b_ada8,))
    mod = mod[:n_b].reshape(n_b, 6, 1, d)
    shift1, scale1, gate1, shift2, scale2, gate2 = [mod[:, i] for i in range(6)]

    (h,) = rw("norm1_fwd", lambda r, b, cc: ([_norm_mod(r[0], cc[0], b[0], b[1])], [], []),
              rows=[_full(x2)], bats=[scale1, shift1], consts=[w["norm1_g"]], outs=[(d, BF16)])
    zsm = _mm("in_proj_sm", h, w_sm, "nn", F32)
    zglu = _mm("in_proj_glu", h, w_glu, "nn", BF16)
    zgate = _mm("in_proj_gate", h, w_gate, "nn", BF16)
    q, k, v = _mla_prep_fwd(zsm, wuq, wuk, wuv, w["q_latent_g"], w["kv_latent_g"], gqn, gkn, rope, n_b, seq)
    attn, lse = _attn_fwd(q, k, v, n_b, seq)
    attn2 = attn.reshape(n_rows, N_HEADS * V_HEAD_DIM)
    u3 = _conv_fwd(zglu.reshape(n_b, seq, 2 * CONV_CH), conv_w, w["conv_b"], w["conv_ln_g"], w["conv_ln_b"], n_b, seq)
    u32 = u3.reshape(n_rows, CONV_CH)
    ya = _mm("mla_out", attn2, w["w_o_mla"], "nn", BF16)
    yb = _mm("conv_out", u32, w["w_pw_out"], "nn", BF16)
    (mrg,) = rw("merge_fwd",
                lambda r, b, cc: ([_sigmoid(r[0].astype(F32)) * r[2].astype(F32) + _sigmoid(r[1].astype(F32)) * r[3].astype(F32)], [], []),
                rows=[(zgate, d, 0), (zgate, d, 1), _full(ya), _full(yb)], outs=[(d, BF16)])
    mixed = _mm("out_proj", mrg, w["w_out"], "nn", BF16)

    def mid_fn(r, b, cc):
        x1 = r[0] + b[0] * r[1].astype(F32)
        return [x1, _norm_mod(x1, cc[0], b[1], b[2])], [], []

    x1, h2 = rw("norm2_fwd", mid_fn, rows=[_full(x2), _full(mixed)], bats=[gate1, scale2, shift2],
                consts=[w["norm2_g"]], outs=[(d, F32), (d, BF16)])
    relu2 = lambda v: jnp.square(jnp.maximum(v, 0.0))
    a = _mm("ff1", h2, w["w_ff1"], "nn", BF16)
    f = _mm("ff2", a, w["w_ff2"], "nn", BF16, a_fn=relu2)

    def loss_fn(r, b, cc):
        ff = r[1].astype(F32)
        err = r[0] + b[0] * ff - r[2]
        dy = err * (1.0 / d)
        sq = jnp.broadcast_to(jnp.sum(err * err, keepdims=True), (1, LANES))
        return [dy, b[0] * dy], [jnp.sum(dy * ff, axis=0, keepdims=True)], [sq]

    dy, df, dgate2, sq_err = rw("loss", loss_fn, rows=[_full(x1), _full(f), _full(t2)], bats=[gate2],
                                outs=[(d, F32), (d, BF16)], bat_outs=[d], tot_outs=[(1, LANES)])

    da = _mm("ff2_bwd", df, w["w_ff2"], "nt", BF16, epi=lambda acc, av: acc * 2.0 * jnp.maximum(av, 0.0), epi_in=(a,))
    g_ff2 = _mm("ff2_dw", a, df, "tn", F32, a_fn=relu2)
    dh2 = _mm("ff1_bwd", da, w["w_ff1"], "nt", F32)
    g_ff1 = _mm("ff1_dw", h2, da, "tn", F32)

    def mid_bwd(r, b, cc):
        x1_, dh2_, dy_, mixed_ = r[0], r[1], r[2], r[3].astype(F32)
        dx, dsc, dsh, dg = _norm_mod_bwd(x1_, cc[0], b[0], dh2_)
        dx1_ = dy_ + dx
        return [dx1_, b[1] * dx1_], [dsc, dsh, jnp.sum(dx1_ * mixed_, axis=0, keepdims=True)], [dg]

    dx1, dmixed, dscale2, dshift2, dgate1, g_norm2 = rw(
        "norm2_bwd", mid_bwd, rows=[_full(x1), _full(dh2), _full(dy), _full(mixed)], bats=[scale2, gate1],
        consts=[w["norm2_g"]], outs=[(d, F32), (d, BF16)], bat_outs=[d, d, d], tot_outs=[(1, d)])

    dmrg = _mm("out_proj_bwd", dmixed, w["w_out"], "nt", BF16)
    g_out = _mm("out_proj_dw", mrg, dmixed, "tn", F32)

    def merge_bwd(r, b, cc):
        dm, ya_, yb_ = r[0].astype(F32), r[3].astype(F32), r[4].astype(F32)
        sa, sb = _sigmoid(r[1].astype(F32)), _sigmoid(r[2].astype(F32))
        return [dm * ya_ * sa * (1.0 - sa), dm * yb_ * sb * (1.0 - sb), dm * sa, dm * sb], [], []

    dzga, dzgb, dya, dyb = rw("merge_bwd", merge_bwd,
                              rows=[_full(dmrg), (zgate, d, 0), (zgate, d, 1), _full(ya), _full(yb)],
                              outs=[(d, BF16)] * 4)
    dattn = _mm("mla_out_bwd", dya, w["w_o_mla"], "nt", BF16)
    g_o_mla = _mm("mla_out_dw", attn2, dya, "tn", F32)
    du3 = _mm("conv_out_bwd", dyb, w["w_pw_out"], "nt", BF16)
    g_pw = _mm("conv_out_dw", u32, dyb, "tn", F32)

    dzglu, g_conv_w, g_conv_b, g_ln_g, g_ln_b = _conv_bwd(
        zglu.reshape(n_b, seq, 2 * CONV_CH), du3.reshape(n_b, seq, CONV_CH), conv_w, w["conv_b"], w["conv_ln_g"],
        w["conv_ln_b"], n_b, seq)
    dzglu = dzglu.reshape(n_rows, 2 * CONV_CH)

    dattn3 = dattn.reshape(n_b, seq, N_HEADS * V_HEAD_DIM)
    dq, delta = _attn_bwd_dq(q, k, v, dattn3, attn, lse, n_b, seq)
    dk, dv = _attn_bwd_dkv(q, k, v, dattn3, lse, delta, n_b, seq)
    dzsm, g_wuq, g_wuk, g_wuv, g_gq, g_gkv, g_gqn, g_gkn = _mla_prep_bwd(
        zsm, dq, dk, dv, wuq, wuk, wuv, w["q_latent_g"], w["kv_latent_g"], gqn, gkn, rope, n_b, seq)

    add = lambda acc, prev: acc + prev
    dh = _mm("in_proj_gate_bwd_a", dzga, w_gate[:, :d], "nt", F32)
    dh = _mm("in_proj_gate_bwd_b", dzgb, w_gate[:, d:], "nt", F32, epi=add, epi_in=(dh,))
    dh = _mm("in_proj_glu_bwd", dzglu, w_glu, "nt", F32, epi=add, epi_in=(dh,))
    dh = _mm("in_proj_sm_bwd", dzsm, w_sm, "nt", F32, epi=add, epi_in=(dh,))
    g_gate_a = _mm("in_proj_gate_dw_a", h, dzga, "tn", F32)
    g_gate_b = _mm("in_proj_gate_dw_b", h, dzgb, "tn", F32)
    g_glu = _mm("in_proj_glu_dw", h, dzglu, "tn", F32)
    g_sm = _mm("in_proj_sm_dw", h, dzsm, "tn", F32)

    def first_bwd(r, b, cc):
        dx, dsc, dsh, dg = _norm_mod_bwd(r[0], cc[0], b[0], r[1])
        return [r[2] + dx], [dsc, dsh], [dg]

    grad_x, dscale1, dshift1, g_norm1 = rw("norm1_bwd", first_bwd, rows=[_full(x2), _full(dh), _full(dx1)],
                                            bats=[scale1], consts=[w["norm1_g"]], outs=[(d, F32)], bat_outs=[d, d],
                                            tot_outs=[(1, d)])

    dmod = jnp.concatenate([dshift1, dscale1, dgate1, dshift2, dscale2, dgate2], axis=1).reshape(n_b, 6 * d)
    dmod8 = jnp.pad(dmod, ((0, ROWS_PAD - n_b), (0, 0)))
    g_ada = _mm("ada_dw", c8, dmod8, "tn", F32, a_fn=silu)
    (g_b_ada,) = _rowwise("ada_db", lambda r, b, cc: ([], [], [jnp.sum(r[0], axis=0, keepdims=True)]), ROWS_PAD, ROWS_PAD,
                          rows=[_full(dmod8)], tot_outs=[(1, 6 * d)])

    g_in = jnp.concatenate([g_sm[:, :OFF_KV], g_sm[:, OFF_KV + QK_NOPE_DIM:OFF_KV + QK_NOPE_DIM + QK_ROPE_DIM], g_glu,
                            g_gate_a, g_gate_b], axis=1)
    g_uq = g_wuq[:, :, :QK_HEAD_DIM].transpose(1, 0, 2).reshape(Q_LORA, N_HEADS * QK_HEAD_DIM)
    g_v = jnp.where(odd, g_wuv[:, :, V_HEAD_DIM:], g_wuv[:, :, :V_HEAD_DIM])
    g_ukv = jnp.concatenate([g_wuk[:, :, :QK_NOPE_DIM], g_v], axis=2).transpose(1, 0, 2).reshape(KV_LORA, -1)
    grads = {
        "w_ada": g_ada, "b_ada": g_b_ada, "norm1_g": g_norm1, "w_in": g_in, "q_latent_g": g_gq, "w_uq": g_uq,
        "kv_latent_g": g_gkv, "w_ukv": g_ukv, "qk_norm_q_g": g_gqn[:, :QK_HEAD_DIM], "qk_norm_k_g": g_gkn[:, :QK_HEAD_DIM],
        "w_o_mla": g_o_mla, "conv_w": g_conv_w[:CONV_WIDTH], "conv_b": g_conv_b, "conv_ln_g": g_ln_g,
        "conv_ln_b": g_ln_b, "w_pw_out": g_pw, "w_out": g_out, "norm2_g": g_norm2, "w_ff1": g_ff1, "w_ff2": g_ff2,
    }
    return sq_err, grad_x.reshape(n_b, seq, d), grads


def _exchange(name, buf, gather):
    rows = buf.shape[-2]

    def body(src_ref, dst_ref, send_sems, recv_sems, local_sem):
        x, y, c = lax.axis_index("x"), lax.axis_index("y"), lax.axis_index("c")
        me = 4 * x + 2 * y + c

        def peer(kk):
            px = 1 - x if kk & 4 else x
            py = 1 - y if kk & 2 else y
            pc = 1 - c if kk & 1 else c
            return (px, py, pc), 4 * px + 2 * py + pc

        def copy(kk):
            dev, pid = peer(kk)
            return pltpu.make_async_remote_copy(
                src_ref=src_ref if gather else src_ref.at[pid], dst_ref=dst_ref.at[me],
                send_sem=send_sems.at[kk - 1], recv_sem=recv_sems.at[kk - 1],
                device_id=dev, device_id_type=pl.DeviceIdType.MESH)

        def arrival(kk):
            dev, pid = peer(kk)
            return pltpu.make_async_remote_copy(
                src_ref=src_ref if gather else src_ref.at[me], dst_ref=dst_ref.at[pid],
                send_sem=send_sems.at[kk - 1], recv_sem=recv_sems.at[kk - 1],
                device_id=dev, device_id_type=pl.DeviceIdType.MESH)

        mine = pltpu.make_async_copy(src_ref if gather else src_ref.at[me], dst_ref.at[me], local_sem)
        mine.start()
        sends = [copy(kk) for kk in range(1, N_DEV)]
        for cp in sends:
            cp.start()
        for kk in range(1, N_DEV):
            arrival(kk).wait_recv()
        for cp in sends:
            cp.wait_send()
        mine.wait()

    return pl.pallas_call(
        body, name=name,
        out_shape=jax.ShapeDtypeStruct((N_DEV, rows, LANES), buf.dtype),
        in_specs=[pl.BlockSpec(memory_space=pltpu.HBM)],
        out_specs=pl.BlockSpec(memory_space=pltpu.HBM),
        scratch_shapes=[pltpu.SemaphoreType.DMA((N_DEV - 1,)), pltpu.SemaphoreType.DMA((N_DEV - 1,)),
                        pltpu.SemaphoreType.DMA],
    )(buf)


def _sum_slots(name, slots):
    rows = slots.shape[1]
    tr = rows
    for cand in (1024, 512, 256, 128, 64, 32, 16, 8):
        if rows % cand == 0:
            tr = cand
            break

    def body(s_ref, o_ref):
        acc = s_ref[0].astype(F32)
        for j in range(1, N_DEV):
            acc = acc + s_ref[j].astype(F32)
        o_ref[...] = acc

    return pl.pallas_call(
        body, name=name, grid=(rows // tr,),
        in_specs=[pl.BlockSpec((N_DEV, tr, LANES), lambda i: (0, i, 0))],
        out_specs=pl.BlockSpec((tr, LANES), lambda i: (i, 0)),
        out_shape=jax.ShapeDtypeStruct((rows, LANES), F32),
        compiler_params=_params(("parallel",)),
    )(slots)


def _adamw(name, w, g, m, v):
    def body(w_ref, g_ref, m_ref, v_ref, d_ref, nm_ref, nv_ref):
        gg = g_ref[...]
        nm = ADAM_B1 * m_ref[...] + (1.0 - ADAM_B1) * gg
        nv = ADAM_B2 * v_ref[...] + (1.0 - ADAM_B2) * jnp.square(gg)
        m_hat = nm / (1.0 - ADAM_B1 ** ADAM_STEP)
        v_hat = nv / (1.0 - ADAM_B2 ** ADAM_STEP)
        d_ref[...] = -ADAM_LR * (m_hat / (jnp.sqrt(v_hat) + ADAM_EPS) + ADAM_WD * w_ref[...])
        nm_ref[...] = nm
        nv_ref[...] = nv

    shape = jax.ShapeDtypeStruct(w.shape, F32)
    return pl.pallas_call(body, name=name, out_shape=[shape] * 3, compiler_params=_params(None))(w, g, m, v)


def _pad_flat(v, align):
    flat = v.reshape(-1)
    return jnp.pad(flat, (0, (-flat.shape[0]) % align))


def _padded_size(n, align):
    return n + (-n) % align


def kernel(x, c, w_ada, b_ada, norm1_g, w_in, q_latent_g, w_uq, kv_latent_g, w_ukv, qk_norm_q_g, qk_norm_k_g, w_o_mla, conv_w, conv_b, conv_ln_g, conv_ln_b, w_pw_out, w_out, norm2_g, w_ff1, w_ff2, loss_target, m_w_ada, m_b_ada, m_norm1_g, m_w_in, m_q_latent_g, m_w_uq, m_kv_latent_g, m_w_ukv, m_qk_norm_q_g, m_qk_norm_k_g, m_w_o_mla, m_conv_w, m_conv_b, m_conv_ln_g, m_conv_ln_b, m_w_pw_out, m_w_out, m_norm2_g, m_w_ff1, m_w_ff2, v_w_ada, v_b_ada, v_norm1_g, v_w_in, v_q_latent_g, v_w_uq, v_kv_latent_g, v_w_ukv, v_qk_norm_q_g, v_qk_norm_k_g, v_w_o_mla, v_conv_w, v_conv_b, v_conv_ln_g, v_conv_ln_b, v_w_pw_out, v_w_out, v_norm2_g, v_w_ff1, v_w_ff2):
    given = dict(locals())
    local = {n: given[n][0] for n in WEIGHTS}
    shard_shape = {n: local[n].shape for n in SHARDED}

    sizes = {n: _padded_size(local[n].size, PACK_ALIGN) for n in SHARDED}
    total = _padded_size(sum(sizes.values()), 512 * LANES)
    packed = jnp.concatenate([_pad_flat(local[n].astype(BF16), PACK_ALIGN) for n in SHARDED])
    packed = jnp.pad(packed, (0, total - packed.shape[0])).reshape(total // LANES, LANES)
    gathered = _exchange("gather_weights", packed, gather=True).reshape(N_DEV, total)
    whole = {}
    off = 0
    for n in SHARDED:
        rows_, cols_ = shard_shape[n]
        blk = gathered[:, off:off + rows_ * cols_].reshape(N_DEV, rows_, cols_)
        off += sizes[n]
        if n in ROW_SHARDED:
            whole[n] = blk.reshape(N_DEV * rows_, cols_)
        else:
            whole[n] = blk.transpose(1, 0, 2).reshape(rows_, N_DEV * cols_)
    for n in REPLICATED:
        whole[n] = local[n].reshape(1, -1)

    sq_err, grad_x, grads = _local_step(x, c, whole, loss_target)
    loss = lax.psum(sq_err[0, 0] * (0.5 / x.shape[-1]), MESH_AXES)

    parts = []
    for n in SHARDED:
        rows_, cols_ = shard_shape[n]
        g = grads[n]
        if n in ROW_SHARDED:
            blk = g.reshape(N_DEV, rows_ * cols_)
        else:
            blk = g.reshape(rows_, N_DEV, cols_).transpose(1, 0, 2).reshape(N_DEV, rows_ * cols_)
        parts.append(jnp.pad(blk.astype(BF16), ((0, 0), (0, sizes[n] - rows_ * cols_))))
    parts.append(jnp.zeros((N_DEV, total - sum(sizes.values())), BF16))
    g_packed = jnp.concatenate(parts, axis=1).reshape(N_DEV, total // LANES, LANES)
    g_slots = _exchange("scatter_grads", g_packed, gather=False)
    g_mine = _sum_slots("sum_grads", g_slots).reshape(total)

    small_sizes = {n: _padded_size(local[n].size, LANES) for n in REPLICATED}
    small_total = _padded_size(sum(small_sizes.values()), 8 * LANES)
    small = jnp.concatenate([_pad_flat(grads[n], LANES) for n in REPLICATED])
    small = jnp.pad(small, (0, small_total - small.shape[0])).reshape(small_total // LANES, LANES)
    small_slots = _exchange("gather_small_grads", small, gather=True)
    small_sum = _sum_slots("sum_small_grads", small_slots).reshape(small_total)

    final = {}
    off = 0
    for n in SHARDED:
        rows_, cols_ = shard_shape[n]
        final[n] = g_mine[off:off + rows_ * cols_].reshape(rows_, cols_)
        off += sizes[n]
    off = 0
    for n in REPLICATED:
        final[n] = small_sum[off:off + local[n].size].reshape(1, -1)
        off += small_sizes[n]

    grad_out, delta_out, m_out, v_out = [], [], [], []
    for n in WEIGHTS:
        shape2 = final[n].shape
        d_w, n_m, n_v = _adamw("adamw_" + n, local[n].reshape(shape2), final[n], given["m_" + n].reshape(shape2),
                               given["v_" + n].reshape(shape2))
        full_shape = given[n].shape
        grad_out.append(final[n].reshape(full_shape))
        delta_out.append(d_w.reshape(full_shape))
        m_out.append(n_m.reshape(full_shape))
        v_out.append(n_v.reshape(full_shape))
    return (loss, grad_x, *grad_out, *delta_out, *m_out, *v_out)
```

```python
import functools

import jax
import jax.numpy as jnp
from jax import lax
from jax.experimental import pallas as pl
from jax.experimental.pallas import tpu as pltpu

F32 = jnp.float32
BF16 = jnp.bfloat16

N_DEV = 8
MESH_AXES = ("x", "y", "c")
EPS = 1e-6
N_HEADS = 8
QK_HEAD_DIM = 96
QK_NOPE_DIM = 64
QK_ROPE_DIM = 32
V_HEAD_DIM = 64
HEAD_PAD = 128
Q_LORA = 256
KV_LORA = 128
CONV_CH = 512
CONV_WIDTH = 31
CONV_HALO = 32
CHUNK = 64
ROPE_THETA = 10000.0
OFF_Q = Q_LORA
OFF_KV = OFF_Q + KV_LORA
OFF_KR = OFF_KV + QK_ROPE_DIM
OFF_GLU = OFF_KR + 2 * CONV_CH
ADAM_LR = 0.001
ADAM_B1 = 0.9
ADAM_B2 = 0.999
ADAM_EPS = 1e-08
ADAM_WD = 0.01
ADAM_STEP = 10
LANES = 128
PACK_ALIGN = 16 * LANES
VMEM_LIMIT = 56 * 1024 * 1024
NEG_BIG = -1e30
ATT_HEADS = 4
ATT_SCALE = QK_HEAD_DIM ** -0.5
LOG2E = 1.4426950408889634
LN2 = 0.6931471805599453
QK_SCALE = ATT_SCALE * LOG2E
ROWS_PAD = 16

SHARDED = ("w_ada", "w_in", "w_uq", "w_ukv", "w_o_mla", "conv_w", "w_pw_out", "w_out", "w_ff1", "w_ff2")
ROW_SHARDED = ("w_out", "w_ff2")
REPLICATED = ("b_ada", "norm1_g", "q_latent_g", "kv_latent_g", "qk_norm_q_g", "qk_norm_k_g", "conv_b", "conv_ln_g",
              "conv_ln_b", "norm2_g")
WEIGHTS = ("w_ada", "b_ada", "norm1_g", "w_in", "q_latent_g", "w_uq", "kv_latent_g", "w_ukv", "qk_norm_q_g",
           "qk_norm_k_g", "w_o_mla", "conv_w", "conv_b", "conv_ln_g", "conv_ln_b", "w_pw_out", "w_out", "norm2_g",
           "w_ff1", "w_ff2")


def _tile(dim, pref):
    if dim <= pref:
        return dim
    t = (pref // LANES) * LANES
    while dim % t:
        t -= LANES
    return t


def _params(semantics):
    return pltpu.CompilerParams(dimension_semantics=semantics, vmem_limit_bytes=VMEM_LIMIT)


def _sigmoid(v):
    return 1.0 / (1.0 + jnp.exp(-v))


_DIMS = {"nn": (((1,), (0,)), ((), ())), "nt": (((1,), (1,)), ((), ())), "tn": (((0,), (0,)), ((), ()))}


def _mm(name, a, b, mode, out_dtype, *, a_fn=None, epi=None, epi_in=(), tm=512, tn=1024, tk=1024):
    if mode == "nn":
        (m, k), n = a.shape, b.shape[1]
    elif mode == "nt":
        (m, k), n = a.shape, b.shape[0]
    else:
        (k, m), n = a.shape, b.shape[1]
    tm, tn, tk = _tile(m, tm), _tile(n, tn), _tile(k, tk)
    nk = k // tk
    a_spec = (pl.BlockSpec((tk, tm), lambda i, j, kk: (kk, i)) if mode == "tn"
              else pl.BlockSpec((tm, tk), lambda i, j, kk: (i, kk)))
    b_spec = (pl.BlockSpec((tn, tk), lambda i, j, kk: (j, kk)) if mode == "nt"
              else pl.BlockSpec((tk, tn), lambda i, j, kk: (kk, j)))
    o_spec = pl.BlockSpec((tm, tn), lambda i, j, kk: (i, j))
    n_epi = len(epi_in)

    def body(a_ref, b_ref, *rest):
        epi_refs, o_ref, acc_ref = rest[:n_epi], rest[n_epi], rest[n_epi + 1]
        kk = pl.program_id(2)

        @pl.when(kk == 0)
        def _():
            acc_ref[...] = jnp.zeros_like(acc_ref)

        av = a_ref[...]
        if a_fn is not None:
            av = a_fn(av.astype(F32))
        acc_ref[...] += lax.dot_general(av.astype(BF16), b_ref[...].astype(BF16), _DIMS[mode],
                                        preferred_element_type=F32)

        @pl.when(kk == nk - 1)
        def _():
            acc = acc_ref[...]
            if epi is not None:
                acc = epi(acc, *[r[...].astype(F32) for r in epi_refs])
            o_ref[...] = acc.astype(out_dtype)

    return pl.pallas_call(
        body, name=name, grid=(m // tm, n // tn, nk),
        in_specs=[a_spec, b_spec] + [o_spec] * n_epi, out_specs=o_spec,
        out_shape=jax.ShapeDtypeStruct((m, n), out_dtype),
        scratch_shapes=[pltpu.VMEM((tm, tn), F32)],
        compiler_params=_params(("parallel", "parallel", "arbitrary")),
    )(a, b, *epi_in)


def _rowwise(name, fn, n_rows, seq, rows, bats=(), consts=(), outs=(), bat_outs=(), tot_outs=(), tm=256):
    tm = min(tm, seq)
    per_seq = seq // tm
    n_b = n_rows // seq
    nr, nb, nc, no, nbo, nto = len(rows), len(bats), len(consts), len(outs), len(bat_outs), len(tot_outs)

    def body(*refs):
        i = pl.program_id(0)
        r_in = [r[...] for r in refs[:nr]]
        b_in = [r[0] for r in refs[nr:nr + nb]]
        c_in = [r[...] for r in refs[nr + nb:nr + nb + nc]]
        o_refs = refs[nr + nb + nc:nr + nb + nc + no]
        bo_refs = refs[nr + nb + nc + no:nr + nb + nc + no + nbo]
        to_refs = refs[nr + nb + nc + no + nbo:]
        o_val, bo_val, to_val = fn(r_in, b_in, c_in)
        for r, v in zip(o_refs, o_val):
            r[...] = v.astype(r.dtype)
        if nbo:
            @pl.when(i % per_seq == 0)
            def _():
                for r in bo_refs:
                    r[...] = jnp.zeros_like(r)

            for r, v in zip(bo_refs, bo_val):
                r[0] += v
        if nto:
            @pl.when(i == 0)
            def _():
                for r in to_refs:
                    r[...] = jnp.zeros_like(r)

            for r, v in zip(to_refs, to_val):
                r[...] += v

    in_specs = [pl.BlockSpec((tm, w), functools.partial(lambda cb, i: (i, cb), cb)) for (_, w, cb) in rows]
    in_specs += [pl.BlockSpec((1, 1, bt.shape[2]), lambda i: (i // per_seq, 0, 0)) for bt in bats]
    in_specs += [pl.BlockSpec(ct.shape, lambda i: (0, 0)) for ct in consts]
    out_specs = [pl.BlockSpec((tm, w), lambda i: (i, 0)) for (w, _) in outs]
    out_specs += [pl.BlockSpec((1, 1, w), lambda i: (i // per_seq, 0, 0)) for w in bat_outs]
    out_specs += [pl.BlockSpec(shp, lambda i: (0, 0)) for shp in tot_outs]
    out_shape = [jax.ShapeDtypeStruct((n_rows, w), dt) for (w, dt) in outs]
    out_shape += [jax.ShapeDtypeStruct((n_b, 1, w), F32) for w in bat_outs]
    out_shape += [jax.ShapeDtypeStruct(shp, F32) for shp in tot_outs]
    res = pl.pallas_call(
        body, name=name, grid=(n_rows // tm,), in_specs=in_specs, out_specs=out_specs, out_shape=out_shape,
        compiler_params=_params(("arbitrary",)),
    )(*[r[0] for r in rows], *bats, *consts)
    return res


def _full(arr):
    return (arr, arr.shape[1], 0)


def _norm_mod(x, g, scale, shift):
    r = lax.rsqrt(jnp.mean(x * x, axis=-1, keepdims=True) + EPS)
    xh = x * r
    return xh * g * (1.0 + scale) + shift


def _norm_mod_bwd(x, g, scale, dh):
    r = lax.rsqrt(jnp.mean(x * x, axis=-1, keepdims=True) + EPS)
    xh = x * r
    dn = dh * (1.0 + scale)
    dxh = dn * g
    dx = r * (dxh - xh * jnp.mean(dxh * xh, axis=-1, keepdims=True))
    dscale = jnp.sum(dh * xh * g, axis=0, keepdims=True)
    dshift = jnp.sum(dh, axis=0, keepdims=True)
    dg = jnp.sum(dn * xh, axis=0, keepdims=True)
    return dx, dscale, dshift, dg


def _rms(v, g):
    r = lax.rsqrt(jnp.mean(v * v, axis=-1, keepdims=True) + EPS)
    return v * r * g


def _rms_bwd(v, g, dy):
    r = lax.rsqrt(jnp.mean(v * v, axis=-1, keepdims=True) + EPS)
    vh = v * r
    dvh = dy * g
    dv = r * (dvh - vh * jnp.mean(dvh * vh, axis=-1, keepdims=True))
    return dv, jnp.sum(dy * vh, axis=0, keepdims=True)


def _head_norm(v, g):
    r = lax.rsqrt(jnp.sum(v * v, axis=-1, keepdims=True) * (1.0 / QK_HEAD_DIM) + EPS)
    return v * r * g


def _head_norm_bwd(v, g, dy):
    r = lax.rsqrt(jnp.sum(v * v, axis=-1, keepdims=True) * (1.0 / QK_HEAD_DIM) + EPS)
    vh = v * r
    dvh = dy * g
    dv = r * (dvh - vh * (jnp.sum(dvh * vh, axis=-1, keepdims=True) * (1.0 / QK_HEAD_DIM)))
    return dv, jnp.sum(dy * vh, axis=0, keepdims=True)


def _rope(v, cos, sin_lo, sin_hi):
    return v * cos + pltpu.roll(v, HEAD_PAD - 16, 1) * sin_lo + pltpu.roll(v, 16, 1) * sin_hi


def _rope_bwd(g, cos, sin_lo, sin_hi):
    return g * cos + pltpu.roll(g * sin_lo, 16, 1) + pltpu.roll(g * sin_hi, HEAD_PAD - 16, 1)


def _mla_prep_fwd(zsm, wuq, wuk, wuv, gq, gkv, gqn, gkn, rope, n_b, seq):
    n_rows = n_b * seq
    tm = min(256, seq)
    per_seq = seq // tm

    def body(z_ref, wuq_ref, wuk_ref, wuv_ref, gq_ref, gkv_ref, gqn_ref, gkn_ref, c_ref, s1_ref, s2_ref,
             q_ref, k_ref, v_ref):
        z = z_ref[...]
        qn = _rms(z[:, :Q_LORA], gq_ref[...]).astype(BF16)
        kvn = _rms(z[:, Q_LORA:Q_LORA + KV_LORA], gkv_ref[...]).astype(BF16)
        krp = z[:, Q_LORA + KV_LORA:]
        cos, s1, s2 = c_ref[...], s1_ref[...], s2_ref[...]
        for h in range(N_HEADS):
            qh = jnp.dot(qn, wuq_ref[h], preferred_element_type=F32)
            q_ref[0, h] = (_rope(_head_norm(qh, gqn_ref[...]), cos, s1, s2) * QK_SCALE).astype(BF16)
            kh = jnp.dot(kvn, wuk_ref[h], preferred_element_type=F32) + krp
            k_ref[0, h] = _rope(_head_norm(kh, gkn_ref[...]), cos, s1, s2).astype(BF16)
            v_ref[0, h] = jnp.dot(kvn, wuv_ref[h], preferred_element_type=F32).astype(BF16)

    whole3 = lambda arr: pl.BlockSpec(arr.shape, lambda i: (0, 0, 0))
    whole2 = lambda arr: pl.BlockSpec(arr.shape, lambda i: (0, 0))
    rope_spec = pl.BlockSpec((tm, HEAD_PAD), lambda i: (i % per_seq, 0))
    head_spec = pl.BlockSpec((1, N_HEADS, tm, HEAD_PAD), lambda i: (i // per_seq, 0, i % per_seq, 0))
    head_shape = jax.ShapeDtypeStruct((n_b, N_HEADS, seq, HEAD_PAD), BF16)
    return pl.pallas_call(
        body, name="mla_prep_fwd", grid=(n_rows // tm,),
        in_specs=[pl.BlockSpec((tm, 512), lambda i: (i, 0)), whole3(wuq), whole3(wuk), whole3(wuv),
                  whole2(gq), whole2(gkv), whole2(gqn), whole2(gkn), rope_spec, rope_spec, rope_spec],
        out_specs=[head_spec] * 3, out_shape=[head_shape] * 3,
        compiler_params=_params(("parallel",)),
    )(zsm, wuq, wuk, wuv, gq, gkv, gqn, gkn, *rope)


def _mla_prep_bwd(zsm, dq, dk, dv, wuq, wuk, wuv, gq, gkv, gqn, gkn, rope, n_b, seq):
    n_rows = n_b * seq
    tm = min(256, seq)
    per_seq = seq // tm
    tn_dims = _DIMS["tn"]
    nt_dims = _DIMS["nt"]

    def body(z_ref, dq_ref, dk_ref, dv_ref, wuq_ref, wuk_ref, wuv_ref, gq_ref, gkv_ref, gqn_ref, gkn_ref,
             c_ref, s1_ref, s2_ref, dz_ref, dwuq_ref, dwuk_ref, dwuv_ref, dgq_ref, dgkv_ref, dgqn_ref, dgkn_ref):
        @pl.when(pl.program_id(0) == 0)
        def _():
            for r in (dwuq_ref, dwuk_ref, dwuv_ref, dgq_ref, dgkv_ref, dgqn_ref, dgkn_ref):
                r[...] = jnp.zeros_like(r)

        z = z_ref[...]
        zq, zkv, krp = z[:, :Q_LORA], z[:, Q_LORA:Q_LORA + KV_LORA], z[:, Q_LORA + KV_LORA:]
        qn = _rms(zq, gq_ref[...]).astype(BF16)
        kvn = _rms(zkv, gkv_ref[...]).astype(BF16)
        cos, s1, s2 = c_ref[...], s1_ref[...], s2_ref[...]
        lane = lax.broadcasted_iota(jnp.int32, (tm, HEAD_PAD), 1)
        rope_lanes = (lane >= QK_NOPE_DIM) & (lane < QK_HEAD_DIM)
        dqn = jnp.zeros((tm, Q_LORA), F32)
        dkvn = jnp.zeros((tm, KV_LORA), F32)
        dkrp = jnp.zeros((tm, HEAD_PAD), F32)
        dgqn = jnp.zeros((1, HEAD_PAD), F32)
        dgkn = jnp.zeros((1, HEAD_PAD), F32)
        for h in range(N_HEADS):
            qh = jnp.dot(qn, wuq_ref[h], preferred_element_type=F32)
            dqh, dg = _head_norm_bwd(qh, gqn_ref[...], _rope_bwd(dq_ref[0, h].astype(F32) * ATT_SCALE, cos, s1, s2))
            dgqn += dg
            dqh = dqh.astype(BF16)
            dwuq_ref[h] += lax.dot_general(qn, dqh, tn_dims, preferred_element_type=F32)
            dqn += lax.dot_general(dqh, wuq_ref[h], nt_dims, preferred_element_type=F32)

            kh = jnp.dot(kvn, wuk_ref[h], preferred_element_type=F32) + krp
            dkh, dg = _head_norm_bwd(kh, gkn_ref[...], _rope_bwd(dk_ref[0, h].astype(F32), cos, s1, s2))
            dgkn += dg
            dkrp += jnp.where(rope_lanes, dkh, 0.0)
            dkh = dkh.astype(BF16)
            dwuk_ref[h] += lax.dot_general(kvn, dkh, tn_dims, preferred_element_type=F32)
            dkvn += lax.dot_general(dkh, wuk_ref[h], nt_dims, preferred_element_type=F32)

            dvh = dv_ref[0, h]
            dwuv_ref[h] += lax.dot_general(kvn, dvh, tn_dims, preferred_element_type=F32)
            dkvn += lax.dot_general(dvh, wuv_ref[h], nt_dims, preferred_element_type=F32)
        dzq, dg = _rms_bwd(zq, gq_ref[...], dqn)
        dgq_ref[...] += dg
        dzkv, dg = _rms_bwd(zkv, gkv_ref[...], dkvn)
        dgkv_ref[...] += dg
        dgqn_ref[...] += dgqn
        dgkn_ref[...] += dgkn
        dz_ref[:, :Q_LORA] = dzq.astype(dz_ref.dtype)
        dz_ref[:, Q_LORA:Q_LORA + KV_LORA] = dzkv.astype(dz_ref.dtype)
        dz_ref[:, Q_LORA + KV_LORA:] = dkrp.astype(dz_ref.dtype)

    whole3 = lambda arr: pl.BlockSpec(arr.shape, lambda i: (0, 0, 0))
    whole2 = lambda arr: pl.BlockSpec(arr.shape, lambda i: (0, 0))
    rope_spec = pl.BlockSpec((tm, HEAD_PAD), lambda i: (i % per_seq, 0))
    head_spec = pl.BlockSpec((1, N_HEADS, tm, HEAD_PAD), lambda i: (i // per_seq, 0, i % per_seq, 0))
    row_spec = pl.BlockSpec((tm, 512), lambda i: (i, 0))
    return pl.pallas_call(
        body, name="mla_prep_bwd", grid=(n_rows // tm,),
        in_specs=[row_spec, head_spec, head_spec, head_spec, whole3(wuq), whole3(wuk), whole3(wuv),
                  whole2(gq), whole2(gkv), whole2(gqn), whole2(gkn), rope_spec, rope_spec, rope_spec],
        out_specs=[row_spec, whole3(wuq), whole3(wuk), whole3(wuv), whole2(gq), whole2(gkv), whole2(gqn), whole2(gkn)],
        out_shape=[jax.ShapeDtypeStruct((n_rows, 512), BF16),
                   jax.ShapeDtypeStruct(wuq.shape, F32), jax.ShapeDtypeStruct(wuk.shape, F32),
                   jax.ShapeDtypeStruct(wuv.shape, F32), jax.ShapeDtypeStruct(gq.shape, F32),
                   jax.ShapeDtypeStruct(gkv.shape, F32), jax.ShapeDtypeStruct(gqn.shape, F32),
                   jax.ShapeDtypeStruct(gkn.shape, F32)],
        compiler_params=_params(("arbitrary",)),
    )(zsm, dq, dk, dv, wuq, wuk, wuv, gq, gkv, gqn, gkn, *rope)


def _chunk_mask(t):
    r = lax.broadcasted_iota(jnp.int32, (t, t), 0) // CHUNK
    c = lax.broadcasted_iota(jnp.int32, (t, t), 1) // CHUNK
    return r >= c


def _attn_fwd(q, k, v, n_b, seq):
    tq = min(256, seq)
    nq = seq // tq
    nt_dims = _DIMS["nt"]
    hpb = ATT_HEADS

    def body(q_ref, k_ref, v_ref, o_ref, lse_ref):
        qi = pl.program_id(2)
        mask = _chunk_mask(tq)
        qs = [q_ref[0, hh] for hh in range(hpb)]

        def step(j, carry, masked):
            rows = pl.ds(pl.multiple_of(j * tq, tq), tq)
            out = []
            for hh in range(hpb):
                m, l, acc = carry[hh]
                s = lax.dot_general(qs[hh], k_ref[0, hh, rows, :], nt_dims, preferred_element_type=F32)
                if masked:
                    s = jnp.where(mask, s, NEG_BIG)
                m_new = jnp.maximum(m, jnp.max(s, axis=-1, keepdims=True))
                alpha = jnp.exp2(m - m_new)
                p = jnp.exp2(s - m_new)
                l = alpha * l + jnp.sum(p, axis=-1, keepdims=True)
                acc = alpha * acc + jnp.dot(p.astype(BF16), v_ref[0, hh, rows, :], preferred_element_type=F32)
                out.append((m_new, l, acc))
            return tuple(out)

        init = tuple((jnp.full((tq, 1), NEG_BIG, F32), jnp.zeros((tq, 1), F32), jnp.zeros((tq, HEAD_PAD), F32))
                     for _ in range(hpb))
        carry = lax.fori_loop(0, qi, functools.partial(step, masked=False), init)
        carry = step(qi, carry, True)
        for pair in range(hpb // 2):
            (m0, l0, a0), (m1, l1, a1) = carry[2 * pair], carry[2 * pair + 1]
            o_ref[0, :, pair * HEAD_PAD:(pair + 1) * HEAD_PAD] = (a0 * (1.0 / l0) + a1 * (1.0 / l1)).astype(BF16)
            lse_ref[0, 2 * pair] = jnp.broadcast_to(m0 + jnp.log2(l0), (tq, HEAD_PAD))
            lse_ref[0, 2 * pair + 1] = jnp.broadcast_to(m1 + jnp.log2(l1), (tq, HEAD_PAD))

    kv_spec = pl.BlockSpec((1, hpb, seq, HEAD_PAD), lambda b, hb, i: (b, hb, 0, 0))
    q_spec = pl.BlockSpec((1, hpb, tq, HEAD_PAD), lambda b, hb, i: (b, hb, i, 0))
    return pl.pallas_call(
        body, name="attn_fwd", grid=(n_b, N_HEADS // hpb, nq),
        in_specs=[q_spec, kv_spec, kv_spec],
        out_specs=[pl.BlockSpec((1, tq, hpb * V_HEAD_DIM), lambda b, hb, i: (b, i, hb)), q_spec],
        out_shape=[jax.ShapeDtypeStruct((n_b, seq, N_HEADS * V_HEAD_DIM), BF16),
                   jax.ShapeDtypeStruct((n_b, N_HEADS, seq, HEAD_PAD), F32)],
        compiler_params=_params(("parallel", "parallel", "arbitrary")),
    )(q, k, v)


def _attn_bwd(q, k, v, do, o, lse, n_b, seq):
    tq = min(256, seq)
    nq = seq // tq
    nt_dims = _DIMS["nt"]
    tn_dims = _DIMS["tn"]
    hpb = ATT_HEADS

    def body(q_ref, k_ref, v_ref, do_ref, o_ref, lse_ref, dq_ref, dk_ref, dv_ref, dk_acc, dv_acc):
        qi = pl.program_id(2)

        @pl.when(qi == 0)
        def _():
            dk_acc[...] = jnp.zeros_like(dk_acc)
            dv_acc[...] = jnp.zeros_like(dv_acc)

        mask = _chunk_mask(tq)
        lane = lax.broadcasted_iota(jnp.int32, (tq, HEAD_PAD), 1)
        qs, dos, deltas, lses = [], [], [], []
        for hh in range(hpb):
            cols = slice((hh // 2) * HEAD_PAD, (hh // 2 + 1) * HEAD_PAD)
            do_pair = do_ref[0, :, cols]
            prod = do_pair.astype(F32) * o_ref[0, :, cols].astype(F32)
            qs.append(q_ref[0, hh])
            dos.append(do_pair)
            deltas.append(jnp.sum(jnp.where(lane // V_HEAD_DIM == hh % 2, prod, 0.0), axis=-1, keepdims=True))
            lses.append(lse_ref[0, hh][:, :1])

        def step(j, dqs, masked):
            rows = pl.ds(pl.multiple_of(j * tq, tq), tq)
            out = []
            for hh in range(hpb):
                kj = k_ref[0, hh, rows, :]
                s = lax.dot_general(qs[hh], kj, nt_dims, preferred_element_type=F32)
                p = jnp.exp2(s - lses[hh])
                if masked:
                    p = jnp.where(mask, p, 0.0)
                dv_acc[hh, rows, :] += lax.dot_general(p.astype(BF16), dos[hh], tn_dims, preferred_element_type=F32)
                dp = lax.dot_general(dos[hh], v_ref[0, hh, rows, :], nt_dims, preferred_element_type=F32)
                ds = (p * (dp - deltas[hh])).astype(BF16)
                dk_acc[hh, rows, :] += lax.dot_general(ds, qs[hh], tn_dims, preferred_element_type=F32)
                out.append(dqs[hh] + jnp.dot(ds, kj, preferred_element_type=F32))
            return tuple(out)

        dqs = tuple(jnp.zeros((tq, HEAD_PAD), F32) for _ in range(hpb))
        dqs = lax.fori_loop(0, qi, functools.partial(step, masked=False), dqs)
        dqs = step(qi, dqs, True)
        for hh in range(hpb):
            dq_ref[0, hh] = dqs[hh].astype(BF16)

        @pl.when(qi == nq - 1)
        def _():
            dk_ref[0] = (dk_acc[...] * LN2).astype(BF16)
            dv_ref[0] = dv_acc[...].astype(BF16)

    full_spec = pl.BlockSpec((1, hpb, seq, HEAD_PAD), lambda b, hb, i: (b, hb, 0, 0))
    q_spec = pl.BlockSpec((1, hpb, tq, HEAD_PAD), lambda b, hb, i: (b, hb, i, 0))
    o_spec = pl.BlockSpec((1, tq, hpb * V_HEAD_DIM), lambda b, hb, i: (b, i, hb))
    head_shape = jax.ShapeDtypeStruct((n_b, N_HEADS, seq, HEAD_PAD), BF16)
    return pl.pallas_call(
        body, name="attn_bwd", grid=(n_b, N_HEADS // hpb, nq),
        in_specs=[q_spec, full_spec, full_spec, o_spec, o_spec, q_spec],
        out_specs=[q_spec, full_spec, full_spec], out_shape=[head_shape] * 3,
        scratch_shapes=[pltpu.VMEM((hpb, seq, HEAD_PAD), F32), pltpu.VMEM((hpb, seq, HEAD_PAD), F32)],
        compiler_params=_params(("parallel", "parallel", "arbitrary")),
    )(q, k, v, do, o, lse)


def _ln_silu(u1, g, b):
    mu = jnp.mean(u1, axis=-1, keepdims=True)
    uc = u1 - mu
    r = lax.rsqrt(jnp.mean(uc * uc, axis=-1, keepdims=True) + EPS)
    y = uc * r * g + b
    return y * _sigmoid(y)


def _conv_fill_glu(z_ref, u0_ref, seq, tile):
    u0_ref[0:CONV_HALO, :] = jnp.zeros((CONV_HALO, CONV_CH), F32)
    for t in range(seq // tile):
        zt = z_ref[0, t * tile:(t + 1) * tile, :].astype(F32)
        u0_ref[CONV_HALO + t * tile:CONV_HALO + (t + 1) * tile, :] = zt[:, :CONV_CH] * _sigmoid(zt[:, CONV_CH:])


def _conv_tile(u0_ref, w_ref, b_ref, t, tile):
    acc = jnp.broadcast_to(b_ref[...], (tile, CONV_CH))
    base = t * tile + CONV_HALO - (CONV_WIDTH - 1)
    for kk in range(CONV_WIDTH):
        acc = acc + w_ref[kk:kk + 1, :] * u0_ref[base + kk:base + kk + tile, :]
    return acc


def _conv_fwd(zglu, conv_w, conv_b, ln_g, ln_b, n_b, seq):
    tile = min(256, seq)

    def body(z_ref, w_ref, b_ref, g_ref, bb_ref, o_ref, u0_ref):
        _conv_fill_glu(z_ref, u0_ref, seq, tile)
        for t in range(seq // tile):
            u1 = _conv_tile(u0_ref, w_ref, b_ref, t, tile)
            o_ref[0, t * tile:(t + 1) * tile, :] = _ln_silu(u1, g_ref[...], bb_ref[...]).astype(BF16)

    whole2 = lambda arr: pl.BlockSpec(arr.shape, lambda b: (0, 0))
    return pl.pallas_call(
        body, name="conv_fwd", grid=(n_b,),
        in_specs=[pl.BlockSpec((1, seq, 2 * CONV_CH), lambda b: (b, 0, 0)), whole2(conv_w), whole2(conv_b),
                  whole2(ln_g), whole2(ln_b)],
        out_specs=pl.BlockSpec((1, seq, CONV_CH), lambda b: (b, 0, 0)),
        out_shape=jax.ShapeDtypeStruct((n_b, seq, CONV_CH), BF16),
        scratch_shapes=[pltpu.VMEM((seq + CONV_HALO, CONV_CH), F32)],
        compiler_params=_params(("parallel",)),
    )(zglu, conv_w, conv_b, ln_g, ln_b)


def _conv_bwd(zglu, du3, conv_w, conv_b, ln_g, ln_b, n_b, seq):
    tile = min(256, seq)
    n_t = seq // tile

    def body(z_ref, du3_ref, w_ref, b_ref, g_ref, bb_ref, dz_ref, dw_ref, db_ref, dg_ref, dbb_ref, u0_ref, du1_ref):
        @pl.when(pl.program_id(0) == 0)
        def _():
            for r in (dw_ref, db_ref, dg_ref, dbb_ref):
                r[...] = jnp.zeros_like(r)

        _conv_fill_glu(z_ref, u0_ref, seq, tile)
        du1_ref[seq:seq + CONV_HALO, :] = jnp.zeros((CONV_HALO, CONV_CH), F32)
        g = g_ref[...]
        for t in range(n_t):
            u1 = _conv_tile(u0_ref, w_ref, b_ref, t, tile)
            mu = jnp.mean(u1, axis=-1, keepdims=True)
            uc = u1 - mu
            r = lax.rsqrt(jnp.mean(uc * uc, axis=-1, keepdims=True) + EPS)
            xh = uc * r
            y = xh * g + bb_ref[...]
            sg = _sigmoid(y)
            dy = du3_ref[0, t * tile:(t + 1) * tile, :].astype(F32) * (sg * (1.0 + y * (1.0 - sg)))
            dg_ref[...] += jnp.sum(dy * xh, axis=0, keepdims=True)
            dbb_ref[...] += jnp.sum(dy, axis=0, keepdims=True)
            dxh = dy * g
            du1 = r * (dxh - jnp.mean(dxh, axis=-1, keepdims=True) - xh * jnp.mean(dxh * xh, axis=-1, keepdims=True))
            db_ref[...] += jnp.sum(du1, axis=0, keepdims=True)
            du1_ref[t * tile:(t + 1) * tile, :] = du1
        for t in range(n_t):
            du1 = du1_ref[t * tile:(t + 1) * tile, :]
            du0 = jnp.zeros((tile, CONV_CH), F32)
            base_u = t * tile + CONV_HALO - (CONV_WIDTH - 1)
            base_d = t * tile + (CONV_WIDTH - 1)
            for kk in range(CONV_WIDTH):
                du0 = du0 + w_ref[kk:kk + 1, :] * du1_ref[base_d - kk:base_d - kk + tile, :]
                dw_ref[kk:kk + 1, :] += jnp.sum(du1 * u0_ref[base_u + kk:base_u + kk + tile, :], axis=0, keepdims=True)
            zt = z_ref[0, t * tile:(t + 1) * tile, :].astype(F32)
            ga, sb = zt[:, :CONV_CH], _sigmoid(zt[:, CONV_CH:])
            dz_ref[0, t * tile:(t + 1) * tile, :CONV_CH] = (du0 * sb).astype(BF16)
            dz_ref[0, t * tile:(t + 1) * tile, CONV_CH:] = (du0 * ga * sb * (1.0 - sb)).astype(BF16)

    whole2 = lambda arr: pl.BlockSpec(arr.shape, lambda b: (0, 0))
    z_spec = pl.BlockSpec((1, seq, 2 * CONV_CH), lambda b: (b, 0, 0))
    return pl.pallas_call(
        body, name="conv_bwd", grid=(n_b,),
        in_specs=[z_spec, pl.BlockSpec((1, seq, CONV_CH), lambda b: (b, 0, 0)), whole2(conv_w), whole2(conv_b),
                  whole2(ln_g), whole2(ln_b)],
        out_specs=[z_spec, whole2(conv_w), whole2(conv_b), whole2(ln_g), whole2(ln_b)],
        out_shape=[jax.ShapeDtypeStruct((n_b, seq, 2 * CONV_CH), BF16), jax.ShapeDtypeStruct(conv_w.shape, F32),
                   jax.ShapeDtypeStruct(conv_b.shape, F32), jax.ShapeDtypeStruct(ln_g.shape, F32),
                   jax.ShapeDtypeStruct(ln_b.shape, F32)],
        scratch_shapes=[pltpu.VMEM((seq + CONV_HALO, CONV_CH), F32), pltpu.VMEM((seq + CONV_HALO, CONV_CH), F32)],
        compiler_params=_params(("arbitrary",)),
    )(zglu, du3, conv_w, conv_b, ln_g, ln_b)


def _rope_tables(seq):
    inv_freq = ROPE_THETA ** (-jnp.arange(0, QK_ROPE_DIM, 2, dtype=F32) / QK_ROPE_DIM)
    ang = jnp.arange(seq, dtype=F32)[:, None] * inv_freq[None, :]
    cos, sin = jnp.cos(ang), jnp.sin(ang)
    half = QK_ROPE_DIM // 2
    z = lambda n: jnp.zeros((seq, n), F32)
    tail = HEAD_PAD - QK_HEAD_DIM
    cos_t = jnp.concatenate([jnp.ones((seq, QK_NOPE_DIM), F32), cos, cos, z(tail)], axis=1)
    sin_lo = jnp.concatenate([z(QK_NOPE_DIM), -sin, z(half), z(tail)], axis=1)
    sin_hi = jnp.concatenate([z(QK_NOPE_DIM), z(half), sin, z(tail)], axis=1)
    return cos_t, sin_lo, sin_hi


def _pad_lanes(v, width=HEAD_PAD):
    return jnp.pad(v, [(0, 0)] * (v.ndim - 1) + [(0, width - v.shape[-1])])


def _local_step(x, c, w, target):
    n_b, seq, d = x.shape
    n_rows = n_b * seq
    x2 = x.reshape(n_rows, d)
    t2 = target.reshape(n_rows, d)
    rw = functools.partial(_rowwise, n_rows=n_rows, seq=seq)

    w_in = w["w_in"]
    zeros = lambda n: jnp.zeros((d, n), BF16)
    w_sm = jnp.concatenate([w_in[:, :OFF_KV], zeros(QK_NOPE_DIM), w_in[:, OFF_KV:OFF_KR], zeros(HEAD_PAD - QK_HEAD_DIM)], axis=1)
    w_glu = w_in[:, OFF_KR:OFF_GLU]
    w_gate = w_in[:, OFF_GLU:]
    wuq = _pad_lanes(w["w_uq"].reshape(Q_LORA, N_HEADS, QK_HEAD_DIM)).transpose(1, 0, 2)
    wukv = w["w_ukv"].reshape(KV_LORA, N_HEADS, QK_NOPE_DIM + V_HEAD_DIM).transpose(1, 0, 2)
    wuk = _pad_lanes(wukv[:, :, :QK_NOPE_DIM])
    wv = wukv[:, :, QK_NOPE_DIM:]
    odd = (jnp.arange(N_HEADS) % 2 == 1)[:, None, None]
    wuv = jnp.where(odd, jnp.pad(wv, ((0, 0), (0, 0), (V_HEAD_DIM, 0))), jnp.pad(wv, ((0, 0), (0, 0), (0, V_HEAD_DIM))))
    gqn = _pad_lanes(w["qk_norm_q_g"])
    gkn = _pad_lanes(w["qk_norm_k_g"])
    conv_w = jnp.pad(w["conv_w"].astype(F32), ((0, 1), (0, 0)))
    rope = _rope_tables(seq)

    c8 = jnp.pad(c, ((0, ROWS_PAD - n_b), (0, 0)))
    silu = lambda v: v * _sigmoid(v)
    b_ada8 = jnp.broadcast_to(w["b_ada"], (ROWS_PAD, w["b_ada"].shape[1]))
    mod = _mm("ada_fwd", c8, w["w_ada"], "nn", F32, a_fn=silu, epi=lambda acc, b: acc + b, epi_in=(b_ada8,))
    mod = mod[:n_b].reshape(n_b, 6, 1, d)
    shift1, scale1, gate1, shift2, scale2, gate2 = [mod[:, i] for i in range(6)]

    (h,) = rw("norm1_fwd", lambda r, b, cc: ([_norm_mod(r[0], cc[0], b[0], b[1])], [], []),
              rows=[_full(x2)], bats=[scale1, shift1], consts=[w["norm1_g"]], outs=[(d, BF16)])
    zsm = _mm("in_proj_sm", h, w_sm, "nn", F32)
    zglu = _mm("in_proj_glu", h, w_glu, "nn", BF16)
    zgate = _mm("in_proj_gate", h, w_gate, "nn", BF16)
    q, k, v = _mla_prep_fwd(zsm, wuq, wuk, wuv, w["q_latent_g"], w["kv_latent_g"], gqn, gkn, rope, n_b, seq)
    attn, lse = _attn_fwd(q, k, v, n_b, seq)
    attn2 = attn.reshape(n_rows, N_HEADS * V_HEAD_DIM)
    u3 = _conv_fwd(zglu.reshape(n_b, seq, 2 * CONV_CH), conv_w, w["conv_b"], w["conv_ln_g"], w["conv_ln_b"], n_b, seq)
    u32 = u3.reshape(n_rows, CONV_CH)
    ya = _mm("mla_out", attn2, w["w_o_mla"], "nn", BF16)
    yb = _mm("conv_out", u32, w["w_pw_out"], "nn", BF16)
    (mrg,) = rw("merge_fwd",
                lambda r, b, cc: ([_sigmoid(r[0].astype(F32)) * r[2].astype(F32) + _sigmoid(r[1].astype(F32)) * r[3].astype(F32)], [], []),
                rows=[(zgate, d, 0), (zgate, d, 1), _full(ya), _full(yb)], outs=[(d, BF16)])
    mixed = _mm("out_proj", mrg, w["w_out"], "nn", BF16)

    def mid_fn(r, b, cc):
        x1 = r[0] + b[0] * r[1].astype(F32)
        return [x1, _norm_mod(x1, cc[0], b[1], b[2])], [], []

    x1, h2 = rw("norm2_fwd", mid_fn, rows=[_full(x2), _full(mixed)], bats=[gate1, scale2, shift2],
                consts=[w["norm2_g"]], outs=[(d, F32), (d, BF16)])
    relu2 = lambda v: jnp.square(jnp.maximum(v, 0.0))
    a = _mm("ff1", h2, w["w_ff1"], "nn", BF16)
    f = _mm("ff2", a, w["w_ff2"], "nn", BF16, a_fn=relu2)

    def loss_fn(r, b, cc):
        ff = r[1].astype(F32)
        err = r[0] + b[0] * ff - r[2]
        dy = err * (1.0 / d)
        sq = jnp.broadcast_to(jnp.sum(err * err, keepdims=True), (1, LANES))
        return [dy, b[0] * dy], [jnp.sum(dy * ff, axis=0, keepdims=True)], [sq]

    dy, df, dgate2, sq_err = rw("loss", loss_fn, rows=[_full(x1), _full(f), _full(t2)], bats=[gate2],
                                outs=[(d, F32), (d, BF16)], bat_outs=[d], tot_outs=[(1, LANES)])

    da = _mm("ff2_bwd", df, w["w_ff2"], "nt", BF16, epi=lambda acc, av: acc * 2.0 * jnp.maximum(av, 0.0), epi_in=(a,))
    g_ff2 = _mm("ff2_dw", a, df, "tn", F32, a_fn=relu2)
    dh2 = _mm("ff1_bwd", da, w["w_ff1"], "nt", F32)
    g_ff1 = _mm("ff1_dw", h2, da, "tn", F32)

    def mid_bwd(r, b, cc):
        x1_, dh2_, dy_, mixed_ = r[0], r[1], r[2], r[3].astype(F32)
        dx, dsc, dsh, dg = _norm_mod_bwd(x1_, cc[0], b[0], dh2_)
        dx1_ = dy_ + dx
        return [dx1_, b[1] * dx1_], [dsc, dsh, jnp.sum(dx1_ * mixed_, axis=0, keepdims=True)], [dg]

    dx1, dmixed, dscale2, dshift2, dgate1, g_norm2 = rw(
        "norm2_bwd", mid_bwd, rows=[_full(x1), _full(dh2), _full(dy), _full(mixed)], bats=[scale2, gate1],
        consts=[w["norm2_g"]], outs=[(d, F32), (d, BF16)], bat_outs=[d, d, d], tot_outs=[(1, d)])

    dmrg = _mm("out_proj_bwd", dmixed, w["w_out"], "nt", BF16)
    g_out = _mm("out_proj_dw", mrg, dmixed, "tn", F32)

    def merge_bwd(r, b, cc):
        dm, ya_, yb_ = r[0].astype(F32), r[3].astype(F32), r[4].astype(F32)
        sa, sb = _sigmoid(r[1].astype(F32)), _sigmoid(r[2].astype(F32))
        return [dm * ya_ * sa * (1.0 - sa), dm * yb_ * sb * (1.0 - sb), dm * sa, dm * sb], [], []

    dzga, dzgb, dya, dyb = rw("merge_bwd", merge_bwd,
                              rows=[_full(dmrg), (zgate, d, 0), (zgate, d, 1), _full(ya), _full(yb)],
                              outs=[(d, BF16)] * 4)
    dattn = _mm("mla_out_bwd", dya, w["w_o_mla"], "nt", BF16)
    g_o_mla = _mm("mla_out_dw", attn2, dya, "tn", F32)
    du3 = _mm("conv_out_bwd", dyb, w["w_pw_out"], "nt", BF16)
    g_pw = _mm("conv_out_dw", u32, dyb, "tn", F32)

    dzglu, g_conv_w, g_conv_b, g_ln_g, g_ln_b = _conv_bwd(
        zglu.reshape(n_b, seq, 2 * CONV_CH), du3.reshape(n_b, seq, CONV_CH), conv_w, w["conv_b"], w["conv_ln_g"],
        w["conv_ln_b"], n_b, seq)
    dzglu = dzglu.reshape(n_rows, 2 * CONV_CH)

    dattn3 = dattn.reshape(n_b, seq, N_HEADS * V_HEAD_DIM)
    dq, dk, dv = _attn_bwd(q, k, v, dattn3, attn, lse, n_b, seq)
    dzsm, g_wuq, g_wuk, g_wuv, g_gq, g_gkv, g_gqn, g_gkn = _mla_prep_bwd(
        zsm, dq, dk, dv, wuq, wuk, wuv, w["q_latent_g"], w["kv_latent_g"], gqn, gkn, rope, n_b, seq)

    add = lambda acc, prev: acc + prev
    dh = _mm("in_proj_gate_bwd_a", dzga, w_gate[:, :d], "nt", F32)
    dh = _mm("in_proj_gate_bwd_b", dzgb, w_gate[:, d:], "nt", F32, epi=add, epi_in=(dh,))
    dh = _mm("in_proj_glu_bwd", dzglu, w_glu, "nt", F32, epi=add, epi_in=(dh,))
    dh = _mm("in_proj_sm_bwd", dzsm, w_sm, "nt", F32, epi=add, epi_in=(dh,))
    g_gate_a = _mm("in_proj_gate_dw_a", h, dzga, "tn", F32)
    g_gate_b = _mm("in_proj_gate_dw_b", h, dzgb, "tn", F32)
    g_glu = _mm("in_proj_glu_dw", h, dzglu, "tn", F32)
    g_sm = _mm("in_proj_sm_dw", h, dzsm, "tn", F32)

    def first_bwd(r, b, cc):
        dx, dsc, dsh, dg = _norm_mod_bwd(r[0], cc[0], b[0], r[1])
        return [r[2] + dx], [dsc, dsh], [dg]

    grad_x, dscale1, dshift1, g_norm1 = rw("norm1_bwd", first_bwd, rows=[_full(x2), _full(dh), _full(dx1)],
                                            bats=[scale1], consts=[w["norm1_g"]], outs=[(d, F32)], bat_outs=[d, d],
                                            tot_outs=[(1, d)])

    dmod = jnp.concatenate([dshift1, dscale1, dgate1, dshift2, dscale2, dgate2], axis=1).reshape(n_b, 6 * d)
    dmod8 = jnp.pad(dmod, ((0, ROWS_PAD - n_b), (0, 0)))
    g_ada = _mm("ada_dw", c8, dmod8, "tn", F32, a_fn=silu)
    (g_b_ada,) = _rowwise("ada_db", lambda r, b, cc: ([], [], [jnp.sum(r[0], axis=0, keepdims=True)]), ROWS_PAD, ROWS_PAD,
                          rows=[_full(dmod8)], tot_outs=[(1, 6 * d)])

    g_in = jnp.concatenate([g_sm[:, :OFF_KV], g_sm[:, OFF_KV + QK_NOPE_DIM:OFF_KV + QK_NOPE_DIM + QK_ROPE_DIM], g_glu,
                            g_gate_a, g_gate_b], axis=1)
    g_uq = g_wuq[:, :, :QK_HEAD_DIM].transpose(1, 0, 2).reshape(Q_LORA, N_HEADS * QK_HEAD_DIM)
    g_v = jnp.where(odd, g_wuv[:, :, V_HEAD_DIM:], g_wuv[:, :, :V_HEAD_DIM])
    g_ukv = jnp.concatenate([g_wuk[:, :, :QK_NOPE_DIM], g_v], axis=2).transpose(1, 0, 2).reshape(KV_LORA, -1)
    grads = {
        "w_ada": g_ada, "b_ada": g_b_ada, "norm1_g": g_norm1, "w_in": g_in, "q_latent_g": g_gq, "w_uq": g_uq,
        "kv_latent_g": g_gkv, "w_ukv": g_ukv, "qk_norm_q_g": g_gqn[:, :QK_HEAD_DIM], "qk_norm_k_g": g_gkn[:, :QK_HEAD_DIM],
        "w_o_mla": g_o_mla, "conv_w": g_conv_w[:CONV_WIDTH], "conv_b": g_conv_b, "conv_ln_g": g_ln_g,
        "conv_ln_b": g_ln_b, "w_pw_out": g_pw, "w_out": g_out, "norm2_g": g_norm2, "w_ff1": g_ff1, "w_ff2": g_ff2,
    }
    return sq_err, grad_x.reshape(n_b, seq, d), grads


def _exchange(name, buf, gather):
    rows = buf.shape[-2]

    def body(src_ref, dst_ref, send_sems, recv_sems, local_sem):
        x, y, c = lax.axis_index("x"), lax.axis_index("y"), lax.axis_index("c")
        me = 4 * x + 2 * y + c

        def peer(kk):
            px = 1 - x if kk & 4 else x
            py = 1 - y if kk & 2 else y
            pc = 1 - c if kk & 1 else c
            return (px, py, pc), 4 * px + 2 * py + pc

        def copy(kk):
            dev, pid = peer(kk)
            return pltpu.make_async_remote_copy(
                src_ref=src_ref if gather else src_ref.at[pid], dst_ref=dst_ref.at[me],
                send_sem=send_sems.at[kk - 1], recv_sem=recv_sems.at[kk - 1],
                device_id=dev, device_id_type=pl.DeviceIdType.MESH)

        def arrival(kk):
            dev, pid = peer(kk)
            return pltpu.make_async_remote_copy(
                src_ref=src_ref if gather else src_ref.at[me], dst_ref=dst_ref.at[pid],
                send_sem=send_sems.at[kk - 1], recv_sem=recv_sems.at[kk - 1],
                device_id=dev, device_id_type=pl.DeviceIdType.MESH)

        mine = pltpu.make_async_copy(src_ref if gather else src_ref.at[me], dst_ref.at[me], local_sem)
        mine.start()
        sends = [copy(kk) for kk in range(1, N_DEV)]
        for cp in sends:
            cp.start()
        for kk in range(1, N_DEV):
            arrival(kk).wait_recv()
        for cp in sends:
            cp.wait_send()
        mine.wait()

    return pl.pallas_call(
        body, name=name,
        out_shape=jax.ShapeDtypeStruct((N_DEV, rows, LANES), buf.dtype),
        in_specs=[pl.BlockSpec(memory_space=pltpu.HBM)],
        out_specs=pl.BlockSpec(memory_space=pltpu.HBM),
        scratch_shapes=[pltpu.SemaphoreType.DMA((N_DEV - 1,)), pltpu.SemaphoreType.DMA((N_DEV - 1,)),
                        pltpu.SemaphoreType.DMA],
    )(buf)


def _sum_slots(name, slots):
    rows = slots.shape[1]
    tr = rows
    for cand in (1024, 512, 256, 128, 64, 32, 16, 8):
        if rows % cand == 0:
            tr = cand
            break

    def body(s_ref, o_ref):
        acc = s_ref[0].astype(F32)
        for j in range(1, N_DEV):
            acc = acc + s_ref[j].astype(F32)
        o_ref[...] = acc

    return pl.pallas_call(
        body, name=name, grid=(rows // tr,),
        in_specs=[pl.BlockSpec((N_DEV, tr, LANES), lambda i: (0, i, 0))],
        out_specs=pl.BlockSpec((tr, LANES), lambda i: (i, 0)),
        out_shape=jax.ShapeDtypeStruct((rows, LANES), F32),
        compiler_params=_params(("parallel",)),
    )(slots)


def _adamw(name, w, g, m, v):
    def body(w_ref, g_ref, m_ref, v_ref, d_ref, nm_ref, nv_ref):
        gg = g_ref[...]
        nm = ADAM_B1 * m_ref[...] + (1.0 - ADAM_B1) * gg
        nv = ADAM_B2 * v_ref[...] + (1.0 - ADAM_B2) * jnp.square(gg)
        m_hat = nm / (1.0 - ADAM_B1 ** ADAM_STEP)
        v_hat = nv / (1.0 - ADAM_B2 ** ADAM_STEP)
        d_ref[...] = -ADAM_LR * (m_hat / (jnp.sqrt(v_hat) + ADAM_EPS) + ADAM_WD * w_ref[...])
        nm_ref[...] = nm
        nv_ref[...] = nv

    shape = jax.ShapeDtypeStruct(w.shape, F32)
    return pl.pallas_call(body, name=name, out_shape=[shape] * 3, compiler_params=_params(None))(w, g, m, v)


def _pad_flat(v, align):
    flat = v.reshape(-1)
    return jnp.pad(flat, (0, (-flat.shape[0]) % align))


def _padded_size(n, align):
    return n + (-n) % align


def kernel(x, c, w_ada, b_ada, norm1_g, w_in, q_latent_g, w_uq, kv_latent_g, w_ukv, qk_norm_q_g, qk_norm_k_g, w_o_mla, conv_w, conv_b, conv_ln_g, conv_ln_b, w_pw_out, w_out, norm2_g, w_ff1, w_ff2, loss_target, m_w_ada, m_b_ada, m_norm1_g, m_w_in, m_q_latent_g, m_w_uq, m_kv_latent_g, m_w_ukv, m_qk_norm_q_g, m_qk_norm_k_g, m_w_o_mla, m_conv_w, m_conv_b, m_conv_ln_g, m_conv_ln_b, m_w_pw_out, m_w_out, m_norm2_g, m_w_ff1, m_w_ff2, v_w_ada, v_b_ada, v_norm1_g, v_w_in, v_q_latent_g, v_w_uq, v_kv_latent_g, v_w_ukv, v_qk_norm_q_g, v_qk_norm_k_g, v_w_o_mla, v_conv_w, v_conv_b, v_conv_ln_g, v_conv_ln_b, v_w_pw_out, v_w_out, v_norm2_g, v_w_ff1, v_w_ff2):
    given = dict(locals())
    local = {n: given[n][0] for n in WEIGHTS}
    shard_shape = {n: local[n].shape for n in SHARDED}

    sizes = {n: _padded_size(local[n].size, PACK_ALIGN) for n in SHARDED}
    total = _padded_size(sum(sizes.values()), 512 * LANES)
    packed = jnp.concatenate([_pad_flat(local[n].astype(BF16), PACK_ALIGN) for n in SHARDED])
    packed = jnp.pad(packed, (0, total - packed.shape[0])).reshape(total // LANES, LANES)
    gathered = _exchange("gather_weights", packed, gather=True).reshape(N_DEV, total)
    whole = {}
    off = 0
    for n in SHARDED:
        rows_, cols_ = shard_shape[n]
        blk = gathered[:, off:off + rows_ * cols_].reshape(N_DEV, rows_, cols_)
        off += sizes[n]
        if n in ROW_SHARDED:
            whole[n] = blk.reshape(N_DEV * rows_, cols_)
        else:
            whole[n] = blk.transpose(1, 0, 2).reshape(rows_, N_DEV * cols_)
    for n in REPLICATED:
        whole[n] = local[n].reshape(1, -1)

    sq_err, grad_x, grads = _local_step(x, c, whole, loss_target)
    loss = lax.psum(sq_err[0, 0] * (0.5 / x.shape[-1]), MESH_AXES)

    parts = []
    for n in SHARDED:
        rows_, cols_ = shard_shape[n]
        g = grads[n]
        if n in ROW_SHARDED:
            blk = g.reshape(N_DEV, rows_ * cols_)
        else:
            blk = g.reshape(rows_, N_DEV, cols_).transpose(1, 0, 2).reshape(N_DEV, rows_ * cols_)
        parts.append(jnp.pad(blk.astype(BF16), ((0, 0), (0, sizes[n] - rows_ * cols_))))
    parts.append(jnp.zeros((N_DEV, total - sum(sizes.values())), BF16))
    g_packed = jnp.concatenate(parts, axis=1).reshape(N_DEV, total // LANES, LANES)
    g_slots = _exchange("scatter_grads", g_packed, gather=False)
    g_mine = _sum_slots("sum_grads", g_slots).reshape(total)

    small_sizes = {n: _padded_size(local[n].size, LANES) for n in REPLICATED}
    small_total = _padded_size(sum(small_sizes.values()), 8 * LANES)
    small = jnp.concatenate([_pad_flat(grads[n], LANES) for n in REPLICATED])
    small = jnp.pad(small, (0, small_total - small.shape[0])).reshape(small_total // LANES, LANES)
    small_slots = _exchange("gather_small_grads", small, gather=True)
    small_sum = _sum_slots("sum_small_grads", small_slots).reshape(small_total)

    final = {}
    off = 0
    for n in SHARDED:
        rows_, cols_ = shard_shape[n]
        final[n] = g_mine[off:off + rows_ * cols_].reshape(rows_, cols_)
        off += sizes[n]
    off = 0
    for n in REPLICATED:
        final[n] = small_sum[off:off + local[n].size].reshape(1, -1)
        off += small_sizes[n]

    grad_out, delta_out, m_out, v_out = [], [], [], []
    for n in WEIGHTS:
        shape2 = final[n].shape
        d_w, n_m, n_v = _adamw("adamw_" + n, local[n].reshape(shape2), final[n], given["m_" + n].reshape(shape2),
                               given["v_" + n].reshape(shape2))
        full_shape = given[n].shape
        grad_out.append(final[n].reshape(full_shape))
        delta_out.append(d_w.reshape(full_shape))
        m_out.append(n_m.reshape(full_shape))
        v_out.append(n_v.reshape(full_shape))
    return (loss, grad_x, *grad_out, *delta_out, *m_out, *v_out)
```

```python
import functools

import jax
import jax.numpy as jnp
from jax import lax
from jax.experimental import pallas as pl
from jax.experimental.pallas import tpu as pltpu

F32 = jnp.float32
BF16 = jnp.bfloat16

N_DEV = 8
MESH_AXES = ("x", "y", "c")
EPS = 1e-6
N_HEADS = 8
QK_HEAD_DIM = 96
QK_NOPE_DIM = 64
QK_ROPE_DIM = 32
V_HEAD_DIM = 64
HEAD_PAD = 128
Q_LORA = 256
KV_LORA = 128
CONV_CH = 512
CONV_WIDTH = 31
CONV_HALO = 32
CHUNK = 64
ROPE_THETA = 10000.0
OFF_Q = Q_LORA
OFF_KV = OFF_Q + KV_LORA
OFF_KR = OFF_KV + QK_ROPE_DIM
OFF_GLU = OFF_KR + 2 * CONV_CH
ADA_CHUNKS = 6
ADAM_LR = 0.001
ADAM_B1 = 0.9
ADAM_B2 = 0.999
ADAM_EPS = 1e-08
ADAM_WD = 0.01
ADAM_STEP = 10
LANES = 128
VMEM_LIMIT = 56 * 1024 * 1024
NEG_BIG = -1e30
ATT_HEADS = 4
ATT_SCALE = QK_HEAD_DIM ** -0.5
LOG2E = 1.4426950408889634
LN2 = 0.6931471805599453
QK_SCALE = ATT_SCALE * LOG2E
ROWS_PAD = 16

REPLICATED = ("b_ada", "norm1_g", "q_latent_g", "kv_latent_g", "qk_norm_q_g", "qk_norm_k_g", "conv_b", "conv_ln_g",
              "conv_ln_b", "norm2_g")
WEIGHTS = ("w_ada", "b_ada", "norm1_g", "w_in", "q_latent_g", "w_uq", "kv_latent_g", "w_ukv", "qk_norm_q_g",
           "qk_norm_k_g", "w_o_mla", "conv_w", "conv_b", "conv_ln_g", "conv_ln_b", "w_pw_out", "w_out", "norm2_g",
           "w_ff1", "w_ff2")


def _tile(dim, pref):
    if dim <= pref:
        return dim
    t = (pref // LANES) * LANES
    while dim % t:
        t -= LANES
    return t


def _params(semantics):
    return pltpu.CompilerParams(dimension_semantics=semantics, vmem_limit_bytes=VMEM_LIMIT)


def _sigmoid(v):
    return 1.0 / (1.0 + jnp.exp(-v))


def _silu(v):
    return v * _sigmoid(v)


def _relu2(v):
    return jnp.square(jnp.maximum(v, 0.0))


_DIMS = {"nn": (((1,), (0,)), ((), ())), "nt": (((1,), (1,)), ((), ())), "tn": (((0,), (0,)), ((), ()))}


def _mm(name, a, b, mode, out_dtype, *, a_fn=None, epi=None, epi_in=(), b_stacked=False, out_stacked=False,
        tm=512, tn=1024, tk=1024):
    if b_stacked:
        shard = b.shape[2]
        b_rows, b_cols = b.shape[1], N_DEV * shard
    else:
        b_rows, b_cols = b.shape
    if mode == "nn":
        (m, k), n = a.shape, b_cols
    elif mode == "nt":
        (m, k), n = a.shape, b_rows
    else:
        (k, m), n = a.shape, b_cols
    if out_stacked:
        shard = n // N_DEV
    tm = _tile(m, tm)
    tn = _tile(shard, tn) if (out_stacked or (b_stacked and mode != "nt")) else _tile(n, tn)
    tk = _tile(shard, tk) if (b_stacked and mode == "nt") else _tile(k, tk)
    nk = k // tk
    a_spec = (pl.BlockSpec((tk, tm), lambda i, j, kk: (kk, i)) if mode == "tn"
              else pl.BlockSpec((tm, tk), lambda i, j, kk: (i, kk)))
    if b_stacked and mode == "nt":
        per = shard // tk
        b_spec = pl.BlockSpec((None, tn, tk), lambda i, j, kk: (kk // per, j, kk % per))
    elif b_stacked:
        per = shard // tn
        b_spec = pl.BlockSpec((None, tk, tn), lambda i, j, kk: (j // per, kk, j % per))
    elif mode == "nt":
        b_spec = pl.BlockSpec((tn, tk), lambda i, j, kk: (j, kk))
    else:
        b_spec = pl.BlockSpec((tk, tn), lambda i, j, kk: (kk, j))
    e_spec = pl.BlockSpec((tm, tn), lambda i, j, kk: (i, j))
    if out_stacked:
        per_o = shard // tn
        o_spec = pl.BlockSpec((None, tm, tn), lambda i, j, kk: (j // per_o, i, j % per_o))
        out_shape = jax.ShapeDtypeStruct((N_DEV, m, shard), out_dtype)
    else:
        o_spec = e_spec
        out_shape = jax.ShapeDtypeStruct((m, n), out_dtype)
    n_epi = len(epi_in)

    def body(a_ref, b_ref, *rest):
        epi_refs, o_ref, acc_ref = rest[:n_epi], rest[n_epi], rest[n_epi + 1]
        kk = pl.program_id(2)

        @pl.when(kk == 0)
        def _():
            acc_ref[...] = jnp.zeros_like(acc_ref)

        av = a_ref[...]
        if a_fn is not None:
            av = a_fn(av.astype(F32))
        acc_ref[...] += lax.dot_general(av.astype(BF16), b_ref[...].astype(BF16), _DIMS[mode],
                                        preferred_element_type=F32)

        @pl.when(kk == nk - 1)
        def _():
            acc = acc_ref[...]
            if epi is not None:
                acc = epi(acc, *[r[...].astype(F32) for r in epi_refs])
            o_ref[...] = acc.astype(out_dtype)

    return pl.pallas_call(
        body, name=name, grid=(m // tm, n // tn, nk),
        in_specs=[a_spec, b_spec] + [e_spec] * n_epi, out_specs=o_spec, out_shape=out_shape,
        scratch_shapes=[pltpu.VMEM((tm, tn), F32)],
        compiler_params=_params(("parallel", "parallel", "arbitrary")),
    )(a, b, *epi_in)


def _rowwise(name, fn, n_rows, seq, rows, bats=(), consts=(), outs=(), bat_outs=(), tot_outs=(), tm=256):
    tm = min(tm, seq)
    per_seq = seq // tm
    n_b = n_rows // seq
    nr, nb, nc, no, nbo, nto = len(rows), len(bats), len(consts), len(outs), len(bat_outs), len(tot_outs)

    def body(*refs):
        i = pl.program_id(0)
        r_in = [r[...] for r in refs[:nr]]
        b_in = [r[0] for r in refs[nr:nr + nb]]
        c_in = [r[...] for r in refs[nr + nb:nr + nb + nc]]
        o_refs = refs[nr + nb + nc:nr + nb + nc + no]
        bo_refs = refs[nr + nb + nc + no:nr + nb + nc + no + nbo]
        to_refs = refs[nr + nb + nc + no + nbo:]
        o_val, bo_val, to_val = fn(r_in, b_in, c_in)
        for r, v in zip(o_refs, o_val):
            r[...] = v.astype(r.dtype)
        if nbo:
            @pl.when(i % per_seq == 0)
            def _():
                for r in bo_refs:
                    r[...] = jnp.zeros_like(r)

            for r, v in zip(bo_refs, bo_val):
                r[0] += v
        if nto:
            @pl.when(i == 0)
            def _():
                for r in to_refs:
                    r[...] = jnp.zeros_like(r)

            for r, v in zip(to_refs, to_val):
                r[...] += v

    in_specs = [pl.BlockSpec((tm, w), functools.partial(lambda cb, i: (i, cb), cb)) for (_, w, cb) in rows]
    in_specs += [pl.BlockSpec((1, 1, bt.shape[2]), lambda i: (i // per_seq, 0, 0)) for bt in bats]
    in_specs += [pl.BlockSpec(ct.shape, lambda i: (0, 0)) for ct in consts]
    out_specs = [pl.BlockSpec((tm, w), lambda i: (i, 0)) for (w, _) in outs]
    out_specs += [pl.BlockSpec((1, 1, w), lambda i: (i // per_seq, 0, 0)) for w in bat_outs]
    out_specs += [pl.BlockSpec(shp, lambda i: (0, 0)) for shp in tot_outs]
    out_shape = [jax.ShapeDtypeStruct((n_rows, w), dt) for (w, dt) in outs]
    out_shape += [jax.ShapeDtypeStruct((n_b, 1, w), F32) for w in bat_outs]
    out_shape += [jax.ShapeDtypeStruct(shp, F32) for shp in tot_outs]
    res = pl.pallas_call(
        body, name=name, grid=(n_rows // tm,), in_specs=in_specs, out_specs=out_specs, out_shape=out_shape,
        compiler_params=_params(("arbitrary",)),
    )(*[r[0] for r in rows], *bats, *consts)
    return res


def _full(arr):
    return (arr, arr.shape[1], 0)


def _norm_mod(x, g, scale, shift):
    r = lax.rsqrt(jnp.mean(x * x, axis=-1, keepdims=True) + EPS)
    xh = x * r
    return xh * g * (1.0 + scale) + shift


def _norm_mod_bwd(x, g, scale, dh):
    r = lax.rsqrt(jnp.mean(x * x, axis=-1, keepdims=True) + EPS)
    xh = x * r
    dn = dh * (1.0 + scale)
    dxh = dn * g
    dx = r * (dxh - xh * jnp.mean(dxh * xh, axis=-1, keepdims=True))
    dscale = jnp.sum(dh * xh * g, axis=0, keepdims=True)
    dshift = jnp.sum(dh, axis=0, keepdims=True)
    dg = jnp.sum(dn * xh, axis=0, keepdims=True)
    return dx, dscale, dshift, dg


def _rms(v, g):
    r = lax.rsqrt(jnp.mean(v * v, axis=-1, keepdims=True) + EPS)
    return v * r * g


def _rms_bwd(v, g, dy):
    r = lax.rsqrt(jnp.mean(v * v, axis=-1, keepdims=True) + EPS)
    vh = v * r
    dvh = dy * g
    dv = r * (dvh - vh * jnp.mean(dvh * vh, axis=-1, keepdims=True))
    return dv, jnp.sum(dy * vh, axis=0, keepdims=True)


def _head_norm(v, g):
    r = lax.rsqrt(jnp.sum(v * v, axis=-1, keepdims=True) * (1.0 / QK_HEAD_DIM) + EPS)
    return v * r * g


def _head_norm_bwd(v, g, dy):
    r = lax.rsqrt(jnp.sum(v * v, axis=-1, keepdims=True) * (1.0 / QK_HEAD_DIM) + EPS)
    vh = v * r
    dvh = dy * g
    dv = r * (dvh - vh * (jnp.sum(dvh * vh, axis=-1, keepdims=True) * (1.0 / QK_HEAD_DIM)))
    return dv, jnp.sum(dy * vh, axis=0, keepdims=True)


def _rope(v, cos, sin_lo, sin_hi):
    return v * cos + pltpu.roll(v, HEAD_PAD - 16, 1) * sin_lo + pltpu.roll(v, 16, 1) * sin_hi


def _rope_bwd(g, cos, sin_lo, sin_hi):
    return g * cos + pltpu.roll(g * sin_lo, 16, 1) + pltpu.roll(g * sin_hi, HEAD_PAD - 16, 1)


def _mla_prep_fwd(zsm, wuq, wuk, wuv, gq, gkv, gqn, gkn, rope, n_b, seq):
    n_rows = n_b * seq
    tm = min(256, seq)
    per_seq = seq // tm

    def body(z_ref, wuq_ref, wuk_ref, wuv_ref, gq_ref, gkv_ref, gqn_ref, gkn_ref, c_ref, s1_ref, s2_ref,
             q_ref, k_ref, v_ref):
        z = z_ref[...]
        qn = _rms(z[:, :Q_LORA], gq_ref[...]).astype(BF16)
        kvn = _rms(z[:, Q_LORA:Q_LORA + KV_LORA], gkv_ref[...]).astype(BF16)
        krp = z[:, Q_LORA + KV_LORA:]
        cos, s1, s2 = c_ref[...], s1_ref[...], s2_ref[...]
        for h in range(N_HEADS):
            qh = jnp.dot(qn, wuq_ref[h], preferred_element_type=F32)
            q_ref[0, h] = (_rope(_head_norm(qh, gqn_ref[...]), cos, s1, s2) * QK_SCALE).astype(BF16)
            kh = jnp.dot(kvn, wuk_ref[h], preferred_element_type=F32) + krp
            k_ref[0, h] = _rope(_head_norm(kh, gkn_ref[...]), cos, s1, s2).astype(BF16)
            v_ref[0, h] = jnp.dot(kvn, wuv_ref[h], preferred_element_type=F32).astype(BF16)

    whole3 = lambda arr: pl.BlockSpec(arr.shape, lambda i: (0, 0, 0))
    whole2 = lambda arr: pl.BlockSpec(arr.shape, lambda i: (0, 0))
    rope_spec = pl.BlockSpec((tm, HEAD_PAD), lambda i: (i % per_seq, 0))
    head_spec = pl.BlockSpec((1, N_HEADS, tm, HEAD_PAD), lambda i: (i // per_seq, 0, i % per_seq, 0))
    head_shape = jax.ShapeDtypeStruct((n_b, N_HEADS, seq, HEAD_PAD), BF16)
    return pl.pallas_call(
        body, name="mla_prep_fwd", grid=(n_rows // tm,),
        in_specs=[pl.BlockSpec((tm, 512), lambda i: (i, 0)), whole3(wuq), whole3(wuk), whole3(wuv),
                  whole2(gq), whole2(gkv), whole2(gqn), whole2(gkn), rope_spec, rope_spec, rope_spec],
        out_specs=[head_spec] * 3, out_shape=[head_shape] * 3,
        compiler_params=_params(("parallel",)),
    )(zsm, wuq, wuk, wuv, gq, gkv, gqn, gkn, *rope)


def _mla_prep_bwd(zsm, dq, dk, dv, wuq, wuk, wuv, gq, gkv, gqn, gkn, rope, n_b, seq):
    n_rows = n_b * seq
    tm = min(256, seq)
    per_seq = seq // tm
    tn_dims = _DIMS["tn"]
    nt_dims = _DIMS["nt"]

    def body(z_ref, dq_ref, dk_ref, dv_ref, wuq_ref, wuk_ref, wuv_ref, gq_ref, gkv_ref, gqn_ref, gkn_ref,
             c_ref, s1_ref, s2_ref, dz_ref, dwuq_ref, dwuk_ref, dwuv_ref, dgq_ref, dgkv_ref, dgqn_ref, dgkn_ref):
        @pl.when(pl.program_id(0) == 0)
        def _():
            for r in (dwuq_ref, dwuk_ref, dwuv_ref, dgq_ref, dgkv_ref, dgqn_ref, dgkn_ref):
                r[...] = jnp.zeros_like(r)

        z = z_ref[...]
        zq, zkv, krp = z[:, :Q_LORA], z[:, Q_LORA:Q_LORA + KV_LORA], z[:, Q_LORA + KV_LORA:]
        qn = _rms(zq, gq_ref[...]).astype(BF16)
        kvn = _rms(zkv, gkv_ref[...]).astype(BF16)
        cos, s1, s2 = c_ref[...], s1_ref[...], s2_ref[...]
        lane = lax.broadcasted_iota(jnp.int32, (tm, HEAD_PAD), 1)
        rope_lanes = (lane >= QK_NOPE_DIM) & (lane < QK_HEAD_DIM)
        dqn = jnp.zeros((tm, Q_LORA), F32)
        dkvn = jnp.zeros((tm, KV_LORA), F32)
        dkrp = jnp.zeros((tm, HEAD_PAD), F32)
        dgqn = jnp.zeros((1, HEAD_PAD), F32)
        dgkn = jnp.zeros((1, HEAD_PAD), F32)
        for h in range(N_HEADS):
            qh = jnp.dot(qn, wuq_ref[h], preferred_element_type=F32)
            dqh, dg = _head_norm_bwd(qh, gqn_ref[...], _rope_bwd(dq_ref[0, h].astype(F32) * ATT_SCALE, cos, s1, s2))
            dgqn += dg
            dqh = dqh.astype(BF16)
            dwuq_ref[h] += lax.dot_general(qn, dqh, tn_dims, preferred_element_type=F32)
            dqn += lax.dot_general(dqh, wuq_ref[h], nt_dims, preferred_element_type=F32)

            kh = jnp.dot(kvn, wuk_ref[h], preferred_element_type=F32) + krp
            dkh, dg = _head_norm_bwd(kh, gkn_ref[...], _rope_bwd(dk_ref[0, h].astype(F32), cos, s1, s2))
            dgkn += dg
            dkrp += jnp.where(rope_lanes, dkh, 0.0)
            dkh = dkh.astype(BF16)
            dwuk_ref[h] += lax.dot_general(kvn, dkh, tn_dims, preferred_element_type=F32)
            dkvn += lax.dot_general(dkh, wuk_ref[h], nt_dims, preferred_element_type=F32)

            dvh = dv_ref[0, h]
            dwuv_ref[h] += lax.dot_general(kvn, dvh, tn_dims, preferred_element_type=F32)
            dkvn += lax.dot_general(dvh, wuv_ref[h], nt_dims, preferred_element_type=F32)
        dzq, dg = _rms_bwd(zq, gq_ref[...], dqn)
        dgq_ref[...] += dg
        dzkv, dg = _rms_bwd(zkv, gkv_ref[...], dkvn)
        dgkv_ref[...] += dg
        dgqn_ref[...] += dgqn
        dgkn_ref[...] += dgkn
        dz_ref[:, :Q_LORA] = dzq.astype(dz_ref.dtype)
        dz_ref[:, Q_LORA:Q_LORA + KV_LORA] = dzkv.astype(dz_ref.dtype)
        dz_ref[:, Q_LORA + KV_LORA:] = dkrp.astype(dz_ref.dtype)

    whole3 = lambda arr: pl.BlockSpec(arr.shape, lambda i: (0, 0, 0))
    whole2 = lambda arr: pl.BlockSpec(arr.shape, lambda i: (0, 0))
    rope_spec = pl.BlockSpec((tm, HEAD_PAD), lambda i: (i % per_seq, 0))
    head_spec = pl.BlockSpec((1, N_HEADS, tm, HEAD_PAD), lambda i: (i // per_seq, 0, i % per_seq, 0))
    row_spec = pl.BlockSpec((tm, 512), lambda i: (i, 0))
    return pl.pallas_call(
        body, name="mla_prep_bwd", grid=(n_rows // tm,),
        in_specs=[row_spec, head_spec, head_spec, head_spec, whole3(wuq), whole3(wuk), whole3(wuv),
                  whole2(gq), whole2(gkv), whole2(gqn), whole2(gkn), rope_spec, rope_spec, rope_spec],
        out_specs=[row_spec, whole3(wuq), whole3(wuk), whole3(wuv), whole2(gq), whole2(gkv), whole2(gqn), whole2(gkn)],
        out_shape=[jax.ShapeDtypeStruct((n_rows, 512), BF16),
                   jax.ShapeDtypeStruct(wuq.shape, F32), jax.ShapeDtypeStruct(wuk.shape, F32),
                   jax.ShapeDtypeStruct(wuv.shape, F32), jax.ShapeDtypeStruct(gq.shape, F32),
                   jax.ShapeDtypeStruct(gkv.shape, F32), jax.ShapeDtypeStruct(gqn.shape, F32),
                   jax.ShapeDtypeStruct(gkn.shape, F32)],
        compiler_params=_params(("arbitrary",)),
    )(zsm, dq, dk, dv, wuq, wuk, wuv, gq, gkv, gqn, gkn, *rope)


HBM_SPEC = pl.BlockSpec(memory_space=pltpu.HBM)


def _xchg_out_shapes(bufs):
    return [jax.ShapeDtypeStruct((N_DEV,) + (a.shape if gather else a.shape[1:]), a.dtype) for a, gather in bufs]


def _xchg_scratch(n_buf):
    return [pltpu.SemaphoreType.DMA((n_buf * (N_DEV - 1),)), pltpu.SemaphoreType.DMA((n_buf * (N_DEV - 1),)),
            pltpu.SemaphoreType.DMA((n_buf,))]


def _xchg_copies(src_refs, dst_refs, gathers, send_sems, recv_sems, local_sems):
    x, y, c = lax.axis_index("x"), lax.axis_index("y"), lax.axis_index("c")
    me = 4 * x + 2 * y + c
    local, starts, arrivals = [], [], []
    for bi, (src, dst, gather) in enumerate(zip(src_refs, dst_refs, gathers)):
        local.append(pltpu.make_async_copy(src if gather else src.at[me], dst.at[me], local_sems.at[bi]))
        for kk in range(1, N_DEV):
            px = 1 - x if kk & 4 else x
            py = 1 - y if kk & 2 else y
            pc = 1 - c if kk & 1 else c
            pid = 4 * px + 2 * py + pc
            sem = bi * (N_DEV - 1) + kk - 1
            starts.append(pltpu.make_async_remote_copy(
                src_ref=src if gather else src.at[pid], dst_ref=dst.at[me],
                send_sem=send_sems.at[sem], recv_sem=recv_sems.at[sem],
                device_id=(px, py, pc), device_id_type=pl.DeviceIdType.MESH))
            arrivals.append(pltpu.make_async_remote_copy(
                src_ref=src if gather else src.at[me], dst_ref=dst.at[pid],
                send_sem=send_sems.at[sem], recv_sem=recv_sems.at[sem],
                device_id=(px, py, pc), device_id_type=pl.DeviceIdType.MESH))
    return local, starts, arrivals


def _xchg_start(copies):
    local, sends, _ = copies
    for cp in local + sends:
        cp.start()


def _xchg_finish(copies):
    local, sends, arrivals = copies
    for cp in arrivals:
        cp.wait_recv()
    for cp in sends:
        cp.wait_send()
    for cp in local:
        cp.wait()


def _exchange(name, bufs):
    n_buf = len(bufs)
    gathers = [g for _, g in bufs]

    def body(*refs):
        srcs, dsts = refs[:n_buf], refs[n_buf:2 * n_buf]
        copies = _xchg_copies(srcs, dsts, gathers, *refs[2 * n_buf:])
        _xchg_start(copies)
        _xchg_finish(copies)

    return pl.pallas_call(
        body, name=name, out_shape=_xchg_out_shapes(bufs),
        in_specs=[HBM_SPEC] * n_buf, out_specs=[HBM_SPEC] * n_buf, scratch_shapes=_xchg_scratch(n_buf),
    )(*[a for a, _ in bufs])


def _chunk_mask(t):
    r = lax.broadcasted_iota(jnp.int32, (t, t), 0) // CHUNK
    c = lax.broadcasted_iota(jnp.int32, (t, t), 1) // CHUNK
    return r >= c


def _grid_ends(grid):
    ids = [pl.program_id(ax) for ax in range(len(grid))]
    first = functools.reduce(jnp.logical_and, [i == 0 for i in ids])
    last = functools.reduce(jnp.logical_and, [i == g - 1 for i, g in zip(ids, grid)])
    return first, last


def _attn_fwd(q, k, v, bufs, n_b, seq):
    tq = min(256, seq)
    nq = seq // tq
    nt_dims = _DIMS["nt"]
    hpb = ATT_HEADS
    grid = (n_b, N_HEADS // hpb, nq)
    n_buf = len(bufs)
    gathers = [g for _, g in bufs]

    def body(q_ref, k_ref, v_ref, *rest):
        srcs, (o_ref, lse_ref), dsts = rest[:n_buf], rest[n_buf:n_buf + 2], rest[n_buf + 2:2 * n_buf + 2]
        copies = _xchg_copies(srcs, dsts, gathers, *rest[2 * n_buf + 2:])
        first, last = _grid_ends(grid)
        pl.when(first)(functools.partial(_xchg_start, copies))

        qi = pl.program_id(2)
        mask = _chunk_mask(tq)
        qs = [q_ref[0, hh] for hh in range(hpb)]

        def step(j, carry, masked):
            rows = pl.ds(pl.multiple_of(j * tq, tq), tq)
            out = []
            for hh in range(hpb):
                m, l, acc = carry[hh]
                s = lax.dot_general(qs[hh], k_ref[0, hh, rows, :], nt_dims, preferred_element_type=F32)
                if masked:
                    s = jnp.where(mask, s, NEG_BIG)
                m_new = jnp.maximum(m, jnp.max(s, axis=-1, keepdims=True))
                alpha = jnp.exp2(m - m_new)
                p = jnp.exp2(s - m_new)
                l = alpha * l + jnp.sum(p, axis=-1, keepdims=True)
                acc = alpha * acc + jnp.dot(p.astype(BF16), v_ref[0, hh, rows, :], preferred_element_type=F32)
                out.append((m_new, l, acc))
            return tuple(out)

        init = tuple((jnp.full((tq, 1), NEG_BIG, F32), jnp.zeros((tq, 1), F32), jnp.zeros((tq, HEAD_PAD), F32))
                     for _ in range(hpb))
        carry = lax.fori_loop(0, qi, functools.partial(step, masked=False), init)
        carry = step(qi, carry, True)
        for pair in range(hpb // 2):
            (m0, l0, a0), (m1, l1, a1) = carry[2 * pair], carry[2 * pair + 1]
            o_ref[0, :, pair * HEAD_PAD:(pair + 1) * HEAD_PAD] = (a0 * (1.0 / l0) + a1 * (1.0 / l1)).astype(BF16)
            lse_ref[0, 2 * pair] = jnp.broadcast_to(m0 + jnp.log2(l0), (tq, HEAD_PAD))
            lse_ref[0, 2 * pair + 1] = jnp.broadcast_to(m1 + jnp.log2(l1), (tq, HEAD_PAD))

        pl.when(last)(functools.partial(_xchg_finish, copies))

    kv_spec = pl.BlockSpec((1, hpb, seq, HEAD_PAD), lambda b, hb, i: (b, hb, 0, 0))
    q_spec = pl.BlockSpec((1, hpb, tq, HEAD_PAD), lambda b, hb, i: (b, hb, i, 0))
    res = pl.pallas_call(
        body, name="attn_fwd", grid=grid,
        in_specs=[q_spec, kv_spec, kv_spec] + [HBM_SPEC] * n_buf,
        out_specs=[pl.BlockSpec((1, tq, hpb * V_HEAD_DIM), lambda b, hb, i: (b, i, hb)), q_spec] + [HBM_SPEC] * n_buf,
        out_shape=[jax.ShapeDtypeStruct((n_b, seq, N_HEADS * V_HEAD_DIM), BF16),
                   jax.ShapeDtypeStruct((n_b, N_HEADS, seq, HEAD_PAD), F32)] + _xchg_out_shapes(bufs),
        scratch_shapes=_xchg_scratch(n_buf),
        compiler_params=_params(("arbitrary", "arbitrary", "arbitrary")),
    )(q, k, v, *[a for a, _ in bufs])
    return res[0], res[1], res[2:]


def _attn_bwd(q, k, v, do, o, lse, bufs, n_b, seq):
    tq = min(256, seq)
    nq = seq // tq
    nt_dims = _DIMS["nt"]
    tn_dims = _DIMS["tn"]
    hpb = ATT_HEADS
    grid = (n_b, N_HEADS // hpb, nq)
    n_buf = len(bufs)
    gathers = [g for _, g in bufs]

    def body(q_ref, k_ref, v_ref, do_ref, o_ref, lse_ref, *rest):
        srcs, (dq_ref, dk_ref, dv_ref), dsts = rest[:n_buf], rest[n_buf:n_buf + 3], rest[n_buf + 3:2 * n_buf + 3]
        dk_acc, dv_acc = rest[2 * n_buf + 3:2 * n_buf + 5]
        copies = _xchg_copies(srcs, dsts, gathers, *rest[2 * n_buf + 5:])
        first, last = _grid_ends(grid)
        pl.when(first)(functools.partial(_xchg_start, copies))

        qi = pl.program_id(2)

        @pl.when(qi == 0)
        def _():
            dk_acc[...] = jnp.zeros_like(dk_acc)
            dv_acc[...] = jnp.zeros_like(dv_acc)

        mask = _chunk_mask(tq)
        lane = lax.broadcasted_iota(jnp.int32, (tq, HEAD_PAD), 1)
        qs, dos, deltas, lses = [], [], [], []
        for hh in range(hpb):
            cols = slice((hh // 2) * HEAD_PAD, (hh // 2 + 1) * HEAD_PAD)
            do_pair = do_ref[0, :, cols]
            prod = do_pair.astype(F32) * o_ref[0, :, cols].astype(F32)
            qs.append(q_ref[0, hh])
            dos.append(do_pair)
            deltas.append(jnp.sum(jnp.where(lane // V_HEAD_DIM == hh % 2, prod, 0.0), axis=-1, keepdims=True))
            lses.append(lse_ref[0, hh][:, :1])

        def step(j, dqs, masked):
            rows = pl.ds(pl.multiple_of(j * tq, tq), tq)
            out = []
            for hh in range(hpb):
                kj = k_ref[0, hh, rows, :]
                s = lax.dot_general(qs[hh], kj, nt_dims, preferred_element_type=F32)
                p = jnp.exp2(s - lses[hh])
                if masked:
                    p = jnp.where(mask, p, 0.0)
                dv_acc[hh, rows, :] += lax.dot_general(p.astype(BF16), dos[hh], tn_dims, preferred_element_type=F32)
                dp = lax.dot_general(dos[hh], v_ref[0, hh, rows, :], nt_dims, preferred_element_type=F32)
                ds = (p * (dp - deltas[hh])).astype(BF16)
                dk_acc[hh, rows, :] += lax.dot_general(ds, qs[hh], tn_dims, preferred_element_type=F32)
                out.append(dqs[hh] + jnp.dot(ds, kj, preferred_element_type=F32))
            return tuple(out)

        dqs = tuple(jnp.zeros((tq, HEAD_PAD), F32) for _ in range(hpb))
        dqs = lax.fori_loop(0, qi, functools.partial(step, masked=False), dqs)
        dqs = step(qi, dqs, True)
        for hh in range(hpb):
            dq_ref[0, hh] = dqs[hh].astype(BF16)

        @pl.when(qi == nq - 1)
        def _():
            dk_ref[0] = (dk_acc[...] * LN2).astype(BF16)
            dv_ref[0] = dv_acc[...].astype(BF16)

        pl.when(last)(functools.partial(_xchg_finish, copies))

    full_spec = pl.BlockSpec((1, hpb, seq, HEAD_PAD), lambda b, hb, i: (b, hb, 0, 0))
    q_spec = pl.BlockSpec((1, hpb, tq, HEAD_PAD), lambda b, hb, i: (b, hb, i, 0))
    o_spec = pl.BlockSpec((1, tq, hpb * V_HEAD_DIM), lambda b, hb, i: (b, i, hb))
    head_shape = jax.ShapeDtypeStruct((n_b, N_HEADS, seq, HEAD_PAD), BF16)
    res = pl.pallas_call(
        body, name="attn_bwd", grid=grid,
        in_specs=[q_spec, full_spec, full_spec, o_spec, o_spec, q_spec] + [HBM_SPEC] * n_buf,
        out_specs=[q_spec, full_spec, full_spec] + [HBM_SPEC] * n_buf,
        out_shape=[head_shape] * 3 + _xchg_out_shapes(bufs),
        scratch_shapes=[pltpu.VMEM((hpb, seq, HEAD_PAD), F32), pltpu.VMEM((hpb, seq, HEAD_PAD), F32)]
        + _xchg_scratch(n_buf),
        compiler_params=_params(("arbitrary", "arbitrary", "arbitrary")),
    )(q, k, v, do, o, lse, *[a for a, _ in bufs])
    return res[0], res[1], res[2], res[3:]


def _ln_silu(u1, g, b):
    mu = jnp.mean(u1, axis=-1, keepdims=True)
    uc = u1 - mu
    r = lax.rsqrt(jnp.mean(uc * uc, axis=-1, keepdims=True) + EPS)
    y = uc * r * g + b
    return y * _sigmoid(y)


def _conv_fill_glu(z_ref, u0_ref, seq, tile):
    u0_ref[0:CONV_HALO, :] = jnp.zeros((CONV_HALO, CONV_CH), F32)
    for t in range(seq // tile):
        zt = z_ref[0, t * tile:(t + 1) * tile, :].astype(F32)
        u0_ref[CONV_HALO + t * tile:CONV_HALO + (t + 1) * tile, :] = zt[:, :CONV_CH] * _sigmoid(zt[:, CONV_CH:])


def _conv_tile(u0_ref, w_ref, b_ref, t, tile):
    acc = jnp.broadcast_to(b_ref[...], (tile, CONV_CH))
    base = t * tile + CONV_HALO - (CONV_WIDTH - 1)
    for kk in range(CONV_WIDTH):
        acc = acc + w_ref[kk:kk + 1, :] * u0_ref[base + kk:base + kk + tile, :]
    return acc


def _conv_fwd(zglu, conv_w, conv_b, ln_g, ln_b, n_b, seq):
    tile = min(256, seq)

    def body(z_ref, w_ref, b_ref, g_ref, bb_ref, o_ref, u0_ref):
        _conv_fill_glu(z_ref, u0_ref, seq, tile)
        for t in range(seq // tile):
            u1 = _conv_tile(u0_ref, w_ref, b_ref, t, tile)
            o_ref[0, t * tile:(t + 1) * tile, :] = _ln_silu(u1, g_ref[...], bb_ref[...]).astype(BF16)

    whole2 = lambda arr: pl.BlockSpec(arr.shape, lambda b: (0, 0))
    return pl.pallas_call(
        body, name="conv_fwd", grid=(n_b,),
        in_specs=[pl.BlockSpec((1, seq, 2 * CONV_CH), lambda b: (b, 0, 0)), whole2(conv_w), whole2(conv_b),
                  whole2(ln_g), whole2(ln_b)],
        out_specs=pl.BlockSpec((1, seq, CONV_CH), lambda b: (b, 0, 0)),
        out_shape=jax.ShapeDtypeStruct((n_b, seq, CONV_CH), BF16),
        scratch_shapes=[pltpu.VMEM((seq + CONV_HALO, CONV_CH), F32)],
        compiler_params=_params(("parallel",)),
    )(zglu, conv_w, conv_b, ln_g, ln_b)


def _conv_bwd(zglu, du3, conv_w, conv_b, ln_g, ln_b, n_b, seq):
    tile = min(256, seq)
    n_t = seq // tile

    def body(z_ref, du3_ref, w_ref, b_ref, g_ref, bb_ref, dz_ref, dw_ref, db_ref, dg_ref, dbb_ref, u0_ref, du1_ref):
        @pl.when(pl.program_id(0) == 0)
        def _():
            for r in (dw_ref, db_ref, dg_ref, dbb_ref):
                r[...] = jnp.zeros_like(r)

        _conv_fill_glu(z_ref, u0_ref, seq, tile)
        du1_ref[seq:seq + CONV_HALO, :] = jnp.zeros((CONV_HALO, CONV_CH), F32)
        g = g_ref[...]
        for t in range(n_t):
            u1 = _conv_tile(u0_ref, w_ref, b_ref, t, tile)
            mu = jnp.mean(u1, axis=-1, keepdims=True)
            uc = u1 - mu
            r = lax.rsqrt(jnp.mean(uc * uc, axis=-1, keepdims=True) + EPS)
            xh = uc * r
            y = xh * g + bb_ref[...]
            sg = _sigmoid(y)
            dy = du3_ref[0, t * tile:(t + 1) * tile, :].astype(F32) * (sg * (1.0 + y * (1.0 - sg)))
            dg_ref[...] += jnp.sum(dy * xh, axis=0, keepdims=True)
            dbb_ref[...] += jnp.sum(dy, axis=0, keepdims=True)
            dxh = dy * g
            du1 = r * (dxh - jnp.mean(dxh, axis=-1, keepdims=True) - xh * jnp.mean(dxh * xh, axis=-1, keepdims=True))
            db_ref[...] += jnp.sum(du1, axis=0, keepdims=True)
            du1_ref[t * tile:(t + 1) * tile, :] = du1
        for t in range(n_t):
            du1 = du1_ref[t * tile:(t + 1) * tile, :]
            du0 = jnp.zeros((tile, CONV_CH), F32)
            base_u = t * tile + CONV_HALO - (CONV_WIDTH - 1)
            base_d = t * tile + (CONV_WIDTH - 1)
            for kk in range(CONV_WIDTH):
                du0 = du0 + w_ref[kk:kk + 1, :] * du1_ref[base_d - kk:base_d - kk + tile, :]
                dw_ref[kk:kk + 1, :] += jnp.sum(du1 * u0_ref[base_u + kk:base_u + kk + tile, :], axis=0, keepdims=True)
            zt = z_ref[0, t * tile:(t + 1) * tile, :].astype(F32)
            ga, sb = zt[:, :CONV_CH], _sigmoid(zt[:, CONV_CH:])
            dz_ref[0, t * tile:(t + 1) * tile, :CONV_CH] = (du0 * sb).astype(BF16)
            dz_ref[0, t * tile:(t + 1) * tile, CONV_CH:] = (du0 * ga * sb * (1.0 - sb)).astype(BF16)

    whole2 = lambda arr: pl.BlockSpec(arr.shape, lambda b: (0, 0))
    z_spec = pl.BlockSpec((1, seq, 2 * CONV_CH), lambda b: (b, 0, 0))
    return pl.pallas_call(
        body, name="conv_bwd", grid=(n_b,),
        in_specs=[z_spec, pl.BlockSpec((1, seq, CONV_CH), lambda b: (b, 0, 0)), whole2(conv_w), whole2(conv_b),
                  whole2(ln_g), whole2(ln_b)],
        out_specs=[z_spec, whole2(conv_w), whole2(conv_b), whole2(ln_g), whole2(ln_b)],
        out_shape=[jax.ShapeDtypeStruct((n_b, seq, 2 * CONV_CH), BF16), jax.ShapeDtypeStruct(conv_w.shape, F32),
                   jax.ShapeDtypeStruct(conv_b.shape, F32), jax.ShapeDtypeStruct(ln_g.shape, F32),
                   jax.ShapeDtypeStruct(ln_b.shape, F32)],
        scratch_shapes=[pltpu.VMEM((seq + CONV_HALO, CONV_CH), F32), pltpu.VMEM((seq + CONV_HALO, CONV_CH), F32)],
        compiler_params=_params(("arbitrary",)),
    )(zglu, du3, conv_w, conv_b, ln_g, ln_b)


def _adamw(name, w, parts, m, v):
    n_parts = parts.shape[0]

    def body(w_ref, p_ref, m_ref, v_ref, g_ref, d_ref, nm_ref, nv_ref):
        gg = p_ref[0].astype(F32)
        for j in range(1, n_parts):
            gg = gg + p_ref[j].astype(F32)
        nm = ADAM_B1 * m_ref[...] + (1.0 - ADAM_B1) * gg
        nv = ADAM_B2 * v_ref[...] + (1.0 - ADAM_B2) * jnp.square(gg)
        m_hat = nm / (1.0 - ADAM_B1 ** ADAM_STEP)
        v_hat = nv / (1.0 - ADAM_B2 ** ADAM_STEP)
        g_ref[...] = gg
        d_ref[...] = -ADAM_LR * (m_hat / (jnp.sqrt(v_hat) + ADAM_EPS) + ADAM_WD * w_ref[...])
        nm_ref[...] = nm
        nv_ref[...] = nv

    shape = jax.ShapeDtypeStruct(w.shape, F32)
    return pl.pallas_call(body, name=name, out_shape=[shape] * 4, compiler_params=_params(None))(w, parts, m, v)


def _rope_tables(seq):
    inv_freq = ROPE_THETA ** (-jnp.arange(0, QK_ROPE_DIM, 2, dtype=F32) / QK_ROPE_DIM)
    ang = jnp.arange(seq, dtype=F32)[:, None] * inv_freq[None, :]
    cos, sin = jnp.cos(ang), jnp.sin(ang)
    half = QK_ROPE_DIM // 2
    z = lambda n: jnp.zeros((seq, n), F32)
    tail = HEAD_PAD - QK_HEAD_DIM
    cos_t = jnp.concatenate([jnp.ones((seq, QK_NOPE_DIM), F32), cos, cos, z(tail)], axis=1)
    sin_lo = jnp.concatenate([z(QK_NOPE_DIM), -sin, z(half), z(tail)], axis=1)
    sin_hi = jnp.concatenate([z(QK_NOPE_DIM), z(half), sin, z(tail)], axis=1)
    return cos_t, sin_lo, sin_hi


def _pad_lanes(v, width=HEAD_PAD):
    return jnp.pad(v, [(0, 0)] * (v.ndim - 1) + [(0, width - v.shape[-1])])


def _unstack_cols(s):
    return s.transpose(1, 0, 2).reshape(s.shape[1], N_DEV * s.shape[2])


def _stack_cols(g, dtype):
    rows, cols = g.shape
    return g.reshape(rows, N_DEV, cols // N_DEV).transpose(1, 0, 2).astype(dtype)


def kernel(x, c, w_ada, b_ada, norm1_g, w_in, q_latent_g, w_uq, kv_latent_g, w_ukv, qk_norm_q_g, qk_norm_k_g, w_o_mla, conv_w, conv_b, conv_ln_g, conv_ln_b, w_pw_out, w_out, norm2_g, w_ff1, w_ff2, loss_target, m_w_ada, m_b_ada, m_norm1_g, m_w_in, m_q_latent_g, m_w_uq, m_kv_latent_g, m_w_ukv, m_qk_norm_q_g, m_qk_norm_k_g, m_w_o_mla, m_conv_w, m_conv_b, m_conv_ln_g, m_conv_ln_b, m_w_pw_out, m_w_out, m_norm2_g, m_w_ff1, m_w_ff2, v_w_ada, v_b_ada, v_norm1_g, v_w_in, v_q_latent_g, v_w_uq, v_kv_latent_g, v_w_ukv, v_qk_norm_q_g, v_qk_norm_k_g, v_w_o_mla, v_conv_w, v_conv_b, v_conv_ln_g, v_conv_ln_b, v_w_pw_out, v_w_out, v_norm2_g, v_w_ff1, v_w_ff2):
    given = dict(locals())
    local = {n: given[n][0] for n in WEIGHTS}
    vec = {n: local[n].reshape(1, -1) for n in REPLICATED}
    bf = lambda n: local[n].astype(BF16)
    n_b, seq, d = x.shape
    n_rows = n_b * seq
    x2 = x.reshape(n_rows, d)
    t2 = loss_target.reshape(n_rows, d)
    rw = functools.partial(_rowwise, n_rows=n_rows, seq=seq)
    me = 4 * lax.axis_index("x") + 2 * lax.axis_index("y") + lax.axis_index("c")
    ada_cols = local["w_ada"].shape[1]

    c_all, w_in_s, w_uq_s, w_ukv_s, conv_w_s = _exchange(
        "gather_early", [(c, True), (bf("w_in"), True), (bf("w_uq"), True), (bf("w_ukv"), True), (local["conv_w"], True)])
    w_in_f = _unstack_cols(w_in_s)
    zeros = lambda n: jnp.zeros((d, n), BF16)
    w_sm = jnp.concatenate([w_in_f[:, :OFF_KV], zeros(QK_NOPE_DIM), w_in_f[:, OFF_KV:OFF_KR], zeros(HEAD_PAD - QK_HEAD_DIM)], axis=1)
    w_glu = w_in_f[:, OFF_KR:OFF_GLU]
    w_gate = w_in_f[:, OFF_GLU:]
    wuq = _pad_lanes(_unstack_cols(w_uq_s).reshape(Q_LORA, N_HEADS, QK_HEAD_DIM)).transpose(1, 0, 2)
    wukv = _unstack_cols(w_ukv_s).reshape(KV_LORA, N_HEADS, QK_NOPE_DIM + V_HEAD_DIM).transpose(1, 0, 2)
    wuk = _pad_lanes(wukv[:, :, :QK_NOPE_DIM])
    wv = wukv[:, :, QK_NOPE_DIM:]
    odd = (jnp.arange(N_HEADS) % 2 == 1)[:, None, None]
    wuv = jnp.where(odd, jnp.pad(wv, ((0, 0), (0, 0), (V_HEAD_DIM, 0))), jnp.pad(wv, ((0, 0), (0, 0), (0, V_HEAD_DIM))))
    gqn = _pad_lanes(vec["qk_norm_q_g"])
    gkn = _pad_lanes(vec["qk_norm_k_g"])
    conv_w_f = jnp.pad(_unstack_cols(conv_w_s), ((0, 1), (0, 0)))
    rope = _rope_tables(seq)

    all_rows = N_DEV * n_b
    pad_rows = (-all_rows) % ROWS_PAD
    c_rows = jnp.pad(c_all.reshape(all_rows, d), ((0, pad_rows), (0, 0)))
    b_cols = lax.dynamic_slice(local["b_ada"], (me * ada_cols,), (ada_cols,))
    mod_cols = _mm("ada_fwd", c_rows, local["w_ada"], "nn", F32, a_fn=_silu, epi=lambda acc, b: acc + b,
                   epi_in=(jnp.broadcast_to(b_cols, (all_rows + pad_rows, ada_cols)),))
    (mod_s,) = _exchange("scatter_mod", [(mod_cols[:all_rows].reshape(N_DEV, n_b, ada_cols), False)])
    mod = mod_s.transpose(1, 0, 2).reshape(n_b, ADA_CHUNKS, 1, d)
    shift1, scale1, gate1, shift2, scale2, gate2 = [mod[:, i] for i in range(ADA_CHUNKS)]

    (h,) = rw("norm1_fwd", lambda r, b, cc: ([_norm_mod(r[0], cc[0], b[0], b[1])], [], []),
              rows=[_full(x2)], bats=[scale1, shift1], consts=[vec["norm1_g"]], outs=[(d, BF16)])
    zsm = _mm("in_proj_sm", h, w_sm, "nn", F32)
    zglu = _mm("in_proj_glu", h, w_glu, "nn", BF16)
    zgate = _mm("in_proj_gate", h, w_gate, "nn", BF16)
    q, k, v = _mla_prep_fwd(zsm, wuq, wuk, wuv, vec["q_latent_g"], vec["kv_latent_g"], gqn, gkn, rope, n_b, seq)
    attn, lse, (w_o_s, w_pw_s, w_out_s, w_ff1_s, w_ff2_s) = _attn_fwd(
        q, k, v, [(bf("w_o_mla"), True), (bf("w_pw_out"), True), (bf("w_out"), True), (bf("w_ff1"), True),
                  (bf("w_ff2"), True)], n_b, seq)
    w_o_f = _unstack_cols(w_o_s)
    w_pw_f = _unstack_cols(w_pw_s)
    w_out_f = w_out_s.reshape(d, d)
    w_ff2_f = w_ff2_s.reshape(N_DEV * w_ff2_s.shape[1], d)
    attn2 = attn.reshape(n_rows, N_HEADS * V_HEAD_DIM)
    u3 = _conv_fwd(zglu.reshape(n_b, seq, 2 * CONV_CH), conv_w_f, vec["conv_b"], vec["conv_ln_g"], vec["conv_ln_b"], n_b, seq)
    u32 = u3.reshape(n_rows, CONV_CH)
    ya = _mm("mla_out", attn2, w_o_f, "nn", BF16)
    yb = _mm("conv_out", u32, w_pw_f, "nn", BF16)
    (mrg,) = rw("merge_fwd",
                lambda r, b, cc: ([_sigmoid(r[0].astype(F32)) * r[2].astype(F32) + _sigmoid(r[1].astype(F32)) * r[3].astype(F32)], [], []),
                rows=[(zgate, d, 0), (zgate, d, 1), _full(ya), _full(yb)], outs=[(d, BF16)])
    mixed = _mm("out_proj", mrg, w_out_f, "nn", BF16)

    def mid_fn(r, b, cc):
        x1_ = r[0] + b[0] * r[1].astype(F32)
        return [x1_, _norm_mod(x1_, cc[0], b[1], b[2])], [], []

    x1, h2 = rw("norm2_fwd", mid_fn, rows=[_full(x2), _full(mixed)], bats=[gate1, scale2, shift2],
                consts=[vec["norm2_g"]], outs=[(d, F32), (d, BF16)])
    a = _mm("ff1", h2, w_ff1_s, "nn", BF16, b_stacked=True)
    f = _mm("ff2", a, w_ff2_f, "nn", BF16, a_fn=_relu2)

    def loss_fn(r, b, cc):
        ff = r[1].astype(F32)
        err = r[0] + b[0] * ff - r[2]
        dy_ = err * (1.0 / d)
        sq = jnp.broadcast_to(jnp.sum(err * err, keepdims=True), (1, LANES))
        return [dy_, b[0] * dy_], [jnp.sum(dy_ * ff, axis=0, keepdims=True)], [sq]

    dy, df, dgate2, sq_err = rw("loss", loss_fn, rows=[_full(x1), _full(f), _full(t2)], bats=[gate2],
                                outs=[(d, F32), (d, BF16)], bat_outs=[d], tot_outs=[(1, LANES)])
    loss = lax.psum(sq_err[0, 0] * (0.5 / d), MESH_AXES)

    da = _mm("ff2_bwd", df, w_ff2_f, "nt", BF16, epi=lambda acc, av: acc * 2.0 * jnp.maximum(av, 0.0), epi_in=(a,))
    g_ff2 = _mm("ff2_dw", a, df, "tn", BF16, a_fn=_relu2)
    dh2 = _mm("ff1_bwd", da, w_ff1_s, "nt", F32, b_stacked=True)
    g_ff1_s = _mm("ff1_dw", h2, da, "tn", BF16, out_stacked=True)

    def mid_bwd(r, b, cc):
        dx, dsc, dsh, dg = _norm_mod_bwd(r[0], cc[0], b[0], r[1])
        dx1_ = r[2] + dx
        return [dx1_, b[1] * dx1_], [dsc, dsh, jnp.sum(dx1_ * r[3].astype(F32), axis=0, keepdims=True)], [dg]

    dx1, dmixed, dscale2, dshift2, dgate1, g_norm2 = rw(
        "norm2_bwd", mid_bwd, rows=[_full(x1), _full(dh2), _full(dy), _full(mixed)], bats=[scale2, gate1],
        consts=[vec["norm2_g"]], outs=[(d, F32), (d, BF16)], bat_outs=[d, d, d], tot_outs=[(1, d)])

    dmrg = _mm("out_proj_bwd", dmixed, w_out_f, "nt", BF16)
    g_out = _mm("out_proj_dw", mrg, dmixed, "tn", BF16)

    def merge_bwd(r, b, cc):
        dm, ya_, yb_ = r[0].astype(F32), r[3].astype(F32), r[4].astype(F32)
        sa, sb = _sigmoid(r[1].astype(F32)), _sigmoid(r[2].astype(F32))
        return [dm * ya_ * sa * (1.0 - sa), dm * yb_ * sb * (1.0 - sb), dm * sa, dm * sb], [], []

    dzga, dzgb, dya, dyb = rw("merge_bwd", merge_bwd,
                              rows=[_full(dmrg), (zgate, d, 0), (zgate, d, 1), _full(ya), _full(yb)],
                              outs=[(d, BF16)] * 4)
    dattn = _mm("mla_out_bwd", dya, w_o_f, "nt", BF16)
    g_o = _mm("mla_out_dw", attn2, dya, "tn", F32)
    du3 = _mm("conv_out_bwd", dyb, w_pw_f, "nt", BF16)
    g_pw = _mm("conv_out_dw", u32, dyb, "tn", F32)

    dzglu, g_conv_w, g_conv_b, g_ln_g, g_ln_b = _conv_bwd(
        zglu.reshape(n_b, seq, 2 * CONV_CH), du3.reshape(n_b, seq, CONV_CH), conv_w_f, vec["conv_b"], vec["conv_ln_g"],
        vec["conv_ln_b"], n_b, seq)
    dzglu = dzglu.reshape(n_rows, 2 * CONV_CH)

    dq, dk, dv, (p_ff2, p_ff1, p_out, p_pw, p_o) = _attn_bwd(
        q, k, v, dattn.reshape(n_b, seq, N_HEADS * V_HEAD_DIM), attn, lse,
        [(g_ff2.reshape(N_DEV, -1, d), False), (g_ff1_s, False), (g_out.reshape(N_DEV, -1, d), False),
         (_stack_cols(g_pw, BF16), False), (_stack_cols(g_o, BF16), False)], n_b, seq)
    dzsm, g_wuq, g_wuk, g_wuv, g_gq, g_gkv, g_gqn, g_gkn = _mla_prep_bwd(
        zsm, dq, dk, dv, wuq, wuk, wuv, vec["q_latent_g"], vec["kv_latent_g"], gqn, gkn, rope, n_b, seq)

    add = lambda acc, prev: acc + prev
    dh = _mm("in_proj_gate_bwd_a", dzga, w_gate[:, :d], "nt", F32)
    dh = _mm("in_proj_gate_bwd_b", dzgb, w_gate[:, d:], "nt", F32, epi=add, epi_in=(dh,))
    dh = _mm("in_proj_glu_bwd", dzglu, w_glu, "nt", F32, epi=add, epi_in=(dh,))
    dh = _mm("in_proj_sm_bwd", dzsm, w_sm, "nt", F32, epi=add, epi_in=(dh,))
    g_gate_a = _mm("in_proj_gate_dw_a", h, dzga, "tn", F32)
    g_gate_b = _mm("in_proj_gate_dw_b", h, dzgb, "tn", F32)
    g_glu = _mm("in_proj_glu_dw", h, dzglu, "tn", F32)
    g_sm = _mm("in_proj_sm_dw", h, dzsm, "tn", F32)

    def first_bwd(r, b, cc):
        dx, dsc, dsh, dg = _norm_mod_bwd(r[0], cc[0], b[0], r[1])
        return [r[2] + dx], [dsc, dsh], [dg]

    grad_x, dscale1, dshift1, g_norm1 = rw("norm1_bwd", first_bwd, rows=[_full(x2), _full(dh), _full(dx1)],
                                            bats=[scale1], consts=[vec["norm1_g"]], outs=[(d, F32)], bat_outs=[d, d],
                                            tot_outs=[(1, d)])

    dmod = jnp.concatenate([dshift1, dscale1, dgate1, dshift2, dscale2, dgate2], axis=1).reshape(n_b, N_DEV, ada_cols)
    g_in = jnp.concatenate([g_sm[:, :OFF_KV], g_sm[:, OFF_KV + QK_NOPE_DIM:OFF_KV + QK_NOPE_DIM + QK_ROPE_DIM], g_glu,
                            g_gate_a, g_gate_b], axis=1)
    g_uq = g_wuq[:, :, :QK_HEAD_DIM].transpose(1, 0, 2).reshape(Q_LORA, N_HEADS * QK_HEAD_DIM)
    g_v = jnp.where(odd, g_wuv[:, :, V_HEAD_DIM:], g_wuv[:, :, :V_HEAD_DIM])
    g_ukv = jnp.concatenate([g_wuk[:, :, :QK_NOPE_DIM], g_v], axis=2).transpose(1, 0, 2).reshape(KV_LORA, -1)
    dmod_s, p_in, p_uq, p_ukv, p_conv_w = _exchange(
        "scatter_late", [(dmod.transpose(1, 0, 2), False), (_stack_cols(g_in, BF16), False),
                         (_stack_cols(g_uq, BF16), False), (_stack_cols(g_ukv, BF16), False),
                         (_stack_cols(g_conv_w[:CONV_WIDTH], F32), False)])
    dmod_rows = jnp.pad(dmod_s.reshape(all_rows, ada_cols), ((0, pad_rows), (0, 0)))
    g_ada = _mm("ada_dw", c_rows, dmod_rows, "tn", F32, a_fn=_silu)
    (g_b_cols,) = _rowwise("ada_db", lambda r, b, cc: ([], [], [jnp.sum(r[0], axis=0, keepdims=True)]),
                           all_rows + pad_rows, all_rows + pad_rows, rows=[_full(dmod_rows)], tot_outs=[(1, ada_cols)])

    partial_of = {"norm1_g": g_norm1, "q_latent_g": g_gq, "kv_latent_g": g_gkv, "qk_norm_q_g": g_gqn,
                  "qk_norm_k_g": g_gkn, "conv_b": g_conv_b, "conv_ln_g": g_ln_g, "conv_ln_b": g_ln_b, "norm2_g": g_norm2}
    names = [n for n in REPLICATED if n != "b_ada"]
    pieces = [_pad_lanes(partial_of[n], -(-partial_of[n].shape[1] // LANES) * LANES) for n in names] + [g_b_cols]
    widths = [p.shape[1] for p in pieces]
    small = jnp.concatenate(pieces, axis=1)
    small = _pad_lanes(small, -(-small.shape[1] // (8 * LANES)) * 8 * LANES).reshape(-1, LANES)
    (small_s,) = _exchange("gather_small_grads", [(small, True)])
    small_s = small_s.reshape(N_DEV, 1, -1)
    parts = {}
    off = 0
    for n, wd in zip(names, widths):
        parts[n] = small_s[:, :, off:off + vec[n].shape[1]]
        off += wd
    parts["b_ada"] = small_s[:, 0, off:off + ada_cols].reshape(1, 1, N_DEV * ada_cols)
    parts.update({"w_ada": g_ada[None], "w_in": p_in, "w_uq": p_uq, "w_ukv": p_ukv, "w_o_mla": p_o, "conv_w": p_conv_w,
                  "w_pw_out": p_pw, "w_out": p_out, "w_ff1": p_ff1, "w_ff2": p_ff2})

    grad_out, delta_out, m_out, v_out = [], [], [], []
    for n in WEIGHTS:
        shape2 = parts[n].shape[1:]
        g_w, d_w, n_m, n_v = _adamw("adamw_" + n, local[n].reshape(shape2), parts[n], given["m_" + n].reshape(shape2),
                                    given["v_" + n].reshape(shape2))
        full_shape = given[n].shape
        grad_out.append(g_w.reshape(full_shape))
        delta_out.append(d_w.reshape(full_shape))
        m_out.append(n_m.reshape(full_shape))
        v_out.append(n_v.reshape(full_shape))
    return (loss, grad_x.reshape(n_b, seq, d), *grad_out, *delta_out, *m_out, *v_out)
```

```python
import functools

import jax
import jax.numpy as jnp
from jax import lax
from jax.experimental import pallas as pl
from jax.experimental.pallas import tpu as pltpu

F32 = jnp.float32
BF16 = jnp.bfloat16

N_DEV = 8
MESH_AXES = ("x", "y", "c")
EPS = 1e-6
N_HEADS = 8
QK_HEAD_DIM = 96
QK_NOPE_DIM = 64
QK_ROPE_DIM = 32
V_HEAD_DIM = 64
HEAD_PAD = 128
Q_LORA = 256
KV_LORA = 128
CONV_CH = 512
CONV_WIDTH = 31
CONV_HALO = 32
CHUNK = 64
ROPE_THETA = 10000.0
OFF_Q = Q_LORA
OFF_KV = OFF_Q + KV_LORA
OFF_KR = OFF_KV + QK_ROPE_DIM
OFF_GLU = OFF_KR + 2 * CONV_CH
ADA_CHUNKS = 6
ADAM_LR = 0.001
ADAM_B1 = 0.9
ADAM_B2 = 0.999
ADAM_EPS = 1e-08
ADAM_WD = 0.01
ADAM_STEP = 10
LANES = 128
VMEM_LIMIT = 56 * 1024 * 1024
NEG_BIG = -1e30
ATT_HEADS = 4
ATT_SCALE = QK_HEAD_DIM ** -0.5
LOG2E = 1.4426950408889634
LN2 = 0.6931471805599453
QK_SCALE = ATT_SCALE * LOG2E
ROWS_PAD = 16

REPLICATED = ("b_ada", "norm1_g", "q_latent_g", "kv_latent_g", "qk_norm_q_g", "qk_norm_k_g", "conv_b", "conv_ln_g",
              "conv_ln_b", "norm2_g")
WEIGHTS = ("w_ada", "b_ada", "norm1_g", "w_in", "q_latent_g", "w_uq", "kv_latent_g", "w_ukv", "qk_norm_q_g",
           "qk_norm_k_g", "w_o_mla", "conv_w", "conv_b", "conv_ln_g", "conv_ln_b", "w_pw_out", "w_out", "norm2_g",
           "w_ff1", "w_ff2")


def _tile(dim, pref):
    if dim <= pref:
        return dim
    t = (pref // LANES) * LANES
    while dim % t:
        t -= LANES
    return t


def _params(semantics):
    return pltpu.CompilerParams(dimension_semantics=semantics, vmem_limit_bytes=VMEM_LIMIT)


def _sigmoid(v):
    return 1.0 / (1.0 + jnp.exp(-v))


def _silu(v):
    return v * _sigmoid(v)


def _relu2(v):
    return jnp.square(jnp.maximum(v, 0.0))


_DIMS = {"nn": (((1,), (0,)), ((), ())), "nt": (((1,), (1,)), ((), ())), "tn": (((0,), (0,)), ((), ()))}


def _mm(name, a, b, mode, out_dtype, *, a_fn=None, epi=None, epi_in=(), b_stacked=False, out_stacked=False,
        tm=1024, tn=1024, tk=1024):
    if b_stacked:
        shard = b.shape[2]
        b_rows, b_cols = b.shape[1], N_DEV * shard
    else:
        b_rows, b_cols = b.shape
    if mode == "nn":
        (m, k), n = a.shape, b_cols
    elif mode == "nt":
        (m, k), n = a.shape, b_rows
    else:
        (k, m), n = a.shape, b_cols
    if out_stacked:
        shard = n // N_DEV
    tm = _tile(m, tm)
    tn = _tile(shard, tn) if (out_stacked or (b_stacked and mode != "nt")) else _tile(n, tn)
    tk = _tile(shard, tk) if (b_stacked and mode == "nt") else _tile(k, tk)
    nk = k // tk
    a_spec = (pl.BlockSpec((tk, tm), lambda i, j, kk: (kk, i)) if mode == "tn"
              else pl.BlockSpec((tm, tk), lambda i, j, kk: (i, kk)))
    if b_stacked and mode == "nt":
        per = shard // tk
        b_spec = pl.BlockSpec((None, tn, tk), lambda i, j, kk: (kk // per, j, kk % per))
    elif b_stacked:
        per = shard // tn
        b_spec = pl.BlockSpec((None, tk, tn), lambda i, j, kk: (j // per, kk, j % per))
    elif mode == "nt":
        b_spec = pl.BlockSpec((tn, tk), lambda i, j, kk: (j, kk))
    else:
        b_spec = pl.BlockSpec((tk, tn), lambda i, j, kk: (kk, j))
    e_spec = pl.BlockSpec((tm, tn), lambda i, j, kk: (i, j))
    if out_stacked:
        per_o = shard // tn
        o_spec = pl.BlockSpec((None, tm, tn), lambda i, j, kk: (j // per_o, i, j % per_o))
        out_shape = jax.ShapeDtypeStruct((N_DEV, m, shard), out_dtype)
    else:
        o_spec = e_spec
        out_shape = jax.ShapeDtypeStruct((m, n), out_dtype)
    n_epi = len(epi_in)

    def body(a_ref, b_ref, *rest):
        epi_refs, o_ref, acc_ref = rest[:n_epi], rest[n_epi], rest[n_epi + 1]
        kk = pl.program_id(2)

        @pl.when(kk == 0)
        def _():
            acc_ref[...] = jnp.zeros_like(acc_ref)

        av = a_ref[...]
        if a_fn is not None:
            av = a_fn(av.astype(F32))
        acc_ref[...] += lax.dot_general(av.astype(BF16), b_ref[...].astype(BF16), _DIMS[mode],
                                        preferred_element_type=F32)

        @pl.when(kk == nk - 1)
        def _():
            acc = acc_ref[...]
            if epi is not None:
                acc = epi(acc, *[r[...].astype(F32) for r in epi_refs])
            o_ref[...] = acc.astype(out_dtype)

    return pl.pallas_call(
        body, name=name, grid=(m // tm, n // tn, nk),
        in_specs=[a_spec, b_spec] + [e_spec] * n_epi, out_specs=o_spec, out_shape=out_shape,
        scratch_shapes=[pltpu.VMEM((tm, tn), F32)],
        compiler_params=_params(("parallel", "parallel", "arbitrary")),
    )(a, b, *epi_in)


def _rowwise(name, fn, n_rows, seq, rows, bats=(), consts=(), outs=(), bat_outs=(), tot_outs=(), tm=256):
    tm = min(tm, seq)
    per_seq = seq // tm
    n_b = n_rows // seq
    nr, nb, nc, no, nbo, nto = len(rows), len(bats), len(consts), len(outs), len(bat_outs), len(tot_outs)

    def body(*refs):
        i = pl.program_id(0)
        r_in = [r[...] for r in refs[:nr]]
        b_in = [r[0] for r in refs[nr:nr + nb]]
        c_in = [r[...] for r in refs[nr + nb:nr + nb + nc]]
        o_refs = refs[nr + nb + nc:nr + nb + nc + no]
        bo_refs = refs[nr + nb + nc + no:nr + nb + nc + no + nbo]
        to_refs = refs[nr + nb + nc + no + nbo:]
        o_val, bo_val, to_val = fn(r_in, b_in, c_in)
        for r, v in zip(o_refs, o_val):
            r[...] = v.astype(r.dtype)
        if nbo:
            @pl.when(i % per_seq == 0)
            def _():
                for r in bo_refs:
                    r[...] = jnp.zeros_like(r)

            for r, v in zip(bo_refs, bo_val):
                r[0] += v
        if nto:
            @pl.when(i == 0)
            def _():
                for r in to_refs:
                    r[...] = jnp.zeros_like(r)

            for r, v in zip(to_refs, to_val):
                r[...] += v

    in_specs = [pl.BlockSpec((tm, w), functools.partial(lambda cb, i: (i, cb), cb)) for (_, w, cb) in rows]
    in_specs += [pl.BlockSpec((1, 1, bt.shape[2]), lambda i: (i // per_seq, 0, 0)) for bt in bats]
    in_specs += [pl.BlockSpec(ct.shape, lambda i: (0, 0)) for ct in consts]
    out_specs = [pl.BlockSpec((tm, w), lambda i: (i, 0)) for (w, _) in outs]
    out_specs += [pl.BlockSpec((1, 1, w), lambda i: (i // per_seq, 0, 0)) for w in bat_outs]
    out_specs += [pl.BlockSpec(shp, lambda i: (0, 0)) for shp in tot_outs]
    out_shape = [jax.ShapeDtypeStruct((n_rows, w), dt) for (w, dt) in outs]
    out_shape += [jax.ShapeDtypeStruct((n_b, 1, w), F32) for w in bat_outs]
    out_shape += [jax.ShapeDtypeStruct(shp, F32) for shp in tot_outs]
    res = pl.pallas_call(
        body, name=name, grid=(n_rows // tm,), in_specs=in_specs, out_specs=out_specs, out_shape=out_shape,
        compiler_params=_params(("arbitrary",)),
    )(*[r[0] for r in rows], *bats, *consts)
    return res


def _full(arr):
    return (arr, arr.shape[1], 0)


def _norm_mod(x, g, scale, shift):
    r = lax.rsqrt(jnp.mean(x * x, axis=-1, keepdims=True) + EPS)
    xh = x * r
    return xh * g * (1.0 + scale) + shift


def _norm_mod_bwd(x, g, scale, dh):
    r = lax.rsqrt(jnp.mean(x * x, axis=-1, keepdims=True) + EPS)
    xh = x * r
    dn = dh * (1.0 + scale)
    dxh = dn * g
    dx = r * (dxh - xh * jnp.mean(dxh * xh, axis=-1, keepdims=True))
    dscale = jnp.sum(dh * xh * g, axis=0, keepdims=True)
    dshift = jnp.sum(dh, axis=0, keepdims=True)
    dg = jnp.sum(dn * xh, axis=0, keepdims=True)
    return dx, dscale, dshift, dg


def _rms(v, g):
    r = lax.rsqrt(jnp.mean(v * v, axis=-1, keepdims=True) + EPS)
    return v * r * g


def _rms_bwd(v, g, dy):
    r = lax.rsqrt(jnp.mean(v * v, axis=-1, keepdims=True) + EPS)
    vh = v * r
    dvh = dy * g
    dv = r * (dvh - vh * jnp.mean(dvh * vh, axis=-1, keepdims=True))
    return dv, jnp.sum(dy * vh, axis=0, keepdims=True)


def _head_norm(v, g):
    r = lax.rsqrt(jnp.sum(v * v, axis=-1, keepdims=True) * (1.0 / QK_HEAD_DIM) + EPS)
    return v * r * g


def _head_norm_bwd(v, g, dy):
    r = lax.rsqrt(jnp.sum(v * v, axis=-1, keepdims=True) * (1.0 / QK_HEAD_DIM) + EPS)
    vh = v * r
    dvh = dy * g
    dv = r * (dvh - vh * (jnp.sum(dvh * vh, axis=-1, keepdims=True) * (1.0 / QK_HEAD_DIM)))
    return dv, jnp.sum(dy * vh, axis=0, keepdims=True)


def _rope(v, cos, sin_lo, sin_hi):
    return v * cos + pltpu.roll(v, HEAD_PAD - 16, 1) * sin_lo + pltpu.roll(v, 16, 1) * sin_hi


def _rope_bwd(g, cos, sin_lo, sin_hi):
    return g * cos + pltpu.roll(g * sin_lo, 16, 1) + pltpu.roll(g * sin_hi, HEAD_PAD - 16, 1)


def _mla_prep_fwd(zsm, wuq, wukv, gq, gkv, gqn, gkn, rope, n_b, seq):
    n_rows = n_b * seq
    tm = min(256, seq)
    per_seq = seq // tm
    k_cols = N_HEADS * HEAD_PAD

    def body(z_ref, wuq_ref, wukv_ref, gq_ref, gkv_ref, gqn_ref, gkn_ref, c_ref, s1_ref, s2_ref,
             q_ref, k_ref, v_ref):
        z = z_ref[...]
        qn = _rms(z[:, :Q_LORA], gq_ref[...]).astype(BF16)
        kvn = _rms(z[:, Q_LORA:Q_LORA + KV_LORA], gkv_ref[...]).astype(BF16)
        krp = z[:, Q_LORA + KV_LORA:]
        cos, s1, s2 = c_ref[...], s1_ref[...], s2_ref[...]
        q_all = jnp.dot(qn, wuq_ref[...], preferred_element_type=F32)
        kv_all = jnp.dot(kvn, wukv_ref[...], preferred_element_type=F32)
        for h in range(N_HEADS):
            cols = slice(h * HEAD_PAD, (h + 1) * HEAD_PAD)
            q_ref[0, h] = (_rope(_head_norm(q_all[:, cols], gqn_ref[...]), cos, s1, s2) * QK_SCALE).astype(BF16)
            k_ref[0, h] = _rope(_head_norm(kv_all[:, cols] + krp, gkn_ref[...]), cos, s1, s2).astype(BF16)
            v_ref[0, h] = kv_all[:, k_cols + h * HEAD_PAD:k_cols + (h + 1) * HEAD_PAD].astype(BF16)

    whole2 = lambda arr: pl.BlockSpec(arr.shape, lambda i: (0, 0))
    rope_spec = pl.BlockSpec((tm, HEAD_PAD), lambda i: (i % per_seq, 0))
    head_spec = pl.BlockSpec((1, N_HEADS, tm, HEAD_PAD), lambda i: (i // per_seq, 0, i % per_seq, 0))
    head_shape = jax.ShapeDtypeStruct((n_b, N_HEADS, seq, HEAD_PAD), BF16)
    return pl.pallas_call(
        body, name="mla_prep_fwd", grid=(n_rows // tm,),
        in_specs=[pl.BlockSpec((tm, 512), lambda i: (i, 0)), whole2(wuq), whole2(wukv),
                  whole2(gq), whole2(gkv), whole2(gqn), whole2(gkn), rope_spec, rope_spec, rope_spec],
        out_specs=[head_spec] * 3, out_shape=[head_shape] * 3,
        compiler_params=_params(("parallel",)),
    )(zsm, wuq, wukv, gq, gkv, gqn, gkn, *rope)


def _mla_prep_bwd(zsm, dq, dk, dv, wuq, wukv, gq, gkv, gqn, gkn, rope, n_b, seq):
    n_rows = n_b * seq
    tm = min(256, seq)
    per_seq = seq // tm
    tn_dims = _DIMS["tn"]
    nt_dims = _DIMS["nt"]
    k_cols = N_HEADS * HEAD_PAD

    def body(z_ref, dq_ref, dk_ref, dv_ref, wuq_ref, wukv_ref, gq_ref, gkv_ref, gqn_ref, gkn_ref,
             c_ref, s1_ref, s2_ref, dz_ref, dwuq_ref, dwukv_ref, dgq_ref, dgkv_ref, dgqn_ref, dgkn_ref):
        @pl.when(pl.program_id(0) == 0)
        def _():
            for r in (dwuq_ref, dwukv_ref, dgq_ref, dgkv_ref, dgqn_ref, dgkn_ref):
                r[...] = jnp.zeros_like(r)

        z = z_ref[...]
        zq, zkv, krp = z[:, :Q_LORA], z[:, Q_LORA:Q_LORA + KV_LORA], z[:, Q_LORA + KV_LORA:]
        qn = _rms(zq, gq_ref[...]).astype(BF16)
        kvn = _rms(zkv, gkv_ref[...]).astype(BF16)
        cos, s1, s2 = c_ref[...], s1_ref[...], s2_ref[...]
        lane = lax.broadcasted_iota(jnp.int32, (tm, HEAD_PAD), 1)
        rope_lanes = (lane >= QK_NOPE_DIM) & (lane < QK_HEAD_DIM)
        q_all = jnp.dot(qn, wuq_ref[...], preferred_element_type=F32)
        k_all = jnp.dot(kvn, wukv_ref[:, :k_cols], preferred_element_type=F32)
        dkrp = jnp.zeros((tm, HEAD_PAD), F32)
        dgqn = jnp.zeros((1, HEAD_PAD), F32)
        dgkn = jnp.zeros((1, HEAD_PAD), F32)
        dq_heads, dk_heads = [], []
        for h in range(N_HEADS):
            cols = slice(h * HEAD_PAD, (h + 1) * HEAD_PAD)
            dqh, dg = _head_norm_bwd(q_all[:, cols], gqn_ref[...],
                                     _rope_bwd(dq_ref[0, h].astype(F32) * ATT_SCALE, cos, s1, s2))
            dgqn += dg
            dq_heads.append(dqh.astype(BF16))
            dkh, dg = _head_norm_bwd(k_all[:, cols] + krp, gkn_ref[...], _rope_bwd(dk_ref[0, h].astype(F32), cos, s1, s2))
            dgkn += dg
            dkrp += jnp.where(rope_lanes, dkh, 0.0)
            dk_heads.append(dkh.astype(BF16))
        dq_all = jnp.concatenate(dq_heads, axis=1)
        dkv_all = jnp.concatenate(dk_heads + [dv_ref[0, h] for h in range(N_HEADS)], axis=1)
        dwuq_ref[...] += lax.dot_general(qn, dq_all, tn_dims, preferred_element_type=F32)
        dqn = lax.dot_general(dq_all, wuq_ref[...], nt_dims, preferred_element_type=F32)
        dwukv_ref[...] += lax.dot_general(kvn, dkv_all, tn_dims, preferred_element_type=F32)
        dkvn = lax.dot_general(dkv_all, wukv_ref[...], nt_dims, preferred_element_type=F32)
        dzq, dg = _rms_bwd(zq, gq_ref[...], dqn)
        dgq_ref[...] += dg
        dzkv, dg = _rms_bwd(zkv, gkv_ref[...], dkvn)
        dgkv_ref[...] += dg
        dgqn_ref[...] += dgqn
        dgkn_ref[...] += dgkn
        dz_ref[:, :Q_LORA] = dzq.astype(dz_ref.dtype)
        dz_ref[:, Q_LORA:Q_LORA + KV_LORA] = dzkv.astype(dz_ref.dtype)
        dz_ref[:, Q_LORA + KV_LORA:] = dkrp.astype(dz_ref.dtype)

    whole2 = lambda arr: pl.BlockSpec(arr.shape, lambda i: (0, 0))
    rope_spec = pl.BlockSpec((tm, HEAD_PAD), lambda i: (i % per_seq, 0))
    head_spec = pl.BlockSpec((1, N_HEADS, tm, HEAD_PAD), lambda i: (i // per_seq, 0, i % per_seq, 0))
    row_spec = pl.BlockSpec((tm, 512), lambda i: (i, 0))
    return pl.pallas_call(
        body, name="mla_prep_bwd", grid=(n_rows // tm,),
        in_specs=[row_spec, head_spec, head_spec, head_spec, whole2(wuq), whole2(wukv),
                  whole2(gq), whole2(gkv), whole2(gqn), whole2(gkn), rope_spec, rope_spec, rope_spec],
        out_specs=[row_spec, whole2(wuq), whole2(wukv), whole2(gq), whole2(gkv), whole2(gqn), whole2(gkn)],
        out_shape=[jax.ShapeDtypeStruct((n_rows, 512), BF16),
                   jax.ShapeDtypeStruct(wuq.shape, F32), jax.ShapeDtypeStruct(wukv.shape, F32),
                   jax.ShapeDtypeStruct(gq.shape, F32), jax.ShapeDtypeStruct(gkv.shape, F32),
                   jax.ShapeDtypeStruct(gqn.shape, F32), jax.ShapeDtypeStruct(gkn.shape, F32)],
        compiler_params=_params(("arbitrary",)),
    )(zsm, dq, dk, dv, wuq, wukv, gq, gkv, gqn, gkn, *rope)


HBM_SPEC = pl.BlockSpec(memory_space=pltpu.HBM)


def _xchg_out_shapes(bufs):
    return [jax.ShapeDtypeStruct((N_DEV,) + (a.shape if gather else a.shape[1:]), a.dtype) for a, gather in bufs]


def _xchg_scratch(n_buf):
    return [pltpu.SemaphoreType.DMA((n_buf * (N_DEV - 1),)), pltpu.SemaphoreType.DMA((n_buf * (N_DEV - 1),)),
            pltpu.SemaphoreType.DMA((n_buf,))]


def _xchg_copies(src_refs, dst_refs, gathers, send_sems, recv_sems, local_sems):
    x, y, c = lax.axis_index("x"), lax.axis_index("y"), lax.axis_index("c")
    me = 4 * x + 2 * y + c
    local, starts, arrivals = [], [], []
    for bi, (src, dst, gather) in enumerate(zip(src_refs, dst_refs, gathers)):
        local.append(pltpu.make_async_copy(src if gather else src.at[me], dst.at[me], local_sems.at[bi]))
        for kk in range(1, N_DEV):
            px = 1 - x if kk & 4 else x
            py = 1 - y if kk & 2 else y
            pc = 1 - c if kk & 1 else c
            pid = 4 * px + 2 * py + pc
            sem = bi * (N_DEV - 1) + kk - 1
            starts.append(pltpu.make_async_remote_copy(
                src_ref=src if gather else src.at[pid], dst_ref=dst.at[me],
                send_sem=send_sems.at[sem], recv_sem=recv_sems.at[sem],
                device_id=(px, py, pc), device_id_type=pl.DeviceIdType.MESH))
            arrivals.append(pltpu.make_async_remote_copy(
                src_ref=src if gather else src.at[me], dst_ref=dst.at[pid],
                send_sem=send_sems.at[sem], recv_sem=recv_sems.at[sem],
                device_id=(px, py, pc), device_id_type=pl.DeviceIdType.MESH))
    return local, starts, arrivals


def _xchg_start(copies):
    local, sends, _ = copies
    for cp in local + sends:
        cp.start()


def _xchg_finish(copies):
    local, sends, arrivals = copies
    for cp in arrivals:
        cp.wait_recv()
    for cp in sends:
        cp.wait_send()
    for cp in local:
        cp.wait()


def _exchange(name, bufs):
    n_buf = len(bufs)
    gathers = [g for _, g in bufs]

    def body(*refs):
        srcs, dsts = refs[:n_buf], refs[n_buf:2 * n_buf]
        copies = _xchg_copies(srcs, dsts, gathers, *refs[2 * n_buf:])
        _xchg_start(copies)
        _xchg_finish(copies)

    return pl.pallas_call(
        body, name=name, out_shape=_xchg_out_shapes(bufs),
        in_specs=[HBM_SPEC] * n_buf, out_specs=[HBM_SPEC] * n_buf, scratch_shapes=_xchg_scratch(n_buf),
    )(*[a for a, _ in bufs])


def _chunk_mask(t):
    r = lax.broadcasted_iota(jnp.int32, (t, t), 0) // CHUNK
    c = lax.broadcasted_iota(jnp.int32, (t, t), 1) // CHUNK
    return r >= c


def _grid_ends(grid):
    ids = [pl.program_id(ax) for ax in range(len(grid))]
    first = functools.reduce(jnp.logical_and, [i == 0 for i in ids])
    last = functools.reduce(jnp.logical_and, [i == g - 1 for i, g in zip(ids, grid)])
    return first, last


def _attn_fwd(q, k, v, bufs, n_b, seq):
    tq = min(256, seq)
    nq = seq // tq
    nt_dims = _DIMS["nt"]
    hpb = ATT_HEADS
    grid = (n_b, N_HEADS // hpb, nq)
    n_buf = len(bufs)
    gathers = [g for _, g in bufs]

    def body(q_ref, k_ref, v_ref, *rest):
        srcs, (o_ref, lse_ref), dsts = rest[:n_buf], rest[n_buf:n_buf + 2], rest[n_buf + 2:2 * n_buf + 2]
        copies = _xchg_copies(srcs, dsts, gathers, *rest[2 * n_buf + 2:])
        first, last = _grid_ends(grid)
        pl.when(first)(functools.partial(_xchg_start, copies))

        qi = pl.program_id(2)
        mask = _chunk_mask(tq)
        qs = [q_ref[0, hh] for hh in range(hpb)]

        def step(j, carry, masked):
            rows = pl.ds(pl.multiple_of(j * tq, tq), tq)
            out = []
            for hh in range(hpb):
                m, l, acc = carry[hh]
                s = lax.dot_general(qs[hh], k_ref[0, hh, rows, :], nt_dims, preferred_element_type=F32)
                if masked:
                    s = jnp.where(mask, s, NEG_BIG)
                m_new = jnp.maximum(m, jnp.max(s, axis=-1, keepdims=True))
                alpha = jnp.exp2(m - m_new)
                p = jnp.exp2(s - m_new)
                l = alpha * l + jnp.sum(p, axis=-1, keepdims=True)
                acc = alpha * acc + jnp.dot(p.astype(BF16), v_ref[0, hh, rows, :], preferred_element_type=F32)
                out.append((m_new, l, acc))
            return tuple(out)

        init = tuple((jnp.full((tq, 1), NEG_BIG, F32), jnp.zeros((tq, 1), F32), jnp.zeros((tq, HEAD_PAD), F32))
                     for _ in range(hpb))
        carry = lax.fori_loop(0, qi, functools.partial(step, masked=False), init)
        carry = step(qi, carry, True)
        for pair in range(hpb // 2):
            (m0, l0, a0), (m1, l1, a1) = carry[2 * pair], carry[2 * pair + 1]
            o_ref[0, :, pair * HEAD_PAD:(pair + 1) * HEAD_PAD] = (a0 * (1.0 / l0) + a1 * (1.0 / l1)).astype(BF16)
            lse_ref[0, 2 * pair] = jnp.broadcast_to(m0 + jnp.log2(l0), (tq, HEAD_PAD))
            lse_ref[0, 2 * pair + 1] = jnp.broadcast_to(m1 + jnp.log2(l1), (tq, HEAD_PAD))

        pl.when(last)(functools.partial(_xchg_finish, copies))

    kv_spec = pl.BlockSpec((1, hpb, seq, HEAD_PAD), lambda b, hb, i: (b, hb, 0, 0))
    q_spec = pl.BlockSpec((1, hpb, tq, HEAD_PAD), lambda b, hb, i: (b, hb, i, 0))
    res = pl.pallas_call(
        body, name="attn_fwd", grid=grid,
        in_specs=[q_spec, kv_spec, kv_spec] + [HBM_SPEC] * n_buf,
        out_specs=[pl.BlockSpec((1, tq, hpb * V_HEAD_DIM), lambda b, hb, i: (b, i, hb)), q_spec] + [HBM_SPEC] * n_buf,
        out_shape=[jax.ShapeDtypeStruct((n_b, seq, N_HEADS * V_HEAD_DIM), BF16),
                   jax.ShapeDtypeStruct((n_b, N_HEADS, seq, HEAD_PAD), F32)] + _xchg_out_shapes(bufs),
        scratch_shapes=_xchg_scratch(n_buf),
        compiler_params=_params(("arbitrary", "arbitrary", "arbitrary")),
    )(q, k, v, *[a for a, _ in bufs])
    return res[0], res[1], res[2:]


def _attn_bwd(q, k, v, do, o, lse, bufs, n_b, seq):
    tq = min(256, seq)
    nq = seq // tq
    nt_dims = _DIMS["nt"]
    tn_dims = _DIMS["tn"]
    hpb = ATT_HEADS
    grid = (n_b, N_HEADS // hpb, nq)
    n_buf = len(bufs)
    gathers = [g for _, g in bufs]

    def body(q_ref, k_ref, v_ref, do_ref, o_ref, lse_ref, *rest):
        srcs, (dq_ref, dk_ref, dv_ref), dsts = rest[:n_buf], rest[n_buf:n_buf + 3], rest[n_buf + 3:2 * n_buf + 3]
        dk_acc, dv_acc = rest[2 * n_buf + 3:2 * n_buf + 5]
        copies = _xchg_copies(srcs, dsts, gathers, *rest[2 * n_buf + 5:])
        first, last = _grid_ends(grid)
        pl.when(first)(functools.partial(_xchg_start, copies))

        qi = pl.program_id(2)

        @pl.when(qi == 0)
        def _():
            dk_acc[...] = jnp.zeros_like(dk_acc)
            dv_acc[...] = jnp.zeros_like(dv_acc)

        mask = _chunk_mask(tq)
        lane = lax.broadcasted_iota(jnp.int32, (tq, HEAD_PAD), 1)
        qs, dos, deltas, lses = [], [], [], []
        for hh in range(hpb):
            cols = slice((hh // 2) * HEAD_PAD, (hh // 2 + 1) * HEAD_PAD)
            do_pair = do_ref[0, :, cols]
            prod = do_pair.astype(F32) * o_ref[0, :, cols].astype(F32)
            qs.append(q_ref[0, hh])
            dos.append(do_pair)
            deltas.append(jnp.sum(jnp.where(lane // V_HEAD_DIM == hh % 2, prod, 0.0), axis=-1, keepdims=True))
            lses.append(lse_ref[0, hh][:, :1])

        def step(j, dqs, masked):
            rows = pl.ds(pl.multiple_of(j * tq, tq), tq)
            out = []
            for hh in range(hpb):
                kj = k_ref[0, hh, rows, :]
                s = lax.dot_general(qs[hh], kj, nt_dims, preferred_element_type=F32)
                p = jnp.exp2(s - lses[hh])
                if masked:
                    p = jnp.where(mask, p, 0.0)
                dv_acc[hh, rows, :] += lax.dot_general(p.astype(BF16), dos[hh], tn_dims, preferred_element_type=F32)
                dp = lax.dot_general(dos[hh], v_ref[0, hh, rows, :], nt_dims, preferred_element_type=F32)
                ds = (p * (dp - deltas[hh])).astype(BF16)
                dk_acc[hh, rows, :] += lax.dot_general(ds, qs[hh], tn_dims, preferred_element_type=F32)
                out.append(dqs[hh] + jnp.dot(ds, kj, preferred_element_type=F32))
            return tuple(out)

        dqs = tuple(jnp.zeros((tq, HEAD_PAD), F32) for _ in range(hpb))
        dqs = lax.fori_loop(0, qi, functools.partial(step, masked=False), dqs)
        dqs = step(qi, dqs, True)
        for hh in range(hpb):
            dq_ref[0, hh] = dqs[hh].astype(BF16)

        @pl.when(qi == nq - 1)
        def _():
            dk_ref[0] = (dk_acc[...] * LN2).astype(BF16)
            dv_ref[0] = dv_acc[...].astype(BF16)

        pl.when(last)(functools.partial(_xchg_finish, copies))

    full_spec = pl.BlockSpec((1, hpb, seq, HEAD_PAD), lambda b, hb, i: (b, hb, 0, 0))
    q_spec = pl.BlockSpec((1, hpb, tq, HEAD_PAD), lambda b, hb, i: (b, hb, i, 0))
    o_spec = pl.BlockSpec((1, tq, hpb * V_HEAD_DIM), lambda b, hb, i: (b, i, hb))
    head_shape = jax.ShapeDtypeStruct((n_b, N_HEADS, seq, HEAD_PAD), BF16)
    res = pl.pallas_call(
        body, name="attn_bwd", grid=grid,
        in_specs=[q_spec, full_spec, full_spec, o_spec, o_spec, q_spec] + [HBM_SPEC] * n_buf,
        out_specs=[q_spec, full_spec, full_spec] + [HBM_SPEC] * n_buf,
        out_shape=[head_shape] * 3 + _xchg_out_shapes(bufs),
        scratch_shapes=[pltpu.VMEM((hpb, seq, HEAD_PAD), F32), pltpu.VMEM((hpb, seq, HEAD_PAD), F32)]
        + _xchg_scratch(n_buf),
        compiler_params=_params(("arbitrary", "arbitrary", "arbitrary")),
    )(q, k, v, do, o, lse, *[a for a, _ in bufs])
    return res[0], res[1], res[2], res[3:]


def _ln_silu(u1, g, b):
    mu = jnp.mean(u1, axis=-1, keepdims=True)
    uc = u1 - mu
    r = lax.rsqrt(jnp.mean(uc * uc, axis=-1, keepdims=True) + EPS)
    y = uc * r * g + b
    return y * _sigmoid(y)


def _conv_fill_glu(z_ref, u0_ref, seq, tile):
    u0_ref[0:CONV_HALO, :] = jnp.zeros((CONV_HALO, CONV_CH), F32)
    for t in range(seq // tile):
        zt = z_ref[0, t * tile:(t + 1) * tile, :].astype(F32)
        u0_ref[CONV_HALO + t * tile:CONV_HALO + (t + 1) * tile, :] = zt[:, :CONV_CH] * _sigmoid(zt[:, CONV_CH:])


def _conv_tile(u0_ref, w_ref, b_ref, t, tile):
    acc = jnp.broadcast_to(b_ref[...], (tile, CONV_CH))
    base = t * tile + CONV_HALO - (CONV_WIDTH - 1)
    for kk in range(CONV_WIDTH):
        acc = acc + w_ref[kk:kk + 1, :] * u0_ref[base + kk:base + kk + tile, :]
    return acc


def _conv_fwd(zglu, conv_w, conv_b, ln_g, ln_b, n_b, seq):
    tile = min(256, seq)

    def body(z_ref, w_ref, b_ref, g_ref, bb_ref, o_ref, u1_ref, u0_ref):
        _conv_fill_glu(z_ref, u0_ref, seq, tile)
        for t in range(seq // tile):
            u1 = _conv_tile(u0_ref, w_ref, b_ref, t, tile)
            u1_ref[0, t * tile:(t + 1) * tile, :] = u1
            o_ref[0, t * tile:(t + 1) * tile, :] = _ln_silu(u1, g_ref[...], bb_ref[...]).astype(BF16)

    whole2 = lambda arr: pl.BlockSpec(arr.shape, lambda b: (0, 0))
    seq_spec = pl.BlockSpec((1, seq, CONV_CH), lambda b: (b, 0, 0))
    return pl.pallas_call(
        body, name="conv_fwd", grid=(n_b,),
        in_specs=[pl.BlockSpec((1, seq, 2 * CONV_CH), lambda b: (b, 0, 0)), whole2(conv_w), whole2(conv_b),
                  whole2(ln_g), whole2(ln_b)],
        out_specs=[seq_spec, seq_spec],
        out_shape=[jax.ShapeDtypeStruct((n_b, seq, CONV_CH), BF16), jax.ShapeDtypeStruct((n_b, seq, CONV_CH), F32)],
        scratch_shapes=[pltpu.VMEM((seq + CONV_HALO, CONV_CH), F32)],
        compiler_params=_params(("parallel",)),
    )(zglu, conv_w, conv_b, ln_g, ln_b)


def _conv_bwd(zglu, u1_saved, du3, conv_w, ln_g, ln_b, n_b, seq):
    tile = min(256, seq)
    n_t = seq // tile

    def body(z_ref, u1_ref, du3_ref, w_ref, g_ref, bb_ref, dz_ref, dw_ref, db_ref, dg_ref, dbb_ref, u0_ref, du1_ref):
        @pl.when(pl.program_id(0) == 0)
        def _():
            for r in (dw_ref, db_ref, dg_ref, dbb_ref):
                r[...] = jnp.zeros_like(r)

        _conv_fill_glu(z_ref, u0_ref, seq, tile)
        du1_ref[seq:seq + CONV_HALO, :] = jnp.zeros((CONV_HALO, CONV_CH), F32)
        g = g_ref[...]
        for t in range(n_t):
            u1 = u1_ref[0, t * tile:(t + 1) * tile, :]
            mu = jnp.mean(u1, axis=-1, keepdims=True)
            uc = u1 - mu
            r = lax.rsqrt(jnp.mean(uc * uc, axis=-1, keepdims=True) + EPS)
            xh = uc * r
            y = xh * g + bb_ref[...]
            sg = _sigmoid(y)
            dy = du3_ref[0, t * tile:(t + 1) * tile, :].astype(F32) * (sg * (1.0 + y * (1.0 - sg)))
            dg_ref[...] += jnp.sum(dy * xh, axis=0, keepdims=True)
            dbb_ref[...] += jnp.sum(dy, axis=0, keepdims=True)
            dxh = dy * g
            du1 = r * (dxh - jnp.mean(dxh, axis=-1, keepdims=True) - xh * jnp.mean(dxh * xh, axis=-1, keepdims=True))
            db_ref[...] += jnp.sum(du1, axis=0, keepdims=True)
            du1_ref[t * tile:(t + 1) * tile, :] = du1
        for t in range(n_t):
            du1 = du1_ref[t * tile:(t + 1) * tile, :]
            du0 = jnp.zeros((tile, CONV_CH), F32)
            base_u = t * tile + CONV_HALO - (CONV_WIDTH - 1)
            base_d = t * tile + (CONV_WIDTH - 1)
            for kk in range(CONV_WIDTH):
                du0 = du0 + w_ref[kk:kk + 1, :] * du1_ref[base_d - kk:base_d - kk + tile, :]
                dw_ref[kk:kk + 1, :] += jnp.sum(du1 * u0_ref[base_u + kk:base_u + kk + tile, :], axis=0, keepdims=True)
            zt = z_ref[0, t * tile:(t + 1) * tile, :].astype(F32)
            ga, sb = zt[:, :CONV_CH], _sigmoid(zt[:, CONV_CH:])
            dz_ref[0, t * tile:(t + 1) * tile, :CONV_CH] = (du0 * sb).astype(BF16)
            dz_ref[0, t * tile:(t + 1) * tile, CONV_CH:] = (du0 * ga * sb * (1.0 - sb)).astype(BF16)

    whole2 = lambda arr: pl.BlockSpec(arr.shape, lambda b: (0, 0))
    z_spec = pl.BlockSpec((1, seq, 2 * CONV_CH), lambda b: (b, 0, 0))
    seq_spec = pl.BlockSpec((1, seq, CONV_CH), lambda b: (b, 0, 0))
    return pl.pallas_call(
        body, name="conv_bwd", grid=(n_b,),
        in_specs=[z_spec, seq_spec, seq_spec, whole2(conv_w), whole2(ln_g), whole2(ln_b)],
        out_specs=[z_spec, whole2(conv_w), whole2(ln_g), whole2(ln_g), whole2(ln_b)],
        out_shape=[jax.ShapeDtypeStruct((n_b, seq, 2 * CONV_CH), BF16), jax.ShapeDtypeStruct(conv_w.shape, F32),
                   jax.ShapeDtypeStruct(ln_g.shape, F32), jax.ShapeDtypeStruct(ln_g.shape, F32),
                   jax.ShapeDtypeStruct(ln_b.shape, F32)],
        scratch_shapes=[pltpu.VMEM((seq + CONV_HALO, CONV_CH), F32), pltpu.VMEM((seq + CONV_HALO, CONV_CH), F32)],
        compiler_params=_params(("arbitrary",)),
    )(zglu, u1_saved, du3, conv_w, ln_g, ln_b)


def _adamw(name, w, parts, m, v):
    n_parts = parts.shape[0]

    def body(w_ref, p_ref, m_ref, v_ref, g_ref, d_ref, nm_ref, nv_ref):
        gg = p_ref[0].astype(F32)
        for j in range(1, n_parts):
            gg = gg + p_ref[j].astype(F32)
        nm = ADAM_B1 * m_ref[...] + (1.0 - ADAM_B1) * gg
        nv = ADAM_B2 * v_ref[...] + (1.0 - ADAM_B2) * jnp.square(gg)
        m_hat = nm / (1.0 - ADAM_B1 ** ADAM_STEP)
        v_hat = nv / (1.0 - ADAM_B2 ** ADAM_STEP)
        g_ref[...] = gg
        d_ref[...] = -ADAM_LR * (m_hat / (jnp.sqrt(v_hat) + ADAM_EPS) + ADAM_WD * w_ref[...])
        nm_ref[...] = nm
        nv_ref[...] = nv

    shape = jax.ShapeDtypeStruct(w.shape, F32)
    return pl.pallas_call(body, name=name, out_shape=[shape] * 4, compiler_params=_params(None))(w, parts, m, v)


def _rope_tables(seq):
    inv_freq = ROPE_THETA ** (-jnp.arange(0, QK_ROPE_DIM, 2, dtype=F32) / QK_ROPE_DIM)
    ang = jnp.arange(seq, dtype=F32)[:, None] * inv_freq[None, :]
    cos, sin = jnp.cos(ang), jnp.sin(ang)
    half = QK_ROPE_DIM // 2
    z = lambda n: jnp.zeros((seq, n), F32)
    tail = HEAD_PAD - QK_HEAD_DIM
    cos_t = jnp.concatenate([jnp.ones((seq, QK_NOPE_DIM), F32), cos, cos, z(tail)], axis=1)
    sin_lo = jnp.concatenate([z(QK_NOPE_DIM), -sin, z(half), z(tail)], axis=1)
    sin_hi = jnp.concatenate([z(QK_NOPE_DIM), z(half), sin, z(tail)], axis=1)
    return cos_t, sin_lo, sin_hi


def _pad_lanes(v, width=HEAD_PAD):
    return jnp.pad(v, [(0, 0)] * (v.ndim - 1) + [(0, width - v.shape[-1])])


def _unstack_cols(s):
    return s.transpose(1, 0, 2).reshape(s.shape[1], N_DEV * s.shape[2])


def _stack_cols(g, dtype):
    rows, cols = g.shape
    return g.reshape(rows, N_DEV, cols // N_DEV).transpose(1, 0, 2).astype(dtype)


def kernel(x, c, w_ada, b_ada, norm1_g, w_in, q_latent_g, w_uq, kv_latent_g, w_ukv, qk_norm_q_g, qk_norm_k_g, w_o_mla, conv_w, conv_b, conv_ln_g, conv_ln_b, w_pw_out, w_out, norm2_g, w_ff1, w_ff2, loss_target, m_w_ada, m_b_ada, m_norm1_g, m_w_in, m_q_latent_g, m_w_uq, m_kv_latent_g, m_w_ukv, m_qk_norm_q_g, m_qk_norm_k_g, m_w_o_mla, m_conv_w, m_conv_b, m_conv_ln_g, m_conv_ln_b, m_w_pw_out, m_w_out, m_norm2_g, m_w_ff1, m_w_ff2, v_w_ada, v_b_ada, v_norm1_g, v_w_in, v_q_latent_g, v_w_uq, v_kv_latent_g, v_w_ukv, v_qk_norm_q_g, v_qk_norm_k_g, v_w_o_mla, v_conv_w, v_conv_b, v_conv_ln_g, v_conv_ln_b, v_w_pw_out, v_w_out, v_norm2_g, v_w_ff1, v_w_ff2):
    given = dict(locals())
    local = {n: given[n][0] for n in WEIGHTS}
    vec = {n: local[n].reshape(1, -1) for n in REPLICATED}
    bf = lambda n: local[n].astype(BF16)
    n_b, seq, d = x.shape
    n_rows = n_b * seq
    x2 = x.reshape(n_rows, d)
    t2 = loss_target.reshape(n_rows, d)
    rw = functools.partial(_rowwise, n_rows=n_rows, seq=seq)
    me = 4 * lax.axis_index("x") + 2 * lax.axis_index("y") + lax.axis_index("c")
    ada_cols = local["w_ada"].shape[1]

    c_all, w_in_s, w_uq_s, w_ukv_s, conv_w_s = _exchange(
        "gather_early", [(c, True), (bf("w_in"), True), (bf("w_uq"), True), (bf("w_ukv"), True), (local["conv_w"], True)])
    w_in_f = _unstack_cols(w_in_s)
    zeros = lambda n: jnp.zeros((d, n), BF16)
    w_sm = jnp.concatenate([w_in_f[:, :OFF_KV], zeros(QK_NOPE_DIM), w_in_f[:, OFF_KV:OFF_KR], zeros(HEAD_PAD - QK_HEAD_DIM)], axis=1)
    w_glu = w_in_f[:, OFF_KR:OFF_GLU]
    w_gate = w_in_f[:, OFF_GLU:]
    wuq = _pad_lanes(_unstack_cols(w_uq_s).reshape(Q_LORA, N_HEADS, QK_HEAD_DIM)).reshape(Q_LORA, N_HEADS * HEAD_PAD)
    wukv_f = _unstack_cols(w_ukv_s).reshape(KV_LORA, N_HEADS, QK_NOPE_DIM + V_HEAD_DIM)
    wv = wukv_f[:, :, QK_NOPE_DIM:]
    odd = (jnp.arange(N_HEADS) % 2 == 1)[None, :, None]
    wuv = jnp.where(odd, jnp.pad(wv, ((0, 0), (0, 0), (V_HEAD_DIM, 0))), jnp.pad(wv, ((0, 0), (0, 0), (0, V_HEAD_DIM))))
    wukv = jnp.concatenate([_pad_lanes(wukv_f[:, :, :QK_NOPE_DIM]), wuv], axis=1).reshape(KV_LORA, 2 * N_HEADS * HEAD_PAD)
    gqn = _pad_lanes(vec["qk_norm_q_g"])
    gkn = _pad_lanes(vec["qk_norm_k_g"])
    conv_w_f = jnp.pad(_unstack_cols(conv_w_s), ((0, 1), (0, 0)))
    rope = _rope_tables(seq)

    all_rows = N_DEV * n_b
    pad_rows = (-all_rows) % ROWS_PAD
    c_rows = jnp.pad(c_all.reshape(all_rows, d), ((0, pad_rows), (0, 0)))
    b_cols = lax.dynamic_slice(local["b_ada"], (me * ada_cols,), (ada_cols,))
    mod_cols = _mm("ada_fwd", c_rows, local["w_ada"], "nn", F32, a_fn=_silu, epi=lambda acc, b: acc + b,
                   epi_in=(jnp.broadcast_to(b_cols, (all_rows + pad_rows, ada_cols)),))
    (mod_s,) = _exchange("scatter_mod", [(mod_cols[:all_rows].reshape(N_DEV, n_b, ada_cols), False)])
    mod = mod_s.transpose(1, 0, 2).reshape(n_b, ADA_CHUNKS, 1, d)
    shift1, scale1, gate1, shift2, scale2, gate2 = [mod[:, i] for i in range(ADA_CHUNKS)]

    (h,) = rw("norm1_fwd", lambda r, b, cc: ([_norm_mod(r[0], cc[0], b[0], b[1])], [], []),
              rows=[_full(x2)], bats=[scale1, shift1], consts=[vec["norm1_g"]], outs=[(d, BF16)])
    zsm = _mm("in_proj_sm", h, w_sm, "nn", F32)
    zglu = _mm("in_proj_glu", h, w_glu, "nn", BF16)
    zgate = _mm("in_proj_gate", h, w_gate, "nn", BF16)
    q, k, v = _mla_prep_fwd(zsm, wuq, wukv, vec["q_latent_g"], vec["kv_latent_g"], gqn, gkn, rope, n_b, seq)
    attn, lse, (w_o_s, w_pw_s, w_out_s, w_ff1_s, w_ff2_s) = _attn_fwd(
        q, k, v, [(bf("w_o_mla"), True), (bf("w_pw_out"), True), (bf("w_out"), True), (bf("w_ff1"), True),
                  (bf("w_ff2"), True)], n_b, seq)
    w_o_f = _unstack_cols(w_o_s)
    w_pw_f = _unstack_cols(w_pw_s)
    w_out_f = w_out_s.reshape(d, d)
    w_ff2_f = w_ff2_s.reshape(N_DEV * w_ff2_s.shape[1], d)
    attn2 = attn.reshape(n_rows, N_HEADS * V_HEAD_DIM)
    u3, u1 = _conv_fwd(zglu.reshape(n_b, seq, 2 * CONV_CH), conv_w_f, vec["conv_b"], vec["conv_ln_g"], vec["conv_ln_b"], n_b, seq)
    u32 = u3.reshape(n_rows, CONV_CH)
    ya = _mm("mla_out", attn2, w_o_f, "nn", BF16)
    yb = _mm("conv_out", u32, w_pw_f, "nn", BF16)
    (mrg,) = rw("merge_fwd",
                lambda r, b, cc: ([_sigmoid(r[0].astype(F32)) * r[2].astype(F32) + _sigmoid(r[1].astype(F32)) * r[3].astype(F32)], [], []),
                rows=[(zgate, d, 0), (zgate, d, 1), _full(ya), _full(yb)], outs=[(d, BF16)])
    mixed = _mm("out_proj", mrg, w_out_f, "nn", BF16)

    def mid_fn(r, b, cc):
        x1_ = r[0] + b[0] * r[1].astype(F32)
        return [x1_, _norm_mod(x1_, cc[0], b[1], b[2])], [], []

    x1, h2 = rw("norm2_fwd", mid_fn, rows=[_full(x2), _full(mixed)], bats=[gate1, scale2, shift2],
                consts=[vec["norm2_g"]], outs=[(d, F32), (d, BF16)])
    a = _mm("ff1", h2, w_ff1_s, "nn", BF16, b_stacked=True)
    f = _mm("ff2", a, w_ff2_f, "nn", BF16, a_fn=_relu2)

    def loss_fn(r, b, cc):
        ff = r[1].astype(F32)
        err = r[0] + b[0] * ff - r[2]
        dy_ = err * (1.0 / d)
        sq = jnp.broadcast_to(jnp.sum(err * err, keepdims=True), (1, LANES))
        return [dy_, b[0] * dy_], [jnp.sum(dy_ * ff, axis=0, keepdims=True)], [sq]

    dy, df, dgate2, sq_err = rw("loss", loss_fn, rows=[_full(x1), _full(f), _full(t2)], bats=[gate2],
                                outs=[(d, F32), (d, BF16)], bat_outs=[d], tot_outs=[(1, LANES)])
    loss = lax.psum(sq_err[0, 0] * (0.5 / d), MESH_AXES)

    da = _mm("ff2_bwd", df, w_ff2_f, "nt", BF16, epi=lambda acc, av: acc * 2.0 * jnp.maximum(av, 0.0), epi_in=(a,))
    g_ff2 = _mm("ff2_dw", a, df, "tn", BF16, a_fn=_relu2)
    dh2 = _mm("ff1_bwd", da, w_ff1_s, "nt", F32, b_stacked=True)
    g_ff1_s = _mm("ff1_dw", h2, da, "tn", BF16, out_stacked=True)

    def mid_bwd(r, b, cc):
        dx, dsc, dsh, dg = _norm_mod_bwd(r[0], cc[0], b[0], r[1])
        dx1_ = r[2] + dx
        return [dx1_, b[1] * dx1_], [dsc, dsh, jnp.sum(dx1_ * r[3].astype(F32), axis=0, keepdims=True)], [dg]

    dx1, dmixed, dscale2, dshift2, dgate1, g_norm2 = rw(
        "norm2_bwd", mid_bwd, rows=[_full(x1), _full(dh2), _full(dy), _full(mixed)], bats=[scale2, gate1],
        consts=[vec["norm2_g"]], outs=[(d, F32), (d, BF16)], bat_outs=[d, d, d], tot_outs=[(1, d)])

    dmrg = _mm("out_proj_bwd", dmixed, w_out_f, "nt", BF16)
    g_out = _mm("out_proj_dw", mrg, dmixed, "tn", BF16)

    def merge_bwd(r, b, cc):
        dm, ya_, yb_ = r[0].astype(F32), r[3].astype(F32), r[4].astype(F32)
        sa, sb = _sigmoid(r[1].astype(F32)), _sigmoid(r[2].astype(F32))
        return [dm * ya_ * sa * (1.0 - sa), dm * yb_ * sb * (1.0 - sb), dm * sa, dm * sb], [], []

    dzga, dzgb, dya, dyb = rw("merge_bwd", merge_bwd,
                              rows=[_full(dmrg), (zgate, d, 0), (zgate, d, 1), _full(ya), _full(yb)],
                              outs=[(d, BF16)] * 4)
    dattn = _mm("mla_out_bwd", dya, w_o_f, "nt", BF16)
    g_o = _mm("mla_out_dw", attn2, dya, "tn", F32)
    du3 = _mm("conv_out_bwd", dyb, w_pw_f, "nt", BF16)
    g_pw = _mm("conv_out_dw", u32, dyb, "tn", F32)

    dzglu, g_conv_w, g_conv_b, g_ln_g, g_ln_b = _conv_bwd(
        zglu.reshape(n_b, seq, 2 * CONV_CH), u1, du3.reshape(n_b, seq, CONV_CH), conv_w_f, vec["conv_ln_g"],
        vec["conv_ln_b"], n_b, seq)
    dzglu = dzglu.reshape(n_rows, 2 * CONV_CH)

    dq, dk, dv, (p_ff2, p_ff1, p_out, p_pw, p_o) = _attn_bwd(
        q, k, v, dattn.reshape(n_b, seq, N_HEADS * V_HEAD_DIM), attn, lse,
        [(g_ff2.reshape(N_DEV, -1, d), False), (g_ff1_s, False), (g_out.reshape(N_DEV, -1, d), False),
         (_stack_cols(g_pw, BF16), False), (_stack_cols(g_o, BF16), False)], n_b, seq)
    dzsm, g_wuq, g_wukv, g_gq, g_gkv, g_gqn, g_gkn = _mla_prep_bwd(
        zsm, dq, dk, dv, wuq, wukv, vec["q_latent_g"], vec["kv_latent_g"], gqn, gkn, rope, n_b, seq)

    add = lambda acc, prev: acc + prev
    dh = _mm("in_proj_gate_bwd_a", dzga, w_gate[:, :d], "nt", F32)
    dh = _mm("in_proj_gate_bwd_b", dzgb, w_gate[:, d:], "nt", F32, epi=add, epi_in=(dh,))
    dh = _mm("in_proj_glu_bwd", dzglu, w_glu, "nt", F32, epi=add, epi_in=(dh,))
    dh = _mm("in_proj_sm_bwd", dzsm, w_sm, "nt", F32, epi=add, epi_in=(dh,))
    g_gate_a = _mm("in_proj_gate_dw_a", h, dzga, "tn", F32)
    g_gate_b = _mm("in_proj_gate_dw_b", h, dzgb, "tn", F32)
    g_glu = _mm("in_proj_glu_dw", h, dzglu, "tn", F32)
    g_sm = _mm("in_proj_sm_dw", h, dzsm, "tn", F32)

    def first_bwd(r, b, cc):
        dx, dsc, dsh, dg = _norm_mod_bwd(r[0], cc[0], b[0], r[1])
        return [r[2] + dx], [dsc, dsh], [dg]

    grad_x, dscale1, dshift1, g_norm1 = rw("norm1_bwd", first_bwd, rows=[_full(x2), _full(dh), _full(dx1)],
                                            bats=[scale1], consts=[vec["norm1_g"]], outs=[(d, F32)], bat_outs=[d, d],
                                            tot_outs=[(1, d)])

    dmod = jnp.concatenate([dshift1, dscale1, dgate1, dshift2, dscale2, dgate2], axis=1).reshape(n_b, N_DEV, ada_cols)
    g_in = jnp.concatenate([g_sm[:, :OFF_KV], g_sm[:, OFF_KV + QK_NOPE_DIM:OFF_KV + QK_NOPE_DIM + QK_ROPE_DIM], g_glu,
                            g_gate_a, g_gate_b], axis=1)
    g_uq = g_wuq.reshape(Q_LORA, N_HEADS, HEAD_PAD)[:, :, :QK_HEAD_DIM].reshape(Q_LORA, N_HEADS * QK_HEAD_DIM)
    g_wukv = g_wukv.reshape(KV_LORA, 2, N_HEADS, HEAD_PAD)
    g_v = jnp.where(odd, g_wukv[:, 1, :, V_HEAD_DIM:], g_wukv[:, 1, :, :V_HEAD_DIM])
    g_ukv = jnp.concatenate([g_wukv[:, 0, :, :QK_NOPE_DIM], g_v], axis=2).reshape(KV_LORA, -1)
    dmod_s, p_in, p_uq, p_ukv, p_conv_w = _exchange(
        "scatter_late", [(dmod.transpose(1, 0, 2), False), (_stack_cols(g_in, BF16), False),
                         (_stack_cols(g_uq, BF16), False), (_stack_cols(g_ukv, BF16), False),
                         (_stack_cols(g_conv_w[:CONV_WIDTH], F32), False)])
    dmod_rows = jnp.pad(dmod_s.reshape(all_rows, ada_cols), ((0, pad_rows), (0, 0)))
    g_ada = _mm("ada_dw", c_rows, dmod_rows, "tn", F32, a_fn=_silu)
    (g_b_cols,) = _rowwise("ada_db", lambda r, b, cc: ([], [], [jnp.sum(r[0], axis=0, keepdims=True)]),
                           all_rows + pad_rows, all_rows + pad_rows, rows=[_full(dmod_rows)], tot_outs=[(1, ada_cols)])

    partial_of = {"norm1_g": g_norm1, "q_latent_g": g_gq, "kv_latent_g": g_gkv, "qk_norm_q_g": g_gqn,
                  "qk_norm_k_g": g_gkn, "conv_b": g_conv_b, "conv_ln_g": g_ln_g, "conv_ln_b": g_ln_b, "norm2_g": g_norm2}
    names = [n for n in REPLICATED if n != "b_ada"]
    pieces = [_pad_lanes(partial_of[n], -(-partial_of[n].shape[1] // LANES) * LANES) for n in names] + [g_b_cols]
    widths = [p.shape[1] for p in pieces]
    small = jnp.concatenate(pieces, axis=1)
    small = _pad_lanes(small, -(-small.shape[1] // (8 * LANES)) * 8 * LANES).reshape(-1, LANES)
    (small_s,) = _exchange("gather_small_grads", [(small, True)])
    small_s = small_s.reshape(N_DEV, 1, -1)
    parts = {}
    off = 0
    for n, wd in zip(names, widths):
        parts[n] = small_s[:, :, off:off + vec[n].shape[1]]
        off += wd
    parts["b_ada"] = small_s[:, 0, off:off + ada_cols].reshape(1, 1, N_DEV * ada_cols)
    parts.update({"w_ada": g_ada[None], "w_in": p_in, "w_uq": p_uq, "w_ukv": p_ukv, "w_o_mla": p_o, "conv_w": p_conv_w,
                  "w_pw_out": p_pw, "w_out": p_out, "w_ff1": p_ff1, "w_ff2": p_ff2})

    grad_out, delta_out, m_out, v_out = [], [], [], []
    for n in WEIGHTS:
        shape2 = parts[n].shape[1:]
        g_w, d_w, n_m, n_v = _adamw("adamw_" + n, local[n].reshape(shape2), parts[n], given["m_" + n].reshape(shape2),
                                    given["v_" + n].reshape(shape2))
        full_shape = given[n].shape
        grad_out.append(g_w.reshape(full_shape))
        delta_out.append(d_w.reshape(full_shape))
        m_out.append(n_m.reshape(full_shape))
        v_out.append(n_v.reshape(full_shape))
    return (loss, grad_x.reshape(n_b, seq, d), *grad_out, *delta_out, *m_out, *v_out)
```

```python
import functools

import jax
import jax.numpy as jnp
from jax import lax
from jax.experimental import pallas as pl
from jax.experimental.pallas import tpu as pltpu

F32 = jnp.float32
BF16 = jnp.bfloat16

N_DEV = 8
MESH_AXES = ("x", "y", "c")
EPS = 1e-6
N_HEADS = 8
QK_HEAD_DIM = 96
QK_NOPE_DIM = 64
QK_ROPE_DIM = 32
V_HEAD_DIM = 64
HEAD_PAD = 128
Q_LORA = 256
KV_LORA = 128
CONV_CH = 512
CONV_WIDTH = 31
CONV_HALO = 32
CHUNK = 64
ROPE_THETA = 10000.0
OFF_Q = Q_LORA
OFF_KV = OFF_Q + KV_LORA
OFF_KR = OFF_KV + QK_ROPE_DIM
OFF_GLU = OFF_KR + 2 * CONV_CH
ADA_CHUNKS = 6
ADAM_LR = 0.001
ADAM_B1 = 0.9
ADAM_B2 = 0.999
ADAM_EPS = 1e-08
ADAM_WD = 0.01
ADAM_STEP = 10
LANES = 128
VMEM_LIMIT = 56 * 1024 * 1024
NEG_BIG = -1e30
ATT_HEADS = 4
ATT_TILE = 256
ATT_SCALE = QK_HEAD_DIM ** -0.5
LOG2E = 1.4426950408889634
LN2 = 0.6931471805599453
QK_SCALE = ATT_SCALE * LOG2E
ROWS_PAD = 16

REPLICATED = ("b_ada", "norm1_g", "q_latent_g", "kv_latent_g", "qk_norm_q_g", "qk_norm_k_g", "conv_b", "conv_ln_g",
              "conv_ln_b", "norm2_g")
WEIGHTS = ("w_ada", "b_ada", "norm1_g", "w_in", "q_latent_g", "w_uq", "kv_latent_g", "w_ukv", "qk_norm_q_g",
           "qk_norm_k_g", "w_o_mla", "conv_w", "conv_b", "conv_ln_g", "conv_ln_b", "w_pw_out", "w_out", "norm2_g",
           "w_ff1", "w_ff2")


def _tile(dim, pref):
    if dim <= pref:
        return dim
    t = (pref // LANES) * LANES
    while dim % t:
        t -= LANES
    return t


def _params(semantics):
    return pltpu.CompilerParams(dimension_semantics=semantics, vmem_limit_bytes=VMEM_LIMIT)


def _sigmoid(v):
    return 1.0 / (1.0 + jnp.exp(-v))


def _silu(v):
    return v * _sigmoid(v)


def _relu2(v):
    return jnp.square(jnp.maximum(v, 0.0))


_DIMS = {"nn": (((1,), (0,)), ((), ())), "nt": (((1,), (1,)), ((), ())), "tn": (((0,), (0,)), ((), ()))}


def _mm(name, a, b, mode, out_dtype, *, a_fn=None, epi=None, epi_in=(), b_stacked=False, out_stacked=False,
        tm=1024, tn=1024, tk=1024):
    if b_stacked:
        shard = b.shape[2]
        b_rows, b_cols = b.shape[1], N_DEV * shard
    else:
        b_rows, b_cols = b.shape
    if mode == "nn":
        (m, k), n = a.shape, b_cols
    elif mode == "nt":
        (m, k), n = a.shape, b_rows
    else:
        (k, m), n = a.shape, b_cols
    if out_stacked:
        shard = n // N_DEV
    tm = _tile(m, tm)
    tn = _tile(shard, tn) if (out_stacked or (b_stacked and mode != "nt")) else _tile(n, tn)
    tk = _tile(shard, tk) if (b_stacked and mode == "nt") else _tile(k, tk)
    nk = k // tk
    a_spec = (pl.BlockSpec((tk, tm), lambda i, j, kk: (kk, i)) if mode == "tn"
              else pl.BlockSpec((tm, tk), lambda i, j, kk: (i, kk)))
    if b_stacked and mode == "nt":
        per = shard // tk
        b_spec = pl.BlockSpec((None, tn, tk), lambda i, j, kk: (kk // per, j, kk % per))
    elif b_stacked:
        per = shard // tn
        b_spec = pl.BlockSpec((None, tk, tn), lambda i, j, kk: (j // per, kk, j % per))
    elif mode == "nt":
        b_spec = pl.BlockSpec((tn, tk), lambda i, j, kk: (j, kk))
    else:
        b_spec = pl.BlockSpec((tk, tn), lambda i, j, kk: (kk, j))
    e_spec = pl.BlockSpec((tm, tn), lambda i, j, kk: (i, j))
    if out_stacked:
        per_o = shard // tn
        o_spec = pl.BlockSpec((None, tm, tn), lambda i, j, kk: (j // per_o, i, j % per_o))
        out_shape = jax.ShapeDtypeStruct((N_DEV, m, shard), out_dtype)
    else:
        o_spec = e_spec
        out_shape = jax.ShapeDtypeStruct((m, n), out_dtype)
    n_epi = len(epi_in)

    def body(a_ref, b_ref, *rest):
        epi_refs, o_ref, acc_ref = rest[:n_epi], rest[n_epi], rest[n_epi + 1]
        kk = pl.program_id(2)

        @pl.when(kk == 0)
        def _():
            acc_ref[...] = jnp.zeros_like(acc_ref)

        av = a_ref[...]
        if a_fn is not None:
            av = a_fn(av.astype(F32))
        acc_ref[...] += lax.dot_general(av.astype(BF16), b_ref[...].astype(BF16), _DIMS[mode],
                                        preferred_element_type=F32)

        @pl.when(kk == nk - 1)
        def _():
            acc = acc_ref[...]
            if epi is not None:
                acc = epi(acc, *[r[...].astype(F32) for r in epi_refs])
            o_ref[...] = acc.astype(out_dtype)

    return pl.pallas_call(
        body, name=name, grid=(m // tm, n // tn, nk),
        in_specs=[a_spec, b_spec] + [e_spec] * n_epi, out_specs=o_spec, out_shape=out_shape,
        scratch_shapes=[pltpu.VMEM((tm, tn), F32)],
        compiler_params=_params(("parallel", "parallel", "arbitrary")),
    )(a, b, *epi_in)


def _rowwise(name, fn, n_rows, seq, rows, bats=(), consts=(), outs=(), bat_outs=(), tot_outs=(), tm=256):
    tm = min(tm, seq)
    per_seq = seq // tm
    n_b = n_rows // seq
    nr, nb, nc, no, nbo, nto = len(rows), len(bats), len(consts), len(outs), len(bat_outs), len(tot_outs)

    def body(*refs):
        i = pl.program_id(0)
        r_in = [r[...] for r in refs[:nr]]
        b_in = [r[0] for r in refs[nr:nr + nb]]
        c_in = [r[...] for r in refs[nr + nb:nr + nb + nc]]
        o_refs = refs[nr + nb + nc:nr + nb + nc + no]
        bo_refs = refs[nr + nb + nc + no:nr + nb + nc + no + nbo]
        to_refs = refs[nr + nb + nc + no + nbo:]
        o_val, bo_val, to_val = fn(r_in, b_in, c_in)
        for r, v in zip(o_refs, o_val):
            r[...] = v.astype(r.dtype)
        if nbo:
            @pl.when(i % per_seq == 0)
            def _():
                for r in bo_refs:
                    r[...] = jnp.zeros_like(r)

            for r, v in zip(bo_refs, bo_val):
                r[0] += v
        if nto:
            @pl.when(i == 0)
            def _():
                for r in to_refs:
                    r[...] = jnp.zeros_like(r)

            for r, v in zip(to_refs, to_val):
                r[...] += v

    in_specs = [pl.BlockSpec((tm, w), functools.partial(lambda cb, i: (i, cb), cb)) for (_, w, cb) in rows]
    in_specs += [pl.BlockSpec((1, 1, bt.shape[2]), lambda i: (i // per_seq, 0, 0)) for bt in bats]
    in_specs += [pl.BlockSpec(ct.shape, lambda i: (0, 0)) for ct in consts]
    out_specs = [pl.BlockSpec((tm, w), lambda i: (i, 0)) for (w, _) in outs]
    out_specs += [pl.BlockSpec((1, 1, w), lambda i: (i // per_seq, 0, 0)) for w in bat_outs]
    out_specs += [pl.BlockSpec(shp, lambda i: (0, 0)) for shp in tot_outs]
    out_shape = [jax.ShapeDtypeStruct((n_rows, w), dt) for (w, dt) in outs]
    out_shape += [jax.ShapeDtypeStruct((n_b, 1, w), F32) for w in bat_outs]
    out_shape += [jax.ShapeDtypeStruct(shp, F32) for shp in tot_outs]
    res = pl.pallas_call(
        body, name=name, grid=(n_rows // tm,), in_specs=in_specs, out_specs=out_specs, out_shape=out_shape,
        compiler_params=_params(("arbitrary",)),
    )(*[r[0] for r in rows], *bats, *consts)
    return res


def _full(arr):
    return (arr, arr.shape[1], 0)


def _norm_mod(x, g, scale, shift):
    r = lax.rsqrt(jnp.mean(x * x, axis=-1, keepdims=True) + EPS)
    xh = x * r
    return xh * g * (1.0 + scale) + shift


def _norm_mod_bwd(x, g, scale, dh):
    r = lax.rsqrt(jnp.mean(x * x, axis=-1, keepdims=True) + EPS)
    xh = x * r
    dn = dh * (1.0 + scale)
    dxh = dn * g
    dx = r * (dxh - xh * jnp.mean(dxh * xh, axis=-1, keepdims=True))
    dscale = jnp.sum(dh * xh * g, axis=0, keepdims=True)
    dshift = jnp.sum(dh, axis=0, keepdims=True)
    dg = jnp.sum(dn * xh, axis=0, keepdims=True)
    return dx, dscale, dshift, dg


def _rms(v, g):
    r = lax.rsqrt(jnp.mean(v * v, axis=-1, keepdims=True) + EPS)
    return v * r * g


def _rms_bwd(v, g, dy):
    r = lax.rsqrt(jnp.mean(v * v, axis=-1, keepdims=True) + EPS)
    vh = v * r
    dvh = dy * g
    dv = r * (dvh - vh * jnp.mean(dvh * vh, axis=-1, keepdims=True))
    return dv, jnp.sum(dy * vh, axis=0, keepdims=True)


def _head_norm(v, g):
    r = lax.rsqrt(jnp.sum(v * v, axis=-1, keepdims=True) * (1.0 / QK_HEAD_DIM) + EPS)
    return v * r * g


def _head_norm_bwd(v, g, dy):
    r = lax.rsqrt(jnp.sum(v * v, axis=-1, keepdims=True) * (1.0 / QK_HEAD_DIM) + EPS)
    vh = v * r
    dvh = dy * g
    dv = r * (dvh - vh * (jnp.sum(dvh * vh, axis=-1, keepdims=True) * (1.0 / QK_HEAD_DIM)))
    return dv, jnp.sum(dy * vh, axis=0, keepdims=True)


def _rope(v, cos, sin_lo, sin_hi):
    return v * cos + pltpu.roll(v, HEAD_PAD - 16, 1) * sin_lo + pltpu.roll(v, 16, 1) * sin_hi


def _rope_bwd(g, cos, sin_lo, sin_hi):
    return g * cos + pltpu.roll(g * sin_lo, 16, 1) + pltpu.roll(g * sin_hi, HEAD_PAD - 16, 1)


def _mla_prep_fwd(zsm, wuq, wukv, wuv_t, gq, gkv, gqn, gkn, rope, n_b, seq):
    n_rows = n_b * seq
    tm = min(ATT_TILE, seq)
    per_seq = seq // tm
    k_cols = N_HEADS * HEAD_PAD
    nt_dims = _DIMS["nt"]

    def body(z_ref, wuq_ref, wukv_ref, wuvt_ref, gq_ref, gkv_ref, gqn_ref, gkn_ref, c_ref, s1_ref, s2_ref,
             q_ref, k_ref, v_ref, kt_ref, vt_ref):
        z = z_ref[...]
        qn = _rms(z[:, :Q_LORA], gq_ref[...]).astype(BF16)
        kvn = _rms(z[:, Q_LORA:Q_LORA + KV_LORA], gkv_ref[...]).astype(BF16)
        krp = z[:, Q_LORA + KV_LORA:]
        cos, s1, s2 = c_ref[...], s1_ref[...], s2_ref[...]
        q_all = jnp.dot(qn, wuq_ref[...], preferred_element_type=F32)
        kv_all = jnp.dot(kvn, wukv_ref[...], preferred_element_type=F32)
        for h in range(N_HEADS):
            cols = slice(h * HEAD_PAD, (h + 1) * HEAD_PAD)
            q_ref[0, h] = (_rope(_head_norm(q_all[:, cols], gqn_ref[...]), cos, s1, s2) * QK_SCALE).astype(BF16)
            kh = _rope(_head_norm(kv_all[:, cols] + krp, gkn_ref[...]), cos, s1, s2)
            k_ref[0, h] = kh.astype(BF16)
            kt_ref[0, h, 0] = kh.T.astype(BF16)
            v_ref[0, h] = kv_all[:, k_cols + h * HEAD_PAD:k_cols + (h + 1) * HEAD_PAD].astype(BF16)
            vt_ref[0, h, 0] = lax.dot_general(wuvt_ref[h], kvn, nt_dims, preferred_element_type=F32).astype(BF16)

    whole2 = lambda arr: pl.BlockSpec(arr.shape, lambda i: (0, 0))
    rope_spec = pl.BlockSpec((tm, HEAD_PAD), lambda i: (i % per_seq, 0))
    head_spec = pl.BlockSpec((1, N_HEADS, tm, HEAD_PAD), lambda i: (i // per_seq, 0, i % per_seq, 0))
    head_shape = jax.ShapeDtypeStruct((n_b, N_HEADS, seq, HEAD_PAD), BF16)
    t_spec = pl.BlockSpec((1, N_HEADS, 1, HEAD_PAD, tm), lambda i: (i // per_seq, 0, i % per_seq, 0, 0))
    t_shape = jax.ShapeDtypeStruct((n_b, N_HEADS, per_seq, HEAD_PAD, tm), BF16)
    return pl.pallas_call(
        body, name="mla_prep_fwd", grid=(n_rows // tm,),
        in_specs=[pl.BlockSpec((tm, 512), lambda i: (i, 0)), whole2(wuq), whole2(wukv),
                  pl.BlockSpec(wuv_t.shape, lambda i: (0, 0, 0)),
                  whole2(gq), whole2(gkv), whole2(gqn), whole2(gkn), rope_spec, rope_spec, rope_spec],
        out_specs=[head_spec] * 3 + [t_spec] * 2, out_shape=[head_shape] * 3 + [t_shape] * 2,
        compiler_params=_params(("parallel",)),
    )(zsm, wuq, wukv, wuv_t, gq, gkv, gqn, gkn, *rope)


def _mla_prep_bwd(zsm, dq, dk, dv, wuq, wukv, gq, gkv, gqn, gkn, rope, n_b, seq):
    n_rows = n_b * seq
    tm = min(256, seq)
    per_seq = seq // tm
    tn_dims = _DIMS["tn"]
    nt_dims = _DIMS["nt"]
    k_cols = N_HEADS * HEAD_PAD

    def body(z_ref, dq_ref, dk_ref, dv_ref, wuq_ref, wukv_ref, gq_ref, gkv_ref, gqn_ref, gkn_ref,
             c_ref, s1_ref, s2_ref, dz_ref, dwuq_ref, dwukv_ref, dgq_ref, dgkv_ref, dgqn_ref, dgkn_ref):
        @pl.when(pl.program_id(0) == 0)
        def _():
            for r in (dwuq_ref, dwukv_ref, dgq_ref, dgkv_ref, dgqn_ref, dgkn_ref):
                r[...] = jnp.zeros_like(r)

        z = z_ref[...]
        zq, zkv, krp = z[:, :Q_LORA], z[:, Q_LORA:Q_LORA + KV_LORA], z[:, Q_LORA + KV_LORA:]
        qn = _rms(zq, gq_ref[...]).astype(BF16)
        kvn = _rms(zkv, gkv_ref[...]).astype(BF16)
        cos, s1, s2 = c_ref[...], s1_ref[...], s2_ref[...]
        lane = lax.broadcasted_iota(jnp.int32, (tm, HEAD_PAD), 1)
        rope_lanes = (lane >= QK_NOPE_DIM) & (lane < QK_HEAD_DIM)
        q_all = jnp.dot(qn, wuq_ref[...], preferred_element_type=F32)
        k_all = jnp.dot(kvn, wukv_ref[:, :k_cols], preferred_element_type=F32)
        dkrp = jnp.zeros((tm, HEAD_PAD), F32)
        dgqn = jnp.zeros((1, HEAD_PAD), F32)
        dgkn = jnp.zeros((1, HEAD_PAD), F32)
        dq_heads, dk_heads = [], []
        for h in range(N_HEADS):
            cols = slice(h * HEAD_PAD, (h + 1) * HEAD_PAD)
            dqh, dg = _head_norm_bwd(q_all[:, cols], gqn_ref[...],
                                     _rope_bwd(dq_ref[0, h].astype(F32) * ATT_SCALE, cos, s1, s2))
            dgqn += dg
            dq_heads.append(dqh.astype(BF16))
            dkh, dg = _head_norm_bwd(k_all[:, cols] + krp, gkn_ref[...], _rope_bwd(dk_ref[0, h].astype(F32), cos, s1, s2))
            dgkn += dg
            dkrp += jnp.where(rope_lanes, dkh, 0.0)
            dk_heads.append(dkh.astype(BF16))
        dq_all = jnp.concatenate(dq_heads, axis=1)
        dkv_all = jnp.concatenate(dk_heads + [dv_ref[0, h] for h in range(N_HEADS)], axis=1)
        dwuq_ref[...] += lax.dot_general(qn, dq_all, tn_dims, preferred_element_type=F32)
        dqn = lax.dot_general(dq_all, wuq_ref[...], nt_dims, preferred_element_type=F32)
        dwukv_ref[...] += lax.dot_general(kvn, dkv_all, tn_dims, preferred_element_type=F32)
        dkvn = lax.dot_general(dkv_all, wukv_ref[...], nt_dims, preferred_element_type=F32)
        dzq, dg = _rms_bwd(zq, gq_ref[...], dqn)
        dgq_ref[...] += dg
        dzkv, dg = _rms_bwd(zkv, gkv_ref[...], dkvn)
        dgkv_ref[...] += dg
        dgqn_ref[...] += dgqn
        dgkn_ref[...] += dgkn
        dz_ref[:, :Q_LORA] = dzq.astype(dz_ref.dtype)
        dz_ref[:, Q_LORA:Q_LORA + KV_LORA] = dzkv.astype(dz_ref.dtype)
        dz_ref[:, Q_LORA + KV_LORA:] = dkrp.astype(dz_ref.dtype)

    whole2 = lambda arr: pl.BlockSpec(arr.shape, lambda i: (0, 0))
    rope_spec = pl.BlockSpec((tm, HEAD_PAD), lambda i: (i % per_seq, 0))
    head_spec = pl.BlockSpec((1, N_HEADS, tm, HEAD_PAD), lambda i: (i // per_seq, 0, i % per_seq, 0))
    row_spec = pl.BlockSpec((tm, 512), lambda i: (i, 0))
    return pl.pallas_call(
        body, name="mla_prep_bwd", grid=(n_rows // tm,),
        in_specs=[row_spec, head_spec, head_spec, head_spec, whole2(wuq), whole2(wukv),
                  whole2(gq), whole2(gkv), whole2(gqn), whole2(gkn), rope_spec, rope_spec, rope_spec],
        out_specs=[row_spec, whole2(wuq), whole2(wukv), whole2(gq), whole2(gkv), whole2(gqn), whole2(gkn)],
        out_shape=[jax.ShapeDtypeStruct((n_rows, 512), BF16),
                   jax.ShapeDtypeStruct(wuq.shape, F32), jax.ShapeDtypeStruct(wukv.shape, F32),
                   jax.ShapeDtypeStruct(gq.shape, F32), jax.ShapeDtypeStruct(gkv.shape, F32),
                   jax.ShapeDtypeStruct(gqn.shape, F32), jax.ShapeDtypeStruct(gkn.shape, F32)],
        compiler_params=_params(("arbitrary",)),
    )(zsm, dq, dk, dv, wuq, wukv, gq, gkv, gqn, gkn, *rope)


HBM_SPEC = pl.BlockSpec(memory_space=pltpu.HBM)


def _xchg_out_shapes(bufs):
    return [jax.ShapeDtypeStruct((N_DEV,) + (a.shape if gather else a.shape[1:]), a.dtype) for a, gather in bufs]


def _xchg_scratch(n_buf):
    return [pltpu.SemaphoreType.DMA((n_buf * (N_DEV - 1),)), pltpu.SemaphoreType.DMA((n_buf * (N_DEV - 1),)),
            pltpu.SemaphoreType.DMA((n_buf,))]


def _xchg_copies(src_refs, dst_refs, gathers, send_sems, recv_sems, local_sems):
    x, y, c = lax.axis_index("x"), lax.axis_index("y"), lax.axis_index("c")
    me = 4 * x + 2 * y + c
    local, starts, arrivals = [], [], []
    for bi, (src, dst, gather) in enumerate(zip(src_refs, dst_refs, gathers)):
        local.append(pltpu.make_async_copy(src if gather else src.at[me], dst.at[me], local_sems.at[bi]))
        for kk in range(1, N_DEV):
            px = 1 - x if kk & 4 else x
            py = 1 - y if kk & 2 else y
            pc = 1 - c if kk & 1 else c
            pid = 4 * px + 2 * py + pc
            sem = bi * (N_DEV - 1) + kk - 1
            starts.append(pltpu.make_async_remote_copy(
                src_ref=src if gather else src.at[pid], dst_ref=dst.at[me],
                send_sem=send_sems.at[sem], recv_sem=recv_sems.at[sem],
                device_id=(px, py, pc), device_id_type=pl.DeviceIdType.MESH))
            arrivals.append(pltpu.make_async_remote_copy(
                src_ref=src if gather else src.at[me], dst_ref=dst.at[pid],
                send_sem=send_sems.at[sem], recv_sem=recv_sems.at[sem],
                device_id=(px, py, pc), device_id_type=pl.DeviceIdType.MESH))
    return local, starts, arrivals


def _xchg_start(copies):
    local, sends, _ = copies
    for cp in local + sends:
        cp.start()


def _xchg_finish(copies):
    local, sends, arrivals = copies
    for cp in arrivals:
        cp.wait_recv()
    for cp in sends:
        cp.wait_send()
    for cp in local:
        cp.wait()


def _exchange(name, bufs):
    n_buf = len(bufs)
    gathers = [g for _, g in bufs]

    def body(*refs):
        srcs, dsts = refs[:n_buf], refs[n_buf:2 * n_buf]
        copies = _xchg_copies(srcs, dsts, gathers, *refs[2 * n_buf:])
        _xchg_start(copies)
        _xchg_finish(copies)

    return pl.pallas_call(
        body, name=name, out_shape=_xchg_out_shapes(bufs),
        in_specs=[HBM_SPEC] * n_buf, out_specs=[HBM_SPEC] * n_buf, scratch_shapes=_xchg_scratch(n_buf),
    )(*[a for a, _ in bufs])


def _chunk_mask_t(t):
    key = lax.broadcasted_iota(jnp.int32, (t, t), 0) // CHUNK
    query = lax.broadcasted_iota(jnp.int32, (t, t), 1) // CHUNK
    return query >= key


def _grid_ends(grid):
    ids = [pl.program_id(ax) for ax in range(len(grid))]
    first = functools.reduce(jnp.logical_and, [i == 0 for i in ids])
    last = functools.reduce(jnp.logical_and, [i == g - 1 for i, g in zip(ids, grid)])
    return first, last


def _attn_fwd(q, k, vt, bufs, n_b, seq):
    tq = min(ATT_TILE, seq)
    nq = seq // tq
    nt_dims = _DIMS["nt"]
    hpb = ATT_HEADS
    grid = (n_b, N_HEADS // hpb, nq)
    n_buf = len(bufs)
    gathers = [g for _, g in bufs]

    def body(q_ref, k_ref, vt_ref, *rest):
        srcs, (o_ref, lse_ref), dsts = rest[:n_buf], rest[n_buf:n_buf + 2], rest[n_buf + 2:2 * n_buf + 2]
        copies = _xchg_copies(srcs, dsts, gathers, *rest[2 * n_buf + 2:])
        first, last = _grid_ends(grid)
        pl.when(first)(functools.partial(_xchg_start, copies))

        qi = pl.program_id(2)
        mask = _chunk_mask_t(tq)
        qs = [q_ref[0, hh] for hh in range(hpb)]

        def step(j, carry, masked):
            rows = pl.ds(pl.multiple_of(j * tq, tq), tq)
            out = []
            for hh in range(hpb):
                m, l, acc = carry[hh]
                s = lax.dot_general(k_ref[0, hh, rows, :], qs[hh], nt_dims, preferred_element_type=F32)
                if masked:
                    s = jnp.where(mask, s, NEG_BIG)
                m_new = jnp.maximum(m, jnp.max(s, axis=0, keepdims=True))
                alpha = jnp.exp2(m - m_new)
                p = jnp.exp2(s - m_new)
                l = alpha * l + jnp.sum(p, axis=0, keepdims=True)
                acc = alpha * acc + jnp.dot(vt_ref[0, hh, j], p.astype(BF16), preferred_element_type=F32)
                out.append((m_new, l, acc))
            return tuple(out)

        init = tuple((jnp.full((1, tq), NEG_BIG, F32), jnp.zeros((1, tq), F32), jnp.zeros((HEAD_PAD, tq), F32))
                     for _ in range(hpb))
        carry = lax.fori_loop(0, qi, functools.partial(step, masked=False), init)
        carry = step(qi, carry, True)
        for pair in range(hpb // 2):
            (m0, l0, a0), (m1, l1, a1) = carry[2 * pair], carry[2 * pair + 1]
            o_t = a0 * (1.0 / l0) + a1 * (1.0 / l1)
            o_ref[0, :, pair * HEAD_PAD:(pair + 1) * HEAD_PAD] = o_t.T.astype(BF16)
            lse_ref[0, 2 * pair] = jnp.broadcast_to(m0 + jnp.log2(l0), (8, tq))
            lse_ref[0, 2 * pair + 1] = jnp.broadcast_to(m1 + jnp.log2(l1), (8, tq))

        pl.when(last)(functools.partial(_xchg_finish, copies))

    k_spec = pl.BlockSpec((1, hpb, seq, HEAD_PAD), lambda b, hb, i: (b, hb, 0, 0))
    t_spec = pl.BlockSpec((1, hpb, nq, HEAD_PAD, tq), lambda b, hb, i: (b, hb, 0, 0, 0))
    q_spec = pl.BlockSpec((1, hpb, tq, HEAD_PAD), lambda b, hb, i: (b, hb, i, 0))
    res = pl.pallas_call(
        body, name="attn_fwd", grid=grid,
        in_specs=[q_spec, k_spec, t_spec] + [HBM_SPEC] * n_buf,
        out_specs=[pl.BlockSpec((1, tq, hpb * V_HEAD_DIM), lambda b, hb, i: (b, i, hb)),
                   pl.BlockSpec((1, hpb, 8, tq), lambda b, hb, i: (b, hb, 0, i))] + [HBM_SPEC] * n_buf,
        out_shape=[jax.ShapeDtypeStruct((n_b, seq, N_HEADS * V_HEAD_DIM), BF16),
                   jax.ShapeDtypeStruct((n_b, N_HEADS, 8, seq), F32)] + _xchg_out_shapes(bufs),
        scratch_shapes=_xchg_scratch(n_buf),
        compiler_params=_params(("arbitrary", "arbitrary", "arbitrary")),
    )(q, k, vt, *[a for a, _ in bufs])
    return res[0], res[1], res[2:]


def _attn_bwd(q, k, v, kt, do, o, lse, bufs, n_b, seq):
    tq = min(ATT_TILE, seq)
    nq = seq // tq
    nt_dims = _DIMS["nt"]
    hpb = ATT_HEADS
    grid = (n_b, N_HEADS // hpb, nq)
    n_buf = len(bufs)
    gathers = [g for _, g in bufs]

    def body(q_ref, k_ref, v_ref, kt_ref, do_ref, o_ref, lse_ref, *rest):
        srcs, (dq_ref, dk_ref, dv_ref), dsts = rest[:n_buf], rest[n_buf:n_buf + 3], rest[n_buf + 3:2 * n_buf + 3]
        dk_acc, dv_acc = rest[2 * n_buf + 3:2 * n_buf + 5]
        copies = _xchg_copies(srcs, dsts, gathers, *rest[2 * n_buf + 5:])
        first, last = _grid_ends(grid)
        pl.when(first)(functools.partial(_xchg_start, copies))

        qi = pl.program_id(2)

        @pl.when(qi == 0)
        def _():
            dk_acc[...] = jnp.zeros_like(dk_acc)
            dv_acc[...] = jnp.zeros_like(dv_acc)

        mask = _chunk_mask_t(tq)
        lane = lax.broadcasted_iota(jnp.int32, (tq, HEAD_PAD), 1)
        qs, dos, deltas, lses = [], [], [], []
        for hh in range(hpb):
            cols = slice((hh // 2) * HEAD_PAD, (hh // 2 + 1) * HEAD_PAD)
            do_pair = do_ref[0, :, cols]
            prod = do_pair.astype(F32) * o_ref[0, :, cols].astype(F32)
            delta = jnp.sum(jnp.where(lane // V_HEAD_DIM == hh % 2, prod, 0.0), axis=-1, keepdims=True)
            qs.append(q_ref[0, hh])
            dos.append(do_pair)
            deltas.append(jnp.broadcast_to(delta, (tq, HEAD_PAD)).T[0:1, :])
            lses.append(lse_ref[0, hh][0:1, :])

        def step(j, dqs, masked):
            rows = pl.ds(pl.multiple_of(j * tq, tq), tq)
            out = []
            for hh in range(hpb):
                s = lax.dot_general(k_ref[0, hh, rows, :], qs[hh], nt_dims, preferred_element_type=F32)
                p = jnp.exp2(s - lses[hh])
                if masked:
                    p = jnp.where(mask, p, 0.0)
                dv_acc[hh, rows, :] += jnp.dot(p.astype(BF16), dos[hh], preferred_element_type=F32)
                dp = lax.dot_general(v_ref[0, hh, rows, :], dos[hh], nt_dims, preferred_element_type=F32)
                ds = (p * (dp - deltas[hh])).astype(BF16)
                dk_acc[hh, rows, :] += jnp.dot(ds, qs[hh], preferred_element_type=F32)
                out.append(dqs[hh] + jnp.dot(kt_ref[0, hh, j], ds, preferred_element_type=F32))
            return tuple(out)

        dqs = tuple(jnp.zeros((HEAD_PAD, tq), F32) for _ in range(hpb))
        dqs = lax.fori_loop(0, qi, functools.partial(step, masked=False), dqs)
        dqs = step(qi, dqs, True)
        for hh in range(hpb):
            dq_ref[0, hh] = dqs[hh].T.astype(BF16)

        @pl.when(qi == nq - 1)
        def _():
            dk_ref[0] = (dk_acc[...] * LN2).astype(BF16)
            dv_ref[0] = dv_acc[...].astype(BF16)

        pl.when(last)(functools.partial(_xchg_finish, copies))

    full_spec = pl.BlockSpec((1, hpb, seq, HEAD_PAD), lambda b, hb, i: (b, hb, 0, 0))
    t_spec = pl.BlockSpec((1, hpb, nq, HEAD_PAD, tq), lambda b, hb, i: (b, hb, 0, 0, 0))
    q_spec = pl.BlockSpec((1, hpb, tq, HEAD_PAD), lambda b, hb, i: (b, hb, i, 0))
    o_spec = pl.BlockSpec((1, tq, hpb * V_HEAD_DIM), lambda b, hb, i: (b, i, hb))
    lse_spec = pl.BlockSpec((1, hpb, 8, tq), lambda b, hb, i: (b, hb, 0, i))
    head_shape = jax.ShapeDtypeStruct((n_b, N_HEADS, seq, HEAD_PAD), BF16)
    res = pl.pallas_call(
        body, name="attn_bwd", grid=grid,
        in_specs=[q_spec, full_spec, full_spec, t_spec, o_spec, o_spec, lse_spec] + [HBM_SPEC] * n_buf,
        out_specs=[q_spec, full_spec, full_spec] + [HBM_SPEC] * n_buf,
        out_shape=[head_shape] * 3 + _xchg_out_shapes(bufs),
        scratch_shapes=[pltpu.VMEM((hpb, seq, HEAD_PAD), F32), pltpu.VMEM((hpb, seq, HEAD_PAD), F32)]
        + _xchg_scratch(n_buf),
        compiler_params=_params(("arbitrary", "arbitrary", "arbitrary")),
    )(q, k, v, kt, do, o, lse, *[a for a, _ in bufs])
    return res[0], res[1], res[2], res[3:]


def _ln_silu(u1, g, b):
    mu = jnp.mean(u1, axis=-1, keepdims=True)
    uc = u1 - mu
    r = lax.rsqrt(jnp.mean(uc * uc, axis=-1, keepdims=True) + EPS)
    y = uc * r * g + b
    return y * _sigmoid(y)


def _conv_fill_glu(z_ref, u0_ref, seq, tile):
    u0_ref[0:CONV_HALO, :] = jnp.zeros((CONV_HALO, CONV_CH), F32)
    for t in range(seq // tile):
        zt = z_ref[0, t * tile:(t + 1) * tile, :].astype(F32)
        u0_ref[CONV_HALO + t * tile:CONV_HALO + (t + 1) * tile, :] = zt[:, :CONV_CH] * _sigmoid(zt[:, CONV_CH:])


def _conv_tile(u0_ref, w_ref, b_ref, t, tile):
    acc = jnp.broadcast_to(b_ref[...], (tile, CONV_CH))
    base = t * tile + CONV_HALO - (CONV_WIDTH - 1)
    for kk in range(CONV_WIDTH):
        acc = acc + w_ref[kk:kk + 1, :] * u0_ref[base + kk:base + kk + tile, :]
    return acc


def _conv_fwd(zglu, conv_w, conv_b, ln_g, ln_b, n_b, seq):
    tile = min(256, seq)

    def body(z_ref, w_ref, b_ref, g_ref, bb_ref, o_ref, u1_ref, u0_ref):
        _conv_fill_glu(z_ref, u0_ref, seq, tile)
        for t in range(seq // tile):
            u1 = _conv_tile(u0_ref, w_ref, b_ref, t, tile)
            u1_ref[0, t * tile:(t + 1) * tile, :] = u1
            o_ref[0, t * tile:(t + 1) * tile, :] = _ln_silu(u1, g_ref[...], bb_ref[...]).astype(BF16)

    whole2 = lambda arr: pl.BlockSpec(arr.shape, lambda b: (0, 0))
    seq_spec = pl.BlockSpec((1, seq, CONV_CH), lambda b: (b, 0, 0))
    return pl.pallas_call(
        body, name="conv_fwd", grid=(n_b,),
        in_specs=[pl.BlockSpec((1, seq, 2 * CONV_CH), lambda b: (b, 0, 0)), whole2(conv_w), whole2(conv_b),
                  whole2(ln_g), whole2(ln_b)],
        out_specs=[seq_spec, seq_spec],
        out_shape=[jax.ShapeDtypeStruct((n_b, seq, CONV_CH), BF16), jax.ShapeDtypeStruct((n_b, seq, CONV_CH), F32)],
        scratch_shapes=[pltpu.VMEM((seq + CONV_HALO, CONV_CH), F32)],
        compiler_params=_params(("parallel",)),
    )(zglu, conv_w, conv_b, ln_g, ln_b)


def _conv_bwd(zglu, u1_saved, du3, conv_w, ln_g, ln_b, n_b, seq):
    tile = min(256, seq)
    n_t = seq // tile

    def body(z_ref, u1_ref, du3_ref, w_ref, g_ref, bb_ref, dz_ref, dw_ref, db_ref, dg_ref, dbb_ref, u0_ref, du1_ref):
        @pl.when(pl.program_id(0) == 0)
        def _():
            for r in (dw_ref, db_ref, dg_ref, dbb_ref):
                r[...] = jnp.zeros_like(r)

        _conv_fill_glu(z_ref, u0_ref, seq, tile)
        du1_ref[seq:seq + CONV_HALO, :] = jnp.zeros((CONV_HALO, CONV_CH), F32)
        g = g_ref[...]
        for t in range(n_t):
            u1 = u1_ref[0, t * tile:(t + 1) * tile, :]
            mu = jnp.mean(u1, axis=-1, keepdims=True)
            uc = u1 - mu
            r = lax.rsqrt(jnp.mean(uc * uc, axis=-1, keepdims=True) + EPS)
            xh = uc * r
            y = xh * g + bb_ref[...]
            sg = _sigmoid(y)
            dy = du3_ref[0, t * tile:(t + 1) * tile, :].astype(F32) * (sg * (1.0 + y * (1.0 - sg)))
            dg_ref[...] += jnp.sum(dy * xh, axis=0, keepdims=True)
            dbb_ref[...] += jnp.sum(dy, axis=0, keepdims=True)
            dxh = dy * g
            du1 = r * (dxh - jnp.mean(dxh, axis=-1, keepdims=True) - xh * jnp.mean(dxh * xh, axis=-1, keepdims=True))
            db_ref[...] += jnp.sum(du1, axis=0, keepdims=True)
            du1_ref[t * tile:(t + 1) * tile, :] = du1
        for t in range(n_t):
            du1 = du1_ref[t * tile:(t + 1) * tile, :]
            du0 = jnp.zeros((tile, CONV_CH), F32)
            base_u = t * tile + CONV_HALO - (CONV_WIDTH - 1)
            base_d = t * tile + (CONV_WIDTH - 1)
            for kk in range(CONV_WIDTH):
                du0 = du0 + w_ref[kk:kk + 1, :] * du1_ref[base_d - kk:base_d - kk + tile, :]
                dw_ref[kk:kk + 1, :] += jnp.sum(du1 * u0_ref[base_u + kk:base_u + kk + tile, :], axis=0, keepdims=True)
            zt = z_ref[0, t * tile:(t + 1) * tile, :].astype(F32)
            ga, sb = zt[:, :CONV_CH], _sigmoid(zt[:, CONV_CH:])
            dz_ref[0, t * tile:(t + 1) * tile, :CONV_CH] = (du0 * sb).astype(BF16)
            dz_ref[0, t * tile:(t + 1) * tile, CONV_CH:] = (du0 * ga * sb * (1.0 - sb)).astype(BF16)

    whole2 = lambda arr: pl.BlockSpec(arr.shape, lambda b: (0, 0))
    z_spec = pl.BlockSpec((1, seq, 2 * CONV_CH), lambda b: (b, 0, 0))
    seq_spec = pl.BlockSpec((1, seq, CONV_CH), lambda b: (b, 0, 0))
    return pl.pallas_call(
        body, name="conv_bwd", grid=(n_b,),
        in_specs=[z_spec, seq_spec, seq_spec, whole2(conv_w), whole2(ln_g), whole2(ln_b)],
        out_specs=[z_spec, whole2(conv_w), whole2(ln_g), whole2(ln_g), whole2(ln_b)],
        out_shape=[jax.ShapeDtypeStruct((n_b, seq, 2 * CONV_CH), BF16), jax.ShapeDtypeStruct(conv_w.shape, F32),
                   jax.ShapeDtypeStruct(ln_g.shape, F32), jax.ShapeDtypeStruct(ln_g.shape, F32),
                   jax.ShapeDtypeStruct(ln_b.shape, F32)],
        scratch_shapes=[pltpu.VMEM((seq + CONV_HALO, CONV_CH), F32), pltpu.VMEM((seq + CONV_HALO, CONV_CH), F32)],
        compiler_params=_params(("arbitrary",)),
    )(zglu, u1_saved, du3, conv_w, ln_g, ln_b)


def _adamw(name, w, parts, m, v):
    n_parts = parts.shape[0]

    def body(w_ref, p_ref, m_ref, v_ref, g_ref, d_ref, nm_ref, nv_ref):
        gg = p_ref[0].astype(F32)
        for j in range(1, n_parts):
            gg = gg + p_ref[j].astype(F32)
        nm = ADAM_B1 * m_ref[...] + (1.0 - ADAM_B1) * gg
        nv = ADAM_B2 * v_ref[...] + (1.0 - ADAM_B2) * jnp.square(gg)
        m_hat = nm / (1.0 - ADAM_B1 ** ADAM_STEP)
        v_hat = nv / (1.0 - ADAM_B2 ** ADAM_STEP)
        g_ref[...] = gg
        d_ref[...] = -ADAM_LR * (m_hat / (jnp.sqrt(v_hat) + ADAM_EPS) + ADAM_WD * w_ref[...])
        nm_ref[...] = nm
        nv_ref[...] = nv

    shape = jax.ShapeDtypeStruct(w.shape, F32)
    return pl.pallas_call(body, name=name, out_shape=[shape] * 4, compiler_params=_params(None))(w, parts, m, v)


def _rope_tables(seq):
    inv_freq = ROPE_THETA ** (-jnp.arange(0, QK_ROPE_DIM, 2, dtype=F32) / QK_ROPE_DIM)
    ang = jnp.arange(seq, dtype=F32)[:, None] * inv_freq[None, :]
    cos, sin = jnp.cos(ang), jnp.sin(ang)
    half = QK_ROPE_DIM // 2
    z = lambda n: jnp.zeros((seq, n), F32)
    tail = HEAD_PAD - QK_HEAD_DIM
    cos_t = jnp.concatenate([jnp.ones((seq, QK_NOPE_DIM), F32), cos, cos, z(tail)], axis=1)
    sin_lo = jnp.concatenate([z(QK_NOPE_DIM), -sin, z(half), z(tail)], axis=1)
    sin_hi = jnp.concatenate([z(QK_NOPE_DIM), z(half), sin, z(tail)], axis=1)
    return cos_t, sin_lo, sin_hi


def _pad_lanes(v, width=HEAD_PAD):
    return jnp.pad(v, [(0, 0)] * (v.ndim - 1) + [(0, width - v.shape[-1])])


def _unstack_cols(s):
    return s.transpose(1, 0, 2).reshape(s.shape[1], N_DEV * s.shape[2])


def _stack_cols(g, dtype):
    rows, cols = g.shape
    return g.reshape(rows, N_DEV, cols // N_DEV).transpose(1, 0, 2).astype(dtype)


def kernel(x, c, w_ada, b_ada, norm1_g, w_in, q_latent_g, w_uq, kv_latent_g, w_ukv, qk_norm_q_g, qk_norm_k_g, w_o_mla, conv_w, conv_b, conv_ln_g, conv_ln_b, w_pw_out, w_out, norm2_g, w_ff1, w_ff2, loss_target, m_w_ada, m_b_ada, m_norm1_g, m_w_in, m_q_latent_g, m_w_uq, m_kv_latent_g, m_w_ukv, m_qk_norm_q_g, m_qk_norm_k_g, m_w_o_mla, m_conv_w, m_conv_b, m_conv_ln_g, m_conv_ln_b, m_w_pw_out, m_w_out, m_norm2_g, m_w_ff1, m_w_ff2, v_w_ada, v_b_ada, v_norm1_g, v_w_in, v_q_latent_g, v_w_uq, v_kv_latent_g, v_w_ukv, v_qk_norm_q_g, v_qk_norm_k_g, v_w_o_mla, v_conv_w, v_conv_b, v_conv_ln_g, v_conv_ln_b, v_w_pw_out, v_w_out, v_norm2_g, v_w_ff1, v_w_ff2):
    given = dict(locals())
    local = {n: given[n][0] for n in WEIGHTS}
    vec = {n: local[n].reshape(1, -1) for n in REPLICATED}
    bf = lambda n: local[n].astype(BF16)
    n_b, seq, d = x.shape
    n_rows = n_b * seq
    x2 = x.reshape(n_rows, d)
    t2 = loss_target.reshape(n_rows, d)
    rw = functools.partial(_rowwise, n_rows=n_rows, seq=seq)
    me = 4 * lax.axis_index("x") + 2 * lax.axis_index("y") + lax.axis_index("c")
    ada_cols = local["w_ada"].shape[1]

    c_all, w_in_s, w_uq_s, w_ukv_s, conv_w_s = _exchange(
        "gather_early", [(c, True), (bf("w_in"), True), (bf("w_uq"), True), (bf("w_ukv"), True), (local["conv_w"], True)])
    w_in_f = _unstack_cols(w_in_s)
    zeros = lambda n: jnp.zeros((d, n), BF16)
    w_sm = jnp.concatenate([w_in_f[:, :OFF_KV], zeros(QK_NOPE_DIM), w_in_f[:, OFF_KV:OFF_KR], zeros(HEAD_PAD - QK_HEAD_DIM)], axis=1)
    w_glu = w_in_f[:, OFF_KR:OFF_GLU]
    w_gate = w_in_f[:, OFF_GLU:]
    wuq = _pad_lanes(_unstack_cols(w_uq_s).reshape(Q_LORA, N_HEADS, QK_HEAD_DIM)).reshape(Q_LORA, N_HEADS * HEAD_PAD)
    wukv_f = _unstack_cols(w_ukv_s).reshape(KV_LORA, N_HEADS, QK_NOPE_DIM + V_HEAD_DIM)
    wv = wukv_f[:, :, QK_NOPE_DIM:]
    odd = (jnp.arange(N_HEADS) % 2 == 1)[None, :, None]
    wuv = jnp.where(odd, jnp.pad(wv, ((0, 0), (0, 0), (V_HEAD_DIM, 0))), jnp.pad(wv, ((0, 0), (0, 0), (0, V_HEAD_DIM))))
    wukv = jnp.concatenate([_pad_lanes(wukv_f[:, :, :QK_NOPE_DIM]), wuv], axis=1).reshape(KV_LORA, 2 * N_HEADS * HEAD_PAD)
    gqn = _pad_lanes(vec["qk_norm_q_g"])
    gkn = _pad_lanes(vec["qk_norm_k_g"])
    conv_w_f = jnp.pad(_unstack_cols(conv_w_s), ((0, 1), (0, 0)))
    rope = _rope_tables(seq)

    all_rows = N_DEV * n_b
    pad_rows = (-all_rows) % ROWS_PAD
    c_rows = jnp.pad(c_all.reshape(all_rows, d), ((0, pad_rows), (0, 0)))
    b_cols = lax.dynamic_slice(local["b_ada"], (me * ada_cols,), (ada_cols,))
    mod_cols = _mm("ada_fwd", c_rows, local["w_ada"], "nn", F32, a_fn=_silu, epi=lambda acc, b: acc + b,
                   epi_in=(jnp.broadcast_to(b_cols, (all_rows + pad_rows, ada_cols)),))
    (mod_s,) = _exchange("scatter_mod", [(mod_cols[:all_rows].reshape(N_DEV, n_b, ada_cols), False)])
    mod = mod_s.transpose(1, 0, 2).reshape(n_b, ADA_CHUNKS, 1, d)
    shift1, scale1, gate1, shift2, scale2, gate2 = [mod[:, i] for i in range(ADA_CHUNKS)]

    (h,) = rw("norm1_fwd", lambda r, b, cc: ([_norm_mod(r[0], cc[0], b[0], b[1])], [], []),
              rows=[_full(x2)], bats=[scale1, shift1], consts=[vec["norm1_g"]], outs=[(d, BF16)])
    zsm = _mm("in_proj_sm", h, w_sm, "nn", F32)
    zglu = _mm("in_proj_glu", h, w_glu, "nn", BF16)
    zgate = _mm("in_proj_gate", h, w_gate, "nn", BF16)
    q, k, v, kt, vt = _mla_prep_fwd(zsm, wuq, wukv, wuv.transpose(1, 2, 0), vec["q_latent_g"], vec["kv_latent_g"],
                                    gqn, gkn, rope, n_b, seq)
    attn, lse, (w_o_s, w_pw_s, w_out_s, w_ff1_s, w_ff2_s) = _attn_fwd(
        q, k, vt, [(bf("w_o_mla"), True), (bf("w_pw_out"), True), (bf("w_out"), True), (bf("w_ff1"), True),
                  (bf("w_ff2"), True)], n_b, seq)
    w_o_f = _unstack_cols(w_o_s)
    w_pw_f = _unstack_cols(w_pw_s)
    w_out_f = w_out_s.reshape(d, d)
    w_ff2_f = w_ff2_s.reshape(N_DEV * w_ff2_s.shape[1], d)
    attn2 = attn.reshape(n_rows, N_HEADS * V_HEAD_DIM)
    u3, u1 = _conv_fwd(zglu.reshape(n_b, seq, 2 * CONV_CH), conv_w_f, vec["conv_b"], vec["conv_ln_g"], vec["conv_ln_b"], n_b, seq)
    u32 = u3.reshape(n_rows, CONV_CH)
    ya = _mm("mla_out", attn2, w_o_f, "nn", BF16)
    yb = _mm("conv_out", u32, w_pw_f, "nn", BF16)
    (mrg,) = rw("merge_fwd",
                lambda r, b, cc: ([_sigmoid(r[0].astype(F32)) * r[2].astype(F32) + _sigmoid(r[1].astype(F32)) * r[3].astype(F32)], [], []),
                rows=[(zgate, d, 0), (zgate, d, 1), _full(ya), _full(yb)], outs=[(d, BF16)])
    mixed = _mm("out_proj", mrg, w_out_f, "nn", BF16)

    def mid_fn(r, b, cc):
        x1_ = r[0] + b[0] * r[1].astype(F32)
        return [x1_, _norm_mod(x1_, cc[0], b[1], b[2])], [], []

    x1, h2 = rw("norm2_fwd", mid_fn, rows=[_full(x2), _full(mixed)], bats=[gate1, scale2, shift2],
                consts=[vec["norm2_g"]], outs=[(d, F32), (d, BF16)])
    a = _mm("ff1", h2, w_ff1_s, "nn", BF16, b_stacked=True)
    f = _mm("ff2", a, w_ff2_f, "nn", BF16, a_fn=_relu2)

    def loss_fn(r, b, cc):
        ff = r[1].astype(F32)
        err = r[0] + b[0] * ff - r[2]
        dy_ = err * (1.0 / d)
        sq = jnp.broadcast_to(jnp.sum(err * err, keepdims=True), (1, LANES))
        return [dy_, b[0] * dy_], [jnp.sum(dy_ * ff, axis=0, keepdims=True)], [sq]

    dy, df, dgate2, sq_err = rw("loss", loss_fn, rows=[_full(x1), _full(f), _full(t2)], bats=[gate2],
                                outs=[(d, F32), (d, BF16)], bat_outs=[d], tot_outs=[(1, LANES)])
    loss = lax.psum(sq_err[0, 0] * (0.5 / d), MESH_AXES)

    da = _mm("ff2_bwd", df, w_ff2_f, "nt", BF16, epi=lambda acc, av: acc * 2.0 * jnp.maximum(av, 0.0), epi_in=(a,))
    g_ff2 = _mm("ff2_dw", a, df, "tn", BF16, a_fn=_relu2)
    dh2 = _mm("ff1_bwd", da, w_ff1_s, "nt", F32, b_stacked=True)
    g_ff1_s = _mm("ff1_dw", h2, da, "tn", BF16, out_stacked=True)

    def mid_bwd(r, b, cc):
        dx, dsc, dsh, dg = _norm_mod_bwd(r[0], cc[0], b[0], r[1])
        dx1_ = r[2] + dx
        return [dx1_, b[1] * dx1_], [dsc, dsh, jnp.sum(dx1_ * r[3].astype(F32), axis=0, keepdims=True)], [dg]

    dx1, dmixed, dscale2, dshift2, dgate1, g_norm2 = rw(
        "norm2_bwd", mid_bwd, rows=[_full(x1), _full(dh2), _full(dy), _full(mixed)], bats=[scale2, gate1],
        consts=[vec["norm2_g"]], outs=[(d, F32), (d, BF16)], bat_outs=[d, d, d], tot_outs=[(1, d)])

    dmrg = _mm("out_proj_bwd", dmixed, w_out_f, "nt", BF16)
    g_out = _mm("out_proj_dw", mrg, dmixed, "tn", BF16)

    def merge_bwd(r, b, cc):
        dm, ya_, yb_ = r[0].astype(F32), r[3].astype(F32), r[4].astype(F32)
        sa, sb = _sigmoid(r[1].astype(F32)), _sigmoid(r[2].astype(F32))
        return [dm * ya_ * sa * (1.0 - sa), dm * yb_ * sb * (1.0 - sb), dm * sa, dm * sb], [], []

    dzga, dzgb, dya, dyb = rw("merge_bwd", merge_bwd,
                              rows=[_full(dmrg), (zgate, d, 0), (zgate, d, 1), _full(ya), _full(yb)],
                              outs=[(d, BF16)] * 4)
    dattn = _mm("mla_out_bwd", dya, w_o_f, "nt", BF16)
    g_o = _mm("mla_out_dw", attn2, dya, "tn", F32)
    du3 = _mm("conv_out_bwd", dyb, w_pw_f, "nt", BF16)
    g_pw = _mm("conv_out_dw", u32, dyb, "tn", F32)

    dzglu, g_conv_w, g_conv_b, g_ln_g, g_ln_b = _conv_bwd(
        zglu.reshape(n_b, seq, 2 * CONV_CH), u1, du3.reshape(n_b, seq, CONV_CH), conv_w_f, vec["conv_ln_g"],
        vec["conv_ln_b"], n_b, seq)
    dzglu = dzglu.reshape(n_rows, 2 * CONV_CH)

    dq, dk, dv, (p_ff2, p_ff1, p_out, p_pw, p_o) = _attn_bwd(
        q, k, v, kt, dattn.reshape(n_b, seq, N_HEADS * V_HEAD_DIM), attn, lse,
        [(g_ff2.reshape(N_DEV, -1, d), False), (g_ff1_s, False), (g_out.reshape(N_DEV, -1, d), False),
         (_stack_cols(g_pw, BF16), False), (_stack_cols(g_o, BF16), False)], n_b, seq)
    dzsm, g_wuq, g_wukv, g_gq, g_gkv, g_gqn, g_gkn = _mla_prep_bwd(
        zsm, dq, dk, dv, wuq, wukv, vec["q_latent_g"], vec["kv_latent_g"], gqn, gkn, rope, n_b, seq)

    add = lambda acc, prev: acc + prev
    dh = _mm("in_proj_gate_bwd_a", dzga, w_gate[:, :d], "nt", F32)
    dh = _mm("in_proj_gate_bwd_b", dzgb, w_gate[:, d:], "nt", F32, epi=add, epi_in=(dh,))
    dh = _mm("in_proj_glu_bwd", dzglu, w_glu, "nt", F32, epi=add, epi_in=(dh,))
    dh = _mm("in_proj_sm_bwd", dzsm, w_sm, "nt", F32, epi=add, epi_in=(dh,))
    g_gate_a = _mm("in_proj_gate_dw_a", h, dzga, "tn", F32)
    g_gate_b = _mm("in_proj_gate_dw_b", h, dzgb, "tn", F32)
    g_glu = _mm("in_proj_glu_dw", h, dzglu, "tn", F32)
    g_sm = _mm("in_proj_sm_dw", h, dzsm, "tn", F32)

    def first_bwd(r, b, cc):
        dx, dsc, dsh, dg = _norm_mod_bwd(r[0], cc[0], b[0], r[1])
        return [r[2] + dx], [dsc, dsh], [dg]

    grad_x, dscale1, dshift1, g_norm1 = rw("norm1_bwd", first_bwd, rows=[_full(x2), _full(dh), _full(dx1)],
                                            bats=[scale1], consts=[vec["norm1_g"]], outs=[(d, F32)], bat_outs=[d, d],
                                            tot_outs=[(1, d)])

    dmod = jnp.concatenate([dshift1, dscale1, dgate1, dshift2, dscale2, dgate2], axis=1).reshape(n_b, N_DEV, ada_cols)
    g_in = jnp.concatenate([g_sm[:, :OFF_KV], g_sm[:, OFF_KV + QK_NOPE_DIM:OFF_KV + QK_NOPE_DIM + QK_ROPE_DIM], g_glu,
                            g_gate_a, g_gate_b], axis=1)
    g_uq = g_wuq.reshape(Q_LORA, N_HEADS, HEAD_PAD)[:, :, :QK_HEAD_DIM].reshape(Q_LORA, N_HEADS * QK_HEAD_DIM)
    g_wukv = g_wukv.reshape(KV_LORA, 2, N_HEADS, HEAD_PAD)
    g_v = jnp.where(odd, g_wukv[:, 1, :, V_HEAD_DIM:], g_wukv[:, 1, :, :V_HEAD_DIM])
    g_ukv = jnp.concatenate([g_wukv[:, 0, :, :QK_NOPE_DIM], g_v], axis=2).reshape(KV_LORA, -1)
    dmod_s, p_in, p_uq, p_ukv, p_conv_w = _exchange(
        "scatter_late", [(dmod.transpose(1, 0, 2), False), (_stack_cols(g_in, BF16), False),
                         (_stack_cols(g_uq, BF16), False), (_stack_cols(g_ukv, BF16), False),
                         (_stack_cols(g_conv_w[:CONV_WIDTH], F32), False)])
    dmod_rows = jnp.pad(dmod_s.reshape(all_rows, ada_cols), ((0, pad_rows), (0, 0)))
    g_ada = _mm("ada_dw", c_rows, dmod_rows, "tn", F32, a_fn=_silu)
    (g_b_cols,) = _rowwise("ada_db", lambda r, b, cc: ([], [], [jnp.sum(r[0], axis=0, keepdims=True)]),
                           all_rows + pad_rows, all_rows + pad_rows, rows=[_full(dmod_rows)], tot_outs=[(1, ada_cols)])

    partial_of = {"norm1_g": g_norm1, "q_latent_g": g_gq, "kv_latent_g": g_gkv, "qk_norm_q_g": g_gqn,
                  "qk_norm_k_g": g_gkn, "conv_b": g_conv_b, "conv_ln_g": g_ln_g, "conv_ln_b": g_ln_b, "norm2_g": g_norm2}
    names = [n for n in REPLICATED if n != "b_ada"]
    pieces = [_pad_lanes(partial_of[n], -(-partial_of[n].shape[1] // LANES) * LANES) for n in names] + [g_b_cols]
    widths = [p.shape[1] for p in pieces]
    small = jnp.concatenate(pieces, axis=1)
    small = _pad_lanes(small, -(-small.shape[1] // (8 * LANES)) * 8 * LANES).reshape(-1, LANES)
    (small_s,) = _exchange("gather_small_grads", [(small, True)])
    small_s = small_s.reshape(N_DEV, 1, -1)
    parts = {}
    off = 0
    for n, wd in zip(names, widths):
        parts[n] = small_s[:, :, off:off + vec[n].shape[1]]
        off += wd
    parts["b_ada"] = small_s[:, 0, off:off + ada_cols].reshape(1, 1, N_DEV * ada_cols)
    parts.update({"w_ada": g_ada[None], "w_in": p_in, "w_uq": p_uq, "w_ukv": p_ukv, "w_o_mla": p_o, "conv_w": p_conv_w,
                  "w_pw_out": p_pw, "w_out": p_out, "w_ff1": p_ff1, "w_ff2": p_ff2})

    grad_out, delta_out, m_out, v_out = [], [], [], []
    for n in WEIGHTS:
        shape2 = parts[n].shape[1:]
        g_w, d_w, n_m, n_v = _adamw("adamw_" + n, local[n].reshape(shape2), parts[n], given["m_" + n].reshape(shape2),
                                    given["v_" + n].reshape(shape2))
        full_shape = given[n].shape
        grad_out.append(g_w.reshape(full_shape))
        delta_out.append(d_w.reshape(full_shape))
        m_out.append(n_m.reshape(full_shape))
        v_out.append(n_v.reshape(full_shape))
    return (loss, grad_x.reshape(n_b, seq, d), *grad_out, *delta_out, *m_out, *v_out)
```

```python
import functools

import jax
import jax.numpy as jnp
from jax import lax
from jax.experimental import pallas as pl
from jax.experimental.pallas import tpu as pltpu

F32 = jnp.float32
BF16 = jnp.bfloat16

N_DEV = 8
MESH_AXES = ("x", "y", "c")
EPS = 1e-6
N_HEADS = 8
QK_HEAD_DIM = 96
QK_NOPE_DIM = 64
QK_ROPE_DIM = 32
V_HEAD_DIM = 64
HEAD_PAD = 128
Q_LORA = 256
KV_LORA = 128
CONV_CH = 512
CONV_WIDTH = 31
CONV_HALO = 32
CHUNK = 64
ROPE_THETA = 10000.0
OFF_Q = Q_LORA
OFF_KV = OFF_Q + KV_LORA
OFF_KR = OFF_KV + QK_ROPE_DIM
OFF_GLU = OFF_KR + 2 * CONV_CH
ADA_CHUNKS = 6
ADAM_LR = 0.001
ADAM_B1 = 0.9
ADAM_B2 = 0.999
ADAM_EPS = 1e-08
ADAM_WD = 0.01
ADAM_STEP = 10
LANES = 128
VMEM_LIMIT = 56 * 1024 * 1024
NEG_BIG = -1e30
ATT_HEADS = 4
ATT_TILE = 256
ATT_SCALE = QK_HEAD_DIM ** -0.5
LOG2E = 1.4426950408889634
LN2 = 0.6931471805599453
QK_SCALE = ATT_SCALE * LOG2E
ROWS_PAD = 16

REPLICATED = ("b_ada", "norm1_g", "q_latent_g", "kv_latent_g", "qk_norm_q_g", "qk_norm_k_g", "conv_b", "conv_ln_g",
              "conv_ln_b", "norm2_g")
WEIGHTS = ("w_ada", "b_ada", "norm1_g", "w_in", "q_latent_g", "w_uq", "kv_latent_g", "w_ukv", "qk_norm_q_g",
           "qk_norm_k_g", "w_o_mla", "conv_w", "conv_b", "conv_ln_g", "conv_ln_b", "w_pw_out", "w_out", "norm2_g",
           "w_ff1", "w_ff2")


def _tile(dim, pref):
    if dim <= pref:
        return dim
    t = (pref // LANES) * LANES
    while dim % t:
        t -= LANES
    return t


def _params(semantics):
    return pltpu.CompilerParams(dimension_semantics=semantics, vmem_limit_bytes=VMEM_LIMIT)


def _sigmoid(v):
    return 1.0 / (1.0 + jnp.exp(-v))


def _silu(v):
    return v * _sigmoid(v)


def _relu2(v):
    return jnp.square(jnp.maximum(v, 0.0))


_DIMS = {"nn": (((1,), (0,)), ((), ())), "nt": (((1,), (1,)), ((), ())), "tn": (((0,), (0,)), ((), ()))}


def _mm(name, a, b, mode, out_dtype, *, a_fn=None, epi=None, epi_in=(), b_stacked=False, out_stacked=False,
        tm=1024, tn=1024, tk=1024):
    if b_stacked:
        shard = b.shape[2]
        b_rows, b_cols = b.shape[1], N_DEV * shard
    else:
        b_rows, b_cols = b.shape
    if mode == "nn":
        (m, k), n = a.shape, b_cols
    elif mode == "nt":
        (m, k), n = a.shape, b_rows
    else:
        (k, m), n = a.shape, b_cols
    if out_stacked:
        shard = n // N_DEV
    tm = _tile(m, tm)
    tn = _tile(shard, tn) if (out_stacked or (b_stacked and mode != "nt")) else _tile(n, tn)
    tk = _tile(shard, tk) if (b_stacked and mode == "nt") else _tile(k, tk)
    nk = k // tk
    a_spec = (pl.BlockSpec((tk, tm), lambda i, j, kk: (kk, i)) if mode == "tn"
              else pl.BlockSpec((tm, tk), lambda i, j, kk: (i, kk)))
    if b_stacked and mode == "nt":
        per = shard // tk
        b_spec = pl.BlockSpec((None, tn, tk), lambda i, j, kk: (kk // per, j, kk % per))
    elif b_stacked:
        per = shard // tn
        b_spec = pl.BlockSpec((None, tk, tn), lambda i, j, kk: (j // per, kk, j % per))
    elif mode == "nt":
        b_spec = pl.BlockSpec((tn, tk), lambda i, j, kk: (j, kk))
    else:
        b_spec = pl.BlockSpec((tk, tn), lambda i, j, kk: (kk, j))
    e_spec = pl.BlockSpec((tm, tn), lambda i, j, kk: (i, j))
    if out_stacked:
        per_o = shard // tn
        o_spec = pl.BlockSpec((None, tm, tn), lambda i, j, kk: (j // per_o, i, j % per_o))
        out_shape = jax.ShapeDtypeStruct((N_DEV, m, shard), out_dtype)
    else:
        o_spec = e_spec
        out_shape = jax.ShapeDtypeStruct((m, n), out_dtype)
    n_epi = len(epi_in)

    def body(a_ref, b_ref, *rest):
        epi_refs, o_ref, acc_ref = rest[:n_epi], rest[n_epi], rest[n_epi + 1]
        kk = pl.program_id(2)

        @pl.when(kk == 0)
        def _():
            acc_ref[...] = jnp.zeros_like(acc_ref)

        av = a_ref[...]
        if a_fn is not None:
            av = a_fn(av.astype(F32))
        acc_ref[...] += lax.dot_general(av.astype(BF16), b_ref[...].astype(BF16), _DIMS[mode],
                                        preferred_element_type=F32)

        @pl.when(kk == nk - 1)
        def _():
            acc = acc_ref[...]
            if epi is not None:
                acc = epi(acc, *[r[...].astype(F32) for r in epi_refs])
            o_ref[...] = acc.astype(out_dtype)

    return pl.pallas_call(
        body, name=name, grid=(m // tm, n // tn, nk),
        in_specs=[a_spec, b_spec] + [e_spec] * n_epi, out_specs=o_spec, out_shape=out_shape,
        scratch_shapes=[pltpu.VMEM((tm, tn), F32)],
        compiler_params=_params(("parallel", "parallel", "arbitrary")),
    )(a, b, *epi_in)


def _rowwise(name, fn, n_rows, seq, rows, bats=(), consts=(), outs=(), bat_outs=(), tot_outs=(), tm=256):
    tm = min(tm, seq)
    per_seq = seq // tm
    n_b = n_rows // seq
    nr, nb, nc, no, nbo, nto = len(rows), len(bats), len(consts), len(outs), len(bat_outs), len(tot_outs)

    def body(*refs):
        i = pl.program_id(0)
        r_in = [r[...] for r in refs[:nr]]
        b_in = [r[0] for r in refs[nr:nr + nb]]
        c_in = [r[...] for r in refs[nr + nb:nr + nb + nc]]
        o_refs = refs[nr + nb + nc:nr + nb + nc + no]
        bo_refs = refs[nr + nb + nc + no:nr + nb + nc + no + nbo]
        to_refs = refs[nr + nb + nc + no + nbo:]
        o_val, bo_val, to_val = fn(r_in, b_in, c_in)
        for r, v in zip(o_refs, o_val):
            r[...] = v.astype(r.dtype)
        if nbo:
            @pl.when(i % per_seq == 0)
            def _():
                for r in bo_refs:
                    r[...] = jnp.zeros_like(r)

            for r, v in zip(bo_refs, bo_val):
                r[0] += v
        if nto:
            @pl.when(i == 0)
            def _():
                for r in to_refs:
                    r[...] = jnp.zeros_like(r)

            for r, v in zip(to_refs, to_val):
                r[...] += v

    in_specs = [pl.BlockSpec((tm, w), functools.partial(lambda cb, i: (i, cb), cb)) for (_, w, cb) in rows]
    in_specs += [pl.BlockSpec((1, 1, bt.shape[2]), lambda i: (i // per_seq, 0, 0)) for bt in bats]
    in_specs += [pl.BlockSpec(ct.shape, lambda i: (0, 0)) for ct in consts]
    out_specs = [pl.BlockSpec((tm, w), lambda i: (i, 0)) for (w, _) in outs]
    out_specs += [pl.BlockSpec((1, 1, w), lambda i: (i // per_seq, 0, 0)) for w in bat_outs]
    out_specs += [pl.BlockSpec(shp, lambda i: (0, 0)) for shp in tot_outs]
    out_shape = [jax.ShapeDtypeStruct((n_rows, w), dt) for (w, dt) in outs]
    out_shape += [jax.ShapeDtypeStruct((n_b, 1, w), F32) for w in bat_outs]
    out_shape += [jax.ShapeDtypeStruct(shp, F32) for shp in tot_outs]
    res = pl.pallas_call(
        body, name=name, grid=(n_rows // tm,), in_specs=in_specs, out_specs=out_specs, out_shape=out_shape,
        compiler_params=_params(("arbitrary",)),
    )(*[r[0] for r in rows], *bats, *consts)
    return res


def _full(arr):
    return (arr, arr.shape[1], 0)


def _norm_mod(x, g, scale, shift):
    r = lax.rsqrt(jnp.mean(x * x, axis=-1, keepdims=True) + EPS)
    xh = x * r
    return xh * g * (1.0 + scale) + shift


def _norm_mod_bwd(x, g, scale, dh):
    r = lax.rsqrt(jnp.mean(x * x, axis=-1, keepdims=True) + EPS)
    xh = x * r
    dn = dh * (1.0 + scale)
    dxh = dn * g
    dx = r * (dxh - xh * jnp.mean(dxh * xh, axis=-1, keepdims=True))
    dscale = jnp.sum(dh * xh * g, axis=0, keepdims=True)
    dshift = jnp.sum(dh, axis=0, keepdims=True)
    dg = jnp.sum(dn * xh, axis=0, keepdims=True)
    return dx, dscale, dshift, dg


def _rms(v, g):
    r = lax.rsqrt(jnp.mean(v * v, axis=-1, keepdims=True) + EPS)
    return v * r * g


def _rms_bwd(v, g, dy):
    r = lax.rsqrt(jnp.mean(v * v, axis=-1, keepdims=True) + EPS)
    vh = v * r
    dvh = dy * g
    dv = r * (dvh - vh * jnp.mean(dvh * vh, axis=-1, keepdims=True))
    return dv, jnp.sum(dy * vh, axis=0, keepdims=True)


def _head_norm(v, g):
    r = lax.rsqrt(jnp.sum(v * v, axis=-1, keepdims=True) * (1.0 / QK_HEAD_DIM) + EPS)
    return v * r * g


def _head_norm_bwd(v, g, dy):
    r = lax.rsqrt(jnp.sum(v * v, axis=-1, keepdims=True) * (1.0 / QK_HEAD_DIM) + EPS)
    vh = v * r
    dvh = dy * g
    dv = r * (dvh - vh * (jnp.sum(dvh * vh, axis=-1, keepdims=True) * (1.0 / QK_HEAD_DIM)))
    return dv, jnp.sum(dy * vh, axis=0, keepdims=True)


def _rope(v, cos, sin_lo, sin_hi):
    return v * cos + pltpu.roll(v, HEAD_PAD - 16, 1) * sin_lo + pltpu.roll(v, 16, 1) * sin_hi


def _rope_bwd(g, cos, sin_lo, sin_hi):
    return g * cos + pltpu.roll(g * sin_lo, 16, 1) + pltpu.roll(g * sin_hi, HEAD_PAD - 16, 1)


def _mla_prep_fwd(zsm, wuq, wukv, gq, gkv, gqn, gkn, rope, n_b, seq):
    n_rows = n_b * seq
    tm = min(ATT_TILE, seq)
    per_seq = seq // tm
    k_cols = N_HEADS * HEAD_PAD

    def body(z_ref, wuq_ref, wukv_ref, gq_ref, gkv_ref, gqn_ref, gkn_ref, c_ref, s1_ref, s2_ref,
             q_ref, k_ref, v_ref, kt_ref):
        z = z_ref[...]
        qn = _rms(z[:, :Q_LORA], gq_ref[...]).astype(BF16)
        kvn = _rms(z[:, Q_LORA:Q_LORA + KV_LORA], gkv_ref[...]).astype(BF16)
        krp = z[:, Q_LORA + KV_LORA:]
        cos, s1, s2 = c_ref[...], s1_ref[...], s2_ref[...]
        q_all = jnp.dot(qn, wuq_ref[...], preferred_element_type=F32)
        kv_all = jnp.dot(kvn, wukv_ref[...], preferred_element_type=F32)
        for h in range(N_HEADS):
            cols = slice(h * HEAD_PAD, (h + 1) * HEAD_PAD)
            q_ref[0, h] = (_rope(_head_norm(q_all[:, cols], gqn_ref[...]), cos, s1, s2) * QK_SCALE).astype(BF16)
            kh = _rope(_head_norm(kv_all[:, cols] + krp, gkn_ref[...]), cos, s1, s2)
            k_ref[0, h] = kh.astype(BF16)
            kt_ref[0, h, 0] = kh.T.astype(BF16)
            v_ref[0, h] = kv_all[:, k_cols + h * HEAD_PAD:k_cols + (h + 1) * HEAD_PAD].astype(BF16)

    whole2 = lambda arr: pl.BlockSpec(arr.shape, lambda i: (0, 0))
    rope_spec = pl.BlockSpec((tm, HEAD_PAD), lambda i: (i % per_seq, 0))
    head_spec = pl.BlockSpec((1, N_HEADS, tm, HEAD_PAD), lambda i: (i // per_seq, 0, i % per_seq, 0))
    head_shape = jax.ShapeDtypeStruct((n_b, N_HEADS, seq, HEAD_PAD), BF16)
    t_spec = pl.BlockSpec((1, N_HEADS, 1, HEAD_PAD, tm), lambda i: (i // per_seq, 0, i % per_seq, 0, 0))
    t_shape = jax.ShapeDtypeStruct((n_b, N_HEADS, per_seq, HEAD_PAD, tm), BF16)
    return pl.pallas_call(
        body, name="mla_prep_fwd", grid=(n_rows // tm,),
        in_specs=[pl.BlockSpec((tm, 512), lambda i: (i, 0)), whole2(wuq), whole2(wukv),
                  whole2(gq), whole2(gkv), whole2(gqn), whole2(gkn), rope_spec, rope_spec, rope_spec],
        out_specs=[head_spec] * 3 + [t_spec], out_shape=[head_shape] * 3 + [t_shape],
        compiler_params=_params(("parallel",)),
    )(zsm, wuq, wukv, gq, gkv, gqn, gkn, *rope)


def _mla_prep_bwd(zsm, dq, dk, dv, wuq, wukv, gq, gkv, gqn, gkn, rope, n_b, seq):
    n_rows = n_b * seq
    tm = min(256, seq)
    per_seq = seq // tm
    tn_dims = _DIMS["tn"]
    nt_dims = _DIMS["nt"]
    k_cols = N_HEADS * HEAD_PAD

    def body(z_ref, dq_ref, dk_ref, dv_ref, wuq_ref, wukv_ref, gq_ref, gkv_ref, gqn_ref, gkn_ref,
             c_ref, s1_ref, s2_ref, dz_ref, dwuq_ref, dwukv_ref, dgq_ref, dgkv_ref, dgqn_ref, dgkn_ref):
        @pl.when(pl.program_id(0) == 0)
        def _():
            for r in (dwuq_ref, dwukv_ref, dgq_ref, dgkv_ref, dgqn_ref, dgkn_ref):
                r[...] = jnp.zeros_like(r)

        z = z_ref[...]
        zq, zkv, krp = z[:, :Q_LORA], z[:, Q_LORA:Q_LORA + KV_LORA], z[:, Q_LORA + KV_LORA:]
        qn = _rms(zq, gq_ref[...]).astype(BF16)
        kvn = _rms(zkv, gkv_ref[...]).astype(BF16)
        cos, s1, s2 = c_ref[...], s1_ref[...], s2_ref[...]
        lane = lax.broadcasted_iota(jnp.int32, (tm, HEAD_PAD), 1)
        rope_lanes = (lane >= QK_NOPE_DIM) & (lane < QK_HEAD_DIM)
        q_all = jnp.dot(qn, wuq_ref[...], preferred_element_type=F32)
        k_all = jnp.dot(kvn, wukv_ref[:, :k_cols], preferred_element_type=F32)
        dkrp = jnp.zeros((tm, HEAD_PAD), F32)
        dgqn = jnp.zeros((1, HEAD_PAD), F32)
        dgkn = jnp.zeros((1, HEAD_PAD), F32)
        dq_heads, dk_heads = [], []
        for h in range(N_HEADS):
            cols = slice(h * HEAD_PAD, (h + 1) * HEAD_PAD)
            dqh, dg = _head_norm_bwd(q_all[:, cols], gqn_ref[...],
                                     _rope_bwd(dq_ref[0, h].astype(F32) * ATT_SCALE, cos, s1, s2))
            dgqn += dg
            dq_heads.append(dqh.astype(BF16))
            dkh, dg = _head_norm_bwd(k_all[:, cols] + krp, gkn_ref[...], _rope_bwd(dk_ref[0, h].astype(F32), cos, s1, s2))
            dgkn += dg
            dkrp += jnp.where(rope_lanes, dkh, 0.0)
            dk_heads.append(dkh.astype(BF16))
        dq_all = jnp.concatenate(dq_heads, axis=1)
        dkv_all = jnp.concatenate(dk_heads + [dv_ref[0, h] for h in range(N_HEADS)], axis=1)
        dwuq_ref[...] += lax.dot_general(qn, dq_all, tn_dims, preferred_element_type=F32)
        dqn = lax.dot_general(dq_all, wuq_ref[...], nt_dims, preferred_element_type=F32)
        dwukv_ref[...] += lax.dot_general(kvn, dkv_all, tn_dims, preferred_element_type=F32)
        dkvn = lax.dot_general(dkv_all, wukv_ref[...], nt_dims, preferred_element_type=F32)
        dzq, dg = _rms_bwd(zq, gq_ref[...], dqn)
        dgq_ref[...] += dg
        dzkv, dg = _rms_bwd(zkv, gkv_ref[...], dkvn)
        dgkv_ref[...] += dg
        dgqn_ref[...] += dgqn
        dgkn_ref[...] += dgkn
        dz_ref[:, :Q_LORA] = dzq.astype(dz_ref.dtype)
        dz_ref[:, Q_LORA:Q_LORA + KV_LORA] = dzkv.astype(dz_ref.dtype)
        dz_ref[:, Q_LORA + KV_LORA:] = dkrp.astype(dz_ref.dtype)

    whole2 = lambda arr: pl.BlockSpec(arr.shape, lambda i: (0, 0))
    rope_spec = pl.BlockSpec((tm, HEAD_PAD), lambda i: (i % per_seq, 0))
    head_spec = pl.BlockSpec((1, N_HEADS, tm, HEAD_PAD), lambda i: (i // per_seq, 0, i % per_seq, 0))
    row_spec = pl.BlockSpec((tm, 512), lambda i: (i, 0))
    return pl.pallas_call(
        body, name="mla_prep_bwd", grid=(n_rows // tm,),
        in_specs=[row_spec, head_spec, head_spec, head_spec, whole2(wuq), whole2(wukv),
                  whole2(gq), whole2(gkv), whole2(gqn), whole2(gkn), rope_spec, rope_spec, rope_spec],
        out_specs=[row_spec, whole2(wuq), whole2(wukv), whole2(gq), whole2(gkv), whole2(gqn), whole2(gkn)],
        out_shape=[jax.ShapeDtypeStruct((n_rows, 512), BF16),
                   jax.ShapeDtypeStruct(wuq.shape, F32), jax.ShapeDtypeStruct(wukv.shape, F32),
                   jax.ShapeDtypeStruct(gq.shape, F32), jax.ShapeDtypeStruct(gkv.shape, F32),
                   jax.ShapeDtypeStruct(gqn.shape, F32), jax.ShapeDtypeStruct(gkn.shape, F32)],
        compiler_params=_params(("arbitrary",)),
    )(zsm, dq, dk, dv, wuq, wukv, gq, gkv, gqn, gkn, *rope)


HBM_SPEC = pl.BlockSpec(memory_space=pltpu.HBM)


def _xchg_out_shapes(bufs):
    return [jax.ShapeDtypeStruct((N_DEV,) + (a.shape if gather else a.shape[1:]), a.dtype) for a, gather in bufs]


def _xchg_scratch(n_buf):
    return [pltpu.SemaphoreType.DMA((n_buf * (N_DEV - 1),)), pltpu.SemaphoreType.DMA((n_buf * (N_DEV - 1),)),
            pltpu.SemaphoreType.DMA((n_buf,))]


def _xchg_copies(src_refs, dst_refs, gathers, send_sems, recv_sems, local_sems):
    x, y, c = lax.axis_index("x"), lax.axis_index("y"), lax.axis_index("c")
    me = 4 * x + 2 * y + c
    local, starts, arrivals = [], [], []
    for bi, (src, dst, gather) in enumerate(zip(src_refs, dst_refs, gathers)):
        local.append(pltpu.make_async_copy(src if gather else src.at[me], dst.at[me], local_sems.at[bi]))
        for kk in range(1, N_DEV):
            px = 1 - x if kk & 4 else x
            py = 1 - y if kk & 2 else y
            pc = 1 - c if kk & 1 else c
            pid = 4 * px + 2 * py + pc
            sem = bi * (N_DEV - 1) + kk - 1
            starts.append(pltpu.make_async_remote_copy(
                src_ref=src if gather else src.at[pid], dst_ref=dst.at[me],
                send_sem=send_sems.at[sem], recv_sem=recv_sems.at[sem],
                device_id=(px, py, pc), device_id_type=pl.DeviceIdType.MESH))
            arrivals.append(pltpu.make_async_remote_copy(
                src_ref=src if gather else src.at[me], dst_ref=dst.at[pid],
                send_sem=send_sems.at[sem], recv_sem=recv_sems.at[sem],
                device_id=(px, py, pc), device_id_type=pl.DeviceIdType.MESH))
    return local, starts, arrivals


def _xchg_start(copies):
    local, sends, _ = copies
    for cp in local + sends:
        cp.start()


def _xchg_finish(copies):
    local, sends, arrivals = copies
    for cp in arrivals:
        cp.wait_recv()
    for cp in sends:
        cp.wait_send()
    for cp in local:
        cp.wait()


def _exchange(name, bufs):
    n_buf = len(bufs)
    gathers = [g for _, g in bufs]

    def body(*refs):
        srcs, dsts = refs[:n_buf], refs[n_buf:2 * n_buf]
        copies = _xchg_copies(srcs, dsts, gathers, *refs[2 * n_buf:])
        _xchg_start(copies)
        _xchg_finish(copies)

    return pl.pallas_call(
        body, name=name, out_shape=_xchg_out_shapes(bufs),
        in_specs=[HBM_SPEC] * n_buf, out_specs=[HBM_SPEC] * n_buf, scratch_shapes=_xchg_scratch(n_buf),
    )(*[a for a, _ in bufs])


def _chunk_mask(t, keys_first):
    key = lax.broadcasted_iota(jnp.int32, (t, t), 0 if keys_first else 1) // CHUNK
    query = lax.broadcasted_iota(jnp.int32, (t, t), 1 if keys_first else 0) // CHUNK
    return query >= key


def _grid_ends(grid):
    ids = [pl.program_id(ax) for ax in range(len(grid))]
    first = functools.reduce(jnp.logical_and, [i == 0 for i in ids])
    last = functools.reduce(jnp.logical_and, [i == g - 1 for i, g in zip(ids, grid)])
    return first, last


def _attn_fwd(q, k, v, bufs, n_b, seq):
    tq = min(ATT_TILE, seq)
    nq = seq // tq
    nt_dims = _DIMS["nt"]
    hpb = ATT_HEADS
    grid = (n_b, N_HEADS // hpb, nq)
    n_buf = len(bufs)
    gathers = [g for _, g in bufs]
    sum_lane = [HEAD_PAD - 1 if hh % 2 == 0 else 0 for hh in range(hpb)]

    def body(q_ref, k_ref, v_ref, *rest):
        srcs, (o_ref, lse_ref), dsts = rest[:n_buf], rest[n_buf:n_buf + 2], rest[n_buf + 2:2 * n_buf + 2]
        copies = _xchg_copies(srcs, dsts, gathers, *rest[2 * n_buf + 2:])
        first, last = _grid_ends(grid)
        pl.when(first)(functools.partial(_xchg_start, copies))

        qi = pl.program_id(2)
        mask = _chunk_mask(tq, keys_first=False)
        lane_row = lax.broadcasted_iota(jnp.int32, (1, HEAD_PAD), 1)
        ones = [(lane_row == sum_lane[hh]).astype(BF16) for hh in range(hpb)]
        qs = [q_ref[0, hh] for hh in range(hpb)]

        def step(j, carry, masked):
            rows = pl.ds(pl.multiple_of(j * tq, tq), tq)
            out = []
            for hh in range(hpb):
                m, acc = carry[hh]
                s = lax.dot_general(qs[hh], k_ref[0, hh, rows, :], nt_dims, preferred_element_type=F32)
                if masked:
                    s = jnp.where(mask, s, NEG_BIG)
                m_new = jnp.maximum(m, jnp.max(s, axis=-1, keepdims=True))
                p = jnp.exp2(s - m_new).astype(BF16)
                acc = jnp.exp2(m - m_new) * acc + jnp.dot(p, v_ref[0, hh, rows, :] + ones[hh], preferred_element_type=F32)
                out.append((m_new, acc))
            return tuple(out)

        init = tuple((jnp.full((tq, 1), NEG_BIG, F32), jnp.zeros((tq, HEAD_PAD), F32)) for _ in range(hpb))
        carry = lax.fori_loop(0, qi, functools.partial(step, masked=False), init)
        carry = step(qi, carry, True)
        lane = lax.broadcasted_iota(jnp.int32, (tq, HEAD_PAD), 1)
        for pair in range(hpb // 2):
            outs = []
            for hh in (2 * pair, 2 * pair + 1):
                m, acc = carry[hh]
                l = jnp.sum(jnp.where(lane == sum_lane[hh], acc, 0.0), axis=-1, keepdims=True)
                outs.append(acc * (1.0 / l))
                lse_ref[0, hh] = jnp.broadcast_to(m + jnp.log2(l), (tq, HEAD_PAD)).T[0:8, :]
            o_ref[0, :, pair * HEAD_PAD:(pair + 1) * HEAD_PAD] = jnp.where(lane < V_HEAD_DIM, outs[0], outs[1]).astype(BF16)

        pl.when(last)(functools.partial(_xchg_finish, copies))

    kv_spec = pl.BlockSpec((1, hpb, seq, HEAD_PAD), lambda b, hb, i: (b, hb, 0, 0))
    q_spec = pl.BlockSpec((1, hpb, tq, HEAD_PAD), lambda b, hb, i: (b, hb, i, 0))
    res = pl.pallas_call(
        body, name="attn_fwd", grid=grid,
        in_specs=[q_spec, kv_spec, kv_spec] + [HBM_SPEC] * n_buf,
        out_specs=[pl.BlockSpec((1, tq, hpb * V_HEAD_DIM), lambda b, hb, i: (b, i, hb)),
                   pl.BlockSpec((1, hpb, 8, tq), lambda b, hb, i: (b, hb, 0, i))] + [HBM_SPEC] * n_buf,
        out_shape=[jax.ShapeDtypeStruct((n_b, seq, N_HEADS * V_HEAD_DIM), BF16),
                   jax.ShapeDtypeStruct((n_b, N_HEADS, 8, seq), F32)] + _xchg_out_shapes(bufs),
        scratch_shapes=_xchg_scratch(n_buf),
        compiler_params=_params(("arbitrary", "arbitrary", "arbitrary")),
    )(q, k, v, *[a for a, _ in bufs])
    return res[0], res[1], res[2:]


def _attn_bwd(q, k, v, kt, do, o, lse, bufs, n_b, seq):
    tq = min(ATT_TILE, seq)
    nq = seq // tq
    nt_dims = _DIMS["nt"]
    hpb = ATT_HEADS
    grid = (n_b, N_HEADS // hpb, nq)
    n_buf = len(bufs)
    gathers = [g for _, g in bufs]

    def body(q_ref, k_ref, v_ref, kt_ref, do_ref, o_ref, lse_ref, *rest):
        srcs, (dq_ref, dk_ref, dv_ref), dsts = rest[:n_buf], rest[n_buf:n_buf + 3], rest[n_buf + 3:2 * n_buf + 3]
        dk_acc, dv_acc = rest[2 * n_buf + 3:2 * n_buf + 5]
        copies = _xchg_copies(srcs, dsts, gathers, *rest[2 * n_buf + 5:])
        first, last = _grid_ends(grid)
        pl.when(first)(functools.partial(_xchg_start, copies))

        qi = pl.program_id(2)

        @pl.when(qi == 0)
        def _():
            dk_acc[...] = jnp.zeros_like(dk_acc)
            dv_acc[...] = jnp.zeros_like(dv_acc)

        mask = _chunk_mask(tq, keys_first=True)
        lane = lax.broadcasted_iota(jnp.int32, (tq, HEAD_PAD), 1)
        qs, dos, deltas, lses = [], [], [], []
        for hh in range(hpb):
            cols = slice((hh // 2) * HEAD_PAD, (hh // 2 + 1) * HEAD_PAD)
            do_pair = do_ref[0, :, cols]
            prod = do_pair.astype(F32) * o_ref[0, :, cols].astype(F32)
            delta = jnp.sum(jnp.where(lane // V_HEAD_DIM == hh % 2, prod, 0.0), axis=-1, keepdims=True)
            qs.append(q_ref[0, hh])
            dos.append(do_pair)
            deltas.append(jnp.broadcast_to(delta, (tq, HEAD_PAD)).T[0:1, :])
            lses.append(lse_ref[0, hh][0:1, :])

        def step(j, dqs, masked):
            rows = pl.ds(pl.multiple_of(j * tq, tq), tq)
            out = []
            for hh in range(hpb):
                s = lax.dot_general(k_ref[0, hh, rows, :], qs[hh], nt_dims, preferred_element_type=F32)
                p = jnp.exp2(s - lses[hh])
                if masked:
                    p = jnp.where(mask, p, 0.0)
                dv_acc[hh, rows, :] += jnp.dot(p.astype(BF16), dos[hh], preferred_element_type=F32)
                dp = lax.dot_general(v_ref[0, hh, rows, :], dos[hh], nt_dims, preferred_element_type=F32)
                ds = (p * (dp - deltas[hh])).astype(BF16)
                dk_acc[hh, rows, :] += jnp.dot(ds, qs[hh], preferred_element_type=F32)
                out.append(dqs[hh] + jnp.dot(kt_ref[0, hh, j], ds, preferred_element_type=F32))
            return tuple(out)

        dqs = tuple(jnp.zeros((HEAD_PAD, tq), F32) for _ in range(hpb))
        dqs = lax.fori_loop(0, qi, functools.partial(step, masked=False), dqs)
        dqs = step(qi, dqs, True)
        for hh in range(hpb):
            dq_ref[0, hh] = dqs[hh].T.astype(BF16)

        @pl.when(qi == nq - 1)
        def _():
            dk_ref[0] = (dk_acc[...] * LN2).astype(BF16)
            dv_ref[0] = dv_acc[...].astype(BF16)

        pl.when(last)(functools.partial(_xchg_finish, copies))

    full_spec = pl.BlockSpec((1, hpb, seq, HEAD_PAD), lambda b, hb, i: (b, hb, 0, 0))
    t_spec = pl.BlockSpec((1, hpb, nq, HEAD_PAD, tq), lambda b, hb, i: (b, hb, 0, 0, 0))
    q_spec = pl.BlockSpec((1, hpb, tq, HEAD_PAD), lambda b, hb, i: (b, hb, i, 0))
    o_spec = pl.BlockSpec((1, tq, hpb * V_HEAD_DIM), lambda b, hb, i: (b, i, hb))
    lse_spec = pl.BlockSpec((1, hpb, 8, tq), lambda b, hb, i: (b, hb, 0, i))
    head_shape = jax.ShapeDtypeStruct((n_b, N_HEADS, seq, HEAD_PAD), BF16)
    res = pl.pallas_call(
        body, name="attn_bwd", grid=grid,
        in_specs=[q_spec, full_spec, full_spec, t_spec, o_spec, o_spec, lse_spec] + [HBM_SPEC] * n_buf,
        out_specs=[q_spec, full_spec, full_spec] + [HBM_SPEC] * n_buf,
        out_shape=[head_shape] * 3 + _xchg_out_shapes(bufs),
        scratch_shapes=[pltpu.VMEM((hpb, seq, HEAD_PAD), F32), pltpu.VMEM((hpb, seq, HEAD_PAD), F32)]
        + _xchg_scratch(n_buf),
        compiler_params=_params(("arbitrary", "arbitrary", "arbitrary")),
    )(q, k, v, kt, do, o, lse, *[a for a, _ in bufs])
    return res[0], res[1], res[2], res[3:]


def _in_proj_bwd(parts, x2, dx1, scale, g, bufs, seq):
    n_rows, d = x2.shape
    tm = min(512, seq)
    per_seq = seq // tm
    grid = (n_rows // tm,)
    n_part, n_buf = len(parts), len(bufs)
    gathers = [gt for _, gt in bufs]
    nt_dims = _DIMS["nt"]

    def body(*refs):
        dz_refs, w_refs = refs[:n_part], refs[n_part:2 * n_part]
        x_ref, dx1_ref, sc_ref, g_ref = refs[2 * n_part:2 * n_part + 4]
        srcs = refs[2 * n_part + 4:2 * n_part + 4 + n_buf]
        gx_ref, dsc_ref, dsh_ref, dg_ref = refs[2 * n_part + 4 + n_buf:2 * n_part + 8 + n_buf]
        dsts = refs[2 * n_part + 8 + n_buf:2 * n_part + 8 + 2 * n_buf]
        copies = _xchg_copies(srcs, dsts, gathers, *refs[2 * n_part + 8 + 2 * n_buf:])
        first, last = _grid_ends(grid)
        pl.when(first)(functools.partial(_xchg_start, copies))

        i = pl.program_id(0)
        dh = None
        for dz_ref, w_ref in zip(dz_refs, w_refs):
            term = lax.dot_general(dz_ref[...], w_ref[...], nt_dims, preferred_element_type=F32)
            dh = term if dh is None else dh + term
        dx, dsc, dsh, dg = _norm_mod_bwd(x_ref[...], g_ref[...], sc_ref[0], dh)
        gx_ref[...] = dx1_ref[...] + dx

        @pl.when(i % per_seq == 0)
        def _():
            dsc_ref[...] = jnp.zeros_like(dsc_ref)
            dsh_ref[...] = jnp.zeros_like(dsh_ref)

        @pl.when(i == 0)
        def _():
            dg_ref[...] = jnp.zeros_like(dg_ref)

        dsc_ref[0] += dsc
        dsh_ref[0] += dsh
        dg_ref[...] += dg
        pl.when(last)(functools.partial(_xchg_finish, copies))

    row = lambda width: pl.BlockSpec((tm, width), lambda i: (i, 0))
    bat = pl.BlockSpec((1, 1, d), lambda i: (i // per_seq, 0, 0))
    whole = lambda arr: pl.BlockSpec(arr.shape, lambda i: (0, 0))
    n_b = n_rows // seq
    res = pl.pallas_call(
        body, name="in_proj_bwd", grid=grid,
        in_specs=[row(dz.shape[1]) for dz, _ in parts] + [whole(w) for _, w in parts]
        + [row(d), row(d), bat, whole(g)] + [HBM_SPEC] * n_buf,
        out_specs=[row(d), bat, bat, whole(g)] + [HBM_SPEC] * n_buf,
        out_shape=[jax.ShapeDtypeStruct((n_rows, d), F32), jax.ShapeDtypeStruct((n_b, 1, d), F32),
                   jax.ShapeDtypeStruct((n_b, 1, d), F32), jax.ShapeDtypeStruct(g.shape, F32)] + _xchg_out_shapes(bufs),
        scratch_shapes=_xchg_scratch(n_buf),
        compiler_params=_params(("arbitrary",)),
    )(*[dz for dz, _ in parts], *[w for _, w in parts], x2, dx1, scale, g, *[a for a, _ in bufs])
    return res[0], res[1], res[2], res[3], res[4:]


def _ln_silu(u1, g, b):
    mu = jnp.mean(u1, axis=-1, keepdims=True)
    uc = u1 - mu
    r = lax.rsqrt(jnp.mean(uc * uc, axis=-1, keepdims=True) + EPS)
    y = uc * r * g + b
    return y * _sigmoid(y)


def _conv_fill_glu(z_ref, u0_ref, seq, tile):
    u0_ref[0:CONV_HALO, :] = jnp.zeros((CONV_HALO, CONV_CH), F32)
    for t in range(seq // tile):
        zt = z_ref[0, t * tile:(t + 1) * tile, :].astype(F32)
        u0_ref[CONV_HALO + t * tile:CONV_HALO + (t + 1) * tile, :] = zt[:, :CONV_CH] * _sigmoid(zt[:, CONV_CH:])


def _conv_tile(u0_ref, w_ref, b_ref, t, tile):
    acc = jnp.broadcast_to(b_ref[...], (tile, CONV_CH))
    base = t * tile + CONV_HALO - (CONV_WIDTH - 1)
    for kk in range(CONV_WIDTH):
        acc = acc + w_ref[kk:kk + 1, :] * u0_ref[base + kk:base + kk + tile, :]
    return acc


def _conv_fwd(zglu, conv_w, conv_b, ln_g, ln_b, n_b, seq):
    tile = min(256, seq)

    def body(z_ref, w_ref, b_ref, g_ref, bb_ref, o_ref, u1_ref, u0_ref):
        _conv_fill_glu(z_ref, u0_ref, seq, tile)
        for t in range(seq // tile):
            u1 = _conv_tile(u0_ref, w_ref, b_ref, t, tile)
            u1_ref[0, t * tile:(t + 1) * tile, :] = u1
            o_ref[0, t * tile:(t + 1) * tile, :] = _ln_silu(u1, g_ref[...], bb_ref[...]).astype(BF16)

    whole2 = lambda arr: pl.BlockSpec(arr.shape, lambda b: (0, 0))
    seq_spec = pl.BlockSpec((1, seq, CONV_CH), lambda b: (b, 0, 0))
    return pl.pallas_call(
        body, name="conv_fwd", grid=(n_b,),
        in_specs=[pl.BlockSpec((1, seq, 2 * CONV_CH), lambda b: (b, 0, 0)), whole2(conv_w), whole2(conv_b),
                  whole2(ln_g), whole2(ln_b)],
        out_specs=[seq_spec, seq_spec],
        out_shape=[jax.ShapeDtypeStruct((n_b, seq, CONV_CH), BF16), jax.ShapeDtypeStruct((n_b, seq, CONV_CH), F32)],
        scratch_shapes=[pltpu.VMEM((seq + CONV_HALO, CONV_CH), F32)],
        compiler_params=_params(("parallel",)),
    )(zglu, conv_w, conv_b, ln_g, ln_b)


def _conv_bwd(zglu, u1_saved, du3, conv_w, ln_g, ln_b, n_b, seq):
    tile = min(256, seq)
    n_t = seq // tile

    def body(z_ref, u1_ref, du3_ref, w_ref, g_ref, bb_ref, dz_ref, dw_ref, db_ref, dg_ref, dbb_ref, u0_ref, du1_ref):
        @pl.when(pl.program_id(0) == 0)
        def _():
            for r in (dw_ref, db_ref, dg_ref, dbb_ref):
                r[...] = jnp.zeros_like(r)

        _conv_fill_glu(z_ref, u0_ref, seq, tile)
        du1_ref[seq:seq + CONV_HALO, :] = jnp.zeros((CONV_HALO, CONV_CH), F32)
        g = g_ref[...]
        for t in range(n_t):
            u1 = u1_ref[0, t * tile:(t + 1) * tile, :]
            mu = jnp.mean(u1, axis=-1, keepdims=True)
            uc = u1 - mu
            r = lax.rsqrt(jnp.mean(uc * uc, axis=-1, keepdims=True) + EPS)
            xh = uc * r
            y = xh * g + bb_ref[...]
            sg = _sigmoid(y)
            dy = du3_ref[0, t * tile:(t + 1) * tile, :].astype(F32) * (sg * (1.0 + y * (1.0 - sg)))
            dg_ref[...] += jnp.sum(dy * xh, axis=0, keepdims=True)
            dbb_ref[...] += jnp.sum(dy, axis=0, keepdims=True)
            dxh = dy * g
            du1 = r * (dxh - jnp.mean(dxh, axis=-1, keepdims=True) - xh * jnp.mean(dxh * xh, axis=-1, keepdims=True))
            db_ref[...] += jnp.sum(du1, axis=0, keepdims=True)
            du1_ref[t * tile:(t + 1) * tile, :] = du1
        for t in range(n_t):
            du1 = du1_ref[t * tile:(t + 1) * tile, :]
            du0 = jnp.zeros((tile, CONV_CH), F32)
            base_u = t * tile + CONV_HALO - (CONV_WIDTH - 1)
            base_d = t * tile + (CONV_WIDTH - 1)
            for kk in range(CONV_WIDTH):
                du0 = du0 + w_ref[kk:kk + 1, :] * du1_ref[base_d - kk:base_d - kk + tile, :]
                dw_ref[kk:kk + 1, :] += jnp.sum(du1 * u0_ref[base_u + kk:base_u + kk + tile, :], axis=0, keepdims=True)
            zt = z_ref[0, t * tile:(t + 1) * tile, :].astype(F32)
            ga, sb = zt[:, :CONV_CH], _sigmoid(zt[:, CONV_CH:])
            dz_ref[0, t * tile:(t + 1) * tile, :CONV_CH] = (du0 * sb).astype(BF16)
            dz_ref[0, t * tile:(t + 1) * tile, CONV_CH:] = (du0 * ga * sb * (1.0 - sb)).astype(BF16)

    whole2 = lambda arr: pl.BlockSpec(arr.shape, lambda b: (0, 0))
    z_spec = pl.BlockSpec((1, seq, 2 * CONV_CH), lambda b: (b, 0, 0))
    seq_spec = pl.BlockSpec((1, seq, CONV_CH), lambda b: (b, 0, 0))
    return pl.pallas_call(
        body, name="conv_bwd", grid=(n_b,),
        in_specs=[z_spec, seq_spec, seq_spec, whole2(conv_w), whole2(ln_g), whole2(ln_b)],
        out_specs=[z_spec, whole2(conv_w), whole2(ln_g), whole2(ln_g), whole2(ln_b)],
        out_shape=[jax.ShapeDtypeStruct((n_b, seq, 2 * CONV_CH), BF16), jax.ShapeDtypeStruct(conv_w.shape, F32),
                   jax.ShapeDtypeStruct(ln_g.shape, F32), jax.ShapeDtypeStruct(ln_g.shape, F32),
                   jax.ShapeDtypeStruct(ln_b.shape, F32)],
        scratch_shapes=[pltpu.VMEM((seq + CONV_HALO, CONV_CH), F32), pltpu.VMEM((seq + CONV_HALO, CONV_CH), F32)],
        compiler_params=_params(("arbitrary",)),
    )(zglu, u1_saved, du3, conv_w, ln_g, ln_b)


def _adamw(name, w, parts, m, v):
    n_parts = parts.shape[0]

    def body(w_ref, p_ref, m_ref, v_ref, g_ref, d_ref, nm_ref, nv_ref):
        gg = p_ref[0].astype(F32)
        for j in range(1, n_parts):
            gg = gg + p_ref[j].astype(F32)
        nm = ADAM_B1 * m_ref[...] + (1.0 - ADAM_B1) * gg
        nv = ADAM_B2 * v_ref[...] + (1.0 - ADAM_B2) * jnp.square(gg)
        m_hat = nm / (1.0 - ADAM_B1 ** ADAM_STEP)
        v_hat = nv / (1.0 - ADAM_B2 ** ADAM_STEP)
        g_ref[...] = gg
        d_ref[...] = -ADAM_LR * (m_hat / (jnp.sqrt(v_hat) + ADAM_EPS) + ADAM_WD * w_ref[...])
        nm_ref[...] = nm
        nv_ref[...] = nv

    shape = jax.ShapeDtypeStruct(w.shape, F32)
    return pl.pallas_call(body, name=name, out_shape=[shape] * 4, compiler_params=_params(None))(w, parts, m, v)


def _rope_tables(seq):
    inv_freq = ROPE_THETA ** (-jnp.arange(0, QK_ROPE_DIM, 2, dtype=F32) / QK_ROPE_DIM)
    ang = jnp.arange(seq, dtype=F32)[:, None] * inv_freq[None, :]
    cos, sin = jnp.cos(ang), jnp.sin(ang)
    half = QK_ROPE_DIM // 2
    z = lambda n: jnp.zeros((seq, n), F32)
    tail = HEAD_PAD - QK_HEAD_DIM
    cos_t = jnp.concatenate([jnp.ones((seq, QK_NOPE_DIM), F32), cos, cos, z(tail)], axis=1)
    sin_lo = jnp.concatenate([z(QK_NOPE_DIM), -sin, z(half), z(tail)], axis=1)
    sin_hi = jnp.concatenate([z(QK_NOPE_DIM), z(half), sin, z(tail)], axis=1)
    return cos_t, sin_lo, sin_hi


def _pad_lanes(v, width=HEAD_PAD):
    return jnp.pad(v, [(0, 0)] * (v.ndim - 1) + [(0, width - v.shape[-1])])


def _unstack_cols(s):
    return s.transpose(1, 0, 2).reshape(s.shape[1], N_DEV * s.shape[2])


def _stack_cols(g, dtype):
    rows, cols = g.shape
    return g.reshape(rows, N_DEV, cols // N_DEV).transpose(1, 0, 2).astype(dtype)


def kernel(x, c, w_ada, b_ada, norm1_g, w_in, q_latent_g, w_uq, kv_latent_g, w_ukv, qk_norm_q_g, qk_norm_k_g, w_o_mla, conv_w, conv_b, conv_ln_g, conv_ln_b, w_pw_out, w_out, norm2_g, w_ff1, w_ff2, loss_target, m_w_ada, m_b_ada, m_norm1_g, m_w_in, m_q_latent_g, m_w_uq, m_kv_latent_g, m_w_ukv, m_qk_norm_q_g, m_qk_norm_k_g, m_w_o_mla, m_conv_w, m_conv_b, m_conv_ln_g, m_conv_ln_b, m_w_pw_out, m_w_out, m_norm2_g, m_w_ff1, m_w_ff2, v_w_ada, v_b_ada, v_norm1_g, v_w_in, v_q_latent_g, v_w_uq, v_kv_latent_g, v_w_ukv, v_qk_norm_q_g, v_qk_norm_k_g, v_w_o_mla, v_conv_w, v_conv_b, v_conv_ln_g, v_conv_ln_b, v_w_pw_out, v_w_out, v_norm2_g, v_w_ff1, v_w_ff2):
    given = dict(locals())
    local = {n: given[n][0] for n in WEIGHTS}
    vec = {n: local[n].reshape(1, -1) for n in REPLICATED}
    bf = lambda n: local[n].astype(BF16)
    n_b, seq, d = x.shape
    n_rows = n_b * seq
    x2 = x.reshape(n_rows, d)
    t2 = loss_target.reshape(n_rows, d)
    rw = functools.partial(_rowwise, n_rows=n_rows, seq=seq)
    me = 4 * lax.axis_index("x") + 2 * lax.axis_index("y") + lax.axis_index("c")
    ada_cols = local["w_ada"].shape[1]

    c_all, w_in_s, w_uq_s, w_ukv_s, conv_w_s = _exchange(
        "gather_early", [(c, True), (bf("w_in"), True), (bf("w_uq"), True), (bf("w_ukv"), True), (local["conv_w"], True)])
    w_in_f = _unstack_cols(w_in_s)
    zeros = lambda n: jnp.zeros((d, n), BF16)
    w_sm = jnp.concatenate([w_in_f[:, :OFF_KV], zeros(QK_NOPE_DIM), w_in_f[:, OFF_KV:OFF_KR], zeros(HEAD_PAD - QK_HEAD_DIM)], axis=1)
    w_glu = w_in_f[:, OFF_KR:OFF_GLU]
    w_gate = w_in_f[:, OFF_GLU:]
    wuq = _pad_lanes(_unstack_cols(w_uq_s).reshape(Q_LORA, N_HEADS, QK_HEAD_DIM)).reshape(Q_LORA, N_HEADS * HEAD_PAD)
    wukv_f = _unstack_cols(w_ukv_s).reshape(KV_LORA, N_HEADS, QK_NOPE_DIM + V_HEAD_DIM)
    wv = wukv_f[:, :, QK_NOPE_DIM:]
    odd = (jnp.arange(N_HEADS) % 2 == 1)[None, :, None]
    wuv = jnp.where(odd, jnp.pad(wv, ((0, 0), (0, 0), (V_HEAD_DIM, 0))), jnp.pad(wv, ((0, 0), (0, 0), (0, V_HEAD_DIM))))
    wukv = jnp.concatenate([_pad_lanes(wukv_f[:, :, :QK_NOPE_DIM]), wuv], axis=1).reshape(KV_LORA, 2 * N_HEADS * HEAD_PAD)
    gqn = _pad_lanes(vec["qk_norm_q_g"])
    gkn = _pad_lanes(vec["qk_norm_k_g"])
    conv_w_f = jnp.pad(_unstack_cols(conv_w_s), ((0, 1), (0, 0)))
    rope = _rope_tables(seq)

    all_rows = N_DEV * n_b
    pad_rows = (-all_rows) % ROWS_PAD
    c_rows = jnp.pad(c_all.reshape(all_rows, d), ((0, pad_rows), (0, 0)))
    b_cols = lax.dynamic_slice(local["b_ada"], (me * ada_cols,), (ada_cols,))
    mod_cols = _mm("ada_fwd", c_rows, local["w_ada"], "nn", F32, a_fn=_silu, epi=lambda acc, b: acc + b,
                   epi_in=(jnp.broadcast_to(b_cols, (all_rows + pad_rows, ada_cols)),))
    (mod_s,) = _exchange("scatter_mod", [(mod_cols[:all_rows].reshape(N_DEV, n_b, ada_cols), False)])
    mod = mod_s.transpose(1, 0, 2).reshape(n_b, ADA_CHUNKS, 1, d)
    shift1, scale1, gate1, shift2, scale2, gate2 = [mod[:, i] for i in range(ADA_CHUNKS)]

    (h,) = rw("norm1_fwd", lambda r, b, cc: ([_norm_mod(r[0], cc[0], b[0], b[1])], [], []),
              rows=[_full(x2)], bats=[scale1, shift1], consts=[vec["norm1_g"]], outs=[(d, BF16)])
    zsm = _mm("in_proj_sm", h, w_sm, "nn", F32)
    zglu = _mm("in_proj_glu", h, w_glu, "nn", BF16)
    zgate = _mm("in_proj_gate", h, w_gate, "nn", BF16)
    q, k, v, kt = _mla_prep_fwd(zsm, wuq, wukv, vec["q_latent_g"], vec["kv_latent_g"], gqn, gkn, rope, n_b, seq)
    attn, lse, (w_o_s, w_pw_s, w_out_s, w_ff1_s, w_ff2_s) = _attn_fwd(
        q, k, v, [(bf("w_o_mla"), True), (bf("w_pw_out"), True), (bf("w_out"), True), (bf("w_ff1"), True),
                  (bf("w_ff2"), True)], n_b, seq)
    w_o_f = _unstack_cols(w_o_s)
    w_pw_f = _unstack_cols(w_pw_s)
    w_out_f = w_out_s.reshape(d, d)
    w_ff2_f = w_ff2_s.reshape(N_DEV * w_ff2_s.shape[1], d)
    attn2 = attn.reshape(n_rows, N_HEADS * V_HEAD_DIM)
    u3, u1 = _conv_fwd(zglu.reshape(n_b, seq, 2 * CONV_CH), conv_w_f, vec["conv_b"], vec["conv_ln_g"], vec["conv_ln_b"], n_b, seq)
    u32 = u3.reshape(n_rows, CONV_CH)
    ya = _mm("mla_out", attn2, w_o_f, "nn", BF16)
    yb = _mm("conv_out", u32, w_pw_f, "nn", BF16)
    (mrg,) = rw("merge_fwd",
                lambda r, b, cc: ([_sigmoid(r[0].astype(F32)) * r[2].astype(F32) + _sigmoid(r[1].astype(F32)) * r[3].astype(F32)], [], []),
                rows=[(zgate, d, 0), (zgate, d, 1), _full(ya), _full(yb)], outs=[(d, BF16)])
    mixed = _mm("out_proj", mrg, w_out_f, "nn", BF16)

    def mid_fn(r, b, cc):
        x1_ = r[0] + b[0] * r[1].astype(F32)
        return [x1_, _norm_mod(x1_, cc[0], b[1], b[2])], [], []

    x1, h2 = rw("norm2_fwd", mid_fn, rows=[_full(x2), _full(mixed)], bats=[gate1, scale2, shift2],
                consts=[vec["norm2_g"]], outs=[(d, F32), (d, BF16)])
    a = _mm("ff1", h2, w_ff1_s, "nn", BF16, b_stacked=True)
    f = _mm("ff2", a, w_ff2_f, "nn", BF16, a_fn=_relu2)

    def loss_fn(r, b, cc):
        ff = r[1].astype(F32)
        err = r[0] + b[0] * ff - r[2]
        dy_ = err * (1.0 / d)
        sq = jnp.broadcast_to(jnp.sum(err * err, keepdims=True), (1, LANES))
        return [dy_, b[0] * dy_], [jnp.sum(dy_ * ff, axis=0, keepdims=True)], [sq]

    dy, df, dgate2, sq_err = rw("loss", loss_fn, rows=[_full(x1), _full(f), _full(t2)], bats=[gate2],
                                outs=[(d, F32), (d, BF16)], bat_outs=[d], tot_outs=[(1, LANES)])
    loss = lax.psum(sq_err[0, 0] * (0.5 / d), MESH_AXES)

    da = _mm("ff2_bwd", df, w_ff2_f, "nt", BF16, epi=lambda acc, av: acc * 2.0 * jnp.maximum(av, 0.0), epi_in=(a,))
    g_ff2 = _mm("ff2_dw", a, df, "tn", BF16, a_fn=_relu2)
    dh2 = _mm("ff1_bwd", da, w_ff1_s, "nt", F32, b_stacked=True)
    g_ff1_s = _mm("ff1_dw", h2, da, "tn", BF16, out_stacked=True)

    def mid_bwd(r, b, cc):
        dx, dsc, dsh, dg = _norm_mod_bwd(r[0], cc[0], b[0], r[1])
        dx1_ = r[2] + dx
        return [dx1_, b[1] * dx1_], [dsc, dsh, jnp.sum(dx1_ * r[3].astype(F32), axis=0, keepdims=True)], [dg]

    dx1, dmixed, dscale2, dshift2, dgate1, g_norm2 = rw(
        "norm2_bwd", mid_bwd, rows=[_full(x1), _full(dh2), _full(dy), _full(mixed)], bats=[scale2, gate1],
        consts=[vec["norm2_g"]], outs=[(d, F32), (d, BF16)], bat_outs=[d, d, d], tot_outs=[(1, d)])

    dmrg = _mm("out_proj_bwd", dmixed, w_out_f, "nt", BF16)
    g_out = _mm("out_proj_dw", mrg, dmixed, "tn", BF16)

    def merge_bwd(r, b, cc):
        dm, ya_, yb_ = r[0].astype(F32), r[3].astype(F32), r[4].astype(F32)
        sa, sb = _sigmoid(r[1].astype(F32)), _sigmoid(r[2].astype(F32))
        return [dm * ya_ * sa * (1.0 - sa), dm * yb_ * sb * (1.0 - sb), dm * sa, dm * sb], [], []

    dzga, dzgb, dya, dyb = rw("merge_bwd", merge_bwd,
                              rows=[_full(dmrg), (zgate, d, 0), (zgate, d, 1), _full(ya), _full(yb)],
                              outs=[(d, BF16)] * 4)
    dattn = _mm("mla_out_bwd", dya, w_o_f, "nt", BF16)
    g_o = _mm("mla_out_dw", attn2, dya, "tn", F32)
    du3 = _mm("conv_out_bwd", dyb, w_pw_f, "nt", BF16)
    g_pw = _mm("conv_out_dw", u32, dyb, "tn", F32)

    dzglu, g_conv_w, g_conv_b, g_ln_g, g_ln_b = _conv_bwd(
        zglu.reshape(n_b, seq, 2 * CONV_CH), u1, du3.reshape(n_b, seq, CONV_CH), conv_w_f, vec["conv_ln_g"],
        vec["conv_ln_b"], n_b, seq)
    dzglu = dzglu.reshape(n_rows, 2 * CONV_CH)

    dq, dk, dv, (p_ff2, p_ff1, p_out, p_pw, p_o) = _attn_bwd(
        q, k, v, kt, dattn.reshape(n_b, seq, N_HEADS * V_HEAD_DIM), attn, lse,
        [(g_ff2.reshape(N_DEV, -1, d), False), (g_ff1_s, False), (g_out.reshape(N_DEV, -1, d), False),
         (_stack_cols(g_pw, BF16), False), (_stack_cols(g_o, BF16), False)], n_b, seq)
    dzsm, g_wuq, g_wukv, g_gq, g_gkv, g_gqn, g_gkn = _mla_prep_bwd(
        zsm, dq, dk, dv, wuq, wukv, vec["q_latent_g"], vec["kv_latent_g"], gqn, gkn, rope, n_b, seq)

    g_gate_a = _mm("in_proj_gate_dw_a", h, dzga, "tn", F32)
    g_gate_b = _mm("in_proj_gate_dw_b", h, dzgb, "tn", F32)
    g_glu = _mm("in_proj_glu_dw", h, dzglu, "tn", F32)
    g_sm = _mm("in_proj_sm_dw", h, dzsm, "tn", F32)
    g_in = jnp.concatenate([g_sm[:, :OFF_KV], g_sm[:, OFF_KV + QK_NOPE_DIM:OFF_KV + QK_NOPE_DIM + QK_ROPE_DIM], g_glu,
                            g_gate_a, g_gate_b], axis=1)
    g_uq = g_wuq.reshape(Q_LORA, N_HEADS, HEAD_PAD)[:, :, :QK_HEAD_DIM].reshape(Q_LORA, N_HEADS * QK_HEAD_DIM)
    g_wukv = g_wukv.reshape(KV_LORA, 2, N_HEADS, HEAD_PAD)
    g_v = jnp.where(odd, g_wukv[:, 1, :, V_HEAD_DIM:], g_wukv[:, 1, :, :V_HEAD_DIM])
    g_ukv = jnp.concatenate([g_wukv[:, 0, :, :QK_NOPE_DIM], g_v], axis=2).reshape(KV_LORA, -1)

    grad_x, dscale1, dshift1, g_norm1, (p_in, p_uq, p_ukv, p_conv_w) = _in_proj_bwd(
        [(dzga, w_gate[:, :d]), (dzgb, w_gate[:, d:]), (dzglu, w_glu), (dzsm, w_sm)], x2, dx1, scale1, vec["norm1_g"],
        [(_stack_cols(g_in, BF16), False), (_stack_cols(g_uq, BF16), False), (_stack_cols(g_ukv, BF16), False),
         (_stack_cols(g_conv_w[:CONV_WIDTH], F32), False)], seq)

    dmod = jnp.concatenate([dshift1, dscale1, dgate1, dshift2, dscale2, dgate2], axis=1).reshape(n_b, N_DEV, ada_cols)
    (dmod_s,) = _exchange("scatter_dmod", [(dmod.transpose(1, 0, 2), False)])
    dmod_rows = jnp.pad(dmod_s.reshape(all_rows, ada_cols), ((0, pad_rows), (0, 0)))
    g_ada = _mm("ada_dw", c_rows, dmod_rows, "tn", F32, a_fn=_silu)
    (g_b_cols,) = _rowwise("ada_db", lambda r, b, cc: ([], [], [jnp.sum(r[0], axis=0, keepdims=True)]),
                           all_rows + pad_rows, all_rows + pad_rows, rows=[_full(dmod_rows)], tot_outs=[(1, ada_cols)])

    partial_of = {"norm1_g": g_norm1, "q_latent_g": g_gq, "kv_latent_g": g_gkv, "qk_norm_q_g": g_gqn,
                  "qk_norm_k_g": g_gkn, "conv_b": g_conv_b, "conv_ln_g": g_ln_g, "conv_ln_b": g_ln_b, "norm2_g": g_norm2}
    names = [n for n in REPLICATED if n != "b_ada"]
    pieces = [_pad_lanes(partial_of[n], -(-partial_of[n].shape[1] // LANES) * LANES) for n in names] + [g_b_cols]
    widths = [p.shape[1] for p in pieces]
    small = jnp.concatenate(pieces, axis=1)
    small = _pad_lanes(small, -(-small.shape[1] // (8 * LANES)) * 8 * LANES).reshape(-1, LANES)
    (small_s,) = _exchange("gather_small_grads", [(small, True)])
    small_s = small_s.reshape(N_DEV, 1, -1)
    parts = {}
    off = 0
    for n, wd in zip(names, widths):
        parts[n] = small_s[:, :, off:off + vec[n].shape[1]]
        off += wd
    parts["b_ada"] = small_s[:, 0, off:off + ada_cols].reshape(1, 1, N_DEV * ada_cols)
    parts.update({"w_ada": g_ada[None], "w_in": p_in, "w_uq": p_uq, "w_ukv": p_ukv, "w_o_mla": p_o, "conv_w": p_conv_w,
                  "w_pw_out": p_pw, "w_out": p_out, "w_ff1": p_ff1, "w_ff2": p_ff2})

    grad_out, delta_out, m_out, v_out = [], [], [], []
    for n in WEIGHTS:
        shape2 = parts[n].shape[1:]
        g_w, d_w, n_m, n_v = _adamw("adamw_" + n, local[n].reshape(shape2), parts[n], given["m_" + n].reshape(shape2),
                                    given["v_" + n].reshape(shape2))
        full_shape = given[n].shape
        grad_out.append(g_w.reshape(full_shape))
        delta_out.append(d_w.reshape(full_shape))
        m_out.append(n_m.reshape(full_shape))
        v_out.append(n_v.reshape(full_shape))
    return (loss, grad_x.reshape(n_b, seq, d), *grad_out, *delta_out, *m_out, *v_out)
```

```python
import functools

import jax
import jax.numpy as jnp
from jax import lax
from jax.experimental import pallas as pl
from jax.experimental.pallas import tpu as pltpu

F32 = jnp.float32
BF16 = jnp.bfloat16

N_DEV = 8
MESH_AXES = ("x", "y", "c")
EPS = 1e-6
N_HEADS = 8
QK_HEAD_DIM = 96
QK_NOPE_DIM = 64
QK_ROPE_DIM = 32
V_HEAD_DIM = 64
HEAD_PAD = 128
Q_LORA = 256
KV_LORA = 128
CONV_CH = 512
CONV_WIDTH = 31
CONV_HALO = 32
CONV_TAIL = 8
CHUNK = 64
ROPE_THETA = 10000.0
OFF_Q = Q_LORA
OFF_KV = OFF_Q + KV_LORA
OFF_KR = OFF_KV + QK_ROPE_DIM
OFF_GLU = OFF_KR + 2 * CONV_CH
ADA_CHUNKS = 6
ADAM_LR = 0.001
ADAM_B1 = 0.9
ADAM_B2 = 0.999
ADAM_EPS = 1e-08
ADAM_WD = 0.01
ADAM_STEP = 10
LANES = 128
VMEM_LIMIT = 56 * 1024 * 1024
NEG_BIG = -1e30
ATT_HEADS = 4
ATT_TILE = 256
ATT_SCALE = QK_HEAD_DIM ** -0.5
LOG2E = 1.4426950408889634
LN2 = 0.6931471805599453
QK_SCALE = ATT_SCALE * LOG2E
ROWS_PAD = 16

REPLICATED = ("b_ada", "norm1_g", "q_latent_g", "kv_latent_g", "qk_norm_q_g", "qk_norm_k_g", "conv_b", "conv_ln_g",
              "conv_ln_b", "norm2_g")
WEIGHTS = ("w_ada", "b_ada", "norm1_g", "w_in", "q_latent_g", "w_uq", "kv_latent_g", "w_ukv", "qk_norm_q_g",
           "qk_norm_k_g", "w_o_mla", "conv_w", "conv_b", "conv_ln_g", "conv_ln_b", "w_pw_out", "w_out", "norm2_g",
           "w_ff1", "w_ff2")


def _tile(dim, pref):
    if dim <= pref:
        return dim
    t = (pref // LANES) * LANES
    while dim % t:
        t -= LANES
    return t


def _params(semantics):
    return pltpu.CompilerParams(dimension_semantics=semantics, vmem_limit_bytes=VMEM_LIMIT)


def _sigmoid(v):
    return 1.0 / (1.0 + jnp.exp(-v))


def _silu(v):
    return v * _sigmoid(v)


def _relu2(v):
    return jnp.square(jnp.maximum(v, 0.0))


_DIMS = {"nn": (((1,), (0,)), ((), ())), "nt": (((1,), (1,)), ((), ())), "tn": (((0,), (0,)), ((), ()))}


def _mm(name, a, b, mode, out_dtype, *, a_fn=None, epi=None, epi_in=(), b_stacked=False, out_stacked=False,
        tm=1024, tn=1024, tk=1024):
    if b_stacked:
        shard = b.shape[2]
        b_rows, b_cols = b.shape[1], N_DEV * shard
    else:
        b_rows, b_cols = b.shape
    if mode == "nn":
        (m, k), n = a.shape, b_cols
    elif mode == "nt":
        (m, k), n = a.shape, b_rows
    else:
        (k, m), n = a.shape, b_cols
    if out_stacked:
        shard = n // N_DEV
    tm = _tile(m, tm)
    tn = _tile(shard, tn) if (out_stacked or (b_stacked and mode != "nt")) else _tile(n, tn)
    tk = _tile(shard, tk) if (b_stacked and mode == "nt") else _tile(k, tk)
    nk = k // tk
    a_spec = (pl.BlockSpec((tk, tm), lambda i, j, kk: (kk, i)) if mode == "tn"
              else pl.BlockSpec((tm, tk), lambda i, j, kk: (i, kk)))
    if b_stacked and mode == "nt":
        per = shard // tk
        b_spec = pl.BlockSpec((None, tn, tk), lambda i, j, kk: (kk // per, j, kk % per))
    elif b_stacked:
        per = shard // tn
        b_spec = pl.BlockSpec((None, tk, tn), lambda i, j, kk: (j // per, kk, j % per))
    elif mode == "nt":
        b_spec = pl.BlockSpec((tn, tk), lambda i, j, kk: (j, kk))
    else:
        b_spec = pl.BlockSpec((tk, tn), lambda i, j, kk: (kk, j))
    e_spec = pl.BlockSpec((tm, tn), lambda i, j, kk: (i, j))
    if out_stacked:
        per_o = shard // tn
        o_spec = pl.BlockSpec((None, tm, tn), lambda i, j, kk: (j // per_o, i, j % per_o))
        out_shape = jax.ShapeDtypeStruct((N_DEV, m, shard), out_dtype)
    else:
        o_spec = e_spec
        out_shape = jax.ShapeDtypeStruct((m, n), out_dtype)
    n_epi = len(epi_in)

    def body(a_ref, b_ref, *rest):
        epi_refs, o_ref, acc_ref = rest[:n_epi], rest[n_epi], rest[n_epi + 1]
        kk = pl.program_id(2)

        @pl.when(kk == 0)
        def _():
            acc_ref[...] = jnp.zeros_like(acc_ref)

        av = a_ref[...]
        if a_fn is not None:
            av = a_fn(av.astype(F32))
        acc_ref[...] += lax.dot_general(av.astype(BF16), b_ref[...].astype(BF16), _DIMS[mode],
                                        preferred_element_type=F32)

        @pl.when(kk == nk - 1)
        def _():
            acc = acc_ref[...]
            if epi is not None:
                acc = epi(acc, *[r[...].astype(F32) for r in epi_refs])
            o_ref[...] = acc.astype(out_dtype)

    return pl.pallas_call(
        body, name=name, grid=(m // tm, n // tn, nk),
        in_specs=[a_spec, b_spec] + [e_spec] * n_epi, out_specs=o_spec, out_shape=out_shape,
        scratch_shapes=[pltpu.VMEM((tm, tn), F32)],
        compiler_params=_params(("parallel", "parallel", "arbitrary")),
    )(a, b, *epi_in)


def _rowwise(name, fn, n_rows, seq, rows, bats=(), consts=(), outs=(), bat_outs=(), tot_outs=(), tm=256):
    tm = min(tm, seq)
    per_seq = seq // tm
    n_b = n_rows // seq
    nr, nb, nc, no, nbo, nto = len(rows), len(bats), len(consts), len(outs), len(bat_outs), len(tot_outs)

    def body(*refs):
        i = pl.program_id(0)
        r_in = [r[...] for r in refs[:nr]]
        b_in = [r[0] for r in refs[nr:nr + nb]]
        c_in = [r[...] for r in refs[nr + nb:nr + nb + nc]]
        o_refs = refs[nr + nb + nc:nr + nb + nc + no]
        bo_refs = refs[nr + nb + nc + no:nr + nb + nc + no + nbo]
        to_refs = refs[nr + nb + nc + no + nbo:]
        o_val, bo_val, to_val = fn(r_in, b_in, c_in)
        for r, v in zip(o_refs, o_val):
            r[...] = v.astype(r.dtype)
        if nbo:
            @pl.when(i % per_seq == 0)
            def _():
                for r in bo_refs:
                    r[...] = jnp.zeros_like(r)

            for r, v in zip(bo_refs, bo_val):
                r[0] += v
        if nto:
            @pl.when(i == 0)
            def _():
                for r in to_refs:
                    r[...] = jnp.zeros_like(r)

            for r, v in zip(to_refs, to_val):
                r[...] += v

    in_specs = [pl.BlockSpec((tm, w), functools.partial(lambda cb, i: (i, cb), cb)) for (_, w, cb) in rows]
    in_specs += [pl.BlockSpec((1, 1, bt.shape[2]), lambda i: (i // per_seq, 0, 0)) for bt in bats]
    in_specs += [pl.BlockSpec(ct.shape, lambda i: (0, 0)) for ct in consts]
    out_specs = [pl.BlockSpec((tm, w), lambda i: (i, 0)) for (w, _) in outs]
    out_specs += [pl.BlockSpec((1, 1, w), lambda i: (i // per_seq, 0, 0)) for w in bat_outs]
    out_specs += [pl.BlockSpec(shp, lambda i: (0, 0)) for shp in tot_outs]
    out_shape = [jax.ShapeDtypeStruct((n_rows, w), dt) for (w, dt) in outs]
    out_shape += [jax.ShapeDtypeStruct((n_b, 1, w), F32) for w in bat_outs]
    out_shape += [jax.ShapeDtypeStruct(shp, F32) for shp in tot_outs]
    res = pl.pallas_call(
        body, name=name, grid=(n_rows // tm,), in_specs=in_specs, out_specs=out_specs, out_shape=out_shape,
        compiler_params=_params(("arbitrary",)),
    )(*[r[0] for r in rows], *bats, *consts)
    return res


def _full(arr):
    return (arr, arr.shape[1], 0)


def _norm_mod(x, g, scale, shift):
    r = lax.rsqrt(jnp.mean(x * x, axis=-1, keepdims=True) + EPS)
    xh = x * r
    return xh * g * (1.0 + scale) + shift


def _norm_mod_bwd(x, g, scale, dh):
    r = lax.rsqrt(jnp.mean(x * x, axis=-1, keepdims=True) + EPS)
    xh = x * r
    dn = dh * (1.0 + scale)
    dxh = dn * g
    dx = r * (dxh - xh * jnp.mean(dxh * xh, axis=-1, keepdims=True))
    dscale = jnp.sum(dh * xh * g, axis=0, keepdims=True)
    dshift = jnp.sum(dh, axis=0, keepdims=True)
    dg = jnp.sum(dn * xh, axis=0, keepdims=True)
    return dx, dscale, dshift, dg


def _rms(v, g):
    r = lax.rsqrt(jnp.mean(v * v, axis=-1, keepdims=True) + EPS)
    return v * r * g


def _rms_bwd(v, g, dy):
    r = lax.rsqrt(jnp.mean(v * v, axis=-1, keepdims=True) + EPS)
    vh = v * r
    dvh = dy * g
    dv = r * (dvh - vh * jnp.mean(dvh * vh, axis=-1, keepdims=True))
    return dv, jnp.sum(dy * vh, axis=0, keepdims=True)


def _head_norm(v, g):
    r = lax.rsqrt(jnp.sum(v * v, axis=-1, keepdims=True) * (1.0 / QK_HEAD_DIM) + EPS)
    return v * r * g


def _head_norm_bwd(v, g, dy):
    r = lax.rsqrt(jnp.sum(v * v, axis=-1, keepdims=True) * (1.0 / QK_HEAD_DIM) + EPS)
    vh = v * r
    dvh = dy * g
    dv = r * (dvh - vh * (jnp.sum(dvh * vh, axis=-1, keepdims=True) * (1.0 / QK_HEAD_DIM)))
    return dv, jnp.sum(dy * vh, axis=0, keepdims=True)


def _rope(v, cos, sin_lo, sin_hi):
    return v * cos + pltpu.roll(v, HEAD_PAD - 16, 1) * sin_lo + pltpu.roll(v, 16, 1) * sin_hi


def _rope_bwd(g, cos, sin_lo, sin_hi):
    return g * cos + pltpu.roll(g * sin_lo, 16, 1) + pltpu.roll(g * sin_hi, HEAD_PAD - 16, 1)


def _mla_prep_fwd(zsm, wuq, wukv, gq, gkv, gqn, gkn, rope, n_b, seq):
    n_rows = n_b * seq
    tm = min(ATT_TILE, seq)
    per_seq = seq // tm
    k_cols = N_HEADS * HEAD_PAD

    def body(z_ref, wuq_ref, wukv_ref, gq_ref, gkv_ref, gqn_ref, gkn_ref, c_ref, s1_ref, s2_ref,
             q_ref, k_ref, v_ref, kt_ref):
        z = z_ref[...]
        qn = _rms(z[:, :Q_LORA], gq_ref[...]).astype(BF16)
        kvn = _rms(z[:, Q_LORA:Q_LORA + KV_LORA], gkv_ref[...]).astype(BF16)
        krp = z[:, Q_LORA + KV_LORA:]
        cos, s1, s2 = c_ref[...], s1_ref[...], s2_ref[...]
        q_all = jnp.dot(qn, wuq_ref[...], preferred_element_type=F32)
        kv_all = jnp.dot(kvn, wukv_ref[...], preferred_element_type=F32)
        for h in range(N_HEADS):
            cols = slice(h * HEAD_PAD, (h + 1) * HEAD_PAD)
            q_ref[0, h] = (_rope(_head_norm(q_all[:, cols], gqn_ref[...]), cos, s1, s2) * QK_SCALE).astype(BF16)
            kh = _rope(_head_norm(kv_all[:, cols] + krp, gkn_ref[...]), cos, s1, s2)
            k_ref[0, h] = kh.astype(BF16)
            kt_ref[0, h, 0] = kh.T.astype(BF16)
            v_ref[0, h] = kv_all[:, k_cols + h * HEAD_PAD:k_cols + (h + 1) * HEAD_PAD].astype(BF16)

    whole2 = lambda arr: pl.BlockSpec(arr.shape, lambda i: (0, 0))
    rope_spec = pl.BlockSpec((tm, HEAD_PAD), lambda i: (i % per_seq, 0))
    head_spec = pl.BlockSpec((1, N_HEADS, tm, HEAD_PAD), lambda i: (i // per_seq, 0, i % per_seq, 0))
    head_shape = jax.ShapeDtypeStruct((n_b, N_HEADS, seq, HEAD_PAD), BF16)
    t_spec = pl.BlockSpec((1, N_HEADS, 1, HEAD_PAD, tm), lambda i: (i // per_seq, 0, i % per_seq, 0, 0))
    t_shape = jax.ShapeDtypeStruct((n_b, N_HEADS, per_seq, HEAD_PAD, tm), BF16)
    return pl.pallas_call(
        body, name="mla_prep_fwd", grid=(n_rows // tm,),
        in_specs=[pl.BlockSpec((tm, 512), lambda i: (i, 0)), whole2(wuq), whole2(wukv),
                  whole2(gq), whole2(gkv), whole2(gqn), whole2(gkn), rope_spec, rope_spec, rope_spec],
        out_specs=[head_spec] * 3 + [t_spec], out_shape=[head_shape] * 3 + [t_shape],
        compiler_params=_params(("parallel",)),
    )(zsm, wuq, wukv, gq, gkv, gqn, gkn, *rope)


def _mla_prep_bwd(zsm, dq, dk, dv, wuq, wukv, gq, gkv, gqn, gkn, rope, n_b, seq):
    n_rows = n_b * seq
    tm = min(256, seq)
    per_seq = seq // tm
    tn_dims = _DIMS["tn"]
    nt_dims = _DIMS["nt"]
    k_cols = N_HEADS * HEAD_PAD

    def body(z_ref, dq_ref, dk_ref, dv_ref, wuq_ref, wukv_ref, gq_ref, gkv_ref, gqn_ref, gkn_ref,
             c_ref, s1_ref, s2_ref, dz_ref, dwuq_ref, dwukv_ref, dgq_ref, dgkv_ref, dgqn_ref, dgkn_ref):
        @pl.when(pl.program_id(0) == 0)
        def _():
            for r in (dwuq_ref, dwukv_ref, dgq_ref, dgkv_ref, dgqn_ref, dgkn_ref):
                r[...] = jnp.zeros_like(r)

        z = z_ref[...]
        zq, zkv, krp = z[:, :Q_LORA], z[:, Q_LORA:Q_LORA + KV_LORA], z[:, Q_LORA + KV_LORA:]
        qn = _rms(zq, gq_ref[...]).astype(BF16)
        kvn = _rms(zkv, gkv_ref[...]).astype(BF16)
        cos, s1, s2 = c_ref[...], s1_ref[...], s2_ref[...]
        lane = lax.broadcasted_iota(jnp.int32, (tm, HEAD_PAD), 1)
        rope_lanes = (lane >= QK_NOPE_DIM) & (lane < QK_HEAD_DIM)
        q_all = jnp.dot(qn, wuq_ref[...], preferred_element_type=F32)
        k_all = jnp.dot(kvn, wukv_ref[:, :k_cols], preferred_element_type=F32)
        dkrp = jnp.zeros((tm, HEAD_PAD), F32)
        dgqn = jnp.zeros((1, HEAD_PAD), F32)
        dgkn = jnp.zeros((1, HEAD_PAD), F32)
        dq_heads, dk_heads = [], []
        for h in range(N_HEADS):
            cols = slice(h * HEAD_PAD, (h + 1) * HEAD_PAD)
            dqh, dg = _head_norm_bwd(q_all[:, cols], gqn_ref[...],
                                     _rope_bwd(dq_ref[0, h].astype(F32) * ATT_SCALE, cos, s1, s2))
            dgqn += dg
            dq_heads.append(dqh.astype(BF16))
            dkh, dg = _head_norm_bwd(k_all[:, cols] + krp, gkn_ref[...], _rope_bwd(dk_ref[0, h].astype(F32), cos, s1, s2))
            dgkn += dg
            dkrp += jnp.where(rope_lanes, dkh, 0.0)
            dk_heads.append(dkh.astype(BF16))
        dq_all = jnp.concatenate(dq_heads, axis=1)
        dkv_all = jnp.concatenate(dk_heads + [dv_ref[0, h] for h in range(N_HEADS)], axis=1)
        dwuq_ref[...] += lax.dot_general(qn, dq_all, tn_dims, preferred_element_type=F32)
        dqn = lax.dot_general(dq_all, wuq_ref[...], nt_dims, preferred_element_type=F32)
        dwukv_ref[...] += lax.dot_general(kvn, dkv_all, tn_dims, preferred_element_type=F32)
        dkvn = lax.dot_general(dkv_all, wukv_ref[...], nt_dims, preferred_element_type=F32)
        dzq, dg = _rms_bwd(zq, gq_ref[...], dqn)
        dgq_ref[...] += dg
        dzkv, dg = _rms_bwd(zkv, gkv_ref[...], dkvn)
        dgkv_ref[...] += dg
        dgqn_ref[...] += dgqn
        dgkn_ref[...] += dgkn
        dz_ref[:, :Q_LORA] = dzq.astype(dz_ref.dtype)
        dz_ref[:, Q_LORA:Q_LORA + KV_LORA] = dzkv.astype(dz_ref.dtype)
        dz_ref[:, Q_LORA + KV_LORA:] = dkrp.astype(dz_ref.dtype)

    whole2 = lambda arr: pl.BlockSpec(arr.shape, lambda i: (0, 0))
    rope_spec = pl.BlockSpec((tm, HEAD_PAD), lambda i: (i % per_seq, 0))
    head_spec = pl.BlockSpec((1, N_HEADS, tm, HEAD_PAD), lambda i: (i // per_seq, 0, i % per_seq, 0))
    row_spec = pl.BlockSpec((tm, 512), lambda i: (i, 0))
    return pl.pallas_call(
        body, name="mla_prep_bwd", grid=(n_rows // tm,),
        in_specs=[row_spec, head_spec, head_spec, head_spec, whole2(wuq), whole2(wukv),
                  whole2(gq), whole2(gkv), whole2(gqn), whole2(gkn), rope_spec, rope_spec, rope_spec],
        out_specs=[row_spec, whole2(wuq), whole2(wukv), whole2(gq), whole2(gkv), whole2(gqn), whole2(gkn)],
        out_shape=[jax.ShapeDtypeStruct((n_rows, 512), BF16),
                   jax.ShapeDtypeStruct(wuq.shape, F32), jax.ShapeDtypeStruct(wukv.shape, F32),
                   jax.ShapeDtypeStruct(gq.shape, F32), jax.ShapeDtypeStruct(gkv.shape, F32),
                   jax.ShapeDtypeStruct(gqn.shape, F32), jax.ShapeDtypeStruct(gkn.shape, F32)],
        compiler_params=_params(("arbitrary",)),
    )(zsm, dq, dk, dv, wuq, wukv, gq, gkv, gqn, gkn, *rope)


HBM_SPEC = pl.BlockSpec(memory_space=pltpu.HBM)


def _xchg_out_shapes(bufs):
    return [jax.ShapeDtypeStruct((N_DEV,) + (a.shape if gather else a.shape[1:]), a.dtype) for a, gather in bufs]


def _xchg_scratch(n_buf):
    return [pltpu.SemaphoreType.DMA((n_buf * (N_DEV - 1),)), pltpu.SemaphoreType.DMA((n_buf * (N_DEV - 1),)),
            pltpu.SemaphoreType.DMA((n_buf,))]


def _xchg_copies(src_refs, dst_refs, gathers, send_sems, recv_sems, local_sems):
    x, y, c = lax.axis_index("x"), lax.axis_index("y"), lax.axis_index("c")
    me = 4 * x + 2 * y + c
    local, starts, arrivals = [], [], []
    for bi, (src, dst, gather) in enumerate(zip(src_refs, dst_refs, gathers)):
        local.append(pltpu.make_async_copy(src if gather else src.at[me], dst.at[me], local_sems.at[bi]))
        for kk in range(1, N_DEV):
            px = 1 - x if kk & 4 else x
            py = 1 - y if kk & 2 else y
            pc = 1 - c if kk & 1 else c
            pid = 4 * px + 2 * py + pc
            sem = bi * (N_DEV - 1) + kk - 1
            starts.append(pltpu.make_async_remote_copy(
                src_ref=src if gather else src.at[pid], dst_ref=dst.at[me],
                send_sem=send_sems.at[sem], recv_sem=recv_sems.at[sem],
                device_id=(px, py, pc), device_id_type=pl.DeviceIdType.MESH))
            arrivals.append(pltpu.make_async_remote_copy(
                src_ref=src if gather else src.at[me], dst_ref=dst.at[pid],
                send_sem=send_sems.at[sem], recv_sem=recv_sems.at[sem],
                device_id=(px, py, pc), device_id_type=pl.DeviceIdType.MESH))
    return local, starts, arrivals


def _xchg_start(copies):
    local, sends, _ = copies
    for cp in local + sends:
        cp.start()


def _xchg_finish(copies):
    local, sends, arrivals = copies
    for cp in arrivals:
        cp.wait_recv()
    for cp in sends:
        cp.wait_send()
    for cp in local:
        cp.wait()


def _exchange(name, bufs):
    n_buf = len(bufs)
    gathers = [g for _, g in bufs]

    def body(*refs):
        srcs, dsts = refs[:n_buf], refs[n_buf:2 * n_buf]
        copies = _xchg_copies(srcs, dsts, gathers, *refs[2 * n_buf:])
        _xchg_start(copies)
        _xchg_finish(copies)

    return pl.pallas_call(
        body, name=name, out_shape=_xchg_out_shapes(bufs),
        in_specs=[HBM_SPEC] * n_buf, out_specs=[HBM_SPEC] * n_buf, scratch_shapes=_xchg_scratch(n_buf),
    )(*[a for a, _ in bufs])


def _chunk_mask(t, keys_first):
    key = lax.broadcasted_iota(jnp.int32, (t, t), 0 if keys_first else 1) // CHUNK
    query = lax.broadcasted_iota(jnp.int32, (t, t), 1 if keys_first else 0) // CHUNK
    return query >= key


def _grid_ends(grid):
    ids = [pl.program_id(ax) for ax in range(len(grid))]
    first = functools.reduce(jnp.logical_and, [i == 0 for i in ids])
    last = functools.reduce(jnp.logical_and, [i == g - 1 for i, g in zip(ids, grid)])
    return first, last


def _attn_fwd(q, k, v, bufs, n_b, seq):
    tq = min(ATT_TILE, seq)
    nq = seq // tq
    nt_dims = _DIMS["nt"]
    hpb = ATT_HEADS
    grid = (n_b, N_HEADS // hpb, nq)
    n_buf = len(bufs)
    gathers = [g for _, g in bufs]
    sum_lane = [HEAD_PAD - 1 if hh % 2 == 0 else 0 for hh in range(hpb)]

    def body(q_ref, k_ref, v_ref, *rest):
        srcs, (o_ref, lse_ref), dsts = rest[:n_buf], rest[n_buf:n_buf + 2], rest[n_buf + 2:2 * n_buf + 2]
        s_ref = rest[2 * n_buf + 2]
        copies = _xchg_copies(srcs, dsts, gathers, *rest[2 * n_buf + 3:])
        first, last = _grid_ends(grid)
        pl.when(first)(functools.partial(_xchg_start, copies))

        qi = pl.program_id(2)
        mask = _chunk_mask(tq, keys_first=False)
        lane_row = lax.broadcasted_iota(jnp.int32, (1, HEAD_PAD), 1)
        ones = [(lane_row == sum_lane[hh]).astype(BF16) for hh in range(hpb)]
        qs = [q_ref[0, hh] for hh in range(hpb)]

        def score_step(j, tops, masked):
            rows = pl.ds(pl.multiple_of(j * tq, tq), tq)
            out = []
            for hh in range(hpb):
                s = lax.dot_general(qs[hh], k_ref[0, hh, rows, :], nt_dims, preferred_element_type=F32)
                if masked:
                    s = jnp.where(mask, s, NEG_BIG)
                s_ref[hh, j] = s
                out.append(jnp.maximum(tops[hh], s))
            return tuple(out)

        tops = tuple(jnp.full((tq, tq), NEG_BIG, F32) for _ in range(hpb))
        tops = lax.fori_loop(0, qi, functools.partial(score_step, masked=False), tops)
        tops = score_step(qi, tops, True)
        ms = [jnp.max(top, axis=-1, keepdims=True) for top in tops]

        def value_step(j, accs):
            rows = pl.ds(pl.multiple_of(j * tq, tq), tq)
            out = []
            for hh in range(hpb):
                p = jnp.exp2(s_ref[hh, j] - ms[hh]).astype(BF16)
                out.append(accs[hh] + jnp.dot(p, v_ref[0, hh, rows, :] + ones[hh], preferred_element_type=F32))
            return tuple(out)

        accs = tuple(jnp.zeros((tq, HEAD_PAD), F32) for _ in range(hpb))
        accs = lax.fori_loop(0, qi + 1, value_step, accs)
        carry = list(zip(ms, accs))
        lane = lax.broadcasted_iota(jnp.int32, (tq, HEAD_PAD), 1)
        for pair in range(hpb // 2):
            outs = []
            for hh in (2 * pair, 2 * pair + 1):
                m, acc = carry[hh]
                l = jnp.sum(jnp.where(lane == sum_lane[hh], acc, 0.0), axis=-1, keepdims=True)
                outs.append(acc * (1.0 / l))
                lse_ref[0, hh] = jnp.broadcast_to(m + jnp.log2(l), (tq, HEAD_PAD)).T[0:8, :]
            o_ref[0, :, pair * HEAD_PAD:(pair + 1) * HEAD_PAD] = jnp.where(lane < V_HEAD_DIM, outs[0], outs[1]).astype(BF16)

        pl.when(last)(functools.partial(_xchg_finish, copies))

    kv_spec = pl.BlockSpec((1, hpb, seq, HEAD_PAD), lambda b, hb, i: (b, hb, 0, 0))
    q_spec = pl.BlockSpec((1, hpb, tq, HEAD_PAD), lambda b, hb, i: (b, hb, i, 0))
    res = pl.pallas_call(
        body, name="attn_fwd", grid=grid,
        in_specs=[q_spec, kv_spec, kv_spec] + [HBM_SPEC] * n_buf,
        out_specs=[pl.BlockSpec((1, tq, hpb * V_HEAD_DIM), lambda b, hb, i: (b, i, hb)),
                   pl.BlockSpec((1, hpb, 8, tq), lambda b, hb, i: (b, hb, 0, i))] + [HBM_SPEC] * n_buf,
        out_shape=[jax.ShapeDtypeStruct((n_b, seq, N_HEADS * V_HEAD_DIM), BF16),
                   jax.ShapeDtypeStruct((n_b, N_HEADS, 8, seq), F32)] + _xchg_out_shapes(bufs),
        scratch_shapes=[pltpu.VMEM((hpb, nq, tq, tq), F32)] + _xchg_scratch(n_buf),
        compiler_params=_params(("arbitrary", "arbitrary", "arbitrary")),
    )(q, k, v, *[a for a, _ in bufs])
    return res[0], res[1], res[2:]


def _attn_bwd(q, k, v, kt, do, o, lse, bufs, n_b, seq):
    tq = min(ATT_TILE, seq)
    nq = seq // tq
    nt_dims = _DIMS["nt"]
    hpb = ATT_HEADS
    grid = (n_b, N_HEADS // hpb, nq)
    n_buf = len(bufs)
    gathers = [g for _, g in bufs]

    def body(q_ref, k_ref, v_ref, kt_ref, do_ref, o_ref, lse_ref, *rest):
        srcs, (dq_ref, dk_ref, dv_ref), dsts = rest[:n_buf], rest[n_buf:n_buf + 3], rest[n_buf + 3:2 * n_buf + 3]
        dk_acc, dv_acc = rest[2 * n_buf + 3:2 * n_buf + 5]
        copies = _xchg_copies(srcs, dsts, gathers, *rest[2 * n_buf + 5:])
        first, last = _grid_ends(grid)
        pl.when(first)(functools.partial(_xchg_start, copies))

        qi = pl.program_id(2)

        @pl.when(qi == 0)
        def _():
            dk_acc[...] = jnp.zeros_like(dk_acc)
            dv_acc[...] = jnp.zeros_like(dv_acc)

        mask = _chunk_mask(tq, keys_first=True)
        lane = lax.broadcasted_iota(jnp.int32, (tq, HEAD_PAD), 1)
        qs, dos, deltas, lses = [], [], [], []
        for hh in range(hpb):
            cols = slice((hh // 2) * HEAD_PAD, (hh // 2 + 1) * HEAD_PAD)
            do_pair = do_ref[0, :, cols]
            prod = do_pair.astype(F32) * o_ref[0, :, cols].astype(F32)
            delta = jnp.sum(jnp.where(lane // V_HEAD_DIM == hh % 2, prod, 0.0), axis=-1, keepdims=True)
            qs.append(q_ref[0, hh])
            dos.append(do_pair)
            deltas.append(jnp.broadcast_to(delta, (tq, HEAD_PAD)).T[0:1, :])
            lses.append(lse_ref[0, hh][0:1, :])

        def step(j, dqs, masked):
            rows = pl.ds(pl.multiple_of(j * tq, tq), tq)
            out = []
            for hh in range(hpb):
                s = lax.dot_general(k_ref[0, hh, rows, :], qs[hh], nt_dims, preferred_element_type=F32)
                p = jnp.exp2(s - lses[hh])
                if masked:
                    p = jnp.where(mask, p, 0.0)
                dv_acc[hh, rows, :] += jnp.dot(p.astype(BF16), dos[hh], preferred_element_type=F32)
                dp = lax.dot_general(v_ref[0, hh, rows, :], dos[hh], nt_dims, preferred_element_type=F32)
                ds = (p * (dp - deltas[hh])).astype(BF16)
                dk_acc[hh, rows, :] += jnp.dot(ds, qs[hh], preferred_element_type=F32)
                out.append(dqs[hh] + jnp.dot(kt_ref[0, hh, j], ds, preferred_element_type=F32))
            return tuple(out)

        dqs = tuple(jnp.zeros((HEAD_PAD, tq), F32) for _ in range(hpb))
        dqs = lax.fori_loop(0, qi, functools.partial(step, masked=False), dqs)
        dqs = step(qi, dqs, True)
        for hh in range(hpb):
            dq_ref[0, hh] = dqs[hh].T.astype(BF16)

        @pl.when(qi == nq - 1)
        def _():
            dk_ref[0] = (dk_acc[...] * LN2).astype(BF16)
            dv_ref[0] = dv_acc[...].astype(BF16)

        pl.when(last)(functools.partial(_xchg_finish, copies))

    full_spec = pl.BlockSpec((1, hpb, seq, HEAD_PAD), lambda b, hb, i: (b, hb, 0, 0))
    t_spec = pl.BlockSpec((1, hpb, nq, HEAD_PAD, tq), lambda b, hb, i: (b, hb, 0, 0, 0))
    q_spec = pl.BlockSpec((1, hpb, tq, HEAD_PAD), lambda b, hb, i: (b, hb, i, 0))
    o_spec = pl.BlockSpec((1, tq, hpb * V_HEAD_DIM), lambda b, hb, i: (b, i, hb))
    lse_spec = pl.BlockSpec((1, hpb, 8, tq), lambda b, hb, i: (b, hb, 0, i))
    head_shape = jax.ShapeDtypeStruct((n_b, N_HEADS, seq, HEAD_PAD), BF16)
    res = pl.pallas_call(
        body, name="attn_bwd", grid=grid,
        in_specs=[q_spec, full_spec, full_spec, t_spec, o_spec, o_spec, lse_spec] + [HBM_SPEC] * n_buf,
        out_specs=[q_spec, full_spec, full_spec] + [HBM_SPEC] * n_buf,
        out_shape=[head_shape] * 3 + _xchg_out_shapes(bufs),
        scratch_shapes=[pltpu.VMEM((hpb, seq, HEAD_PAD), F32), pltpu.VMEM((hpb, seq, HEAD_PAD), F32)]
        + _xchg_scratch(n_buf),
        compiler_params=_params(("arbitrary", "arbitrary", "arbitrary")),
    )(q, k, v, kt, do, o, lse, *[a for a, _ in bufs])
    return res[0], res[1], res[2], res[3:]


def _in_proj_bwd(parts, x2, dx1, scale, g, bufs, seq):
    n_rows, d = x2.shape
    tm = min(512, seq)
    per_seq = seq // tm
    grid = (n_rows // tm,)
    n_part, n_buf = len(parts), len(bufs)
    gathers = [gt for _, gt in bufs]
    nt_dims = _DIMS["nt"]

    def body(*refs):
        dz_refs, w_refs = refs[:n_part], refs[n_part:2 * n_part]
        x_ref, dx1_ref, sc_ref, g_ref = refs[2 * n_part:2 * n_part + 4]
        srcs = refs[2 * n_part + 4:2 * n_part + 4 + n_buf]
        gx_ref, dsc_ref, dsh_ref, dg_ref = refs[2 * n_part + 4 + n_buf:2 * n_part + 8 + n_buf]
        dsts = refs[2 * n_part + 8 + n_buf:2 * n_part + 8 + 2 * n_buf]
        copies = _xchg_copies(srcs, dsts, gathers, *refs[2 * n_part + 8 + 2 * n_buf:])
        first, last = _grid_ends(grid)
        pl.when(first)(functools.partial(_xchg_start, copies))

        i = pl.program_id(0)
        dh = None
        for dz_ref, w_ref in zip(dz_refs, w_refs):
            term = lax.dot_general(dz_ref[...], w_ref[...], nt_dims, preferred_element_type=F32)
            dh = term if dh is None else dh + term
        dx, dsc, dsh, dg = _norm_mod_bwd(x_ref[...], g_ref[...], sc_ref[0], dh)
        gx_ref[...] = dx1_ref[...] + dx

        @pl.when(i % per_seq == 0)
        def _():
            dsc_ref[...] = jnp.zeros_like(dsc_ref)
            dsh_ref[...] = jnp.zeros_like(dsh_ref)

        @pl.when(i == 0)
        def _():
            dg_ref[...] = jnp.zeros_like(dg_ref)

        dsc_ref[0] += dsc
        dsh_ref[0] += dsh
        dg_ref[...] += dg
        pl.when(last)(functools.partial(_xchg_finish, copies))

    row = lambda width: pl.BlockSpec((tm, width), lambda i: (i, 0))
    bat = pl.BlockSpec((1, 1, d), lambda i: (i // per_seq, 0, 0))
    whole = lambda arr: pl.BlockSpec(arr.shape, lambda i: (0, 0))
    n_b = n_rows // seq
    res = pl.pallas_call(
        body, name="in_proj_bwd", grid=grid,
        in_specs=[row(dz.shape[1]) for dz, _ in parts] + [whole(w) for _, w in parts]
        + [row(d), row(d), bat, whole(g)] + [HBM_SPEC] * n_buf,
        out_specs=[row(d), bat, bat, whole(g)] + [HBM_SPEC] * n_buf,
        out_shape=[jax.ShapeDtypeStruct((n_rows, d), F32), jax.ShapeDtypeStruct((n_b, 1, d), F32),
                   jax.ShapeDtypeStruct((n_b, 1, d), F32), jax.ShapeDtypeStruct(g.shape, F32)] + _xchg_out_shapes(bufs),
        scratch_shapes=_xchg_scratch(n_buf),
        compiler_params=_params(("arbitrary",)),
    )(*[dz for dz, _ in parts], *[w for _, w in parts], x2, dx1, scale, g, *[a for a, _ in bufs])
    return res[0], res[1], res[2], res[3], res[4:]


def _ln_silu(u1, g, b):
    mu = jnp.mean(u1, axis=-1, keepdims=True)
    uc = u1 - mu
    r = lax.rsqrt(jnp.mean(uc * uc, axis=-1, keepdims=True) + EPS)
    y = uc * r * g + b
    return y * _sigmoid(y)


def _conv_fill_glu(z_ref, u0_ref, seq, tile):
    u0_ref[0:CONV_HALO, :] = jnp.zeros((CONV_HALO, CONV_CH), F32)
    u0_ref[CONV_HALO + seq:CONV_HALO + seq + CONV_TAIL, :] = jnp.zeros((CONV_TAIL, CONV_CH), F32)
    for t in range(seq // tile):
        zt = z_ref[0, t * tile:(t + 1) * tile, :].astype(F32)
        u0_ref[CONV_HALO + t * tile:CONV_HALO + (t + 1) * tile, :] = zt[:, :CONV_CH] * _sigmoid(zt[:, CONV_CH:])


def _conv_windows(ref, t, tile):
    return [ref[t * tile + b:t * tile + b + tile + CONV_HALO, :] for b in range(8)]


def _conv_tap(views, offset, tile):
    return views[offset % 8][8 * (offset // 8):8 * (offset // 8) + tile]


def _conv_tile(u0_ref, w_ref, b_ref, t, tile):
    views = _conv_windows(u0_ref, t, tile)
    acc = jnp.broadcast_to(b_ref[...], (tile, CONV_CH))
    for kk in range(CONV_WIDTH):
        acc = acc + w_ref[kk:kk + 1, :] * _conv_tap(views, kk + CONV_HALO - (CONV_WIDTH - 1), tile)
    return acc


def _conv_fwd(zglu, conv_w, conv_b, ln_g, ln_b, n_b, seq):
    tile = min(256, seq)

    def body(z_ref, w_ref, b_ref, g_ref, bb_ref, o_ref, u1_ref, u0_ref):
        _conv_fill_glu(z_ref, u0_ref, seq, tile)
        for t in range(seq // tile):
            u1 = _conv_tile(u0_ref, w_ref, b_ref, t, tile)
            u1_ref[0, t * tile:(t + 1) * tile, :] = u1
            o_ref[0, t * tile:(t + 1) * tile, :] = _ln_silu(u1, g_ref[...], bb_ref[...]).astype(BF16)

    whole2 = lambda arr: pl.BlockSpec(arr.shape, lambda b: (0, 0))
    seq_spec = pl.BlockSpec((1, seq, CONV_CH), lambda b: (b, 0, 0))
    return pl.pallas_call(
        body, name="conv_fwd", grid=(n_b,),
        in_specs=[pl.BlockSpec((1, seq, 2 * CONV_CH), lambda b: (b, 0, 0)), whole2(conv_w), whole2(conv_b),
                  whole2(ln_g), whole2(ln_b)],
        out_specs=[seq_spec, seq_spec],
        out_shape=[jax.ShapeDtypeStruct((n_b, seq, CONV_CH), BF16), jax.ShapeDtypeStruct((n_b, seq, CONV_CH), F32)],
        scratch_shapes=[pltpu.VMEM((seq + CONV_HALO + CONV_TAIL, CONV_CH), F32)],
        compiler_params=_params(("parallel",)),
    )(zglu, conv_w, conv_b, ln_g, ln_b)


def _conv_bwd(zglu, u1_saved, du3, conv_w, ln_g, ln_b, n_b, seq):
    tile = min(256, seq)
    n_t = seq // tile

    def body(z_ref, u1_ref, du3_ref, w_ref, g_ref, bb_ref, dz_ref, dw_ref, db_ref, dg_ref, dbb_ref, u0_ref, du1_ref):
        @pl.when(pl.program_id(0) == 0)
        def _():
            for r in (dw_ref, db_ref, dg_ref, dbb_ref):
                r[...] = jnp.zeros_like(r)

        _conv_fill_glu(z_ref, u0_ref, seq, tile)
        du1_ref[seq:seq + CONV_HALO + CONV_TAIL, :] = jnp.zeros((CONV_HALO + CONV_TAIL, CONV_CH), F32)
        g = g_ref[...]
        for t in range(n_t):
            u1 = u1_ref[0, t * tile:(t + 1) * tile, :]
            mu = jnp.mean(u1, axis=-1, keepdims=True)
            uc = u1 - mu
            r = lax.rsqrt(jnp.mean(uc * uc, axis=-1, keepdims=True) + EPS)
            xh = uc * r
            y = xh * g + bb_ref[...]
            sg = _sigmoid(y)
            dy = du3_ref[0, t * tile:(t + 1) * tile, :].astype(F32) * (sg * (1.0 + y * (1.0 - sg)))
            dg_ref[...] += jnp.sum(dy * xh, axis=0, keepdims=True)
            dbb_ref[...] += jnp.sum(dy, axis=0, keepdims=True)
            dxh = dy * g
            du1 = r * (dxh - jnp.mean(dxh, axis=-1, keepdims=True) - xh * jnp.mean(dxh * xh, axis=-1, keepdims=True))
            db_ref[...] += jnp.sum(du1, axis=0, keepdims=True)
            du1_ref[t * tile:(t + 1) * tile, :] = du1
        for t in range(n_t):
            du1 = du1_ref[t * tile:(t + 1) * tile, :]
            du0 = jnp.zeros((tile, CONV_CH), F32)
            u0_views = _conv_windows(u0_ref, t, tile)
            du1_views = _conv_windows(du1_ref, t, tile)
            for kk in range(CONV_WIDTH):
                du0 = du0 + w_ref[kk:kk + 1, :] * _conv_tap(du1_views, CONV_WIDTH - 1 - kk, tile)
                u0_tap = _conv_tap(u0_views, kk + CONV_HALO - (CONV_WIDTH - 1), tile)
                dw_ref[kk:kk + 1, :] += jnp.sum(du1 * u0_tap, axis=0, keepdims=True)
            zt = z_ref[0, t * tile:(t + 1) * tile, :].astype(F32)
            ga, sb = zt[:, :CONV_CH], _sigmoid(zt[:, CONV_CH:])
            dz_ref[0, t * tile:(t + 1) * tile, :CONV_CH] = (du0 * sb).astype(BF16)
            dz_ref[0, t * tile:(t + 1) * tile, CONV_CH:] = (du0 * ga * sb * (1.0 - sb)).astype(BF16)

    whole2 = lambda arr: pl.BlockSpec(arr.shape, lambda b: (0, 0))
    z_spec = pl.BlockSpec((1, seq, 2 * CONV_CH), lambda b: (b, 0, 0))
    seq_spec = pl.BlockSpec((1, seq, CONV_CH), lambda b: (b, 0, 0))
    return pl.pallas_call(
        body, name="conv_bwd", grid=(n_b,),
        in_specs=[z_spec, seq_spec, seq_spec, whole2(conv_w), whole2(ln_g), whole2(ln_b)],
        out_specs=[z_spec, whole2(conv_w), whole2(ln_g), whole2(ln_g), whole2(ln_b)],
        out_shape=[jax.ShapeDtypeStruct((n_b, seq, 2 * CONV_CH), BF16), jax.ShapeDtypeStruct(conv_w.shape, F32),
                   jax.ShapeDtypeStruct(ln_g.shape, F32), jax.ShapeDtypeStruct(ln_g.shape, F32),
                   jax.ShapeDtypeStruct(ln_b.shape, F32)],
        scratch_shapes=[pltpu.VMEM((seq + CONV_HALO + CONV_TAIL, CONV_CH), F32)] * 2,
        compiler_params=_params(("arbitrary",)),
    )(zglu, u1_saved, du3, conv_w, ln_g, ln_b)


def _adamw(name, w, parts, m, v):
    n_parts = parts.shape[0]

    def body(w_ref, p_ref, m_ref, v_ref, g_ref, d_ref, nm_ref, nv_ref):
        gg = p_ref[0].astype(F32)
        for j in range(1, n_parts):
            gg = gg + p_ref[j].astype(F32)
        nm = ADAM_B1 * m_ref[...] + (1.0 - ADAM_B1) * gg
        nv = ADAM_B2 * v_ref[...] + (1.0 - ADAM_B2) * jnp.square(gg)
        m_hat = nm / (1.0 - ADAM_B1 ** ADAM_STEP)
        v_hat = nv / (1.0 - ADAM_B2 ** ADAM_STEP)
        g_ref[...] = gg
        d_ref[...] = -ADAM_LR * (m_hat / (jnp.sqrt(v_hat) + ADAM_EPS) + ADAM_WD * w_ref[...])
        nm_ref[...] = nm
        nv_ref[...] = nv

    shape = jax.ShapeDtypeStruct(w.shape, F32)
    return pl.pallas_call(body, name=name, out_shape=[shape] * 4, compiler_params=_params(None))(w, parts, m, v)


def _rope_tables(seq):
    inv_freq = ROPE_THETA ** (-jnp.arange(0, QK_ROPE_DIM, 2, dtype=F32) / QK_ROPE_DIM)
    ang = jnp.arange(seq, dtype=F32)[:, None] * inv_freq[None, :]
    cos, sin = jnp.cos(ang), jnp.sin(ang)
    half = QK_ROPE_DIM // 2
    z = lambda n: jnp.zeros((seq, n), F32)
    tail = HEAD_PAD - QK_HEAD_DIM
    cos_t = jnp.concatenate([jnp.ones((seq, QK_NOPE_DIM), F32), cos, cos, z(tail)], axis=1)
    sin_lo = jnp.concatenate([z(QK_NOPE_DIM), -sin, z(half), z(tail)], axis=1)
    sin_hi = jnp.concatenate([z(QK_NOPE_DIM), z(half), sin, z(tail)], axis=1)
    return cos_t, sin_lo, sin_hi


def _pad_lanes(v, width=HEAD_PAD):
    return jnp.pad(v, [(0, 0)] * (v.ndim - 1) + [(0, width - v.shape[-1])])


def _unstack_cols(s):
    return s.transpose(1, 0, 2).reshape(s.shape[1], N_DEV * s.shape[2])


def _stack_cols(g, dtype):
    rows, cols = g.shape
    return g.reshape(rows, N_DEV, cols // N_DEV).transpose(1, 0, 2).astype(dtype)


def kernel(x, c, w_ada, b_ada, norm1_g, w_in, q_latent_g, w_uq, kv_latent_g, w_ukv, qk_norm_q_g, qk_norm_k_g, w_o_mla, conv_w, conv_b, conv_ln_g, conv_ln_b, w_pw_out, w_out, norm2_g, w_ff1, w_ff2, loss_target, m_w_ada, m_b_ada, m_norm1_g, m_w_in, m_q_latent_g, m_w_uq, m_kv_latent_g, m_w_ukv, m_qk_norm_q_g, m_qk_norm_k_g, m_w_o_mla, m_conv_w, m_conv_b, m_conv_ln_g, m_conv_ln_b, m_w_pw_out, m_w_out, m_norm2_g, m_w_ff1, m_w_ff2, v_w_ada, v_b_ada, v_norm1_g, v_w_in, v_q_latent_g, v_w_uq, v_kv_latent_g, v_w_ukv, v_qk_norm_q_g, v_qk_norm_k_g, v_w_o_mla, v_conv_w, v_conv_b, v_conv_ln_g, v_conv_ln_b, v_w_pw_out, v_w_out, v_norm2_g, v_w_ff1, v_w_ff2):
    given = dict(locals())
    local = {n: given[n][0] for n in WEIGHTS}
    vec = {n: local[n].reshape(1, -1) for n in REPLICATED}
    bf = lambda n: local[n].astype(BF16)
    n_b, seq, d = x.shape
    n_rows = n_b * seq
    x2 = x.reshape(n_rows, d)
    t2 = loss_target.reshape(n_rows, d)
    rw = functools.partial(_rowwise, n_rows=n_rows, seq=seq)
    me = 4 * lax.axis_index("x") + 2 * lax.axis_index("y") + lax.axis_index("c")
    ada_cols = local["w_ada"].shape[1]

    c_all, w_in_s, w_uq_s, w_ukv_s, conv_w_s = _exchange(
        "gather_early", [(c, True), (bf("w_in"), True), (bf("w_uq"), True), (bf("w_ukv"), True), (local["conv_w"], True)])
    w_in_f = _unstack_cols(w_in_s)
    zeros = lambda n: jnp.zeros((d, n), BF16)
    w_sm = jnp.concatenate([w_in_f[:, :OFF_KV], zeros(QK_NOPE_DIM), w_in_f[:, OFF_KV:OFF_KR], zeros(HEAD_PAD - QK_HEAD_DIM)], axis=1)
    w_glu = w_in_f[:, OFF_KR:OFF_GLU]
    w_gate = w_in_f[:, OFF_GLU:]
    wuq = _pad_lanes(_unstack_cols(w_uq_s).reshape(Q_LORA, N_HEADS, QK_HEAD_DIM)).reshape(Q_LORA, N_HEADS * HEAD_PAD)
    wukv_f = _unstack_cols(w_ukv_s).reshape(KV_LORA, N_HEADS, QK_NOPE_DIM + V_HEAD_DIM)
    wv = wukv_f[:, :, QK_NOPE_DIM:]
    odd = (jnp.arange(N_HEADS) % 2 == 1)[None, :, None]
    wuv = jnp.where(odd, jnp.pad(wv, ((0, 0), (0, 0), (V_HEAD_DIM, 0))), jnp.pad(wv, ((0, 0), (0, 0), (0, V_HEAD_DIM))))
    wukv = jnp.concatenate([_pad_lanes(wukv_f[:, :, :QK_NOPE_DIM]), wuv], axis=1).reshape(KV_LORA, 2 * N_HEADS * HEAD_PAD)
    gqn = _pad_lanes(vec["qk_norm_q_g"])
    gkn = _pad_lanes(vec["qk_norm_k_g"])
    conv_w_f = jnp.pad(_unstack_cols(conv_w_s), ((0, 1), (0, 0)))
    rope = _rope_tables(seq)

    all_rows = N_DEV * n_b
    pad_rows = (-all_rows) % ROWS_PAD
    c_rows = jnp.pad(c_all.reshape(all_rows, d), ((0, pad_rows), (0, 0)))
    b_cols = lax.dynamic_slice(local["b_ada"], (me * ada_cols,), (ada_cols,))
    mod_cols = _mm("ada_fwd", c_rows, local["w_ada"], "nn", F32, a_fn=_silu, epi=lambda acc, b: acc + b,
                   epi_in=(jnp.broadcast_to(b_cols, (all_rows + pad_rows, ada_cols)),))
    (mod_s,) = _exchange("scatter_mod", [(mod_cols[:all_rows].reshape(N_DEV, n_b, ada_cols), False)])
    mod = mod_s.transpose(1, 0, 2).reshape(n_b, ADA_CHUNKS, 1, d)
    shift1, scale1, gate1, shift2, scale2, gate2 = [mod[:, i] for i in range(ADA_CHUNKS)]

    (h,) = rw("norm1_fwd", lambda r, b, cc: ([_norm_mod(r[0], cc[0], b[0], b[1])], [], []),
              rows=[_full(x2)], bats=[scale1, shift1], consts=[vec["norm1_g"]], outs=[(d, BF16)])
    zsm = _mm("in_proj_sm", h, w_sm, "nn", F32)
    zglu = _mm("in_proj_glu", h, w_glu, "nn", BF16)
    zgate = _mm("in_proj_gate", h, w_gate, "nn", BF16)
    q, k, v, kt = _mla_prep_fwd(zsm, wuq, wukv, vec["q_latent_g"], vec["kv_latent_g"], gqn, gkn, rope, n_b, seq)
    attn, lse, (w_o_s, w_pw_s, w_out_s, w_ff1_s, w_ff2_s) = _attn_fwd(
        q, k, v, [(bf("w_o_mla"), True), (bf("w_pw_out"), True), (bf("w_out"), True), (bf("w_ff1"), True),
                  (bf("w_ff2"), True)], n_b, seq)
    w_o_f = _unstack_cols(w_o_s)
    w_pw_f = _unstack_cols(w_pw_s)
    w_out_f = w_out_s.reshape(d, d)
    w_ff2_f = w_ff2_s.reshape(N_DEV * w_ff2_s.shape[1], d)
    attn2 = attn.reshape(n_rows, N_HEADS * V_HEAD_DIM)
    u3, u1 = _conv_fwd(zglu.reshape(n_b, seq, 2 * CONV_CH), conv_w_f, vec["conv_b"], vec["conv_ln_g"], vec["conv_ln_b"], n_b, seq)
    u32 = u3.reshape(n_rows, CONV_CH)
    ya = _mm("mla_out", attn2, w_o_f, "nn", BF16)
    yb = _mm("conv_out", u32, w_pw_f, "nn", BF16)
    (mrg,) = rw("merge_fwd",
                lambda r, b, cc: ([_sigmoid(r[0].astype(F32)) * r[2].astype(F32) + _sigmoid(r[1].astype(F32)) * r[3].astype(F32)], [], []),
                rows=[(zgate, d, 0), (zgate, d, 1), _full(ya), _full(yb)], outs=[(d, BF16)])
    mixed = _mm("out_proj", mrg, w_out_f, "nn", BF16)

    def mid_fn(r, b, cc):
        x1_ = r[0] + b[0] * r[1].astype(F32)
        return [x1_, _norm_mod(x1_, cc[0], b[1], b[2])], [], []

    x1, h2 = rw("norm2_fwd", mid_fn, rows=[_full(x2), _full(mixed)], bats=[gate1, scale2, shift2],
                consts=[vec["norm2_g"]], outs=[(d, F32), (d, BF16)])
    a = _mm("ff1", h2, w_ff1_s, "nn", BF16, b_stacked=True)
    f = _mm("ff2", a, w_ff2_f, "nn", BF16, a_fn=_relu2)

    def loss_fn(r, b, cc):
        ff = r[1].astype(F32)
        err = r[0] + b[0] * ff - r[2]
        dy_ = err * (1.0 / d)
        sq = jnp.broadcast_to(jnp.sum(err * err, keepdims=True), (1, LANES))
        return [dy_, b[0] * dy_], [jnp.sum(dy_ * ff, axis=0, keepdims=True)], [sq]

    dy, df, dgate2, sq_err = rw("loss", loss_fn, rows=[_full(x1), _full(f), _full(t2)], bats=[gate2],
                                outs=[(d, F32), (d, BF16)], bat_outs=[d], tot_outs=[(1, LANES)])
    loss = lax.psum(sq_err[0, 0] * (0.5 / d), MESH_AXES)

    da = _mm("ff2_bwd", df, w_ff2_f, "nt", BF16, epi=lambda acc, av: acc * 2.0 * jnp.maximum(av, 0.0), epi_in=(a,))
    g_ff2 = _mm("ff2_dw", a, df, "tn", BF16, a_fn=_relu2)
    dh2 = _mm("ff1_bwd", da, w_ff1_s, "nt", F32, b_stacked=True)
    g_ff1_s = _mm("ff1_dw", h2, da, "tn", BF16, out_stacked=True)

    def mid_bwd(r, b, cc):
        dx, dsc, dsh, dg = _norm_mod_bwd(r[0], cc[0], b[0], r[1])
        dx1_ = r[2] + dx
        return [dx1_, b[1] * dx1_], [dsc, dsh, jnp.sum(dx1_ * r[3].astype(F32), axis=0, keepdims=True)], [dg]

    dx1, dmixed, dscale2, dshift2, dgate1, g_norm2 = rw(
        "norm2_bwd", mid_bwd, rows=[_full(x1), _full(dh2), _full(dy), _full(mixed)], bats=[scale2, gate1],
        consts=[vec["norm2_g"]], outs=[(d, F32), (d, BF16)], bat_outs=[d, d, d], tot_outs=[(1, d)])

    dmrg = _mm("out_proj_bwd", dmixed, w_out_f, "nt", BF16)
    g_out = _mm("out_proj_dw", mrg, dmixed, "tn", BF16)

    def merge_bwd(r, b, cc):
        dm, ya_, yb_ = r[0].astype(F32), r[3].astype(F32), r[4].astype(F32)
        sa, sb = _sigmoid(r[1].astype(F32)), _sigmoid(r[2].astype(F32))
        return [dm * ya_ * sa * (1.0 - sa), dm * yb_ * sb * (1.0 - sb), dm * sa, dm * sb], [], []

    dzga, dzgb, dya, dyb = rw("merge_bwd", merge_bwd,
                              rows=[_full(dmrg), (zgate, d, 0), (zgate, d, 1), _full(ya), _full(yb)],
                              outs=[(d, BF16)] * 4)
    dattn = _mm("mla_out_bwd", dya, w_o_f, "nt", BF16)
    g_o = _mm("mla_out_dw", attn2, dya, "tn", F32)
    du3 = _mm("conv_out_bwd", dyb, w_pw_f, "nt", BF16)
    g_pw = _mm("conv_out_dw", u32, dyb, "tn", F32)

    dzglu, g_conv_w, g_conv_b, g_ln_g, g_ln_b = _conv_bwd(
        zglu.reshape(n_b, seq, 2 * CONV_CH), u1, du3.reshape(n_b, seq, CONV_CH), conv_w_f, vec["conv_ln_g"],
        vec["conv_ln_b"], n_b, seq)
    dzglu = dzglu.reshape(n_rows, 2 * CONV_CH)

    dq, dk, dv, (p_ff2, p_ff1, p_out, p_pw, p_o) = _attn_bwd(
        q, k, v, kt, dattn.reshape(n_b, seq, N_HEADS * V_HEAD_DIM), attn, lse,
        [(g_ff2.reshape(N_DEV, -1, d), False), (g_ff1_s, False), (g_out.reshape(N_DEV, -1, d), False),
         (_stack_cols(g_pw, BF16), False), (_stack_cols(g_o, BF16), False)], n_b, seq)
    dzsm, g_wuq, g_wukv, g_gq, g_gkv, g_gqn, g_gkn = _mla_prep_bwd(
        zsm, dq, dk, dv, wuq, wukv, vec["q_latent_g"], vec["kv_latent_g"], gqn, gkn, rope, n_b, seq)

    g_gate_a = _mm("in_proj_gate_dw_a", h, dzga, "tn", F32)
    g_gate_b = _mm("in_proj_gate_dw_b", h, dzgb, "tn", F32)
    g_glu = _mm("in_proj_glu_dw", h, dzglu, "tn", F32)
    g_sm = _mm("in_proj_sm_dw", h, dzsm, "tn", F32)
    g_in = jnp.concatenate([g_sm[:, :OFF_KV], g_sm[:, OFF_KV + QK_NOPE_DIM:OFF_KV + QK_NOPE_DIM + QK_ROPE_DIM], g_glu,
                            g_gate_a, g_gate_b], axis=1)
    g_uq = g_wuq.reshape(Q_LORA, N_HEADS, HEAD_PAD)[:, :, :QK_HEAD_DIM].reshape(Q_LORA, N_HEADS * QK_HEAD_DIM)
    g_wukv = g_wukv.reshape(KV_LORA, 2, N_HEADS, HEAD_PAD)
    g_v = jnp.where(odd, g_wukv[:, 1, :, V_HEAD_DIM:], g_wukv[:, 1, :, :V_HEAD_DIM])
    g_ukv = jnp.concatenate([g_wukv[:, 0, :, :QK_NOPE_DIM], g_v], axis=2).reshape(KV_LORA, -1)

    grad_x, dscale1, dshift1, g_norm1, (p_in, p_uq, p_ukv, p_conv_w) = _in_proj_bwd(
        [(dzga, w_gate[:, :d]), (dzgb, w_gate[:, d:]), (dzglu, w_glu), (dzsm, w_sm)], x2, dx1, scale1, vec["norm1_g"],
        [(_stack_cols(g_in, BF16), False), (_stack_cols(g_uq, BF16), False), (_stack_cols(g_ukv, BF16), False),
         (_stack_cols(g_conv_w[:CONV_WIDTH], F32), False)], seq)

    dmod = jnp.concatenate([dshift1, dscale1, dgate1, dshift2, dscale2, dgate2], axis=1).reshape(n_b, N_DEV, ada_cols)
    (dmod_s,) = _exchange("scatter_dmod", [(dmod.transpose(1, 0, 2), False)])
    dmod_rows = jnp.pad(dmod_s.reshape(all_rows, ada_cols), ((0, pad_rows), (0, 0)))
    g_ada = _mm("ada_dw", c_rows, dmod_rows, "tn", F32, a_fn=_silu)
    (g_b_cols,) = _rowwise("ada_db", lambda r, b, cc: ([], [], [jnp.sum(r[0], axis=0, keepdims=True)]),
                           all_rows + pad_rows, all_rows + pad_rows, rows=[_full(dmod_rows)], tot_outs=[(1, ada_cols)])

    partial_of = {"norm1_g": g_norm1, "q_latent_g": g_gq, "kv_latent_g": g_gkv, "qk_norm_q_g": g_gqn,
                  "qk_norm_k_g": g_gkn, "conv_b": g_conv_b, "conv_ln_g": g_ln_g, "conv_ln_b": g_ln_b, "norm2_g": g_norm2}
    names = [n for n in REPLICATED if n != "b_ada"]
    pieces = [_pad_lanes(partial_of[n], -(-partial_of[n].shape[1] // LANES) * LANES) for n in names] + [g_b_cols]
    widths = [p.shape[1] for p in pieces]
    small = jnp.concatenate(pieces, axis=1)
    small = _pad_lanes(small, -(-small.shape[1] // (8 * LANES)) * 8 * LANES).reshape(-1, LANES)
    (small_s,) = _exchange("gather_small_grads", [(small, True)])
    small_s = small_s.reshape(N_DEV, 1, -1)
    parts = {}
    off = 0
    for n, wd in zip(names, widths):
        parts[n] = small_s[:, :, off:off + vec[n].shape[1]]
        off += wd
    parts["b_ada"] = small_s[:, 0, off:off + ada_cols].reshape(1, 1, N_DEV * ada_cols)
    parts.update({"w_ada": g_ada[None], "w_in": p_in, "w_uq": p_uq, "w_ukv": p_ukv, "w_o_mla": p_o, "conv_w": p_conv_w,
                  "w_pw_out": p_pw, "w_out": p_out, "w_ff1": p_ff1, "w_ff2": p_ff2})

    grad_out, delta_out, m_out, v_out = [], [], [], []
    for n in WEIGHTS:
        shape2 = parts[n].shape[1:]
        g_w, d_w, n_m, n_v = _adamw("adamw_" + n, local[n].reshape(shape2), parts[n], given["m_" + n].reshape(shape2),
                                    given["v_" + n].reshape(shape2))
        full_shape = given[n].shape
        grad_out.append(g_w.reshape(full_shape))
        delta_out.append(d_w.reshape(full_shape))
        m_out.append(n_m.reshape(full_shape))
        v_out.append(n_v.reshape(full_shape))
    return (loss, grad_x.reshape(n_b, seq, d), *grad_out, *delta_out, *m_out, *v_out)
```

```python
import functools

import jax
import jax.numpy as jnp
from jax import lax
from jax.experimental import pallas as pl
from jax.experimental.pallas import tpu as pltpu

F32 = jnp.float32
BF16 = jnp.bfloat16

N_DEV = 8
MESH_AXES = ("x", "y", "c")
EPS = 1e-6
N_HEADS = 8
QK_HEAD_DIM = 96
QK_NOPE_DIM = 64
QK_ROPE_DIM = 32
V_HEAD_DIM = 64
HEAD_PAD = 128
Q_LORA = 256
KV_LORA = 128
CONV_CH = 512
CONV_WIDTH = 31
CONV_HALO = 32
CONV_TAIL = 8
CHUNK = 64
ROPE_THETA = 10000.0
OFF_Q = Q_LORA
OFF_KV = OFF_Q + KV_LORA
OFF_KR = OFF_KV + QK_ROPE_DIM
OFF_GLU = OFF_KR + 2 * CONV_CH
ADA_CHUNKS = 6
ADAM_LR = 0.001
ADAM_B1 = 0.9
ADAM_B2 = 0.999
ADAM_EPS = 1e-08
ADAM_WD = 0.01
ADAM_STEP = 10
LANES = 128
VMEM_LIMIT = 56 * 1024 * 1024
NEG_BIG = -1e30
ATT_HEADS = 4
ATT_TILE = 256
PREP_TILE = 512
ATT_SCALE = QK_HEAD_DIM ** -0.5
LOG2E = 1.4426950408889634
LN2 = 0.6931471805599453
QK_SCALE = ATT_SCALE * LOG2E
ROWS_PAD = 16

REPLICATED = ("b_ada", "norm1_g", "q_latent_g", "kv_latent_g", "qk_norm_q_g", "qk_norm_k_g", "conv_b", "conv_ln_g",
              "conv_ln_b", "norm2_g")
WEIGHTS = ("w_ada", "b_ada", "norm1_g", "w_in", "q_latent_g", "w_uq", "kv_latent_g", "w_ukv", "qk_norm_q_g",
           "qk_norm_k_g", "w_o_mla", "conv_w", "conv_b", "conv_ln_g", "conv_ln_b", "w_pw_out", "w_out", "norm2_g",
           "w_ff1", "w_ff2")


def _tile(dim, pref):
    if dim <= pref:
        return dim
    t = (pref // LANES) * LANES
    while dim % t:
        t -= LANES
    return t


def _params(semantics):
    return pltpu.CompilerParams(dimension_semantics=semantics, vmem_limit_bytes=VMEM_LIMIT)


def _sigmoid(v):
    return 1.0 / (1.0 + jnp.exp(-v))


def _silu(v):
    return v * _sigmoid(v)


def _relu2(v):
    return jnp.square(jnp.maximum(v, 0.0))


_DIMS = {"nn": (((1,), (0,)), ((), ())), "nt": (((1,), (1,)), ((), ())), "tn": (((0,), (0,)), ((), ()))}


def _mm(name, a, b, mode, out_dtype, *, a_fn=None, epi=None, epi_in=(), b_stacked=False, out_stacked=False,
        tm=1024, tn=1024, tk=1024):
    if b_stacked:
        shard = b.shape[2]
        b_rows, b_cols = b.shape[1], N_DEV * shard
    else:
        b_rows, b_cols = b.shape
    if mode == "nn":
        (m, k), n = a.shape, b_cols
    elif mode == "nt":
        (m, k), n = a.shape, b_rows
    else:
        (k, m), n = a.shape, b_cols
    if out_stacked:
        shard = n // N_DEV
    tm = _tile(m, tm)
    tn = _tile(shard, tn) if (out_stacked or (b_stacked and mode != "nt")) else _tile(n, tn)
    tk = _tile(shard, tk) if (b_stacked and mode == "nt") else _tile(k, tk)
    nk = k // tk
    a_spec = (pl.BlockSpec((tk, tm), lambda i, j, kk: (kk, i)) if mode == "tn"
              else pl.BlockSpec((tm, tk), lambda i, j, kk: (i, kk)))
    if b_stacked and mode == "nt":
        per = shard // tk
        b_spec = pl.BlockSpec((None, tn, tk), lambda i, j, kk: (kk // per, j, kk % per))
    elif b_stacked:
        per = shard // tn
        b_spec = pl.BlockSpec((None, tk, tn), lambda i, j, kk: (j // per, kk, j % per))
    elif mode == "nt":
        b_spec = pl.BlockSpec((tn, tk), lambda i, j, kk: (j, kk))
    else:
        b_spec = pl.BlockSpec((tk, tn), lambda i, j, kk: (kk, j))
    e_spec = pl.BlockSpec((tm, tn), lambda i, j, kk: (i, j))
    if out_stacked:
        per_o = shard // tn
        o_spec = pl.BlockSpec((None, tm, tn), lambda i, j, kk: (j // per_o, i, j % per_o))
        out_shape = jax.ShapeDtypeStruct((N_DEV, m, shard), out_dtype)
    else:
        o_spec = e_spec
        out_shape = jax.ShapeDtypeStruct((m, n), out_dtype)
    n_epi = len(epi_in)

    def body(a_ref, b_ref, *rest):
        epi_refs, o_ref, acc_ref = rest[:n_epi], rest[n_epi], rest[n_epi + 1]
        kk = pl.program_id(2)

        @pl.when(kk == 0)
        def _():
            acc_ref[...] = jnp.zeros_like(acc_ref)

        av = a_ref[...]
        if a_fn is not None:
            av = a_fn(av.astype(F32))
        acc_ref[...] += lax.dot_general(av.astype(BF16), b_ref[...].astype(BF16), _DIMS[mode],
                                        preferred_element_type=F32)

        @pl.when(kk == nk - 1)
        def _():
            acc = acc_ref[...]
            if epi is not None:
                acc = epi(acc, *[r[...].astype(F32) for r in epi_refs])
            o_ref[...] = acc.astype(out_dtype)

    return pl.pallas_call(
        body, name=name, grid=(m // tm, n // tn, nk),
        in_specs=[a_spec, b_spec] + [e_spec] * n_epi, out_specs=o_spec, out_shape=out_shape,
        scratch_shapes=[pltpu.VMEM((tm, tn), F32)],
        compiler_params=_params(("parallel", "parallel", "arbitrary")),
    )(a, b, *epi_in)


def _rowwise(name, fn, n_rows, seq, rows, bats=(), consts=(), outs=(), bat_outs=(), tot_outs=(), tm=256):
    tm = min(tm, seq)
    per_seq = seq // tm
    n_b = n_rows // seq
    nr, nb, nc, no, nbo, nto = len(rows), len(bats), len(consts), len(outs), len(bat_outs), len(tot_outs)

    def body(*refs):
        i = pl.program_id(0)
        r_in = [r[...] for r in refs[:nr]]
        b_in = [r[0] for r in refs[nr:nr + nb]]
        c_in = [r[...] for r in refs[nr + nb:nr + nb + nc]]
        o_refs = refs[nr + nb + nc:nr + nb + nc + no]
        bo_refs = refs[nr + nb + nc + no:nr + nb + nc + no + nbo]
        to_refs = refs[nr + nb + nc + no + nbo:]
        o_val, bo_val, to_val = fn(r_in, b_in, c_in)
        for r, v in zip(o_refs, o_val):
            r[...] = v.astype(r.dtype)
        if nbo:
            @pl.when(i % per_seq == 0)
            def _():
                for r in bo_refs:
                    r[...] = jnp.zeros_like(r)

            for r, v in zip(bo_refs, bo_val):
                r[0] += v
        if nto:
            @pl.when(i == 0)
            def _():
                for r in to_refs:
                    r[...] = jnp.zeros_like(r)

            for r, v in zip(to_refs, to_val):
                r[...] += v

    in_specs = [pl.BlockSpec((tm, w), functools.partial(lambda cb, i: (i, cb), cb)) for (_, w, cb) in rows]
    in_specs += [pl.BlockSpec((1, 1, bt.shape[2]), lambda i: (i // per_seq, 0, 0)) for bt in bats]
    in_specs += [pl.BlockSpec(ct.shape, lambda i: (0, 0)) for ct in consts]
    out_specs = [pl.BlockSpec((tm, w), lambda i: (i, 0)) for (w, _) in outs]
    out_specs += [pl.BlockSpec((1, 1, w), lambda i: (i // per_seq, 0, 0)) for w in bat_outs]
    out_specs += [pl.BlockSpec(shp, lambda i: (0, 0)) for shp in tot_outs]
    out_shape = [jax.ShapeDtypeStruct((n_rows, w), dt) for (w, dt) in outs]
    out_shape += [jax.ShapeDtypeStruct((n_b, 1, w), F32) for w in bat_outs]
    out_shape += [jax.ShapeDtypeStruct(shp, F32) for shp in tot_outs]
    res = pl.pallas_call(
        body, name=name, grid=(n_rows // tm,), in_specs=in_specs, out_specs=out_specs, out_shape=out_shape,
        compiler_params=_params(("arbitrary",)),
    )(*[r[0] for r in rows], *bats, *consts)
    return res


def _full(arr):
    return (arr, arr.shape[1], 0)


def _norm_mod(x, g, scale, shift):
    r = lax.rsqrt(jnp.mean(x * x, axis=-1, keepdims=True) + EPS)
    xh = x * r
    return xh * g * (1.0 + scale) + shift


def _norm_mod_bwd(x, g, scale, dh):
    r = lax.rsqrt(jnp.mean(x * x, axis=-1, keepdims=True) + EPS)
    xh = x * r
    dn = dh * (1.0 + scale)
    dxh = dn * g
    dx = r * (dxh - xh * jnp.mean(dxh * xh, axis=-1, keepdims=True))
    dscale = jnp.sum(dh * xh * g, axis=0, keepdims=True)
    dshift = jnp.sum(dh, axis=0, keepdims=True)
    dg = jnp.sum(dn * xh, axis=0, keepdims=True)
    return dx, dscale, dshift, dg


def _rms(v, g):
    r = lax.rsqrt(jnp.mean(v * v, axis=-1, keepdims=True) + EPS)
    return v * r * g


def _rms_bwd(v, g, dy):
    r = lax.rsqrt(jnp.mean(v * v, axis=-1, keepdims=True) + EPS)
    vh = v * r
    dvh = dy * g
    dv = r * (dvh - vh * jnp.mean(dvh * vh, axis=-1, keepdims=True))
    return dv, jnp.sum(dy * vh, axis=0, keepdims=True)


def _head_norm(v, g):
    r = lax.rsqrt(jnp.sum(v * v, axis=-1, keepdims=True) * (1.0 / QK_HEAD_DIM) + EPS)
    return v * r * g


def _head_norm_bwd(v, g, dy):
    r = lax.rsqrt(jnp.sum(v * v, axis=-1, keepdims=True) * (1.0 / QK_HEAD_DIM) + EPS)
    vh = v * r
    dvh = dy * g
    dv = r * (dvh - vh * (jnp.sum(dvh * vh, axis=-1, keepdims=True) * (1.0 / QK_HEAD_DIM)))
    return dv, jnp.sum(dy * vh, axis=0, keepdims=True)


def _rope(v, cos, sin_lo, sin_hi):
    return v * cos + pltpu.roll(v, HEAD_PAD - 16, 1) * sin_lo + pltpu.roll(v, 16, 1) * sin_hi


def _rope_bwd(g, cos, sin_lo, sin_hi):
    return g * cos + pltpu.roll(g * sin_lo, 16, 1) + pltpu.roll(g * sin_hi, HEAD_PAD - 16, 1)


def _mla_prep_fwd(zsm, wuq, wukv, gq, gkv, gqn, gkn, rope, n_b, seq):
    n_rows = n_b * seq
    tm = min(PREP_TILE, seq)
    per_seq = seq // tm
    att_tile = min(ATT_TILE, seq)
    k_cols = N_HEADS * HEAD_PAD

    def body(z_ref, wuq_ref, wukv_ref, gq_ref, gkv_ref, gqn_ref, gkn_ref, c_ref, s1_ref, s2_ref,
             q_ref, k_ref, v_ref, kt_ref):
        z = z_ref[...]
        qn = _rms(z[:, :Q_LORA], gq_ref[...]).astype(BF16)
        kvn = _rms(z[:, Q_LORA:Q_LORA + KV_LORA], gkv_ref[...]).astype(BF16)
        krp = z[:, Q_LORA + KV_LORA:]
        cos, s1, s2 = c_ref[...], s1_ref[...], s2_ref[...]
        q_all = jnp.dot(qn, wuq_ref[...], preferred_element_type=F32)
        kv_all = jnp.dot(kvn, wukv_ref[...], preferred_element_type=F32)
        for h in range(N_HEADS):
            cols = slice(h * HEAD_PAD, (h + 1) * HEAD_PAD)
            q_ref[0, h] = (_rope(_head_norm(q_all[:, cols], gqn_ref[...]), cos, s1, s2) * QK_SCALE).astype(BF16)
            kh = _rope(_head_norm(kv_all[:, cols] + krp, gkn_ref[...]), cos, s1, s2)
            k_ref[0, h] = kh.astype(BF16)
            for part in range(tm // att_tile):
                kt_ref[0, h, part] = kh[part * att_tile:(part + 1) * att_tile].T.astype(BF16)
            v_ref[0, h] = kv_all[:, k_cols + h * HEAD_PAD:k_cols + (h + 1) * HEAD_PAD].astype(BF16)

    whole2 = lambda arr: pl.BlockSpec(arr.shape, lambda i: (0, 0))
    rope_spec = pl.BlockSpec((tm, HEAD_PAD), lambda i: (i % per_seq, 0))
    head_spec = pl.BlockSpec((1, N_HEADS, tm, HEAD_PAD), lambda i: (i // per_seq, 0, i % per_seq, 0))
    head_shape = jax.ShapeDtypeStruct((n_b, N_HEADS, seq, HEAD_PAD), BF16)
    t_spec = pl.BlockSpec((1, N_HEADS, tm // att_tile, HEAD_PAD, att_tile), lambda i: (i // per_seq, 0, i % per_seq, 0, 0))
    t_shape = jax.ShapeDtypeStruct((n_b, N_HEADS, seq // att_tile, HEAD_PAD, att_tile), BF16)
    return pl.pallas_call(
        body, name="mla_prep_fwd", grid=(n_rows // tm,),
        in_specs=[pl.BlockSpec((tm, 512), lambda i: (i, 0)), whole2(wuq), whole2(wukv),
                  whole2(gq), whole2(gkv), whole2(gqn), whole2(gkn), rope_spec, rope_spec, rope_spec],
        out_specs=[head_spec] * 3 + [t_spec], out_shape=[head_shape] * 3 + [t_shape],
        compiler_params=_params(("parallel",)),
    )(zsm, wuq, wukv, gq, gkv, gqn, gkn, *rope)


def _mla_prep_bwd(zsm, dq, dk, dv, wuq, wukv, gq, gkv, gqn, gkn, rope, n_b, seq):
    n_rows = n_b * seq
    tm = min(PREP_TILE, seq)
    per_seq = seq // tm
    tn_dims = _DIMS["tn"]
    nt_dims = _DIMS["nt"]
    k_cols = N_HEADS * HEAD_PAD

    def body(z_ref, dq_ref, dk_ref, dv_ref, wuq_ref, wukv_ref, gq_ref, gkv_ref, gqn_ref, gkn_ref,
             c_ref, s1_ref, s2_ref, dz_ref, dwuq_ref, dwukv_ref, dgq_ref, dgkv_ref, dgqn_ref, dgkn_ref):
        @pl.when(pl.program_id(0) == 0)
        def _():
            for r in (dwuq_ref, dwukv_ref, dgq_ref, dgkv_ref, dgqn_ref, dgkn_ref):
                r[...] = jnp.zeros_like(r)

        z = z_ref[...]
        zq, zkv, krp = z[:, :Q_LORA], z[:, Q_LORA:Q_LORA + KV_LORA], z[:, Q_LORA + KV_LORA:]
        qn = _rms(zq, gq_ref[...]).astype(BF16)
        kvn = _rms(zkv, gkv_ref[...]).astype(BF16)
        cos, s1, s2 = c_ref[...], s1_ref[...], s2_ref[...]
        lane = lax.broadcasted_iota(jnp.int32, (tm, HEAD_PAD), 1)
        rope_lanes = (lane >= QK_NOPE_DIM) & (lane < QK_HEAD_DIM)
        q_all = jnp.dot(qn, wuq_ref[...], preferred_element_type=F32)
        k_all = jnp.dot(kvn, wukv_ref[:, :k_cols], preferred_element_type=F32)
        dkrp = jnp.zeros((tm, HEAD_PAD), F32)
        dgqn = jnp.zeros((1, HEAD_PAD), F32)
        dgkn = jnp.zeros((1, HEAD_PAD), F32)
        dq_heads, dk_heads = [], []
        for h in range(N_HEADS):
            cols = slice(h * HEAD_PAD, (h + 1) * HEAD_PAD)
            dqh, dg = _head_norm_bwd(q_all[:, cols], gqn_ref[...],
                                     _rope_bwd(dq_ref[0, h].astype(F32) * ATT_SCALE, cos, s1, s2))
            dgqn += dg
            dq_heads.append(dqh.astype(BF16))
            dkh, dg = _head_norm_bwd(k_all[:, cols] + krp, gkn_ref[...], _rope_bwd(dk_ref[0, h].astype(F32), cos, s1, s2))
            dgkn += dg
            dkrp += jnp.where(rope_lanes, dkh, 0.0)
            dk_heads.append(dkh.astype(BF16))
        dq_all = jnp.concatenate(dq_heads, axis=1)
        dkv_all = jnp.concatenate(dk_heads + [dv_ref[0, h] for h in range(N_HEADS)], axis=1)
        dwuq_ref[...] += lax.dot_general(qn, dq_all, tn_dims, preferred_element_type=F32)
        dqn = lax.dot_general(dq_all, wuq_ref[...], nt_dims, preferred_element_type=F32)
        dwukv_ref[...] += lax.dot_general(kvn, dkv_all, tn_dims, preferred_element_type=F32)
        dkvn = lax.dot_general(dkv_all, wukv_ref[...], nt_dims, preferred_element_type=F32)
        dzq, dg = _rms_bwd(zq, gq_ref[...], dqn)
        dgq_ref[...] += dg
        dzkv, dg = _rms_bwd(zkv, gkv_ref[...], dkvn)
        dgkv_ref[...] += dg
        dgqn_ref[...] += dgqn
        dgkn_ref[...] += dgkn
        dz_ref[:, :Q_LORA] = dzq.astype(dz_ref.dtype)
        dz_ref[:, Q_LORA:Q_LORA + KV_LORA] = dzkv.astype(dz_ref.dtype)
        dz_ref[:, Q_LORA + KV_LORA:] = dkrp.astype(dz_ref.dtype)

    whole2 = lambda arr: pl.BlockSpec(arr.shape, lambda i: (0, 0))
    rope_spec = pl.BlockSpec((tm, HEAD_PAD), lambda i: (i % per_seq, 0))
    head_spec = pl.BlockSpec((1, N_HEADS, tm, HEAD_PAD), lambda i: (i // per_seq, 0, i % per_seq, 0))
    row_spec = pl.BlockSpec((tm, 512), lambda i: (i, 0))
    return pl.pallas_call(
        body, name="mla_prep_bwd", grid=(n_rows // tm,),
        in_specs=[row_spec, head_spec, head_spec, head_spec, whole2(wuq), whole2(wukv),
                  whole2(gq), whole2(gkv), whole2(gqn), whole2(gkn), rope_spec, rope_spec, rope_spec],
        out_specs=[row_spec, whole2(wuq), whole2(wukv), whole2(gq), whole2(gkv), whole2(gqn), whole2(gkn)],
        out_shape=[jax.ShapeDtypeStruct((n_rows, 512), BF16),
                   jax.ShapeDtypeStruct(wuq.shape, F32), jax.ShapeDtypeStruct(wukv.shape, F32),
                   jax.ShapeDtypeStruct(gq.shape, F32), jax.ShapeDtypeStruct(gkv.shape, F32),
                   jax.ShapeDtypeStruct(gqn.shape, F32), jax.ShapeDtypeStruct(gkn.shape, F32)],
        compiler_params=_params(("arbitrary",)),
    )(zsm, dq, dk, dv, wuq, wukv, gq, gkv, gqn, gkn, *rope)


HBM_SPEC = pl.BlockSpec(memory_space=pltpu.HBM)


def _xchg_out_shapes(bufs):
    return [jax.ShapeDtypeStruct((N_DEV,) + (a.shape if gather else a.shape[1:]), a.dtype) for a, gather in bufs]


def _xchg_scratch(n_buf):
    return [pltpu.SemaphoreType.DMA((n_buf * (N_DEV - 1),)), pltpu.SemaphoreType.DMA((n_buf * (N_DEV - 1),)),
            pltpu.SemaphoreType.DMA((n_buf,))]


def _xchg_copies(src_refs, dst_refs, gathers, send_sems, recv_sems, local_sems):
    x, y, c = lax.axis_index("x"), lax.axis_index("y"), lax.axis_index("c")
    me = 4 * x + 2 * y + c
    local, starts, arrivals = [], [], []
    for bi, (src, dst, gather) in enumerate(zip(src_refs, dst_refs, gathers)):
        local.append(pltpu.make_async_copy(src if gather else src.at[me], dst.at[me], local_sems.at[bi]))
        for kk in range(1, N_DEV):
            px = 1 - x if kk & 4 else x
            py = 1 - y if kk & 2 else y
            pc = 1 - c if kk & 1 else c
            pid = 4 * px + 2 * py + pc
            sem = bi * (N_DEV - 1) + kk - 1
            starts.append(pltpu.make_async_remote_copy(
                src_ref=src if gather else src.at[pid], dst_ref=dst.at[me],
                send_sem=send_sems.at[sem], recv_sem=recv_sems.at[sem],
                device_id=(px, py, pc), device_id_type=pl.DeviceIdType.MESH))
            arrivals.append(pltpu.make_async_remote_copy(
                src_ref=src if gather else src.at[me], dst_ref=dst.at[pid],
                send_sem=send_sems.at[sem], recv_sem=recv_sems.at[sem],
                device_id=(px, py, pc), device_id_type=pl.DeviceIdType.MESH))
    return local, starts, arrivals


def _xchg_start(copies):
    local, sends, _ = copies
    for cp in local + sends:
        cp.start()


def _xchg_finish(copies):
    local, sends, arrivals = copies
    for cp in arrivals:
        cp.wait_recv()
    for cp in sends:
        cp.wait_send()
    for cp in local:
        cp.wait()


def _gather_by_chip(src_refs, dst_refs, send_sems, recv_sems, local_sems):
    x, y, c = lax.axis_index("x"), lax.axis_index("y"), lax.axis_index("c")
    me = 4 * x + 2 * y + c
    sibling = (x, y, 1 - c)

    def place(kk):
        px = 1 - x if kk & 4 else x
        py = 1 - y if kk & 2 else y
        pc = 1 - c if kk & 1 else c
        return (px, py, pc), 4 * px + 2 * py + pc

    def copy(bi, kk, src, dst, to):
        sem = bi * (N_DEV - 1) + kk - 1
        return pltpu.make_async_remote_copy(src_ref=src, dst_ref=dst, send_sem=send_sems.at[sem],
                                            recv_sem=recv_sems.at[sem], device_id=to, device_id_type=pl.DeviceIdType.MESH)

    local, sends = [], []
    for bi, (src, dst) in enumerate(zip(src_refs, dst_refs)):
        local.append(pltpu.make_async_copy(src, dst.at[me], local_sems.at[bi]))
        sends += [copy(bi, kk, src, dst.at[me], place(kk)[0]) for kk in (1, 2, 4, 6)]
    for cp in local + sends:
        cp.start()
    for kk in (2, 4, 6):
        for bi, (src, dst) in enumerate(zip(src_refs, dst_refs)):
            dev, pid = place(kk)
            copy(bi, kk, src, dst.at[pid], dev).wait_recv()
            passed = copy(bi, kk | 1, dst.at[pid], dst.at[pid], sibling)
            passed.start()
            sends.append(passed)
    for kk in (1, 3, 5, 7):
        for bi, (src, dst) in enumerate(zip(src_refs, dst_refs)):
            dev, pid = place(kk)
            copy(bi, kk, src, dst.at[pid], sibling).wait_recv()
    for cp in sends:
        cp.wait_send()
    for cp in local:
        cp.wait()


def _exchange(name, bufs, by_chip=False):
    n_buf = len(bufs)
    gathers = [g for _, g in bufs]
    assert not by_chip or all(gathers)

    def body(*refs):
        srcs, dsts = refs[:n_buf], refs[n_buf:2 * n_buf]
        if by_chip:
            _gather_by_chip(srcs, dsts, *refs[2 * n_buf:])
            return
        copies = _xchg_copies(srcs, dsts, gathers, *refs[2 * n_buf:])
        _xchg_start(copies)
        _xchg_finish(copies)

    return pl.pallas_call(
        body, name=name, out_shape=_xchg_out_shapes(bufs),
        in_specs=[HBM_SPEC] * n_buf, out_specs=[HBM_SPEC] * n_buf, scratch_shapes=_xchg_scratch(n_buf),
    )(*[a for a, _ in bufs])


def _chunk_mask(t, keys_first):
    key = lax.broadcasted_iota(jnp.int32, (t, t), 0 if keys_first else 1) // CHUNK
    query = lax.broadcasted_iota(jnp.int32, (t, t), 1 if keys_first else 0) // CHUNK
    return query >= key


def _grid_ends(grid):
    ids = [pl.program_id(ax) for ax in range(len(grid))]
    first = functools.reduce(jnp.logical_and, [i == 0 for i in ids])
    last = functools.reduce(jnp.logical_and, [i == g - 1 for i, g in zip(ids, grid)])
    return first, last


def _attn_fwd(q, k, v, bufs, n_b, seq):
    tq = min(ATT_TILE, seq)
    nq = seq // tq
    nt_dims = _DIMS["nt"]
    hpb = ATT_HEADS
    grid = (n_b, N_HEADS // hpb, nq)
    n_buf = len(bufs)
    gathers = [g for _, g in bufs]
    sum_lane = [HEAD_PAD - 1 if hh % 2 == 0 else 0 for hh in range(hpb)]

    def body(q_ref, k_ref, v_ref, *rest):
        srcs, (o_ref, lse_ref), dsts = rest[:n_buf], rest[n_buf:n_buf + 2], rest[n_buf + 2:2 * n_buf + 2]
        s_ref = rest[2 * n_buf + 2]
        copies = _xchg_copies(srcs, dsts, gathers, *rest[2 * n_buf + 3:])
        first, last = _grid_ends(grid)
        pl.when(first)(functools.partial(_xchg_start, copies))

        qi = pl.program_id(2)
        mask = _chunk_mask(tq, keys_first=False)
        lane_row = lax.broadcasted_iota(jnp.int32, (1, HEAD_PAD), 1)
        ones = [(lane_row == sum_lane[hh]).astype(BF16) for hh in range(hpb)]
        qs = [q_ref[0, hh] for hh in range(hpb)]

        def score_step(j, tops, masked):
            rows = pl.ds(pl.multiple_of(j * tq, tq), tq)
            out = []
            for hh in range(hpb):
                s = lax.dot_general(qs[hh], k_ref[0, hh, rows, :], nt_dims, preferred_element_type=F32)
                if masked:
                    s = jnp.where(mask, s, NEG_BIG)
                s_ref[hh, j] = s
                out.append(jnp.maximum(tops[hh], s))
            return tuple(out)

        tops = tuple(jnp.full((tq, tq), NEG_BIG, F32) for _ in range(hpb))
        tops = lax.fori_loop(0, qi, functools.partial(score_step, masked=False), tops)
        tops = score_step(qi, tops, True)
        ms = [jnp.max(top, axis=-1, keepdims=True) for top in tops]

        def value_step(j, accs):
            rows = pl.ds(pl.multiple_of(j * tq, tq), tq)
            out = []
            for hh in range(hpb):
                p = jnp.exp2(s_ref[hh, j] - ms[hh]).astype(BF16)
                out.append(accs[hh] + jnp.dot(p, v_ref[0, hh, rows, :] + ones[hh], preferred_element_type=F32))
            return tuple(out)

        accs = tuple(jnp.zeros((tq, HEAD_PAD), F32) for _ in range(hpb))
        accs = lax.fori_loop(0, qi + 1, value_step, accs)
        carry = list(zip(ms, accs))
        lane = lax.broadcasted_iota(jnp.int32, (tq, HEAD_PAD), 1)
        for pair in range(hpb // 2):
            outs = []
            for hh in (2 * pair, 2 * pair + 1):
                m, acc = carry[hh]
                l = jnp.sum(jnp.where(lane == sum_lane[hh], acc, 0.0), axis=-1, keepdims=True)
                outs.append(acc * (1.0 / l))
                lse_ref[0, hh] = jnp.broadcast_to(m + jnp.log2(l), (tq, HEAD_PAD)).T[0:8, :]
            o_ref[0, :, pair * HEAD_PAD:(pair + 1) * HEAD_PAD] = jnp.where(lane < V_HEAD_DIM, outs[0], outs[1]).astype(BF16)

        pl.when(last)(functools.partial(_xchg_finish, copies))

    kv_spec = pl.BlockSpec((1, hpb, seq, HEAD_PAD), lambda b, hb, i: (b, hb, 0, 0))
    q_spec = pl.BlockSpec((1, hpb, tq, HEAD_PAD), lambda b, hb, i: (b, hb, i, 0))
    res = pl.pallas_call(
        body, name="attn_fwd", grid=grid,
        in_specs=[q_spec, kv_spec, kv_spec] + [HBM_SPEC] * n_buf,
        out_specs=[pl.BlockSpec((1, tq, hpb * V_HEAD_DIM), lambda b, hb, i: (b, i, hb)),
                   pl.BlockSpec((1, hpb, 8, tq), lambda b, hb, i: (b, hb, 0, i))] + [HBM_SPEC] * n_buf,
        out_shape=[jax.ShapeDtypeStruct((n_b, seq, N_HEADS * V_HEAD_DIM), BF16),
                   jax.ShapeDtypeStruct((n_b, N_HEADS, 8, seq), F32)] + _xchg_out_shapes(bufs),
        scratch_shapes=[pltpu.VMEM((hpb, nq, tq, tq), F32)] + _xchg_scratch(n_buf),
        compiler_params=_params(("arbitrary", "arbitrary", "arbitrary")),
    )(q, k, v, *[a for a, _ in bufs])
    return res[0], res[1], res[2:]


def _attn_bwd(q, k, v, kt, do, o, lse, bufs, n_b, seq):
    tq = min(ATT_TILE, seq)
    nq = seq // tq
    nt_dims = _DIMS["nt"]
    hpb = ATT_HEADS
    grid = (n_b, N_HEADS // hpb, nq)
    n_buf = len(bufs)
    gathers = [g for _, g in bufs]

    def body(q_ref, k_ref, v_ref, kt_ref, do_ref, o_ref, lse_ref, *rest):
        srcs, (dq_ref, dk_ref, dv_ref), dsts = rest[:n_buf], rest[n_buf:n_buf + 3], rest[n_buf + 3:2 * n_buf + 3]
        dk_acc, dv_acc = rest[2 * n_buf + 3:2 * n_buf + 5]
        copies = _xchg_copies(srcs, dsts, gathers, *rest[2 * n_buf + 5:])
        first, last = _grid_ends(grid)
        pl.when(first)(functools.partial(_xchg_start, copies))

        qi = pl.program_id(2)

        @pl.when(qi == 0)
        def _():
            dk_acc[...] = jnp.zeros_like(dk_acc)
            dv_acc[...] = jnp.zeros_like(dv_acc)

        mask = _chunk_mask(tq, keys_first=True)
        lane = lax.broadcasted_iota(jnp.int32, (tq, HEAD_PAD), 1)
        qs, dos, deltas, lses = [], [], [], []
        for hh in range(hpb):
            cols = slice((hh // 2) * HEAD_PAD, (hh // 2 + 1) * HEAD_PAD)
            do_pair = do_ref[0, :, cols]
            prod = do_pair.astype(F32) * o_ref[0, :, cols].astype(F32)
            delta = jnp.sum(jnp.where(lane // V_HEAD_DIM == hh % 2, prod, 0.0), axis=-1, keepdims=True)
            qs.append(q_ref[0, hh])
            dos.append(do_pair)
            deltas.append(jnp.broadcast_to(delta, (tq, HEAD_PAD)).T[0:1, :])
            lses.append(lse_ref[0, hh][0:1, :])

        def step(j, dqs, masked):
            rows = pl.ds(pl.multiple_of(j * tq, tq), tq)
            out = []
            for hh in range(hpb):
                s = lax.dot_general(k_ref[0, hh, rows, :], qs[hh], nt_dims, preferred_element_type=F32)
                p = jnp.exp2(s - lses[hh])
                if masked:
                    p = jnp.where(mask, p, 0.0)
                dv_acc[hh, rows, :] += jnp.dot(p.astype(BF16), dos[hh], preferred_element_type=F32)
                dp = lax.dot_general(v_ref[0, hh, rows, :], dos[hh], nt_dims, preferred_element_type=F32)
                ds = (p * (dp - deltas[hh])).astype(BF16)
                dk_acc[hh, rows, :] += jnp.dot(ds, qs[hh], preferred_element_type=F32)
                out.append(dqs[hh] + jnp.dot(kt_ref[0, hh, j], ds, preferred_element_type=F32))
            return tuple(out)

        dqs = tuple(jnp.zeros((HEAD_PAD, tq), F32) for _ in range(hpb))
        dqs = lax.fori_loop(0, qi, functools.partial(step, masked=False), dqs)
        dqs = step(qi, dqs, True)
        for hh in range(hpb):
            dq_ref[0, hh] = dqs[hh].T.astype(BF16)

        @pl.when(qi == nq - 1)
        def _():
            dk_ref[0] = (dk_acc[...] * LN2).astype(BF16)
            dv_ref[0] = dv_acc[...].astype(BF16)

        pl.when(last)(functools.partial(_xchg_finish, copies))

    full_spec = pl.BlockSpec((1, hpb, seq, HEAD_PAD), lambda b, hb, i: (b, hb, 0, 0))
    t_spec = pl.BlockSpec((1, hpb, nq, HEAD_PAD, tq), lambda b, hb, i: (b, hb, 0, 0, 0))
    q_spec = pl.BlockSpec((1, hpb, tq, HEAD_PAD), lambda b, hb, i: (b, hb, i, 0))
    o_spec = pl.BlockSpec((1, tq, hpb * V_HEAD_DIM), lambda b, hb, i: (b, i, hb))
    lse_spec = pl.BlockSpec((1, hpb, 8, tq), lambda b, hb, i: (b, hb, 0, i))
    head_shape = jax.ShapeDtypeStruct((n_b, N_HEADS, seq, HEAD_PAD), BF16)
    res = pl.pallas_call(
        body, name="attn_bwd", grid=grid,
        in_specs=[q_spec, full_spec, full_spec, t_spec, o_spec, o_spec, lse_spec] + [HBM_SPEC] * n_buf,
        out_specs=[q_spec, full_spec, full_spec] + [HBM_SPEC] * n_buf,
        out_shape=[head_shape] * 3 + _xchg_out_shapes(bufs),
        scratch_shapes=[pltpu.VMEM((hpb, seq, HEAD_PAD), F32), pltpu.VMEM((hpb, seq, HEAD_PAD), F32)]
        + _xchg_scratch(n_buf),
        compiler_params=_params(("arbitrary", "arbitrary", "arbitrary")),
    )(q, k, v, kt, do, o, lse, *[a for a, _ in bufs])
    return res[0], res[1], res[2], res[3:]


def _in_proj_bwd(parts, x2, dx1, scale, g, bufs, seq):
    n_rows, d = x2.shape
    tm = min(512, seq)
    per_seq = seq // tm
    grid = (n_rows // tm,)
    n_part, n_buf = len(parts), len(bufs)
    gathers = [gt for _, gt in bufs]
    nt_dims = _DIMS["nt"]

    def body(*refs):
        dz_refs, w_refs = refs[:n_part], refs[n_part:2 * n_part]
        x_ref, dx1_ref, sc_ref, g_ref = refs[2 * n_part:2 * n_part + 4]
        srcs = refs[2 * n_part + 4:2 * n_part + 4 + n_buf]
        gx_ref, dsc_ref, dsh_ref, dg_ref = refs[2 * n_part + 4 + n_buf:2 * n_part + 8 + n_buf]
        dsts = refs[2 * n_part + 8 + n_buf:2 * n_part + 8 + 2 * n_buf]
        copies = _xchg_copies(srcs, dsts, gathers, *refs[2 * n_part + 8 + 2 * n_buf:])
        first, last = _grid_ends(grid)
        pl.when(first)(functools.partial(_xchg_start, copies))

        i = pl.program_id(0)
        dh = None
        for dz_ref, w_ref in zip(dz_refs, w_refs):
            term = lax.dot_general(dz_ref[...], w_ref[...], nt_dims, preferred_element_type=F32)
            dh = term if dh is None else dh + term
        dx, dsc, dsh, dg = _norm_mod_bwd(x_ref[...], g_ref[...], sc_ref[0], dh)
        gx_ref[...] = dx1_ref[...] + dx

        @pl.when(i % per_seq == 0)
        def _():
            dsc_ref[...] = jnp.zeros_like(dsc_ref)
            dsh_ref[...] = jnp.zeros_like(dsh_ref)

        @pl.when(i == 0)
        def _():
            dg_ref[...] = jnp.zeros_like(dg_ref)

        dsc_ref[0] += dsc
        dsh_ref[0] += dsh
        dg_ref[...] += dg
        pl.when(last)(functools.partial(_xchg_finish, copies))

    row = lambda width: pl.BlockSpec((tm, width), lambda i: (i, 0))
    bat = pl.BlockSpec((1, 1, d), lambda i: (i // per_seq, 0, 0))
    whole = lambda arr: pl.BlockSpec(arr.shape, lambda i: (0, 0))
    n_b = n_rows // seq
    res = pl.pallas_call(
        body, name="in_proj_bwd", grid=grid,
        in_specs=[row(dz.shape[1]) for dz, _ in parts] + [whole(w) for _, w in parts]
        + [row(d), row(d), bat, whole(g)] + [HBM_SPEC] * n_buf,
        out_specs=[row(d), bat, bat, whole(g)] + [HBM_SPEC] * n_buf,
        out_shape=[jax.ShapeDtypeStruct((n_rows, d), F32), jax.ShapeDtypeStruct((n_b, 1, d), F32),
                   jax.ShapeDtypeStruct((n_b, 1, d), F32), jax.ShapeDtypeStruct(g.shape, F32)] + _xchg_out_shapes(bufs),
        scratch_shapes=_xchg_scratch(n_buf),
        compiler_params=_params(("arbitrary",)),
    )(*[dz for dz, _ in parts], *[w for _, w in parts], x2, dx1, scale, g, *[a for a, _ in bufs])
    return res[0], res[1], res[2], res[3], res[4:]


def _ln_silu(u1, g, b):
    mu = jnp.mean(u1, axis=-1, keepdims=True)
    uc = u1 - mu
    r = lax.rsqrt(jnp.mean(uc * uc, axis=-1, keepdims=True) + EPS)
    y = uc * r * g + b
    return y * _sigmoid(y)


def _conv_fill_glu(z_ref, u0_ref, seq, tile):
    u0_ref[0:CONV_HALO, :] = jnp.zeros((CONV_HALO, CONV_CH), F32)
    u0_ref[CONV_HALO + seq:CONV_HALO + seq + CONV_TAIL, :] = jnp.zeros((CONV_TAIL, CONV_CH), F32)
    for t in range(seq // tile):
        zt = z_ref[0, t * tile:(t + 1) * tile, :].astype(F32)
        u0_ref[CONV_HALO + t * tile:CONV_HALO + (t + 1) * tile, :] = zt[:, :CONV_CH] * _sigmoid(zt[:, CONV_CH:])


def _conv_windows(ref, views_ref, t, tile):
    for b in range(8):
        views_ref[b] = ref[t * tile + b:t * tile + b + tile + CONV_HALO, :]


def _conv_tap(views_ref, offset, tile):
    return views_ref[offset % 8, 8 * (offset // 8):8 * (offset // 8) + tile, :]


def _conv_tile(u0_ref, views_ref, w_ref, b_ref, t, tile):
    _conv_windows(u0_ref, views_ref, t, tile)
    acc = jnp.broadcast_to(b_ref[...], (tile, CONV_CH))
    for kk in range(CONV_WIDTH):
        acc = acc + w_ref[kk:kk + 1, :] * _conv_tap(views_ref, kk + CONV_HALO - (CONV_WIDTH - 1), tile)
    return acc


def _conv_fwd(zglu, conv_w, conv_b, ln_g, ln_b, n_b, seq):
    tile = min(256, seq)

    def body(z_ref, w_ref, b_ref, g_ref, bb_ref, o_ref, u1_ref, u0_ref, views_ref):
        _conv_fill_glu(z_ref, u0_ref, seq, tile)
        for t in range(seq // tile):
            u1 = _conv_tile(u0_ref, views_ref, w_ref, b_ref, t, tile)
            u1_ref[0, t * tile:(t + 1) * tile, :] = u1
            o_ref[0, t * tile:(t + 1) * tile, :] = _ln_silu(u1, g_ref[...], bb_ref[...]).astype(BF16)

    whole2 = lambda arr: pl.BlockSpec(arr.shape, lambda b: (0, 0))
    seq_spec = pl.BlockSpec((1, seq, CONV_CH), lambda b: (b, 0, 0))
    return pl.pallas_call(
        body, name="conv_fwd", grid=(n_b,),
        in_specs=[pl.BlockSpec((1, seq, 2 * CONV_CH), lambda b: (b, 0, 0)), whole2(conv_w), whole2(conv_b),
                  whole2(ln_g), whole2(ln_b)],
        out_specs=[seq_spec, seq_spec],
        out_shape=[jax.ShapeDtypeStruct((n_b, seq, CONV_CH), BF16), jax.ShapeDtypeStruct((n_b, seq, CONV_CH), F32)],
        scratch_shapes=[pltpu.VMEM((seq + CONV_HALO + CONV_TAIL, CONV_CH), F32),
                        pltpu.VMEM((8, tile + CONV_HALO, CONV_CH), F32)],
        compiler_params=_params(("parallel",)),
    )(zglu, conv_w, conv_b, ln_g, ln_b)


def _conv_bwd(zglu, u1_saved, du3, conv_w, ln_g, ln_b, n_b, seq):
    tile = min(256, seq)
    n_t = seq // tile

    def body(z_ref, u1_ref, du3_ref, w_ref, g_ref, bb_ref, dz_ref, dw_ref, db_ref, dg_ref, dbb_ref, u0_ref, du1_ref,
             u0_views, du1_views):
        @pl.when(pl.program_id(0) == 0)
        def _():
            for r in (dw_ref, db_ref, dg_ref, dbb_ref):
                r[...] = jnp.zeros_like(r)

        _conv_fill_glu(z_ref, u0_ref, seq, tile)
        du1_ref[seq:seq + CONV_HALO + CONV_TAIL, :] = jnp.zeros((CONV_HALO + CONV_TAIL, CONV_CH), F32)
        g = g_ref[...]
        for t in range(n_t):
            u1 = u1_ref[0, t * tile:(t + 1) * tile, :]
            mu = jnp.mean(u1, axis=-1, keepdims=True)
            uc = u1 - mu
            r = lax.rsqrt(jnp.mean(uc * uc, axis=-1, keepdims=True) + EPS)
            xh = uc * r
            y = xh * g + bb_ref[...]
            sg = _sigmoid(y)
            dy = du3_ref[0, t * tile:(t + 1) * tile, :].astype(F32) * (sg * (1.0 + y * (1.0 - sg)))
            dg_ref[...] += jnp.sum(dy * xh, axis=0, keepdims=True)
            dbb_ref[...] += jnp.sum(dy, axis=0, keepdims=True)
            dxh = dy * g
            du1 = r * (dxh - jnp.mean(dxh, axis=-1, keepdims=True) - xh * jnp.mean(dxh * xh, axis=-1, keepdims=True))
            db_ref[...] += jnp.sum(du1, axis=0, keepdims=True)
            du1_ref[t * tile:(t + 1) * tile, :] = du1
        for t in range(n_t):
            du1 = du1_ref[t * tile:(t + 1) * tile, :]
            du0 = jnp.zeros((tile, CONV_CH), F32)
            _conv_windows(u0_ref, u0_views, t, tile)
            _conv_windows(du1_ref, du1_views, t, tile)
            for kk in range(CONV_WIDTH):
                du0 = du0 + w_ref[kk:kk + 1, :] * _conv_tap(du1_views, CONV_WIDTH - 1 - kk, tile)
                u0_tap = _conv_tap(u0_views, kk + CONV_HALO - (CONV_WIDTH - 1), tile)
                dw_ref[kk:kk + 1, :] += jnp.sum(du1 * u0_tap, axis=0, keepdims=True)
            zt = z_ref[0, t * tile:(t + 1) * tile, :].astype(F32)
            ga, sb = zt[:, :CONV_CH], _sigmoid(zt[:, CONV_CH:])
            dz_ref[0, t * tile:(t + 1) * tile, :CONV_CH] = (du0 * sb).astype(BF16)
            dz_ref[0, t * tile:(t + 1) * tile, CONV_CH:] = (du0 * ga * sb * (1.0 - sb)).astype(BF16)

    whole2 = lambda arr: pl.BlockSpec(arr.shape, lambda b: (0, 0))
    z_spec = pl.BlockSpec((1, seq, 2 * CONV_CH), lambda b: (b, 0, 0))
    seq_spec = pl.BlockSpec((1, seq, CONV_CH), lambda b: (b, 0, 0))
    return pl.pallas_call(
        body, name="conv_bwd", grid=(n_b,),
        in_specs=[z_spec, seq_spec, seq_spec, whole2(conv_w), whole2(ln_g), whole2(ln_b)],
        out_specs=[z_spec, whole2(conv_w), whole2(ln_g), whole2(ln_g), whole2(ln_b)],
        out_shape=[jax.ShapeDtypeStruct((n_b, seq, 2 * CONV_CH), BF16), jax.ShapeDtypeStruct(conv_w.shape, F32),
                   jax.ShapeDtypeStruct(ln_g.shape, F32), jax.ShapeDtypeStruct(ln_g.shape, F32),
                   jax.ShapeDtypeStruct(ln_b.shape, F32)],
        scratch_shapes=[pltpu.VMEM((seq + CONV_HALO + CONV_TAIL, CONV_CH), F32)] * 2
        + [pltpu.VMEM((8, tile + CONV_HALO, CONV_CH), F32)] * 2,
        compiler_params=_params(("arbitrary",)),
    )(zglu, u1_saved, du3, conv_w, ln_g, ln_b)


def _adamw(name, w, parts, m, v):
    n_parts = parts.shape[0]

    def body(w_ref, p_ref, m_ref, v_ref, g_ref, d_ref, nm_ref, nv_ref):
        gg = p_ref[0].astype(F32)
        for j in range(1, n_parts):
            gg = gg + p_ref[j].astype(F32)
        nm = ADAM_B1 * m_ref[...] + (1.0 - ADAM_B1) * gg
        nv = ADAM_B2 * v_ref[...] + (1.0 - ADAM_B2) * jnp.square(gg)
        m_hat = nm / (1.0 - ADAM_B1 ** ADAM_STEP)
        v_hat = nv / (1.0 - ADAM_B2 ** ADAM_STEP)
        g_ref[...] = gg
        d_ref[...] = -ADAM_LR * (m_hat / (jnp.sqrt(v_hat) + ADAM_EPS) + ADAM_WD * w_ref[...])
        nm_ref[...] = nm
        nv_ref[...] = nv

    shape = jax.ShapeDtypeStruct(w.shape, F32)
    return pl.pallas_call(body, name=name, out_shape=[shape] * 4, compiler_params=_params(None))(w, parts, m, v)


def _rope_tables(seq):
    inv_freq = ROPE_THETA ** (-jnp.arange(0, QK_ROPE_DIM, 2, dtype=F32) / QK_ROPE_DIM)
    ang = jnp.arange(seq, dtype=F32)[:, None] * inv_freq[None, :]
    cos, sin = jnp.cos(ang), jnp.sin(ang)
    half = QK_ROPE_DIM // 2
    z = lambda n: jnp.zeros((seq, n), F32)
    tail = HEAD_PAD - QK_HEAD_DIM
    cos_t = jnp.concatenate([jnp.ones((seq, QK_NOPE_DIM), F32), cos, cos, z(tail)], axis=1)
    sin_lo = jnp.concatenate([z(QK_NOPE_DIM), -sin, z(half), z(tail)], axis=1)
    sin_hi = jnp.concatenate([z(QK_NOPE_DIM), z(half), sin, z(tail)], axis=1)
    return cos_t, sin_lo, sin_hi


def _pad_lanes(v, width=HEAD_PAD):
    return jnp.pad(v, [(0, 0)] * (v.ndim - 1) + [(0, width - v.shape[-1])])


def _unstack_cols(s):
    return s.transpose(1, 0, 2).reshape(s.shape[1], N_DEV * s.shape[2])


def _stack_cols(g, dtype):
    rows, cols = g.shape
    return g.reshape(rows, N_DEV, cols // N_DEV).transpose(1, 0, 2).astype(dtype)


def kernel(x, c, w_ada, b_ada, norm1_g, w_in, q_latent_g, w_uq, kv_latent_g, w_ukv, qk_norm_q_g, qk_norm_k_g, w_o_mla, conv_w, conv_b, conv_ln_g, conv_ln_b, w_pw_out, w_out, norm2_g, w_ff1, w_ff2, loss_target, m_w_ada, m_b_ada, m_norm1_g, m_w_in, m_q_latent_g, m_w_uq, m_kv_latent_g, m_w_ukv, m_qk_norm_q_g, m_qk_norm_k_g, m_w_o_mla, m_conv_w, m_conv_b, m_conv_ln_g, m_conv_ln_b, m_w_pw_out, m_w_out, m_norm2_g, m_w_ff1, m_w_ff2, v_w_ada, v_b_ada, v_norm1_g, v_w_in, v_q_latent_g, v_w_uq, v_kv_latent_g, v_w_ukv, v_qk_norm_q_g, v_qk_norm_k_g, v_w_o_mla, v_conv_w, v_conv_b, v_conv_ln_g, v_conv_ln_b, v_w_pw_out, v_w_out, v_norm2_g, v_w_ff1, v_w_ff2):
    given = dict(locals())
    local = {n: given[n][0] for n in WEIGHTS}
    vec = {n: local[n].reshape(1, -1) for n in REPLICATED}
    bf = lambda n: local[n].astype(BF16)
    n_b, seq, d = x.shape
    n_rows = n_b * seq
    x2 = x.reshape(n_rows, d)
    t2 = loss_target.reshape(n_rows, d)
    rw = functools.partial(_rowwise, n_rows=n_rows, seq=seq)
    me = 4 * lax.axis_index("x") + 2 * lax.axis_index("y") + lax.axis_index("c")
    ada_cols = local["w_ada"].shape[1]

    c_all, w_in_s, w_uq_s, w_ukv_s, conv_w_s = _exchange(
        "gather_early", [(c, True), (bf("w_in"), True), (bf("w_uq"), True), (bf("w_ukv"), True), (local["conv_w"], True)],
        by_chip=True)
    w_in_f = _unstack_cols(w_in_s)
    zeros = lambda n: jnp.zeros((d, n), BF16)
    w_sm = jnp.concatenate([w_in_f[:, :OFF_KV], zeros(QK_NOPE_DIM), w_in_f[:, OFF_KV:OFF_KR], zeros(HEAD_PAD - QK_HEAD_DIM)], axis=1)
    w_glu = w_in_f[:, OFF_KR:OFF_GLU]
    w_gate = w_in_f[:, OFF_GLU:]
    wuq = _pad_lanes(_unstack_cols(w_uq_s).reshape(Q_LORA, N_HEADS, QK_HEAD_DIM)).reshape(Q_LORA, N_HEADS * HEAD_PAD)
    wukv_f = _unstack_cols(w_ukv_s).reshape(KV_LORA, N_HEADS, QK_NOPE_DIM + V_HEAD_DIM)
    wv = wukv_f[:, :, QK_NOPE_DIM:]
    odd = (jnp.arange(N_HEADS) % 2 == 1)[None, :, None]
    wuv = jnp.where(odd, jnp.pad(wv, ((0, 0), (0, 0), (V_HEAD_DIM, 0))), jnp.pad(wv, ((0, 0), (0, 0), (0, V_HEAD_DIM))))
    wukv = jnp.concatenate([_pad_lanes(wukv_f[:, :, :QK_NOPE_DIM]), wuv], axis=1).reshape(KV_LORA, 2 * N_HEADS * HEAD_PAD)
    gqn = _pad_lanes(vec["qk_norm_q_g"])
    gkn = _pad_lanes(vec["qk_norm_k_g"])
    conv_w_f = jnp.pad(_unstack_cols(conv_w_s), ((0, 1), (0, 0)))
    rope = _rope_tables(seq)

    all_rows = N_DEV * n_b
    pad_rows = (-all_rows) % ROWS_PAD
    c_rows = jnp.pad(c_all.reshape(all_rows, d), ((0, pad_rows), (0, 0)))
    b_cols = lax.dynamic_slice(local["b_ada"], (me * ada_cols,), (ada_cols,))
    mod_cols = _mm("ada_fwd", c_rows, local["w_ada"], "nn", F32, a_fn=_silu, epi=lambda acc, b: acc + b,
                   epi_in=(jnp.broadcast_to(b_cols, (all_rows + pad_rows, ada_cols)),))
    (mod_s,) = _exchange("scatter_mod", [(mod_cols[:all_rows].reshape(N_DEV, n_b, ada_cols), False)])
    mod = mod_s.transpose(1, 0, 2).reshape(n_b, ADA_CHUNKS, 1, d)
    shift1, scale1, gate1, shift2, scale2, gate2 = [mod[:, i] for i in range(ADA_CHUNKS)]

    (h,) = rw("norm1_fwd", lambda r, b, cc: ([_norm_mod(r[0], cc[0], b[0], b[1])], [], []),
              rows=[_full(x2)], bats=[scale1, shift1], consts=[vec["norm1_g"]], outs=[(d, BF16)])
    zsm = _mm("in_proj_sm", h, w_sm, "nn", F32)
    zglu = _mm("in_proj_glu", h, w_glu, "nn", BF16)
    zgate = _mm("in_proj_gate", h, w_gate, "nn", BF16)
    q, k, v, kt = _mla_prep_fwd(zsm, wuq, wukv, vec["q_latent_g"], vec["kv_latent_g"], gqn, gkn, rope, n_b, seq)
    attn, lse, (w_o_s, w_pw_s, w_out_s, w_ff1_s, w_ff2_s) = _attn_fwd(
        q, k, v, [(bf("w_o_mla"), True), (bf("w_pw_out"), True), (bf("w_out"), True), (bf("w_ff1"), True),
                  (bf("w_ff2"), True)], n_b, seq)
    w_o_f = _unstack_cols(w_o_s)
    w_pw_f = _unstack_cols(w_pw_s)
    w_out_f = w_out_s.reshape(d, d)
    w_ff2_f = w_ff2_s.reshape(N_DEV * w_ff2_s.shape[1], d)
    attn2 = attn.reshape(n_rows, N_HEADS * V_HEAD_DIM)
    u3, u1 = _conv_fwd(zglu.reshape(n_b, seq, 2 * CONV_CH), conv_w_f, vec["conv_b"], vec["conv_ln_g"], vec["conv_ln_b"], n_b, seq)
    u32 = u3.reshape(n_rows, CONV_CH)
    ya = _mm("mla_out", attn2, w_o_f, "nn", BF16)
    yb = _mm("conv_out", u32, w_pw_f, "nn", BF16)
    (mrg,) = rw("merge_fwd",
                lambda r, b, cc: ([_sigmoid(r[0].astype(F32)) * r[2].astype(F32) + _sigmoid(r[1].astype(F32)) * r[3].astype(F32)], [], []),
                rows=[(zgate, d, 0), (zgate, d, 1), _full(ya), _full(yb)], outs=[(d, BF16)])
    mixed = _mm("out_proj", mrg, w_out_f, "nn", BF16)

    def mid_fn(r, b, cc):
        x1_ = r[0] + b[0] * r[1].astype(F32)
        return [x1_, _norm_mod(x1_, cc[0], b[1], b[2])], [], []

    x1, h2 = rw("norm2_fwd", mid_fn, rows=[_full(x2), _full(mixed)], bats=[gate1, scale2, shift2],
                consts=[vec["norm2_g"]], outs=[(d, F32), (d, BF16)])
    a = _mm("ff1", h2, w_ff1_s, "nn", BF16, b_stacked=True)
    f = _mm("ff2", a, w_ff2_f, "nn", BF16, a_fn=_relu2)

    def loss_fn(r, b, cc):
        ff = r[1].astype(F32)
        err = r[0] + b[0] * ff - r[2]
        dy_ = err * (1.0 / d)
        sq = jnp.broadcast_to(jnp.sum(err * err, keepdims=True), (1, LANES))
        return [dy_, b[0] * dy_], [jnp.sum(dy_ * ff, axis=0, keepdims=True)], [sq]

    dy, df, dgate2, sq_err = rw("loss", loss_fn, rows=[_full(x1), _full(f), _full(t2)], bats=[gate2],
                                outs=[(d, F32), (d, BF16)], bat_outs=[d], tot_outs=[(1, LANES)])
    loss = lax.psum(sq_err[0, 0] * (0.5 / d), MESH_AXES)

    da = _mm("ff2_bwd", df, w_ff2_f, "nt", BF16, epi=lambda acc, av: acc * 2.0 * jnp.maximum(av, 0.0), epi_in=(a,))
    g_ff2 = _mm("ff2_dw", a, df, "tn", BF16, a_fn=_relu2)
    dh2 = _mm("ff1_bwd", da, w_ff1_s, "nt", F32, b_stacked=True)
    g_ff1_s = _mm("ff1_dw", h2, da, "tn", BF16, out_stacked=True)

    def mid_bwd(r, b, cc):
        dx, dsc, dsh, dg = _norm_mod_bwd(r[0], cc[0], b[0], r[1])
        dx1_ = r[2] + dx
        return [dx1_, b[1] * dx1_], [dsc, dsh, jnp.sum(dx1_ * r[3].astype(F32), axis=0, keepdims=True)], [dg]

    dx1, dmixed, dscale2, dshift2, dgate1, g_norm2 = rw(
        "norm2_bwd", mid_bwd, rows=[_full(x1), _full(dh2), _full(dy), _full(mixed)], bats=[scale2, gate1],
        consts=[vec["norm2_g"]], outs=[(d, F32), (d, BF16)], bat_outs=[d, d, d], tot_outs=[(1, d)])

    dmrg = _mm("out_proj_bwd", dmixed, w_out_f, "nt", BF16)
    g_out = _mm("out_proj_dw", mrg, dmixed, "tn", BF16)

    def merge_bwd(r, b, cc):
        dm, ya_, yb_ = r[0].astype(F32), r[3].astype(F32), r[4].astype(F32)
        sa, sb = _sigmoid(r[1].astype(F32)), _sigmoid(r[2].astype(F32))
        return [dm * ya_ * sa * (1.0 - sa), dm * yb_ * sb * (1.0 - sb), dm * sa, dm * sb], [], []

    dzga, dzgb, dya, dyb = rw("merge_bwd", merge_bwd,
                              rows=[_full(dmrg), (zgate, d, 0), (zgate, d, 1), _full(ya), _full(yb)],
                              outs=[(d, BF16)] * 4)
    dattn = _mm("mla_out_bwd", dya, w_o_f, "nt", BF16)
    g_o = _mm("mla_out_dw", attn2, dya, "tn", F32)
    du3 = _mm("conv_out_bwd", dyb, w_pw_f, "nt", BF16)
    g_pw = _mm("conv_out_dw", u32, dyb, "tn", F32)

    dzglu, g_conv_w, g_conv_b, g_ln_g, g_ln_b = _conv_bwd(
        zglu.reshape(n_b, seq, 2 * CONV_CH), u1, du3.reshape(n_b, seq, CONV_CH), conv_w_f, vec["conv_ln_g"],
        vec["conv_ln_b"], n_b, seq)
    dzglu = dzglu.reshape(n_rows, 2 * CONV_CH)

    dq, dk, dv, (p_ff2, p_ff1, p_out, p_pw, p_o) = _attn_bwd(
        q, k, v, kt, dattn.reshape(n_b, seq, N_HEADS * V_HEAD_DIM), attn, lse,
        [(g_ff2.reshape(N_DEV, -1, d), False), (g_ff1_s, False), (g_out.reshape(N_DEV, -1, d), False),
         (_stack_cols(g_pw, BF16), False), (_stack_cols(g_o, BF16), False)], n_b, seq)
    dzsm, g_wuq, g_wukv, g_gq, g_gkv, g_gqn, g_gkn = _mla_prep_bwd(
        zsm, dq, dk, dv, wuq, wukv, vec["q_latent_g"], vec["kv_latent_g"], gqn, gkn, rope, n_b, seq)

    g_gate_a = _mm("in_proj_gate_dw_a", h, dzga, "tn", F32)
    g_gate_b = _mm("in_proj_gate_dw_b", h, dzgb, "tn", F32)
    g_glu = _mm("in_proj_glu_dw", h, dzglu, "tn", F32)
    g_sm = _mm("in_proj_sm_dw", h, dzsm, "tn", F32)
    g_in = jnp.concatenate([g_sm[:, :OFF_KV], g_sm[:, OFF_KV + QK_NOPE_DIM:OFF_KV + QK_NOPE_DIM + QK_ROPE_DIM], g_glu,
                            g_gate_a, g_gate_b], axis=1)
    g_uq = g_wuq.reshape(Q_LORA, N_HEADS, HEAD_PAD)[:, :, :QK_HEAD_DIM].reshape(Q_LORA, N_HEADS * QK_HEAD_DIM)
    g_wukv = g_wukv.reshape(KV_LORA, 2, N_HEADS, HEAD_PAD)
    g_v = jnp.where(odd, g_wukv[:, 1, :, V_HEAD_DIM:], g_wukv[:, 1, :, :V_HEAD_DIM])
    g_ukv = jnp.concatenate([g_wukv[:, 0, :, :QK_NOPE_DIM], g_v], axis=2).reshape(KV_LORA, -1)

    grad_x, dscale1, dshift1, g_norm1, (p_in, p_uq, p_ukv, p_conv_w) = _in_proj_bwd(
        [(dzga, w_gate[:, :d]), (dzgb, w_gate[:, d:]), (dzglu, w_glu), (dzsm, w_sm)], x2, dx1, scale1, vec["norm1_g"],
        [(_stack_cols(g_in, BF16), False), (_stack_cols(g_uq, BF16), False), (_stack_cols(g_ukv, BF16), False),
         (_stack_cols(g_conv_w[:CONV_WIDTH], F32), False)], seq)

    dmod = jnp.concatenate([dshift1, dscale1, dgate1, dshift2, dscale2, dgate2], axis=1).reshape(n_b, N_DEV, ada_cols)
    (dmod_s,) = _exchange("scatter_dmod", [(dmod.transpose(1, 0, 2), False)])
    dmod_rows = jnp.pad(dmod_s.reshape(all_rows, ada_cols), ((0, pad_rows), (0, 0)))
    g_ada = _mm("ada_dw", c_rows, dmod_rows, "tn", F32, a_fn=_silu)
    (g_b_cols,) = _rowwise("ada_db", lambda r, b, cc: ([], [], [jnp.sum(r[0], axis=0, keepdims=True)]),
                           all_rows + pad_rows, all_rows + pad_rows, rows=[_full(dmod_rows)], tot_outs=[(1, ada_cols)])

    partial_of = {"norm1_g": g_norm1, "q_latent_g": g_gq, "kv_latent_g": g_gkv, "qk_norm_q_g": g_gqn,
                  "qk_norm_k_g": g_gkn, "conv_b": g_conv_b, "conv_ln_g": g_ln_g, "conv_ln_b": g_ln_b, "norm2_g": g_norm2}
    names = [n for n in REPLICATED if n != "b_ada"]
    pieces = [_pad_lanes(partial_of[n], -(-partial_of[n].shape[1] // LANES) * LANES) for n in names] + [g_b_cols]
    widths = [p.shape[1] for p in pieces]
    small = jnp.concatenate(pieces, axis=1)
    small = _pad_lanes(small, -(-small.shape[1] // (8 * LANES)) * 8 * LANES).reshape(-1, LANES)
    (small_s,) = _exchange("gather_small_grads", [(small, True)])
    small_s = small_s.reshape(N_DEV, 1, -1)
    parts = {}
    off = 0
    for n, wd in zip(names, widths):
        parts[n] = small_s[:, :, off:off + vec[n].shape[1]]
        off += wd
    parts["b_ada"] = small_s[:, 0, off:off + ada_cols].reshape(1, 1, N_DEV * ada_cols)
    parts.update({"w_ada": g_ada[None], "w_in": p_in, "w_uq": p_uq, "w_ukv": p_ukv, "w_o_mla": p_o, "conv_w": p_conv_w,
                  "w_pw_out": p_pw, "w_out": p_out, "w_ff1": p_ff1, "w_ff2": p_ff2})

    grad_out, delta_out, m_out, v_out = [], [], [], []
    for n in WEIGHTS:
        shape2 = parts[n].shape[1:]
        g_w, d_w, n_m, n_v = _adamw("adamw_" + n, local[n].reshape(shape2), parts[n], given["m_" + n].reshape(shape2),
                                    given["v_" + n].reshape(shape2))
        full_shape = given[n].shape
        grad_out.append(g_w.reshape(full_shape))
        delta_out.append(d_w.reshape(full_shape))
        m_out.append(n_m.reshape(full_shape))
        v_out.append(n_v.reshape(full_shape))
    return (loss, grad_x.reshape(n_b, seq, d), *grad_out, *delta_out, *m_out, *v_out)
```

```python
import functools

import jax
import jax.numpy as jnp
from jax import lax
from jax.experimental import pallas as pl
from jax.experimental.pallas import tpu as pltpu

F32 = jnp.float32
BF16 = jnp.bfloat16

N_DEV = 8
MESH_AXES = ("x", "y", "c")
EPS = 1e-6
N_HEADS = 8
QK_HEAD_DIM = 96
QK_NOPE_DIM = 64
QK_ROPE_DIM = 32
V_HEAD_DIM = 64
HEAD_PAD = 128
Q_LORA = 256
KV_LORA = 128
CONV_CH = 512
CONV_WIDTH = 31
CONV_HALO = 32
CONV_TAIL = 8
CHUNK = 64
ROPE_THETA = 10000.0
OFF_Q = Q_LORA
OFF_KV = OFF_Q + KV_LORA
OFF_KR = OFF_KV + QK_ROPE_DIM
OFF_GLU = OFF_KR + 2 * CONV_CH
ADA_CHUNKS = 6
ADAM_LR = 0.001
ADAM_B1 = 0.9
ADAM_B2 = 0.999
ADAM_EPS = 1e-08
ADAM_WD = 0.01
ADAM_STEP = 10
LANES = 128
VMEM_LIMIT = 56 * 1024 * 1024
NEG_BIG = -1e30
ATT_HEADS = 4
ATT_TILE = 256
PREP_TILE = 1024
ATT_SCALE = QK_HEAD_DIM ** -0.5
LOG2E = 1.4426950408889634
LN2 = 0.6931471805599453
QK_SCALE = ATT_SCALE * LOG2E
ROWS_PAD = 16

REPLICATED = ("b_ada", "norm1_g", "q_latent_g", "kv_latent_g", "qk_norm_q_g", "qk_norm_k_g", "conv_b", "conv_ln_g",
              "conv_ln_b", "norm2_g")
WEIGHTS = ("w_ada", "b_ada", "norm1_g", "w_in", "q_latent_g", "w_uq", "kv_latent_g", "w_ukv", "qk_norm_q_g",
           "qk_norm_k_g", "w_o_mla", "conv_w", "conv_b", "conv_ln_g", "conv_ln_b", "w_pw_out", "w_out", "norm2_g",
           "w_ff1", "w_ff2")


def _tile(dim, pref):
    if dim <= pref:
        return dim
    t = (pref // LANES) * LANES
    while dim % t:
        t -= LANES
    return t


def _params(semantics):
    return pltpu.CompilerParams(dimension_semantics=semantics, vmem_limit_bytes=VMEM_LIMIT)


def _sigmoid(v):
    return 1.0 / (1.0 + jnp.exp(-v))


def _silu(v):
    return v * _sigmoid(v)


def _relu2(v):
    return jnp.square(jnp.maximum(v, 0.0))


_DIMS = {"nn": (((1,), (0,)), ((), ())), "nt": (((1,), (1,)), ((), ())), "tn": (((0,), (0,)), ((), ()))}


def _mm(name, a, b, mode, out_dtype, *, a_fn=None, epi=None, epi_in=(), b_stacked=False, out_stacked=False,
        tm=1024, tn=1024, tk=1024):
    if b_stacked:
        shard = b.shape[2]
        b_rows, b_cols = b.shape[1], N_DEV * shard
    else:
        b_rows, b_cols = b.shape
    if mode == "nn":
        (m, k), n = a.shape, b_cols
    elif mode == "nt":
        (m, k), n = a.shape, b_rows
    else:
        (k, m), n = a.shape, b_cols
    if out_stacked:
        shard = n // N_DEV
    tm = _tile(m, tm)
    tn = _tile(shard, tn) if (out_stacked or (b_stacked and mode != "nt")) else _tile(n, tn)
    tk = _tile(shard, tk) if (b_stacked and mode == "nt") else _tile(k, tk)
    nk = k // tk
    a_spec = (pl.BlockSpec((tk, tm), lambda i, j, kk: (kk, i)) if mode == "tn"
              else pl.BlockSpec((tm, tk), lambda i, j, kk: (i, kk)))
    if b_stacked and mode == "nt":
        per = shard // tk
        b_spec = pl.BlockSpec((None, tn, tk), lambda i, j, kk: (kk // per, j, kk % per))
    elif b_stacked:
        per = shard // tn
        b_spec = pl.BlockSpec((None, tk, tn), lambda i, j, kk: (j // per, kk, j % per))
    elif mode == "nt":
        b_spec = pl.BlockSpec((tn, tk), lambda i, j, kk: (j, kk))
    else:
        b_spec = pl.BlockSpec((tk, tn), lambda i, j, kk: (kk, j))
    e_spec = pl.BlockSpec((tm, tn), lambda i, j, kk: (i, j))
    if out_stacked:
        per_o = shard // tn
        o_spec = pl.BlockSpec((None, tm, tn), lambda i, j, kk: (j // per_o, i, j % per_o))
        out_shape = jax.ShapeDtypeStruct((N_DEV, m, shard), out_dtype)
    else:
        o_spec = e_spec
        out_shape = jax.ShapeDtypeStruct((m, n), out_dtype)
    n_epi = len(epi_in)

    def body(a_ref, b_ref, *rest):
        epi_refs, o_ref, acc_ref = rest[:n_epi], rest[n_epi], rest[n_epi + 1]
        kk = pl.program_id(2)

        @pl.when(kk == 0)
        def _():
            acc_ref[...] = jnp.zeros_like(acc_ref)

        av = a_ref[...]
        if a_fn is not None:
            av = a_fn(av.astype(F32))
        acc_ref[...] += lax.dot_general(av.astype(BF16), b_ref[...].astype(BF16), _DIMS[mode],
                                        preferred_element_type=F32)

        @pl.when(kk == nk - 1)
        def _():
            acc = acc_ref[...]
            if epi is not None:
                acc = epi(acc, *[r[...].astype(F32) for r in epi_refs])
            o_ref[...] = acc.astype(out_dtype)

    return pl.pallas_call(
        body, name=name, grid=(m // tm, n // tn, nk),
        in_specs=[a_spec, b_spec] + [e_spec] * n_epi, out_specs=o_spec, out_shape=out_shape,
        scratch_shapes=[pltpu.VMEM((tm, tn), F32)],
        compiler_params=_params(("parallel", "parallel", "arbitrary")),
    )(a, b, *epi_in)


def _rowwise(name, fn, n_rows, seq, rows, bats=(), consts=(), outs=(), bat_outs=(), tot_outs=(), tm=256):
    tm = min(tm, seq)
    per_seq = seq // tm
    n_b = n_rows // seq
    nr, nb, nc, no, nbo, nto = len(rows), len(bats), len(consts), len(outs), len(bat_outs), len(tot_outs)

    def body(*refs):
        i = pl.program_id(0)
        r_in = [r[...] for r in refs[:nr]]
        b_in = [r[0] for r in refs[nr:nr + nb]]
        c_in = [r[...] for r in refs[nr + nb:nr + nb + nc]]
        o_refs = refs[nr + nb + nc:nr + nb + nc + no]
        bo_refs = refs[nr + nb + nc + no:nr + nb + nc + no + nbo]
        to_refs = refs[nr + nb + nc + no + nbo:]
        o_val, bo_val, to_val = fn(r_in, b_in, c_in)
        for r, v in zip(o_refs, o_val):
            r[...] = v.astype(r.dtype)
        if nbo:
            @pl.when(i % per_seq == 0)
            def _():
                for r in bo_refs:
                    r[...] = jnp.zeros_like(r)

            for r, v in zip(bo_refs, bo_val):
                r[0] += v
        if nto:
            @pl.when(i == 0)
            def _():
                for r in to_refs:
                    r[...] = jnp.zeros_like(r)

            for r, v in zip(to_refs, to_val):
                r[...] += v

    in_specs = [pl.BlockSpec((tm, w), functools.partial(lambda cb, i: (i, cb), cb)) for (_, w, cb) in rows]
    in_specs += [pl.BlockSpec((1, 1, bt.shape[2]), lambda i: (i // per_seq, 0, 0)) for bt in bats]
    in_specs += [pl.BlockSpec(ct.shape, lambda i: (0, 0)) for ct in consts]
    out_specs = [pl.BlockSpec((tm, w), lambda i: (i, 0)) for (w, _) in outs]
    out_specs += [pl.BlockSpec((1, 1, w), lambda i: (i // per_seq, 0, 0)) for w in bat_outs]
    out_specs += [pl.BlockSpec(shp, lambda i: (0, 0)) for shp in tot_outs]
    out_shape = [jax.ShapeDtypeStruct((n_rows, w), dt) for (w, dt) in outs]
    out_shape += [jax.ShapeDtypeStruct((n_b, 1, w), F32) for w in bat_outs]
    out_shape += [jax.ShapeDtypeStruct(shp, F32) for shp in tot_outs]
    res = pl.pallas_call(
        body, name=name, grid=(n_rows // tm,), in_specs=in_specs, out_specs=out_specs, out_shape=out_shape,
        compiler_params=_params(("arbitrary",)),
    )(*[r[0] for r in rows], *bats, *consts)
    return res


def _full(arr):
    return (arr, arr.shape[1], 0)


def _mm_rows(name, a_rows, a_fn, w, mode, fn, n_rows, seq, rows=(), bats=(), consts=(), outs=(), bat_outs=(),
             tot_outs=(), w_stacked=False, a_out=None, tm=512, tk=1024):
    tm = min(tm, seq)
    per_seq = seq // tm
    n_b = n_rows // seq
    k = a_rows[0][1]
    if w_stacked:
        n_out, tk = w.shape[1], _tile(w.shape[2], tk)
        w_spec = pl.BlockSpec((None, n_out, tk), lambda i, kk: (kk // (w.shape[2] // tk), 0, kk % (w.shape[2] // tk)))
    elif mode == "nt":
        n_out, tk = w.shape[0], _tile(k, tk)
        w_spec = pl.BlockSpec((n_out, tk), lambda i, kk: (0, kk))
    else:
        n_out, tk = w.shape[1], _tile(k, tk)
        w_spec = pl.BlockSpec((tk, n_out), lambda i, kk: (kk, 0))
    nk = k // tk
    na, nr, nb, nc = len(a_rows), len(rows), len(bats), len(consts)
    n_extra = 0 if a_out is None else 1
    no, nbo, nto = len(outs), len(bat_outs), len(tot_outs)

    def body(*refs):
        i, kk = pl.program_id(0), pl.program_id(1)
        a_refs, w_ref = refs[:na], refs[na]
        pos = na + 1
        r_refs, b_refs, c_refs = refs[pos:pos + nr], refs[pos + nr:pos + nr + nb], refs[pos + nr + nb:pos + nr + nb + nc]
        pos += nr + nb + nc
        ao_refs = refs[pos:pos + n_extra]
        pos += n_extra
        o_refs, bo_refs, to_refs = refs[pos:pos + no], refs[pos + no:pos + no + nbo], refs[pos + no + nbo:pos + no + nbo + nto]
        acc_ref = refs[pos + no + nbo + nto]

        @pl.when(kk == 0)
        def _():
            acc_ref[...] = jnp.zeros_like(acc_ref)

        tiles = [r[...] for r in a_refs]
        av = a_fn([t.astype(F32) for t in tiles]) if a_fn is not None else tiles[0]
        av = av.astype(BF16)
        if n_extra:
            ao_refs[0][...] = av.astype(ao_refs[0].dtype)
        acc_ref[...] += lax.dot_general(av, w_ref[...].astype(BF16), _DIMS[mode], preferred_element_type=F32)

        @pl.when(kk == nk - 1)
        def _():
            o_val, bo_val, to_val = fn(acc_ref[...], [r[...] for r in r_refs], [r[0] for r in b_refs],
                                       [r[...] for r in c_refs])
            for r, v in zip(o_refs, o_val):
                r[...] = v.astype(r.dtype)
            if nbo:
                @pl.when(i % per_seq == 0)
                def _():
                    for r in bo_refs:
                        r[...] = jnp.zeros_like(r)

                for r, v in zip(bo_refs, bo_val):
                    r[0] += v
            if nto:
                @pl.when(i == 0)
                def _():
                    for r in to_refs:
                        r[...] = jnp.zeros_like(r)

                for r, v in zip(to_refs, to_val):
                    r[...] += v

    in_specs = [pl.BlockSpec((tm, tk), functools.partial(lambda cb, i, kk: (i, kk + cb), cb)) for (_, _, cb) in a_rows]
    in_specs += [w_spec]
    in_specs += [pl.BlockSpec((tm, wd), functools.partial(lambda cb, i, kk: (i, cb), cb)) for (_, wd, cb) in rows]
    in_specs += [pl.BlockSpec((1, 1, bt.shape[2]), lambda i, kk: (i // per_seq, 0, 0)) for bt in bats]
    in_specs += [pl.BlockSpec(ct.shape, lambda i, kk: (0, 0)) for ct in consts]
    out_specs = [pl.BlockSpec((tm, tk), lambda i, kk: (i, kk))] * n_extra
    out_specs += [pl.BlockSpec((tm, wd), lambda i, kk: (i, 0)) for (wd, _) in outs]
    out_specs += [pl.BlockSpec((1, 1, wd), lambda i, kk: (i // per_seq, 0, 0)) for wd in bat_outs]
    out_specs += [pl.BlockSpec(shp, lambda i, kk: (0, 0)) for shp in tot_outs]
    out_shape = [jax.ShapeDtypeStruct((n_rows, k), a_out)] if n_extra else []
    out_shape += [jax.ShapeDtypeStruct((n_rows, wd), dt) for (wd, dt) in outs]
    out_shape += [jax.ShapeDtypeStruct((n_b, 1, wd), F32) for wd in bat_outs]
    out_shape += [jax.ShapeDtypeStruct(shp, F32) for shp in tot_outs]
    return pl.pallas_call(
        body, name=name, grid=(n_rows // tm, nk), in_specs=in_specs, out_specs=out_specs, out_shape=out_shape,
        scratch_shapes=[pltpu.VMEM((tm, n_out), F32)],
        compiler_params=_params(("arbitrary", "arbitrary")),
    )(*[a for a, _, _ in a_rows], w, *[r[0] for r in rows], *bats, *consts)


def _norm_mod(x, g, scale, shift):
    r = lax.rsqrt(jnp.mean(x * x, axis=-1, keepdims=True) + EPS)
    xh = x * r
    return xh * g * (1.0 + scale) + shift


def _norm_mod_bwd(x, g, scale, dh):
    r = lax.rsqrt(jnp.mean(x * x, axis=-1, keepdims=True) + EPS)
    xh = x * r
    dn = dh * (1.0 + scale)
    dxh = dn * g
    dx = r * (dxh - xh * jnp.mean(dxh * xh, axis=-1, keepdims=True))
    dscale = jnp.sum(dh * xh * g, axis=0, keepdims=True)
    dshift = jnp.sum(dh, axis=0, keepdims=True)
    dg = jnp.sum(dn * xh, axis=0, keepdims=True)
    return dx, dscale, dshift, dg


def _rms(v, g):
    r = lax.rsqrt(jnp.mean(v * v, axis=-1, keepdims=True) + EPS)
    return v * r * g


def _rms_bwd(v, g, dy):
    r = lax.rsqrt(jnp.mean(v * v, axis=-1, keepdims=True) + EPS)
    vh = v * r
    dvh = dy * g
    dv = r * (dvh - vh * jnp.mean(dvh * vh, axis=-1, keepdims=True))
    return dv, jnp.sum(dy * vh, axis=0, keepdims=True)


def _head_norm(v, g):
    r = lax.rsqrt(jnp.sum(v * v, axis=-1, keepdims=True) * (1.0 / QK_HEAD_DIM) + EPS)
    return v * r * g


def _head_norm_bwd(v, g, dy):
    r = lax.rsqrt(jnp.sum(v * v, axis=-1, keepdims=True) * (1.0 / QK_HEAD_DIM) + EPS)
    vh = v * r
    dvh = dy * g
    dv = r * (dvh - vh * (jnp.sum(dvh * vh, axis=-1, keepdims=True) * (1.0 / QK_HEAD_DIM)))
    return dv, jnp.sum(dy * vh, axis=0, keepdims=True)


def _rope(v, cos, sin_lo, sin_hi):
    return v * cos + pltpu.roll(v, HEAD_PAD - 16, 1) * sin_lo + pltpu.roll(v, 16, 1) * sin_hi


def _rope_bwd(g, cos, sin_lo, sin_hi):
    return g * cos + pltpu.roll(g * sin_lo, 16, 1) + pltpu.roll(g * sin_hi, HEAD_PAD - 16, 1)


def _mla_prep_fwd(zsm, wuq, wukv, gq, gkv, gqn, gkn, rope, n_b, seq):
    n_rows = n_b * seq
    tm = min(PREP_TILE, seq)
    per_seq = seq // tm
    att_tile = min(ATT_TILE, seq)
    k_cols = N_HEADS * HEAD_PAD

    def body(z_ref, wuq_ref, wukv_ref, gq_ref, gkv_ref, gqn_ref, gkn_ref, c_ref, s1_ref, s2_ref,
             q_ref, k_ref, v_ref, kt_ref):
        z = z_ref[...]
        qn = _rms(z[:, :Q_LORA], gq_ref[...]).astype(BF16)
        kvn = _rms(z[:, Q_LORA:Q_LORA + KV_LORA], gkv_ref[...]).astype(BF16)
        krp = z[:, Q_LORA + KV_LORA:]
        cos, s1, s2 = c_ref[...], s1_ref[...], s2_ref[...]
        q_all = jnp.dot(qn, wuq_ref[...], preferred_element_type=F32)
        kv_all = jnp.dot(kvn, wukv_ref[...], preferred_element_type=F32)
        for h in range(N_HEADS):
            cols = slice(h * HEAD_PAD, (h + 1) * HEAD_PAD)
            q_ref[0, h] = (_rope(_head_norm(q_all[:, cols], gqn_ref[...]), cos, s1, s2) * QK_SCALE).astype(BF16)
            kh = _rope(_head_norm(kv_all[:, cols] + krp, gkn_ref[...]), cos, s1, s2)
            k_ref[0, h] = kh.astype(BF16)
            for part in range(tm // att_tile):
                kt_ref[0, h, part] = kh[part * att_tile:(part + 1) * att_tile].T.astype(BF16)
            v_ref[0, h] = kv_all[:, k_cols + h * HEAD_PAD:k_cols + (h + 1) * HEAD_PAD].astype(BF16)

    whole2 = lambda arr: pl.BlockSpec(arr.shape, lambda i: (0, 0))
    rope_spec = pl.BlockSpec((tm, HEAD_PAD), lambda i: (i % per_seq, 0))
    head_spec = pl.BlockSpec((1, N_HEADS, tm, HEAD_PAD), lambda i: (i // per_seq, 0, i % per_seq, 0))
    head_shape = jax.ShapeDtypeStruct((n_b, N_HEADS, seq, HEAD_PAD), BF16)
    t_spec = pl.BlockSpec((1, N_HEADS, tm // att_tile, HEAD_PAD, att_tile), lambda i: (i // per_seq, 0, i % per_seq, 0, 0))
    t_shape = jax.ShapeDtypeStruct((n_b, N_HEADS, seq // att_tile, HEAD_PAD, att_tile), BF16)
    return pl.pallas_call(
        body, name="mla_prep_fwd", grid=(n_rows // tm,),
        in_specs=[pl.BlockSpec((tm, 512), lambda i: (i, 0)), whole2(wuq), whole2(wukv),
                  whole2(gq), whole2(gkv), whole2(gqn), whole2(gkn), rope_spec, rope_spec, rope_spec],
        out_specs=[head_spec] * 3 + [t_spec], out_shape=[head_shape] * 3 + [t_shape],
        compiler_params=_params(("parallel",)),
    )(zsm, wuq, wukv, gq, gkv, gqn, gkn, *rope)


def _mla_prep_bwd(zsm, dq, dk, dv, wuq, wukv, gq, gkv, gqn, gkn, rope, n_b, seq):
    n_rows = n_b * seq
    tm = min(PREP_TILE, seq)
    per_seq = seq // tm
    tn_dims = _DIMS["tn"]
    nt_dims = _DIMS["nt"]
    k_cols = N_HEADS * HEAD_PAD

    def body(z_ref, dq_ref, dk_ref, dv_ref, wuq_ref, wukv_ref, gq_ref, gkv_ref, gqn_ref, gkn_ref,
             c_ref, s1_ref, s2_ref, dz_ref, dwuq_ref, dwukv_ref, dgq_ref, dgkv_ref, dgqn_ref, dgkn_ref):
        @pl.when(pl.program_id(0) == 0)
        def _():
            for r in (dwuq_ref, dwukv_ref, dgq_ref, dgkv_ref, dgqn_ref, dgkn_ref):
                r[...] = jnp.zeros_like(r)

        z = z_ref[...]
        zq, zkv, krp = z[:, :Q_LORA], z[:, Q_LORA:Q_LORA + KV_LORA], z[:, Q_LORA + KV_LORA:]
        qn = _rms(zq, gq_ref[...]).astype(BF16)
        kvn = _rms(zkv, gkv_ref[...]).astype(BF16)
        cos, s1, s2 = c_ref[...], s1_ref[...], s2_ref[...]
        lane = lax.broadcasted_iota(jnp.int32, (tm, HEAD_PAD), 1)
        rope_lanes = (lane >= QK_NOPE_DIM) & (lane < QK_HEAD_DIM)
        q_all = jnp.dot(qn, wuq_ref[...], preferred_element_type=F32)
        k_all = jnp.dot(kvn, wukv_ref[:, :k_cols], preferred_element_type=F32)
        dkrp = jnp.zeros((tm, HEAD_PAD), F32)
        dgqn = jnp.zeros((1, HEAD_PAD), F32)
        dgkn = jnp.zeros((1, HEAD_PAD), F32)
        dq_heads, dk_heads = [], []
        for h in range(N_HEADS):
            cols = slice(h * HEAD_PAD, (h + 1) * HEAD_PAD)
            dqh, dg = _head_norm_bwd(q_all[:, cols], gqn_ref[...],
                                     _rope_bwd(dq_ref[0, h].astype(F32) * ATT_SCALE, cos, s1, s2))
            dgqn += dg
            dq_heads.append(dqh.astype(BF16))
            dkh, dg = _head_norm_bwd(k_all[:, cols] + krp, gkn_ref[...], _rope_bwd(dk_ref[0, h].astype(F32), cos, s1, s2))
            dgkn += dg
            dkrp += jnp.where(rope_lanes, dkh, 0.0)
            dk_heads.append(dkh.astype(BF16))
        dq_all = jnp.concatenate(dq_heads, axis=1)
        dkv_all = jnp.concatenate(dk_heads + [dv_ref[0, h] for h in range(N_HEADS)], axis=1)
        dwuq_ref[...] += lax.dot_general(qn, dq_all, tn_dims, preferred_element_type=F32)
        dqn = lax.dot_general(dq_all, wuq_ref[...], nt_dims, preferred_element_type=F32)
        dwukv_ref[...] += lax.dot_general(kvn, dkv_all, tn_dims, preferred_element_type=F32)
        dkvn = lax.dot_general(dkv_all, wukv_ref[...], nt_dims, preferred_element_type=F32)
        dzq, dg = _rms_bwd(zq, gq_ref[...], dqn)
        dgq_ref[...] += dg
        dzkv, dg = _rms_bwd(zkv, gkv_ref[...], dkvn)
        dgkv_ref[...] += dg
        dgqn_ref[...] += dgqn
        dgkn_ref[...] += dgkn
        dz_ref[:, :Q_LORA] = dzq.astype(dz_ref.dtype)
        dz_ref[:, Q_LORA:Q_LORA + KV_LORA] = dzkv.astype(dz_ref.dtype)
        dz_ref[:, Q_LORA + KV_LORA:] = dkrp.astype(dz_ref.dtype)

    whole2 = lambda arr: pl.BlockSpec(arr.shape, lambda i: (0, 0))
    rope_spec = pl.BlockSpec((tm, HEAD_PAD), lambda i: (i % per_seq, 0))
    head_spec = pl.BlockSpec((1, N_HEADS, tm, HEAD_PAD), lambda i: (i // per_seq, 0, i % per_seq, 0))
    row_spec = pl.BlockSpec((tm, 512), lambda i: (i, 0))
    return pl.pallas_call(
        body, name="mla_prep_bwd", grid=(n_rows // tm,),
        in_specs=[row_spec, head_spec, head_spec, head_spec, whole2(wuq), whole2(wukv),
                  whole2(gq), whole2(gkv), whole2(gqn), whole2(gkn), rope_spec, rope_spec, rope_spec],
        out_specs=[row_spec, whole2(wuq), whole2(wukv), whole2(gq), whole2(gkv), whole2(gqn), whole2(gkn)],
        out_shape=[jax.ShapeDtypeStruct((n_rows, 512), BF16),
                   jax.ShapeDtypeStruct(wuq.shape, F32), jax.ShapeDtypeStruct(wukv.shape, F32),
                   jax.ShapeDtypeStruct(gq.shape, F32), jax.ShapeDtypeStruct(gkv.shape, F32),
                   jax.ShapeDtypeStruct(gqn.shape, F32), jax.ShapeDtypeStruct(gkn.shape, F32)],
        compiler_params=_params(("arbitrary",)),
    )(zsm, dq, dk, dv, wuq, wukv, gq, gkv, gqn, gkn, *rope)


HBM_SPEC = pl.BlockSpec(memory_space=pltpu.HBM)


def _xchg_out_shapes(bufs):
    return [jax.ShapeDtypeStruct((N_DEV,) + (a.shape if gather else a.shape[1:]), a.dtype) for a, gather in bufs]


def _xchg_scratch(n_buf):
    return [pltpu.SemaphoreType.DMA((n_buf * (N_DEV - 1),)), pltpu.SemaphoreType.DMA((n_buf * (N_DEV - 1),)),
            pltpu.SemaphoreType.DMA((n_buf,))]


def _xchg_copies(src_refs, dst_refs, gathers, send_sems, recv_sems, local_sems):
    x, y, c = lax.axis_index("x"), lax.axis_index("y"), lax.axis_index("c")
    me = 4 * x + 2 * y + c
    local, starts, arrivals = [], [], []
    for bi, (src, dst, gather) in enumerate(zip(src_refs, dst_refs, gathers)):
        local.append(pltpu.make_async_copy(src if gather else src.at[me], dst.at[me], local_sems.at[bi]))
        for kk in range(1, N_DEV):
            px = 1 - x if kk & 4 else x
            py = 1 - y if kk & 2 else y
            pc = 1 - c if kk & 1 else c
            pid = 4 * px + 2 * py + pc
            sem = bi * (N_DEV - 1) + kk - 1
            starts.append(pltpu.make_async_remote_copy(
                src_ref=src if gather else src.at[pid], dst_ref=dst.at[me],
                send_sem=send_sems.at[sem], recv_sem=recv_sems.at[sem],
                device_id=(px, py, pc), device_id_type=pl.DeviceIdType.MESH))
            arrivals.append(pltpu.make_async_remote_copy(
                src_ref=src if gather else src.at[me], dst_ref=dst.at[pid],
                send_sem=send_sems.at[sem], recv_sem=recv_sems.at[sem],
                device_id=(px, py, pc), device_id_type=pl.DeviceIdType.MESH))
    return local, starts, arrivals


def _xchg_start(copies):
    local, sends, _ = copies
    for cp in local + sends:
        cp.start()


def _xchg_finish(copies):
    local, sends, arrivals = copies
    for cp in arrivals:
        cp.wait_recv()
    for cp in sends:
        cp.wait_send()
    for cp in local:
        cp.wait()


def _gather_by_chip(src_refs, dst_refs, send_sems, recv_sems, local_sems):
    x, y, c = lax.axis_index("x"), lax.axis_index("y"), lax.axis_index("c")
    me = 4 * x + 2 * y + c
    sibling = (x, y, 1 - c)

    def place(kk):
        px = 1 - x if kk & 4 else x
        py = 1 - y if kk & 2 else y
        pc = 1 - c if kk & 1 else c
        return (px, py, pc), 4 * px + 2 * py + pc

    def copy(bi, kk, src, dst, to):
        sem = bi * (N_DEV - 1) + kk - 1
        return pltpu.make_async_remote_copy(src_ref=src, dst_ref=dst, send_sem=send_sems.at[sem],
                                            recv_sem=recv_sems.at[sem], device_id=to, device_id_type=pl.DeviceIdType.MESH)

    local, sends = [], []
    for bi, (src, dst) in enumerate(zip(src_refs, dst_refs)):
        local.append(pltpu.make_async_copy(src, dst.at[me], local_sems.at[bi]))
        sends += [copy(bi, kk, src, dst.at[me], place(kk)[0]) for kk in (1, 2, 4, 6)]
    for cp in local + sends:
        cp.start()
    for kk in (2, 4, 6):
        for bi, (src, dst) in enumerate(zip(src_refs, dst_refs)):
            dev, pid = place(kk)
            copy(bi, kk, src, dst.at[pid], dev).wait_recv()
            passed = copy(bi, kk | 1, dst.at[pid], dst.at[pid], sibling)
            passed.start()
            sends.append(passed)
    for kk in (1, 3, 5, 7):
        for bi, (src, dst) in enumerate(zip(src_refs, dst_refs)):
            dev, pid = place(kk)
            copy(bi, kk, src, dst.at[pid], sibling).wait_recv()
    for cp in sends:
        cp.wait_send()
    for cp in local:
        cp.wait()


def _exchange(name, bufs, by_chip=False):
    n_buf = len(bufs)
    gathers = [g for _, g in bufs]
    assert not by_chip or all(gathers)

    def body(*refs):
        srcs, dsts = refs[:n_buf], refs[n_buf:2 * n_buf]
        if by_chip:
            _gather_by_chip(srcs, dsts, *refs[2 * n_buf:])
            return
        copies = _xchg_copies(srcs, dsts, gathers, *refs[2 * n_buf:])
        _xchg_start(copies)
        _xchg_finish(copies)

    return pl.pallas_call(
        body, name=name, out_shape=_xchg_out_shapes(bufs),
        in_specs=[HBM_SPEC] * n_buf, out_specs=[HBM_SPEC] * n_buf, scratch_shapes=_xchg_scratch(n_buf),
    )(*[a for a, _ in bufs])


def _chunk_mask(t, keys_first):
    key = lax.broadcasted_iota(jnp.int32, (t, t), 0 if keys_first else 1) // CHUNK
    query = lax.broadcasted_iota(jnp.int32, (t, t), 1 if keys_first else 0) // CHUNK
    return query >= key


def _grid_ends(grid):
    ids = [pl.program_id(ax) for ax in range(len(grid))]
    first = functools.reduce(jnp.logical_and, [i == 0 for i in ids])
    last = functools.reduce(jnp.logical_and, [i == g - 1 for i, g in zip(ids, grid)])
    return first, last


def _attn_fwd(q, k, v, bufs, n_b, seq):
    tq = min(ATT_TILE, seq)
    nq = seq // tq
    nt_dims = _DIMS["nt"]
    hpb = ATT_HEADS
    grid = (n_b, N_HEADS // hpb, nq)
    n_buf = len(bufs)
    gathers = [g for _, g in bufs]
    sum_lane = [HEAD_PAD - 1 if hh % 2 == 0 else 0 for hh in range(hpb)]

    def body(q_ref, k_ref, v_ref, *rest):
        srcs, (o_ref, lse_ref), dsts = rest[:n_buf], rest[n_buf:n_buf + 2], rest[n_buf + 2:2 * n_buf + 2]
        s_ref = rest[2 * n_buf + 2]
        copies = _xchg_copies(srcs, dsts, gathers, *rest[2 * n_buf + 3:])
        first, last = _grid_ends(grid)
        pl.when(first)(functools.partial(_xchg_start, copies))

        qi = pl.program_id(2)
        mask = _chunk_mask(tq, keys_first=False)
        lane_row = lax.broadcasted_iota(jnp.int32, (1, HEAD_PAD), 1)
        ones = [(lane_row == sum_lane[hh]).astype(BF16) for hh in range(hpb)]
        qs = [q_ref[0, hh] for hh in range(hpb)]

        def score_step(j, tops, masked):
            rows = pl.ds(pl.multiple_of(j * tq, tq), tq)
            out = []
            for hh in range(hpb):
                s = lax.dot_general(qs[hh], k_ref[0, hh, rows, :], nt_dims, preferred_element_type=F32)
                if masked:
                    s = jnp.where(mask, s, NEG_BIG)
                s_ref[hh, j] = s
                out.append(jnp.maximum(tops[hh], s))
            return tuple(out)

        tops = tuple(jnp.full((tq, tq), NEG_BIG, F32) for _ in range(hpb))
        tops = lax.fori_loop(0, qi, functools.partial(score_step, masked=False), tops)
        tops = score_step(qi, tops, True)
        ms = [jnp.max(top, axis=-1, keepdims=True) for top in tops]

        def value_step(j, accs):
            rows = pl.ds(pl.multiple_of(j * tq, tq), tq)
            out = []
            for hh in range(hpb):
                p = jnp.exp2(s_ref[hh, j] - ms[hh]).astype(BF16)
                out.append(accs[hh] + jnp.dot(p, v_ref[0, hh, rows, :] + ones[hh], preferred_element_type=F32))
            return tuple(out)

        accs = tuple(jnp.zeros((tq, HEAD_PAD), F32) for _ in range(hpb))
        accs = lax.fori_loop(0, qi + 1, value_step, accs)
        carry = list(zip(ms, accs))
        lane = lax.broadcasted_iota(jnp.int32, (tq, HEAD_PAD), 1)
        for pair in range(hpb // 2):
            outs = []
            for hh in (2 * pair, 2 * pair + 1):
                m, acc = carry[hh]
                l = jnp.sum(jnp.where(lane == sum_lane[hh], acc, 0.0), axis=-1, keepdims=True)
                outs.append(acc * (1.0 / l))
                lse_ref[0, hh] = jnp.broadcast_to(m + jnp.log2(l), (tq, HEAD_PAD)).T[0:8, :]
            o_ref[0, :, pair * HEAD_PAD:(pair + 1) * HEAD_PAD] = jnp.where(lane < V_HEAD_DIM, outs[0], outs[1]).astype(BF16)

        pl.when(last)(functools.partial(_xchg_finish, copies))

    kv_spec = pl.BlockSpec((1, hpb, seq, HEAD_PAD), lambda b, hb, i: (b, hb, 0, 0))
    q_spec = pl.BlockSpec((1, hpb, tq, HEAD_PAD), lambda b, hb, i: (b, hb, i, 0))
    res = pl.pallas_call(
        body, name="attn_fwd", grid=grid,
        in_specs=[q_spec, kv_spec, kv_spec] + [HBM_SPEC] * n_buf,
        out_specs=[pl.BlockSpec((1, tq, hpb * V_HEAD_DIM), lambda b, hb, i: (b, i, hb)),
                   pl.BlockSpec((1, hpb, 8, tq), lambda b, hb, i: (b, hb, 0, i))] + [HBM_SPEC] * n_buf,
        out_shape=[jax.ShapeDtypeStruct((n_b, seq, N_HEADS * V_HEAD_DIM), BF16),
                   jax.ShapeDtypeStruct((n_b, N_HEADS, 8, seq), F32)] + _xchg_out_shapes(bufs),
        scratch_shapes=[pltpu.VMEM((hpb, nq, tq, tq), F32)] + _xchg_scratch(n_buf),
        compiler_params=_params(("arbitrary", "arbitrary", "arbitrary")),
    )(q, k, v, *[a for a, _ in bufs])
    return res[0], res[1], res[2:]


def _attn_bwd(q, k, v, kt, do, o, lse, bufs, n_b, seq):
    tq = min(ATT_TILE, seq)
    nq = seq // tq
    nt_dims = _DIMS["nt"]
    hpb = ATT_HEADS
    grid = (n_b, N_HEADS // hpb, nq)
    n_buf = len(bufs)
    gathers = [g for _, g in bufs]

    def body(q_ref, k_ref, v_ref, kt_ref, do_ref, o_ref, lse_ref, *rest):
        srcs, (dq_ref, dk_ref, dv_ref), dsts = rest[:n_buf], rest[n_buf:n_buf + 3], rest[n_buf + 3:2 * n_buf + 3]
        dk_acc, dv_acc = rest[2 * n_buf + 3:2 * n_buf + 5]
        copies = _xchg_copies(srcs, dsts, gathers, *rest[2 * n_buf + 5:])
        first, last = _grid_ends(grid)
        pl.when(first)(functools.partial(_xchg_start, copies))

        qi = pl.program_id(2)

        @pl.when(qi == 0)
        def _():
            dk_acc[...] = jnp.zeros_like(dk_acc)
            dv_acc[...] = jnp.zeros_like(dv_acc)

        mask = _chunk_mask(tq, keys_first=True)
        lane = lax.broadcasted_iota(jnp.int32, (tq, HEAD_PAD), 1)
        qs, dos, deltas, lses = [], [], [], []
        for hh in range(hpb):
            cols = slice((hh // 2) * HEAD_PAD, (hh // 2 + 1) * HEAD_PAD)
            do_pair = do_ref[0, :, cols]
            prod = do_pair.astype(F32) * o_ref[0, :, cols].astype(F32)
            delta = jnp.sum(jnp.where(lane // V_HEAD_DIM == hh % 2, prod, 0.0), axis=-1, keepdims=True)
            qs.append(q_ref[0, hh])
            dos.append(do_pair)
            deltas.append(jnp.broadcast_to(delta, (tq, HEAD_PAD)).T[0:1, :])
            lses.append(lse_ref[0, hh][0:1, :])

        def step(j, dqs, masked):
            rows = pl.ds(pl.multiple_of(j * tq, tq), tq)
            out = []
            for hh in range(hpb):
                s = lax.dot_general(k_ref[0, hh, rows, :], qs[hh], nt_dims, preferred_element_type=F32)
                p = jnp.exp2(s - lses[hh])
                if masked:
                    p = jnp.where(mask, p, 0.0)
                dv_acc[hh, rows, :] += jnp.dot(p.astype(BF16), dos[hh], preferred_element_type=F32)
                dp = lax.dot_general(v_ref[0, hh, rows, :], dos[hh], nt_dims, preferred_element_type=F32)
                ds = (p * (dp - deltas[hh])).astype(BF16)
                dk_acc[hh, rows, :] += jnp.dot(ds, qs[hh], preferred_element_type=F32)
                out.append(dqs[hh] + jnp.dot(kt_ref[0, hh, j], ds, preferred_element_type=F32))
            return tuple(out)

        dqs = tuple(jnp.zeros((HEAD_PAD, tq), F32) for _ in range(hpb))
        dqs = lax.fori_loop(0, qi, functools.partial(step, masked=False), dqs)
        dqs = step(qi, dqs, True)
        for hh in range(hpb):
            dq_ref[0, hh] = dqs[hh].T.astype(BF16)

        @pl.when(qi == nq - 1)
        def _():
            dk_ref[0] = (dk_acc[...] * LN2).astype(BF16)
            dv_ref[0] = dv_acc[...].astype(BF16)

        pl.when(last)(functools.partial(_xchg_finish, copies))

    full_spec = pl.BlockSpec((1, hpb, seq, HEAD_PAD), lambda b, hb, i: (b, hb, 0, 0))
    t_spec = pl.BlockSpec((1, hpb, nq, HEAD_PAD, tq), lambda b, hb, i: (b, hb, 0, 0, 0))
    q_spec = pl.BlockSpec((1, hpb, tq, HEAD_PAD), lambda b, hb, i: (b, hb, i, 0))
    o_spec = pl.BlockSpec((1, tq, hpb * V_HEAD_DIM), lambda b, hb, i: (b, i, hb))
    lse_spec = pl.BlockSpec((1, hpb, 8, tq), lambda b, hb, i: (b, hb, 0, i))
    head_shape = jax.ShapeDtypeStruct((n_b, N_HEADS, seq, HEAD_PAD), BF16)
    res = pl.pallas_call(
        body, name="attn_bwd", grid=grid,
        in_specs=[q_spec, full_spec, full_spec, t_spec, o_spec, o_spec, lse_spec] + [HBM_SPEC] * n_buf,
        out_specs=[q_spec, full_spec, full_spec] + [HBM_SPEC] * n_buf,
        out_shape=[head_shape] * 3 + _xchg_out_shapes(bufs),
        scratch_shapes=[pltpu.VMEM((hpb, seq, HEAD_PAD), F32), pltpu.VMEM((hpb, seq, HEAD_PAD), F32)]
        + _xchg_scratch(n_buf),
        compiler_params=_params(("arbitrary", "arbitrary", "arbitrary")),
    )(q, k, v, kt, do, o, lse, *[a for a, _ in bufs])
    return res[0], res[1], res[2], res[3:]


def _in_proj_bwd(parts, x2, dx1, scale, g, bufs, seq):
    n_rows, d = x2.shape
    tm = min(512, seq)
    per_seq = seq // tm
    grid = (n_rows // tm,)
    n_part, n_buf = len(parts), len(bufs)
    gathers = [gt for _, gt in bufs]
    nt_dims = _DIMS["nt"]

    def body(*refs):
        dz_refs, w_refs = refs[:n_part], refs[n_part:2 * n_part]
        x_ref, dx1_ref, sc_ref, g_ref = refs[2 * n_part:2 * n_part + 4]
        srcs = refs[2 * n_part + 4:2 * n_part + 4 + n_buf]
        gx_ref, dsc_ref, dsh_ref, dg_ref = refs[2 * n_part + 4 + n_buf:2 * n_part + 8 + n_buf]
        dsts = refs[2 * n_part + 8 + n_buf:2 * n_part + 8 + 2 * n_buf]
        copies = _xchg_copies(srcs, dsts, gathers, *refs[2 * n_part + 8 + 2 * n_buf:])
        first, last = _grid_ends(grid)
        pl.when(first)(functools.partial(_xchg_start, copies))

        i = pl.program_id(0)
        dh = None
        for dz_ref, w_ref in zip(dz_refs, w_refs):
            term = lax.dot_general(dz_ref[...], w_ref[...], nt_dims, preferred_element_type=F32)
            dh = term if dh is None else dh + term
        dx, dsc, dsh, dg = _norm_mod_bwd(x_ref[...], g_ref[...], sc_ref[0], dh)
        gx_ref[...] = dx1_ref[...] + dx

        @pl.when(i % per_seq == 0)
        def _():
            dsc_ref[...] = jnp.zeros_like(dsc_ref)
            dsh_ref[...] = jnp.zeros_like(dsh_ref)

        @pl.when(i == 0)
        def _():
            dg_ref[...] = jnp.zeros_like(dg_ref)

        dsc_ref[0] += dsc
        dsh_ref[0] += dsh
        dg_ref[...] += dg
        pl.when(last)(functools.partial(_xchg_finish, copies))

    row = lambda width: pl.BlockSpec((tm, width), lambda i: (i, 0))
    bat = pl.BlockSpec((1, 1, d), lambda i: (i // per_seq, 0, 0))
    whole = lambda arr: pl.BlockSpec(arr.shape, lambda i: (0, 0))
    n_b = n_rows // seq
    res = pl.pallas_call(
        body, name="in_proj_bwd", grid=grid,
        in_specs=[row(dz.shape[1]) for dz, _ in parts] + [whole(w) for _, w in parts]
        + [row(d), row(d), bat, whole(g)] + [HBM_SPEC] * n_buf,
        out_specs=[row(d), bat, bat, whole(g)] + [HBM_SPEC] * n_buf,
        out_shape=[jax.ShapeDtypeStruct((n_rows, d), F32), jax.ShapeDtypeStruct((n_b, 1, d), F32),
                   jax.ShapeDtypeStruct((n_b, 1, d), F32), jax.ShapeDtypeStruct(g.shape, F32)] + _xchg_out_shapes(bufs),
        scratch_shapes=_xchg_scratch(n_buf),
        compiler_params=_params(("arbitrary",)),
    )(*[dz for dz, _ in parts], *[w for _, w in parts], x2, dx1, scale, g, *[a for a, _ in bufs])
    return res[0], res[1], res[2], res[3], res[4:]


def _ln_silu(u1, g, b):
    mu = jnp.mean(u1, axis=-1, keepdims=True)
    uc = u1 - mu
    r = lax.rsqrt(jnp.mean(uc * uc, axis=-1, keepdims=True) + EPS)
    y = uc * r * g + b
    return y * _sigmoid(y)


def _conv_fill_glu(z_ref, u0_ref, seq, tile):
    u0_ref[0:CONV_HALO, :] = jnp.zeros((CONV_HALO, CONV_CH), F32)
    u0_ref[CONV_HALO + seq:CONV_HALO + seq + CONV_TAIL, :] = jnp.zeros((CONV_TAIL, CONV_CH), F32)
    for t in range(seq // tile):
        zt = z_ref[0, t * tile:(t + 1) * tile, :].astype(F32)
        u0_ref[CONV_HALO + t * tile:CONV_HALO + (t + 1) * tile, :] = zt[:, :CONV_CH] * _sigmoid(zt[:, CONV_CH:])


def _conv_windows(ref, views_ref, t, tile):
    for b in range(8):
        views_ref[b] = ref[t * tile + b:t * tile + b + tile + CONV_HALO, :]


def _conv_tap(views_ref, offset, tile):
    return views_ref[offset % 8, 8 * (offset // 8):8 * (offset // 8) + tile, :]


def _conv_tile(u0_ref, views_ref, w_ref, b_ref, t, tile):
    _conv_windows(u0_ref, views_ref, t, tile)
    acc = jnp.broadcast_to(b_ref[...], (tile, CONV_CH))
    for kk in range(CONV_WIDTH):
        acc = acc + w_ref[kk:kk + 1, :] * _conv_tap(views_ref, kk + CONV_HALO - (CONV_WIDTH - 1), tile)
    return acc


def _conv_fwd(zglu, conv_w, conv_b, ln_g, ln_b, n_b, seq):
    tile = min(256, seq)

    def body(z_ref, w_ref, b_ref, g_ref, bb_ref, o_ref, u1_ref, u0_ref, views_ref):
        _conv_fill_glu(z_ref, u0_ref, seq, tile)
        for t in range(seq // tile):
            u1 = _conv_tile(u0_ref, views_ref, w_ref, b_ref, t, tile)
            u1_ref[0, t * tile:(t + 1) * tile, :] = u1
            o_ref[0, t * tile:(t + 1) * tile, :] = _ln_silu(u1, g_ref[...], bb_ref[...]).astype(BF16)

    whole2 = lambda arr: pl.BlockSpec(arr.shape, lambda b: (0, 0))
    seq_spec = pl.BlockSpec((1, seq, CONV_CH), lambda b: (b, 0, 0))
    return pl.pallas_call(
        body, name="conv_fwd", grid=(n_b,),
        in_specs=[pl.BlockSpec((1, seq, 2 * CONV_CH), lambda b: (b, 0, 0)), whole2(conv_w), whole2(conv_b),
                  whole2(ln_g), whole2(ln_b)],
        out_specs=[seq_spec, seq_spec],
        out_shape=[jax.ShapeDtypeStruct((n_b, seq, CONV_CH), BF16), jax.ShapeDtypeStruct((n_b, seq, CONV_CH), F32)],
        scratch_shapes=[pltpu.VMEM((seq + CONV_HALO + CONV_TAIL, CONV_CH), F32),
                        pltpu.VMEM((8, tile + CONV_HALO, CONV_CH), F32)],
        compiler_params=_params(("parallel",)),
    )(zglu, conv_w, conv_b, ln_g, ln_b)


def _conv_bwd(zglu, u1_saved, du3, conv_w, ln_g, ln_b, n_b, seq):
    tile = min(256, seq)
    n_t = seq // tile

    def body(z_ref, u1_ref, du3_ref, w_ref, g_ref, bb_ref, dz_ref, dw_ref, db_ref, dg_ref, dbb_ref, u0_ref, du1_ref,
             u0_views, du1_views):
        @pl.when(pl.program_id(0) == 0)
        def _():
            for r in (dw_ref, db_ref, dg_ref, dbb_ref):
                r[...] = jnp.zeros_like(r)

        _conv_fill_glu(z_ref, u0_ref, seq, tile)
        du1_ref[seq:seq + CONV_HALO + CONV_TAIL, :] = jnp.zeros((CONV_HALO + CONV_TAIL, CONV_CH), F32)
        g = g_ref[...]
        for t in range(n_t):
            u1 = u1_ref[0, t * tile:(t + 1) * tile, :]
            mu = jnp.mean(u1, axis=-1, keepdims=True)
            uc = u1 - mu
            r = lax.rsqrt(jnp.mean(uc * uc, axis=-1, keepdims=True) + EPS)
            xh = uc * r
            y = xh * g + bb_ref[...]
            sg = _sigmoid(y)
            dy = du3_ref[0, t * tile:(t + 1) * tile, :].astype(F32) * (sg * (1.0 + y * (1.0 - sg)))
            dg_ref[...] += jnp.sum(dy * xh, axis=0, keepdims=True)
            dbb_ref[...] += jnp.sum(dy, axis=0, keepdims=True)
            dxh = dy * g
            du1 = r * (dxh - jnp.mean(dxh, axis=-1, keepdims=True) - xh * jnp.mean(dxh * xh, axis=-1, keepdims=True))
            db_ref[...] += jnp.sum(du1, axis=0, keepdims=True)
            du1_ref[t * tile:(t + 1) * tile, :] = du1
        for t in range(n_t):
            du1 = du1_ref[t * tile:(t + 1) * tile, :]
            du0 = jnp.zeros((tile, CONV_CH), F32)
            _conv_windows(u0_ref, u0_views, t, tile)
            _conv_windows(du1_ref, du1_views, t, tile)
            for kk in range(CONV_WIDTH):
                du0 = du0 + w_ref[kk:kk + 1, :] * _conv_tap(du1_views, CONV_WIDTH - 1 - kk, tile)
                u0_tap = _conv_tap(u0_views, kk + CONV_HALO - (CONV_WIDTH - 1), tile)
                dw_ref[kk:kk + 1, :] += jnp.sum(du1 * u0_tap, axis=0, keepdims=True)
            zt = z_ref[0, t * tile:(t + 1) * tile, :].astype(F32)
            ga, sb = zt[:, :CONV_CH], _sigmoid(zt[:, CONV_CH:])
            dz_ref[0, t * tile:(t + 1) * tile, :CONV_CH] = (du0 * sb).astype(BF16)
            dz_ref[0, t * tile:(t + 1) * tile, CONV_CH:] = (du0 * ga * sb * (1.0 - sb)).astype(BF16)

    whole2 = lambda arr: pl.BlockSpec(arr.shape, lambda b: (0, 0))
    z_spec = pl.BlockSpec((1, seq, 2 * CONV_CH), lambda b: (b, 0, 0))
    seq_spec = pl.BlockSpec((1, seq, CONV_CH), lambda b: (b, 0, 0))
    return pl.pallas_call(
        body, name="conv_bwd", grid=(n_b,),
        in_specs=[z_spec, seq_spec, seq_spec, whole2(conv_w), whole2(ln_g), whole2(ln_b)],
        out_specs=[z_spec, whole2(conv_w), whole2(ln_g), whole2(ln_g), whole2(ln_b)],
        out_shape=[jax.ShapeDtypeStruct((n_b, seq, 2 * CONV_CH), BF16), jax.ShapeDtypeStruct(conv_w.shape, F32),
                   jax.ShapeDtypeStruct(ln_g.shape, F32), jax.ShapeDtypeStruct(ln_g.shape, F32),
                   jax.ShapeDtypeStruct(ln_b.shape, F32)],
        scratch_shapes=[pltpu.VMEM((seq + CONV_HALO + CONV_TAIL, CONV_CH), F32)] * 2
        + [pltpu.VMEM((8, tile + CONV_HALO, CONV_CH), F32)] * 2,
        compiler_params=_params(("arbitrary",)),
    )(zglu, u1_saved, du3, conv_w, ln_g, ln_b)


def _adamw(name, w, parts, m, v):
    n_parts = parts.shape[0]

    def body(w_ref, p_ref, m_ref, v_ref, g_ref, d_ref, nm_ref, nv_ref):
        gg = p_ref[0].astype(F32)
        for j in range(1, n_parts):
            gg = gg + p_ref[j].astype(F32)
        nm = ADAM_B1 * m_ref[...] + (1.0 - ADAM_B1) * gg
        nv = ADAM_B2 * v_ref[...] + (1.0 - ADAM_B2) * jnp.square(gg)
        m_hat = nm / (1.0 - ADAM_B1 ** ADAM_STEP)
        v_hat = nv / (1.0 - ADAM_B2 ** ADAM_STEP)
        g_ref[...] = gg
        d_ref[...] = -ADAM_LR * (m_hat / (jnp.sqrt(v_hat) + ADAM_EPS) + ADAM_WD * w_ref[...])
        nm_ref[...] = nm
        nv_ref[...] = nv

    shape = jax.ShapeDtypeStruct(w.shape, F32)
    return pl.pallas_call(body, name=name, out_shape=[shape] * 4, compiler_params=_params(None))(w, parts, m, v)


def _rope_tables(seq):
    inv_freq = ROPE_THETA ** (-jnp.arange(0, QK_ROPE_DIM, 2, dtype=F32) / QK_ROPE_DIM)
    ang = jnp.arange(seq, dtype=F32)[:, None] * inv_freq[None, :]
    cos, sin = jnp.cos(ang), jnp.sin(ang)
    half = QK_ROPE_DIM // 2
    z = lambda n: jnp.zeros((seq, n), F32)
    tail = HEAD_PAD - QK_HEAD_DIM
    cos_t = jnp.concatenate([jnp.ones((seq, QK_NOPE_DIM), F32), cos, cos, z(tail)], axis=1)
    sin_lo = jnp.concatenate([z(QK_NOPE_DIM), -sin, z(half), z(tail)], axis=1)
    sin_hi = jnp.concatenate([z(QK_NOPE_DIM), z(half), sin, z(tail)], axis=1)
    return cos_t, sin_lo, sin_hi


def _pad_lanes(v, width=HEAD_PAD):
    return jnp.pad(v, [(0, 0)] * (v.ndim - 1) + [(0, width - v.shape[-1])])


def _unstack_cols(s):
    return s.transpose(1, 0, 2).reshape(s.shape[1], N_DEV * s.shape[2])


def _stack_cols(g, dtype):
    rows, cols = g.shape
    return g.reshape(rows, N_DEV, cols // N_DEV).transpose(1, 0, 2).astype(dtype)


def kernel(x, c, w_ada, b_ada, norm1_g, w_in, q_latent_g, w_uq, kv_latent_g, w_ukv, qk_norm_q_g, qk_norm_k_g, w_o_mla, conv_w, conv_b, conv_ln_g, conv_ln_b, w_pw_out, w_out, norm2_g, w_ff1, w_ff2, loss_target, m_w_ada, m_b_ada, m_norm1_g, m_w_in, m_q_latent_g, m_w_uq, m_kv_latent_g, m_w_ukv, m_qk_norm_q_g, m_qk_norm_k_g, m_w_o_mla, m_conv_w, m_conv_b, m_conv_ln_g, m_conv_ln_b, m_w_pw_out, m_w_out, m_norm2_g, m_w_ff1, m_w_ff2, v_w_ada, v_b_ada, v_norm1_g, v_w_in, v_q_latent_g, v_w_uq, v_kv_latent_g, v_w_ukv, v_qk_norm_q_g, v_qk_norm_k_g, v_w_o_mla, v_conv_w, v_conv_b, v_conv_ln_g, v_conv_ln_b, v_w_pw_out, v_w_out, v_norm2_g, v_w_ff1, v_w_ff2):
    given = dict(locals())
    local = {n: given[n][0] for n in WEIGHTS}
    vec = {n: local[n].reshape(1, -1) for n in REPLICATED}
    bf = lambda n: local[n].astype(BF16)
    n_b, seq, d = x.shape
    n_rows = n_b * seq
    x2 = x.reshape(n_rows, d)
    t2 = loss_target.reshape(n_rows, d)
    rw = functools.partial(_rowwise, n_rows=n_rows, seq=seq)
    me = 4 * lax.axis_index("x") + 2 * lax.axis_index("y") + lax.axis_index("c")
    ada_cols = local["w_ada"].shape[1]

    c_all, w_in_s, w_uq_s, w_ukv_s, conv_w_s = _exchange(
        "gather_early", [(c, True), (bf("w_in"), True), (bf("w_uq"), True), (bf("w_ukv"), True), (local["conv_w"], True)],
        by_chip=True)
    w_in_f = _unstack_cols(w_in_s)
    zeros = lambda n: jnp.zeros((d, n), BF16)
    w_sm = jnp.concatenate([w_in_f[:, :OFF_KV], zeros(QK_NOPE_DIM), w_in_f[:, OFF_KV:OFF_KR], zeros(HEAD_PAD - QK_HEAD_DIM)], axis=1)
    w_glu = w_in_f[:, OFF_KR:OFF_GLU]
    w_gate = w_in_f[:, OFF_GLU:]
    wuq = _pad_lanes(_unstack_cols(w_uq_s).reshape(Q_LORA, N_HEADS, QK_HEAD_DIM)).reshape(Q_LORA, N_HEADS * HEAD_PAD)
    wukv_f = _unstack_cols(w_ukv_s).reshape(KV_LORA, N_HEADS, QK_NOPE_DIM + V_HEAD_DIM)
    wv = wukv_f[:, :, QK_NOPE_DIM:]
    odd = (jnp.arange(N_HEADS) % 2 == 1)[None, :, None]
    wuv = jnp.where(odd, jnp.pad(wv, ((0, 0), (0, 0), (V_HEAD_DIM, 0))), jnp.pad(wv, ((0, 0), (0, 0), (0, V_HEAD_DIM))))
    wukv = jnp.concatenate([_pad_lanes(wukv_f[:, :, :QK_NOPE_DIM]), wuv], axis=1).reshape(KV_LORA, 2 * N_HEADS * HEAD_PAD)
    gqn = _pad_lanes(vec["qk_norm_q_g"])
    gkn = _pad_lanes(vec["qk_norm_k_g"])
    conv_w_f = jnp.pad(_unstack_cols(conv_w_s), ((0, 1), (0, 0)))
    rope = _rope_tables(seq)

    all_rows = N_DEV * n_b
    pad_rows = (-all_rows) % ROWS_PAD
    c_rows = jnp.pad(c_all.reshape(all_rows, d), ((0, pad_rows), (0, 0)))
    b_cols = lax.dynamic_slice(local["b_ada"], (me * ada_cols,), (ada_cols,))
    mod_cols = _mm("ada_fwd", c_rows, local["w_ada"], "nn", F32, a_fn=_silu, epi=lambda acc, b: acc + b,
                   epi_in=(jnp.broadcast_to(b_cols, (all_rows + pad_rows, ada_cols)),))
    (mod_s,) = _exchange("scatter_mod", [(mod_cols[:all_rows].reshape(N_DEV, n_b, ada_cols), False)])
    mod = mod_s.transpose(1, 0, 2).reshape(n_b, ADA_CHUNKS, 1, d)
    shift1, scale1, gate1, shift2, scale2, gate2 = [mod[:, i] for i in range(ADA_CHUNKS)]

    (h,) = rw("norm1_fwd", lambda r, b, cc: ([_norm_mod(r[0], cc[0], b[0], b[1])], [], []),
              rows=[_full(x2)], bats=[scale1, shift1], consts=[vec["norm1_g"]], outs=[(d, BF16)])
    zsm = _mm("in_proj_sm", h, w_sm, "nn", F32)
    zglu = _mm("in_proj_glu", h, w_glu, "nn", BF16)
    zgate = _mm("in_proj_gate", h, w_gate, "nn", BF16)
    q, k, v, kt = _mla_prep_fwd(zsm, wuq, wukv, vec["q_latent_g"], vec["kv_latent_g"], gqn, gkn, rope, n_b, seq)
    attn, lse, (w_o_s, w_pw_s, w_out_s, w_ff1_s, w_ff2_s) = _attn_fwd(
        q, k, v, [(bf("w_o_mla"), True), (bf("w_pw_out"), True), (bf("w_out"), True), (bf("w_ff1"), True),
                  (bf("w_ff2"), True)], n_b, seq)
    w_o_f = _unstack_cols(w_o_s)
    w_pw_f = _unstack_cols(w_pw_s)
    w_out_f = w_out_s.reshape(d, d)
    w_ff2_f = w_ff2_s.reshape(N_DEV * w_ff2_s.shape[1], d)
    attn2 = attn.reshape(n_rows, N_HEADS * V_HEAD_DIM)
    u3, u1 = _conv_fwd(zglu.reshape(n_b, seq, 2 * CONV_CH), conv_w_f, vec["conv_b"], vec["conv_ln_g"], vec["conv_ln_b"], n_b, seq)
    u32 = u3.reshape(n_rows, CONV_CH)
    ya = _mm("mla_out", attn2, w_o_f, "nn", BF16)
    yb = _mm("conv_out", u32, w_pw_f, "nn", BF16)
    mmr = functools.partial(_mm_rows, n_rows=n_rows, seq=seq)

    def merge_fn(t):
        return _sigmoid(t[0]) * t[2] + _sigmoid(t[1]) * t[3]

    def mid_fn(acc, r, b, cc):
        x1_ = r[0] + b[0] * acc
        return [acc, x1_, _norm_mod(x1_, cc[0], b[1], b[2])], [], []

    mrg, mixed, x1, h2 = mmr("out_proj", [(zgate, d, 0), (zgate, d, 1), (ya, d, 0), (yb, d, 0)], merge_fn, w_out_f, "nn",
                             mid_fn, rows=[_full(x2)], bats=[gate1, scale2, shift2], consts=[vec["norm2_g"]],
                             outs=[(d, BF16), (d, F32), (d, BF16)], a_out=BF16)

    a = _mm("ff1", h2, w_ff1_s, "nn", BF16, b_stacked=True)

    def loss_fn(ff, r, b, cc):
        err = r[0] + b[0] * ff - r[1]
        dy_ = err * (1.0 / d)
        sq = jnp.broadcast_to(jnp.sum(err * err, keepdims=True), (1, LANES))
        return [dy_, b[0] * dy_], [jnp.sum(dy_ * ff, axis=0, keepdims=True)], [sq]

    dy, df, dgate2, sq_err = mmr("ff2_loss", [(a, a.shape[1], 0)], lambda t: _relu2(t[0]), w_ff2_f, "nn", loss_fn,
                                 rows=[_full(x1), _full(t2)], bats=[gate2], outs=[(d, F32), (d, BF16)], bat_outs=[d],
                                 tot_outs=[(1, LANES)], tm=1024)
    loss = lax.psum(sq_err[0, 0] * (0.5 / d), MESH_AXES)

    da = _mm("ff2_bwd", df, w_ff2_f, "nt", BF16, epi=lambda acc, av: acc * 2.0 * jnp.maximum(av, 0.0), epi_in=(a,))
    g_ff2 = _mm("ff2_dw", a, df, "tn", BF16, a_fn=_relu2)
    g_ff1_s = _mm("ff1_dw", h2, da, "tn", BF16, out_stacked=True)

    def mid_bwd(dh2_, r, b, cc):
        dx, dsc, dsh, dg = _norm_mod_bwd(r[0], cc[0], b[0], dh2_)
        dx1_ = r[1] + dx
        return [dx1_, b[1] * dx1_], [dsc, dsh, jnp.sum(dx1_ * r[2].astype(F32), axis=0, keepdims=True)], [dg]

    dx1, dmixed, dscale2, dshift2, dgate1, g_norm2 = mmr(
        "ff1_bwd", [(da, da.shape[1], 0)], None, w_ff1_s, "nt", mid_bwd, rows=[_full(x1), _full(dy), _full(mixed)],
        bats=[scale2, gate1], consts=[vec["norm2_g"]], outs=[(d, F32), (d, BF16)], bat_outs=[d, d, d],
        tot_outs=[(1, d)], w_stacked=True, tm=1024)

    g_out = _mm("out_proj_dw", mrg, dmixed, "tn", BF16)

    def merge_bwd(dm, r, b, cc):
        ya_, yb_ = r[2].astype(F32), r[3].astype(F32)
        sa, sb = _sigmoid(r[0].astype(F32)), _sigmoid(r[1].astype(F32))
        return [dm * ya_ * sa * (1.0 - sa), dm * yb_ * sb * (1.0 - sb), dm * sa, dm * sb], [], []

    dzga, dzgb, dya, dyb = mmr("out_proj_bwd", [(dmixed, d, 0)], None, w_out_f, "nt", merge_bwd,
                               rows=[(zgate, d, 0), (zgate, d, 1), _full(ya), _full(yb)], outs=[(d, BF16)] * 4)
    dattn = _mm("mla_out_bwd", dya, w_o_f, "nt", BF16)
    g_o = _mm("mla_out_dw", attn2, dya, "tn", F32)
    du3 = _mm("conv_out_bwd", dyb, w_pw_f, "nt", BF16)
    g_pw = _mm("conv_out_dw", u32, dyb, "tn", F32)

    dzglu, g_conv_w, g_conv_b, g_ln_g, g_ln_b = _conv_bwd(
        zglu.reshape(n_b, seq, 2 * CONV_CH), u1, du3.reshape(n_b, seq, CONV_CH), conv_w_f, vec["conv_ln_g"],
        vec["conv_ln_b"], n_b, seq)
    dzglu = dzglu.reshape(n_rows, 2 * CONV_CH)

    dq, dk, dv, (p_ff2, p_ff1, p_out, p_pw, p_o) = _attn_bwd(
        q, k, v, kt, dattn.reshape(n_b, seq, N_HEADS * V_HEAD_DIM), attn, lse,
        [(g_ff2.reshape(N_DEV, -1, d), False), (g_ff1_s, False), (g_out.reshape(N_DEV, -1, d), False),
         (_stack_cols(g_pw, BF16), False), (_stack_cols(g_o, BF16), False)], n_b, seq)
    dzsm, g_wuq, g_wukv, g_gq, g_gkv, g_gqn, g_gkn = _mla_prep_bwd(
        zsm, dq, dk, dv, wuq, wukv, vec["q_latent_g"], vec["kv_latent_g"], gqn, gkn, rope, n_b, seq)

    g_gate_a = _mm("in_proj_gate_dw_a", h, dzga, "tn", F32)
    g_gate_b = _mm("in_proj_gate_dw_b", h, dzgb, "tn", F32)
    g_glu = _mm("in_proj_glu_dw", h, dzglu, "tn", F32)
    g_sm = _mm("in_proj_sm_dw", h, dzsm, "tn", F32)
    g_in = jnp.concatenate([g_sm[:, :OFF_KV], g_sm[:, OFF_KV + QK_NOPE_DIM:OFF_KV + QK_NOPE_DIM + QK_ROPE_DIM], g_glu,
                            g_gate_a, g_gate_b], axis=1)
    g_uq = g_wuq.reshape(Q_LORA, N_HEADS, HEAD_PAD)[:, :, :QK_HEAD_DIM].reshape(Q_LORA, N_HEADS * QK_HEAD_DIM)
    g_wukv = g_wukv.reshape(KV_LORA, 2, N_HEADS, HEAD_PAD)
    g_v = jnp.where(odd, g_wukv[:, 1, :, V_HEAD_DIM:], g_wukv[:, 1, :, :V_HEAD_DIM])
    g_ukv = jnp.concatenate([g_wukv[:, 0, :, :QK_NOPE_DIM], g_v], axis=2).reshape(KV_LORA, -1)

    grad_x, dscale1, dshift1, g_norm1, (p_in, p_uq, p_ukv, p_conv_w) = _in_proj_bwd(
        [(dzga, w_gate[:, :d]), (dzgb, w_gate[:, d:]), (dzglu, w_glu), (dzsm, w_sm)], x2, dx1, scale1, vec["norm1_g"],
        [(_stack_cols(g_in, BF16), False), (_stack_cols(g_uq, BF16), False), (_stack_cols(g_ukv, BF16), False),
         (_stack_cols(g_conv_w[:CONV_WIDTH], F32), False)], seq)

    dmod = jnp.concatenate([dshift1, dscale1, dgate1, dshift2, dscale2, dgate2], axis=1).reshape(n_b, N_DEV, ada_cols)
    (dmod_s,) = _exchange("scatter_dmod", [(dmod.transpose(1, 0, 2), False)])
    dmod_rows = jnp.pad(dmod_s.reshape(all_rows, ada_cols), ((0, pad_rows), (0, 0)))
    g_ada = _mm("ada_dw", c_rows, dmod_rows, "tn", F32, a_fn=_silu)
    (g_b_cols,) = _rowwise("ada_db", lambda r, b, cc: ([], [], [jnp.sum(r[0], axis=0, keepdims=True)]),
                           all_rows + pad_rows, all_rows + pad_rows, rows=[_full(dmod_rows)], tot_outs=[(1, ada_cols)])

    partial_of = {"norm1_g": g_norm1, "q_latent_g": g_gq, "kv_latent_g": g_gkv, "qk_norm_q_g": g_gqn,
                  "qk_norm_k_g": g_gkn, "conv_b": g_conv_b, "conv_ln_g": g_ln_g, "conv_ln_b": g_ln_b, "norm2_g": g_norm2}
    names = [n for n in REPLICATED if n != "b_ada"]
    pieces = [_pad_lanes(partial_of[n], -(-partial_of[n].shape[1] // LANES) * LANES) for n in names] + [g_b_cols]
    widths = [p.shape[1] for p in pieces]
    small = jnp.concatenate(pieces, axis=1)
    small = _pad_lanes(small, -(-small.shape[1] // (8 * LANES)) * 8 * LANES).reshape(-1, LANES)
    (small_s,) = _exchange("gather_small_grads", [(small, True)])
    small_s = small_s.reshape(N_DEV, 1, -1)
    parts = {}
    off = 0
    for n, wd in zip(names, widths):
        parts[n] = small_s[:, :, off:off + vec[n].shape[1]]
        off += wd
    parts["b_ada"] = small_s[:, 0, off:off + ada_cols].reshape(1, 1, N_DEV * ada_cols)
    parts.update({"w_ada": g_ada[None], "w_in": p_in, "w_uq": p_uq, "w_ukv": p_ukv, "w_o_mla": p_o, "conv_w": p_conv_w,
                  "w_pw_out": p_pw, "w_out": p_out, "w_ff1": p_ff1, "w_ff2": p_ff2})

    grad_out, delta_out, m_out, v_out = [], [], [], []
    for n in WEIGHTS:
        shape2 = parts[n].shape[1:]
        g_w, d_w, n_m, n_v = _adamw("adamw_" + n, local[n].reshape(shape2), parts[n], given["m_" + n].reshape(shape2),
                                    given["v_" + n].reshape(shape2))
        full_shape = given[n].shape
        grad_out.append(g_w.reshape(full_shape))
        delta_out.append(d_w.reshape(full_shape))
        m_out.append(n_m.reshape(full_shape))
        v_out.append(n_v.reshape(full_shape))
    return (loss, grad_x.reshape(n_b, seq, d), *grad_out, *delta_out, *m_out, *v_out)
```

```python
import functools

import jax
import jax.numpy as jnp
from jax import lax
from jax.experimental import pallas as pl
from jax.experimental.pallas import tpu as pltpu

F32 = jnp.float32
BF16 = jnp.bfloat16

N_DEV = 8
EPS = 1e-6
N_HEADS = 8
QK_HEAD_DIM = 96
QK_NOPE_DIM = 64
QK_ROPE_DIM = 32
V_HEAD_DIM = 64
HEAD_PAD = 128
Q_LORA = 256
KV_LORA = 128
CONV_CH = 512
CONV_WIDTH = 31
CONV_HALO = 32
CONV_TAIL = 8
CHUNK = 64
ROPE_THETA = 10000.0
OFF_Q = Q_LORA
OFF_KV = OFF_Q + KV_LORA
OFF_KR = OFF_KV + QK_ROPE_DIM
OFF_GLU = OFF_KR + 2 * CONV_CH
ADA_CHUNKS = 6
ADAM_LR = 0.001
ADAM_B1 = 0.9
ADAM_B2 = 0.999
ADAM_EPS = 1e-08
ADAM_WD = 0.01
ADAM_STEP = 10
LANES = 128
VMEM_LIMIT = 56 * 1024 * 1024
NEG_BIG = -1e30
ATT_HEADS = 4
ATT_TILE = 256
PREP_TILE = 1024
ATT_SCALE = QK_HEAD_DIM ** -0.5
LOG2E = 1.4426950408889634
LN2 = 0.6931471805599453
QK_SCALE = ATT_SCALE * LOG2E
ADAM_ROWS = 256
ROWS_PAD = 16

REPLICATED = ("b_ada", "norm1_g", "q_latent_g", "kv_latent_g", "qk_norm_q_g", "qk_norm_k_g", "conv_b", "conv_ln_g",
              "conv_ln_b", "norm2_g")
WEIGHTS = ("w_ada", "b_ada", "norm1_g", "w_in", "q_latent_g", "w_uq", "kv_latent_g", "w_ukv", "qk_norm_q_g",
           "qk_norm_k_g", "w_o_mla", "conv_w", "conv_b", "conv_ln_g", "conv_ln_b", "w_pw_out", "w_out", "norm2_g",
           "w_ff1", "w_ff2")


def _tile(dim, pref):
    if dim <= pref:
        return dim
    t = (pref // LANES) * LANES
    while dim % t:
        t -= LANES
    return t


def _params(semantics):
    return pltpu.CompilerParams(dimension_semantics=semantics, vmem_limit_bytes=VMEM_LIMIT)


def _sigmoid(v):
    return 1.0 / (1.0 + jnp.exp(-v))


def _silu(v):
    return v * _sigmoid(v)


def _relu2(v):
    return jnp.square(jnp.maximum(v, 0.0))


_DIMS = {"nn": (((1,), (0,)), ((), ())), "nt": (((1,), (1,)), ((), ())), "tn": (((0,), (0,)), ((), ()))}


def _mm(name, a, b, mode, out_dtype, *, a_fn=None, epi=None, epi_in=(), tm=1024, tn=1024, tk=1024):
    if mode == "nn":
        (m, k), n = a.shape, b.shape[1]
    elif mode == "nt":
        (m, k), n = a.shape, b.shape[0]
    else:
        (k, m), n = a.shape, b.shape[1]
    tm, tn, tk = _tile(m, tm), _tile(n, tn), _tile(k, tk)
    nk = k // tk
    a_spec = (pl.BlockSpec((tk, tm), lambda i, j, kk: (kk, i)) if mode == "tn"
              else pl.BlockSpec((tm, tk), lambda i, j, kk: (i, kk)))
    b_spec = (pl.BlockSpec((tn, tk), lambda i, j, kk: (j, kk)) if mode == "nt"
              else pl.BlockSpec((tk, tn), lambda i, j, kk: (kk, j)))
    o_spec = e_spec = pl.BlockSpec((tm, tn), lambda i, j, kk: (i, j))
    out_shape = jax.ShapeDtypeStruct((m, n), out_dtype)
    n_epi = len(epi_in)

    def body(a_ref, b_ref, *rest):
        epi_refs, o_ref, acc_ref = rest[:n_epi], rest[n_epi], rest[n_epi + 1]
        kk = pl.program_id(2)

        @pl.when(kk == 0)
        def _():
            acc_ref[...] = jnp.zeros_like(acc_ref)

        av = a_ref[...]
        if a_fn is not None:
            av = a_fn(av.astype(F32))
        acc_ref[...] += lax.dot_general(av.astype(BF16), b_ref[...].astype(BF16), _DIMS[mode],
                                        preferred_element_type=F32)

        @pl.when(kk == nk - 1)
        def _():
            acc = acc_ref[...]
            if epi is not None:
                acc = epi(acc, *[r[...].astype(F32) for r in epi_refs])
            o_ref[...] = acc.astype(out_dtype)

    return pl.pallas_call(
        body, name=name, grid=(m // tm, n // tn, nk),
        in_specs=[a_spec, b_spec] + [e_spec] * n_epi, out_specs=o_spec, out_shape=out_shape,
        scratch_shapes=[pltpu.VMEM((tm, tn), F32)],
        compiler_params=_params(("parallel", "parallel", "arbitrary")),
    )(a, b, *epi_in)


def _rowwise(name, fn, n_rows, seq, rows, bats=(), consts=(), outs=(), bat_outs=(), tot_outs=(), tm=256):
    tm = min(tm, seq)
    per_seq = seq // tm
    n_b = n_rows // seq
    nr, nb, nc, no, nbo, nto = len(rows), len(bats), len(consts), len(outs), len(bat_outs), len(tot_outs)

    def body(*refs):
        i = pl.program_id(0)
        r_in = [r[...] for r in refs[:nr]]
        b_in = [r[0] for r in refs[nr:nr + nb]]
        c_in = [r[...] for r in refs[nr + nb:nr + nb + nc]]
        o_refs = refs[nr + nb + nc:nr + nb + nc + no]
        bo_refs = refs[nr + nb + nc + no:nr + nb + nc + no + nbo]
        to_refs = refs[nr + nb + nc + no + nbo:]
        o_val, bo_val, to_val = fn(r_in, b_in, c_in)
        for r, v in zip(o_refs, o_val):
            r[...] = v.astype(r.dtype)
        if nbo:
            @pl.when(i % per_seq == 0)
            def _():
                for r in bo_refs:
                    r[...] = jnp.zeros_like(r)

            for r, v in zip(bo_refs, bo_val):
                r[0] += v
        if nto:
            @pl.when(i == 0)
            def _():
                for r in to_refs:
                    r[...] = jnp.zeros_like(r)

            for r, v in zip(to_refs, to_val):
                r[...] += v

    in_specs = [pl.BlockSpec((tm, w), functools.partial(lambda cb, i: (i, cb), cb)) for (_, w, cb) in rows]
    in_specs += [pl.BlockSpec((1, 1, bt.shape[2]), lambda i: (i // per_seq, 0, 0)) for bt in bats]
    in_specs += [pl.BlockSpec(ct.shape, lambda i: (0, 0)) for ct in consts]
    out_specs = [pl.BlockSpec((tm, w), lambda i: (i, 0)) for (w, _) in outs]
    out_specs += [pl.BlockSpec((1, 1, w), lambda i: (i // per_seq, 0, 0)) for w in bat_outs]
    out_specs += [pl.BlockSpec(shp, lambda i: (0, 0)) for shp in tot_outs]
    out_shape = [jax.ShapeDtypeStruct((n_rows, w), dt) for (w, dt) in outs]
    out_shape += [jax.ShapeDtypeStruct((n_b, 1, w), F32) for w in bat_outs]
    out_shape += [jax.ShapeDtypeStruct(shp, F32) for shp in tot_outs]
    res = pl.pallas_call(
        body, name=name, grid=(n_rows // tm,), in_specs=in_specs, out_specs=out_specs, out_shape=out_shape,
        compiler_params=_params(("arbitrary",)),
    )(*[r[0] for r in rows], *bats, *consts)
    return res


def _full(arr):
    return (arr, arr.shape[1], 0)


def _mm_rows(name, a_rows, a_fn, w, mode, fn, n_rows, seq, rows=(), bats=(), consts=(), outs=(), bat_outs=(),
             tot_outs=(), a_out=None, tm=512, tk=1024):
    tm = min(tm, seq)
    per_seq = seq // tm
    n_b = n_rows // seq
    k = a_rows[0][1]
    if mode == "nt":
        n_out, tk = w.shape[0], _tile(k, tk)
        w_spec = pl.BlockSpec((n_out, tk), lambda i, kk: (0, kk))
    else:
        n_out, tk = w.shape[1], _tile(k, tk)
        w_spec = pl.BlockSpec((tk, n_out), lambda i, kk: (kk, 0))
    nk = k // tk
    na, nr, nb, nc = len(a_rows), len(rows), len(bats), len(consts)
    n_extra = 0 if a_out is None else 1
    no, nbo, nto = len(outs), len(bat_outs), len(tot_outs)

    def body(*refs):
        i, kk = pl.program_id(0), pl.program_id(1)
        a_refs, w_ref = refs[:na], refs[na]
        pos = na + 1
        r_refs, b_refs, c_refs = refs[pos:pos + nr], refs[pos + nr:pos + nr + nb], refs[pos + nr + nb:pos + nr + nb + nc]
        pos += nr + nb + nc
        ao_refs = refs[pos:pos + n_extra]
        pos += n_extra
        o_refs, bo_refs, to_refs = refs[pos:pos + no], refs[pos + no:pos + no + nbo], refs[pos + no + nbo:pos + no + nbo + nto]
        acc_ref = refs[pos + no + nbo + nto]

        @pl.when(kk == 0)
        def _():
            acc_ref[...] = jnp.zeros_like(acc_ref)

        tiles = [r[...] for r in a_refs]
        av = a_fn([t.astype(F32) for t in tiles]) if a_fn is not None else tiles[0]
        av = av.astype(BF16)
        if n_extra:
            ao_refs[0][...] = av.astype(ao_refs[0].dtype)
        acc_ref[...] += lax.dot_general(av, w_ref[...].astype(BF16), _DIMS[mode], preferred_element_type=F32)

        @pl.when(kk == nk - 1)
        def _():
            o_val, bo_val, to_val = fn(acc_ref[...], [r[...] for r in r_refs], [r[0] for r in b_refs],
                                       [r[...] for r in c_refs])
            for r, v in zip(o_refs, o_val):
                r[...] = v.astype(r.dtype)
            if nbo:
                @pl.when(i % per_seq == 0)
                def _():
                    for r in bo_refs:
                        r[...] = jnp.zeros_like(r)

                for r, v in zip(bo_refs, bo_val):
                    r[0] += v
            if nto:
                @pl.when(i == 0)
                def _():
                    for r in to_refs:
                        r[...] = jnp.zeros_like(r)

                for r, v in zip(to_refs, to_val):
                    r[...] += v

    in_specs = [pl.BlockSpec((tm, tk), functools.partial(lambda cb, i, kk: (i, kk + cb), cb)) for (_, _, cb) in a_rows]
    in_specs += [w_spec]
    in_specs += [pl.BlockSpec((tm, wd), functools.partial(lambda cb, i, kk: (i, cb), cb)) for (_, wd, cb) in rows]
    in_specs += [pl.BlockSpec((1, 1, bt.shape[2]), lambda i, kk: (i // per_seq, 0, 0)) for bt in bats]
    in_specs += [pl.BlockSpec(ct.shape, lambda i, kk: (0, 0)) for ct in consts]
    out_specs = [pl.BlockSpec((tm, tk), lambda i, kk: (i, kk))] * n_extra
    out_specs += [pl.BlockSpec((tm, wd), lambda i, kk: (i, 0)) for (wd, _) in outs]
    out_specs += [pl.BlockSpec((1, 1, wd), lambda i, kk: (i // per_seq, 0, 0)) for wd in bat_outs]
    out_specs += [pl.BlockSpec(shp, lambda i, kk: (0, 0)) for shp in tot_outs]
    out_shape = [jax.ShapeDtypeStruct((n_rows, k), a_out)] if n_extra else []
    out_shape += [jax.ShapeDtypeStruct((n_rows, wd), dt) for (wd, dt) in outs]
    out_shape += [jax.ShapeDtypeStruct((n_b, 1, wd), F32) for wd in bat_outs]
    out_shape += [jax.ShapeDtypeStruct(shp, F32) for shp in tot_outs]
    return pl.pallas_call(
        body, name=name, grid=(n_rows // tm, nk), in_specs=in_specs, out_specs=out_specs, out_shape=out_shape,
        scratch_shapes=[pltpu.VMEM((tm, n_out), F32)],
        compiler_params=_params(("arbitrary", "arbitrary")),
    )(*[a for a, _, _ in a_rows], w, *[r[0] for r in rows], *bats, *consts)


def _norm_mod(x, g, scale, shift):
    r = lax.rsqrt(jnp.mean(x * x, axis=-1, keepdims=True) + EPS)
    xh = x * r
    return xh * g * (1.0 + scale) + shift


def _norm_mod_bwd(x, g, scale, dh):
    r = lax.rsqrt(jnp.mean(x * x, axis=-1, keepdims=True) + EPS)
    xh = x * r
    dn = dh * (1.0 + scale)
    dxh = dn * g
    dx = r * (dxh - xh * jnp.mean(dxh * xh, axis=-1, keepdims=True))
    dscale = jnp.sum(dh * xh * g, axis=0, keepdims=True)
    dshift = jnp.sum(dh, axis=0, keepdims=True)
    dg = jnp.sum(dn * xh, axis=0, keepdims=True)
    return dx, dscale, dshift, dg


def _rms(v, g):
    r = lax.rsqrt(jnp.mean(v * v, axis=-1, keepdims=True) + EPS)
    return v * r * g


def _rms_bwd(v, g, dy):
    r = lax.rsqrt(jnp.mean(v * v, axis=-1, keepdims=True) + EPS)
    vh = v * r
    dvh = dy * g
    dv = r * (dvh - vh * jnp.mean(dvh * vh, axis=-1, keepdims=True))
    return dv, jnp.sum(dy * vh, axis=0, keepdims=True)


def _head_norm(v, g):
    r = lax.rsqrt(jnp.sum(v * v, axis=-1, keepdims=True) * (1.0 / QK_HEAD_DIM) + EPS)
    return v * r * g


def _head_norm_bwd(v, g, dy):
    r = lax.rsqrt(jnp.sum(v * v, axis=-1, keepdims=True) * (1.0 / QK_HEAD_DIM) + EPS)
    vh = v * r
    dvh = dy * g
    dv = r * (dvh - vh * (jnp.sum(dvh * vh, axis=-1, keepdims=True) * (1.0 / QK_HEAD_DIM)))
    return dv, jnp.sum(dy * vh, axis=0, keepdims=True)


def _rope(v, cos, sin_lo, sin_hi):
    return v * cos + pltpu.roll(v, HEAD_PAD - 16, 1) * sin_lo + pltpu.roll(v, 16, 1) * sin_hi


def _rope_bwd(g, cos, sin_lo, sin_hi):
    return g * cos + pltpu.roll(g * sin_lo, 16, 1) + pltpu.roll(g * sin_hi, HEAD_PAD - 16, 1)


def _mla_prep_fwd(zsm, wuq, wukv, gq, gkv, gqn, gkn, rope, n_b, seq):
    n_rows = n_b * seq
    tm = min(PREP_TILE, seq)
    per_seq = seq // tm
    att_tile = min(ATT_TILE, seq)
    k_cols = N_HEADS * HEAD_PAD

    def body(z_ref, wuq_ref, wukv_ref, gq_ref, gkv_ref, gqn_ref, gkn_ref, c_ref, s1_ref, s2_ref,
             q_ref, k_ref, v_ref, kt_ref):
        z = z_ref[...]
        qn = _rms(z[:, :Q_LORA], gq_ref[...]).astype(BF16)
        kvn = _rms(z[:, Q_LORA:Q_LORA + KV_LORA], gkv_ref[...]).astype(BF16)
        krp = z[:, Q_LORA + KV_LORA:]
        cos, s1, s2 = c_ref[...], s1_ref[...], s2_ref[...]
        q_all = jnp.dot(qn, wuq_ref[...], preferred_element_type=F32)
        kv_all = jnp.dot(kvn, wukv_ref[...], preferred_element_type=F32)
        for h in range(N_HEADS):
            cols = slice(h * HEAD_PAD, (h + 1) * HEAD_PAD)
            q_ref[0, h] = (_rope(_head_norm(q_all[:, cols], gqn_ref[...]), cos, s1, s2) * QK_SCALE).astype(BF16)
            kh = _rope(_head_norm(kv_all[:, cols] + krp, gkn_ref[...]), cos, s1, s2)
            k_ref[0, h] = kh.astype(BF16)
            for part in range(tm // att_tile):
                kt_ref[0, h, part] = kh[part * att_tile:(part + 1) * att_tile].T.astype(BF16)
            v_ref[0, h] = kv_all[:, k_cols + h * HEAD_PAD:k_cols + (h + 1) * HEAD_PAD].astype(BF16)

    whole2 = lambda arr: pl.BlockSpec(arr.shape, lambda i: (0, 0))
    rope_spec = pl.BlockSpec((tm, HEAD_PAD), lambda i: (i % per_seq, 0))
    head_spec = pl.BlockSpec((1, N_HEADS, tm, HEAD_PAD), lambda i: (i // per_seq, 0, i % per_seq, 0))
    head_shape = jax.ShapeDtypeStruct((n_b, N_HEADS, seq, HEAD_PAD), BF16)
    t_spec = pl.BlockSpec((1, N_HEADS, tm // att_tile, HEAD_PAD, att_tile), lambda i: (i // per_seq, 0, i % per_seq, 0, 0))
    t_shape = jax.ShapeDtypeStruct((n_b, N_HEADS, seq // att_tile, HEAD_PAD, att_tile), BF16)
    return pl.pallas_call(
        body, name="mla_prep_fwd", grid=(n_rows // tm,),
        in_specs=[pl.BlockSpec((tm, 512), lambda i: (i, 0)), whole2(wuq), whole2(wukv),
                  whole2(gq), whole2(gkv), whole2(gqn), whole2(gkn), rope_spec, rope_spec, rope_spec],
        out_specs=[head_spec] * 3 + [t_spec], out_shape=[head_shape] * 3 + [t_shape],
        compiler_params=_params(("parallel",)),
    )(zsm, wuq, wukv, gq, gkv, gqn, gkn, *rope)


def _mla_prep_bwd(zsm, dq, dk, dv, wuq, wukv, gq, gkv, gqn, gkn, rope, n_b, seq):
    n_rows = n_b * seq
    tm = min(PREP_TILE, seq)
    per_seq = seq // tm
    tn_dims = _DIMS["tn"]
    nt_dims = _DIMS["nt"]
    k_cols = N_HEADS * HEAD_PAD

    def body(z_ref, dq_ref, dk_ref, dv_ref, wuq_ref, wukv_ref, gq_ref, gkv_ref, gqn_ref, gkn_ref,
             c_ref, s1_ref, s2_ref, dz_ref, dwuq_ref, dwukv_ref, dgq_ref, dgkv_ref, dgqn_ref, dgkn_ref):
        @pl.when(pl.program_id(0) == 0)
        def _():
            for r in (dwuq_ref, dwukv_ref, dgq_ref, dgkv_ref, dgqn_ref, dgkn_ref):
                r[...] = jnp.zeros_like(r)

        z = z_ref[...]
        zq, zkv, krp = z[:, :Q_LORA], z[:, Q_LORA:Q_LORA + KV_LORA], z[:, Q_LORA + KV_LORA:]
        qn = _rms(zq, gq_ref[...]).astype(BF16)
        kvn = _rms(zkv, gkv_ref[...]).astype(BF16)
        cos, s1, s2 = c_ref[...], s1_ref[...], s2_ref[...]
        lane = lax.broadcasted_iota(jnp.int32, (tm, HEAD_PAD), 1)
        rope_lanes = (lane >= QK_NOPE_DIM) & (lane < QK_HEAD_DIM)
        q_all = jnp.dot(qn, wuq_ref[...], preferred_element_type=F32)
        k_all = jnp.dot(kvn, wukv_ref[:, :k_cols], preferred_element_type=F32)
        dkrp = jnp.zeros((tm, HEAD_PAD), F32)
        dgqn = jnp.zeros((1, HEAD_PAD), F32)
        dgkn = jnp.zeros((1, HEAD_PAD), F32)
        dq_heads, dk_heads = [], []
        for h in range(N_HEADS):
            cols = slice(h * HEAD_PAD, (h + 1) * HEAD_PAD)
            dqh, dg = _head_norm_bwd(q_all[:, cols], gqn_ref[...],
                                     _rope_bwd(dq_ref[0, h].astype(F32) * ATT_SCALE, cos, s1, s2))
            dgqn += dg
            dq_heads.append(dqh.astype(BF16))
            dkh, dg = _head_norm_bwd(k_all[:, cols] + krp, gkn_ref[...], _rope_bwd(dk_ref[0, h].astype(F32), cos, s1, s2))
            dgkn += dg
            dkrp += jnp.where(rope_lanes, dkh, 0.0)
            dk_heads.append(dkh.astype(BF16))
        dq_all = jnp.concatenate(dq_heads, axis=1)
        dkv_all = jnp.concatenate(dk_heads + [dv_ref[0, h] for h in range(N_HEADS)], axis=1)
        dwuq_ref[...] += lax.dot_general(qn, dq_all, tn_dims, preferred_element_type=F32)
        dqn = lax.dot_general(dq_all, wuq_ref[...], nt_dims, preferred_element_type=F32)
        dwukv_ref[...] += lax.dot_general(kvn, dkv_all, tn_dims, preferred_element_type=F32)
        dkvn = lax.dot_general(dkv_all, wukv_ref[...], nt_dims, preferred_element_type=F32)
        dzq, dg = _rms_bwd(zq, gq_ref[...], dqn)
        dgq_ref[...] += dg
        dzkv, dg = _rms_bwd(zkv, gkv_ref[...], dkvn)
        dgkv_ref[...] += dg
        dgqn_ref[...] += dgqn
        dgkn_ref[...] += dgkn
        dz_ref[:, :Q_LORA] = dzq.astype(dz_ref.dtype)
        dz_ref[:, Q_LORA:Q_LORA + KV_LORA] = dzkv.astype(dz_ref.dtype)
        dz_ref[:, Q_LORA + KV_LORA:] = dkrp.astype(dz_ref.dtype)

    whole2 = lambda arr: pl.BlockSpec(arr.shape, lambda i: (0, 0))
    rope_spec = pl.BlockSpec((tm, HEAD_PAD), lambda i: (i % per_seq, 0))
    head_spec = pl.BlockSpec((1, N_HEADS, tm, HEAD_PAD), lambda i: (i // per_seq, 0, i % per_seq, 0))
    row_spec = pl.BlockSpec((tm, 512), lambda i: (i, 0))
    return pl.pallas_call(
        body, name="mla_prep_bwd", grid=(n_rows // tm,),
        in_specs=[row_spec, head_spec, head_spec, head_spec, whole2(wuq), whole2(wukv),
                  whole2(gq), whole2(gkv), whole2(gqn), whole2(gkn), rope_spec, rope_spec, rope_spec],
        out_specs=[row_spec, whole2(wuq), whole2(wukv), whole2(gq), whole2(gkv), whole2(gqn), whole2(gkn)],
        out_shape=[jax.ShapeDtypeStruct((n_rows, 512), BF16),
                   jax.ShapeDtypeStruct(wuq.shape, F32), jax.ShapeDtypeStruct(wukv.shape, F32),
                   jax.ShapeDtypeStruct(gq.shape, F32), jax.ShapeDtypeStruct(gkv.shape, F32),
                   jax.ShapeDtypeStruct(gqn.shape, F32), jax.ShapeDtypeStruct(gkn.shape, F32)],
        compiler_params=_params(("arbitrary",)),
    )(zsm, dq, dk, dv, wuq, wukv, gq, gkv, gqn, gkn, *rope)


HBM_SPEC = pl.BlockSpec(memory_space=pltpu.HBM)


def _xchg_out_shapes(bufs):
    return [jax.ShapeDtypeStruct((N_DEV,) + (a.shape if gather else a.shape[1:]), a.dtype) for a, gather in bufs]


def _xchg_scratch(n_buf):
    return [pltpu.SemaphoreType.DMA((n_buf * (N_DEV - 1),)), pltpu.SemaphoreType.DMA((n_buf * (N_DEV - 1),)),
            pltpu.SemaphoreType.DMA((n_buf,))]


def _xchg_copies(src_refs, dst_refs, gathers, send_sems, recv_sems, local_sems):
    x, y, c = lax.axis_index("x"), lax.axis_index("y"), lax.axis_index("c")
    me = 4 * x + 2 * y + c
    local, starts, arrivals = [], [], []
    for bi, (src, dst, gather) in enumerate(zip(src_refs, dst_refs, gathers)):
        local.append(pltpu.make_async_copy(src if gather else src.at[me], dst.at[me], local_sems.at[bi]))
        for kk in range(1, N_DEV):
            px = 1 - x if kk & 4 else x
            py = 1 - y if kk & 2 else y
            pc = 1 - c if kk & 1 else c
            pid = 4 * px + 2 * py + pc
            sem = bi * (N_DEV - 1) + kk - 1
            starts.append(pltpu.make_async_remote_copy(
                src_ref=src if gather else src.at[pid], dst_ref=dst.at[me],
                send_sem=send_sems.at[sem], recv_sem=recv_sems.at[sem],
                device_id=(px, py, pc), device_id_type=pl.DeviceIdType.MESH))
            arrivals.append(pltpu.make_async_remote_copy(
                src_ref=src if gather else src.at[me], dst_ref=dst.at[pid],
                send_sem=send_sems.at[sem], recv_sem=recv_sems.at[sem],
                device_id=(px, py, pc), device_id_type=pl.DeviceIdType.MESH))
    return local, starts, arrivals


def _xchg_start(copies):
    local, sends, _ = copies
    for cp in local + sends:
        cp.start()


def _xchg_finish(copies):
    local, sends, arrivals = copies
    for cp in arrivals:
        cp.wait_recv()
    for cp in sends:
        cp.wait_send()
    for cp in local:
        cp.wait()


def _gather_by_chip(src_refs, dst_refs, send_sems, recv_sems, local_sems):
    x, y, c = lax.axis_index("x"), lax.axis_index("y"), lax.axis_index("c")
    me = 4 * x + 2 * y + c
    sibling = (x, y, 1 - c)

    def place(kk):
        px = 1 - x if kk & 4 else x
        py = 1 - y if kk & 2 else y
        pc = 1 - c if kk & 1 else c
        return (px, py, pc), 4 * px + 2 * py + pc

    def copy(bi, kk, src, dst, to):
        sem = bi * (N_DEV - 1) + kk - 1
        return pltpu.make_async_remote_copy(src_ref=src, dst_ref=dst, send_sem=send_sems.at[sem],
                                            recv_sem=recv_sems.at[sem], device_id=to, device_id_type=pl.DeviceIdType.MESH)

    local, sends = [], []
    for bi, (src, dst) in enumerate(zip(src_refs, dst_refs)):
        local.append(pltpu.make_async_copy(src, dst.at[me], local_sems.at[bi]))
        sends += [copy(bi, kk, src, dst.at[me], place(kk)[0]) for kk in (1, 2, 4, 6)]
    for cp in local + sends:
        cp.start()
    for kk in (2, 4, 6):
        for bi, (src, dst) in enumerate(zip(src_refs, dst_refs)):
            dev, pid = place(kk)
            copy(bi, kk, src, dst.at[pid], dev).wait_recv()
            passed = copy(bi, kk | 1, dst.at[pid], dst.at[pid], sibling)
            passed.start()
            sends.append(passed)
    for kk in (1, 3, 5, 7):
        for bi, (src, dst) in enumerate(zip(src_refs, dst_refs)):
            dev, pid = place(kk)
            copy(bi, kk, src, dst.at[pid], sibling).wait_recv()
    for cp in sends:
        cp.wait_send()
    for cp in local:
        cp.wait()


def _exchange(name, bufs, by_chip=False):
    n_buf = len(bufs)
    gathers = [g for _, g in bufs]
    assert not by_chip or all(gathers)

    def body(*refs):
        srcs, dsts = refs[:n_buf], refs[n_buf:2 * n_buf]
        if by_chip:
            _gather_by_chip(srcs, dsts, *refs[2 * n_buf:])
            return
        copies = _xchg_copies(srcs, dsts, gathers, *refs[2 * n_buf:])
        _xchg_start(copies)
        _xchg_finish(copies)

    return pl.pallas_call(
        body, name=name, out_shape=_xchg_out_shapes(bufs),
        in_specs=[HBM_SPEC] * n_buf, out_specs=[HBM_SPEC] * n_buf, scratch_shapes=_xchg_scratch(n_buf),
    )(*[a for a, _ in bufs])


def _chunk_mask(t, keys_first):
    key = lax.broadcasted_iota(jnp.int32, (t, t), 0 if keys_first else 1) // CHUNK
    query = lax.broadcasted_iota(jnp.int32, (t, t), 1 if keys_first else 0) // CHUNK
    return query >= key


def _grid_ends(grid):
    ids = [pl.program_id(ax) for ax in range(len(grid))]
    first = functools.reduce(jnp.logical_and, [i == 0 for i in ids])
    last = functools.reduce(jnp.logical_and, [i == g - 1 for i, g in zip(ids, grid)])
    return first, last


def _attn_fwd(q, k, v, bufs, n_b, seq):
    tq = min(ATT_TILE, seq)
    nq = seq // tq
    nt_dims = _DIMS["nt"]
    hpb = ATT_HEADS
    grid = (n_b, N_HEADS // hpb, nq)
    n_buf = len(bufs)
    gathers = [g for _, g in bufs]
    sum_lane = [HEAD_PAD - 1 if hh % 2 == 0 else 0 for hh in range(hpb)]

    def body(q_ref, k_ref, v_ref, *rest):
        srcs, (o_ref, lse_ref), dsts = rest[:n_buf], rest[n_buf:n_buf + 2], rest[n_buf + 2:2 * n_buf + 2]
        s_ref = rest[2 * n_buf + 2]
        copies = _xchg_copies(srcs, dsts, gathers, *rest[2 * n_buf + 3:])
        first, last = _grid_ends(grid)
        pl.when(first)(functools.partial(_xchg_start, copies))

        qi = pl.program_id(2)
        mask = _chunk_mask(tq, keys_first=False)
        lane_row = lax.broadcasted_iota(jnp.int32, (1, HEAD_PAD), 1)
        ones = [(lane_row == sum_lane[hh]).astype(BF16) for hh in range(hpb)]
        qs = [q_ref[0, hh] for hh in range(hpb)]

        def score_step(j, tops, masked):
            rows = pl.ds(pl.multiple_of(j * tq, tq), tq)
            out = []
            for hh in range(hpb):
                s = lax.dot_general(qs[hh], k_ref[0, hh, rows, :], nt_dims, preferred_element_type=F32)
                if masked:
                    s = jnp.where(mask, s, NEG_BIG)
                s_ref[hh, j] = s
                out.append(jnp.maximum(tops[hh], s))
            return tuple(out)

        tops = tuple(jnp.full((tq, tq), NEG_BIG, F32) for _ in range(hpb))
        tops = lax.fori_loop(0, qi, functools.partial(score_step, masked=False), tops)
        tops = score_step(qi, tops, True)
        ms = [jnp.max(top, axis=-1, keepdims=True) for top in tops]

        def value_step(j, accs):
            rows = pl.ds(pl.multiple_of(j * tq, tq), tq)
            out = []
            for hh in range(hpb):
                p = jnp.exp2(s_ref[hh, j] - ms[hh]).astype(BF16)
                out.append(accs[hh] + jnp.dot(p, v_ref[0, hh, rows, :] + ones[hh], preferred_element_type=F32))
            return tuple(out)

        accs = tuple(jnp.zeros((tq, HEAD_PAD), F32) for _ in range(hpb))
        accs = lax.fori_loop(0, qi + 1, value_step, accs)
        carry = list(zip(ms, accs))
        lane = lax.broadcasted_iota(jnp.int32, (tq, HEAD_PAD), 1)
        for pair in range(hpb // 2):
            outs = []
            for hh in (2 * pair, 2 * pair + 1):
                m, acc = carry[hh]
                l = jnp.sum(jnp.where(lane == sum_lane[hh], acc, 0.0), axis=-1, keepdims=True)
                outs.append(acc * (1.0 / l))
                lse_ref[0, hh] = jnp.broadcast_to(m + jnp.log2(l), (tq, HEAD_PAD)).T[0:8, :]
            o_ref[0, :, pair * HEAD_PAD:(pair + 1) * HEAD_PAD] = jnp.where(lane < V_HEAD_DIM, outs[0], outs[1]).astype(BF16)

        pl.when(last)(functools.partial(_xchg_finish, copies))

    kv_spec = pl.BlockSpec((1, hpb, seq, HEAD_PAD), lambda b, hb, i: (b, hb, 0, 0))
    q_spec = pl.BlockSpec((1, hpb, tq, HEAD_PAD), lambda b, hb, i: (b, hb, i, 0))
    res = pl.pallas_call(
        body, name="attn_fwd", grid=grid,
        in_specs=[q_spec, kv_spec, kv_spec] + [HBM_SPEC] * n_buf,
        out_specs=[pl.BlockSpec((1, tq, hpb * V_HEAD_DIM), lambda b, hb, i: (b, i, hb)),
                   pl.BlockSpec((1, hpb, 8, tq), lambda b, hb, i: (b, hb, 0, i))] + [HBM_SPEC] * n_buf,
        out_shape=[jax.ShapeDtypeStruct((n_b, seq, N_HEADS * V_HEAD_DIM), BF16),
                   jax.ShapeDtypeStruct((n_b, N_HEADS, 8, seq), F32)] + _xchg_out_shapes(bufs),
        scratch_shapes=[pltpu.VMEM((hpb, nq, tq, tq), F32)] + _xchg_scratch(n_buf),
        compiler_params=_params(("arbitrary", "arbitrary", "arbitrary")),
    )(q, k, v, *[a for a, _ in bufs])
    return res[0], res[1], res[2:]


def _attn_bwd(q, k, v, kt, do, o, lse, bufs, n_b, seq):
    tq = min(ATT_TILE, seq)
    nq = seq // tq
    nt_dims = _DIMS["nt"]
    hpb = ATT_HEADS
    grid = (n_b, N_HEADS // hpb, nq)
    n_buf = len(bufs)
    gathers = [g for _, g in bufs]

    def body(q_ref, k_ref, v_ref, kt_ref, do_ref, o_ref, lse_ref, *rest):
        srcs, (dq_ref, dk_ref, dv_ref), dsts = rest[:n_buf], rest[n_buf:n_buf + 3], rest[n_buf + 3:2 * n_buf + 3]
        dk_acc, dv_acc = rest[2 * n_buf + 3:2 * n_buf + 5]
        copies = _xchg_copies(srcs, dsts, gathers, *rest[2 * n_buf + 5:])
        first, last = _grid_ends(grid)
        pl.when(first)(functools.partial(_xchg_start, copies))

        qi = pl.program_id(2)

        @pl.when(qi == 0)
        def _():
            dk_acc[...] = jnp.zeros_like(dk_acc)
            dv_acc[...] = jnp.zeros_like(dv_acc)

        mask = _chunk_mask(tq, keys_first=True)
        lane = lax.broadcasted_iota(jnp.int32, (tq, HEAD_PAD), 1)
        qs, dos, deltas, lses = [], [], [], []
        for hh in range(hpb):
            cols = slice((hh // 2) * HEAD_PAD, (hh // 2 + 1) * HEAD_PAD)
            do_pair = do_ref[0, :, cols]
            prod = do_pair.astype(F32) * o_ref[0, :, cols].astype(F32)
            delta = jnp.sum(jnp.where(lane // V_HEAD_DIM == hh % 2, prod, 0.0), axis=-1, keepdims=True)
            qs.append(q_ref[0, hh])
            dos.append(do_pair)
            deltas.append(jnp.broadcast_to(delta, (tq, HEAD_PAD)).T[0:1, :])
            lses.append(lse_ref[0, hh][0:1, :])

        def step(j, dqs, masked):
            rows = pl.ds(pl.multiple_of(j * tq, tq), tq)
            out = []
            for hh in range(hpb):
                s = lax.dot_general(k_ref[0, hh, rows, :], qs[hh], nt_dims, preferred_element_type=F32)
                p = jnp.exp2(s - lses[hh])
                if masked:
                    p = jnp.where(mask, p, 0.0)
                dv_acc[hh, rows, :] += jnp.dot(p.astype(BF16), dos[hh], preferred_element_type=F32)
                dp = lax.dot_general(v_ref[0, hh, rows, :], dos[hh], nt_dims, preferred_element_type=F32)
                ds = (p * (dp - deltas[hh])).astype(BF16)
                dk_acc[hh, rows, :] += jnp.dot(ds, qs[hh], preferred_element_type=F32)
                out.append(dqs[hh] + jnp.dot(kt_ref[0, hh, j], ds, preferred_element_type=F32))
            return tuple(out)

        dqs = tuple(jnp.zeros((HEAD_PAD, tq), F32) for _ in range(hpb))
        dqs = lax.fori_loop(0, qi, functools.partial(step, masked=False), dqs)
        dqs = step(qi, dqs, True)
        for hh in range(hpb):
            dq_ref[0, hh] = dqs[hh].T.astype(BF16)

        @pl.when(qi == nq - 1)
        def _():
            dk_ref[0] = (dk_acc[...] * LN2).astype(BF16)
            dv_ref[0] = dv_acc[...].astype(BF16)

        pl.when(last)(functools.partial(_xchg_finish, copies))

    full_spec = pl.BlockSpec((1, hpb, seq, HEAD_PAD), lambda b, hb, i: (b, hb, 0, 0))
    t_spec = pl.BlockSpec((1, hpb, nq, HEAD_PAD, tq), lambda b, hb, i: (b, hb, 0, 0, 0))
    q_spec = pl.BlockSpec((1, hpb, tq, HEAD_PAD), lambda b, hb, i: (b, hb, i, 0))
    o_spec = pl.BlockSpec((1, tq, hpb * V_HEAD_DIM), lambda b, hb, i: (b, i, hb))
    lse_spec = pl.BlockSpec((1, hpb, 8, tq), lambda b, hb, i: (b, hb, 0, i))
    head_shape = jax.ShapeDtypeStruct((n_b, N_HEADS, seq, HEAD_PAD), BF16)
    res = pl.pallas_call(
        body, name="attn_bwd", grid=grid,
        in_specs=[q_spec, full_spec, full_spec, t_spec, o_spec, o_spec, lse_spec] + [HBM_SPEC] * n_buf,
        out_specs=[q_spec, full_spec, full_spec] + [HBM_SPEC] * n_buf,
        out_shape=[head_shape] * 3 + _xchg_out_shapes(bufs),
        scratch_shapes=[pltpu.VMEM((hpb, seq, HEAD_PAD), F32), pltpu.VMEM((hpb, seq, HEAD_PAD), F32)]
        + _xchg_scratch(n_buf),
        compiler_params=_params(("arbitrary", "arbitrary", "arbitrary")),
    )(q, k, v, kt, do, o, lse, *[a for a, _ in bufs])
    return res[0], res[1], res[2], res[3:]


def _in_proj_fwd(x2, scale, shift, g, w_parts, z_dtypes, seq):
    n_rows, d = x2.shape
    tm = min(512, seq)
    per_seq = seq // tm
    n_part = len(w_parts)
    nt_dims = _DIMS["nt"]

    def body(x_ref, sc_ref, sh_ref, g_ref, *rest):
        w_refs, h_ref, z_refs = rest[:n_part], rest[n_part], rest[n_part + 1:]
        h = _norm_mod(x_ref[...], g_ref[...], sc_ref[0], sh_ref[0]).astype(BF16)
        h_ref[...] = h
        for w_ref, z_ref in zip(w_refs, z_refs):
            z_ref[...] = lax.dot_general(h, w_ref[...], nt_dims, preferred_element_type=F32).astype(z_ref.dtype)

    row = lambda width: pl.BlockSpec((tm, width), lambda i: (i, 0))
    bat = pl.BlockSpec((1, 1, d), lambda i: (i // per_seq, 0, 0))
    whole = lambda arr: pl.BlockSpec(arr.shape, lambda i: (0, 0))
    return pl.pallas_call(
        body, name="in_proj_fwd", grid=(n_rows // tm,),
        in_specs=[row(d), bat, bat, whole(g)] + [whole(w) for w in w_parts],
        out_specs=[row(d)] + [row(w.shape[0]) for w in w_parts],
        out_shape=[jax.ShapeDtypeStruct((n_rows, d), BF16)]
        + [jax.ShapeDtypeStruct((n_rows, w.shape[0]), dt) for w, dt in zip(w_parts, z_dtypes)],
        compiler_params=_params(("parallel",)),
    )(x2, scale, shift, g, *w_parts)
def _in_proj_bwd(parts, x2, dx1, scale, g, bufs, seq):
    n_rows, d = x2.shape
    tm = min(512, seq)
    per_seq = seq // tm
    grid = (n_rows // tm,)
    n_part, n_buf = len(parts), len(bufs)
    gathers = [gt for _, gt in bufs]

    def body(*refs):
        dz_refs, w_refs = refs[:n_part], refs[n_part:2 * n_part]
        x_ref, dx1_ref, sc_ref, g_ref = refs[2 * n_part:2 * n_part + 4]
        srcs = refs[2 * n_part + 4:2 * n_part + 4 + n_buf]
        gx_ref, dsc_ref, dsh_ref, dg_ref = refs[2 * n_part + 4 + n_buf:2 * n_part + 8 + n_buf]
        dsts = refs[2 * n_part + 8 + n_buf:2 * n_part + 8 + 2 * n_buf]
        copies = _xchg_copies(srcs, dsts, gathers, *refs[2 * n_part + 8 + 2 * n_buf:])
        first, last = _grid_ends(grid)
        pl.when(first)(functools.partial(_xchg_start, copies))

        i = pl.program_id(0)
        dh = None
        for dz_ref, w_ref in zip(dz_refs, w_refs):
            term = jnp.dot(dz_ref[...], w_ref[...], preferred_element_type=F32)
            dh = term if dh is None else dh + term
        dx, dsc, dsh, dg = _norm_mod_bwd(x_ref[...], g_ref[...], sc_ref[0], dh)
        gx_ref[...] = dx1_ref[...] + dx

        @pl.when(i % per_seq == 0)
        def _():
            dsc_ref[...] = jnp.zeros_like(dsc_ref)
            dsh_ref[...] = jnp.zeros_like(dsh_ref)

        @pl.when(i == 0)
        def _():
            dg_ref[...] = jnp.zeros_like(dg_ref)

        dsc_ref[0] += dsc
        dsh_ref[0] += dsh
        dg_ref[...] += dg
        pl.when(last)(functools.partial(_xchg_finish, copies))

    row = lambda width: pl.BlockSpec((tm, width), lambda i: (i, 0))
    bat = pl.BlockSpec((1, 1, d), lambda i: (i // per_seq, 0, 0))
    whole = lambda arr: pl.BlockSpec(arr.shape, lambda i: (0, 0))
    n_b = n_rows // seq
    res = pl.pallas_call(
        body, name="in_proj_bwd", grid=grid,
        in_specs=[row(dz.shape[1]) for dz, _ in parts] + [whole(w) for _, w in parts]
        + [row(d), row(d), bat, whole(g)] + [HBM_SPEC] * n_buf,
        out_specs=[row(d), bat, bat, whole(g)] + [HBM_SPEC] * n_buf,
        out_shape=[jax.ShapeDtypeStruct((n_rows, d), F32), jax.ShapeDtypeStruct((n_b, 1, d), F32),
                   jax.ShapeDtypeStruct((n_b, 1, d), F32), jax.ShapeDtypeStruct(g.shape, F32)] + _xchg_out_shapes(bufs),
        scratch_shapes=_xchg_scratch(n_buf),
        compiler_params=_params(("arbitrary",)),
    )(*[dz for dz, _ in parts], *[w for _, w in parts], x2, dx1, scale, g, *[a for a, _ in bufs])
    return res[0], res[1], res[2], res[3], res[4:]


def _ln_silu(u1, g, b):
    mu = jnp.mean(u1, axis=-1, keepdims=True)
    uc = u1 - mu
    r = lax.rsqrt(jnp.mean(uc * uc, axis=-1, keepdims=True) + EPS)
    y = uc * r * g + b
    return y * _sigmoid(y)


def _conv_fill_glu(z_ref, u0_ref, seq, tile):
    u0_ref[0:CONV_HALO, :] = jnp.zeros((CONV_HALO, CONV_CH), F32)
    u0_ref[CONV_HALO + seq:CONV_HALO + seq + CONV_TAIL, :] = jnp.zeros((CONV_TAIL, CONV_CH), F32)
    for t in range(seq // tile):
        zt = z_ref[0, t * tile:(t + 1) * tile, :].astype(F32)
        u0_ref[CONV_HALO + t * tile:CONV_HALO + (t + 1) * tile, :] = zt[:, :CONV_CH] * _sigmoid(zt[:, CONV_CH:])


def _conv_windows(ref, views_ref, t, tile):
    for b in range(8):
        views_ref[b] = ref[t * tile + b:t * tile + b + tile + CONV_HALO, :]


def _conv_tap(views_ref, offset, tile):
    return views_ref[offset % 8, 8 * (offset // 8):8 * (offset // 8) + tile, :]


def _conv_tile(u0_ref, views_ref, w_ref, b_ref, t, tile):
    _conv_windows(u0_ref, views_ref, t, tile)
    acc = jnp.broadcast_to(b_ref[...], (tile, CONV_CH))
    for kk in range(CONV_WIDTH):
        acc = acc + w_ref[kk:kk + 1, :] * _conv_tap(views_ref, kk + CONV_HALO - (CONV_WIDTH - 1), tile)
    return acc


def _conv_fwd(zglu, conv_w, conv_b, ln_g, ln_b, n_b, seq):
    tile = min(256, seq)

    def body(z_ref, w_ref, b_ref, g_ref, bb_ref, o_ref, u1_ref, u0_ref, views_ref):
        _conv_fill_glu(z_ref, u0_ref, seq, tile)
        for t in range(seq // tile):
            u1 = _conv_tile(u0_ref, views_ref, w_ref, b_ref, t, tile)
            u1_ref[0, t * tile:(t + 1) * tile, :] = u1
            o_ref[0, t * tile:(t + 1) * tile, :] = _ln_silu(u1, g_ref[...], bb_ref[...]).astype(BF16)

    whole2 = lambda arr: pl.BlockSpec(arr.shape, lambda b: (0, 0))
    seq_spec = pl.BlockSpec((1, seq, CONV_CH), lambda b: (b, 0, 0))
    return pl.pallas_call(
        body, name="conv_fwd", grid=(n_b,),
        in_specs=[pl.BlockSpec((1, seq, 2 * CONV_CH), lambda b: (b, 0, 0)), whole2(conv_w), whole2(conv_b),
                  whole2(ln_g), whole2(ln_b)],
        out_specs=[seq_spec, seq_spec],
        out_shape=[jax.ShapeDtypeStruct((n_b, seq, CONV_CH), BF16), jax.ShapeDtypeStruct((n_b, seq, CONV_CH), F32)],
        scratch_shapes=[pltpu.VMEM((seq + CONV_HALO + CONV_TAIL, CONV_CH), F32),
                        pltpu.VMEM((8, tile + CONV_HALO, CONV_CH), F32)],
        compiler_params=_params(("parallel",)),
    )(zglu, conv_w, conv_b, ln_g, ln_b)


def _conv_bwd(zglu, u1_saved, du3, conv_w, ln_g, ln_b, n_b, seq):
    tile = min(256, seq)
    n_t = seq // tile

    def body(z_ref, u1_ref, du3_ref, w_ref, g_ref, bb_ref, dz_ref, dw_ref, db_ref, dg_ref, dbb_ref, u0_ref, du1_ref,
             u0_views, du1_views):
        @pl.when(pl.program_id(0) == 0)
        def _():
            for r in (dw_ref, db_ref, dg_ref, dbb_ref):
                r[...] = jnp.zeros_like(r)

        _conv_fill_glu(z_ref, u0_ref, seq, tile)
        du1_ref[seq:seq + CONV_HALO + CONV_TAIL, :] = jnp.zeros((CONV_HALO + CONV_TAIL, CONV_CH), F32)
        g = g_ref[...]
        for t in range(n_t):
            u1 = u1_ref[0, t * tile:(t + 1) * tile, :]
            mu = jnp.mean(u1, axis=-1, keepdims=True)
            uc = u1 - mu
            r = lax.rsqrt(jnp.mean(uc * uc, axis=-1, keepdims=True) + EPS)
            xh = uc * r
            y = xh * g + bb_ref[...]
            sg = _sigmoid(y)
            dy = du3_ref[0, t * tile:(t + 1) * tile, :].astype(F32) * (sg * (1.0 + y * (1.0 - sg)))
            dg_ref[...] += jnp.sum(dy * xh, axis=0, keepdims=True)
            dbb_ref[...] += jnp.sum(dy, axis=0, keepdims=True)
            dxh = dy * g
            du1 = r * (dxh - jnp.mean(dxh, axis=-1, keepdims=True) - xh * jnp.mean(dxh * xh, axis=-1, keepdims=True))
            db_ref[...] += jnp.sum(du1, axis=0, keepdims=True)
            du1_ref[t * tile:(t + 1) * tile, :] = du1
        for t in range(n_t):
            du1 = du1_ref[t * tile:(t + 1) * tile, :]
            du0 = jnp.zeros((tile, CONV_CH), F32)
            _conv_windows(u0_ref, u0_views, t, tile)
            _conv_windows(du1_ref, du1_views, t, tile)
            for kk in range(CONV_WIDTH):
                du0 = du0 + w_ref[kk:kk + 1, :] * _conv_tap(du1_views, CONV_WIDTH - 1 - kk, tile)
                u0_tap = _conv_tap(u0_views, kk + CONV_HALO - (CONV_WIDTH - 1), tile)
                dw_ref[kk:kk + 1, :] += jnp.sum(du1 * u0_tap, axis=0, keepdims=True)
            zt = z_ref[0, t * tile:(t + 1) * tile, :].astype(F32)
            ga, sb = zt[:, :CONV_CH], _sigmoid(zt[:, CONV_CH:])
            dz_ref[0, t * tile:(t + 1) * tile, :CONV_CH] = (du0 * sb).astype(BF16)
            dz_ref[0, t * tile:(t + 1) * tile, CONV_CH:] = (du0 * ga * sb * (1.0 - sb)).astype(BF16)

    whole2 = lambda arr: pl.BlockSpec(arr.shape, lambda b: (0, 0))
    z_spec = pl.BlockSpec((1, seq, 2 * CONV_CH), lambda b: (b, 0, 0))
    seq_spec = pl.BlockSpec((1, seq, CONV_CH), lambda b: (b, 0, 0))
    return pl.pallas_call(
        body, name="conv_bwd", grid=(n_b,),
        in_specs=[z_spec, seq_spec, seq_spec, whole2(conv_w), whole2(ln_g), whole2(ln_b)],
        out_specs=[z_spec, whole2(conv_w), whole2(ln_g), whole2(ln_g), whole2(ln_b)],
        out_shape=[jax.ShapeDtypeStruct((n_b, seq, 2 * CONV_CH), BF16), jax.ShapeDtypeStruct(conv_w.shape, F32),
                   jax.ShapeDtypeStruct(ln_g.shape, F32), jax.ShapeDtypeStruct(ln_g.shape, F32),
                   jax.ShapeDtypeStruct(ln_b.shape, F32)],
        scratch_shapes=[pltpu.VMEM((seq + CONV_HALO + CONV_TAIL, CONV_CH), F32)] * 2
        + [pltpu.VMEM((8, tile + CONV_HALO, CONV_CH), F32)] * 2,
        compiler_params=_params(("arbitrary",)),
    )(zglu, u1_saved, du3, conv_w, ln_g, ln_b)


def _sum_parts(name, parts):
    n_parts = parts.shape[0]

    def body(p_ref, o_ref):
        gg = p_ref[0].astype(F32)
        for j in range(1, n_parts):
            gg = gg + p_ref[j].astype(F32)
        o_ref[...] = gg

    return pl.pallas_call(body, name=name, out_shape=jax.ShapeDtypeStruct(parts.shape[1:], F32),
                          compiler_params=_params(None))(parts)


def _adamw(name, w, parts, m, v, transposed=False):
    n_parts = parts.shape[0]
    rows, cols = w.shape
    tr = ADAM_ROWS if rows % ADAM_ROWS == 0 else rows

    def body(w_ref, p_ref, m_ref, v_ref, g_ref, d_ref, nm_ref, nv_ref):
        gg = p_ref[0].astype(F32)
        for j in range(1, n_parts):
            gg = gg + p_ref[j].astype(F32)
        if transposed:
            gg = gg.T
        nm = ADAM_B1 * m_ref[...] + (1.0 - ADAM_B1) * gg
        nv = ADAM_B2 * v_ref[...] + (1.0 - ADAM_B2) * jnp.square(gg)
        m_hat = nm / (1.0 - ADAM_B1 ** ADAM_STEP)
        v_hat = nv / (1.0 - ADAM_B2 ** ADAM_STEP)
        g_ref[...] = gg
        d_ref[...] = -ADAM_LR * (m_hat / (jnp.sqrt(v_hat) + ADAM_EPS) + ADAM_WD * w_ref[...])
        nm_ref[...] = nm
        nv_ref[...] = nv

    shape = jax.ShapeDtypeStruct(w.shape, F32)
    blk = pl.BlockSpec((tr, cols), lambda i: (i, 0))
    p_spec = (pl.BlockSpec((n_parts, cols, tr), lambda i: (0, 0, i)) if transposed
              else pl.BlockSpec((n_parts, tr, cols), lambda i: (0, i, 0)))
    return pl.pallas_call(body, name=name, grid=(rows // tr,), in_specs=[blk, p_spec, blk, blk], out_specs=[blk] * 4,
                          out_shape=[shape] * 4, compiler_params=_params(("parallel",)))(w, parts, m, v)


def _rope_tables(seq):
    inv_freq = ROPE_THETA ** (-jnp.arange(0, QK_ROPE_DIM, 2, dtype=F32) / QK_ROPE_DIM)
    ang = jnp.arange(seq, dtype=F32)[:, None] * inv_freq[None, :]
    cos, sin = jnp.cos(ang), jnp.sin(ang)
    half = QK_ROPE_DIM // 2
    z = lambda n: jnp.zeros((seq, n), F32)
    tail = HEAD_PAD - QK_HEAD_DIM
    cos_t = jnp.concatenate([jnp.ones((seq, QK_NOPE_DIM), F32), cos, cos, z(tail)], axis=1)
    sin_lo = jnp.concatenate([z(QK_NOPE_DIM), -sin, z(half), z(tail)], axis=1)
    sin_hi = jnp.concatenate([z(QK_NOPE_DIM), z(half), sin, z(tail)], axis=1)
    return cos_t, sin_lo, sin_hi


def _pad_lanes(v, width=HEAD_PAD):
    return jnp.pad(v, [(0, 0)] * (v.ndim - 1) + [(0, width - v.shape[-1])])


def _unstack_cols(s):
    return s.transpose(1, 0, 2).reshape(s.shape[1], N_DEV * s.shape[2])


def _stack_cols(g, dtype):
    rows, cols = g.shape
    return g.reshape(rows, N_DEV, cols // N_DEV).transpose(1, 0, 2).astype(dtype)


def kernel(x, c, w_ada, b_ada, norm1_g, w_in, q_latent_g, w_uq, kv_latent_g, w_ukv, qk_norm_q_g, qk_norm_k_g, w_o_mla, conv_w, conv_b, conv_ln_g, conv_ln_b, w_pw_out, w_out, norm2_g, w_ff1, w_ff2, loss_target, m_w_ada, m_b_ada, m_norm1_g, m_w_in, m_q_latent_g, m_w_uq, m_kv_latent_g, m_w_ukv, m_qk_norm_q_g, m_qk_norm_k_g, m_w_o_mla, m_conv_w, m_conv_b, m_conv_ln_g, m_conv_ln_b, m_w_pw_out, m_w_out, m_norm2_g, m_w_ff1, m_w_ff2, v_w_ada, v_b_ada, v_norm1_g, v_w_in, v_q_latent_g, v_w_uq, v_kv_latent_g, v_w_ukv, v_qk_norm_q_g, v_qk_norm_k_g, v_w_o_mla, v_conv_w, v_conv_b, v_conv_ln_g, v_conv_ln_b, v_w_pw_out, v_w_out, v_norm2_g, v_w_ff1, v_w_ff2):
    given = dict(locals())
    local = {n: given[n][0] for n in WEIGHTS}
    vec = {n: local[n].reshape(1, -1) for n in REPLICATED}
    bf = lambda n: local[n].astype(BF16)
    n_b, seq, d = x.shape
    n_rows = n_b * seq
    x2 = x.reshape(n_rows, d)
    t2 = loss_target.reshape(n_rows, d)
    me = 4 * lax.axis_index("x") + 2 * lax.axis_index("y") + lax.axis_index("c")
    ada_cols = local["w_ada"].shape[1]

    tsh = lambda n: local[n].T.astype(BF16)
    c_all, w_in_s, w_uq_s, w_ukv_s, conv_w_s = _exchange(
        "gather_early", [(c, True), (tsh("w_in"), True), (bf("w_uq"), True), (bf("w_ukv"), True), (local["conv_w"], True)],
        by_chip=True)
    w_in_t = w_in_s.reshape(-1, d)
    zrows = lambda n: jnp.zeros((n, d), BF16)
    w_sm_t = jnp.concatenate([w_in_t[:OFF_KV], zrows(QK_NOPE_DIM), w_in_t[OFF_KV:OFF_KR], zrows(HEAD_PAD - QK_HEAD_DIM)], axis=0)
    w_glu_t = w_in_t[OFF_KR:OFF_GLU]
    w_gate_t = w_in_t[OFF_GLU:]
    wuq = _pad_lanes(_unstack_cols(w_uq_s).reshape(Q_LORA, N_HEADS, QK_HEAD_DIM)).reshape(Q_LORA, N_HEADS * HEAD_PAD)
    wukv_f = _unstack_cols(w_ukv_s).reshape(KV_LORA, N_HEADS, QK_NOPE_DIM + V_HEAD_DIM)
    wv = wukv_f[:, :, QK_NOPE_DIM:]
    odd = (jnp.arange(N_HEADS) % 2 == 1)[None, :, None]
    wuv = jnp.where(odd, jnp.pad(wv, ((0, 0), (0, 0), (V_HEAD_DIM, 0))), jnp.pad(wv, ((0, 0), (0, 0), (0, V_HEAD_DIM))))
    wukv = jnp.concatenate([_pad_lanes(wukv_f[:, :, :QK_NOPE_DIM]), wuv], axis=1).reshape(KV_LORA, 2 * N_HEADS * HEAD_PAD)
    gqn = _pad_lanes(vec["qk_norm_q_g"])
    gkn = _pad_lanes(vec["qk_norm_k_g"])
    conv_w_f = jnp.pad(_unstack_cols(conv_w_s), ((0, 1), (0, 0)))
    rope = _rope_tables(seq)

    all_rows = N_DEV * n_b
    pad_rows = (-all_rows) % ROWS_PAD
    c_rows = jnp.pad(c_all.reshape(all_rows, d), ((0, pad_rows), (0, 0)))
    b_cols = lax.dynamic_slice(local["b_ada"], (me * ada_cols,), (ada_cols,))
    mod_cols = _mm("ada_fwd", c_rows, local["w_ada"], "nn", F32, a_fn=_silu, epi=lambda acc, b: acc + b,
                   epi_in=(jnp.broadcast_to(b_cols, (all_rows + pad_rows, ada_cols)),))
    (mod_s,) = _exchange("scatter_mod", [(mod_cols[:all_rows].reshape(N_DEV, n_b, ada_cols), False)])
    mod = mod_s.transpose(1, 0, 2).reshape(n_b, ADA_CHUNKS, 1, d)
    shift1, scale1, gate1, shift2, scale2, gate2 = [mod[:, i] for i in range(ADA_CHUNKS)]

    h, zgate, zglu, zsm = _in_proj_fwd(x2, scale1, shift1, vec["norm1_g"], [w_gate_t, w_glu_t, w_sm_t],
                                       [BF16, BF16, F32], seq)
    q, k, v, kt = _mla_prep_fwd(zsm, wuq, wukv, vec["q_latent_g"], vec["kv_latent_g"], gqn, gkn, rope, n_b, seq)
    attn, lse, (w_o_s, w_pw_s, w_out_s, w_ff1_s, w_ff2_s) = _attn_fwd(
        q, k, v, [(tsh("w_o_mla"), True), (tsh("w_pw_out"), True), (bf("w_out"), True), (tsh("w_ff1"), True),
                  (bf("w_ff2"), True)], n_b, seq)
    w_o_t = w_o_s.reshape(d, -1)
    w_pw_t = w_pw_s.reshape(d, -1)
    w_out_f = w_out_s.reshape(d, d)
    w_ff1_t = w_ff1_s.reshape(-1, d)
    w_ff2_f = w_ff2_s.reshape(-1, d)
    attn2 = attn.reshape(n_rows, N_HEADS * V_HEAD_DIM)
    u3, u1 = _conv_fwd(zglu.reshape(n_b, seq, 2 * CONV_CH), conv_w_f, vec["conv_b"], vec["conv_ln_g"], vec["conv_ln_b"], n_b, seq)
    u32 = u3.reshape(n_rows, CONV_CH)
    ya = _mm("mla_out", attn2, w_o_t, "nt", BF16)
    yb = _mm("conv_out", u32, w_pw_t, "nt", BF16)
    mmr = functools.partial(_mm_rows, n_rows=n_rows, seq=seq)

    def merge_fn(t):
        return _sigmoid(t[0]) * t[2] + _sigmoid(t[1]) * t[3]

    def mid_fn(acc, r, b, cc):
        x1_ = r[0] + b[0] * acc
        return [acc, x1_, _norm_mod(x1_, cc[0], b[1], b[2])], [], []

    mrg, mixed, x1, h2 = mmr("out_proj", [(zgate, d, 0), (zgate, d, 1), (ya, d, 0), (yb, d, 0)], merge_fn, w_out_f, "nn",
                             mid_fn, rows=[_full(x2)], bats=[gate1, scale2, shift2], consts=[vec["norm2_g"]],
                             outs=[(d, BF16), (d, F32), (d, BF16)], a_out=BF16)

    a = _mm("ff1", h2, w_ff1_t, "nt", BF16)

    def loss_fn(ff, r, b, cc):
        err = r[0] + b[0] * ff - r[1]
        dy_ = err * (1.0 / d)
        sq = jnp.broadcast_to(jnp.sum(err * err, keepdims=True), (1, LANES))
        return [dy_, b[0] * dy_], [jnp.sum(dy_ * ff, axis=0, keepdims=True)], [sq]

    dy, df, dgate2, sq_err = mmr("ff2_loss", [(a, a.shape[1], 0)], lambda t: _relu2(t[0]), w_ff2_f, "nn", loss_fn,
                                 rows=[_full(x1), _full(t2)], bats=[gate2], outs=[(d, F32), (d, BF16)], bat_outs=[d],
                                 tot_outs=[(1, LANES)], tm=1024)

    da = _mm("ff2_bwd", df, w_ff2_f, "nt", BF16, epi=lambda acc, av: acc * 2.0 * jnp.maximum(av, 0.0), epi_in=(a,))
    g_ff2 = _mm("ff2_dw", a, df, "tn", BF16, a_fn=_relu2)
    g_ff1_t = _mm("ff1_dw", da, h2, "tn", BF16)

    def mid_bwd(dh2_, r, b, cc):
        dx, dsc, dsh, dg = _norm_mod_bwd(r[0], cc[0], b[0], dh2_)
        dx1_ = r[1] + dx
        return [dx1_, b[1] * dx1_], [dsc, dsh, jnp.sum(dx1_ * r[2].astype(F32), axis=0, keepdims=True)], [dg]

    dx1, dmixed, dscale2, dshift2, dgate1, g_norm2 = mmr(
        "ff1_bwd", [(da, da.shape[1], 0)], None, w_ff1_t, "nn", mid_bwd, rows=[_full(x1), _full(dy), _full(mixed)],
        bats=[scale2, gate1], consts=[vec["norm2_g"]], outs=[(d, F32), (d, BF16)], bat_outs=[d, d, d],
        tot_outs=[(1, d)], tm=1024)

    g_out = _mm("out_proj_dw", mrg, dmixed, "tn", BF16)

    def merge_bwd(dm, r, b, cc):
        ya_, yb_ = r[2].astype(F32), r[3].astype(F32)
        sa, sb = _sigmoid(r[0].astype(F32)), _sigmoid(r[1].astype(F32))
        return [dm * ya_ * sa * (1.0 - sa), dm * yb_ * sb * (1.0 - sb), dm * sa, dm * sb], [], []

    dzga, dzgb, dya, dyb = mmr("out_proj_bwd", [(dmixed, d, 0)], None, w_out_f, "nt", merge_bwd,
                               rows=[(zgate, d, 0), (zgate, d, 1), _full(ya), _full(yb)], outs=[(d, BF16)] * 4)
    dattn = _mm("mla_out_bwd", dya, w_o_t, "nn", BF16)
    g_o_t = _mm("mla_out_dw", dya, attn2, "tn", BF16)
    du3 = _mm("conv_out_bwd", dyb, w_pw_t, "nn", BF16)
    g_pw_t = _mm("conv_out_dw", dyb, u32, "tn", BF16)

    dzglu, g_conv_w, g_conv_b, g_ln_g, g_ln_b = _conv_bwd(
        zglu.reshape(n_b, seq, 2 * CONV_CH), u1, du3.reshape(n_b, seq, CONV_CH), conv_w_f, vec["conv_ln_g"],
        vec["conv_ln_b"], n_b, seq)
    dzglu = dzglu.reshape(n_rows, 2 * CONV_CH)

    dq, dk, dv, (p_ff2, p_ff1, p_out, p_pw, p_o) = _attn_bwd(
        q, k, v, kt, dattn.reshape(n_b, seq, N_HEADS * V_HEAD_DIM), attn, lse,
        [(g_ff2.reshape(N_DEV, -1, d), False), (g_ff1_t.reshape(N_DEV, -1, d), False), (g_out.reshape(N_DEV, -1, d), False),
         (g_pw_t.reshape(N_DEV, -1, CONV_CH), False), (g_o_t.reshape(N_DEV, -1, N_HEADS * V_HEAD_DIM), False)], n_b, seq)
    dzsm, g_wuq, g_wukv, g_gq, g_gkv, g_gqn, g_gkn = _mla_prep_bwd(
        zsm, dq, dk, dv, wuq, wukv, vec["q_latent_g"], vec["kv_latent_g"], gqn, gkn, rope, n_b, seq)

    g_gate_a_t = _mm("in_proj_gate_dw_a", dzga, h, "tn", BF16)
    g_gate_b_t = _mm("in_proj_gate_dw_b", dzgb, h, "tn", BF16)
    g_glu_t = _mm("in_proj_glu_dw", dzglu, h, "tn", BF16)
    g_sm_t = _mm("in_proj_sm_dw", dzsm, h, "tn", BF16)
    g_in_t = jnp.concatenate([g_sm_t[:OFF_KV], g_sm_t[OFF_KV + QK_NOPE_DIM:OFF_KV + QK_NOPE_DIM + QK_ROPE_DIM], g_glu_t,
                              g_gate_a_t, g_gate_b_t], axis=0)
    g_uq = g_wuq.reshape(Q_LORA, N_HEADS, HEAD_PAD)[:, :, :QK_HEAD_DIM].reshape(Q_LORA, N_HEADS * QK_HEAD_DIM)
    g_wukv = g_wukv.reshape(KV_LORA, 2, N_HEADS, HEAD_PAD)
    g_v = jnp.where(odd, g_wukv[:, 1, :, V_HEAD_DIM:], g_wukv[:, 1, :, :V_HEAD_DIM])
    g_ukv = jnp.concatenate([g_wukv[:, 0, :, :QK_NOPE_DIM], g_v], axis=2).reshape(KV_LORA, -1)

    grad_x, dscale1, dshift1, g_norm1, (p_in, p_uq, p_ukv, p_conv_w) = _in_proj_bwd(
        [(dzga, w_gate_t[:d]), (dzgb, w_gate_t[d:]), (dzglu, w_glu_t), (dzsm, w_sm_t)], x2, dx1, scale1, vec["norm1_g"],
        [(g_in_t.reshape(N_DEV, -1, d), False), (_stack_cols(g_uq, BF16), False), (_stack_cols(g_ukv, BF16), False),
         (_stack_cols(g_conv_w[:CONV_WIDTH], F32), False)], seq)

    dmod = jnp.concatenate([dshift1, dscale1, dgate1, dshift2, dscale2, dgate2], axis=1).reshape(n_b, N_DEV, ada_cols)
    (dmod_s,) = _exchange("scatter_dmod", [(dmod.transpose(1, 0, 2), False)])
    dmod_rows = jnp.pad(dmod_s.reshape(all_rows, ada_cols), ((0, pad_rows), (0, 0)))
    g_ada = _mm("ada_dw", c_rows, dmod_rows, "tn", F32, a_fn=_silu)
    (g_b_cols,) = _rowwise("ada_db", lambda r, b, cc: ([], [], [jnp.sum(r[0], axis=0, keepdims=True)]),
                           all_rows + pad_rows, all_rows + pad_rows, rows=[_full(dmod_rows)], tot_outs=[(1, ada_cols)])

    partial_of = {"norm1_g": g_norm1, "q_latent_g": g_gq, "kv_latent_g": g_gkv, "qk_norm_q_g": g_gqn,
                  "qk_norm_k_g": g_gkn, "conv_b": g_conv_b, "conv_ln_g": g_ln_g, "conv_ln_b": g_ln_b, "norm2_g": g_norm2}
    names = [n for n in REPLICATED if n != "b_ada"]
    pieces = [_pad_lanes(partial_of[n], -(-partial_of[n].shape[1] // LANES) * LANES) for n in names] + [g_b_cols, sq_err]
    widths = [p.shape[1] for p in pieces]
    small = jnp.concatenate(pieces, axis=1)
    small = _pad_lanes(small, -(-small.shape[1] // (8 * LANES)) * 8 * LANES).reshape(-1, LANES)
    (small_s,) = _exchange("gather_small_grads", [(small, True)])
    small_s = small_s.reshape(N_DEV, 1, -1)
    parts = {}
    off = 0
    for n, wd in zip(names, widths):
        parts[n] = small_s[:, :, off:off + vec[n].shape[1]]
        off += wd
    parts["b_ada"] = small_s[:, 0, off:off + ada_cols].reshape(1, 1, N_DEV * ada_cols)
    loss = jnp.sum(small_s[:, 0, off + ada_cols]) * (0.5 / d)
    g_in_mine = _sum_parts("sum_w_in", p_in).T
    parts.update({"w_ada": g_ada[None], "w_in": g_in_mine[None], "w_uq": p_uq, "w_ukv": p_ukv, "w_o_mla": p_o,
                  "conv_w": p_conv_w, "w_pw_out": p_pw, "w_out": p_out, "w_ff1": p_ff1, "w_ff2": p_ff2})
    transposed = ("w_o_mla", "w_pw_out", "w_ff1")

    grad_out, delta_out, m_out, v_out = [], [], [], []
    for n in WEIGHTS:
        shape2 = local[n].shape if local[n].ndim == 2 else (1, local[n].shape[0])
        g_w, d_w, n_m, n_v = _adamw("adamw_" + n, local[n].reshape(shape2), parts[n], given["m_" + n].reshape(shape2),
                                    given["v_" + n].reshape(shape2), transposed=n in transposed)
        full_shape = given[n].shape
        grad_out.append(g_w.reshape(full_shape))
        delta_out.append(d_w.reshape(full_shape))
        m_out.append(n_m.reshape(full_shape))
        v_out.append(n_v.reshape(full_shape))
    return (loss, grad_x.reshape(n_b, seq, d), *grad_out, *delta_out, *m_out, *v_out)
```

```python
import functools

import jax
import jax.numpy as jnp
from jax import lax
from jax.experimental import pallas as pl
from jax.experimental.pallas import tpu as pltpu

F32 = jnp.float32
BF16 = jnp.bfloat16

N_DEV = 8
EPS = 1e-6
N_HEADS = 8
QK_HEAD_DIM = 96
QK_NOPE_DIM = 64
QK_ROPE_DIM = 32
V_HEAD_DIM = 64
HEAD_PAD = 128
Q_LORA = 256
KV_LORA = 128
CONV_CH = 512
CONV_WIDTH = 31
CONV_HALO = 32
CONV_TAIL = 8
CHUNK = 64
ROPE_THETA = 10000.0
OFF_Q = Q_LORA
OFF_KV = OFF_Q + KV_LORA
OFF_KR = OFF_KV + QK_ROPE_DIM
OFF_GLU = OFF_KR + 2 * CONV_CH
ADA_CHUNKS = 6
ADAM_LR = 0.001
ADAM_B1 = 0.9
ADAM_B2 = 0.999
ADAM_EPS = 1e-08
ADAM_WD = 0.01
ADAM_STEP = 10
LANES = 128
VMEM_LIMIT = 56 * 1024 * 1024
NEG_BIG = -1e30
ATT_HEADS = 4
ATT_TILE = 512
PREP_TILE = 1024
ATT_SCALE = QK_HEAD_DIM ** -0.5
LOG2E = 1.4426950408889634
LN2 = 0.6931471805599453
QK_SCALE = ATT_SCALE * LOG2E
ADAM_ROWS = 256
ROWS_PAD = 16

REPLICATED = ("b_ada", "norm1_g", "q_latent_g", "kv_latent_g", "qk_norm_q_g", "qk_norm_k_g", "conv_b", "conv_ln_g",
              "conv_ln_b", "norm2_g")
WEIGHTS = ("w_ada", "b_ada", "norm1_g", "w_in", "q_latent_g", "w_uq", "kv_latent_g", "w_ukv", "qk_norm_q_g",
           "qk_norm_k_g", "w_o_mla", "conv_w", "conv_b", "conv_ln_g", "conv_ln_b", "w_pw_out", "w_out", "norm2_g",
           "w_ff1", "w_ff2")


def _tile(dim, pref):
    if dim <= pref:
        return dim
    t = (pref // LANES) * LANES
    while dim % t:
        t -= LANES
    return t


def _params(semantics):
    return pltpu.CompilerParams(dimension_semantics=semantics, vmem_limit_bytes=VMEM_LIMIT)


def _sigmoid(v):
    return 1.0 / (1.0 + jnp.exp(-v))


def _silu(v):
    return v * _sigmoid(v)


def _relu2(v):
    return jnp.square(jnp.maximum(v, 0.0))


_DIMS = {"nn": (((1,), (0,)), ((), ())), "nt": (((1,), (1,)), ((), ())), "tn": (((0,), (0,)), ((), ()))}


def _mm(name, a, b, mode, out_dtype, *, a_fn=None, epi=None, epi_in=(), tm=1024, tn=1024, tk=1024):
    if mode == "nn":
        (m, k), n = a.shape, b.shape[1]
    elif mode == "nt":
        (m, k), n = a.shape, b.shape[0]
    else:
        (k, m), n = a.shape, b.shape[1]
    tm, tn, tk = _tile(m, tm), _tile(n, tn), _tile(k, tk)
    nk = k // tk
    a_spec = (pl.BlockSpec((tk, tm), lambda i, j, kk: (kk, i)) if mode == "tn"
              else pl.BlockSpec((tm, tk), lambda i, j, kk: (i, kk)))
    b_spec = (pl.BlockSpec((tn, tk), lambda i, j, kk: (j, kk)) if mode == "nt"
              else pl.BlockSpec((tk, tn), lambda i, j, kk: (kk, j)))
    o_spec = e_spec = pl.BlockSpec((tm, tn), lambda i, j, kk: (i, j))
    out_shape = jax.ShapeDtypeStruct((m, n), out_dtype)
    n_epi = len(epi_in)

    def body(a_ref, b_ref, *rest):
        epi_refs, o_ref, acc_ref = rest[:n_epi], rest[n_epi], rest[n_epi + 1]
        kk = pl.program_id(2)

        @pl.when(kk == 0)
        def _():
            acc_ref[...] = jnp.zeros_like(acc_ref)

        av = a_ref[...]
        if a_fn is not None:
            av = a_fn(av.astype(F32))
        acc_ref[...] += lax.dot_general(av.astype(BF16), b_ref[...].astype(BF16), _DIMS[mode],
                                        preferred_element_type=F32)

        @pl.when(kk == nk - 1)
        def _():
            acc = acc_ref[...]
            if epi is not None:
                acc = epi(acc, *[r[...].astype(F32) for r in epi_refs])
            o_ref[...] = acc.astype(out_dtype)

    return pl.pallas_call(
        body, name=name, grid=(m // tm, n // tn, nk),
        in_specs=[a_spec, b_spec] + [e_spec] * n_epi, out_specs=o_spec, out_shape=out_shape,
        scratch_shapes=[pltpu.VMEM((tm, tn), F32)],
        compiler_params=_params(("parallel", "parallel", "arbitrary")),
    )(a, b, *epi_in)


def _rowwise(name, fn, n_rows, seq, rows, bats=(), consts=(), outs=(), bat_outs=(), tot_outs=(), tm=256):
    tm = min(tm, seq)
    per_seq = seq // tm
    n_b = n_rows // seq
    nr, nb, nc, no, nbo, nto = len(rows), len(bats), len(consts), len(outs), len(bat_outs), len(tot_outs)

    def body(*refs):
        i = pl.program_id(0)
        r_in = [r[...] for r in refs[:nr]]
        b_in = [r[0] for r in refs[nr:nr + nb]]
        c_in = [r[...] for r in refs[nr + nb:nr + nb + nc]]
        o_refs = refs[nr + nb + nc:nr + nb + nc + no]
        bo_refs = refs[nr + nb + nc + no:nr + nb + nc + no + nbo]
        to_refs = refs[nr + nb + nc + no + nbo:]
        o_val, bo_val, to_val = fn(r_in, b_in, c_in)
        for r, v in zip(o_refs, o_val):
            r[...] = v.astype(r.dtype)
        if nbo:
            @pl.when(i % per_seq == 0)
            def _():
                for r in bo_refs:
                    r[...] = jnp.zeros_like(r)

            for r, v in zip(bo_refs, bo_val):
                r[0] += v
        if nto:
            @pl.when(i == 0)
            def _():
                for r in to_refs:
                    r[...] = jnp.zeros_like(r)

            for r, v in zip(to_refs, to_val):
                r[...] += v

    in_specs = [pl.BlockSpec((tm, w), functools.partial(lambda cb, i: (i, cb), cb)) for (_, w, cb) in rows]
    in_specs += [pl.BlockSpec((1, 1, bt.shape[2]), lambda i: (i // per_seq, 0, 0)) for bt in bats]
    in_specs += [pl.BlockSpec(ct.shape, lambda i: (0, 0)) for ct in consts]
    out_specs = [pl.BlockSpec((tm, w), lambda i: (i, 0)) for (w, _) in outs]
    out_specs += [pl.BlockSpec((1, 1, w), lambda i: (i // per_seq, 0, 0)) for w in bat_outs]
    out_specs += [pl.BlockSpec(shp, lambda i: (0, 0)) for shp in tot_outs]
    out_shape = [jax.ShapeDtypeStruct((n_rows, w), dt) for (w, dt) in outs]
    out_shape += [jax.ShapeDtypeStruct((n_b, 1, w), F32) for w in bat_outs]
    out_shape += [jax.ShapeDtypeStruct(shp, F32) for shp in tot_outs]
    res = pl.pallas_call(
        body, name=name, grid=(n_rows // tm,), in_specs=in_specs, out_specs=out_specs, out_shape=out_shape,
        compiler_params=_params(("arbitrary",)),
    )(*[r[0] for r in rows], *bats, *consts)
    return res


def _full(arr):
    return (arr, arr.shape[1], 0)


def _mm_rows(name, a_rows, a_fn, w, mode, fn, n_rows, seq, rows=(), bats=(), consts=(), outs=(), bat_outs=(),
             tot_outs=(), a_out=None, tm=512, tk=1024):
    tm = min(tm, seq)
    per_seq = seq // tm
    n_b = n_rows // seq
    k = a_rows[0][1]
    if mode == "nt":
        n_out, tk = w.shape[0], _tile(k, tk)
        w_spec = pl.BlockSpec((n_out, tk), lambda i, kk: (0, kk))
    else:
        n_out, tk = w.shape[1], _tile(k, tk)
        w_spec = pl.BlockSpec((tk, n_out), lambda i, kk: (kk, 0))
    nk = k // tk
    na, nr, nb, nc = len(a_rows), len(rows), len(bats), len(consts)
    n_extra = 0 if a_out is None else 1
    no, nbo, nto = len(outs), len(bat_outs), len(tot_outs)

    def body(*refs):
        i, kk = pl.program_id(0), pl.program_id(1)
        a_refs, w_ref = refs[:na], refs[na]
        pos = na + 1
        r_refs, b_refs, c_refs = refs[pos:pos + nr], refs[pos + nr:pos + nr + nb], refs[pos + nr + nb:pos + nr + nb + nc]
        pos += nr + nb + nc
        ao_refs = refs[pos:pos + n_extra]
        pos += n_extra
        o_refs, bo_refs, to_refs = refs[pos:pos + no], refs[pos + no:pos + no + nbo], refs[pos + no + nbo:pos + no + nbo + nto]
        acc_ref = refs[pos + no + nbo + nto]

        @pl.when(kk == 0)
        def _():
            acc_ref[...] = jnp.zeros_like(acc_ref)

        tiles = [r[...] for r in a_refs]
        av = a_fn([t.astype(F32) for t in tiles]) if a_fn is not None else tiles[0]
        av = av.astype(BF16)
        if n_extra:
            ao_refs[0][...] = av.astype(ao_refs[0].dtype)
        acc_ref[...] += lax.dot_general(av, w_ref[...].astype(BF16), _DIMS[mode], preferred_element_type=F32)

        @pl.when(kk == nk - 1)
        def _():
            o_val, bo_val, to_val = fn(acc_ref[...], [r[...] for r in r_refs], [r[0] for r in b_refs],
                                       [r[...] for r in c_refs])
            for r, v in zip(o_refs, o_val):
                r[...] = v.astype(r.dtype)
            if nbo:
                @pl.when(i % per_seq == 0)
                def _():
                    for r in bo_refs:
                        r[...] = jnp.zeros_like(r)

                for r, v in zip(bo_refs, bo_val):
                    r[0] += v
            if nto:
                @pl.when(i == 0)
                def _():
                    for r in to_refs:
                        r[...] = jnp.zeros_like(r)

                for r, v in zip(to_refs, to_val):
                    r[...] += v

    in_specs = [pl.BlockSpec((tm, tk), functools.partial(lambda cb, i, kk: (i, kk + cb), cb)) for (_, _, cb) in a_rows]
    in_specs += [w_spec]
    in_specs += [pl.BlockSpec((tm, wd), functools.partial(lambda cb, i, kk: (i, cb), cb)) for (_, wd, cb) in rows]
    in_specs += [pl.BlockSpec((1, 1, bt.shape[2]), lambda i, kk: (i // per_seq, 0, 0)) for bt in bats]
    in_specs += [pl.BlockSpec(ct.shape, lambda i, kk: (0, 0)) for ct in consts]
    out_specs = [pl.BlockSpec((tm, tk), lambda i, kk: (i, kk))] * n_extra
    out_specs += [pl.BlockSpec((tm, wd), lambda i, kk: (i, 0)) for (wd, _) in outs]
    out_specs += [pl.BlockSpec((1, 1, wd), lambda i, kk: (i // per_seq, 0, 0)) for wd in bat_outs]
    out_specs += [pl.BlockSpec(shp, lambda i, kk: (0, 0)) for shp in tot_outs]
    out_shape = [jax.ShapeDtypeStruct((n_rows, k), a_out)] if n_extra else []
    out_shape += [jax.ShapeDtypeStruct((n_rows, wd), dt) for (wd, dt) in outs]
    out_shape += [jax.ShapeDtypeStruct((n_b, 1, wd), F32) for wd in bat_outs]
    out_shape += [jax.ShapeDtypeStruct(shp, F32) for shp in tot_outs]
    return pl.pallas_call(
        body, name=name, grid=(n_rows // tm, nk), in_specs=in_specs, out_specs=out_specs, out_shape=out_shape,
        scratch_shapes=[pltpu.VMEM((tm, n_out), F32)],
        compiler_params=_params(("arbitrary", "arbitrary")),
    )(*[a for a, _, _ in a_rows], w, *[r[0] for r in rows], *bats, *consts)


def _norm_mod(x, g, scale, shift):
    r = lax.rsqrt(jnp.mean(x * x, axis=-1, keepdims=True) + EPS)
    xh = x * r
    return xh * g * (1.0 + scale) + shift


def _norm_mod_bwd(x, g, scale, dh):
    r = lax.rsqrt(jnp.mean(x * x, axis=-1, keepdims=True) + EPS)
    xh = x * r
    dn = dh * (1.0 + scale)
    dxh = dn * g
    dx = r * (dxh - xh * jnp.mean(dxh * xh, axis=-1, keepdims=True))
    dscale = jnp.sum(dh * xh * g, axis=0, keepdims=True)
    dshift = jnp.sum(dh, axis=0, keepdims=True)
    dg = jnp.sum(dn * xh, axis=0, keepdims=True)
    return dx, dscale, dshift, dg


def _rms(v, g):
    r = lax.rsqrt(jnp.mean(v * v, axis=-1, keepdims=True) + EPS)
    return v * r * g


def _rms_bwd(v, g, dy):
    r = lax.rsqrt(jnp.mean(v * v, axis=-1, keepdims=True) + EPS)
    vh = v * r
    dvh = dy * g
    dv = r * (dvh - vh * jnp.mean(dvh * vh, axis=-1, keepdims=True))
    return dv, jnp.sum(dy * vh, axis=0, keepdims=True)


def _head_norm(v, g):
    r = lax.rsqrt(jnp.sum(v * v, axis=-1, keepdims=True) * (1.0 / QK_HEAD_DIM) + EPS)
    return v * r * g


def _head_norm_bwd(v, g, dy):
    r = lax.rsqrt(jnp.sum(v * v, axis=-1, keepdims=True) * (1.0 / QK_HEAD_DIM) + EPS)
    vh = v * r
    dvh = dy * g
    dv = r * (dvh - vh * (jnp.sum(dvh * vh, axis=-1, keepdims=True) * (1.0 / QK_HEAD_DIM)))
    return dv, jnp.sum(dy * vh, axis=0, keepdims=True)


def _rope(v, cos, sin_lo, sin_hi):
    return v * cos + pltpu.roll(v, HEAD_PAD - 16, 1) * sin_lo + pltpu.roll(v, 16, 1) * sin_hi


def _rope_bwd(g, cos, sin_lo, sin_hi):
    return g * cos + pltpu.roll(g * sin_lo, 16, 1) + pltpu.roll(g * sin_hi, HEAD_PAD - 16, 1)


def _mla_prep_fwd(zsm, wuq, wukv, gq, gkv, gqn, gkn, rope, n_b, seq):
    n_rows = n_b * seq
    tm = min(PREP_TILE, seq)
    per_seq = seq // tm
    att_tile = min(ATT_TILE, seq)
    k_cols = N_HEADS * HEAD_PAD

    def body(z_ref, wuq_ref, wukv_ref, gq_ref, gkv_ref, gqn_ref, gkn_ref, c_ref, s1_ref, s2_ref,
             q_ref, k_ref, v_ref, kt_ref):
        z = z_ref[...]
        qn = _rms(z[:, :Q_LORA], gq_ref[...]).astype(BF16)
        kvn = _rms(z[:, Q_LORA:Q_LORA + KV_LORA], gkv_ref[...]).astype(BF16)
        krp = z[:, Q_LORA + KV_LORA:]
        cos, s1, s2 = c_ref[...], s1_ref[...], s2_ref[...]
        q_all = jnp.dot(qn, wuq_ref[...], preferred_element_type=F32)
        kv_all = jnp.dot(kvn, wukv_ref[...], preferred_element_type=F32)
        for h in range(N_HEADS):
            cols = slice(h * HEAD_PAD, (h + 1) * HEAD_PAD)
            q_ref[0, h] = (_rope(_head_norm(q_all[:, cols], gqn_ref[...]), cos, s1, s2) * QK_SCALE).astype(BF16)
            kh = _rope(_head_norm(kv_all[:, cols] + krp, gkn_ref[...]), cos, s1, s2)
            k_ref[0, h] = kh.astype(BF16)
            for part in range(tm // att_tile):
                kt_ref[0, h, part] = kh[part * att_tile:(part + 1) * att_tile].T.astype(BF16)
            v_ref[0, h] = kv_all[:, k_cols + h * HEAD_PAD:k_cols + (h + 1) * HEAD_PAD].astype(BF16)

    whole2 = lambda arr: pl.BlockSpec(arr.shape, lambda i: (0, 0))
    rope_spec = pl.BlockSpec((tm, HEAD_PAD), lambda i: (i % per_seq, 0))
    head_spec = pl.BlockSpec((1, N_HEADS, tm, HEAD_PAD), lambda i: (i // per_seq, 0, i % per_seq, 0))
    head_shape = jax.ShapeDtypeStruct((n_b, N_HEADS, seq, HEAD_PAD), BF16)
    t_spec = pl.BlockSpec((1, N_HEADS, tm // att_tile, HEAD_PAD, att_tile), lambda i: (i // per_seq, 0, i % per_seq, 0, 0))
    t_shape = jax.ShapeDtypeStruct((n_b, N_HEADS, seq // att_tile, HEAD_PAD, att_tile), BF16)
    return pl.pallas_call(
        body, name="mla_prep_fwd", grid=(n_rows // tm,),
        in_specs=[pl.BlockSpec((tm, 512), lambda i: (i, 0)), whole2(wuq), whole2(wukv),
                  whole2(gq), whole2(gkv), whole2(gqn), whole2(gkn), rope_spec, rope_spec, rope_spec],
        out_specs=[head_spec] * 3 + [t_spec], out_shape=[head_shape] * 3 + [t_shape],
        compiler_params=_params(("parallel",)),
    )(zsm, wuq, wukv, gq, gkv, gqn, gkn, *rope)


def _mla_prep_bwd(zsm, dq, dk, dv, wuq, wukv, gq, gkv, gqn, gkn, rope, n_b, seq):
    n_rows = n_b * seq
    tm = min(PREP_TILE, seq)
    per_seq = seq // tm
    tn_dims = _DIMS["tn"]
    nt_dims = _DIMS["nt"]
    k_cols = N_HEADS * HEAD_PAD

    def body(z_ref, dq_ref, dk_ref, dv_ref, wuq_ref, wukv_ref, gq_ref, gkv_ref, gqn_ref, gkn_ref,
             c_ref, s1_ref, s2_ref, dz_ref, dwuq_ref, dwukv_ref, dgq_ref, dgkv_ref, dgqn_ref, dgkn_ref):
        @pl.when(pl.program_id(0) == 0)
        def _():
            for r in (dwuq_ref, dwukv_ref, dgq_ref, dgkv_ref, dgqn_ref, dgkn_ref):
                r[...] = jnp.zeros_like(r)

        z = z_ref[...]
        zq, zkv, krp = z[:, :Q_LORA], z[:, Q_LORA:Q_LORA + KV_LORA], z[:, Q_LORA + KV_LORA:]
        qn = _rms(zq, gq_ref[...]).astype(BF16)
        kvn = _rms(zkv, gkv_ref[...]).astype(BF16)
        cos, s1, s2 = c_ref[...], s1_ref[...], s2_ref[...]
        lane = lax.broadcasted_iota(jnp.int32, (tm, HEAD_PAD), 1)
        rope_lanes = (lane >= QK_NOPE_DIM) & (lane < QK_HEAD_DIM)
        q_all = jnp.dot(qn, wuq_ref[...], preferred_element_type=F32)
        k_all = jnp.dot(kvn, wukv_ref[:, :k_cols], preferred_element_type=F32)
        dkrp = jnp.zeros((tm, HEAD_PAD), F32)
        dgqn = jnp.zeros((1, HEAD_PAD), F32)
        dgkn = jnp.zeros((1, HEAD_PAD), F32)
        dq_heads, dk_heads = [], []
        for h in range(N_HEADS):
            cols = slice(h * HEAD_PAD, (h + 1) * HEAD_PAD)
            dqh, dg = _head_norm_bwd(q_all[:, cols], gqn_ref[...],
                                     _rope_bwd(dq_ref[0, h].astype(F32) * ATT_SCALE, cos, s1, s2))
            dgqn += dg
            dq_heads.append(dqh.astype(BF16))
            dkh, dg = _head_norm_bwd(k_all[:, cols] + krp, gkn_ref[...], _rope_bwd(dk_ref[0, h].astype(F32), cos, s1, s2))
            dgkn += dg
            dkrp += jnp.where(rope_lanes, dkh, 0.0)
            dk_heads.append(dkh.astype(BF16))
        dq_all = jnp.concatenate(dq_heads, axis=1)
        dkv_all = jnp.concatenate(dk_heads + [dv_ref[0, h] for h in range(N_HEADS)], axis=1)
        dwuq_ref[...] += lax.dot_general(qn, dq_all, tn_dims, preferred_element_type=F32)
        dqn = lax.dot_general(dq_all, wuq_ref[...], nt_dims, preferred_element_type=F32)
        dwukv_ref[...] += lax.dot_general(kvn, dkv_all, tn_dims, preferred_element_type=F32)
        dkvn = lax.dot_general(dkv_all, wukv_ref[...], nt_dims, preferred_element_type=F32)
        dzq, dg = _rms_bwd(zq, gq_ref[...], dqn)
        dgq_ref[...] += dg
        dzkv, dg = _rms_bwd(zkv, gkv_ref[...], dkvn)
        dgkv_ref[...] += dg
        dgqn_ref[...] += dgqn
        dgkn_ref[...] += dgkn
        dz_ref[:, :Q_LORA] = dzq.astype(dz_ref.dtype)
        dz_ref[:, Q_LORA:Q_LORA + KV_LORA] = dzkv.astype(dz_ref.dtype)
        dz_ref[:, Q_LORA + KV_LORA:] = dkrp.astype(dz_ref.dtype)

    whole2 = lambda arr: pl.BlockSpec(arr.shape, lambda i: (0, 0))
    rope_spec = pl.BlockSpec((tm, HEAD_PAD), lambda i: (i % per_seq, 0))
    head_spec = pl.BlockSpec((1, N_HEADS, tm, HEAD_PAD), lambda i: (i // per_seq, 0, i % per_seq, 0))
    row_spec = pl.BlockSpec((tm, 512), lambda i: (i, 0))
    return pl.pallas_call(
        body, name="mla_prep_bwd", grid=(n_rows // tm,),
        in_specs=[row_spec, head_spec, head_spec, head_spec, whole2(wuq), whole2(wukv),
                  whole2(gq), whole2(gkv), whole2(gqn), whole2(gkn), rope_spec, rope_spec, rope_spec],
        out_specs=[row_spec, whole2(wuq), whole2(wukv), whole2(gq), whole2(gkv), whole2(gqn), whole2(gkn)],
        out_shape=[jax.ShapeDtypeStruct((n_rows, 512), BF16),
                   jax.ShapeDtypeStruct(wuq.shape, F32), jax.ShapeDtypeStruct(wukv.shape, F32),
                   jax.ShapeDtypeStruct(gq.shape, F32), jax.ShapeDtypeStruct(gkv.shape, F32),
                   jax.ShapeDtypeStruct(gqn.shape, F32), jax.ShapeDtypeStruct(gkn.shape, F32)],
        compiler_params=_params(("arbitrary",)),
    )(zsm, dq, dk, dv, wuq, wukv, gq, gkv, gqn, gkn, *rope)


HBM_SPEC = pl.BlockSpec(memory_space=pltpu.HBM)


def _xchg_out_shapes(bufs):
    return [jax.ShapeDtypeStruct((N_DEV,) + (a.shape if gather else a.shape[1:]), a.dtype) for a, gather in bufs]


def _xchg_scratch(n_buf):
    return [pltpu.SemaphoreType.DMA((n_buf * (N_DEV - 1),)), pltpu.SemaphoreType.DMA((n_buf * (N_DEV - 1),)),
            pltpu.SemaphoreType.DMA((n_buf,))]


def _xchg_copies(src_refs, dst_refs, gathers, send_sems, recv_sems, local_sems):
    x, y, c = lax.axis_index("x"), lax.axis_index("y"), lax.axis_index("c")
    me = 4 * x + 2 * y + c
    local, starts, arrivals = [], [], []
    for bi, (src, dst, gather) in enumerate(zip(src_refs, dst_refs, gathers)):
        local.append(pltpu.make_async_copy(src if gather else src.at[me], dst.at[me], local_sems.at[bi]))
        for kk in range(1, N_DEV):
            px = 1 - x if kk & 4 else x
            py = 1 - y if kk & 2 else y
            pc = 1 - c if kk & 1 else c
            pid = 4 * px + 2 * py + pc
            sem = bi * (N_DEV - 1) + kk - 1
            starts.append(pltpu.make_async_remote_copy(
                src_ref=src if gather else src.at[pid], dst_ref=dst.at[me],
                send_sem=send_sems.at[sem], recv_sem=recv_sems.at[sem],
                device_id=(px, py, pc), device_id_type=pl.DeviceIdType.MESH))
            arrivals.append(pltpu.make_async_remote_copy(
                src_ref=src if gather else src.at[me], dst_ref=dst.at[pid],
                send_sem=send_sems.at[sem], recv_sem=recv_sems.at[sem],
                device_id=(px, py, pc), device_id_type=pl.DeviceIdType.MESH))
    return local, starts, arrivals


def _xchg_start(copies):
    local, sends, _ = copies
    for cp in local + sends:
        cp.start()


def _xchg_finish(copies):
    local, sends, arrivals = copies
    for cp in arrivals:
        cp.wait_recv()
    for cp in sends:
        cp.wait_send()
    for cp in local:
        cp.wait()


def _gather_by_chip(src_refs, dst_refs, send_sems, recv_sems, local_sems):
    x, y, c = lax.axis_index("x"), lax.axis_index("y"), lax.axis_index("c")
    me = 4 * x + 2 * y + c
    sibling = (x, y, 1 - c)

    def place(kk):
        px = 1 - x if kk & 4 else x
        py = 1 - y if kk & 2 else y
        pc = 1 - c if kk & 1 else c
        return (px, py, pc), 4 * px + 2 * py + pc

    def copy(bi, kk, src, dst, to):
        sem = bi * (N_DEV - 1) + kk - 1
        return pltpu.make_async_remote_copy(src_ref=src, dst_ref=dst, send_sem=send_sems.at[sem],
                                            recv_sem=recv_sems.at[sem], device_id=to, device_id_type=pl.DeviceIdType.MESH)

    local, sends = [], []
    for bi, (src, dst) in enumerate(zip(src_refs, dst_refs)):
        local.append(pltpu.make_async_copy(src, dst.at[me], local_sems.at[bi]))
        sends += [copy(bi, kk, src, dst.at[me], place(kk)[0]) for kk in (1, 2, 4, 6)]
    for cp in local + sends:
        cp.start()
    for kk in (2, 4, 6):
        for bi, (src, dst) in enumerate(zip(src_refs, dst_refs)):
            dev, pid = place(kk)
            copy(bi, kk, src, dst.at[pid], dev).wait_recv()
            passed = copy(bi, kk | 1, dst.at[pid], dst.at[pid], sibling)
            passed.start()
            sends.append(passed)
    for kk in (1, 3, 5, 7):
        for bi, (src, dst) in enumerate(zip(src_refs, dst_refs)):
            dev, pid = place(kk)
            copy(bi, kk, src, dst.at[pid], sibling).wait_recv()
    for cp in sends:
        cp.wait_send()
    for cp in local:
        cp.wait()


def _exchange(name, bufs, by_chip=False):
    n_buf = len(bufs)
    gathers = [g for _, g in bufs]
    assert not by_chip or all(gathers)

    def body(*refs):
        srcs, dsts = refs[:n_buf], refs[n_buf:2 * n_buf]
        if by_chip:
            _gather_by_chip(srcs, dsts, *refs[2 * n_buf:])
            return
        copies = _xchg_copies(srcs, dsts, gathers, *refs[2 * n_buf:])
        _xchg_start(copies)
        _xchg_finish(copies)

    return pl.pallas_call(
        body, name=name, out_shape=_xchg_out_shapes(bufs),
        in_specs=[HBM_SPEC] * n_buf, out_specs=[HBM_SPEC] * n_buf, scratch_shapes=_xchg_scratch(n_buf),
    )(*[a for a, _ in bufs])


def _chunk_mask(t, keys_first):
    key = lax.broadcasted_iota(jnp.int32, (t, t), 0 if keys_first else 1) // CHUNK
    query = lax.broadcasted_iota(jnp.int32, (t, t), 1 if keys_first else 0) // CHUNK
    return query >= key


def _grid_ends(grid):
    ids = [pl.program_id(ax) for ax in range(len(grid))]
    first = functools.reduce(jnp.logical_and, [i == 0 for i in ids])
    last = functools.reduce(jnp.logical_and, [i == g - 1 for i, g in zip(ids, grid)])
    return first, last


def _attn_fwd(q, k, v, bufs, n_b, seq):
    tq = min(ATT_TILE, seq)
    nq = seq // tq
    nt_dims = _DIMS["nt"]
    hpb = ATT_HEADS
    grid = (n_b, N_HEADS // hpb, nq)
    n_buf = len(bufs)
    gathers = [g for _, g in bufs]
    sum_lane = [HEAD_PAD - 1 if hh % 2 == 0 else 0 for hh in range(hpb)]

    def body(q_ref, k_ref, v_ref, *rest):
        srcs, (o_ref, lse_ref), dsts = rest[:n_buf], rest[n_buf:n_buf + 2], rest[n_buf + 2:2 * n_buf + 2]
        s_ref = rest[2 * n_buf + 2]
        copies = _xchg_copies(srcs, dsts, gathers, *rest[2 * n_buf + 3:])
        first, last = _grid_ends(grid)
        pl.when(first)(functools.partial(_xchg_start, copies))

        qi = pl.program_id(2)
        mask = _chunk_mask(tq, keys_first=False)
        lane_row = lax.broadcasted_iota(jnp.int32, (1, HEAD_PAD), 1)
        ones = [(lane_row == sum_lane[hh]).astype(BF16) for hh in range(hpb)]
        qs = [q_ref[0, hh] for hh in range(hpb)]

        def score_step(j, tops, masked):
            rows = pl.ds(pl.multiple_of(j * tq, tq), tq)
            out = []
            for hh in range(hpb):
                s = lax.dot_general(qs[hh], k_ref[0, hh, rows, :], nt_dims, preferred_element_type=F32)
                if masked:
                    s = jnp.where(mask, s, NEG_BIG)
                s_ref[hh, j] = s
                out.append(jnp.maximum(tops[hh], s))
            return tuple(out)

        tops = tuple(jnp.full((tq, tq), NEG_BIG, F32) for _ in range(hpb))
        tops = lax.fori_loop(0, qi, functools.partial(score_step, masked=False), tops)
        tops = score_step(qi, tops, True)
        ms = [jnp.max(top, axis=-1, keepdims=True) for top in tops]

        def value_step(j, accs):
            rows = pl.ds(pl.multiple_of(j * tq, tq), tq)
            out = []
            for hh in range(hpb):
                p = jnp.exp2(s_ref[hh, j] - ms[hh]).astype(BF16)
                out.append(accs[hh] + jnp.dot(p, v_ref[0, hh, rows, :] + ones[hh], preferred_element_type=F32))
            return tuple(out)

        accs = tuple(jnp.zeros((tq, HEAD_PAD), F32) for _ in range(hpb))
        accs = lax.fori_loop(0, qi + 1, value_step, accs)
        carry = list(zip(ms, accs))
        lane = lax.broadcasted_iota(jnp.int32, (tq, HEAD_PAD), 1)
        for pair in range(hpb // 2):
            outs = []
            for hh in (2 * pair, 2 * pair + 1):
                m, acc = carry[hh]
                l = jnp.sum(jnp.where(lane == sum_lane[hh], acc, 0.0), axis=-1, keepdims=True)
                outs.append(acc * (1.0 / l))
                lse_ref[0, hh] = jnp.broadcast_to(m + jnp.log2(l), (tq, HEAD_PAD)).T[0:8, :]
            o_ref[0, :, pair * HEAD_PAD:(pair + 1) * HEAD_PAD] = jnp.where(lane < V_HEAD_DIM, outs[0], outs[1]).astype(BF16)

        pl.when(last)(functools.partial(_xchg_finish, copies))

    kv_spec = pl.BlockSpec((1, hpb, seq, HEAD_PAD), lambda b, hb, i: (b, hb, 0, 0))
    q_spec = pl.BlockSpec((1, hpb, tq, HEAD_PAD), lambda b, hb, i: (b, hb, i, 0))
    res = pl.pallas_call(
        body, name="attn_fwd", grid=grid,
        in_specs=[q_spec, kv_spec, kv_spec] + [HBM_SPEC] * n_buf,
        out_specs=[pl.BlockSpec((1, tq, hpb * V_HEAD_DIM), lambda b, hb, i: (b, i, hb)),
                   pl.BlockSpec((1, hpb, 8, tq), lambda b, hb, i: (b, hb, 0, i))] + [HBM_SPEC] * n_buf,
        out_shape=[jax.ShapeDtypeStruct((n_b, seq, N_HEADS * V_HEAD_DIM), BF16),
                   jax.ShapeDtypeStruct((n_b, N_HEADS, 8, seq), F32)] + _xchg_out_shapes(bufs),
        scratch_shapes=[pltpu.VMEM((hpb, nq, tq, tq), F32)] + _xchg_scratch(n_buf),
        compiler_params=_params(("arbitrary", "arbitrary", "arbitrary")),
    )(q, k, v, *[a for a, _ in bufs])
    return res[0], res[1], res[2:]


def _attn_bwd(q, k, v, kt, do, o, lse, bufs, n_b, seq):
    tq = min(ATT_TILE, seq)
    nq = seq // tq
    nt_dims = _DIMS["nt"]
    hpb = ATT_HEADS
    grid = (n_b, N_HEADS // hpb, nq)
    n_buf = len(bufs)
    gathers = [g for _, g in bufs]

    def body(q_ref, k_ref, v_ref, kt_ref, do_ref, o_ref, lse_ref, *rest):
        srcs, (dq_ref, dk_ref, dv_ref), dsts = rest[:n_buf], rest[n_buf:n_buf + 3], rest[n_buf + 3:2 * n_buf + 3]
        dk_acc, dv_acc = rest[2 * n_buf + 3:2 * n_buf + 5]
        copies = _xchg_copies(srcs, dsts, gathers, *rest[2 * n_buf + 5:])
        first, last = _grid_ends(grid)
        pl.when(first)(functools.partial(_xchg_start, copies))

        qi = pl.program_id(2)

        @pl.when(qi == 0)
        def _():
            dk_acc[...] = jnp.zeros_like(dk_acc)
            dv_acc[...] = jnp.zeros_like(dv_acc)

        mask = _chunk_mask(tq, keys_first=True)
        lane = lax.broadcasted_iota(jnp.int32, (tq, HEAD_PAD), 1)
        qs, dos, deltas, lses = [], [], [], []
        for hh in range(hpb):
            cols = slice((hh // 2) * HEAD_PAD, (hh // 2 + 1) * HEAD_PAD)
            do_pair = do_ref[0, :, cols]
            prod = do_pair.astype(F32) * o_ref[0, :, cols].astype(F32)
            delta = jnp.sum(jnp.where(lane // V_HEAD_DIM == hh % 2, prod, 0.0), axis=-1, keepdims=True)
            qs.append(q_ref[0, hh])
            dos.append(do_pair)
            deltas.append(jnp.broadcast_to(delta, (tq, HEAD_PAD)).T[0:1, :])
            lses.append(lse_ref[0, hh][0:1, :])

        def step(j, dqs, masked):
            rows = pl.ds(pl.multiple_of(j * tq, tq), tq)
            out = []
            for hh in range(hpb):
                s = lax.dot_general(k_ref[0, hh, rows, :], qs[hh], nt_dims, preferred_element_type=F32)
                p = jnp.exp2(s - lses[hh])
                if masked:
                    p = jnp.where(mask, p, 0.0)
                dv_acc[hh, rows, :] += jnp.dot(p.astype(BF16), dos[hh], preferred_element_type=F32)
                dp = lax.dot_general(v_ref[0, hh, rows, :], dos[hh], nt_dims, preferred_element_type=F32)
                ds = (p * (dp - deltas[hh])).astype(BF16)
                dk_acc[hh, rows, :] += jnp.dot(ds, qs[hh], preferred_element_type=F32)
                out.append(dqs[hh] + jnp.dot(kt_ref[0, hh, j], ds, preferred_element_type=F32))
            return tuple(out)

        dqs = tuple(jnp.zeros((HEAD_PAD, tq), F32) for _ in range(hpb))
        dqs = lax.fori_loop(0, qi, functools.partial(step, masked=False), dqs)
        dqs = step(qi, dqs, True)
        for hh in range(hpb):
            dq_ref[0, hh] = dqs[hh].T.astype(BF16)

        @pl.when(qi == nq - 1)
        def _():
            dk_ref[0] = (dk_acc[...] * LN2).astype(BF16)
            dv_ref[0] = dv_acc[...].astype(BF16)

        pl.when(last)(functools.partial(_xchg_finish, copies))

    full_spec = pl.BlockSpec((1, hpb, seq, HEAD_PAD), lambda b, hb, i: (b, hb, 0, 0))
    t_spec = pl.BlockSpec((1, hpb, nq, HEAD_PAD, tq), lambda b, hb, i: (b, hb, 0, 0, 0))
    q_spec = pl.BlockSpec((1, hpb, tq, HEAD_PAD), lambda b, hb, i: (b, hb, i, 0))
    o_spec = pl.BlockSpec((1, tq, hpb * V_HEAD_DIM), lambda b, hb, i: (b, i, hb))
    lse_spec = pl.BlockSpec((1, hpb, 8, tq), lambda b, hb, i: (b, hb, 0, i))
    head_shape = jax.ShapeDtypeStruct((n_b, N_HEADS, seq, HEAD_PAD), BF16)
    res = pl.pallas_call(
        body, name="attn_bwd", grid=grid,
        in_specs=[q_spec, full_spec, full_spec, t_spec, o_spec, o_spec, lse_spec] + [HBM_SPEC] * n_buf,
        out_specs=[q_spec, full_spec, full_spec] + [HBM_SPEC] * n_buf,
        out_shape=[head_shape] * 3 + _xchg_out_shapes(bufs),
        scratch_shapes=[pltpu.VMEM((hpb, seq, HEAD_PAD), F32), pltpu.VMEM((hpb, seq, HEAD_PAD), F32)]
        + _xchg_scratch(n_buf),
        compiler_params=_params(("arbitrary", "arbitrary", "arbitrary")),
    )(q, k, v, kt, do, o, lse, *[a for a, _ in bufs])
    return res[0], res[1], res[2], res[3:]


def _in_proj_fwd(x2, scale, shift, g, w_parts, z_dtypes, seq):
    n_rows, d = x2.shape
    tm = min(512, seq)
    per_seq = seq // tm
    n_part = len(w_parts)
    nt_dims = _DIMS["nt"]

    def body(x_ref, sc_ref, sh_ref, g_ref, *rest):
        w_refs, h_ref, z_refs = rest[:n_part], rest[n_part], rest[n_part + 1:]
        h = _norm_mod(x_ref[...], g_ref[...], sc_ref[0], sh_ref[0]).astype(BF16)
        h_ref[...] = h
        for w_ref, z_ref in zip(w_refs, z_refs):
            z_ref[...] = lax.dot_general(h, w_ref[...], nt_dims, preferred_element_type=F32).astype(z_ref.dtype)

    row = lambda width: pl.BlockSpec((tm, width), lambda i: (i, 0))
    bat = pl.BlockSpec((1, 1, d), lambda i: (i // per_seq, 0, 0))
    whole = lambda arr: pl.BlockSpec(arr.shape, lambda i: (0, 0))
    return pl.pallas_call(
        body, name="in_proj_fwd", grid=(n_rows // tm,),
        in_specs=[row(d), bat, bat, whole(g)] + [whole(w) for w in w_parts],
        out_specs=[row(d)] + [row(w.shape[0]) for w in w_parts],
        out_shape=[jax.ShapeDtypeStruct((n_rows, d), BF16)]
        + [jax.ShapeDtypeStruct((n_rows, w.shape[0]), dt) for w, dt in zip(w_parts, z_dtypes)],
        compiler_params=_params(("parallel",)),
    )(x2, scale, shift, g, *w_parts)
def _in_proj_bwd(parts, x2, dx1, scale, g, bufs, seq):
    n_rows, d = x2.shape
    tm = min(512, seq)
    per_seq = seq // tm
    grid = (n_rows // tm,)
    n_part, n_buf = len(parts), len(bufs)
    gathers = [gt for _, gt in bufs]

    def body(*refs):
        dz_refs, w_refs = refs[:n_part], refs[n_part:2 * n_part]
        x_ref, dx1_ref, sc_ref, g_ref = refs[2 * n_part:2 * n_part + 4]
        srcs = refs[2 * n_part + 4:2 * n_part + 4 + n_buf]
        gx_ref, dsc_ref, dsh_ref, dg_ref = refs[2 * n_part + 4 + n_buf:2 * n_part + 8 + n_buf]
        dsts = refs[2 * n_part + 8 + n_buf:2 * n_part + 8 + 2 * n_buf]
        copies = _xchg_copies(srcs, dsts, gathers, *refs[2 * n_part + 8 + 2 * n_buf:])
        first, last = _grid_ends(grid)
        pl.when(first)(functools.partial(_xchg_start, copies))

        i = pl.program_id(0)
        dh = None
        for dz_ref, w_ref in zip(dz_refs, w_refs):
            term = jnp.dot(dz_ref[...], w_ref[...], preferred_element_type=F32)
            dh = term if dh is None else dh + term
        dx, dsc, dsh, dg = _norm_mod_bwd(x_ref[...], g_ref[...], sc_ref[0], dh)
        gx_ref[...] = dx1_ref[...] + dx

        @pl.when(i % per_seq == 0)
        def _():
            dsc_ref[...] = jnp.zeros_like(dsc_ref)
            dsh_ref[...] = jnp.zeros_like(dsh_ref)

        @pl.when(i == 0)
        def _():
            dg_ref[...] = jnp.zeros_like(dg_ref)

        dsc_ref[0] += dsc
        dsh_ref[0] += dsh
        dg_ref[...] += dg
        pl.when(last)(functools.partial(_xchg_finish, copies))

    row = lambda width: pl.BlockSpec((tm, width), lambda i: (i, 0))
    bat = pl.BlockSpec((1, 1, d), lambda i: (i // per_seq, 0, 0))
    whole = lambda arr: pl.BlockSpec(arr.shape, lambda i: (0, 0))
    n_b = n_rows // seq
    res = pl.pallas_call(
        body, name="in_proj_bwd", grid=grid,
        in_specs=[row(dz.shape[1]) for dz, _ in parts] + [whole(w) for _, w in parts]
        + [row(d), row(d), bat, whole(g)] + [HBM_SPEC] * n_buf,
        out_specs=[row(d), bat, bat, whole(g)] + [HBM_SPEC] * n_buf,
        out_shape=[jax.ShapeDtypeStruct((n_rows, d), F32), jax.ShapeDtypeStruct((n_b, 1, d), F32),
                   jax.ShapeDtypeStruct((n_b, 1, d), F32), jax.ShapeDtypeStruct(g.shape, F32)] + _xchg_out_shapes(bufs),
        scratch_shapes=_xchg_scratch(n_buf),
        compiler_params=_params(("arbitrary",)),
    )(*[dz for dz, _ in parts], *[w for _, w in parts], x2, dx1, scale, g, *[a for a, _ in bufs])
    return res[0], res[1], res[2], res[3], res[4:]


def _ln_silu(u1, g, b):
    mu = jnp.mean(u1, axis=-1, keepdims=True)
    uc = u1 - mu
    r = lax.rsqrt(jnp.mean(uc * uc, axis=-1, keepdims=True) + EPS)
    y = uc * r * g + b
    return y * _sigmoid(y)


def _conv_fill_glu(z_ref, u0_ref, seq, tile):
    u0_ref[0:CONV_HALO, :] = jnp.zeros((CONV_HALO, CONV_CH), F32)
    u0_ref[CONV_HALO + seq:CONV_HALO + seq + CONV_TAIL, :] = jnp.zeros((CONV_TAIL, CONV_CH), F32)
    for t in range(seq // tile):
        zt = z_ref[0, t * tile:(t + 1) * tile, :].astype(F32)
        u0_ref[CONV_HALO + t * tile:CONV_HALO + (t + 1) * tile, :] = zt[:, :CONV_CH] * _sigmoid(zt[:, CONV_CH:])


def _conv_windows(ref, views_ref, t, tile):
    for b in range(8):
        views_ref[b] = ref[t * tile + b:t * tile + b + tile + CONV_HALO, :]


def _conv_tap(views_ref, offset, tile):
    return views_ref[offset % 8, 8 * (offset // 8):8 * (offset // 8) + tile, :]


def _conv_tile(u0_ref, views_ref, w_ref, b_ref, t, tile):
    _conv_windows(u0_ref, views_ref, t, tile)
    acc = jnp.broadcast_to(b_ref[...], (tile, CONV_CH))
    for kk in range(CONV_WIDTH):
        acc = acc + w_ref[kk:kk + 1, :] * _conv_tap(views_ref, kk + CONV_HALO - (CONV_WIDTH - 1), tile)
    return acc


def _conv_fwd(zglu, conv_w, conv_b, ln_g, ln_b, n_b, seq):
    tile = min(256, seq)

    def body(z_ref, w_ref, b_ref, g_ref, bb_ref, o_ref, u1_ref, u0_ref, views_ref):
        _conv_fill_glu(z_ref, u0_ref, seq, tile)
        for t in range(seq // tile):
            u1 = _conv_tile(u0_ref, views_ref, w_ref, b_ref, t, tile)
            u1_ref[0, t * tile:(t + 1) * tile, :] = u1
            o_ref[0, t * tile:(t + 1) * tile, :] = _ln_silu(u1, g_ref[...], bb_ref[...]).astype(BF16)

    whole2 = lambda arr: pl.BlockSpec(arr.shape, lambda b: (0, 0))
    seq_spec = pl.BlockSpec((1, seq, CONV_CH), lambda b: (b, 0, 0))
    return pl.pallas_call(
        body, name="conv_fwd", grid=(n_b,),
        in_specs=[pl.BlockSpec((1, seq, 2 * CONV_CH), lambda b: (b, 0, 0)), whole2(conv_w), whole2(conv_b),
                  whole2(ln_g), whole2(ln_b)],
        out_specs=[seq_spec, seq_spec],
        out_shape=[jax.ShapeDtypeStruct((n_b, seq, CONV_CH), BF16), jax.ShapeDtypeStruct((n_b, seq, CONV_CH), F32)],
        scratch_shapes=[pltpu.VMEM((seq + CONV_HALO + CONV_TAIL, CONV_CH), F32),
                        pltpu.VMEM((8, tile + CONV_HALO, CONV_CH), F32)],
        compiler_params=_params(("parallel",)),
    )(zglu, conv_w, conv_b, ln_g, ln_b)


def _conv_bwd(zglu, u1_saved, du3, conv_w, ln_g, ln_b, n_b, seq):
    tile = min(256, seq)
    n_t = seq // tile

    def body(z_ref, u1_ref, du3_ref, w_ref, g_ref, bb_ref, dz_ref, dw_ref, db_ref, dg_ref, dbb_ref, u0_ref, du1_ref,
             u0_views, du1_views):
        @pl.when(pl.program_id(0) == 0)
        def _():
            for r in (dw_ref, db_ref, dg_ref, dbb_ref):
                r[...] = jnp.zeros_like(r)

        _conv_fill_glu(z_ref, u0_ref, seq, tile)
        du1_ref[seq:seq + CONV_HALO + CONV_TAIL, :] = jnp.zeros((CONV_HALO + CONV_TAIL, CONV_CH), F32)
        g = g_ref[...]
        for t in range(n_t):
            u1 = u1_ref[0, t * tile:(t + 1) * tile, :]
            mu = jnp.mean(u1, axis=-1, keepdims=True)
            uc = u1 - mu
            r = lax.rsqrt(jnp.mean(uc * uc, axis=-1, keepdims=True) + EPS)
            xh = uc * r
            y = xh * g + bb_ref[...]
            sg = _sigmoid(y)
            dy = du3_ref[0, t * tile:(t + 1) * tile, :].astype(F32) * (sg * (1.0 + y * (1.0 - sg)))
            dg_ref[...] += jnp.sum(dy * xh, axis=0, keepdims=True)
            dbb_ref[...] += jnp.sum(dy, axis=0, keepdims=True)
            dxh = dy * g
            du1 = r * (dxh - jnp.mean(dxh, axis=-1, keepdims=True) - xh * jnp.mean(dxh * xh, axis=-1, keepdims=True))
            db_ref[...] += jnp.sum(du1, axis=0, keepdims=True)
            du1_ref[t * tile:(t + 1) * tile, :] = du1
        for t in range(n_t):
            du1 = du1_ref[t * tile:(t + 1) * tile, :]
            du0 = jnp.zeros((tile, CONV_CH), F32)
            _conv_windows(u0_ref, u0_views, t, tile)
            _conv_windows(du1_ref, du1_views, t, tile)
            for kk in range(CONV_WIDTH):
                du0 = du0 + w_ref[kk:kk + 1, :] * _conv_tap(du1_views, CONV_WIDTH - 1 - kk, tile)
                u0_tap = _conv_tap(u0_views, kk + CONV_HALO - (CONV_WIDTH - 1), tile)
                dw_ref[kk:kk + 1, :] += jnp.sum(du1 * u0_tap, axis=0, keepdims=True)
            zt = z_ref[0, t * tile:(t + 1) * tile, :].astype(F32)
            ga, sb = zt[:, :CONV_CH], _sigmoid(zt[:, CONV_CH:])
            dz_ref[0, t * tile:(t + 1) * tile, :CONV_CH] = (du0 * sb).astype(BF16)
            dz_ref[0, t * tile:(t + 1) * tile, CONV_CH:] = (du0 * ga * sb * (1.0 - sb)).astype(BF16)

    whole2 = lambda arr: pl.BlockSpec(arr.shape, lambda b: (0, 0))
    z_spec = pl.BlockSpec((1, seq, 2 * CONV_CH), lambda b: (b, 0, 0))
    seq_spec = pl.BlockSpec((1, seq, CONV_CH), lambda b: (b, 0, 0))
    return pl.pallas_call(
        body, name="conv_bwd", grid=(n_b,),
        in_specs=[z_spec, seq_spec, seq_spec, whole2(conv_w), whole2(ln_g), whole2(ln_b)],
        out_specs=[z_spec, whole2(conv_w), whole2(ln_g), whole2(ln_g), whole2(ln_b)],
        out_shape=[jax.ShapeDtypeStruct((n_b, seq, 2 * CONV_CH), BF16), jax.ShapeDtypeStruct(conv_w.shape, F32),
                   jax.ShapeDtypeStruct(ln_g.shape, F32), jax.ShapeDtypeStruct(ln_g.shape, F32),
                   jax.ShapeDtypeStruct(ln_b.shape, F32)],
        scratch_shapes=[pltpu.VMEM((seq + CONV_HALO + CONV_TAIL, CONV_CH), F32)] * 2
        + [pltpu.VMEM((8, tile + CONV_HALO, CONV_CH), F32)] * 2,
        compiler_params=_params(("arbitrary",)),
    )(zglu, u1_saved, du3, conv_w, ln_g, ln_b)


def _sum_parts(name, parts):
    n_parts = parts.shape[0]

    def body(p_ref, o_ref):
        gg = p_ref[0].astype(F32)
        for j in range(1, n_parts):
            gg = gg + p_ref[j].astype(F32)
        o_ref[...] = gg

    return pl.pallas_call(body, name=name, out_shape=jax.ShapeDtypeStruct(parts.shape[1:], F32),
                          compiler_params=_params(None))(parts)


def _adamw(name, w, parts, m, v, transposed=False):
    n_parts = parts.shape[0]
    rows, cols = w.shape
    tr = ADAM_ROWS if rows % ADAM_ROWS == 0 else rows

    def body(w_ref, p_ref, m_ref, v_ref, g_ref, d_ref, nm_ref, nv_ref):
        gg = p_ref[0].astype(F32)
        for j in range(1, n_parts):
            gg = gg + p_ref[j].astype(F32)
        if transposed:
            gg = gg.T
        nm = ADAM_B1 * m_ref[...] + (1.0 - ADAM_B1) * gg
        nv = ADAM_B2 * v_ref[...] + (1.0 - ADAM_B2) * jnp.square(gg)
        m_hat = nm / (1.0 - ADAM_B1 ** ADAM_STEP)
        v_hat = nv / (1.0 - ADAM_B2 ** ADAM_STEP)
        g_ref[...] = gg
        d_ref[...] = -ADAM_LR * (m_hat / (jnp.sqrt(v_hat) + ADAM_EPS) + ADAM_WD * w_ref[...])
        nm_ref[...] = nm
        nv_ref[...] = nv

    shape = jax.ShapeDtypeStruct(w.shape, F32)
    blk = pl.BlockSpec((tr, cols), lambda i: (i, 0))
    p_spec = (pl.BlockSpec((n_parts, cols, tr), lambda i: (0, 0, i)) if transposed
              else pl.BlockSpec((n_parts, tr, cols), lambda i: (0, i, 0)))
    return pl.pallas_call(body, name=name, grid=(rows // tr,), in_specs=[blk, p_spec, blk, blk], out_specs=[blk] * 4,
                          out_shape=[shape] * 4, compiler_params=_params(("parallel",)))(w, parts, m, v)


def _rope_tables(seq):
    inv_freq = ROPE_THETA ** (-jnp.arange(0, QK_ROPE_DIM, 2, dtype=F32) / QK_ROPE_DIM)
    ang = jnp.arange(seq, dtype=F32)[:, None] * inv_freq[None, :]
    cos, sin = jnp.cos(ang), jnp.sin(ang)
    half = QK_ROPE_DIM // 2
    z = lambda n: jnp.zeros((seq, n), F32)
    tail = HEAD_PAD - QK_HEAD_DIM
    cos_t = jnp.concatenate([jnp.ones((seq, QK_NOPE_DIM), F32), cos, cos, z(tail)], axis=1)
    sin_lo = jnp.concatenate([z(QK_NOPE_DIM), -sin, z(half), z(tail)], axis=1)
    sin_hi = jnp.concatenate([z(QK_NOPE_DIM), z(half), sin, z(tail)], axis=1)
    return cos_t, sin_lo, sin_hi


def _pad_lanes(v, width=HEAD_PAD):
    return jnp.pad(v, [(0, 0)] * (v.ndim - 1) + [(0, width - v.shape[-1])])


def _unstack_cols(s):
    return s.transpose(1, 0, 2).reshape(s.shape[1], N_DEV * s.shape[2])


def _stack_cols(g, dtype):
    rows, cols = g.shape
    return g.reshape(rows, N_DEV, cols // N_DEV).transpose(1, 0, 2).astype(dtype)


def kernel(x, c, w_ada, b_ada, norm1_g, w_in, q_latent_g, w_uq, kv_latent_g, w_ukv, qk_norm_q_g, qk_norm_k_g, w_o_mla, conv_w, conv_b, conv_ln_g, conv_ln_b, w_pw_out, w_out, norm2_g, w_ff1, w_ff2, loss_target, m_w_ada, m_b_ada, m_norm1_g, m_w_in, m_q_latent_g, m_w_uq, m_kv_latent_g, m_w_ukv, m_qk_norm_q_g, m_qk_norm_k_g, m_w_o_mla, m_conv_w, m_conv_b, m_conv_ln_g, m_conv_ln_b, m_w_pw_out, m_w_out, m_norm2_g, m_w_ff1, m_w_ff2, v_w_ada, v_b_ada, v_norm1_g, v_w_in, v_q_latent_g, v_w_uq, v_kv_latent_g, v_w_ukv, v_qk_norm_q_g, v_qk_norm_k_g, v_w_o_mla, v_conv_w, v_conv_b, v_conv_ln_g, v_conv_ln_b, v_w_pw_out, v_w_out, v_norm2_g, v_w_ff1, v_w_ff2):
    given = dict(locals())
    local = {n: given[n][0] for n in WEIGHTS}
    vec = {n: local[n].reshape(1, -1) for n in REPLICATED}
    bf = lambda n: local[n].astype(BF16)
    n_b, seq, d = x.shape
    n_rows = n_b * seq
    x2 = x.reshape(n_rows, d)
    t2 = loss_target.reshape(n_rows, d)
    me = 4 * lax.axis_index("x") + 2 * lax.axis_index("y") + lax.axis_index("c")
    ada_cols = local["w_ada"].shape[1]

    tsh = lambda n: local[n].T.astype(BF16)
    c_all, w_in_s, w_uq_s, w_ukv_s, conv_w_s = _exchange(
        "gather_early", [(c, True), (tsh("w_in"), True), (bf("w_uq"), True), (bf("w_ukv"), True), (local["conv_w"], True)],
        by_chip=True)
    w_in_t = w_in_s.reshape(-1, d)
    zrows = lambda n: jnp.zeros((n, d), BF16)
    w_sm_t = jnp.concatenate([w_in_t[:OFF_KV], zrows(QK_NOPE_DIM), w_in_t[OFF_KV:OFF_KR], zrows(HEAD_PAD - QK_HEAD_DIM)], axis=0)
    w_glu_t = w_in_t[OFF_KR:OFF_GLU]
    w_gate_t = w_in_t[OFF_GLU:]
    wuq = _pad_lanes(_unstack_cols(w_uq_s).reshape(Q_LORA, N_HEADS, QK_HEAD_DIM)).reshape(Q_LORA, N_HEADS * HEAD_PAD)
    wukv_f = _unstack_cols(w_ukv_s).reshape(KV_LORA, N_HEADS, QK_NOPE_DIM + V_HEAD_DIM)
    wv = wukv_f[:, :, QK_NOPE_DIM:]
    odd = (jnp.arange(N_HEADS) % 2 == 1)[None, :, None]
    wuv = jnp.where(odd, jnp.pad(wv, ((0, 0), (0, 0), (V_HEAD_DIM, 0))), jnp.pad(wv, ((0, 0), (0, 0), (0, V_HEAD_DIM))))
    wukv = jnp.concatenate([_pad_lanes(wukv_f[:, :, :QK_NOPE_DIM]), wuv], axis=1).reshape(KV_LORA, 2 * N_HEADS * HEAD_PAD)
    gqn = _pad_lanes(vec["qk_norm_q_g"])
    gkn = _pad_lanes(vec["qk_norm_k_g"])
    conv_w_f = jnp.pad(_unstack_cols(conv_w_s), ((0, 1), (0, 0)))
    rope = _rope_tables(seq)

    all_rows = N_DEV * n_b
    pad_rows = (-all_rows) % ROWS_PAD
    c_rows = jnp.pad(c_all.reshape(all_rows, d), ((0, pad_rows), (0, 0)))
    b_cols = lax.dynamic_slice(local["b_ada"], (me * ada_cols,), (ada_cols,))
    mod_cols = _mm("ada_fwd", c_rows, local["w_ada"], "nn", F32, a_fn=_silu, epi=lambda acc, b: acc + b,
                   epi_in=(jnp.broadcast_to(b_cols, (all_rows + pad_rows, ada_cols)),))
    (mod_s,) = _exchange("scatter_mod", [(mod_cols[:all_rows].reshape(N_DEV, n_b, ada_cols), False)])
    mod = mod_s.transpose(1, 0, 2).reshape(n_b, ADA_CHUNKS, 1, d)
    shift1, scale1, gate1, shift2, scale2, gate2 = [mod[:, i] for i in range(ADA_CHUNKS)]

    h, zgate, zglu, zsm = _in_proj_fwd(x2, scale1, shift1, vec["norm1_g"], [w_gate_t, w_glu_t, w_sm_t],
                                       [BF16, BF16, F32], seq)
    q, k, v, kt = _mla_prep_fwd(zsm, wuq, wukv, vec["q_latent_g"], vec["kv_latent_g"], gqn, gkn, rope, n_b, seq)
    attn, lse, (w_o_s, w_pw_s, w_out_s, w_ff1_s, w_ff2_s) = _attn_fwd(
        q, k, v, [(tsh("w_o_mla"), True), (tsh("w_pw_out"), True), (bf("w_out"), True), (tsh("w_ff1"), True),
                  (bf("w_ff2"), True)], n_b, seq)
    w_o_t = w_o_s.reshape(d, -1)
    w_pw_t = w_pw_s.reshape(d, -1)
    w_out_f = w_out_s.reshape(d, d)
    w_ff1_t = w_ff1_s.reshape(-1, d)
    w_ff2_f = w_ff2_s.reshape(-1, d)
    attn2 = attn.reshape(n_rows, N_HEADS * V_HEAD_DIM)
    u3, u1 = _conv_fwd(zglu.reshape(n_b, seq, 2 * CONV_CH), conv_w_f, vec["conv_b"], vec["conv_ln_g"], vec["conv_ln_b"], n_b, seq)
    u32 = u3.reshape(n_rows, CONV_CH)
    ya = _mm("mla_out", attn2, w_o_t, "nt", BF16)
    yb = _mm("conv_out", u32, w_pw_t, "nt", BF16)
    mmr = functools.partial(_mm_rows, n_rows=n_rows, seq=seq)

    def merge_fn(t):
        return _sigmoid(t[0]) * t[2] + _sigmoid(t[1]) * t[3]

    def mid_fn(acc, r, b, cc):
        x1_ = r[0] + b[0] * acc
        return [acc, x1_, _norm_mod(x1_, cc[0], b[1], b[2])], [], []

    mrg, mixed, x1, h2 = mmr("out_proj", [(zgate, d, 0), (zgate, d, 1), (ya, d, 0), (yb, d, 0)], merge_fn, w_out_f, "nn",
                             mid_fn, rows=[_full(x2)], bats=[gate1, scale2, shift2], consts=[vec["norm2_g"]],
                             outs=[(d, BF16), (d, F32), (d, BF16)], a_out=BF16)

    a = _mm("ff1", h2, w_ff1_t, "nt", BF16)

    def loss_fn(ff, r, b, cc):
        err = r[0] + b[0] * ff - r[1]
        dy_ = err * (1.0 / d)
        sq = jnp.broadcast_to(jnp.sum(err * err, keepdims=True), (1, LANES))
        return [dy_, b[0] * dy_], [jnp.sum(dy_ * ff, axis=0, keepdims=True)], [sq]

    dy, df, dgate2, sq_err = mmr("ff2_loss", [(a, a.shape[1], 0)], lambda t: _relu2(t[0]), w_ff2_f, "nn", loss_fn,
                                 rows=[_full(x1), _full(t2)], bats=[gate2], outs=[(d, F32), (d, BF16)], bat_outs=[d],
                                 tot_outs=[(1, LANES)], tk=a.shape[1])

    da = _mm("ff2_bwd", df, w_ff2_f, "nt", BF16, epi=lambda acc, av: acc * 2.0 * jnp.maximum(av, 0.0), epi_in=(a,))
    g_ff2 = _mm("ff2_dw", a, df, "tn", BF16, a_fn=_relu2)
    g_ff1_t = _mm("ff1_dw", da, h2, "tn", BF16)

    def mid_bwd(dh2_, r, b, cc):
        dx, dsc, dsh, dg = _norm_mod_bwd(r[0], cc[0], b[0], dh2_)
        dx1_ = r[1] + dx
        return [dx1_, b[1] * dx1_], [dsc, dsh, jnp.sum(dx1_ * r[2].astype(F32), axis=0, keepdims=True)], [dg]

    dx1, dmixed, dscale2, dshift2, dgate1, g_norm2 = mmr(
        "ff1_bwd", [(da, da.shape[1], 0)], None, w_ff1_t, "nn", mid_bwd, rows=[_full(x1), _full(dy), _full(mixed)],
        bats=[scale2, gate1], consts=[vec["norm2_g"]], outs=[(d, F32), (d, BF16)], bat_outs=[d, d, d],
        tot_outs=[(1, d)], tk=da.shape[1])

    g_out = _mm("out_proj_dw", mrg, dmixed, "tn", BF16)

    def merge_bwd(dm, r, b, cc):
        ya_, yb_ = r[2].astype(F32), r[3].astype(F32)
        sa, sb = _sigmoid(r[0].astype(F32)), _sigmoid(r[1].astype(F32))
        return [dm * ya_ * sa * (1.0 - sa), dm * yb_ * sb * (1.0 - sb), dm * sa, dm * sb], [], []

    dzga, dzgb, dya, dyb = mmr("out_proj_bwd", [(dmixed, d, 0)], None, w_out_f, "nt", merge_bwd,
                               rows=[(zgate, d, 0), (zgate, d, 1), _full(ya), _full(yb)], outs=[(d, BF16)] * 4)
    dattn = _mm("mla_out_bwd", dya, w_o_t, "nn", BF16)
    g_o_t = _mm("mla_out_dw", dya, attn2, "tn", BF16)
    du3 = _mm("conv_out_bwd", dyb, w_pw_t, "nn", BF16)
    g_pw_t = _mm("conv_out_dw", dyb, u32, "tn", BF16)

    dzglu, g_conv_w, g_conv_b, g_ln_g, g_ln_b = _conv_bwd(
        zglu.reshape(n_b, seq, 2 * CONV_CH), u1, du3.reshape(n_b, seq, CONV_CH), conv_w_f, vec["conv_ln_g"],
        vec["conv_ln_b"], n_b, seq)
    dzglu = dzglu.reshape(n_rows, 2 * CONV_CH)

    dq, dk, dv, (p_ff2, p_ff1, p_out, p_pw, p_o) = _attn_bwd(
        q, k, v, kt, dattn.reshape(n_b, seq, N_HEADS * V_HEAD_DIM), attn, lse,
        [(g_ff2.reshape(N_DEV, -1, d), False), (g_ff1_t.reshape(N_DEV, -1, d), False), (g_out.reshape(N_DEV, -1, d), False),
         (g_pw_t.reshape(N_DEV, -1, CONV_CH), False), (g_o_t.reshape(N_DEV, -1, N_HEADS * V_HEAD_DIM), False)], n_b, seq)
    dzsm, g_wuq, g_wukv, g_gq, g_gkv, g_gqn, g_gkn = _mla_prep_bwd(
        zsm, dq, dk, dv, wuq, wukv, vec["q_latent_g"], vec["kv_latent_g"], gqn, gkn, rope, n_b, seq)

    g_gate_a_t = _mm("in_proj_gate_dw_a", dzga, h, "tn", BF16)
    g_gate_b_t = _mm("in_proj_gate_dw_b", dzgb, h, "tn", BF16)
    g_glu_t = _mm("in_proj_glu_dw", dzglu, h, "tn", BF16)
    g_sm_t = _mm("in_proj_sm_dw", dzsm, h, "tn", BF16)
    g_in_t = jnp.concatenate([g_sm_t[:OFF_KV], g_sm_t[OFF_KV + QK_NOPE_DIM:OFF_KV + QK_NOPE_DIM + QK_ROPE_DIM], g_glu_t,
                              g_gate_a_t, g_gate_b_t], axis=0)
    g_uq = g_wuq.reshape(Q_LORA, N_HEADS, HEAD_PAD)[:, :, :QK_HEAD_DIM].reshape(Q_LORA, N_HEADS * QK_HEAD_DIM)
    g_wukv = g_wukv.reshape(KV_LORA, 2, N_HEADS, HEAD_PAD)
    g_v = jnp.where(odd, g_wukv[:, 1, :, V_HEAD_DIM:], g_wukv[:, 1, :, :V_HEAD_DIM])
    g_ukv = jnp.concatenate([g_wukv[:, 0, :, :QK_NOPE_DIM], g_v], axis=2).reshape(KV_LORA, -1)

    grad_x, dscale1, dshift1, g_norm1, (p_in, p_uq, p_ukv, p_conv_w) = _in_proj_bwd(
        [(dzga, w_gate_t[:d]), (dzgb, w_gate_t[d:]), (dzglu, w_glu_t), (dzsm, w_sm_t)], x2, dx1, scale1, vec["norm1_g"],
        [(g_in_t.reshape(N_DEV, -1, d), False), (_stack_cols(g_uq, BF16), False), (_stack_cols(g_ukv, BF16), False),
         (_stack_cols(g_conv_w[:CONV_WIDTH], F32), False)], seq)

    dmod = jnp.concatenate([dshift1, dscale1, dgate1, dshift2, dscale2, dgate2], axis=1).reshape(n_b, N_DEV, ada_cols)
    (dmod_s,) = _exchange("scatter_dmod", [(dmod.transpose(1, 0, 2), False)])
    dmod_rows = jnp.pad(dmod_s.reshape(all_rows, ada_cols), ((0, pad_rows), (0, 0)))
    g_ada = _mm("ada_dw", c_rows, dmod_rows, "tn", F32, a_fn=_silu)
    (g_b_cols,) = _rowwise("ada_db", lambda r, b, cc: ([], [], [jnp.sum(r[0], axis=0, keepdims=True)]),
                           all_rows + pad_rows, all_rows + pad_rows, rows=[_full(dmod_rows)], tot_outs=[(1, ada_cols)])

    partial_of = {"norm1_g": g_norm1, "q_latent_g": g_gq, "kv_latent_g": g_gkv, "qk_norm_q_g": g_gqn,
                  "qk_norm_k_g": g_gkn, "conv_b": g_conv_b, "conv_ln_g": g_ln_g, "conv_ln_b": g_ln_b, "norm2_g": g_norm2}
    names = [n for n in REPLICATED if n != "b_ada"]
    pieces = [_pad_lanes(partial_of[n], -(-partial_of[n].shape[1] // LANES) * LANES) for n in names] + [g_b_cols, sq_err]
    widths = [p.shape[1] for p in pieces]
    small = jnp.concatenate(pieces, axis=1)
    small = _pad_lanes(small, -(-small.shape[1] // (8 * LANES)) * 8 * LANES).reshape(-1, LANES)
    (small_s,) = _exchange("gather_small_grads", [(small, True)])
    small_s = small_s.reshape(N_DEV, 1, -1)
    parts = {}
    off = 0
    for n, wd in zip(names, widths):
        parts[n] = small_s[:, :, off:off + vec[n].shape[1]]
        off += wd
    parts["b_ada"] = small_s[:, 0, off:off + ada_cols].reshape(1, 1, N_DEV * ada_cols)
    loss = jnp.sum(small_s[:, 0, off + ada_cols]) * (0.5 / d)
    g_in_mine = _sum_parts("sum_w_in", p_in).T
    parts.update({"w_ada": g_ada[None], "w_in": g_in_mine[None], "w_uq": p_uq, "w_ukv": p_ukv, "w_o_mla": p_o,
                  "conv_w": p_conv_w, "w_pw_out": p_pw, "w_out": p_out, "w_ff1": p_ff1, "w_ff2": p_ff2})
    transposed = ("w_o_mla", "w_pw_out", "w_ff1")

    grad_out, delta_out, m_out, v_out = [], [], [], []
    for n in WEIGHTS:
        shape2 = local[n].shape if local[n].ndim == 2 else (1, local[n].shape[0])
        g_w, d_w, n_m, n_v = _adamw("adamw_" + n, local[n].reshape(shape2), parts[n], given["m_" + n].reshape(shape2),
                                    given["v_" + n].reshape(shape2), transposed=n in transposed)
        full_shape = given[n].shape
        grad_out.append(g_w.reshape(full_shape))
        delta_out.append(d_w.reshape(full_shape))
        m_out.append(n_m.reshape(full_shape))
        v_out.append(n_v.reshape(full_shape))
    return (loss, grad_x.reshape(n_b, seq, d), *grad_out, *delta_out, *m_out, *v_out)
```

```python
import functools

import jax
import jax.numpy as jnp
from jax import lax
from jax.experimental import pallas as pl
from jax.experimental.pallas import tpu as pltpu

F32 = jnp.float32
BF16 = jnp.bfloat16

N_DEV = 8
EPS = 1e-6
N_HEADS = 8
QK_HEAD_DIM = 96
QK_NOPE_DIM = 64
QK_ROPE_DIM = 32
V_HEAD_DIM = 64
HEAD_PAD = 128
Q_LORA = 256
KV_LORA = 128
CONV_CH = 512
CONV_WIDTH = 31
CONV_HALO = 32
CONV_TAIL = 8
CHUNK = 64
ROPE_THETA = 10000.0
OFF_Q = Q_LORA
OFF_KV = OFF_Q + KV_LORA
OFF_KR = OFF_KV + QK_ROPE_DIM
OFF_GLU = OFF_KR + 2 * CONV_CH
ADA_CHUNKS = 6
ADAM_LR = 0.001
ADAM_B1 = 0.9
ADAM_B2 = 0.999
ADAM_EPS = 1e-08
ADAM_WD = 0.01
ADAM_STEP = 10
LANES = 128
VMEM_LIMIT = 56 * 1024 * 1024
NEG_BIG = -1e30
ATT_HEADS = 4
ATT_TILE = 512
PREP_TILE = 1024
ATT_SCALE = QK_HEAD_DIM ** -0.5
LOG2E = 1.4426950408889634
LN2 = 0.6931471805599453
QK_SCALE = ATT_SCALE * LOG2E
ADAM_ROWS = 256
ROWS_PAD = 16

REPLICATED = ("b_ada", "norm1_g", "q_latent_g", "kv_latent_g", "qk_norm_q_g", "qk_norm_k_g", "conv_b", "conv_ln_g",
              "conv_ln_b", "norm2_g")
WEIGHTS = ("w_ada", "b_ada", "norm1_g", "w_in", "q_latent_g", "w_uq", "kv_latent_g", "w_ukv", "qk_norm_q_g",
           "qk_norm_k_g", "w_o_mla", "conv_w", "conv_b", "conv_ln_g", "conv_ln_b", "w_pw_out", "w_out", "norm2_g",
           "w_ff1", "w_ff2")


def _tile(dim, pref):
    if dim <= pref:
        return dim
    t = (pref // LANES) * LANES
    while dim % t:
        t -= LANES
    return t


def _params(semantics):
    return pltpu.CompilerParams(dimension_semantics=semantics, vmem_limit_bytes=VMEM_LIMIT)


def _sigmoid(v):
    return 1.0 / (1.0 + jnp.exp(-v))


def _silu(v):
    return v * _sigmoid(v)


def _relu2(v):
    return jnp.square(jnp.maximum(v, 0.0))


_DIMS = {"nn": (((1,), (0,)), ((), ())), "nt": (((1,), (1,)), ((), ())), "tn": (((0,), (0,)), ((), ()))}


def _mm(name, a, b, mode, out_dtype, *, a_fn=None, epi=None, epi_in=(), tm=1024, tn=1024, tk=1024):
    if mode == "nn":
        (m, k), n = a.shape, b.shape[1]
    elif mode == "nt":
        (m, k), n = a.shape, b.shape[0]
    else:
        (k, m), n = a.shape, b.shape[1]
    tm, tn, tk = _tile(m, tm), _tile(n, tn), _tile(k, tk)
    nk = k // tk
    a_spec = (pl.BlockSpec((tk, tm), lambda i, j, kk: (kk, i)) if mode == "tn"
              else pl.BlockSpec((tm, tk), lambda i, j, kk: (i, kk)))
    b_spec = (pl.BlockSpec((tn, tk), lambda i, j, kk: (j, kk)) if mode == "nt"
              else pl.BlockSpec((tk, tn), lambda i, j, kk: (kk, j)))
    o_spec = e_spec = pl.BlockSpec((tm, tn), lambda i, j, kk: (i, j))
    out_shape = jax.ShapeDtypeStruct((m, n), out_dtype)
    n_epi = len(epi_in)

    def body(a_ref, b_ref, *rest):
        epi_refs, o_ref, acc_ref = rest[:n_epi], rest[n_epi], rest[n_epi + 1]
        kk = pl.program_id(2)

        @pl.when(kk == 0)
        def _():
            acc_ref[...] = jnp.zeros_like(acc_ref)

        av = a_ref[...]
        if a_fn is not None:
            av = a_fn(av.astype(F32))
        acc_ref[...] += lax.dot_general(av.astype(BF16), b_ref[...].astype(BF16), _DIMS[mode],
                                        preferred_element_type=F32)

        @pl.when(kk == nk - 1)
        def _():
            acc = acc_ref[...]
            if epi is not None:
                acc = epi(acc, *[r[...].astype(F32) for r in epi_refs])
            o_ref[...] = acc.astype(out_dtype)

    return pl.pallas_call(
        body, name=name, grid=(m // tm, n // tn, nk),
        in_specs=[a_spec, b_spec] + [e_spec] * n_epi, out_specs=o_spec, out_shape=out_shape,
        scratch_shapes=[pltpu.VMEM((tm, tn), F32)],
        compiler_params=_params(("parallel", "parallel", "arbitrary")),
    )(a, b, *epi_in)


def _rowwise(name, fn, n_rows, seq, rows, bats=(), consts=(), outs=(), bat_outs=(), tot_outs=(), tm=256):
    tm = min(tm, seq)
    per_seq = seq // tm
    n_b = n_rows // seq
    nr, nb, nc, no, nbo, nto = len(rows), len(bats), len(consts), len(outs), len(bat_outs), len(tot_outs)

    def body(*refs):
        i = pl.program_id(0)
        r_in = [r[...] for r in refs[:nr]]
        b_in = [r[0] for r in refs[nr:nr + nb]]
        c_in = [r[...] for r in refs[nr + nb:nr + nb + nc]]
        o_refs = refs[nr + nb + nc:nr + nb + nc + no]
        bo_refs = refs[nr + nb + nc + no:nr + nb + nc + no + nbo]
        to_refs = refs[nr + nb + nc + no + nbo:]
        o_val, bo_val, to_val = fn(r_in, b_in, c_in)
        for r, v in zip(o_refs, o_val):
            r[...] = v.astype(r.dtype)
        if nbo:
            @pl.when(i % per_seq == 0)
            def _():
                for r in bo_refs:
                    r[...] = jnp.zeros_like(r)

            for r, v in zip(bo_refs, bo_val):
                r[0] += v
        if nto:
            @pl.when(i == 0)
            def _():
                for r in to_refs:
                    r[...] = jnp.zeros_like(r)

            for r, v in zip(to_refs, to_val):
                r[...] += v

    in_specs = [pl.BlockSpec((tm, w), functools.partial(lambda cb, i: (i, cb), cb)) for (_, w, cb) in rows]
    in_specs += [pl.BlockSpec((1, 1, bt.shape[2]), lambda i: (i // per_seq, 0, 0)) for bt in bats]
    in_specs += [pl.BlockSpec(ct.shape, lambda i: (0, 0)) for ct in consts]
    out_specs = [pl.BlockSpec((tm, w), lambda i: (i, 0)) for (w, _) in outs]
    out_specs += [pl.BlockSpec((1, 1, w), lambda i: (i // per_seq, 0, 0)) for w in bat_outs]
    out_specs += [pl.BlockSpec(shp, lambda i: (0, 0)) for shp in tot_outs]
    out_shape = [jax.ShapeDtypeStruct((n_rows, w), dt) for (w, dt) in outs]
    out_shape += [jax.ShapeDtypeStruct((n_b, 1, w), F32) for w in bat_outs]
    out_shape += [jax.ShapeDtypeStruct(shp, F32) for shp in tot_outs]
    res = pl.pallas_call(
        body, name=name, grid=(n_rows // tm,), in_specs=in_specs, out_specs=out_specs, out_shape=out_shape,
        compiler_params=_params(("arbitrary",)),
    )(*[r[0] for r in rows], *bats, *consts)
    return res


def _full(arr):
    return (arr, arr.shape[1], 0)


def _mm_rows(name, a_rows, a_fn, w, mode, fn, n_rows, seq, rows=(), bats=(), consts=(), outs=(), bat_outs=(),
             tot_outs=(), a_out=None, tm=512, tk=1024):
    tm = min(tm, seq)
    per_seq = seq // tm
    n_b = n_rows // seq
    k = a_rows[0][1]
    if mode == "nt":
        n_out, tk = w.shape[0], _tile(k, tk)
        w_spec = pl.BlockSpec((n_out, tk), lambda i, kk: (0, kk))
    else:
        n_out, tk = w.shape[1], _tile(k, tk)
        w_spec = pl.BlockSpec((tk, n_out), lambda i, kk: (kk, 0))
    nk = k // tk
    na, nr, nb, nc = len(a_rows), len(rows), len(bats), len(consts)
    n_extra = 0 if a_out is None else 1
    no, nbo, nto = len(outs), len(bat_outs), len(tot_outs)

    def body(*refs):
        i, kk = pl.program_id(0), pl.program_id(1)
        a_refs, w_ref = refs[:na], refs[na]
        pos = na + 1
        r_refs, b_refs, c_refs = refs[pos:pos + nr], refs[pos + nr:pos + nr + nb], refs[pos + nr + nb:pos + nr + nb + nc]
        pos += nr + nb + nc
        ao_refs = refs[pos:pos + n_extra]
        pos += n_extra
        o_refs, bo_refs, to_refs = refs[pos:pos + no], refs[pos + no:pos + no + nbo], refs[pos + no + nbo:pos + no + nbo + nto]
        acc_ref = refs[pos + no + nbo + nto]

        @pl.when(kk == 0)
        def _():
            acc_ref[...] = jnp.zeros_like(acc_ref)

        tiles = [r[...] for r in a_refs]
        av = a_fn([t.astype(F32) for t in tiles]) if a_fn is not None else tiles[0]
        av = av.astype(BF16)
        if n_extra:
            ao_refs[0][...] = av.astype(ao_refs[0].dtype)
        acc_ref[...] += lax.dot_general(av, w_ref[...].astype(BF16), _DIMS[mode], preferred_element_type=F32)

        @pl.when(kk == nk - 1)
        def _():
            o_val, bo_val, to_val = fn(acc_ref[...], [r[...] for r in r_refs], [r[0] for r in b_refs],
                                       [r[...] for r in c_refs])
            for r, v in zip(o_refs, o_val):
                r[...] = v.astype(r.dtype)
            if nbo:
                @pl.when(i % per_seq == 0)
                def _():
                    for r in bo_refs:
                        r[...] = jnp.zeros_like(r)

                for r, v in zip(bo_refs, bo_val):
                    r[0] += v
            if nto:
                @pl.when(i == 0)
                def _():
                    for r in to_refs:
                        r[...] = jnp.zeros_like(r)

                for r, v in zip(to_refs, to_val):
                    r[...] += v

    in_specs = [pl.BlockSpec((tm, tk), functools.partial(lambda cb, i, kk: (i, kk + cb), cb)) for (_, _, cb) in a_rows]
    in_specs += [w_spec]
    in_specs += [pl.BlockSpec((tm, wd), functools.partial(lambda cb, i, kk: (i, cb), cb)) for (_, wd, cb) in rows]
    in_specs += [pl.BlockSpec((1, 1, bt.shape[2]), lambda i, kk: (i // per_seq, 0, 0)) for bt in bats]
    in_specs += [pl.BlockSpec(ct.shape, lambda i, kk: (0, 0)) for ct in consts]
    out_specs = [pl.BlockSpec((tm, tk), lambda i, kk: (i, kk))] * n_extra
    out_specs += [pl.BlockSpec((tm, wd), lambda i, kk: (i, 0)) for (wd, _) in outs]
    out_specs += [pl.BlockSpec((1, 1, wd), lambda i, kk: (i // per_seq, 0, 0)) for wd in bat_outs]
    out_specs += [pl.BlockSpec(shp, lambda i, kk: (0, 0)) for shp in tot_outs]
    out_shape = [jax.ShapeDtypeStruct((n_rows, k), a_out)] if n_extra else []
    out_shape += [jax.ShapeDtypeStruct((n_rows, wd), dt) for (wd, dt) in outs]
    out_shape += [jax.ShapeDtypeStruct((n_b, 1, wd), F32) for wd in bat_outs]
    out_shape += [jax.ShapeDtypeStruct(shp, F32) for shp in tot_outs]
    return pl.pallas_call(
        body, name=name, grid=(n_rows // tm, nk), in_specs=in_specs, out_specs=out_specs, out_shape=out_shape,
        scratch_shapes=[pltpu.VMEM((tm, n_out), F32)],
        compiler_params=_params(("arbitrary", "arbitrary")),
    )(*[a for a, _, _ in a_rows], w, *[r[0] for r in rows], *bats, *consts)


def _norm_mod(x, g, scale, shift):
    r = lax.rsqrt(jnp.mean(x * x, axis=-1, keepdims=True) + EPS)
    xh = x * r
    return xh * g * (1.0 + scale) + shift


def _norm_mod_bwd(x, g, scale, dh):
    r = lax.rsqrt(jnp.mean(x * x, axis=-1, keepdims=True) + EPS)
    xh = x * r
    dn = dh * (1.0 + scale)
    dxh = dn * g
    dx = r * (dxh - xh * jnp.mean(dxh * xh, axis=-1, keepdims=True))
    dscale = jnp.sum(dh * xh * g, axis=0, keepdims=True)
    dshift = jnp.sum(dh, axis=0, keepdims=True)
    dg = jnp.sum(dn * xh, axis=0, keepdims=True)
    return dx, dscale, dshift, dg


def _rms(v, g):
    r = lax.rsqrt(jnp.mean(v * v, axis=-1, keepdims=True) + EPS)
    return v * r * g


def _rms_bwd(v, g, dy):
    r = lax.rsqrt(jnp.mean(v * v, axis=-1, keepdims=True) + EPS)
    vh = v * r
    dvh = dy * g
    dv = r * (dvh - vh * jnp.mean(dvh * vh, axis=-1, keepdims=True))
    return dv, jnp.sum(dy * vh, axis=0, keepdims=True)


def _head_norm(v, g):
    r = lax.rsqrt(jnp.sum(v * v, axis=-1, keepdims=True) * (1.0 / QK_HEAD_DIM) + EPS)
    return v * r * g


def _head_norm_bwd(v, g, dy):
    r = lax.rsqrt(jnp.sum(v * v, axis=-1, keepdims=True) * (1.0 / QK_HEAD_DIM) + EPS)
    vh = v * r
    dvh = dy * g
    dv = r * (dvh - vh * (jnp.sum(dvh * vh, axis=-1, keepdims=True) * (1.0 / QK_HEAD_DIM)))
    return dv, jnp.sum(dy * vh, axis=0, keepdims=True)


def _rope(v, cos, sin_lo, sin_hi):
    return v * cos + pltpu.roll(v, HEAD_PAD - 16, 1) * sin_lo + pltpu.roll(v, 16, 1) * sin_hi


def _rope_bwd(g, cos, sin_lo, sin_hi):
    return g * cos + pltpu.roll(g * sin_lo, 16, 1) + pltpu.roll(g * sin_hi, HEAD_PAD - 16, 1)


def _mla_prep_fwd(zsm, wuq, wukv, gq, gkv, gqn, gkn, rope, n_b, seq):
    n_rows = n_b * seq
    tm = min(PREP_TILE, seq)
    per_seq = seq // tm
    att_tile = min(ATT_TILE, seq)
    k_cols = N_HEADS * HEAD_PAD

    def body(z_ref, wuq_ref, wukv_ref, gq_ref, gkv_ref, gqn_ref, gkn_ref, c_ref, s1_ref, s2_ref,
             q_ref, k_ref, v_ref, kt_ref):
        z = z_ref[...]
        qn = _rms(z[:, :Q_LORA], gq_ref[...]).astype(BF16)
        kvn = _rms(z[:, Q_LORA:Q_LORA + KV_LORA], gkv_ref[...]).astype(BF16)
        krp = z[:, Q_LORA + KV_LORA:]
        cos, s1, s2 = c_ref[...], s1_ref[...], s2_ref[...]
        q_all = jnp.dot(qn, wuq_ref[...], preferred_element_type=F32)
        kv_all = jnp.dot(kvn, wukv_ref[...], preferred_element_type=F32)
        for h in range(N_HEADS):
            cols = slice(h * HEAD_PAD, (h + 1) * HEAD_PAD)
            q_ref[0, h] = (_rope(_head_norm(q_all[:, cols], gqn_ref[...]), cos, s1, s2) * QK_SCALE).astype(BF16)
            kh = _rope(_head_norm(kv_all[:, cols] + krp, gkn_ref[...]), cos, s1, s2)
            k_ref[0, h] = kh.astype(BF16)
            for part in range(tm // att_tile):
                kt_ref[0, h, part] = kh[part * att_tile:(part + 1) * att_tile].T.astype(BF16)
            v_ref[0, h] = kv_all[:, k_cols + h * HEAD_PAD:k_cols + (h + 1) * HEAD_PAD].astype(BF16)

    whole2 = lambda arr: pl.BlockSpec(arr.shape, lambda i: (0, 0))
    rope_spec = pl.BlockSpec((tm, HEAD_PAD), lambda i: (i % per_seq, 0))
    head_spec = pl.BlockSpec((1, N_HEADS, tm, HEAD_PAD), lambda i: (i // per_seq, 0, i % per_seq, 0))
    head_shape = jax.ShapeDtypeStruct((n_b, N_HEADS, seq, HEAD_PAD), BF16)
    t_spec = pl.BlockSpec((1, N_HEADS, tm // att_tile, HEAD_PAD, att_tile), lambda i: (i // per_seq, 0, i % per_seq, 0, 0))
    t_shape = jax.ShapeDtypeStruct((n_b, N_HEADS, seq // att_tile, HEAD_PAD, att_tile), BF16)
    return pl.pallas_call(
        body, name="mla_prep_fwd", grid=(n_rows // tm,),
        in_specs=[pl.BlockSpec((tm, 512), lambda i: (i, 0)), whole2(wuq), whole2(wukv),
                  whole2(gq), whole2(gkv), whole2(gqn), whole2(gkn), rope_spec, rope_spec, rope_spec],
        out_specs=[head_spec] * 3 + [t_spec], out_shape=[head_shape] * 3 + [t_shape],
        compiler_params=_params(("parallel",)),
    )(zsm, wuq, wukv, gq, gkv, gqn, gkn, *rope)


def _mla_prep_bwd(zsm, dq, dk, dv, wuq, wukv, gq, gkv, gqn, gkn, rope, n_b, seq):
    n_rows = n_b * seq
    tm = min(PREP_TILE, seq)
    per_seq = seq // tm
    tn_dims = _DIMS["tn"]
    nt_dims = _DIMS["nt"]
    k_cols = N_HEADS * HEAD_PAD

    def body(z_ref, dq_ref, dk_ref, dv_ref, wuq_ref, wukv_ref, gq_ref, gkv_ref, gqn_ref, gkn_ref,
             c_ref, s1_ref, s2_ref, dz_ref, dwuq_ref, dwukv_ref, dgq_ref, dgkv_ref, dgqn_ref, dgkn_ref):
        @pl.when(pl.program_id(0) == 0)
        def _():
            for r in (dwuq_ref, dwukv_ref, dgq_ref, dgkv_ref, dgqn_ref, dgkn_ref):
                r[...] = jnp.zeros_like(r)

        z = z_ref[...]
        zq, zkv, krp = z[:, :Q_LORA], z[:, Q_LORA:Q_LORA + KV_LORA], z[:, Q_LORA + KV_LORA:]
        qn = _rms(zq, gq_ref[...]).astype(BF16)
        kvn = _rms(zkv, gkv_ref[...]).astype(BF16)
        cos, s1, s2 = c_ref[...], s1_ref[...], s2_ref[...]
        lane = lax.broadcasted_iota(jnp.int32, (tm, HEAD_PAD), 1)
        rope_lanes = (lane >= QK_NOPE_DIM) & (lane < QK_HEAD_DIM)
        q_all = jnp.dot(qn, wuq_ref[...], preferred_element_type=F32)
        k_all = jnp.dot(kvn, wukv_ref[:, :k_cols], preferred_element_type=F32)
        dkrp = jnp.zeros((tm, HEAD_PAD), F32)
        dgqn = jnp.zeros((1, HEAD_PAD), F32)
        dgkn = jnp.zeros((1, HEAD_PAD), F32)
        dq_heads, dk_heads = [], []
        for h in range(N_HEADS):
            cols = slice(h * HEAD_PAD, (h + 1) * HEAD_PAD)
            dqh, dg = _head_norm_bwd(q_all[:, cols], gqn_ref[...],
                                     _rope_bwd(dq_ref[0, h].astype(F32) * ATT_SCALE, cos, s1, s2))
            dgqn += dg
            dq_heads.append(dqh.astype(BF16))
            dkh, dg = _head_norm_bwd(k_all[:, cols] + krp, gkn_ref[...], _rope_bwd(dk_ref[0, h].astype(F32), cos, s1, s2))
            dgkn += dg
            dkrp += jnp.where(rope_lanes, dkh, 0.0)
            dk_heads.append(dkh.astype(BF16))
        dq_all = jnp.concatenate(dq_heads, axis=1)
        dkv_all = jnp.concatenate(dk_heads + [dv_ref[0, h] for h in range(N_HEADS)], axis=1)
        dwuq_ref[...] += lax.dot_general(qn, dq_all, tn_dims, preferred_element_type=F32)
        dqn = lax.dot_general(dq_all, wuq_ref[...], nt_dims, preferred_element_type=F32)
        dwukv_ref[...] += lax.dot_general(kvn, dkv_all, tn_dims, preferred_element_type=F32)
        dkvn = lax.dot_general(dkv_all, wukv_ref[...], nt_dims, preferred_element_type=F32)
        dzq, dg = _rms_bwd(zq, gq_ref[...], dqn)
        dgq_ref[...] += dg
        dzkv, dg = _rms_bwd(zkv, gkv_ref[...], dkvn)
        dgkv_ref[...] += dg
        dgqn_ref[...] += dgqn
        dgkn_ref[...] += dgkn
        dz_ref[:, :Q_LORA] = dzq.astype(dz_ref.dtype)
        dz_ref[:, Q_LORA:Q_LORA + KV_LORA] = dzkv.astype(dz_ref.dtype)
        dz_ref[:, Q_LORA + KV_LORA:] = dkrp.astype(dz_ref.dtype)

    whole2 = lambda arr: pl.BlockSpec(arr.shape, lambda i: (0, 0))
    rope_spec = pl.BlockSpec((tm, HEAD_PAD), lambda i: (i % per_seq, 0))
    head_spec = pl.BlockSpec((1, N_HEADS, tm, HEAD_PAD), lambda i: (i // per_seq, 0, i % per_seq, 0))
    row_spec = pl.BlockSpec((tm, 512), lambda i: (i, 0))
    return pl.pallas_call(
        body, name="mla_prep_bwd", grid=(n_rows // tm,),
        in_specs=[row_spec, head_spec, head_spec, head_spec, whole2(wuq), whole2(wukv),
                  whole2(gq), whole2(gkv), whole2(gqn), whole2(gkn), rope_spec, rope_spec, rope_spec],
        out_specs=[row_spec, whole2(wuq), whole2(wukv), whole2(gq), whole2(gkv), whole2(gqn), whole2(gkn)],
        out_shape=[jax.ShapeDtypeStruct((n_rows, 512), BF16),
                   jax.ShapeDtypeStruct(wuq.shape, F32), jax.ShapeDtypeStruct(wukv.shape, F32),
                   jax.ShapeDtypeStruct(gq.shape, F32), jax.ShapeDtypeStruct(gkv.shape, F32),
                   jax.ShapeDtypeStruct(gqn.shape, F32), jax.ShapeDtypeStruct(gkn.shape, F32)],
        compiler_params=_params(("arbitrary",)),
    )(zsm, dq, dk, dv, wuq, wukv, gq, gkv, gqn, gkn, *rope)


HBM_SPEC = pl.BlockSpec(memory_space=pltpu.HBM)


def _xchg_out_shapes(bufs):
    return [jax.ShapeDtypeStruct((N_DEV,) + (a.shape if gather else a.shape[1:]), a.dtype) for a, gather in bufs]


def _xchg_scratch(n_buf):
    return [pltpu.SemaphoreType.DMA((n_buf * (N_DEV - 1),)), pltpu.SemaphoreType.DMA((n_buf * (N_DEV - 1),)),
            pltpu.SemaphoreType.DMA((n_buf,))]


def _xchg_copies(src_refs, dst_refs, gathers, send_sems, recv_sems, local_sems):
    x, y, c = lax.axis_index("x"), lax.axis_index("y"), lax.axis_index("c")
    me = 4 * x + 2 * y + c
    local, starts, arrivals = [], [], []
    for bi, (src, dst, gather) in enumerate(zip(src_refs, dst_refs, gathers)):
        local.append(pltpu.make_async_copy(src if gather else src.at[me], dst.at[me], local_sems.at[bi]))
        for kk in range(1, N_DEV):
            px = 1 - x if kk & 4 else x
            py = 1 - y if kk & 2 else y
            pc = 1 - c if kk & 1 else c
            pid = 4 * px + 2 * py + pc
            sem = bi * (N_DEV - 1) + kk - 1
            starts.append(pltpu.make_async_remote_copy(
                src_ref=src if gather else src.at[pid], dst_ref=dst.at[me],
                send_sem=send_sems.at[sem], recv_sem=recv_sems.at[sem],
                device_id=(px, py, pc), device_id_type=pl.DeviceIdType.MESH))
            arrivals.append(pltpu.make_async_remote_copy(
                src_ref=src if gather else src.at[me], dst_ref=dst.at[pid],
                send_sem=send_sems.at[sem], recv_sem=recv_sems.at[sem],
                device_id=(px, py, pc), device_id_type=pl.DeviceIdType.MESH))
    return local, starts, arrivals


def _xchg_start(copies):
    local, sends, _ = copies
    for cp in local + sends:
        cp.start()


def _xchg_finish(copies):
    local, sends, arrivals = copies
    for cp in arrivals:
        cp.wait_recv()
    for cp in sends:
        cp.wait_send()
    for cp in local:
        cp.wait()


def _gather_by_chip(src_refs, dst_refs, send_sems, recv_sems, local_sems):
    x, y, c = lax.axis_index("x"), lax.axis_index("y"), lax.axis_index("c")
    me = 4 * x + 2 * y + c
    sibling = (x, y, 1 - c)

    def place(kk):
        px = 1 - x if kk & 4 else x
        py = 1 - y if kk & 2 else y
        pc = 1 - c if kk & 1 else c
        return (px, py, pc), 4 * px + 2 * py + pc

    def copy(bi, kk, src, dst, to):
        sem = bi * (N_DEV - 1) + kk - 1
        return pltpu.make_async_remote_copy(src_ref=src, dst_ref=dst, send_sem=send_sems.at[sem],
                                            recv_sem=recv_sems.at[sem], device_id=to, device_id_type=pl.DeviceIdType.MESH)

    local, sends = [], []
    for bi, (src, dst) in enumerate(zip(src_refs, dst_refs)):
        local.append(pltpu.make_async_copy(src, dst.at[me], local_sems.at[bi]))
        sends += [copy(bi, kk, src, dst.at[me], place(kk)[0]) for kk in (1, 2, 4, 6)]
    for cp in local + sends:
        cp.start()
    for kk in (2, 4, 6):
        for bi, (src, dst) in enumerate(zip(src_refs, dst_refs)):
            dev, pid = place(kk)
            copy(bi, kk, src, dst.at[pid], dev).wait_recv()
            passed = copy(bi, kk | 1, dst.at[pid], dst.at[pid], sibling)
            passed.start()
            sends.append(passed)
    for kk in (1, 3, 5, 7):
        for bi, (src, dst) in enumerate(zip(src_refs, dst_refs)):
            dev, pid = place(kk)
            copy(bi, kk, src, dst.at[pid], sibling).wait_recv()
    for cp in sends:
        cp.wait_send()
    for cp in local:
        cp.wait()


def _exchange(name, bufs, by_chip=False):
    n_buf = len(bufs)
    gathers = [g for _, g in bufs]
    assert not by_chip or all(gathers)

    def body(*refs):
        srcs, dsts = refs[:n_buf], refs[n_buf:2 * n_buf]
        if by_chip:
            _gather_by_chip(srcs, dsts, *refs[2 * n_buf:])
            return
        copies = _xchg_copies(srcs, dsts, gathers, *refs[2 * n_buf:])
        _xchg_start(copies)
        _xchg_finish(copies)

    return pl.pallas_call(
        body, name=name, out_shape=_xchg_out_shapes(bufs),
        in_specs=[HBM_SPEC] * n_buf, out_specs=[HBM_SPEC] * n_buf, scratch_shapes=_xchg_scratch(n_buf),
    )(*[a for a, _ in bufs])


def _chunk_mask(t, keys_first):
    key = lax.broadcasted_iota(jnp.int32, (t, t), 0 if keys_first else 1) // CHUNK
    query = lax.broadcasted_iota(jnp.int32, (t, t), 1 if keys_first else 0) // CHUNK
    return query >= key


def _grid_ends(grid):
    ids = [pl.program_id(ax) for ax in range(len(grid))]
    first = functools.reduce(jnp.logical_and, [i == 0 for i in ids])
    last = functools.reduce(jnp.logical_and, [i == g - 1 for i, g in zip(ids, grid)])
    return first, last


def _attn_fwd(q, k, v, bufs, n_b, seq):
    tq = min(ATT_TILE, seq)
    nq = seq // tq
    nt_dims = _DIMS["nt"]
    hpb = ATT_HEADS
    grid = (n_b, N_HEADS // hpb, nq)
    n_buf = len(bufs)
    gathers = [g for _, g in bufs]
    sum_lane = [HEAD_PAD - 1 if hh % 2 == 0 else 0 for hh in range(hpb)]

    def body(q_ref, k_ref, v_ref, *rest):
        srcs, (o_ref, lse_ref), dsts = rest[:n_buf], rest[n_buf:n_buf + 2], rest[n_buf + 2:2 * n_buf + 2]
        s_ref = rest[2 * n_buf + 2]
        copies = _xchg_copies(srcs, dsts, gathers, *rest[2 * n_buf + 3:])
        first, last = _grid_ends(grid)
        pl.when(first)(functools.partial(_xchg_start, copies))

        qi = pl.program_id(2)
        mask = _chunk_mask(tq, keys_first=False)
        lane_row = lax.broadcasted_iota(jnp.int32, (1, HEAD_PAD), 1)
        ones = [(lane_row == sum_lane[hh]).astype(BF16) for hh in range(hpb)]
        qs = [q_ref[0, hh] for hh in range(hpb)]

        def score_step(j, tops, masked):
            rows = pl.ds(j * tq, tq)
            out = []
            for hh in range(hpb):
                s = lax.dot_general(qs[hh], k_ref[0, hh, rows, :], nt_dims, preferred_element_type=F32)
                if masked:
                    s = jnp.where(mask, s, NEG_BIG)
                s_ref[hh, j] = s
                out.append(jnp.maximum(tops[hh], s))
            return tuple(out)

        def value_step(j, accs, ms):
            rows = pl.ds(j * tq, tq)
            out = []
            for hh in range(hpb):
                p = jnp.exp2(s_ref[hh, j] - ms[hh]).astype(BF16)
                out.append(accs[hh] + jnp.dot(p, v_ref[0, hh, rows, :] + ones[hh], preferred_element_type=F32))
            return tuple(out)

        def visible_blocks(n_blk):
            def run():
                tops = tuple(jnp.full((tq, tq), NEG_BIG, F32) for _ in range(hpb))
                for j in range(n_blk):
                    tops = score_step(j, tops, masked=j == n_blk - 1)
                ms = tuple(jnp.max(top, axis=-1, keepdims=True) for top in tops)
                accs = tuple(jnp.zeros((tq, HEAD_PAD), F32) for _ in range(hpb))
                for j in range(n_blk):
                    accs = value_step(j, accs, ms)
                return ms, accs
            return run

        ms, accs = lax.switch(qi, [visible_blocks(n + 1) for n in range(nq)])
        carry = list(zip(ms, accs))
        lane = lax.broadcasted_iota(jnp.int32, (tq, HEAD_PAD), 1)
        for pair in range(hpb // 2):
            outs = []
            for hh in (2 * pair, 2 * pair + 1):
                m, acc = carry[hh]
                l = jnp.sum(jnp.where(lane == sum_lane[hh], acc, 0.0), axis=-1, keepdims=True)
                outs.append(acc * (1.0 / l))
                lse_ref[0, hh] = jnp.broadcast_to(m + jnp.log2(l), (tq, HEAD_PAD)).T[0:8, :]
            o_ref[0, :, pair * HEAD_PAD:(pair + 1) * HEAD_PAD] = jnp.where(lane < V_HEAD_DIM, outs[0], outs[1]).astype(BF16)

        pl.when(last)(functools.partial(_xchg_finish, copies))

    kv_spec = pl.BlockSpec((1, hpb, seq, HEAD_PAD), lambda b, hb, i: (b, hb, 0, 0))
    q_spec = pl.BlockSpec((1, hpb, tq, HEAD_PAD), lambda b, hb, i: (b, hb, i, 0))
    res = pl.pallas_call(
        body, name="attn_fwd", grid=grid,
        in_specs=[q_spec, kv_spec, kv_spec] + [HBM_SPEC] * n_buf,
        out_specs=[pl.BlockSpec((1, tq, hpb * V_HEAD_DIM), lambda b, hb, i: (b, i, hb)),
                   pl.BlockSpec((1, hpb, 8, tq), lambda b, hb, i: (b, hb, 0, i))] + [HBM_SPEC] * n_buf,
        out_shape=[jax.ShapeDtypeStruct((n_b, seq, N_HEADS * V_HEAD_DIM), BF16),
                   jax.ShapeDtypeStruct((n_b, N_HEADS, 8, seq), F32)] + _xchg_out_shapes(bufs),
        scratch_shapes=[pltpu.VMEM((hpb, nq, tq, tq), F32)] + _xchg_scratch(n_buf),
        compiler_params=_params(("arbitrary", "arbitrary", "arbitrary")),
    )(q, k, v, *[a for a, _ in bufs])
    return res[0], res[1], res[2:]


def _attn_bwd(q, k, v, kt, do, o, lse, bufs, n_b, seq):
    tq = min(ATT_TILE, seq)
    nq = seq // tq
    nt_dims = _DIMS["nt"]
    hpb = ATT_HEADS
    grid = (n_b, N_HEADS // hpb, nq)
    n_buf = len(bufs)
    gathers = [g for _, g in bufs]

    def body(q_ref, k_ref, v_ref, kt_ref, do_ref, o_ref, lse_ref, *rest):
        srcs, (dq_ref, dk_ref, dv_ref), dsts = rest[:n_buf], rest[n_buf:n_buf + 3], rest[n_buf + 3:2 * n_buf + 3]
        dk_acc, dv_acc = rest[2 * n_buf + 3:2 * n_buf + 5]
        copies = _xchg_copies(srcs, dsts, gathers, *rest[2 * n_buf + 5:])
        first, last = _grid_ends(grid)
        pl.when(first)(functools.partial(_xchg_start, copies))

        qi = pl.program_id(2)

        @pl.when(qi == 0)
        def _():
            dk_acc[...] = jnp.zeros_like(dk_acc)
            dv_acc[...] = jnp.zeros_like(dv_acc)

        mask = _chunk_mask(tq, keys_first=True)
        lane = lax.broadcasted_iota(jnp.int32, (tq, HEAD_PAD), 1)
        qs, dos, deltas, lses = [], [], [], []
        for hh in range(hpb):
            cols = slice((hh // 2) * HEAD_PAD, (hh // 2 + 1) * HEAD_PAD)
            do_pair = do_ref[0, :, cols]
            prod = do_pair.astype(F32) * o_ref[0, :, cols].astype(F32)
            delta = jnp.sum(jnp.where(lane // V_HEAD_DIM == hh % 2, prod, 0.0), axis=-1, keepdims=True)
            qs.append(q_ref[0, hh])
            dos.append(do_pair)
            deltas.append(jnp.broadcast_to(delta, (tq, HEAD_PAD)).T[0:1, :])
            lses.append(lse_ref[0, hh][0:1, :])

        def step(j, dqs, masked):
            rows = pl.ds(j * tq, tq)
            out = []
            for hh in range(hpb):
                s = lax.dot_general(k_ref[0, hh, rows, :], qs[hh], nt_dims, preferred_element_type=F32)
                p = jnp.exp2(s - lses[hh])
                if masked:
                    p = jnp.where(mask, p, 0.0)
                dv_acc[hh, rows, :] += jnp.dot(p.astype(BF16), dos[hh], preferred_element_type=F32)
                dp = lax.dot_general(v_ref[0, hh, rows, :], dos[hh], nt_dims, preferred_element_type=F32)
                ds = (p * (dp - deltas[hh])).astype(BF16)
                dk_acc[hh, rows, :] += jnp.dot(ds, qs[hh], preferred_element_type=F32)
                out.append(dqs[hh] + jnp.dot(kt_ref[0, hh, j], ds, preferred_element_type=F32))
            return tuple(out)

        def visible_blocks(n_blk):
            def run():
                dqs = tuple(jnp.zeros((HEAD_PAD, tq), F32) for _ in range(hpb))
                for j in range(n_blk):
                    dqs = step(j, dqs, masked=j == n_blk - 1)
                return dqs
            return run

        dqs = lax.switch(qi, [visible_blocks(n + 1) for n in range(nq)])
        for hh in range(hpb):
            dq_ref[0, hh] = dqs[hh].T.astype(BF16)

        @pl.when(qi == nq - 1)
        def _():
            dk_ref[0] = (dk_acc[...] * LN2).astype(BF16)
            dv_ref[0] = dv_acc[...].astype(BF16)

        pl.when(last)(functools.partial(_xchg_finish, copies))

    full_spec = pl.BlockSpec((1, hpb, seq, HEAD_PAD), lambda b, hb, i: (b, hb, 0, 0))
    t_spec = pl.BlockSpec((1, hpb, nq, HEAD_PAD, tq), lambda b, hb, i: (b, hb, 0, 0, 0))
    q_spec = pl.BlockSpec((1, hpb, tq, HEAD_PAD), lambda b, hb, i: (b, hb, i, 0))
    o_spec = pl.BlockSpec((1, tq, hpb * V_HEAD_DIM), lambda b, hb, i: (b, i, hb))
    lse_spec = pl.BlockSpec((1, hpb, 8, tq), lambda b, hb, i: (b, hb, 0, i))
    head_shape = jax.ShapeDtypeStruct((n_b, N_HEADS, seq, HEAD_PAD), BF16)
    res = pl.pallas_call(
        body, name="attn_bwd", grid=grid,
        in_specs=[q_spec, full_spec, full_spec, t_spec, o_spec, o_spec, lse_spec] + [HBM_SPEC] * n_buf,
        out_specs=[q_spec, full_spec, full_spec] + [HBM_SPEC] * n_buf,
        out_shape=[head_shape] * 3 + _xchg_out_shapes(bufs),
        scratch_shapes=[pltpu.VMEM((hpb, seq, HEAD_PAD), F32), pltpu.VMEM((hpb, seq, HEAD_PAD), F32)]
        + _xchg_scratch(n_buf),
        compiler_params=_params(("arbitrary", "arbitrary", "arbitrary")),
    )(q, k, v, kt, do, o, lse, *[a for a, _ in bufs])
    return res[0], res[1], res[2], res[3:]


def _in_proj_fwd(x2, scale, shift, g, w_parts, z_dtypes, seq):
    n_rows, d = x2.shape
    tm = min(512, seq)
    per_seq = seq // tm
    n_part = len(w_parts)
    nt_dims = _DIMS["nt"]

    def body(x_ref, sc_ref, sh_ref, g_ref, *rest):
        w_refs, h_ref, z_refs = rest[:n_part], rest[n_part], rest[n_part + 1:]
        h = _norm_mod(x_ref[...], g_ref[...], sc_ref[0], sh_ref[0]).astype(BF16)
        h_ref[...] = h
        for w_ref, z_ref in zip(w_refs, z_refs):
            z_ref[...] = lax.dot_general(h, w_ref[...], nt_dims, preferred_element_type=F32).astype(z_ref.dtype)

    row = lambda width: pl.BlockSpec((tm, width), lambda i: (i, 0))
    bat = pl.BlockSpec((1, 1, d), lambda i: (i // per_seq, 0, 0))
    whole = lambda arr: pl.BlockSpec(arr.shape, lambda i: (0, 0))
    return pl.pallas_call(
        body, name="in_proj_fwd", grid=(n_rows // tm,),
        in_specs=[row(d), bat, bat, whole(g)] + [whole(w) for w in w_parts],
        out_specs=[row(d)] + [row(w.shape[0]) for w in w_parts],
        out_shape=[jax.ShapeDtypeStruct((n_rows, d), BF16)]
        + [jax.ShapeDtypeStruct((n_rows, w.shape[0]), dt) for w, dt in zip(w_parts, z_dtypes)],
        compiler_params=_params(("parallel",)),
    )(x2, scale, shift, g, *w_parts)
def _in_proj_bwd(parts, x2, dx1, scale, g, bufs, seq):
    n_rows, d = x2.shape
    tm = min(512, seq)
    per_seq = seq // tm
    grid = (n_rows // tm,)
    n_part, n_buf = len(parts), len(bufs)
    gathers = [gt for _, gt in bufs]

    def body(*refs):
        dz_refs, w_refs = refs[:n_part], refs[n_part:2 * n_part]
        x_ref, dx1_ref, sc_ref, g_ref = refs[2 * n_part:2 * n_part + 4]
        srcs = refs[2 * n_part + 4:2 * n_part + 4 + n_buf]
        gx_ref, dsc_ref, dsh_ref, dg_ref = refs[2 * n_part + 4 + n_buf:2 * n_part + 8 + n_buf]
        dsts = refs[2 * n_part + 8 + n_buf:2 * n_part + 8 + 2 * n_buf]
        copies = _xchg_copies(srcs, dsts, gathers, *refs[2 * n_part + 8 + 2 * n_buf:])
        first, last = _grid_ends(grid)
        pl.when(first)(functools.partial(_xchg_start, copies))

        i = pl.program_id(0)
        dh = None
        for dz_ref, w_ref in zip(dz_refs, w_refs):
            term = jnp.dot(dz_ref[...], w_ref[...], preferred_element_type=F32)
            dh = term if dh is None else dh + term
        dx, dsc, dsh, dg = _norm_mod_bwd(x_ref[...], g_ref[...], sc_ref[0], dh)
        gx_ref[...] = dx1_ref[...] + dx

        @pl.when(i % per_seq == 0)
        def _():
            dsc_ref[...] = jnp.zeros_like(dsc_ref)
            dsh_ref[...] = jnp.zeros_like(dsh_ref)

        @pl.when(i == 0)
        def _():
            dg_ref[...] = jnp.zeros_like(dg_ref)

        dsc_ref[0] += dsc
        dsh_ref[0] += dsh
        dg_ref[...] += dg
        pl.when(last)(functools.partial(_xchg_finish, copies))

    row = lambda width: pl.BlockSpec((tm, width), lambda i: (i, 0))
    bat = pl.BlockSpec((1, 1, d), lambda i: (i // per_seq, 0, 0))
    whole = lambda arr: pl.BlockSpec(arr.shape, lambda i: (0, 0))
    n_b = n_rows // seq
    res = pl.pallas_call(
        body, name="in_proj_bwd", grid=grid,
        in_specs=[row(dz.shape[1]) for dz, _ in parts] + [whole(w) for _, w in parts]
        + [row(d), row(d), bat, whole(g)] + [HBM_SPEC] * n_buf,
        out_specs=[row(d), bat, bat, whole(g)] + [HBM_SPEC] * n_buf,
        out_shape=[jax.ShapeDtypeStruct((n_rows, d), F32), jax.ShapeDtypeStruct((n_b, 1, d), F32),
                   jax.ShapeDtypeStruct((n_b, 1, d), F32), jax.ShapeDtypeStruct(g.shape, F32)] + _xchg_out_shapes(bufs),
        scratch_shapes=_xchg_scratch(n_buf),
        compiler_params=_params(("arbitrary",)),
    )(*[dz for dz, _ in parts], *[w for _, w in parts], x2, dx1, scale, g, *[a for a, _ in bufs])
    return res[0], res[1], res[2], res[3], res[4:]


def _ln_silu(u1, g, b):
    mu = jnp.mean(u1, axis=-1, keepdims=True)
    uc = u1 - mu
    r = lax.rsqrt(jnp.mean(uc * uc, axis=-1, keepdims=True) + EPS)
    y = uc * r * g + b
    return y * _sigmoid(y)


def _conv_fill_glu(z_ref, u0_ref, seq, tile):
    u0_ref[0:CONV_HALO, :] = jnp.zeros((CONV_HALO, CONV_CH), F32)
    u0_ref[CONV_HALO + seq:CONV_HALO + seq + CONV_TAIL, :] = jnp.zeros((CONV_TAIL, CONV_CH), F32)
    for t in range(seq // tile):
        zt = z_ref[0, t * tile:(t + 1) * tile, :].astype(F32)
        u0_ref[CONV_HALO + t * tile:CONV_HALO + (t + 1) * tile, :] = zt[:, :CONV_CH] * _sigmoid(zt[:, CONV_CH:])


def _conv_windows(ref, views_ref, t, tile):
    for b in range(8):
        views_ref[b] = ref[t * tile + b:t * tile + b + tile + CONV_HALO, :]


def _conv_tap(views_ref, offset, tile):
    return views_ref[offset % 8, 8 * (offset // 8):8 * (offset // 8) + tile, :]


def _conv_tile(u0_ref, views_ref, w_ref, b_ref, t, tile):
    _conv_windows(u0_ref, views_ref, t, tile)
    acc = jnp.broadcast_to(b_ref[...], (tile, CONV_CH))
    for kk in range(CONV_WIDTH):
        acc = acc + w_ref[kk:kk + 1, :] * _conv_tap(views_ref, kk + CONV_HALO - (CONV_WIDTH - 1), tile)
    return acc


def _conv_fwd(zglu, conv_w, conv_b, ln_g, ln_b, n_b, seq):
    tile = min(256, seq)

    def body(z_ref, w_ref, b_ref, g_ref, bb_ref, o_ref, u1_ref, u0_ref, views_ref):
        _conv_fill_glu(z_ref, u0_ref, seq, tile)
        for t in range(seq // tile):
            u1 = _conv_tile(u0_ref, views_ref, w_ref, b_ref, t, tile)
            u1_ref[0, t * tile:(t + 1) * tile, :] = u1
            o_ref[0, t * tile:(t + 1) * tile, :] = _ln_silu(u1, g_ref[...], bb_ref[...]).astype(BF16)

    whole2 = lambda arr: pl.BlockSpec(arr.shape, lambda b: (0, 0))
    seq_spec = pl.BlockSpec((1, seq, CONV_CH), lambda b: (b, 0, 0))
    return pl.pallas_call(
        body, name="conv_fwd", grid=(n_b,),
        in_specs=[pl.BlockSpec((1, seq, 2 * CONV_CH), lambda b: (b, 0, 0)), whole2(conv_w), whole2(conv_b),
                  whole2(ln_g), whole2(ln_b)],
        out_specs=[seq_spec, seq_spec],
        out_shape=[jax.ShapeDtypeStruct((n_b, seq, CONV_CH), BF16), jax.ShapeDtypeStruct((n_b, seq, CONV_CH), F32)],
        scratch_shapes=[pltpu.VMEM((seq + CONV_HALO + CONV_TAIL, CONV_CH), F32),
                        pltpu.VMEM((8, tile + CONV_HALO, CONV_CH), F32)],
        compiler_params=_params(("parallel",)),
    )(zglu, conv_w, conv_b, ln_g, ln_b)


def _conv_bwd(zglu, u1_saved, du3, conv_w, ln_g, ln_b, n_b, seq):
    tile = min(256, seq)
    n_t = seq // tile

    def body(z_ref, u1_ref, du3_ref, w_ref, g_ref, bb_ref, dz_ref, dw_ref, db_ref, dg_ref, dbb_ref, u0_ref, du1_ref,
             u0_views, du1_views):
        @pl.when(pl.program_id(0) == 0)
        def _():
            for r in (dw_ref, db_ref, dg_ref, dbb_ref):
                r[...] = jnp.zeros_like(r)

        _conv_fill_glu(z_ref, u0_ref, seq, tile)
        du1_ref[seq:seq + CONV_HALO + CONV_TAIL, :] = jnp.zeros((CONV_HALO + CONV_TAIL, CONV_CH), F32)
        g = g_ref[...]
        for t in range(n_t):
            u1 = u1_ref[0, t * tile:(t + 1) * tile, :]
            mu = jnp.mean(u1, axis=-1, keepdims=True)
            uc = u1 - mu
            r = lax.rsqrt(jnp.mean(uc * uc, axis=-1, keepdims=True) + EPS)
            xh = uc * r
            y = xh * g + bb_ref[...]
            sg = _sigmoid(y)
            dy = du3_ref[0, t * tile:(t + 1) * tile, :].astype(F32) * (sg * (1.0 + y * (1.0 - sg)))
            dg_ref[...] += jnp.sum(dy * xh, axis=0, keepdims=True)
            dbb_ref[...] += jnp.sum(dy, axis=0, keepdims=True)
            dxh = dy * g
            du1 = r * (dxh - jnp.mean(dxh, axis=-1, keepdims=True) - xh * jnp.mean(dxh * xh, axis=-1, keepdims=True))
            db_ref[...] += jnp.sum(du1, axis=0, keepdims=True)
            du1_ref[t * tile:(t + 1) * tile, :] = du1
        for t in range(n_t):
            du1 = du1_ref[t * tile:(t + 1) * tile, :]
            du0 = jnp.zeros((tile, CONV_CH), F32)
            _conv_windows(u0_ref, u0_views, t, tile)
            _conv_windows(du1_ref, du1_views, t, tile)
            for kk in range(CONV_WIDTH):
                du0 = du0 + w_ref[kk:kk + 1, :] * _conv_tap(du1_views, CONV_WIDTH - 1 - kk, tile)
                u0_tap = _conv_tap(u0_views, kk + CONV_HALO - (CONV_WIDTH - 1), tile)
                dw_ref[kk:kk + 1, :] += jnp.sum(du1 * u0_tap, axis=0, keepdims=True)
            zt = z_ref[0, t * tile:(t + 1) * tile, :].astype(F32)
            ga, sb = zt[:, :CONV_CH], _sigmoid(zt[:, CONV_CH:])
            dz_ref[0, t * tile:(t + 1) * tile, :CONV_CH] = (du0 * sb).astype(BF16)
            dz_ref[0, t * tile:(t + 1) * tile, CONV_CH:] = (du0 * ga * sb * (1.0 - sb)).astype(BF16)

    whole2 = lambda arr: pl.BlockSpec(arr.shape, lambda b: (0, 0))
    z_spec = pl.BlockSpec((1, seq, 2 * CONV_CH), lambda b: (b, 0, 0))
    seq_spec = pl.BlockSpec((1, seq, CONV_CH), lambda b: (b, 0, 0))
    return pl.pallas_call(
        body, name="conv_bwd", grid=(n_b,),
        in_specs=[z_spec, seq_spec, seq_spec, whole2(conv_w), whole2(ln_g), whole2(ln_b)],
        out_specs=[z_spec, whole2(conv_w), whole2(ln_g), whole2(ln_g), whole2(ln_b)],
        out_shape=[jax.ShapeDtypeStruct((n_b, seq, 2 * CONV_CH), BF16), jax.ShapeDtypeStruct(conv_w.shape, F32),
                   jax.ShapeDtypeStruct(ln_g.shape, F32), jax.ShapeDtypeStruct(ln_g.shape, F32),
                   jax.ShapeDtypeStruct(ln_b.shape, F32)],
        scratch_shapes=[pltpu.VMEM((seq + CONV_HALO + CONV_TAIL, CONV_CH), F32)] * 2
        + [pltpu.VMEM((8, tile + CONV_HALO, CONV_CH), F32)] * 2,
        compiler_params=_params(("arbitrary",)),
    )(zglu, u1_saved, du3, conv_w, ln_g, ln_b)


def _sum_parts(name, parts):
    n_parts = parts.shape[0]

    def body(p_ref, o_ref):
        gg = p_ref[0].astype(F32)
        for j in range(1, n_parts):
            gg = gg + p_ref[j].astype(F32)
        o_ref[...] = gg

    return pl.pallas_call(body, name=name, out_shape=jax.ShapeDtypeStruct(parts.shape[1:], F32),
                          compiler_params=_params(None))(parts)


def _adamw(name, w, parts, m, v, transposed=False):
    n_parts = parts.shape[0]
    rows, cols = w.shape
    tr = ADAM_ROWS if rows % ADAM_ROWS == 0 else rows

    def body(w_ref, p_ref, m_ref, v_ref, g_ref, d_ref, nm_ref, nv_ref):
        gg = p_ref[0].astype(F32)
        for j in range(1, n_parts):
            gg = gg + p_ref[j].astype(F32)
        if transposed:
            gg = gg.T
        nm = ADAM_B1 * m_ref[...] + (1.0 - ADAM_B1) * gg
        nv = ADAM_B2 * v_ref[...] + (1.0 - ADAM_B2) * jnp.square(gg)
        m_hat = nm / (1.0 - ADAM_B1 ** ADAM_STEP)
        v_hat = nv / (1.0 - ADAM_B2 ** ADAM_STEP)
        g_ref[...] = gg
        d_ref[...] = -ADAM_LR * (m_hat / (jnp.sqrt(v_hat) + ADAM_EPS) + ADAM_WD * w_ref[...])
        nm_ref[...] = nm
        nv_ref[...] = nv

    shape = jax.ShapeDtypeStruct(w.shape, F32)
    blk = pl.BlockSpec((tr, cols), lambda i: (i, 0))
    p_spec = (pl.BlockSpec((n_parts, cols, tr), lambda i: (0, 0, i)) if transposed
              else pl.BlockSpec((n_parts, tr, cols), lambda i: (0, i, 0)))
    return pl.pallas_call(body, name=name, grid=(rows // tr,), in_specs=[blk, p_spec, blk, blk], out_specs=[blk] * 4,
                          out_shape=[shape] * 4, compiler_params=_params(("parallel",)))(w, parts, m, v)


def _rope_tables(seq):
    inv_freq = ROPE_THETA ** (-jnp.arange(0, QK_ROPE_DIM, 2, dtype=F32) / QK_ROPE_DIM)
    ang = jnp.arange(seq, dtype=F32)[:, None] * inv_freq[None, :]
    cos, sin = jnp.cos(ang), jnp.sin(ang)
    half = QK_ROPE_DIM // 2
    z = lambda n: jnp.zeros((seq, n), F32)
    tail = HEAD_PAD - QK_HEAD_DIM
    cos_t = jnp.concatenate([jnp.ones((seq, QK_NOPE_DIM), F32), cos, cos, z(tail)], axis=1)
    sin_lo = jnp.concatenate([z(QK_NOPE_DIM), -sin, z(half), z(tail)], axis=1)
    sin_hi = jnp.concatenate([z(QK_NOPE_DIM), z(half), sin, z(tail)], axis=1)
    return cos_t, sin_lo, sin_hi


def _pad_lanes(v, width=HEAD_PAD):
    return jnp.pad(v, [(0, 0)] * (v.ndim - 1) + [(0, width - v.shape[-1])])


def _unstack_cols(s):
    return s.transpose(1, 0, 2).reshape(s.shape[1], N_DEV * s.shape[2])


def _stack_cols(g, dtype):
    rows, cols = g.shape
    return g.reshape(rows, N_DEV, cols // N_DEV).transpose(1, 0, 2).astype(dtype)


def kernel(x, c, w_ada, b_ada, norm1_g, w_in, q_latent_g, w_uq, kv_latent_g, w_ukv, qk_norm_q_g, qk_norm_k_g, w_o_mla, conv_w, conv_b, conv_ln_g, conv_ln_b, w_pw_out, w_out, norm2_g, w_ff1, w_ff2, loss_target, m_w_ada, m_b_ada, m_norm1_g, m_w_in, m_q_latent_g, m_w_uq, m_kv_latent_g, m_w_ukv, m_qk_norm_q_g, m_qk_norm_k_g, m_w_o_mla, m_conv_w, m_conv_b, m_conv_ln_g, m_conv_ln_b, m_w_pw_out, m_w_out, m_norm2_g, m_w_ff1, m_w_ff2, v_w_ada, v_b_ada, v_norm1_g, v_w_in, v_q_latent_g, v_w_uq, v_kv_latent_g, v_w_ukv, v_qk_norm_q_g, v_qk_norm_k_g, v_w_o_mla, v_conv_w, v_conv_b, v_conv_ln_g, v_conv_ln_b, v_w_pw_out, v_w_out, v_norm2_g, v_w_ff1, v_w_ff2):
    given = dict(locals())
    local = {n: given[n][0] for n in WEIGHTS}
    vec = {n: local[n].reshape(1, -1) for n in REPLICATED}
    bf = lambda n: local[n].astype(BF16)
    n_b, seq, d = x.shape
    n_rows = n_b * seq
    x2 = x.reshape(n_rows, d)
    t2 = loss_target.reshape(n_rows, d)
    me = 4 * lax.axis_index("x") + 2 * lax.axis_index("y") + lax.axis_index("c")
    ada_cols = local["w_ada"].shape[1]

    tsh = lambda n: local[n].T.astype(BF16)
    c_all, w_in_s, w_uq_s, w_ukv_s, conv_w_s = _exchange(
        "gather_early", [(c, True), (tsh("w_in"), True), (bf("w_uq"), True), (bf("w_ukv"), True), (local["conv_w"], True)],
        by_chip=True)
    w_in_t = w_in_s.reshape(-1, d)
    zrows = lambda n: jnp.zeros((n, d), BF16)
    w_sm_t = jnp.concatenate([w_in_t[:OFF_KV], zrows(QK_NOPE_DIM), w_in_t[OFF_KV:OFF_KR], zrows(HEAD_PAD - QK_HEAD_DIM)], axis=0)
    w_glu_t = w_in_t[OFF_KR:OFF_GLU]
    w_gate_t = w_in_t[OFF_GLU:]
    wuq = _pad_lanes(_unstack_cols(w_uq_s).reshape(Q_LORA, N_HEADS, QK_HEAD_DIM)).reshape(Q_LORA, N_HEADS * HEAD_PAD)
    wukv_f = _unstack_cols(w_ukv_s).reshape(KV_LORA, N_HEADS, QK_NOPE_DIM + V_HEAD_DIM)
    wv = wukv_f[:, :, QK_NOPE_DIM:]
    odd = (jnp.arange(N_HEADS) % 2 == 1)[None, :, None]
    wuv = jnp.where(odd, jnp.pad(wv, ((0, 0), (0, 0), (V_HEAD_DIM, 0))), jnp.pad(wv, ((0, 0), (0, 0), (0, V_HEAD_DIM))))
    wukv = jnp.concatenate([_pad_lanes(wukv_f[:, :, :QK_NOPE_DIM]), wuv], axis=1).reshape(KV_LORA, 2 * N_HEADS * HEAD_PAD)
    gqn = _pad_lanes(vec["qk_norm_q_g"])
    gkn = _pad_lanes(vec["qk_norm_k_g"])
    conv_w_f = jnp.pad(_unstack_cols(conv_w_s), ((0, 1), (0, 0)))
    rope = _rope_tables(seq)

    all_rows = N_DEV * n_b
    pad_rows = (-all_rows) % ROWS_PAD
    c_rows = jnp.pad(c_all.reshape(all_rows, d), ((0, pad_rows), (0, 0)))
    b_cols = lax.dynamic_slice(local["b_ada"], (me * ada_cols,), (ada_cols,))
    mod_cols = _mm("ada_fwd", c_rows, local["w_ada"], "nn", F32, a_fn=_silu, epi=lambda acc, b: acc + b,
                   epi_in=(jnp.broadcast_to(b_cols, (all_rows + pad_rows, ada_cols)),))
    (mod_s,) = _exchange("scatter_mod", [(mod_cols[:all_rows].reshape(N_DEV, n_b, ada_cols), False)])
    mod = mod_s.transpose(1, 0, 2).reshape(n_b, ADA_CHUNKS, 1, d)
    shift1, scale1, gate1, shift2, scale2, gate2 = [mod[:, i] for i in range(ADA_CHUNKS)]

    h, zgate, zglu, zsm = _in_proj_fwd(x2, scale1, shift1, vec["norm1_g"], [w_gate_t, w_glu_t, w_sm_t],
                                       [BF16, BF16, F32], seq)
    q, k, v, kt = _mla_prep_fwd(zsm, wuq, wukv, vec["q_latent_g"], vec["kv_latent_g"], gqn, gkn, rope, n_b, seq)
    attn, lse, (w_o_s, w_pw_s, w_out_s, w_ff1_s, w_ff2_s) = _attn_fwd(
        q, k, v, [(tsh("w_o_mla"), True), (tsh("w_pw_out"), True), (bf("w_out"), True), (tsh("w_ff1"), True),
                  (bf("w_ff2"), True)], n_b, seq)
    w_o_t = w_o_s.reshape(d, -1)
    w_pw_t = w_pw_s.reshape(d, -1)
    w_out_f = w_out_s.reshape(d, d)
    w_ff1_t = w_ff1_s.reshape(-1, d)
    w_ff2_f = w_ff2_s.reshape(-1, d)
    attn2 = attn.reshape(n_rows, N_HEADS * V_HEAD_DIM)
    u3, u1 = _conv_fwd(zglu.reshape(n_b, seq, 2 * CONV_CH), conv_w_f, vec["conv_b"], vec["conv_ln_g"], vec["conv_ln_b"], n_b, seq)
    u32 = u3.reshape(n_rows, CONV_CH)
    ya = _mm("mla_out", attn2, w_o_t, "nt", BF16)
    yb = _mm("conv_out", u32, w_pw_t, "nt", BF16)
    mmr = functools.partial(_mm_rows, n_rows=n_rows, seq=seq)

    def merge_fn(t):
        return _sigmoid(t[0]) * t[2] + _sigmoid(t[1]) * t[3]

    def mid_fn(acc, r, b, cc):
        x1_ = r[0] + b[0] * acc
        return [acc, x1_, _norm_mod(x1_, cc[0], b[1], b[2])], [], []

    mrg, mixed, x1, h2 = mmr("out_proj", [(zgate, d, 0), (zgate, d, 1), (ya, d, 0), (yb, d, 0)], merge_fn, w_out_f, "nn",
                             mid_fn, rows=[_full(x2)], bats=[gate1, scale2, shift2], consts=[vec["norm2_g"]],
                             outs=[(d, BF16), (d, F32), (d, BF16)], a_out=BF16)

    a = _mm("ff1", h2, w_ff1_t, "nt", BF16)

    def loss_fn(ff, r, b, cc):
        err = r[0] + b[0] * ff - r[1]
        dy_ = err * (1.0 / d)
        sq = jnp.broadcast_to(jnp.sum(err * err, keepdims=True), (1, LANES))
        return [dy_, b[0] * dy_], [jnp.sum(dy_ * ff, axis=0, keepdims=True)], [sq]

    dy, df, dgate2, sq_err = mmr("ff2_loss", [(a, a.shape[1], 0)], lambda t: _relu2(t[0]), w_ff2_f, "nn", loss_fn,
                                 rows=[_full(x1), _full(t2)], bats=[gate2], outs=[(d, F32), (d, BF16)], bat_outs=[d],
                                 tot_outs=[(1, LANES)], tk=a.shape[1])

    da = _mm("ff2_bwd", df, w_ff2_f, "nt", BF16, epi=lambda acc, av: acc * 2.0 * jnp.maximum(av, 0.0), epi_in=(a,))
    g_ff2 = _mm("ff2_dw", a, df, "tn", BF16, a_fn=_relu2)
    g_ff1_t = _mm("ff1_dw", da, h2, "tn", BF16)

    def mid_bwd(dh2_, r, b, cc):
        dx, dsc, dsh, dg = _norm_mod_bwd(r[0], cc[0], b[0], dh2_)
        dx1_ = r[1] + dx
        return [dx1_, b[1] * dx1_], [dsc, dsh, jnp.sum(dx1_ * r[2].astype(F32), axis=0, keepdims=True)], [dg]

    dx1, dmixed, dscale2, dshift2, dgate1, g_norm2 = mmr(
        "ff1_bwd", [(da, da.shape[1], 0)], None, w_ff1_t, "nn", mid_bwd, rows=[_full(x1), _full(dy), _full(mixed)],
        bats=[scale2, gate1], consts=[vec["norm2_g"]], outs=[(d, F32), (d, BF16)], bat_outs=[d, d, d],
        tot_outs=[(1, d)], tk=da.shape[1])

    g_out = _mm("out_proj_dw", mrg, dmixed, "tn", BF16)

    def merge_bwd(dm, r, b, cc):
        ya_, yb_ = r[2].astype(F32), r[3].astype(F32)
        sa, sb = _sigmoid(r[0].astype(F32)), _sigmoid(r[1].astype(F32))
        return [dm * ya_ * sa * (1.0 - sa), dm * yb_ * sb * (1.0 - sb), dm * sa, dm * sb], [], []

    dzga, dzgb, dya, dyb = mmr("out_proj_bwd", [(dmixed, d, 0)], None, w_out_f, "nt", merge_bwd,
                               rows=[(zgate, d, 0), (zgate, d, 1), _full(ya), _full(yb)], outs=[(d, BF16)] * 4)
    dattn = _mm("mla_out_bwd", dya, w_o_t, "nn", BF16)
    g_o_t = _mm("mla_out_dw", dya, attn2, "tn", BF16)
    du3 = _mm("conv_out_bwd", dyb, w_pw_t, "nn", BF16)
    g_pw_t = _mm("conv_out_dw", dyb, u32, "tn", BF16)

    dzglu, g_conv_w, g_conv_b, g_ln_g, g_ln_b = _conv_bwd(
        zglu.reshape(n_b, seq, 2 * CONV_CH), u1, du3.reshape(n_b, seq, CONV_CH), conv_w_f, vec["conv_ln_g"],
        vec["conv_ln_b"], n_b, seq)
    dzglu = dzglu.reshape(n_rows, 2 * CONV_CH)

    dq, dk, dv, (p_ff2, p_ff1, p_out, p_pw, p_o) = _attn_bwd(
        q, k, v, kt, dattn.reshape(n_b, seq, N_HEADS * V_HEAD_DIM), attn, lse,
        [(g_ff2.reshape(N_DEV, -1, d), False), (g_ff1_t.reshape(N_DEV, -1, d), False), (g_out.reshape(N_DEV, -1, d), False),
         (g_pw_t.reshape(N_DEV, -1, CONV_CH), False), (g_o_t.reshape(N_DEV, -1, N_HEADS * V_HEAD_DIM), False)], n_b, seq)
    dzsm, g_wuq, g_wukv, g_gq, g_gkv, g_gqn, g_gkn = _mla_prep_bwd(
        zsm, dq, dk, dv, wuq, wukv, vec["q_latent_g"], vec["kv_latent_g"], gqn, gkn, rope, n_b, seq)

    g_gate_a_t = _mm("in_proj_gate_dw_a", dzga, h, "tn", BF16)
    g_gate_b_t = _mm("in_proj_gate_dw_b", dzgb, h, "tn", BF16)
    g_glu_t = _mm("in_proj_glu_dw", dzglu, h, "tn", BF16)
    g_sm_t = _mm("in_proj_sm_dw", dzsm, h, "tn", BF16)
    g_in_t = jnp.concatenate([g_sm_t[:OFF_KV], g_sm_t[OFF_KV + QK_NOPE_DIM:OFF_KV + QK_NOPE_DIM + QK_ROPE_DIM], g_glu_t,
                              g_gate_a_t, g_gate_b_t], axis=0)
    g_uq = g_wuq.reshape(Q_LORA, N_HEADS, HEAD_PAD)[:, :, :QK_HEAD_DIM].reshape(Q_LORA, N_HEADS * QK_HEAD_DIM)
    g_wukv = g_wukv.reshape(KV_LORA, 2, N_HEADS, HEAD_PAD)
    g_v = jnp.where(odd, g_wukv[:, 1, :, V_HEAD_DIM:], g_wukv[:, 1, :, :V_HEAD_DIM])
    g_ukv = jnp.concatenate([g_wukv[:, 0, :, :QK_NOPE_DIM], g_v], axis=2).reshape(KV_LORA, -1)

    grad_x, dscale1, dshift1, g_norm1, (p_in, p_uq, p_ukv, p_conv_w) = _in_proj_bwd(
        [(dzga, w_gate_t[:d]), (dzgb, w_gate_t[d:]), (dzglu, w_glu_t), (dzsm, w_sm_t)], x2, dx1, scale1, vec["norm1_g"],
        [(g_in_t.reshape(N_DEV, -1, d), False), (_stack_cols(g_uq, BF16), False), (_stack_cols(g_ukv, BF16), False),
         (_stack_cols(g_conv_w[:CONV_WIDTH], F32), False)], seq)

    dmod = jnp.concatenate([dshift1, dscale1, dgate1, dshift2, dscale2, dgate2], axis=1).reshape(n_b, N_DEV, ada_cols)
    (dmod_s,) = _exchange("scatter_dmod", [(dmod.transpose(1, 0, 2), False)])
    dmod_rows = jnp.pad(dmod_s.reshape(all_rows, ada_cols), ((0, pad_rows), (0, 0)))
    g_ada = _mm("ada_dw", c_rows, dmod_rows, "tn", F32, a_fn=_silu)
    (g_b_cols,) = _rowwise("ada_db", lambda r, b, cc: ([], [], [jnp.sum(r[0], axis=0, keepdims=True)]),
                           all_rows + pad_rows, all_rows + pad_rows, rows=[_full(dmod_rows)], tot_outs=[(1, ada_cols)])

    partial_of = {"norm1_g": g_norm1, "q_latent_g": g_gq, "kv_latent_g": g_gkv, "qk_norm_q_g": g_gqn,
                  "qk_norm_k_g": g_gkn, "conv_b": g_conv_b, "conv_ln_g": g_ln_g, "conv_ln_b": g_ln_b, "norm2_g": g_norm2}
    names = [n for n in REPLICATED if n != "b_ada"]
    pieces = [_pad_lanes(partial_of[n], -(-partial_of[n].shape[1] // LANES) * LANES) for n in names] + [g_b_cols, sq_err]
    widths = [p.shape[1] for p in pieces]
    small = jnp.concatenate(pieces, axis=1)
    small = _pad_lanes(small, -(-small.shape[1] // (8 * LANES)) * 8 * LANES).reshape(-1, LANES)
    (small_s,) = _exchange("gather_small_grads", [(small, True)])
    small_s = small_s.reshape(N_DEV, 1, -1)
    parts = {}
    off = 0
    for n, wd in zip(names, widths):
        parts[n] = small_s[:, :, off:off + vec[n].shape[1]]
        off += wd
    parts["b_ada"] = small_s[:, 0, off:off + ada_cols].reshape(1, 1, N_DEV * ada_cols)
    loss = jnp.sum(small_s[:, 0, off + ada_cols]) * (0.5 / d)
    g_in_mine = _sum_parts("sum_w_in", p_in).T
    parts.update({"w_ada": g_ada[None], "w_in": g_in_mine[None], "w_uq": p_uq, "w_ukv": p_ukv, "w_o_mla": p_o,
                  "conv_w": p_conv_w, "w_pw_out": p_pw, "w_out": p_out, "w_ff1": p_ff1, "w_ff2": p_ff2})
    transposed = ("w_o_mla", "w_pw_out", "w_ff1")

    grad_out, delta_out, m_out, v_out = [], [], [], []
    for n in WEIGHTS:
        shape2 = local[n].shape if local[n].ndim == 2 else (1, local[n].shape[0])
        g_w, d_w, n_m, n_v = _adamw("adamw_" + n, local[n].reshape(shape2), parts[n], given["m_" + n].reshape(shape2),
                                    given["v_" + n].reshape(shape2), transposed=n in transposed)
        full_shape = given[n].shape
        grad_out.append(g_w.reshape(full_shape))
        delta_out.append(d_w.reshape(full_shape))
        m_out.append(n_m.reshape(full_shape))
        v_out.append(n_v.reshape(full_shape))
    return (loss, grad_x.reshape(n_b, seq, d), *grad_out, *delta_out, *m_out, *v_out)
```

```python
import functools

import jax
import jax.numpy as jnp
from jax import lax
from jax.experimental import pallas as pl
from jax.experimental.pallas import tpu as pltpu

F32 = jnp.float32
BF16 = jnp.bfloat16

N_DEV = 8
EPS = 1e-6
N_HEADS = 8
QK_HEAD_DIM = 96
QK_NOPE_DIM = 64
QK_ROPE_DIM = 32
V_HEAD_DIM = 64
HEAD_PAD = 128
Q_LORA = 256
KV_LORA = 128
CONV_CH = 512
CONV_WIDTH = 31
CONV_HALO = 32
CONV_TAIL = 8
CHUNK = 64
ROPE_THETA = 10000.0
OFF_Q = Q_LORA
OFF_KV = OFF_Q + KV_LORA
OFF_KR = OFF_KV + QK_ROPE_DIM
OFF_GLU = OFF_KR + 2 * CONV_CH
ADA_CHUNKS = 6
ADAM_LR = 0.001
ADAM_B1 = 0.9
ADAM_B2 = 0.999
ADAM_EPS = 1e-08
ADAM_WD = 0.01
ADAM_STEP = 10
LANES = 128
VMEM_LIMIT = 56 * 1024 * 1024
NEG_BIG = -1e30
ATT_HEADS = 4
ATT_TILE = 512
PREP_TILE = 1024
ATT_SCALE = QK_HEAD_DIM ** -0.5
LOG2E = 1.4426950408889634
LN2 = 0.6931471805599453
QK_SCALE = ATT_SCALE * LOG2E
ADAM_ROWS = 256
ROWS_PAD = 16

REPLICATED = ("b_ada", "norm1_g", "q_latent_g", "kv_latent_g", "qk_norm_q_g", "qk_norm_k_g", "conv_b", "conv_ln_g",
              "conv_ln_b", "norm2_g")
WEIGHTS = ("w_ada", "b_ada", "norm1_g", "w_in", "q_latent_g", "w_uq", "kv_latent_g", "w_ukv", "qk_norm_q_g",
           "qk_norm_k_g", "w_o_mla", "conv_w", "conv_b", "conv_ln_g", "conv_ln_b", "w_pw_out", "w_out", "norm2_g",
           "w_ff1", "w_ff2")


def _tile(dim, pref):
    if dim <= pref:
        return dim
    t = (pref // LANES) * LANES
    while dim % t:
        t -= LANES
    return t


def _params(semantics):
    return pltpu.CompilerParams(dimension_semantics=semantics, vmem_limit_bytes=VMEM_LIMIT)


def _sigmoid(v):
    return 1.0 / (1.0 + jnp.exp(-v))


def _silu(v):
    return v * _sigmoid(v)


def _relu2(v):
    return jnp.square(jnp.maximum(v, 0.0))


_DIMS = {"nn": (((1,), (0,)), ((), ())), "nt": (((1,), (1,)), ((), ())), "tn": (((0,), (0,)), ((), ()))}


def _mm(name, a, b, mode, out_dtype, *, a_fn=None, epi=None, epi_in=(), tm=1024, tn=1024, tk=1024):
    if mode == "nn":
        (m, k), n = a.shape, b.shape[1]
    elif mode == "nt":
        (m, k), n = a.shape, b.shape[0]
    else:
        (k, m), n = a.shape, b.shape[1]
    tm, tn, tk = _tile(m, tm), _tile(n, tn), _tile(k, tk)
    nk = k // tk
    a_spec = (pl.BlockSpec((tk, tm), lambda i, j, kk: (kk, i)) if mode == "tn"
              else pl.BlockSpec((tm, tk), lambda i, j, kk: (i, kk)))
    b_spec = (pl.BlockSpec((tn, tk), lambda i, j, kk: (j, kk)) if mode == "nt"
              else pl.BlockSpec((tk, tn), lambda i, j, kk: (kk, j)))
    o_spec = e_spec = pl.BlockSpec((tm, tn), lambda i, j, kk: (i, j))
    out_shape = jax.ShapeDtypeStruct((m, n), out_dtype)
    n_epi = len(epi_in)

    def body(a_ref, b_ref, *rest):
        epi_refs, o_ref, acc_ref = rest[:n_epi], rest[n_epi], rest[n_epi + 1]
        kk = pl.program_id(2)

        @pl.when(kk == 0)
        def _():
            acc_ref[...] = jnp.zeros_like(acc_ref)

        av = a_ref[...]
        if a_fn is not None:
            av = a_fn(av.astype(F32))
        acc_ref[...] += lax.dot_general(av.astype(BF16), b_ref[...].astype(BF16), _DIMS[mode],
                                        preferred_element_type=F32)

        @pl.when(kk == nk - 1)
        def _():
            acc = acc_ref[...]
            if epi is not None:
                acc = epi(acc, *[r[...].astype(F32) for r in epi_refs])
            o_ref[...] = acc.astype(out_dtype)

    return pl.pallas_call(
        body, name=name, grid=(m // tm, n // tn, nk),
        in_specs=[a_spec, b_spec] + [e_spec] * n_epi, out_specs=o_spec, out_shape=out_shape,
        scratch_shapes=[pltpu.VMEM((tm, tn), F32)],
        compiler_params=_params(("parallel", "parallel", "arbitrary")),
    )(a, b, *epi_in)


def _rowwise(name, fn, n_rows, seq, rows, bats=(), consts=(), outs=(), bat_outs=(), tot_outs=(), tm=256):
    tm = min(tm, seq)
    per_seq = seq // tm
    n_b = n_rows // seq
    nr, nb, nc, no, nbo, nto = len(rows), len(bats), len(consts), len(outs), len(bat_outs), len(tot_outs)

    def body(*refs):
        i = pl.program_id(0)
        r_in = [r[...] for r in refs[:nr]]
        b_in = [r[0] for r in refs[nr:nr + nb]]
        c_in = [r[...] for r in refs[nr + nb:nr + nb + nc]]
        o_refs = refs[nr + nb + nc:nr + nb + nc + no]
        bo_refs = refs[nr + nb + nc + no:nr + nb + nc + no + nbo]
        to_refs = refs[nr + nb + nc + no + nbo:]
        o_val, bo_val, to_val = fn(r_in, b_in, c_in)
        for r, v in zip(o_refs, o_val):
            r[...] = v.astype(r.dtype)
        if nbo:
            @pl.when(i % per_seq == 0)
            def _():
                for r in bo_refs:
                    r[...] = jnp.zeros_like(r)

            for r, v in zip(bo_refs, bo_val):
                r[0] += v
        if nto:
            @pl.when(i == 0)
            def _():
                for r in to_refs:
                    r[...] = jnp.zeros_like(r)

            for r, v in zip(to_refs, to_val):
                r[...] += v

    in_specs = [pl.BlockSpec((tm, w), functools.partial(lambda cb, i: (i, cb), cb)) for (_, w, cb) in rows]
    in_specs += [pl.BlockSpec((1, 1, bt.shape[2]), lambda i: (i // per_seq, 0, 0)) for bt in bats]
    in_specs += [pl.BlockSpec(ct.shape, lambda i: (0, 0)) for ct in consts]
    out_specs = [pl.BlockSpec((tm, w), lambda i: (i, 0)) for (w, _) in outs]
    out_specs += [pl.BlockSpec((1, 1, w), lambda i: (i // per_seq, 0, 0)) for w in bat_outs]
    out_specs += [pl.BlockSpec(shp, lambda i: (0, 0)) for shp in tot_outs]
    out_shape = [jax.ShapeDtypeStruct((n_rows, w), dt) for (w, dt) in outs]
    out_shape += [jax.ShapeDtypeStruct((n_b, 1, w), F32) for w in bat_outs]
    out_shape += [jax.ShapeDtypeStruct(shp, F32) for shp in tot_outs]
    res = pl.pallas_call(
        body, name=name, grid=(n_rows // tm,), in_specs=in_specs, out_specs=out_specs, out_shape=out_shape,
        compiler_params=_params(("arbitrary",)),
    )(*[r[0] for r in rows], *bats, *consts)
    return res


def _full(arr):
    return (arr, arr.shape[1], 0)


def _mm_rows(name, a_rows, a_fn, w, mode, fn, n_rows, seq, rows=(), bats=(), consts=(), outs=(), bat_outs=(),
             tot_outs=(), a_out=None, tm=512, tk=1024):
    tm = min(tm, seq)
    per_seq = seq // tm
    n_b = n_rows // seq
    k = a_rows[0][1]
    if mode == "nt":
        n_out, tk = w.shape[0], _tile(k, tk)
        w_spec = pl.BlockSpec((n_out, tk), lambda i, kk: (0, kk))
    else:
        n_out, tk = w.shape[1], _tile(k, tk)
        w_spec = pl.BlockSpec((tk, n_out), lambda i, kk: (kk, 0))
    nk = k // tk
    na, nr, nb, nc = len(a_rows), len(rows), len(bats), len(consts)
    n_extra = 0 if a_out is None else 1
    no, nbo, nto = len(outs), len(bat_outs), len(tot_outs)

    def body(*refs):
        i, kk = pl.program_id(0), pl.program_id(1)
        a_refs, w_ref = refs[:na], refs[na]
        pos = na + 1
        r_refs, b_refs, c_refs = refs[pos:pos + nr], refs[pos + nr:pos + nr + nb], refs[pos + nr + nb:pos + nr + nb + nc]
        pos += nr + nb + nc
        ao_refs = refs[pos:pos + n_extra]
        pos += n_extra
        o_refs, bo_refs, to_refs = refs[pos:pos + no], refs[pos + no:pos + no + nbo], refs[pos + no + nbo:pos + no + nbo + nto]
        acc_ref = refs[pos + no + nbo + nto]

        @pl.when(kk == 0)
        def _():
            acc_ref[...] = jnp.zeros_like(acc_ref)

        tiles = [r[...] for r in a_refs]
        av = a_fn([t.astype(F32) for t in tiles]) if a_fn is not None else tiles[0]
        av = av.astype(BF16)
        if n_extra:
            ao_refs[0][...] = av.astype(ao_refs[0].dtype)
        acc_ref[...] += lax.dot_general(av, w_ref[...].astype(BF16), _DIMS[mode], preferred_element_type=F32)

        @pl.when(kk == nk - 1)
        def _():
            o_val, bo_val, to_val = fn(acc_ref[...], [r[...] for r in r_refs], [r[0] for r in b_refs],
                                       [r[...] for r in c_refs])
            for r, v in zip(o_refs, o_val):
                r[...] = v.astype(r.dtype)
            if nbo:
                @pl.when(i % per_seq == 0)
                def _():
                    for r in bo_refs:
                        r[...] = jnp.zeros_like(r)

                for r, v in zip(bo_refs, bo_val):
                    r[0] += v
            if nto:
                @pl.when(i == 0)
                def _():
                    for r in to_refs:
                        r[...] = jnp.zeros_like(r)

                for r, v in zip(to_refs, to_val):
                    r[...] += v

    in_specs = [pl.BlockSpec((tm, tk), functools.partial(lambda cb, i, kk: (i, kk + cb), cb)) for (_, _, cb) in a_rows]
    in_specs += [w_spec]
    in_specs += [pl.BlockSpec((tm, wd), functools.partial(lambda cb, i, kk: (i, cb), cb)) for (_, wd, cb) in rows]
    in_specs += [pl.BlockSpec((1, 1, bt.shape[2]), lambda i, kk: (i // per_seq, 0, 0)) for bt in bats]
    in_specs += [pl.BlockSpec(ct.shape, lambda i, kk: (0, 0)) for ct in consts]
    out_specs = [pl.BlockSpec((tm, tk), lambda i, kk: (i, kk))] * n_extra
    out_specs += [pl.BlockSpec((tm, wd), lambda i, kk: (i, 0)) for (wd, _) in outs]
    out_specs += [pl.BlockSpec((1, 1, wd), lambda i, kk: (i // per_seq, 0, 0)) for wd in bat_outs]
    out_specs += [pl.BlockSpec(shp, lambda i, kk: (0, 0)) for shp in tot_outs]
    out_shape = [jax.ShapeDtypeStruct((n_rows, k), a_out)] if n_extra else []
    out_shape += [jax.ShapeDtypeStruct((n_rows, wd), dt) for (wd, dt) in outs]
    out_shape += [jax.ShapeDtypeStruct((n_b, 1, wd), F32) for wd in bat_outs]
    out_shape += [jax.ShapeDtypeStruct(shp, F32) for shp in tot_outs]
    return pl.pallas_call(
        body, name=name, grid=(n_rows // tm, nk), in_specs=in_specs, out_specs=out_specs, out_shape=out_shape,
        scratch_shapes=[pltpu.VMEM((tm, n_out), F32)],
        compiler_params=_params(("arbitrary", "arbitrary")),
    )(*[a for a, _, _ in a_rows], w, *[r[0] for r in rows], *bats, *consts)


def _norm_mod(x, g, scale, shift):
    r = lax.rsqrt(jnp.mean(x * x, axis=-1, keepdims=True) + EPS)
    xh = x * r
    return xh * g * (1.0 + scale) + shift


def _norm_mod_bwd(x, g, scale, dh):
    r = lax.rsqrt(jnp.mean(x * x, axis=-1, keepdims=True) + EPS)
    xh = x * r
    dn = dh * (1.0 + scale)
    dxh = dn * g
    dx = r * (dxh - xh * jnp.mean(dxh * xh, axis=-1, keepdims=True))
    dscale = jnp.sum(dh * xh * g, axis=0, keepdims=True)
    dshift = jnp.sum(dh, axis=0, keepdims=True)
    dg = jnp.sum(dn * xh, axis=0, keepdims=True)
    return dx, dscale, dshift, dg


def _rms(v, g):
    r = lax.rsqrt(jnp.mean(v * v, axis=-1, keepdims=True) + EPS)
    return v * r * g


def _rms_bwd(v, g, dy):
    r = lax.rsqrt(jnp.mean(v * v, axis=-1, keepdims=True) + EPS)
    vh = v * r
    dvh = dy * g
    dv = r * (dvh - vh * jnp.mean(dvh * vh, axis=-1, keepdims=True))
    return dv, jnp.sum(dy * vh, axis=0, keepdims=True)


def _head_norm(v, g):
    r = lax.rsqrt(jnp.sum(v * v, axis=-1, keepdims=True) * (1.0 / QK_HEAD_DIM) + EPS)
    return v * r * g


def _head_norm_bwd(v, g, dy):
    r = lax.rsqrt(jnp.sum(v * v, axis=-1, keepdims=True) * (1.0 / QK_HEAD_DIM) + EPS)
    vh = v * r
    dvh = dy * g
    dv = r * (dvh - vh * (jnp.sum(dvh * vh, axis=-1, keepdims=True) * (1.0 / QK_HEAD_DIM)))
    return dv, jnp.sum(dy * vh, axis=0, keepdims=True)


def _rope(v, cos, sin_lo, sin_hi):
    return v * cos + pltpu.roll(v, HEAD_PAD - 16, 1) * sin_lo + pltpu.roll(v, 16, 1) * sin_hi


def _rope_bwd(g, cos, sin_lo, sin_hi):
    return g * cos + pltpu.roll(g * sin_lo, 16, 1) + pltpu.roll(g * sin_hi, HEAD_PAD - 16, 1)


def _mla_prep_fwd(zsm, wuq, wukv, gq, gkv, gqn, gkn, rope, n_b, seq):
    n_rows = n_b * seq
    tm = min(PREP_TILE, seq)
    per_seq = seq // tm
    att_tile = min(ATT_TILE, seq)
    k_cols = N_HEADS * HEAD_PAD

    def body(z_ref, wuq_ref, wukv_ref, gq_ref, gkv_ref, gqn_ref, gkn_ref, c_ref, s1_ref, s2_ref,
             q_ref, k_ref, v_ref, kt_ref):
        z = z_ref[...]
        qn = _rms(z[:, :Q_LORA], gq_ref[...]).astype(BF16)
        kvn = _rms(z[:, Q_LORA:Q_LORA + KV_LORA], gkv_ref[...]).astype(BF16)
        krp = z[:, Q_LORA + KV_LORA:]
        cos, s1, s2 = c_ref[...], s1_ref[...], s2_ref[...]
        q_all = jnp.dot(qn, wuq_ref[...], preferred_element_type=F32)
        kv_all = jnp.dot(kvn, wukv_ref[...], preferred_element_type=F32)
        for h in range(N_HEADS):
            cols = slice(h * HEAD_PAD, (h + 1) * HEAD_PAD)
            q_ref[0, h] = (_rope(_head_norm(q_all[:, cols], gqn_ref[...]), cos, s1, s2) * QK_SCALE).astype(BF16)
            kh = _rope(_head_norm(kv_all[:, cols] + krp, gkn_ref[...]), cos, s1, s2)
            k_ref[0, h] = kh.astype(BF16)
            for part in range(tm // att_tile):
                kt_ref[0, h, part] = kh[part * att_tile:(part + 1) * att_tile].T.astype(BF16)
            v_ref[0, h] = kv_all[:, k_cols + h * HEAD_PAD:k_cols + (h + 1) * HEAD_PAD].astype(BF16)

    whole2 = lambda arr: pl.BlockSpec(arr.shape, lambda i: (0, 0))
    rope_spec = pl.BlockSpec((tm, HEAD_PAD), lambda i: (i % per_seq, 0))
    head_spec = pl.BlockSpec((1, N_HEADS, tm, HEAD_PAD), lambda i: (i // per_seq, 0, i % per_seq, 0))
    head_shape = jax.ShapeDtypeStruct((n_b, N_HEADS, seq, HEAD_PAD), BF16)
    t_spec = pl.BlockSpec((1, N_HEADS, tm // att_tile, HEAD_PAD, att_tile), lambda i: (i // per_seq, 0, i % per_seq, 0, 0))
    t_shape = jax.ShapeDtypeStruct((n_b, N_HEADS, seq // att_tile, HEAD_PAD, att_tile), BF16)
    return pl.pallas_call(
        body, name="mla_prep_fwd", grid=(n_rows // tm,),
        in_specs=[pl.BlockSpec((tm, 512), lambda i: (i, 0)), whole2(wuq), whole2(wukv),
                  whole2(gq), whole2(gkv), whole2(gqn), whole2(gkn), rope_spec, rope_spec, rope_spec],
        out_specs=[head_spec] * 3 + [t_spec], out_shape=[head_shape] * 3 + [t_shape],
        compiler_params=_params(("parallel",)),
    )(zsm, wuq, wukv, gq, gkv, gqn, gkn, *rope)


def _mla_prep_bwd(zsm, dq, dk, dv, wuq, wukv, gq, gkv, gqn, gkn, rope, n_b, seq):
    n_rows = n_b * seq
    tm = min(PREP_TILE, seq)
    per_seq = seq // tm
    tn_dims = _DIMS["tn"]
    nt_dims = _DIMS["nt"]
    k_cols = N_HEADS * HEAD_PAD

    def body(z_ref, dq_ref, dk_ref, dv_ref, wuq_ref, wukv_ref, gq_ref, gkv_ref, gqn_ref, gkn_ref,
             c_ref, s1_ref, s2_ref, dz_ref, dwuq_ref, dwukv_ref, dgq_ref, dgkv_ref, dgqn_ref, dgkn_ref):
        @pl.when(pl.program_id(0) == 0)
        def _():
            for r in (dwuq_ref, dwukv_ref, dgq_ref, dgkv_ref, dgqn_ref, dgkn_ref):
                r[...] = jnp.zeros_like(r)

        z = z_ref[...]
        zq, zkv, krp = z[:, :Q_LORA], z[:, Q_LORA:Q_LORA + KV_LORA], z[:, Q_LORA + KV_LORA:]
        qn = _rms(zq, gq_ref[...]).astype(BF16)
        kvn = _rms(zkv, gkv_ref[...]).astype(BF16)
        cos, s1, s2 = c_ref[...], s1_ref[...], s2_ref[...]
        lane = lax.broadcasted_iota(jnp.int32, (tm, HEAD_PAD), 1)
        rope_lanes = (lane >= QK_NOPE_DIM) & (lane < QK_HEAD_DIM)
        q_all = jnp.dot(qn, wuq_ref[...], preferred_element_type=F32)
        k_all = jnp.dot(kvn, wukv_ref[:, :k_cols], preferred_element_type=F32)
        dkrp = jnp.zeros((tm, HEAD_PAD), F32)
        dgqn = jnp.zeros((1, HEAD_PAD), F32)
        dgkn = jnp.zeros((1, HEAD_PAD), F32)
        dq_heads, dk_heads = [], []
        for h in range(N_HEADS):
            cols = slice(h * HEAD_PAD, (h + 1) * HEAD_PAD)
            dqh, dg = _head_norm_bwd(q_all[:, cols], gqn_ref[...],
                                     _rope_bwd(dq_ref[0, h].astype(F32) * ATT_SCALE, cos, s1, s2))
            dgqn += dg
            dq_heads.append(dqh.astype(BF16))
            dkh, dg = _head_norm_bwd(k_all[:, cols] + krp, gkn_ref[...], _rope_bwd(dk_ref[0, h].astype(F32), cos, s1, s2))
            dgkn += dg
            dkrp += jnp.where(rope_lanes, dkh, 0.0)
            dk_heads.append(dkh.astype(BF16))
        dq_all = jnp.concatenate(dq_heads, axis=1)
        dkv_all = jnp.concatenate(dk_heads + [dv_ref[0, h] for h in range(N_HEADS)], axis=1)
        dwuq_ref[...] += lax.dot_general(qn, dq_all, tn_dims, preferred_element_type=F32)
        dqn = lax.dot_general(dq_all, wuq_ref[...], nt_dims, preferred_element_type=F32)
        dwukv_ref[...] += lax.dot_general(kvn, dkv_all, tn_dims, preferred_element_type=F32)
        dkvn = lax.dot_general(dkv_all, wukv_ref[...], nt_dims, preferred_element_type=F32)
        dzq, dg = _rms_bwd(zq, gq_ref[...], dqn)
        dgq_ref[...] += dg
        dzkv, dg = _rms_bwd(zkv, gkv_ref[...], dkvn)
        dgkv_ref[...] += dg
        dgqn_ref[...] += dgqn
        dgkn_ref[...] += dgkn
        dz_ref[:, :Q_LORA] = dzq.astype(dz_ref.dtype)
        dz_ref[:, Q_LORA:Q_LORA + KV_LORA] = dzkv.astype(dz_ref.dtype)
        dz_ref[:, Q_LORA + KV_LORA:] = dkrp.astype(dz_ref.dtype)

    whole2 = lambda arr: pl.BlockSpec(arr.shape, lambda i: (0, 0))
    rope_spec = pl.BlockSpec((tm, HEAD_PAD), lambda i: (i % per_seq, 0))
    head_spec = pl.BlockSpec((1, N_HEADS, tm, HEAD_PAD), lambda i: (i // per_seq, 0, i % per_seq, 0))
    row_spec = pl.BlockSpec((tm, 512), lambda i: (i, 0))
    return pl.pallas_call(
        body, name="mla_prep_bwd", grid=(n_rows // tm,),
        in_specs=[row_spec, head_spec, head_spec, head_spec, whole2(wuq), whole2(wukv),
                  whole2(gq), whole2(gkv), whole2(gqn), whole2(gkn), rope_spec, rope_spec, rope_spec],
        out_specs=[row_spec, whole2(wuq), whole2(wukv), whole2(gq), whole2(gkv), whole2(gqn), whole2(gkn)],
        out_shape=[jax.ShapeDtypeStruct((n_rows, 512), BF16),
                   jax.ShapeDtypeStruct(wuq.shape, F32), jax.ShapeDtypeStruct(wukv.shape, F32),
                   jax.ShapeDtypeStruct(gq.shape, F32), jax.ShapeDtypeStruct(gkv.shape, F32),
                   jax.ShapeDtypeStruct(gqn.shape, F32), jax.ShapeDtypeStruct(gkn.shape, F32)],
        compiler_params=_params(("arbitrary",)),
    )(zsm, dq, dk, dv, wuq, wukv, gq, gkv, gqn, gkn, *rope)


HBM_SPEC = pl.BlockSpec(memory_space=pltpu.HBM)


def _xchg_out_shapes(bufs):
    return [jax.ShapeDtypeStruct((N_DEV,) + (a.shape if gather else a.shape[1:]), a.dtype) for a, gather in bufs]


def _xchg_scratch(n_buf):
    return [pltpu.SemaphoreType.DMA((n_buf * (N_DEV - 1),)), pltpu.SemaphoreType.DMA((n_buf * (N_DEV - 1),)),
            pltpu.SemaphoreType.DMA((n_buf,))]


def _xchg_copies(src_refs, dst_refs, gathers, send_sems, recv_sems, local_sems):
    x, y, c = lax.axis_index("x"), lax.axis_index("y"), lax.axis_index("c")
    me = 4 * x + 2 * y + c
    local, starts, arrivals = [], [], []
    for bi, (src, dst, gather) in enumerate(zip(src_refs, dst_refs, gathers)):
        local.append(pltpu.make_async_copy(src if gather else src.at[me], dst.at[me], local_sems.at[bi]))
        for kk in range(1, N_DEV):
            px = 1 - x if kk & 4 else x
            py = 1 - y if kk & 2 else y
            pc = 1 - c if kk & 1 else c
            pid = 4 * px + 2 * py + pc
            sem = bi * (N_DEV - 1) + kk - 1
            starts.append(pltpu.make_async_remote_copy(
                src_ref=src if gather else src.at[pid], dst_ref=dst.at[me],
                send_sem=send_sems.at[sem], recv_sem=recv_sems.at[sem],
                device_id=(px, py, pc), device_id_type=pl.DeviceIdType.MESH))
            arrivals.append(pltpu.make_async_remote_copy(
                src_ref=src if gather else src.at[me], dst_ref=dst.at[pid],
                send_sem=send_sems.at[sem], recv_sem=recv_sems.at[sem],
                device_id=(px, py, pc), device_id_type=pl.DeviceIdType.MESH))
    return local, starts, arrivals


def _xchg_start(copies):
    local, sends, _ = copies
    for cp in local + sends:
        cp.start()


def _xchg_finish(copies):
    local, sends, arrivals = copies
    for cp in arrivals:
        cp.wait_recv()
    for cp in sends:
        cp.wait_send()
    for cp in local:
        cp.wait()


def _gather_by_chip(src_refs, dst_refs, send_sems, recv_sems, local_sems, start=True, finish=True):
    x, y, c = lax.axis_index("x"), lax.axis_index("y"), lax.axis_index("c")
    me = 4 * x + 2 * y + c
    sibling = (x, y, 1 - c)

    def place(kk):
        px = 1 - x if kk & 4 else x
        py = 1 - y if kk & 2 else y
        pc = 1 - c if kk & 1 else c
        return (px, py, pc), 4 * px + 2 * py + pc

    def copy(bi, kk, src, dst, to):
        sem = bi * (N_DEV - 1) + kk - 1
        return pltpu.make_async_remote_copy(src_ref=src, dst_ref=dst, send_sem=send_sems.at[sem],
                                            recv_sem=recv_sems.at[sem], device_id=to, device_id_type=pl.DeviceIdType.MESH)

    local, sends = [], []
    for bi, (src, dst) in enumerate(zip(src_refs, dst_refs)):
        local.append(pltpu.make_async_copy(src, dst.at[me], local_sems.at[bi]))
        sends += [copy(bi, kk, src, dst.at[me], place(kk)[0]) for kk in (1, 2, 4, 6)]
    if start:
        for cp in local + sends:
            cp.start()
    if not finish:
        return
    for kk in (2, 4, 6):
        for bi, (src, dst) in enumerate(zip(src_refs, dst_refs)):
            dev, pid = place(kk)
            copy(bi, kk, src, dst.at[pid], dev).wait_recv()
            passed = copy(bi, kk | 1, dst.at[pid], dst.at[pid], sibling)
            passed.start()
            sends.append(passed)
    for kk in (1, 3, 5, 7):
        for bi, (src, dst) in enumerate(zip(src_refs, dst_refs)):
            dev, pid = place(kk)
            copy(bi, kk, src, dst.at[pid], sibling).wait_recv()
    for cp in sends:
        cp.wait_send()
    for cp in local:
        cp.wait()


def _exchange(name, bufs, by_chip=False):
    n_buf = len(bufs)
    gathers = [g for _, g in bufs]
    assert not by_chip or all(gathers)

    def body(*refs):
        srcs, dsts = refs[:n_buf], refs[n_buf:2 * n_buf]
        if by_chip:
            _gather_by_chip(srcs, dsts, *refs[2 * n_buf:])
            return
        copies = _xchg_copies(srcs, dsts, gathers, *refs[2 * n_buf:])
        _xchg_start(copies)
        _xchg_finish(copies)

    return pl.pallas_call(
        body, name=name, out_shape=_xchg_out_shapes(bufs),
        in_specs=[HBM_SPEC] * n_buf, out_specs=[HBM_SPEC] * n_buf, scratch_shapes=_xchg_scratch(n_buf),
    )(*[a for a, _ in bufs])


def _chunk_mask(t, keys_first):
    key = lax.broadcasted_iota(jnp.int32, (t, t), 0 if keys_first else 1) // CHUNK
    query = lax.broadcasted_iota(jnp.int32, (t, t), 1 if keys_first else 0) // CHUNK
    return query >= key


def _grid_ends(grid):
    ids = [pl.program_id(ax) for ax in range(len(grid))]
    first = functools.reduce(jnp.logical_and, [i == 0 for i in ids])
    last = functools.reduce(jnp.logical_and, [i == g - 1 for i, g in zip(ids, grid)])
    return first, last


def _attn_fwd(q, k, v, bufs, n_b, seq):
    tq = min(ATT_TILE, seq)
    nq = seq // tq
    nt_dims = _DIMS["nt"]
    hpb = ATT_HEADS
    grid = (n_b, N_HEADS // hpb, nq)
    n_buf = len(bufs)
    gathers = [g for _, g in bufs]
    sum_lane = [HEAD_PAD - 1 if hh % 2 == 0 else 0 for hh in range(hpb)]

    def body(q_ref, k_ref, v_ref, *rest):
        srcs, (o_ref, lse_ref), dsts = rest[:n_buf], rest[n_buf:n_buf + 2], rest[n_buf + 2:2 * n_buf + 2]
        gather = functools.partial(_gather_by_chip, srcs, dsts, *rest[2 * n_buf + 2:])
        first, last = _grid_ends(grid)
        pl.when(first)(functools.partial(gather, start=True, finish=False))

        qi = pl.program_id(2)
        mask = _chunk_mask(tq, keys_first=False)
        lane_row = lax.broadcasted_iota(jnp.int32, (1, HEAD_PAD), 1)
        ones = [(lane_row == sum_lane[hh]).astype(BF16) for hh in range(hpb)]
        qs = [q_ref[0, hh] for hh in range(hpb)]

        def step(j, carry, masked):
            rows = pl.ds(pl.multiple_of(j * tq, tq), tq)
            out = []
            for hh in range(hpb):
                m, acc = carry[hh]
                s = lax.dot_general(qs[hh], k_ref[0, hh, rows, :], nt_dims, preferred_element_type=F32)
                if masked:
                    s = jnp.where(mask, s, NEG_BIG)
                m_new = jnp.maximum(m, jnp.max(s, axis=-1, keepdims=True))
                p = jnp.exp2(s - m_new).astype(BF16)
                acc = jnp.exp2(m - m_new) * acc + jnp.dot(p, v_ref[0, hh, rows, :] + ones[hh], preferred_element_type=F32)
                out.append((m_new, acc))
            return tuple(out)

        init = tuple((jnp.full((tq, 1), NEG_BIG, F32), jnp.zeros((tq, HEAD_PAD), F32)) for _ in range(hpb))
        online = lax.fori_loop(0, qi, functools.partial(step, masked=False), init)
        online = step(qi, online, True)
        ms, accs = [c[0] for c in online], [c[1] for c in online]
        carry = list(zip(ms, accs))
        lane = lax.broadcasted_iota(jnp.int32, (tq, HEAD_PAD), 1)
        for pair in range(hpb // 2):
            outs = []
            for hh in (2 * pair, 2 * pair + 1):
                m, acc = carry[hh]
                l = jnp.sum(jnp.where(lane == sum_lane[hh], acc, 0.0), axis=-1, keepdims=True)
                outs.append(acc * (1.0 / l))
                lse_ref[0, hh] = jnp.broadcast_to(m + jnp.log2(l), (tq, HEAD_PAD)).T[0:8, :]
            o_ref[0, :, pair * HEAD_PAD:(pair + 1) * HEAD_PAD] = jnp.where(lane < V_HEAD_DIM, outs[0], outs[1]).astype(BF16)

        pl.when(last)(functools.partial(gather, start=False, finish=True))

    kv_spec = pl.BlockSpec((1, hpb, seq, HEAD_PAD), lambda b, hb, i: (b, hb, 0, 0))
    q_spec = pl.BlockSpec((1, hpb, tq, HEAD_PAD), lambda b, hb, i: (b, hb, i, 0))
    res = pl.pallas_call(
        body, name="attn_fwd", grid=grid,
        in_specs=[q_spec, kv_spec, kv_spec] + [HBM_SPEC] * n_buf,
        out_specs=[pl.BlockSpec((1, tq, hpb * V_HEAD_DIM), lambda b, hb, i: (b, i, hb)),
                   pl.BlockSpec((1, hpb, 8, tq), lambda b, hb, i: (b, hb, 0, i))] + [HBM_SPEC] * n_buf,
        out_shape=[jax.ShapeDtypeStruct((n_b, seq, N_HEADS * V_HEAD_DIM), BF16),
                   jax.ShapeDtypeStruct((n_b, N_HEADS, 8, seq), F32)] + _xchg_out_shapes(bufs),
        scratch_shapes=_xchg_scratch(n_buf),
        compiler_params=_params(("arbitrary", "arbitrary", "arbitrary")),
    )(q, k, v, *[a for a, _ in bufs])
    return res[0], res[1], res[2:]


def _attn_bwd(q, k, v, kt, do, o, lse, bufs, n_b, seq):
    tq = min(ATT_TILE, seq)
    nq = seq // tq
    nt_dims = _DIMS["nt"]
    hpb = ATT_HEADS
    grid = (n_b, N_HEADS // hpb, nq)
    n_buf = len(bufs)
    gathers = [g for _, g in bufs]

    def body(q_ref, k_ref, v_ref, kt_ref, do_ref, o_ref, lse_ref, *rest):
        srcs, (dq_ref, dk_ref, dv_ref), dsts = rest[:n_buf], rest[n_buf:n_buf + 3], rest[n_buf + 3:2 * n_buf + 3]
        dk_acc, dv_acc = rest[2 * n_buf + 3:2 * n_buf + 5]
        copies = _xchg_copies(srcs, dsts, gathers, *rest[2 * n_buf + 5:])
        first, last = _grid_ends(grid)
        pl.when(first)(functools.partial(_xchg_start, copies))

        qi = pl.program_id(2)

        @pl.when(qi == 0)
        def _():
            dk_acc[...] = jnp.zeros_like(dk_acc)
            dv_acc[...] = jnp.zeros_like(dv_acc)

        mask = _chunk_mask(tq, keys_first=True)
        lane = lax.broadcasted_iota(jnp.int32, (tq, HEAD_PAD), 1)
        qs, dos, deltas, lses = [], [], [], []
        for hh in range(hpb):
            cols = slice((hh // 2) * HEAD_PAD, (hh // 2 + 1) * HEAD_PAD)
            do_pair = do_ref[0, :, cols]
            prod = do_pair.astype(F32) * o_ref[0, :, cols].astype(F32)
            delta = jnp.sum(jnp.where(lane // V_HEAD_DIM == hh % 2, prod, 0.0), axis=-1, keepdims=True)
            qs.append(q_ref[0, hh])
            dos.append(do_pair)
            deltas.append(jnp.broadcast_to(delta, (tq, HEAD_PAD)).T[0:1, :])
            lses.append(lse_ref[0, hh][0:1, :])

        def step(j, dqs, masked):
            rows = pl.ds(pl.multiple_of(j * tq, tq), tq)
            out = []
            for hh in range(hpb):
                s = lax.dot_general(k_ref[0, hh, rows, :], qs[hh], nt_dims, preferred_element_type=F32)
                p = jnp.exp2(s - lses[hh])
                if masked:
                    p = jnp.where(mask, p, 0.0)
                dv_acc[hh, rows, :] += jnp.dot(p.astype(BF16), dos[hh], preferred_element_type=F32)
                dp = lax.dot_general(v_ref[0, hh, rows, :], dos[hh], nt_dims, preferred_element_type=F32)
                ds = (p * (dp - deltas[hh])).astype(BF16)
                dk_acc[hh, rows, :] += jnp.dot(ds, qs[hh], preferred_element_type=F32)
                out.append(dqs[hh] + jnp.dot(kt_ref[0, hh, j], ds, preferred_element_type=F32))
            return tuple(out)

        dqs = tuple(jnp.zeros((HEAD_PAD, tq), F32) for _ in range(hpb))
        dqs = lax.fori_loop(0, qi, functools.partial(step, masked=False), dqs)
        dqs = step(qi, dqs, True)
        for hh in range(hpb):
            dq_ref[0, hh] = dqs[hh].T.astype(BF16)

        @pl.when(qi == nq - 1)
        def _():
            dk_ref[0] = (dk_acc[...] * LN2).astype(BF16)
            dv_ref[0] = dv_acc[...].astype(BF16)

        pl.when(last)(functools.partial(_xchg_finish, copies))

    full_spec = pl.BlockSpec((1, hpb, seq, HEAD_PAD), lambda b, hb, i: (b, hb, 0, 0))
    t_spec = pl.BlockSpec((1, hpb, nq, HEAD_PAD, tq), lambda b, hb, i: (b, hb, 0, 0, 0))
    q_spec = pl.BlockSpec((1, hpb, tq, HEAD_PAD), lambda b, hb, i: (b, hb, i, 0))
    o_spec = pl.BlockSpec((1, tq, hpb * V_HEAD_DIM), lambda b, hb, i: (b, i, hb))
    lse_spec = pl.BlockSpec((1, hpb, 8, tq), lambda b, hb, i: (b, hb, 0, i))
    head_shape = jax.ShapeDtypeStruct((n_b, N_HEADS, seq, HEAD_PAD), BF16)
    res = pl.pallas_call(
        body, name="attn_bwd", grid=grid,
        in_specs=[q_spec, full_spec, full_spec, t_spec, o_spec, o_spec, lse_spec] + [HBM_SPEC] * n_buf,
        out_specs=[q_spec, full_spec, full_spec] + [HBM_SPEC] * n_buf,
        out_shape=[head_shape] * 3 + _xchg_out_shapes(bufs),
        scratch_shapes=[pltpu.VMEM((hpb, seq, HEAD_PAD), F32), pltpu.VMEM((hpb, seq, HEAD_PAD), F32)]
        + _xchg_scratch(n_buf),
        compiler_params=_params(("arbitrary", "arbitrary", "arbitrary")),
    )(q, k, v, kt, do, o, lse, *[a for a, _ in bufs])
    return res[0], res[1], res[2], res[3:]


def _in_proj_fwd(x2, scale, shift, g, w_parts, z_dtypes, seq):
    n_rows, d = x2.shape
    tm = min(512, seq)
    per_seq = seq // tm
    n_part = len(w_parts)
    nt_dims = _DIMS["nt"]

    def body(x_ref, sc_ref, sh_ref, g_ref, *rest):
        w_refs, h_ref, z_refs = rest[:n_part], rest[n_part], rest[n_part + 1:]
        h = _norm_mod(x_ref[...], g_ref[...], sc_ref[0], sh_ref[0]).astype(BF16)
        h_ref[...] = h
        for w_ref, z_ref in zip(w_refs, z_refs):
            z_ref[...] = lax.dot_general(h, w_ref[...], nt_dims, preferred_element_type=F32).astype(z_ref.dtype)

    row = lambda width: pl.BlockSpec((tm, width), lambda i: (i, 0))
    bat = pl.BlockSpec((1, 1, d), lambda i: (i // per_seq, 0, 0))
    whole = lambda arr: pl.BlockSpec(arr.shape, lambda i: (0, 0))
    return pl.pallas_call(
        body, name="in_proj_fwd", grid=(n_rows // tm,),
        in_specs=[row(d), bat, bat, whole(g)] + [whole(w) for w in w_parts],
        out_specs=[row(d)] + [row(w.shape[0]) for w in w_parts],
        out_shape=[jax.ShapeDtypeStruct((n_rows, d), BF16)]
        + [jax.ShapeDtypeStruct((n_rows, w.shape[0]), dt) for w, dt in zip(w_parts, z_dtypes)],
        compiler_params=_params(("parallel",)),
    )(x2, scale, shift, g, *w_parts)
def _in_proj_bwd(parts, x2, dx1, scale, g, bufs, seq):
    n_rows, d = x2.shape
    tm = min(512, seq)
    per_seq = seq // tm
    grid = (n_rows // tm,)
    n_part, n_buf = len(parts), len(bufs)
    gathers = [gt for _, gt in bufs]

    def body(*refs):
        dz_refs, w_refs = refs[:n_part], refs[n_part:2 * n_part]
        x_ref, dx1_ref, sc_ref, g_ref = refs[2 * n_part:2 * n_part + 4]
        srcs = refs[2 * n_part + 4:2 * n_part + 4 + n_buf]
        gx_ref, dsc_ref, dsh_ref, dg_ref = refs[2 * n_part + 4 + n_buf:2 * n_part + 8 + n_buf]
        dsts = refs[2 * n_part + 8 + n_buf:2 * n_part + 8 + 2 * n_buf]
        copies = _xchg_copies(srcs, dsts, gathers, *refs[2 * n_part + 8 + 2 * n_buf:])
        first, last = _grid_ends(grid)
        pl.when(first)(functools.partial(_xchg_start, copies))

        i = pl.program_id(0)
        dh = None
        for dz_ref, w_ref in zip(dz_refs, w_refs):
            term = jnp.dot(dz_ref[...], w_ref[...], preferred_element_type=F32)
            dh = term if dh is None else dh + term
        dx, dsc, dsh, dg = _norm_mod_bwd(x_ref[...], g_ref[...], sc_ref[0], dh)
        gx_ref[...] = dx1_ref[...] + dx

        @pl.when(i % per_seq == 0)
        def _():
            dsc_ref[...] = jnp.zeros_like(dsc_ref)
            dsh_ref[...] = jnp.zeros_like(dsh_ref)

        @pl.when(i == 0)
        def _():
            dg_ref[...] = jnp.zeros_like(dg_ref)

        dsc_ref[0] += dsc
        dsh_ref[0] += dsh
        dg_ref[...] += dg
        pl.when(last)(functools.partial(_xchg_finish, copies))

    row = lambda width: pl.BlockSpec((tm, width), lambda i: (i, 0))
    bat = pl.BlockSpec((1, 1, d), lambda i: (i // per_seq, 0, 0))
    whole = lambda arr: pl.BlockSpec(arr.shape, lambda i: (0, 0))
    n_b = n_rows // seq
    res = pl.pallas_call(
        body, name="in_proj_bwd", grid=grid,
        in_specs=[row(dz.shape[1]) for dz, _ in parts] + [whole(w) for _, w in parts]
        + [row(d), row(d), bat, whole(g)] + [HBM_SPEC] * n_buf,
        out_specs=[row(d), bat, bat, whole(g)] + [HBM_SPEC] * n_buf,
        out_shape=[jax.ShapeDtypeStruct((n_rows, d), F32), jax.ShapeDtypeStruct((n_b, 1, d), F32),
                   jax.ShapeDtypeStruct((n_b, 1, d), F32), jax.ShapeDtypeStruct(g.shape, F32)] + _xchg_out_shapes(bufs),
        scratch_shapes=_xchg_scratch(n_buf),
        compiler_params=_params(("arbitrary",)),
    )(*[dz for dz, _ in parts], *[w for _, w in parts], x2, dx1, scale, g, *[a for a, _ in bufs])
    return res[0], res[1], res[2], res[3], res[4:]


def _ln_silu(u1, g, b):
    mu = jnp.mean(u1, axis=-1, keepdims=True)
    uc = u1 - mu
    r = lax.rsqrt(jnp.mean(uc * uc, axis=-1, keepdims=True) + EPS)
    y = uc * r * g + b
    return y * _sigmoid(y)


def _conv_fill_glu(z_ref, u0_ref, seq, tile):
    u0_ref[0:CONV_HALO, :] = jnp.zeros((CONV_HALO, CONV_CH), F32)
    u0_ref[CONV_HALO + seq:CONV_HALO + seq + CONV_TAIL, :] = jnp.zeros((CONV_TAIL, CONV_CH), F32)
    for t in range(seq // tile):
        zt = z_ref[0, t * tile:(t + 1) * tile, :].astype(F32)
        u0_ref[CONV_HALO + t * tile:CONV_HALO + (t + 1) * tile, :] = zt[:, :CONV_CH] * _sigmoid(zt[:, CONV_CH:])


def _conv_windows(ref, views_ref, t, tile):
    for b in range(8):
        views_ref[b] = ref[t * tile + b:t * tile + b + tile + CONV_HALO, :]


def _conv_tap(views_ref, offset, tile):
    return views_ref[offset % 8, 8 * (offset // 8):8 * (offset // 8) + tile, :]


def _conv_tile(u0_ref, views_ref, w_ref, b_ref, t, tile):
    _conv_windows(u0_ref, views_ref, t, tile)
    acc = jnp.broadcast_to(b_ref[...], (tile, CONV_CH))
    for kk in range(CONV_WIDTH):
        acc = acc + w_ref[kk:kk + 1, :] * _conv_tap(views_ref, kk + CONV_HALO - (CONV_WIDTH - 1), tile)
    return acc


def _conv_fwd(zglu, conv_w, conv_b, ln_g, ln_b, n_b, seq):
    tile = min(256, seq)

    def body(z_ref, w_ref, b_ref, g_ref, bb_ref, o_ref, u1_ref, u0_ref, views_ref):
        _conv_fill_glu(z_ref, u0_ref, seq, tile)
        for t in range(seq // tile):
            u1 = _conv_tile(u0_ref, views_ref, w_ref, b_ref, t, tile)
            u1_ref[0, t * tile:(t + 1) * tile, :] = u1
            o_ref[0, t * tile:(t + 1) * tile, :] = _ln_silu(u1, g_ref[...], bb_ref[...]).astype(BF16)

    whole2 = lambda arr: pl.BlockSpec(arr.shape, lambda b: (0, 0))
    seq_spec = pl.BlockSpec((1, seq, CONV_CH), lambda b: (b, 0, 0))
    return pl.pallas_call(
        body, name="conv_fwd", grid=(n_b,),
        in_specs=[pl.BlockSpec((1, seq, 2 * CONV_CH), lambda b: (b, 0, 0)), whole2(conv_w), whole2(conv_b),
                  whole2(ln_g), whole2(ln_b)],
        out_specs=[seq_spec, seq_spec],
        out_shape=[jax.ShapeDtypeStruct((n_b, seq, CONV_CH), BF16), jax.ShapeDtypeStruct((n_b, seq, CONV_CH), F32)],
        scratch_shapes=[pltpu.VMEM((seq + CONV_HALO + CONV_TAIL, CONV_CH), F32),
                        pltpu.VMEM((8, tile + CONV_HALO, CONV_CH), F32)],
        compiler_params=_params(("parallel",)),
    )(zglu, conv_w, conv_b, ln_g, ln_b)


def _conv_bwd(zglu, u1_saved, du3, conv_w, ln_g, ln_b, n_b, seq):
    tile = min(256, seq)
    n_t = seq // tile

    def body(z_ref, u1_ref, du3_ref, w_ref, g_ref, bb_ref, dz_ref, dw_ref, db_ref, dg_ref, dbb_ref, u0_ref, du1_ref,
             u0_views, du1_views):
        @pl.when(pl.program_id(0) == 0)
        def _():
            for r in (dw_ref, db_ref, dg_ref, dbb_ref):
                r[...] = jnp.zeros_like(r)

        _conv_fill_glu(z_ref, u0_ref, seq, tile)
        du1_ref[seq:seq + CONV_HALO + CONV_TAIL, :] = jnp.zeros((CONV_HALO + CONV_TAIL, CONV_CH), F32)
        g = g_ref[...]
        for t in range(n_t):
            u1 = u1_ref[0, t * tile:(t + 1) * tile, :]
            mu = jnp.mean(u1, axis=-1, keepdims=True)
            uc = u1 - mu
            r = lax.rsqrt(jnp.mean(uc * uc, axis=-1, keepdims=True) + EPS)
            xh = uc * r
            y = xh * g + bb_ref[...]
            sg = _sigmoid(y)
            dy = du3_ref[0, t * tile:(t + 1) * tile, :].astype(F32) * (sg * (1.0 + y * (1.0 - sg)))
            dg_ref[...] += jnp.sum(dy * xh, axis=0, keepdims=True)
            dbb_ref[...] += jnp.sum(dy, axis=0, keepdims=True)
            dxh = dy * g
            du1 = r * (dxh - jnp.mean(dxh, axis=-1, keepdims=True) - xh * jnp.mean(dxh * xh, axis=-1, keepdims=True))
            db_ref[...] += jnp.sum(du1, axis=0, keepdims=True)
            du1_ref[t * tile:(t + 1) * tile, :] = du1
        for t in range(n_t):
            du1 = du1_ref[t * tile:(t + 1) * tile, :]
            du0 = jnp.zeros((tile, CONV_CH), F32)
            _conv_windows(u0_ref, u0_views, t, tile)
            _conv_windows(du1_ref, du1_views, t, tile)
            for kk in range(CONV_WIDTH):
                du0 = du0 + w_ref[kk:kk + 1, :] * _conv_tap(du1_views, CONV_WIDTH - 1 - kk, tile)
                u0_tap = _conv_tap(u0_views, kk + CONV_HALO - (CONV_WIDTH - 1), tile)
                dw_ref[kk:kk + 1, :] += jnp.sum(du1 * u0_tap, axis=0, keepdims=True)
            zt = z_ref[0, t * tile:(t + 1) * tile, :].astype(F32)
            ga, sb = zt[:, :CONV_CH], _sigmoid(zt[:, CONV_CH:])
            dz_ref[0, t * tile:(t + 1) * tile, :CONV_CH] = (du0 * sb).astype(BF16)
            dz_ref[0, t * tile:(t + 1) * tile, CONV_CH:] = (du0 * ga * sb * (1.0 - sb)).astype(BF16)

    whole2 = lambda arr: pl.BlockSpec(arr.shape, lambda b: (0, 0))
    z_spec = pl.BlockSpec((1, seq, 2 * CONV_CH), lambda b: (b, 0, 0))
    seq_spec = pl.BlockSpec((1, seq, CONV_CH), lambda b: (b, 0, 0))
    return pl.pallas_call(
        body, name="conv_bwd", grid=(n_b,),
        in_specs=[z_spec, seq_spec, seq_spec, whole2(conv_w), whole2(ln_g), whole2(ln_b)],
        out_specs=[z_spec, whole2(conv_w), whole2(ln_g), whole2(ln_g), whole2(ln_b)],
        out_shape=[jax.ShapeDtypeStruct((n_b, seq, 2 * CONV_CH), BF16), jax.ShapeDtypeStruct(conv_w.shape, F32),
                   jax.ShapeDtypeStruct(ln_g.shape, F32), jax.ShapeDtypeStruct(ln_g.shape, F32),
                   jax.ShapeDtypeStruct(ln_b.shape, F32)],
        scratch_shapes=[pltpu.VMEM((seq + CONV_HALO + CONV_TAIL, CONV_CH), F32)] * 2
        + [pltpu.VMEM((8, tile + CONV_HALO, CONV_CH), F32)] * 2,
        compiler_params=_params(("arbitrary",)),
    )(zglu, u1_saved, du3, conv_w, ln_g, ln_b)


def _sum_parts(name, parts):
    n_parts = parts.shape[0]

    def body(p_ref, o_ref):
        gg = p_ref[0].astype(F32)
        for j in range(1, n_parts):
            gg = gg + p_ref[j].astype(F32)
        o_ref[...] = gg

    return pl.pallas_call(body, name=name, out_shape=jax.ShapeDtypeStruct(parts.shape[1:], F32),
                          compiler_params=_params(None))(parts)


def _adamw(name, w, parts, m, v, transposed=False):
    n_parts = parts.shape[0]
    rows, cols = w.shape
    tr = ADAM_ROWS if rows % ADAM_ROWS == 0 else rows

    def body(w_ref, p_ref, m_ref, v_ref, g_ref, d_ref, nm_ref, nv_ref):
        gg = p_ref[0].astype(F32)
        for j in range(1, n_parts):
            gg = gg + p_ref[j].astype(F32)
        if transposed:
            gg = gg.T
        nm = ADAM_B1 * m_ref[...] + (1.0 - ADAM_B1) * gg
        nv = ADAM_B2 * v_ref[...] + (1.0 - ADAM_B2) * jnp.square(gg)
        m_hat = nm / (1.0 - ADAM_B1 ** ADAM_STEP)
        v_hat = nv / (1.0 - ADAM_B2 ** ADAM_STEP)
        g_ref[...] = gg
        d_ref[...] = -ADAM_LR * (m_hat / (jnp.sqrt(v_hat) + ADAM_EPS) + ADAM_WD * w_ref[...])
        nm_ref[...] = nm
        nv_ref[...] = nv

    shape = jax.ShapeDtypeStruct(w.shape, F32)
    blk = pl.BlockSpec((tr, cols), lambda i: (i, 0))
    p_spec = (pl.BlockSpec((n_parts, cols, tr), lambda i: (0, 0, i)) if transposed
              else pl.BlockSpec((n_parts, tr, cols), lambda i: (0, i, 0)))
    return pl.pallas_call(body, name=name, grid=(rows // tr,), in_specs=[blk, p_spec, blk, blk], out_specs=[blk] * 4,
                          out_shape=[shape] * 4, compiler_params=_params(("parallel",)))(w, parts, m, v)


def _rope_tables(seq):
    inv_freq = ROPE_THETA ** (-jnp.arange(0, QK_ROPE_DIM, 2, dtype=F32) / QK_ROPE_DIM)
    ang = jnp.arange(seq, dtype=F32)[:, None] * inv_freq[None, :]
    cos, sin = jnp.cos(ang), jnp.sin(ang)
    half = QK_ROPE_DIM // 2
    z = lambda n: jnp.zeros((seq, n), F32)
    tail = HEAD_PAD - QK_HEAD_DIM
    cos_t = jnp.concatenate([jnp.ones((seq, QK_NOPE_DIM), F32), cos, cos, z(tail)], axis=1)
    sin_lo = jnp.concatenate([z(QK_NOPE_DIM), -sin, z(half), z(tail)], axis=1)
    sin_hi = jnp.concatenate([z(QK_NOPE_DIM), z(half), sin, z(tail)], axis=1)
    return cos_t, sin_lo, sin_hi


def _pad_lanes(v, width=HEAD_PAD):
    return jnp.pad(v, [(0, 0)] * (v.ndim - 1) + [(0, width - v.shape[-1])])


def _unstack_cols(s):
    return s.transpose(1, 0, 2).reshape(s.shape[1], N_DEV * s.shape[2])


def _stack_cols(g, dtype):
    rows, cols = g.shape
    return g.reshape(rows, N_DEV, cols // N_DEV).transpose(1, 0, 2).astype(dtype)


def kernel(x, c, w_ada, b_ada, norm1_g, w_in, q_latent_g, w_uq, kv_latent_g, w_ukv, qk_norm_q_g, qk_norm_k_g, w_o_mla, conv_w, conv_b, conv_ln_g, conv_ln_b, w_pw_out, w_out, norm2_g, w_ff1, w_ff2, loss_target, m_w_ada, m_b_ada, m_norm1_g, m_w_in, m_q_latent_g, m_w_uq, m_kv_latent_g, m_w_ukv, m_qk_norm_q_g, m_qk_norm_k_g, m_w_o_mla, m_conv_w, m_conv_b, m_conv_ln_g, m_conv_ln_b, m_w_pw_out, m_w_out, m_norm2_g, m_w_ff1, m_w_ff2, v_w_ada, v_b_ada, v_norm1_g, v_w_in, v_q_latent_g, v_w_uq, v_kv_latent_g, v_w_ukv, v_qk_norm_q_g, v_qk_norm_k_g, v_w_o_mla, v_conv_w, v_conv_b, v_conv_ln_g, v_conv_ln_b, v_w_pw_out, v_w_out, v_norm2_g, v_w_ff1, v_w_ff2):
    given = dict(locals())
    local = {n: given[n][0] for n in WEIGHTS}
    vec = {n: local[n].reshape(1, -1) for n in REPLICATED}
    bf = lambda n: local[n].astype(BF16)
    n_b, seq, d = x.shape
    n_rows = n_b * seq
    x2 = x.reshape(n_rows, d)
    t2 = loss_target.reshape(n_rows, d)
    me = 4 * lax.axis_index("x") + 2 * lax.axis_index("y") + lax.axis_index("c")
    ada_cols = local["w_ada"].shape[1]

    tsh = lambda n: local[n].T.astype(BF16)
    c_all, w_in_s, w_uq_s, w_ukv_s, conv_w_s = _exchange(
        "gather_early", [(c, True), (tsh("w_in"), True), (bf("w_uq"), True), (bf("w_ukv"), True), (local["conv_w"], True)],
        by_chip=True)
    w_in_t = w_in_s.reshape(-1, d)
    zrows = lambda n: jnp.zeros((n, d), BF16)
    w_sm_t = jnp.concatenate([w_in_t[:OFF_KV], zrows(QK_NOPE_DIM), w_in_t[OFF_KV:OFF_KR], zrows(HEAD_PAD - QK_HEAD_DIM)], axis=0)
    w_glu_t = w_in_t[OFF_KR:OFF_GLU]
    w_gate_t = w_in_t[OFF_GLU:]
    wuq = _pad_lanes(_unstack_cols(w_uq_s).reshape(Q_LORA, N_HEADS, QK_HEAD_DIM)).reshape(Q_LORA, N_HEADS * HEAD_PAD)
    wukv_f = _unstack_cols(w_ukv_s).reshape(KV_LORA, N_HEADS, QK_NOPE_DIM + V_HEAD_DIM)
    wv = wukv_f[:, :, QK_NOPE_DIM:]
    odd = (jnp.arange(N_HEADS) % 2 == 1)[None, :, None]
    wuv = jnp.where(odd, jnp.pad(wv, ((0, 0), (0, 0), (V_HEAD_DIM, 0))), jnp.pad(wv, ((0, 0), (0, 0), (0, V_HEAD_DIM))))
    wukv = jnp.concatenate([_pad_lanes(wukv_f[:, :, :QK_NOPE_DIM]), wuv], axis=1).reshape(KV_LORA, 2 * N_HEADS * HEAD_PAD)
    gqn = _pad_lanes(vec["qk_norm_q_g"])
    gkn = _pad_lanes(vec["qk_norm_k_g"])
    conv_w_f = jnp.pad(_unstack_cols(conv_w_s), ((0, 1), (0, 0)))
    rope = _rope_tables(seq)

    all_rows = N_DEV * n_b
    pad_rows = (-all_rows) % ROWS_PAD
    c_rows = jnp.pad(c_all.reshape(all_rows, d), ((0, pad_rows), (0, 0)))
    b_cols = lax.dynamic_slice(local["b_ada"], (me * ada_cols,), (ada_cols,))
    mod_cols = _mm("ada_fwd", c_rows, local["w_ada"], "nn", F32, a_fn=_silu, epi=lambda acc, b: acc + b,
                   epi_in=(jnp.broadcast_to(b_cols, (all_rows + pad_rows, ada_cols)),))
    (mod_s,) = _exchange("scatter_mod", [(mod_cols[:all_rows].reshape(N_DEV, n_b, ada_cols), False)])
    mod = mod_s.transpose(1, 0, 2).reshape(n_b, ADA_CHUNKS, 1, d)
    shift1, scale1, gate1, shift2, scale2, gate2 = [mod[:, i] for i in range(ADA_CHUNKS)]

    h, zgate, zglu, zsm = _in_proj_fwd(x2, scale1, shift1, vec["norm1_g"], [w_gate_t, w_glu_t, w_sm_t],
                                       [BF16, BF16, F32], seq)
    q, k, v, kt = _mla_prep_fwd(zsm, wuq, wukv, vec["q_latent_g"], vec["kv_latent_g"], gqn, gkn, rope, n_b, seq)
    attn, lse, (w_o_s, w_pw_s, w_out_s, w_ff1_s, w_ff2_s) = _attn_fwd(
        q, k, v, [(tsh("w_o_mla"), True), (tsh("w_pw_out"), True), (bf("w_out"), True), (tsh("w_ff1"), True),
                  (bf("w_ff2"), True)], n_b, seq)
    w_o_t = w_o_s.reshape(d, -1)
    w_pw_t = w_pw_s.reshape(d, -1)
    w_out_f = w_out_s.reshape(d, d)
    w_ff1_t = w_ff1_s.reshape(-1, d)
    w_ff2_f = w_ff2_s.reshape(-1, d)
    attn2 = attn.reshape(n_rows, N_HEADS * V_HEAD_DIM)
    u3, u1 = _conv_fwd(zglu.reshape(n_b, seq, 2 * CONV_CH), conv_w_f, vec["conv_b"], vec["conv_ln_g"], vec["conv_ln_b"], n_b, seq)
    u32 = u3.reshape(n_rows, CONV_CH)
    ya = _mm("mla_out", attn2, w_o_t, "nt", BF16)
    yb = _mm("conv_out", u32, w_pw_t, "nt", BF16)
    mmr = functools.partial(_mm_rows, n_rows=n_rows, seq=seq)

    def merge_fn(t):
        return _sigmoid(t[0]) * t[2] + _sigmoid(t[1]) * t[3]

    def mid_fn(acc, r, b, cc):
        x1_ = r[0] + b[0] * acc
        return [acc, x1_, _norm_mod(x1_, cc[0], b[1], b[2])], [], []

    mrg, mixed, x1, h2 = mmr("out_proj", [(zgate, d, 0), (zgate, d, 1), (ya, d, 0), (yb, d, 0)], merge_fn, w_out_f, "nn",
                             mid_fn, rows=[_full(x2)], bats=[gate1, scale2, shift2], consts=[vec["norm2_g"]],
                             outs=[(d, BF16), (d, F32), (d, BF16)], a_out=BF16)

    a = _mm("ff1", h2, w_ff1_t, "nt", BF16)

    def loss_fn(ff, r, b, cc):
        err = r[0] + b[0] * ff - r[1]
        dy_ = err * (1.0 / d)
        sq = jnp.broadcast_to(jnp.sum(err * err, keepdims=True), (1, LANES))
        return [dy_, b[0] * dy_], [jnp.sum(dy_ * ff, axis=0, keepdims=True)], [sq]

    dy, df, dgate2, sq_err = mmr("ff2_loss", [(a, a.shape[1], 0)], lambda t: _relu2(t[0]), w_ff2_f, "nn", loss_fn,
                                 rows=[_full(x1), _full(t2)], bats=[gate2], outs=[(d, F32), (d, BF16)], bat_outs=[d],
                                 tot_outs=[(1, LANES)], tk=a.shape[1])

    da = _mm("ff2_bwd", df, w_ff2_f, "nt", BF16, epi=lambda acc, av: acc * 2.0 * jnp.maximum(av, 0.0), epi_in=(a,))
    g_ff2 = _mm("ff2_dw", a, df, "tn", BF16, a_fn=_relu2)
    g_ff1_t = _mm("ff1_dw", da, h2, "tn", BF16)

    def mid_bwd(dh2_, r, b, cc):
        dx, dsc, dsh, dg = _norm_mod_bwd(r[0], cc[0], b[0], dh2_)
        dx1_ = r[1] + dx
        return [dx1_, b[1] * dx1_], [dsc, dsh, jnp.sum(dx1_ * r[2].astype(F32), axis=0, keepdims=True)], [dg]

    dx1, dmixed, dscale2, dshift2, dgate1, g_norm2 = mmr(
        "ff1_bwd", [(da, da.shape[1], 0)], None, w_ff1_t, "nn", mid_bwd, rows=[_full(x1), _full(dy), _full(mixed)],
        bats=[scale2, gate1], consts=[vec["norm2_g"]], outs=[(d, F32), (d, BF16)], bat_outs=[d, d, d],
        tot_outs=[(1, d)], tk=da.shape[1])

    g_out = _mm("out_proj_dw", mrg, dmixed, "tn", BF16)

    def merge_bwd(dm, r, b, cc):
        ya_, yb_ = r[2].astype(F32), r[3].astype(F32)
        sa, sb = _sigmoid(r[0].astype(F32)), _sigmoid(r[1].astype(F32))
        return [dm * ya_ * sa * (1.0 - sa), dm * yb_ * sb * (1.0 - sb), dm * sa, dm * sb], [], []

    dzga, dzgb, dya, dyb = mmr("out_proj_bwd", [(dmixed, d, 0)], None, w_out_f, "nt", merge_bwd,
                               rows=[(zgate, d, 0), (zgate, d, 1), _full(ya), _full(yb)], outs=[(d, BF16)] * 4)
    dattn = _mm("mla_out_bwd", dya, w_o_t, "nn", BF16)
    g_o_t = _mm("mla_out_dw", dya, attn2, "tn", BF16)
    du3 = _mm("conv_out_bwd", dyb, w_pw_t, "nn", BF16)
    g_pw_t = _mm("conv_out_dw", dyb, u32, "tn", BF16)

    dzglu, g_conv_w, g_conv_b, g_ln_g, g_ln_b = _conv_bwd(
        zglu.reshape(n_b, seq, 2 * CONV_CH), u1, du3.reshape(n_b, seq, CONV_CH), conv_w_f, vec["conv_ln_g"],
        vec["conv_ln_b"], n_b, seq)
    dzglu = dzglu.reshape(n_rows, 2 * CONV_CH)

    dq, dk, dv, (p_ff2, p_ff1, p_out, p_pw, p_o) = _attn_bwd(
        q, k, v, kt, dattn.reshape(n_b, seq, N_HEADS * V_HEAD_DIM), attn, lse,
        [(g_ff2.reshape(N_DEV, -1, d), False), (g_ff1_t.reshape(N_DEV, -1, d), False), (g_out.reshape(N_DEV, -1, d), False),
         (g_pw_t.reshape(N_DEV, -1, CONV_CH), False), (g_o_t.reshape(N_DEV, -1, N_HEADS * V_HEAD_DIM), False)], n_b, seq)
    dzsm, g_wuq, g_wukv, g_gq, g_gkv, g_gqn, g_gkn = _mla_prep_bwd(
        zsm, dq, dk, dv, wuq, wukv, vec["q_latent_g"], vec["kv_latent_g"], gqn, gkn, rope, n_b, seq)

    g_gate_a_t = _mm("in_proj_gate_dw_a", dzga, h, "tn", BF16)
    g_gate_b_t = _mm("in_proj_gate_dw_b", dzgb, h, "tn", BF16)
    g_glu_t = _mm("in_proj_glu_dw", dzglu, h, "tn", BF16)
    g_sm_t = _mm("in_proj_sm_dw", dzsm, h, "tn", BF16)
    g_in_t = jnp.concatenate([g_sm_t[:OFF_KV], g_sm_t[OFF_KV + QK_NOPE_DIM:OFF_KV + QK_NOPE_DIM + QK_ROPE_DIM], g_glu_t,
                              g_gate_a_t, g_gate_b_t], axis=0)
    g_uq = g_wuq.reshape(Q_LORA, N_HEADS, HEAD_PAD)[:, :, :QK_HEAD_DIM].reshape(Q_LORA, N_HEADS * QK_HEAD_DIM)
    g_wukv = g_wukv.reshape(KV_LORA, 2, N_HEADS, HEAD_PAD)
    g_v = jnp.where(odd, g_wukv[:, 1, :, V_HEAD_DIM:], g_wukv[:, 1, :, :V_HEAD_DIM])
    g_ukv = jnp.concatenate([g_wukv[:, 0, :, :QK_NOPE_DIM], g_v], axis=2).reshape(KV_LORA, -1)

    grad_x, dscale1, dshift1, g_norm1, (p_in, p_uq, p_ukv, p_conv_w) = _in_proj_bwd(
        [(dzga, w_gate_t[:d]), (dzgb, w_gate_t[d:]), (dzglu, w_glu_t), (dzsm, w_sm_t)], x2, dx1, scale1, vec["norm1_g"],
        [(g_in_t.reshape(N_DEV, -1, d), False), (_stack_cols(g_uq, BF16), False), (_stack_cols(g_ukv, BF16), False),
         (_stack_cols(g_conv_w[:CONV_WIDTH], F32), False)], seq)

    dmod = jnp.concatenate([dshift1, dscale1, dgate1, dshift2, dscale2, dgate2], axis=1).reshape(n_b, N_DEV, ada_cols)
    (dmod_s,) = _exchange("scatter_dmod", [(dmod.transpose(1, 0, 2), False)])
    dmod_rows = jnp.pad(dmod_s.reshape(all_rows, ada_cols), ((0, pad_rows), (0, 0)))
    g_ada = _mm("ada_dw", c_rows, dmod_rows, "tn", F32, a_fn=_silu)
    (g_b_cols,) = _rowwise("ada_db", lambda r, b, cc: ([], [], [jnp.sum(r[0], axis=0, keepdims=True)]),
                           all_rows + pad_rows, all_rows + pad_rows, rows=[_full(dmod_rows)], tot_outs=[(1, ada_cols)])

    partial_of = {"norm1_g": g_norm1, "q_latent_g": g_gq, "kv_latent_g": g_gkv, "qk_norm_q_g": g_gqn,
                  "qk_norm_k_g": g_gkn, "conv_b": g_conv_b, "conv_ln_g": g_ln_g, "conv_ln_b": g_ln_b, "norm2_g": g_norm2}
    names = [n for n in REPLICATED if n != "b_ada"]
    pieces = [_pad_lanes(partial_of[n], -(-partial_of[n].shape[1] // LANES) * LANES) for n in names] + [g_b_cols, sq_err]
    widths = [p.shape[1] for p in pieces]
    small = jnp.concatenate(pieces, axis=1)
    small = _pad_lanes(small, -(-small.shape[1] // (8 * LANES)) * 8 * LANES).reshape(-1, LANES)
    (small_s,) = _exchange("gather_small_grads", [(small, True)])
    small_s = small_s.reshape(N_DEV, 1, -1)
    parts = {}
    off = 0
    for n, wd in zip(names, widths):
        parts[n] = small_s[:, :, off:off + vec[n].shape[1]]
        off += wd
    parts["b_ada"] = small_s[:, 0, off:off + ada_cols].reshape(1, 1, N_DEV * ada_cols)
    loss = jnp.sum(small_s[:, 0, off + ada_cols]) * (0.5 / d)
    g_in_mine = _sum_parts("sum_w_in", p_in).T
    parts.update({"w_ada": g_ada[None], "w_in": g_in_mine[None], "w_uq": p_uq, "w_ukv": p_ukv, "w_o_mla": p_o,
                  "conv_w": p_conv_w, "w_pw_out": p_pw, "w_out": p_out, "w_ff1": p_ff1, "w_ff2": p_ff2})
    transposed = ("w_o_mla", "w_pw_out", "w_ff1")

    grad_out, delta_out, m_out, v_out = [], [], [], []
    for n in WEIGHTS:
        shape2 = local[n].shape if local[n].ndim == 2 else (1, local[n].shape[0])
        g_w, d_w, n_m, n_v = _adamw("adamw_" + n, local[n].reshape(shape2), parts[n], given["m_" + n].reshape(shape2),
                                    given["v_" + n].reshape(shape2), transposed=n in transposed)
        full_shape = given[n].shape
        grad_out.append(g_w.reshape(full_shape))
        delta_out.append(d_w.reshape(full_shape))
        m_out.append(n_m.reshape(full_shape))
        v_out.append(n_v.reshape(full_shape))
    return (loss, grad_x.reshape(n_b, seq, d), *grad_out, *delta_out, *m_out, *v_out)
```

```python
import functools

import jax
import jax.numpy as jnp
from jax import lax
from jax.experimental import pallas as pl
from jax.experimental.pallas import tpu as pltpu

F32 = jnp.float32
BF16 = jnp.bfloat16

N_DEV = 8
EPS = 1e-6
N_HEADS = 8
QK_HEAD_DIM = 96
QK_NOPE_DIM = 64
QK_ROPE_DIM = 32
V_HEAD_DIM = 64
HEAD_PAD = 128
Q_LORA = 256
KV_LORA = 128
CONV_CH = 512
CONV_WIDTH = 31
CONV_HALO = 32
CONV_TAIL = 8
CHUNK = 64
ROPE_THETA = 10000.0
OFF_Q = Q_LORA
OFF_KV = OFF_Q + KV_LORA
OFF_KR = OFF_KV + QK_ROPE_DIM
OFF_GLU = OFF_KR + 2 * CONV_CH
ADA_CHUNKS = 6
ADAM_LR = 0.001
ADAM_B1 = 0.9
ADAM_B2 = 0.999
ADAM_EPS = 1e-08
ADAM_WD = 0.01
ADAM_STEP = 10
LANES = 128
VMEM_LIMIT = 56 * 1024 * 1024
NEG_BIG = -1e30
ATT_HEADS = 4
ATT_TILE = 512
PREP_TILE = 1024
ATT_SCALE = QK_HEAD_DIM ** -0.5
LOG2E = 1.4426950408889634
LN2 = 0.6931471805599453
QK_SCALE = ATT_SCALE * LOG2E
ADAM_ROWS = 256
ROWS_PAD = 16

REPLICATED = ("b_ada", "norm1_g", "q_latent_g", "kv_latent_g", "qk_norm_q_g", "qk_norm_k_g", "conv_b", "conv_ln_g",
              "conv_ln_b", "norm2_g")
WEIGHTS = ("w_ada", "b_ada", "norm1_g", "w_in", "q_latent_g", "w_uq", "kv_latent_g", "w_ukv", "qk_norm_q_g",
           "qk_norm_k_g", "w_o_mla", "conv_w", "conv_b", "conv_ln_g", "conv_ln_b", "w_pw_out", "w_out", "norm2_g",
           "w_ff1", "w_ff2")


def _tile(dim, pref):
    if dim <= pref:
        return dim
    t = (pref // LANES) * LANES
    while dim % t:
        t -= LANES
    return t


def _params(semantics):
    return pltpu.CompilerParams(dimension_semantics=semantics, vmem_limit_bytes=VMEM_LIMIT)


def _sigmoid(v):
    return 1.0 / (1.0 + jnp.exp(-v))


def _silu(v):
    return v * _sigmoid(v)


def _relu2(v):
    return jnp.square(jnp.maximum(v, 0.0))


_DIMS = {"nn": (((1,), (0,)), ((), ())), "nt": (((1,), (1,)), ((), ())), "tn": (((0,), (0,)), ((), ()))}


def _mm(name, a, b, mode, out_dtype, *, a_fn=None, epi=None, epi_in=(), tm=1024, tn=1024, tk=1024):
    if mode == "nn":
        (m, k), n = a.shape, b.shape[1]
    elif mode == "nt":
        (m, k), n = a.shape, b.shape[0]
    else:
        (k, m), n = a.shape, b.shape[1]
    tm, tn, tk = _tile(m, tm), _tile(n, tn), _tile(k, tk)
    nk = k // tk
    a_spec = (pl.BlockSpec((tk, tm), lambda i, j, kk: (kk, i)) if mode == "tn"
              else pl.BlockSpec((tm, tk), lambda i, j, kk: (i, kk)))
    b_spec = (pl.BlockSpec((tn, tk), lambda i, j, kk: (j, kk)) if mode == "nt"
              else pl.BlockSpec((tk, tn), lambda i, j, kk: (kk, j)))
    o_spec = e_spec = pl.BlockSpec((tm, tn), lambda i, j, kk: (i, j))
    out_shape = jax.ShapeDtypeStruct((m, n), out_dtype)
    n_epi = len(epi_in)

    def body(a_ref, b_ref, *rest):
        epi_refs, o_ref, acc_ref = rest[:n_epi], rest[n_epi], rest[n_epi + 1]
        kk = pl.program_id(2)

        @pl.when(kk == 0)
        def _():
            acc_ref[...] = jnp.zeros_like(acc_ref)

        av = a_ref[...]
        if a_fn is not None:
            av = a_fn(av.astype(F32))
        acc_ref[...] += lax.dot_general(av.astype(BF16), b_ref[...].astype(BF16), _DIMS[mode],
                                        preferred_element_type=F32)

        @pl.when(kk == nk - 1)
        def _():
            acc = acc_ref[...]
            if epi is not None:
                acc = epi(acc, *[r[...].astype(F32) for r in epi_refs])
            o_ref[...] = acc.astype(out_dtype)

    return pl.pallas_call(
        body, name=name, grid=(m // tm, n // tn, nk),
        in_specs=[a_spec, b_spec] + [e_spec] * n_epi, out_specs=o_spec, out_shape=out_shape,
        scratch_shapes=[pltpu.VMEM((tm, tn), F32)],
        compiler_params=_params(("parallel", "parallel", "arbitrary")),
    )(a, b, *epi_in)


def _rowwise(name, fn, n_rows, seq, rows, bats=(), consts=(), outs=(), bat_outs=(), tot_outs=(), tm=256):
    tm = min(tm, seq)
    per_seq = seq // tm
    n_b = n_rows // seq
    nr, nb, nc, no, nbo, nto = len(rows), len(bats), len(consts), len(outs), len(bat_outs), len(tot_outs)

    def body(*refs):
        i = pl.program_id(0)
        r_in = [r[...] for r in refs[:nr]]
        b_in = [r[0] for r in refs[nr:nr + nb]]
        c_in = [r[...] for r in refs[nr + nb:nr + nb + nc]]
        o_refs = refs[nr + nb + nc:nr + nb + nc + no]
        bo_refs = refs[nr + nb + nc + no:nr + nb + nc + no + nbo]
        to_refs = refs[nr + nb + nc + no + nbo:]
        o_val, bo_val, to_val = fn(r_in, b_in, c_in)
        for r, v in zip(o_refs, o_val):
            r[...] = v.astype(r.dtype)
        if nbo:
            @pl.when(i % per_seq == 0)
            def _():
                for r in bo_refs:
                    r[...] = jnp.zeros_like(r)

            for r, v in zip(bo_refs, bo_val):
                r[0] += v
        if nto:
            @pl.when(i == 0)
            def _():
                for r in to_refs:
                    r[...] = jnp.zeros_like(r)

            for r, v in zip(to_refs, to_val):
                r[...] += v

    in_specs = [pl.BlockSpec((tm, w), functools.partial(lambda cb, i: (i, cb), cb)) for (_, w, cb) in rows]
    in_specs += [pl.BlockSpec((1, 1, bt.shape[2]), lambda i: (i // per_seq, 0, 0)) for bt in bats]
    in_specs += [pl.BlockSpec(ct.shape, lambda i: (0, 0)) for ct in consts]
    out_specs = [pl.BlockSpec((tm, w), lambda i: (i, 0)) for (w, _) in outs]
    out_specs += [pl.BlockSpec((1, 1, w), lambda i: (i // per_seq, 0, 0)) for w in bat_outs]
    out_specs += [pl.BlockSpec(shp, lambda i: (0, 0)) for shp in tot_outs]
    out_shape = [jax.ShapeDtypeStruct((n_rows, w), dt) for (w, dt) in outs]
    out_shape += [jax.ShapeDtypeStruct((n_b, 1, w), F32) for w in bat_outs]
    out_shape += [jax.ShapeDtypeStruct(shp, F32) for shp in tot_outs]
    res = pl.pallas_call(
        body, name=name, grid=(n_rows // tm,), in_specs=in_specs, out_specs=out_specs, out_shape=out_shape,
        compiler_params=_params(("arbitrary",)),
    )(*[r[0] for r in rows], *bats, *consts)
    return res


def _full(arr):
    return (arr, arr.shape[1], 0)


def _mm_rows(name, a_rows, a_fn, w, mode, fn, n_rows, seq, rows=(), bats=(), consts=(), outs=(), bat_outs=(),
             tot_outs=(), a_out=None, tm=512, tk=1024):
    tm = min(tm, seq)
    per_seq = seq // tm
    n_b = n_rows // seq
    k = a_rows[0][1]
    if mode == "nt":
        n_out, tk = w.shape[0], _tile(k, tk)
        w_spec = pl.BlockSpec((n_out, tk), lambda i, kk: (0, kk))
    else:
        n_out, tk = w.shape[1], _tile(k, tk)
        w_spec = pl.BlockSpec((tk, n_out), lambda i, kk: (kk, 0))
    nk = k // tk
    na, nr, nb, nc = len(a_rows), len(rows), len(bats), len(consts)
    n_extra = 0 if a_out is None else 1
    no, nbo, nto = len(outs), len(bat_outs), len(tot_outs)

    def body(*refs):
        i, kk = pl.program_id(0), pl.program_id(1)
        a_refs, w_ref = refs[:na], refs[na]
        pos = na + 1
        r_refs, b_refs, c_refs = refs[pos:pos + nr], refs[pos + nr:pos + nr + nb], refs[pos + nr + nb:pos + nr + nb + nc]
        pos += nr + nb + nc
        ao_refs = refs[pos:pos + n_extra]
        pos += n_extra
        o_refs, bo_refs, to_refs = refs[pos:pos + no], refs[pos + no:pos + no + nbo], refs[pos + no + nbo:pos + no + nbo + nto]
        acc_ref = refs[pos + no + nbo + nto]

        @pl.when(kk == 0)
        def _():
            acc_ref[...] = jnp.zeros_like(acc_ref)

        tiles = [r[...] for r in a_refs]
        av = a_fn([t.astype(F32) for t in tiles]) if a_fn is not None else tiles[0]
        av = av.astype(BF16)
        if n_extra:
            ao_refs[0][...] = av.astype(ao_refs[0].dtype)
        acc_ref[...] += lax.dot_general(av, w_ref[...].astype(BF16), _DIMS[mode], preferred_element_type=F32)

        @pl.when(kk == nk - 1)
        def _():
            o_val, bo_val, to_val = fn(acc_ref[...], [r[...] for r in r_refs], [r[0] for r in b_refs],
                                       [r[...] for r in c_refs])
            for r, v in zip(o_refs, o_val):
                r[...] = v.astype(r.dtype)
            if nbo:
                @pl.when(i % per_seq == 0)
                def _():
                    for r in bo_refs:
                        r[...] = jnp.zeros_like(r)

                for r, v in zip(bo_refs, bo_val):
                    r[0] += v
            if nto:
                @pl.when(i == 0)
                def _():
                    for r in to_refs:
                        r[...] = jnp.zeros_like(r)

                for r, v in zip(to_refs, to_val):
                    r[...] += v

    in_specs = [pl.BlockSpec((tm, tk), functools.partial(lambda cb, i, kk: (i, kk + cb), cb)) for (_, _, cb) in a_rows]
    in_specs += [w_spec]
    in_specs += [pl.BlockSpec((tm, wd), functools.partial(lambda cb, i, kk: (i, cb), cb)) for (_, wd, cb) in rows]
    in_specs += [pl.BlockSpec((1, 1, bt.shape[2]), lambda i, kk: (i // per_seq, 0, 0)) for bt in bats]
    in_specs += [pl.BlockSpec(ct.shape, lambda i, kk: (0, 0)) for ct in consts]
    out_specs = [pl.BlockSpec((tm, tk), lambda i, kk: (i, kk))] * n_extra
    out_specs += [pl.BlockSpec((tm, wd), lambda i, kk: (i, 0)) for (wd, _) in outs]
    out_specs += [pl.BlockSpec((1, 1, wd), lambda i, kk: (i // per_seq, 0, 0)) for wd in bat_outs]
    out_specs += [pl.BlockSpec(shp, lambda i, kk: (0, 0)) for shp in tot_outs]
    out_shape = [jax.ShapeDtypeStruct((n_rows, k), a_out)] if n_extra else []
    out_shape += [jax.ShapeDtypeStruct((n_rows, wd), dt) for (wd, dt) in outs]
    out_shape += [jax.ShapeDtypeStruct((n_b, 1, wd), F32) for wd in bat_outs]
    out_shape += [jax.ShapeDtypeStruct(shp, F32) for shp in tot_outs]
    return pl.pallas_call(
        body, name=name, grid=(n_rows // tm, nk), in_specs=in_specs, out_specs=out_specs, out_shape=out_shape,
        scratch_shapes=[pltpu.VMEM((tm, n_out), F32)],
        compiler_params=_params(("arbitrary", "arbitrary")),
    )(*[a for a, _, _ in a_rows], w, *[r[0] for r in rows], *bats, *consts)


def _norm_mod(x, g, scale, shift):
    r = lax.rsqrt(jnp.mean(x * x, axis=-1, keepdims=True) + EPS)
    xh = x * r
    return xh * g * (1.0 + scale) + shift


def _norm_mod_bwd(x, g, scale, dh):
    r = lax.rsqrt(jnp.mean(x * x, axis=-1, keepdims=True) + EPS)
    xh = x * r
    dn = dh * (1.0 + scale)
    dxh = dn * g
    dx = r * (dxh - xh * jnp.mean(dxh * xh, axis=-1, keepdims=True))
    dscale = jnp.sum(dh * xh * g, axis=0, keepdims=True)
    dshift = jnp.sum(dh, axis=0, keepdims=True)
    dg = jnp.sum(dn * xh, axis=0, keepdims=True)
    return dx, dscale, dshift, dg


def _rms(v, g):
    r = lax.rsqrt(jnp.mean(v * v, axis=-1, keepdims=True) + EPS)
    return v * r * g


def _rms_bwd(v, g, dy):
    r = lax.rsqrt(jnp.mean(v * v, axis=-1, keepdims=True) + EPS)
    vh = v * r
    dvh = dy * g
    dv = r * (dvh - vh * jnp.mean(dvh * vh, axis=-1, keepdims=True))
    return dv, jnp.sum(dy * vh, axis=0, keepdims=True)


def _head_norm(v, g):
    r = lax.rsqrt(jnp.sum(v * v, axis=-1, keepdims=True) * (1.0 / QK_HEAD_DIM) + EPS)
    return v * r * g


def _head_norm_bwd(v, g, dy):
    r = lax.rsqrt(jnp.sum(v * v, axis=-1, keepdims=True) * (1.0 / QK_HEAD_DIM) + EPS)
    vh = v * r
    dvh = dy * g
    dv = r * (dvh - vh * (jnp.sum(dvh * vh, axis=-1, keepdims=True) * (1.0 / QK_HEAD_DIM)))
    return dv, jnp.sum(dy * vh, axis=0, keepdims=True)


def _rope(v, cos, sin):
    return v * cos + pltpu.roll(v, HEAD_PAD // 2, 1) * sin


def _rope_bwd(g, cos, sin):
    return g * cos + pltpu.roll(g * sin, HEAD_PAD // 2, 1)


def _mla_prep_fwd(zsm, wuq, wukv, gq, gkv, gqn, gkn, rope, n_b, seq):
    n_rows = n_b * seq
    tm = min(PREP_TILE, seq)
    per_seq = seq // tm
    att_tile = min(ATT_TILE, seq)
    k_cols = N_HEADS * HEAD_PAD

    def body(z_ref, wuq_ref, wukv_ref, gq_ref, gkv_ref, gqn_ref, gkn_ref, c_ref, s_ref, q_ref, k_ref, v_ref, kt_ref):
        z = z_ref[...]
        qn = _rms(z[:, :Q_LORA], gq_ref[...]).astype(BF16)
        kvn = _rms(z[:, Q_LORA:Q_LORA + KV_LORA], gkv_ref[...]).astype(BF16)
        krp = z[:, Q_LORA + KV_LORA:]
        cos, sin = c_ref[...], s_ref[...]
        q_all = jnp.dot(qn, wuq_ref[...], preferred_element_type=F32)
        kv_all = jnp.dot(kvn, wukv_ref[...], preferred_element_type=F32)
        for h in range(N_HEADS):
            cols = slice(h * HEAD_PAD, (h + 1) * HEAD_PAD)
            q_ref[0, h] = (_rope(_head_norm(q_all[:, cols], gqn_ref[...]), cos, sin) * QK_SCALE).astype(BF16)
            kh = _rope(_head_norm(kv_all[:, cols] + krp, gkn_ref[...]), cos, sin)
            k_ref[0, h] = kh.astype(BF16)
            for part in range(tm // att_tile):
                kt_ref[0, h, part] = kh[part * att_tile:(part + 1) * att_tile].T.astype(BF16)
            v_ref[0, h] = kv_all[:, k_cols + h * HEAD_PAD:k_cols + (h + 1) * HEAD_PAD].astype(BF16)

    whole2 = lambda arr: pl.BlockSpec(arr.shape, lambda i: (0, 0))
    rope_spec = pl.BlockSpec((tm, HEAD_PAD), lambda i: (i % per_seq, 0))
    head_spec = pl.BlockSpec((1, N_HEADS, tm, HEAD_PAD), lambda i: (i // per_seq, 0, i % per_seq, 0))
    head_shape = jax.ShapeDtypeStruct((n_b, N_HEADS, seq, HEAD_PAD), BF16)
    t_spec = pl.BlockSpec((1, N_HEADS, tm // att_tile, HEAD_PAD, att_tile), lambda i: (i // per_seq, 0, i % per_seq, 0, 0))
    t_shape = jax.ShapeDtypeStruct((n_b, N_HEADS, seq // att_tile, HEAD_PAD, att_tile), BF16)
    return pl.pallas_call(
        body, name="mla_prep_fwd", grid=(n_rows // tm,),
        in_specs=[pl.BlockSpec((tm, 512), lambda i: (i, 0)), whole2(wuq), whole2(wukv),
                  whole2(gq), whole2(gkv), whole2(gqn), whole2(gkn), rope_spec, rope_spec],
        out_specs=[head_spec] * 3 + [t_spec], out_shape=[head_shape] * 3 + [t_shape],
        compiler_params=_params(("parallel",)),
    )(zsm, wuq, wukv, gq, gkv, gqn, gkn, *rope)


def _mla_prep_bwd(zsm, dq, dk, dv, wuq, wukv, gq, gkv, gqn, gkn, rope, n_b, seq):
    n_rows = n_b * seq
    tm = min(PREP_TILE, seq)
    per_seq = seq // tm
    tn_dims = _DIMS["tn"]
    nt_dims = _DIMS["nt"]
    k_cols = N_HEADS * HEAD_PAD

    def body(z_ref, dq_ref, dk_ref, dv_ref, wuq_ref, wukv_ref, gq_ref, gkv_ref, gqn_ref, gkn_ref,
             c_ref, s_ref, dz_ref, dwuq_ref, dwukv_ref, dgq_ref, dgkv_ref, dgqn_ref, dgkn_ref):
        @pl.when(pl.program_id(0) == 0)
        def _():
            for r in (dwuq_ref, dwukv_ref, dgq_ref, dgkv_ref, dgqn_ref, dgkn_ref):
                r[...] = jnp.zeros_like(r)

        z = z_ref[...]
        zq, zkv, krp = z[:, :Q_LORA], z[:, Q_LORA:Q_LORA + KV_LORA], z[:, Q_LORA + KV_LORA:]
        qn = _rms(zq, gq_ref[...]).astype(BF16)
        kvn = _rms(zkv, gkv_ref[...]).astype(BF16)
        cos, sin = c_ref[...], s_ref[...]
        lane = lax.broadcasted_iota(jnp.int32, (tm, HEAD_PAD), 1)
        rope_lanes = (lane % (HEAD_PAD // 2)) < QK_ROPE_DIM // 2
        q_all = jnp.dot(qn, wuq_ref[...], preferred_element_type=F32)
        k_all = jnp.dot(kvn, wukv_ref[:, :k_cols], preferred_element_type=F32)
        dkrp = jnp.zeros((tm, HEAD_PAD), F32)
        dgqn = jnp.zeros((1, HEAD_PAD), F32)
        dgkn = jnp.zeros((1, HEAD_PAD), F32)
        dq_heads, dk_heads = [], []
        for h in range(N_HEADS):
            cols = slice(h * HEAD_PAD, (h + 1) * HEAD_PAD)
            dqh, dg = _head_norm_bwd(q_all[:, cols], gqn_ref[...],
                                     _rope_bwd(dq_ref[0, h].astype(F32) * ATT_SCALE, cos, sin))
            dgqn += dg
            dq_heads.append(dqh.astype(BF16))
            dkh, dg = _head_norm_bwd(k_all[:, cols] + krp, gkn_ref[...], _rope_bwd(dk_ref[0, h].astype(F32), cos, sin))
            dgkn += dg
            dkrp += jnp.where(rope_lanes, dkh, 0.0)
            dk_heads.append(dkh.astype(BF16))
        dq_all = jnp.concatenate(dq_heads, axis=1)
        dkv_all = jnp.concatenate(dk_heads + [dv_ref[0, h] for h in range(N_HEADS)], axis=1)
        dwuq_ref[...] += lax.dot_general(qn, dq_all, tn_dims, preferred_element_type=F32)
        dqn = lax.dot_general(dq_all, wuq_ref[...], nt_dims, preferred_element_type=F32)
        dwukv_ref[...] += lax.dot_general(kvn, dkv_all, tn_dims, preferred_element_type=F32)
        dkvn = lax.dot_general(dkv_all, wukv_ref[...], nt_dims, preferred_element_type=F32)
        dzq, dg = _rms_bwd(zq, gq_ref[...], dqn)
        dgq_ref[...] += dg
        dzkv, dg = _rms_bwd(zkv, gkv_ref[...], dkvn)
        dgkv_ref[...] += dg
        dgqn_ref[...] += dgqn
        dgkn_ref[...] += dgkn
        dz_ref[:, :Q_LORA] = dzq.astype(dz_ref.dtype)
        dz_ref[:, Q_LORA:Q_LORA + KV_LORA] = dzkv.astype(dz_ref.dtype)
        dz_ref[:, Q_LORA + KV_LORA:] = dkrp.astype(dz_ref.dtype)

    whole2 = lambda arr: pl.BlockSpec(arr.shape, lambda i: (0, 0))
    rope_spec = pl.BlockSpec((tm, HEAD_PAD), lambda i: (i % per_seq, 0))
    head_spec = pl.BlockSpec((1, N_HEADS, tm, HEAD_PAD), lambda i: (i // per_seq, 0, i % per_seq, 0))
    row_spec = pl.BlockSpec((tm, 512), lambda i: (i, 0))
    return pl.pallas_call(
        body, name="mla_prep_bwd", grid=(n_rows // tm,),
        in_specs=[row_spec, head_spec, head_spec, head_spec, whole2(wuq), whole2(wukv),
                  whole2(gq), whole2(gkv), whole2(gqn), whole2(gkn), rope_spec, rope_spec],
        out_specs=[row_spec, whole2(wuq), whole2(wukv), whole2(gq), whole2(gkv), whole2(gqn), whole2(gkn)],
        out_shape=[jax.ShapeDtypeStruct((n_rows, 512), BF16),
                   jax.ShapeDtypeStruct(wuq.shape, F32), jax.ShapeDtypeStruct(wukv.shape, F32),
                   jax.ShapeDtypeStruct(gq.shape, F32), jax.ShapeDtypeStruct(gkv.shape, F32),
                   jax.ShapeDtypeStruct(gqn.shape, F32), jax.ShapeDtypeStruct(gkn.shape, F32)],
        compiler_params=_params(("arbitrary",)),
    )(zsm, dq, dk, dv, wuq, wukv, gq, gkv, gqn, gkn, *rope)


HBM_SPEC = pl.BlockSpec(memory_space=pltpu.HBM)


def _xchg_out_shapes(bufs):
    return [jax.ShapeDtypeStruct((N_DEV,) + (a.shape if gather else a.shape[1:]), a.dtype) for a, gather in bufs]


def _xchg_scratch(n_buf):
    return [pltpu.SemaphoreType.DMA((n_buf * (N_DEV - 1),)), pltpu.SemaphoreType.DMA((n_buf * (N_DEV - 1),)),
            pltpu.SemaphoreType.DMA((n_buf,))]


def _xchg_copies(src_refs, dst_refs, gathers, send_sems, recv_sems, local_sems):
    x, y, c = lax.axis_index("x"), lax.axis_index("y"), lax.axis_index("c")
    me = 4 * x + 2 * y + c
    local, starts, arrivals = [], [], []
    for bi, (src, dst, gather) in enumerate(zip(src_refs, dst_refs, gathers)):
        local.append(pltpu.make_async_copy(src if gather else src.at[me], dst.at[me], local_sems.at[bi]))
        for kk in range(1, N_DEV):
            px = 1 - x if kk & 4 else x
            py = 1 - y if kk & 2 else y
            pc = 1 - c if kk & 1 else c
            pid = 4 * px + 2 * py + pc
            sem = bi * (N_DEV - 1) + kk - 1
            starts.append(pltpu.make_async_remote_copy(
                src_ref=src if gather else src.at[pid], dst_ref=dst.at[me],
                send_sem=send_sems.at[sem], recv_sem=recv_sems.at[sem],
                device_id=(px, py, pc), device_id_type=pl.DeviceIdType.MESH))
            arrivals.append(pltpu.make_async_remote_copy(
                src_ref=src if gather else src.at[me], dst_ref=dst.at[pid],
                send_sem=send_sems.at[sem], recv_sem=recv_sems.at[sem],
                device_id=(px, py, pc), device_id_type=pl.DeviceIdType.MESH))
    return local, starts, arrivals


def _xchg_start(copies):
    local, sends, _ = copies
    for cp in local + sends:
        cp.start()


def _xchg_finish(copies):
    local, sends, arrivals = copies
    for cp in arrivals:
        cp.wait_recv()
    for cp in sends:
        cp.wait_send()
    for cp in local:
        cp.wait()


def _gather_by_chip(src_refs, dst_refs, send_sems, recv_sems, local_sems, start=True, finish=True):
    x, y, c = lax.axis_index("x"), lax.axis_index("y"), lax.axis_index("c")
    me = 4 * x + 2 * y + c
    sibling = (x, y, 1 - c)

    def place(kk):
        px = 1 - x if kk & 4 else x
        py = 1 - y if kk & 2 else y
        pc = 1 - c if kk & 1 else c
        return (px, py, pc), 4 * px + 2 * py + pc

    def copy(bi, kk, src, dst, to):
        sem = bi * (N_DEV - 1) + kk - 1
        return pltpu.make_async_remote_copy(src_ref=src, dst_ref=dst, send_sem=send_sems.at[sem],
                                            recv_sem=recv_sems.at[sem], device_id=to, device_id_type=pl.DeviceIdType.MESH)

    local, sends = [], []
    for bi, (src, dst) in enumerate(zip(src_refs, dst_refs)):
        local.append(pltpu.make_async_copy(src, dst.at[me], local_sems.at[bi]))
        sends += [copy(bi, kk, src, dst.at[me], place(kk)[0]) for kk in (1, 2, 4, 6)]
    if start:
        for cp in local + sends:
            cp.start()
    if not finish:
        return
    for kk in (2, 4, 6):
        for bi, (src, dst) in enumerate(zip(src_refs, dst_refs)):
            dev, pid = place(kk)
            copy(bi, kk, src, dst.at[pid], dev).wait_recv()
            passed = copy(bi, kk | 1, dst.at[pid], dst.at[pid], sibling)
            passed.start()
            sends.append(passed)
    for kk in (1, 3, 5, 7):
        for bi, (src, dst) in enumerate(zip(src_refs, dst_refs)):
            dev, pid = place(kk)
            copy(bi, kk, src, dst.at[pid], sibling).wait_recv()
    for cp in sends:
        cp.wait_send()
    for cp in local:
        cp.wait()


def _exchange(name, bufs, by_chip=False):
    n_buf = len(bufs)
    gathers = [g for _, g in bufs]
    assert not by_chip or all(gathers)

    def body(*refs):
        srcs, dsts = refs[:n_buf], refs[n_buf:2 * n_buf]
        if by_chip:
            _gather_by_chip(srcs, dsts, *refs[2 * n_buf:])
            return
        copies = _xchg_copies(srcs, dsts, gathers, *refs[2 * n_buf:])
        _xchg_start(copies)
        _xchg_finish(copies)

    return pl.pallas_call(
        body, name=name, out_shape=_xchg_out_shapes(bufs),
        in_specs=[HBM_SPEC] * n_buf, out_specs=[HBM_SPEC] * n_buf, scratch_shapes=_xchg_scratch(n_buf),
    )(*[a for a, _ in bufs])


def _chunk_mask(t, keys_first):
    key = lax.broadcasted_iota(jnp.int32, (t, t), 0 if keys_first else 1) // CHUNK
    query = lax.broadcasted_iota(jnp.int32, (t, t), 1 if keys_first else 0) // CHUNK
    return query >= key


def _grid_ends(grid):
    ids = [pl.program_id(ax) for ax in range(len(grid))]
    first = functools.reduce(jnp.logical_and, [i == 0 for i in ids])
    last = functools.reduce(jnp.logical_and, [i == g - 1 for i, g in zip(ids, grid)])
    return first, last


def _attn_fwd(q, k, v, bufs, n_b, seq):
    tq = min(ATT_TILE, seq)
    nq = seq // tq
    nt_dims = _DIMS["nt"]
    hpb = ATT_HEADS
    grid = (n_b, N_HEADS // hpb, nq)
    n_buf = len(bufs)
    gathers = [g for _, g in bufs]
    sum_lane = [HEAD_PAD - 1 if hh % 2 == 0 else 0 for hh in range(hpb)]

    def body(q_ref, k_ref, v_ref, *rest):
        srcs, (o_ref, lse_ref), dsts = rest[:n_buf], rest[n_buf:n_buf + 2], rest[n_buf + 2:2 * n_buf + 2]
        gather = functools.partial(_gather_by_chip, srcs, dsts, *rest[2 * n_buf + 2:])
        first, last = _grid_ends(grid)
        pl.when(first)(functools.partial(gather, start=True, finish=False))

        qi = pl.program_id(2)
        mask = _chunk_mask(tq, keys_first=False)
        lane_row = lax.broadcasted_iota(jnp.int32, (1, HEAD_PAD), 1)
        ones = [(lane_row == sum_lane[hh]).astype(BF16) for hh in range(hpb)]
        qs = [q_ref[0, hh] for hh in range(hpb)]

        def step(j, carry, masked):
            rows = pl.ds(pl.multiple_of(j * tq, tq), tq)
            out = []
            for hh in range(hpb):
                m, acc = carry[hh]
                s = lax.dot_general(qs[hh], k_ref[0, hh, rows, :], nt_dims, preferred_element_type=F32)
                if masked:
                    s = jnp.where(mask, s, NEG_BIG)
                m_new = jnp.maximum(m, jnp.max(s, axis=-1, keepdims=True))
                p = jnp.exp2(s - m_new).astype(BF16)
                acc = jnp.exp2(m - m_new) * acc + jnp.dot(p, v_ref[0, hh, rows, :] + ones[hh], preferred_element_type=F32)
                out.append((m_new, acc))
            return tuple(out)

        init = tuple((jnp.full((tq, 1), NEG_BIG, F32), jnp.zeros((tq, HEAD_PAD), F32)) for _ in range(hpb))
        online = lax.fori_loop(0, qi, functools.partial(step, masked=False), init)
        online = step(qi, online, True)
        ms, accs = [c[0] for c in online], [c[1] for c in online]
        carry = list(zip(ms, accs))
        lane = lax.broadcasted_iota(jnp.int32, (tq, HEAD_PAD), 1)
        for pair in range(hpb // 2):
            outs = []
            for hh in (2 * pair, 2 * pair + 1):
                m, acc = carry[hh]
                l = jnp.sum(jnp.where(lane == sum_lane[hh], acc, 0.0), axis=-1, keepdims=True)
                outs.append(acc * (1.0 / l))
                lse_ref[0, hh] = jnp.broadcast_to(m + jnp.log2(l), (tq, HEAD_PAD)).T[0:8, :]
            o_ref[0, :, pair * HEAD_PAD:(pair + 1) * HEAD_PAD] = jnp.where(lane < V_HEAD_DIM, outs[0], outs[1]).astype(BF16)

        pl.when(last)(functools.partial(gather, start=False, finish=True))

    kv_spec = pl.BlockSpec((1, hpb, seq, HEAD_PAD), lambda b, hb, i: (b, hb, 0, 0))
    q_spec = pl.BlockSpec((1, hpb, tq, HEAD_PAD), lambda b, hb, i: (b, hb, i, 0))
    res = pl.pallas_call(
        body, name="attn_fwd", grid=grid,
        in_specs=[q_spec, kv_spec, kv_spec] + [HBM_SPEC] * n_buf,
        out_specs=[pl.BlockSpec((1, tq, hpb * V_HEAD_DIM), lambda b, hb, i: (b, i, hb)),
                   pl.BlockSpec((1, hpb, 8, tq), lambda b, hb, i: (b, hb, 0, i))] + [HBM_SPEC] * n_buf,
        out_shape=[jax.ShapeDtypeStruct((n_b, seq, N_HEADS * V_HEAD_DIM), BF16),
                   jax.ShapeDtypeStruct((n_b, N_HEADS, 8, seq), F32)] + _xchg_out_shapes(bufs),
        scratch_shapes=_xchg_scratch(n_buf),
        compiler_params=_params(("arbitrary", "arbitrary", "arbitrary")),
    )(q, k, v, *[a for a, _ in bufs])
    return res[0], res[1], res[2:]


def _attn_bwd(q, k, v, kt, do, o, lse, bufs, n_b, seq):
    tq = min(ATT_TILE, seq)
    nq = seq // tq
    nt_dims = _DIMS["nt"]
    hpb = ATT_HEADS
    grid = (n_b, N_HEADS // hpb, nq)
    n_buf = len(bufs)
    gathers = [g for _, g in bufs]

    def body(q_ref, k_ref, v_ref, kt_ref, do_ref, o_ref, lse_ref, *rest):
        srcs, (dq_ref, dk_ref, dv_ref), dsts = rest[:n_buf], rest[n_buf:n_buf + 3], rest[n_buf + 3:2 * n_buf + 3]
        dk_acc, dv_acc = rest[2 * n_buf + 3:2 * n_buf + 5]
        copies = _xchg_copies(srcs, dsts, gathers, *rest[2 * n_buf + 5:])
        first, last = _grid_ends(grid)
        pl.when(first)(functools.partial(_xchg_start, copies))

        qi = pl.program_id(2)

        @pl.when(qi == 0)
        def _():
            dk_acc[...] = jnp.zeros_like(dk_acc)
            dv_acc[...] = jnp.zeros_like(dv_acc)

        mask = _chunk_mask(tq, keys_first=True)
        lane = lax.broadcasted_iota(jnp.int32, (tq, HEAD_PAD), 1)
        qs, dos, deltas, lses = [], [], [], []
        for hh in range(hpb):
            cols = slice((hh // 2) * HEAD_PAD, (hh // 2 + 1) * HEAD_PAD)
            do_pair = do_ref[0, :, cols]
            prod = do_pair.astype(F32) * o_ref[0, :, cols].astype(F32)
            delta = jnp.sum(jnp.where(lane // V_HEAD_DIM == hh % 2, prod, 0.0), axis=-1, keepdims=True)
            qs.append(q_ref[0, hh])
            dos.append(do_pair)
            deltas.append(jnp.broadcast_to(delta, (tq, HEAD_PAD)).T[0:1, :])
            lses.append(lse_ref[0, hh][0:1, :])

        def step(j, dqs, masked):
            rows = pl.ds(pl.multiple_of(j * tq, tq), tq)
            out = []
            for hh in range(hpb):
                s = lax.dot_general(k_ref[0, hh, rows, :], qs[hh], nt_dims, preferred_element_type=F32)
                p = jnp.exp2(s - lses[hh])
                if masked:
                    p = jnp.where(mask, p, 0.0)
                dv_acc[hh, rows, :] += jnp.dot(p.astype(BF16), dos[hh], preferred_element_type=F32)
                dp = lax.dot_general(v_ref[0, hh, rows, :], dos[hh], nt_dims, preferred_element_type=F32)
                ds = (p * (dp - deltas[hh])).astype(BF16)
                dk_acc[hh, rows, :] += jnp.dot(ds, qs[hh], preferred_element_type=F32)
                out.append(dqs[hh] + jnp.dot(kt_ref[0, hh, j], ds, preferred_element_type=F32))
            return tuple(out)

        dqs = tuple(jnp.zeros((HEAD_PAD, tq), F32) for _ in range(hpb))
        dqs = lax.fori_loop(0, qi, functools.partial(step, masked=False), dqs)
        dqs = step(qi, dqs, True)
        for hh in range(hpb):
            dq_ref[0, hh] = dqs[hh].T.astype(BF16)

        @pl.when(qi == nq - 1)
        def _():
            dk_ref[0] = (dk_acc[...] * LN2).astype(BF16)
            dv_ref[0] = dv_acc[...].astype(BF16)

        pl.when(last)(functools.partial(_xchg_finish, copies))

    full_spec = pl.BlockSpec((1, hpb, seq, HEAD_PAD), lambda b, hb, i: (b, hb, 0, 0))
    t_spec = pl.BlockSpec((1, hpb, nq, HEAD_PAD, tq), lambda b, hb, i: (b, hb, 0, 0, 0))
    q_spec = pl.BlockSpec((1, hpb, tq, HEAD_PAD), lambda b, hb, i: (b, hb, i, 0))
    o_spec = pl.BlockSpec((1, tq, hpb * V_HEAD_DIM), lambda b, hb, i: (b, i, hb))
    lse_spec = pl.BlockSpec((1, hpb, 8, tq), lambda b, hb, i: (b, hb, 0, i))
    head_shape = jax.ShapeDtypeStruct((n_b, N_HEADS, seq, HEAD_PAD), BF16)
    res = pl.pallas_call(
        body, name="attn_bwd", grid=grid,
        in_specs=[q_spec, full_spec, full_spec, t_spec, o_spec, o_spec, lse_spec] + [HBM_SPEC] * n_buf,
        out_specs=[q_spec, full_spec, full_spec] + [HBM_SPEC] * n_buf,
        out_shape=[head_shape] * 3 + _xchg_out_shapes(bufs),
        scratch_shapes=[pltpu.VMEM((hpb, seq, HEAD_PAD), F32), pltpu.VMEM((hpb, seq, HEAD_PAD), F32)]
        + _xchg_scratch(n_buf),
        compiler_params=_params(("arbitrary", "arbitrary", "arbitrary")),
    )(q, k, v, kt, do, o, lse, *[a for a, _ in bufs])
    return res[0], res[1], res[2], res[3:]


def _in_proj_fwd(x2, scale, shift, g, w_parts, z_dtypes, seq):
    n_rows, d = x2.shape
    tm = min(512, seq)
    per_seq = seq // tm
    n_part = len(w_parts)
    nt_dims = _DIMS["nt"]

    def body(x_ref, sc_ref, sh_ref, g_ref, *rest):
        w_refs, h_ref, z_refs = rest[:n_part], rest[n_part], rest[n_part + 1:]
        h = _norm_mod(x_ref[...], g_ref[...], sc_ref[0], sh_ref[0]).astype(BF16)
        h_ref[...] = h
        for w_ref, z_ref in zip(w_refs, z_refs):
            z_ref[...] = lax.dot_general(h, w_ref[...], nt_dims, preferred_element_type=F32).astype(z_ref.dtype)

    row = lambda width: pl.BlockSpec((tm, width), lambda i: (i, 0))
    bat = pl.BlockSpec((1, 1, d), lambda i: (i // per_seq, 0, 0))
    whole = lambda arr: pl.BlockSpec(arr.shape, lambda i: (0, 0))
    return pl.pallas_call(
        body, name="in_proj_fwd", grid=(n_rows // tm,),
        in_specs=[row(d), bat, bat, whole(g)] + [whole(w) for w in w_parts],
        out_specs=[row(d)] + [row(w.shape[0]) for w in w_parts],
        out_shape=[jax.ShapeDtypeStruct((n_rows, d), BF16)]
        + [jax.ShapeDtypeStruct((n_rows, w.shape[0]), dt) for w, dt in zip(w_parts, z_dtypes)],
        compiler_params=_params(("parallel",)),
    )(x2, scale, shift, g, *w_parts)
def _in_proj_bwd(parts, x2, dx1, scale, g, bufs, seq):
    n_rows, d = x2.shape
    tm = min(512, seq)
    per_seq = seq // tm
    grid = (n_rows // tm,)
    n_part, n_buf = len(parts), len(bufs)
    gathers = [gt for _, gt in bufs]

    def body(*refs):
        dz_refs, w_refs = refs[:n_part], refs[n_part:2 * n_part]
        x_ref, dx1_ref, sc_ref, g_ref = refs[2 * n_part:2 * n_part + 4]
        srcs = refs[2 * n_part + 4:2 * n_part + 4 + n_buf]
        gx_ref, dsc_ref, dsh_ref, dg_ref = refs[2 * n_part + 4 + n_buf:2 * n_part + 8 + n_buf]
        dsts = refs[2 * n_part + 8 + n_buf:2 * n_part + 8 + 2 * n_buf]
        copies = _xchg_copies(srcs, dsts, gathers, *refs[2 * n_part + 8 + 2 * n_buf:])
        first, last = _grid_ends(grid)
        pl.when(first)(functools.partial(_xchg_start, copies))

        i = pl.program_id(0)
        dh = None
        for dz_ref, w_ref in zip(dz_refs, w_refs):
            term = jnp.dot(dz_ref[...], w_ref[...], preferred_element_type=F32)
            dh = term if dh is None else dh + term
        dx, dsc, dsh, dg = _norm_mod_bwd(x_ref[...], g_ref[...], sc_ref[0], dh)
        gx_ref[...] = dx1_ref[...] + dx

        @pl.when(i % per_seq == 0)
        def _():
            dsc_ref[...] = jnp.zeros_like(dsc_ref)
            dsh_ref[...] = jnp.zeros_like(dsh_ref)

        @pl.when(i == 0)
        def _():
            dg_ref[...] = jnp.zeros_like(dg_ref)

        dsc_ref[0] += dsc
        dsh_ref[0] += dsh
        dg_ref[...] += dg
        pl.when(last)(functools.partial(_xchg_finish, copies))

    row = lambda width: pl.BlockSpec((tm, width), lambda i: (i, 0))
    bat = pl.BlockSpec((1, 1, d), lambda i: (i // per_seq, 0, 0))
    whole = lambda arr: pl.BlockSpec(arr.shape, lambda i: (0, 0))
    n_b = n_rows // seq
    res = pl.pallas_call(
        body, name="in_proj_bwd", grid=grid,
        in_specs=[row(dz.shape[1]) for dz, _ in parts] + [whole(w) for _, w in parts]
        + [row(d), row(d), bat, whole(g)] + [HBM_SPEC] * n_buf,
        out_specs=[row(d), bat, bat, whole(g)] + [HBM_SPEC] * n_buf,
        out_shape=[jax.ShapeDtypeStruct((n_rows, d), F32), jax.ShapeDtypeStruct((n_b, 1, d), F32),
                   jax.ShapeDtypeStruct((n_b, 1, d), F32), jax.ShapeDtypeStruct(g.shape, F32)] + _xchg_out_shapes(bufs),
        scratch_shapes=_xchg_scratch(n_buf),
        compiler_params=_params(("arbitrary",)),
    )(*[dz for dz, _ in parts], *[w for _, w in parts], x2, dx1, scale, g, *[a for a, _ in bufs])
    return res[0], res[1], res[2], res[3], res[4:]


def _ln_silu(u1, g, b):
    mu = jnp.mean(u1, axis=-1, keepdims=True)
    uc = u1 - mu
    r = lax.rsqrt(jnp.mean(uc * uc, axis=-1, keepdims=True) + EPS)
    y = uc * r * g + b
    return y * _sigmoid(y)


def _conv_fill_glu(z_ref, u0_ref, seq, tile):
    u0_ref[0:CONV_HALO, :] = jnp.zeros((CONV_HALO, CONV_CH), F32)
    u0_ref[CONV_HALO + seq:CONV_HALO + seq + CONV_TAIL, :] = jnp.zeros((CONV_TAIL, CONV_CH), F32)
    for t in range(seq // tile):
        zt = z_ref[0, t * tile:(t + 1) * tile, :].astype(F32)
        u0_ref[CONV_HALO + t * tile:CONV_HALO + (t + 1) * tile, :] = zt[:, :CONV_CH] * _sigmoid(zt[:, CONV_CH:])


def _conv_windows(ref, views_ref, t, tile):
    for b in range(8):
        views_ref[b] = ref[t * tile + b:t * tile + b + tile + CONV_HALO, :]


def _conv_tap(views_ref, offset, tile):
    return views_ref[offset % 8, 8 * (offset // 8):8 * (offset // 8) + tile, :]


def _conv_tile(u0_ref, views_ref, w_ref, b_ref, t, tile):
    _conv_windows(u0_ref, views_ref, t, tile)
    acc = jnp.broadcast_to(b_ref[...], (tile, CONV_CH))
    for kk in range(CONV_WIDTH):
        acc = acc + w_ref[kk:kk + 1, :] * _conv_tap(views_ref, kk + CONV_HALO - (CONV_WIDTH - 1), tile)
    return acc


def _conv_fwd(zglu, conv_w, conv_b, ln_g, ln_b, n_b, seq):
    tile = min(256, seq)

    def body(z_ref, w_ref, b_ref, g_ref, bb_ref, o_ref, u1_ref, u0_ref, views_ref):
        _conv_fill_glu(z_ref, u0_ref, seq, tile)
        for t in range(seq // tile):
            u1 = _conv_tile(u0_ref, views_ref, w_ref, b_ref, t, tile)
            u1_ref[0, t * tile:(t + 1) * tile, :] = u1
            o_ref[0, t * tile:(t + 1) * tile, :] = _ln_silu(u1, g_ref[...], bb_ref[...]).astype(BF16)

    whole2 = lambda arr: pl.BlockSpec(arr.shape, lambda b: (0, 0))
    seq_spec = pl.BlockSpec((1, seq, CONV_CH), lambda b: (b, 0, 0))
    return pl.pallas_call(
        body, name="conv_fwd", grid=(n_b,),
        in_specs=[pl.BlockSpec((1, seq, 2 * CONV_CH), lambda b: (b, 0, 0)), whole2(conv_w), whole2(conv_b),
                  whole2(ln_g), whole2(ln_b)],
        out_specs=[seq_spec, seq_spec],
        out_shape=[jax.ShapeDtypeStruct((n_b, seq, CONV_CH), BF16), jax.ShapeDtypeStruct((n_b, seq, CONV_CH), F32)],
        scratch_shapes=[pltpu.VMEM((seq + CONV_HALO + CONV_TAIL, CONV_CH), F32),
                        pltpu.VMEM((8, tile + CONV_HALO, CONV_CH), F32)],
        compiler_params=_params(("parallel",)),
    )(zglu, conv_w, conv_b, ln_g, ln_b)


def _conv_bwd(zglu, u1_saved, du3, conv_w, ln_g, ln_b, n_b, seq):
    tile = min(256, seq)
    n_t = seq // tile

    def body(z_ref, u1_ref, du3_ref, w_ref, g_ref, bb_ref, dz_ref, dw_ref, db_ref, dg_ref, dbb_ref, u0_ref, du1_ref,
             u0_views, du1_views):
        @pl.when(pl.program_id(0) == 0)
        def _():
            for r in (dw_ref, db_ref, dg_ref, dbb_ref):
                r[...] = jnp.zeros_like(r)

        _conv_fill_glu(z_ref, u0_ref, seq, tile)
        du1_ref[seq:seq + CONV_HALO + CONV_TAIL, :] = jnp.zeros((CONV_HALO + CONV_TAIL, CONV_CH), F32)
        g = g_ref[...]
        for t in range(n_t):
            u1 = u1_ref[0, t * tile:(t + 1) * tile, :]
            mu = jnp.mean(u1, axis=-1, keepdims=True)
            uc = u1 - mu
            r = lax.rsqrt(jnp.mean(uc * uc, axis=-1, keepdims=True) + EPS)
            xh = uc * r
            y = xh * g + bb_ref[...]
            sg = _sigmoid(y)
            dy = du3_ref[0, t * tile:(t + 1) * tile, :].astype(F32) * (sg * (1.0 + y * (1.0 - sg)))
            dg_ref[...] += jnp.sum(dy * xh, axis=0, keepdims=True)
            dbb_ref[...] += jnp.sum(dy, axis=0, keepdims=True)
            dxh = dy * g
            du1 = r * (dxh - jnp.mean(dxh, axis=-1, keepdims=True) - xh * jnp.mean(dxh * xh, axis=-1, keepdims=True))
            db_ref[...] += jnp.sum(du1, axis=0, keepdims=True)
            du1_ref[t * tile:(t + 1) * tile, :] = du1
        for t in range(n_t):
            du1 = du1_ref[t * tile:(t + 1) * tile, :]
            du0 = jnp.zeros((tile, CONV_CH), F32)
            _conv_windows(u0_ref, u0_views, t, tile)
            _conv_windows(du1_ref, du1_views, t, tile)
            for kk in range(CONV_WIDTH):
                du0 = du0 + w_ref[kk:kk + 1, :] * _conv_tap(du1_views, CONV_WIDTH - 1 - kk, tile)
                u0_tap = _conv_tap(u0_views, kk + CONV_HALO - (CONV_WIDTH - 1), tile)
                dw_ref[kk:kk + 1, :] += jnp.sum(du1 * u0_tap, axis=0, keepdims=True)
            zt = z_ref[0, t * tile:(t + 1) * tile, :].astype(F32)
            ga, sb = zt[:, :CONV_CH], _sigmoid(zt[:, CONV_CH:])
            dz_ref[0, t * tile:(t + 1) * tile, :CONV_CH] = (du0 * sb).astype(BF16)
            dz_ref[0, t * tile:(t + 1) * tile, CONV_CH:] = (du0 * ga * sb * (1.0 - sb)).astype(BF16)

    whole2 = lambda arr: pl.BlockSpec(arr.shape, lambda b: (0, 0))
    z_spec = pl.BlockSpec((1, seq, 2 * CONV_CH), lambda b: (b, 0, 0))
    seq_spec = pl.BlockSpec((1, seq, CONV_CH), lambda b: (b, 0, 0))
    return pl.pallas_call(
        body, name="conv_bwd", grid=(n_b,),
        in_specs=[z_spec, seq_spec, seq_spec, whole2(conv_w), whole2(ln_g), whole2(ln_b)],
        out_specs=[z_spec, whole2(conv_w), whole2(ln_g), whole2(ln_g), whole2(ln_b)],
        out_shape=[jax.ShapeDtypeStruct((n_b, seq, 2 * CONV_CH), BF16), jax.ShapeDtypeStruct(conv_w.shape, F32),
                   jax.ShapeDtypeStruct(ln_g.shape, F32), jax.ShapeDtypeStruct(ln_g.shape, F32),
                   jax.ShapeDtypeStruct(ln_b.shape, F32)],
        scratch_shapes=[pltpu.VMEM((seq + CONV_HALO + CONV_TAIL, CONV_CH), F32)] * 2
        + [pltpu.VMEM((8, tile + CONV_HALO, CONV_CH), F32)] * 2,
        compiler_params=_params(("arbitrary",)),
    )(zglu, u1_saved, du3, conv_w, ln_g, ln_b)


def _sum_parts(name, parts):
    n_parts = parts.shape[0]

    def body(p_ref, o_ref):
        gg = p_ref[0].astype(F32)
        for j in range(1, n_parts):
            gg = gg + p_ref[j].astype(F32)
        o_ref[...] = gg

    return pl.pallas_call(body, name=name, out_shape=jax.ShapeDtypeStruct(parts.shape[1:], F32),
                          compiler_params=_params(None))(parts)


def _adamw(name, w, parts, m, v, transposed=False):
    n_parts = parts.shape[0]
    rows, cols = w.shape
    tr = ADAM_ROWS if rows % ADAM_ROWS == 0 else rows

    def body(w_ref, p_ref, m_ref, v_ref, g_ref, d_ref, nm_ref, nv_ref):
        gg = p_ref[0].astype(F32)
        for j in range(1, n_parts):
            gg = gg + p_ref[j].astype(F32)
        if transposed:
            gg = gg.T
        nm = ADAM_B1 * m_ref[...] + (1.0 - ADAM_B1) * gg
        nv = ADAM_B2 * v_ref[...] + (1.0 - ADAM_B2) * jnp.square(gg)
        m_hat = nm / (1.0 - ADAM_B1 ** ADAM_STEP)
        v_hat = nv / (1.0 - ADAM_B2 ** ADAM_STEP)
        g_ref[...] = gg
        d_ref[...] = -ADAM_LR * (m_hat / (jnp.sqrt(v_hat) + ADAM_EPS) + ADAM_WD * w_ref[...])
        nm_ref[...] = nm
        nv_ref[...] = nv

    shape = jax.ShapeDtypeStruct(w.shape, F32)
    blk = pl.BlockSpec((tr, cols), lambda i: (i, 0))
    p_spec = (pl.BlockSpec((n_parts, cols, tr), lambda i: (0, 0, i)) if transposed
              else pl.BlockSpec((n_parts, tr, cols), lambda i: (0, i, 0)))
    return pl.pallas_call(body, name=name, grid=(rows // tr,), in_specs=[blk, p_spec, blk, blk], out_specs=[blk] * 4,
                          out_shape=[shape] * 4, compiler_params=_params(("parallel",)))(w, parts, m, v)


def _rope_tables(seq):
    inv_freq = ROPE_THETA ** (-jnp.arange(0, QK_ROPE_DIM, 2, dtype=F32) / QK_ROPE_DIM)
    ang = jnp.arange(seq, dtype=F32)[:, None] * inv_freq[None, :]
    cos, sin = jnp.cos(ang), jnp.sin(ang)
    half = QK_ROPE_DIM // 2
    lane_half = HEAD_PAD // 2
    one = lambda n: jnp.ones((seq, n), F32)
    z = lambda n: jnp.zeros((seq, n), F32)
    used_hi = QK_HEAD_DIM - lane_half - half
    cos_t = jnp.concatenate([cos, one(lane_half - half), cos, one(used_hi), z(lane_half - half - used_hi)], axis=1)
    sin_t = jnp.concatenate([-sin, z(lane_half - half), sin, z(lane_half - half)], axis=1)
    return cos_t, sin_t


def _pad_lanes(v, width=HEAD_PAD):
    return jnp.pad(v, [(0, 0)] * (v.ndim - 1) + [(0, width - v.shape[-1])])


_LANE_HALF_NOPE = HEAD_PAD // 2 - QK_ROPE_DIM // 2


def _head_lanes(v):
    rot = v[..., QK_NOPE_DIM:]
    half = QK_ROPE_DIM // 2
    return _pad_lanes(jnp.concatenate([rot[..., :half], v[..., :_LANE_HALF_NOPE], rot[..., half:],
                                       v[..., _LANE_HALF_NOPE:QK_NOPE_DIM]], axis=-1))


def _head_dims(g):
    half = QK_ROPE_DIM // 2
    lane_half = HEAD_PAD // 2
    return jnp.concatenate([g[..., half:lane_half], g[..., lane_half + half:QK_HEAD_DIM], g[..., :half],
                            g[..., lane_half:lane_half + half]], axis=-1)


def _unstack_cols(s):
    return s.transpose(1, 0, 2).reshape(s.shape[1], N_DEV * s.shape[2])


def _stack_cols(g, dtype):
    rows, cols = g.shape
    return g.reshape(rows, N_DEV, cols // N_DEV).transpose(1, 0, 2).astype(dtype)


def kernel(x, c, w_ada, b_ada, norm1_g, w_in, q_latent_g, w_uq, kv_latent_g, w_ukv, qk_norm_q_g, qk_norm_k_g, w_o_mla, conv_w, conv_b, conv_ln_g, conv_ln_b, w_pw_out, w_out, norm2_g, w_ff1, w_ff2, loss_target, m_w_ada, m_b_ada, m_norm1_g, m_w_in, m_q_latent_g, m_w_uq, m_kv_latent_g, m_w_ukv, m_qk_norm_q_g, m_qk_norm_k_g, m_w_o_mla, m_conv_w, m_conv_b, m_conv_ln_g, m_conv_ln_b, m_w_pw_out, m_w_out, m_norm2_g, m_w_ff1, m_w_ff2, v_w_ada, v_b_ada, v_norm1_g, v_w_in, v_q_latent_g, v_w_uq, v_kv_latent_g, v_w_ukv, v_qk_norm_q_g, v_qk_norm_k_g, v_w_o_mla, v_conv_w, v_conv_b, v_conv_ln_g, v_conv_ln_b, v_w_pw_out, v_w_out, v_norm2_g, v_w_ff1, v_w_ff2):
    given = dict(locals())
    local = {n: given[n][0] for n in WEIGHTS}
    vec = {n: local[n].reshape(1, -1) for n in REPLICATED}
    bf = lambda n: local[n].astype(BF16)
    n_b, seq, d = x.shape
    n_rows = n_b * seq
    x2 = x.reshape(n_rows, d)
    t2 = loss_target.reshape(n_rows, d)
    me = 4 * lax.axis_index("x") + 2 * lax.axis_index("y") + lax.axis_index("c")
    ada_cols = local["w_ada"].shape[1]

    tsh = lambda n: local[n].T.astype(BF16)
    c_all, w_in_s, w_uq_s, w_ukv_s, conv_w_s = _exchange(
        "gather_early", [(c, True), (tsh("w_in"), True), (bf("w_uq"), True), (bf("w_ukv"), True), (local["conv_w"], True)],
        by_chip=True)
    w_in_t = w_in_s.reshape(-1, d)
    zrows = lambda n: jnp.zeros((n, d), BF16)
    rot_half = QK_ROPE_DIM // 2
    w_sm_t = jnp.concatenate([w_in_t[:OFF_KV + rot_half], zrows(HEAD_PAD // 2 - rot_half), w_in_t[OFF_KV + rot_half:OFF_KR],
                              zrows(HEAD_PAD // 2 - rot_half)], axis=0)
    w_glu_t = w_in_t[OFF_KR:OFF_GLU]
    w_gate_t = w_in_t[OFF_GLU:]
    wuq = _head_lanes(_unstack_cols(w_uq_s).reshape(Q_LORA, N_HEADS, QK_HEAD_DIM)).reshape(Q_LORA, N_HEADS * HEAD_PAD)
    wukv_f = _unstack_cols(w_ukv_s).reshape(KV_LORA, N_HEADS, QK_NOPE_DIM + V_HEAD_DIM)
    wv = wukv_f[:, :, QK_NOPE_DIM:]
    odd = (jnp.arange(N_HEADS) % 2 == 1)[None, :, None]
    wuv = jnp.where(odd, jnp.pad(wv, ((0, 0), (0, 0), (V_HEAD_DIM, 0))), jnp.pad(wv, ((0, 0), (0, 0), (0, V_HEAD_DIM))))
    wuk = _head_lanes(_pad_lanes(wukv_f[:, :, :QK_NOPE_DIM], QK_HEAD_DIM))
    wukv = jnp.concatenate([wuk, wuv], axis=1).reshape(KV_LORA, 2 * N_HEADS * HEAD_PAD)
    gqn = _head_lanes(vec["qk_norm_q_g"])
    gkn = _head_lanes(vec["qk_norm_k_g"])
    conv_w_f = jnp.pad(_unstack_cols(conv_w_s), ((0, 1), (0, 0)))
    rope = _rope_tables(seq)

    all_rows = N_DEV * n_b
    pad_rows = (-all_rows) % ROWS_PAD
    c_rows = jnp.pad(c_all.reshape(all_rows, d), ((0, pad_rows), (0, 0)))
    b_cols = lax.dynamic_slice(local["b_ada"], (me * ada_cols,), (ada_cols,))
    mod_cols = _mm("ada_fwd", c_rows, local["w_ada"], "nn", F32, a_fn=_silu, epi=lambda acc, b: acc + b,
                   epi_in=(jnp.broadcast_to(b_cols, (all_rows + pad_rows, ada_cols)),))
    (mod_s,) = _exchange("scatter_mod", [(mod_cols[:all_rows].reshape(N_DEV, n_b, ada_cols), False)])
    mod = mod_s.transpose(1, 0, 2).reshape(n_b, ADA_CHUNKS, 1, d)
    shift1, scale1, gate1, shift2, scale2, gate2 = [mod[:, i] for i in range(ADA_CHUNKS)]

    h, zgate, zglu, zsm = _in_proj_fwd(x2, scale1, shift1, vec["norm1_g"], [w_gate_t, w_glu_t, w_sm_t],
                                       [BF16, BF16, F32], seq)
    q, k, v, kt = _mla_prep_fwd(zsm, wuq, wukv, vec["q_latent_g"], vec["kv_latent_g"], gqn, gkn, rope, n_b, seq)
    attn, lse, (w_o_s, w_pw_s, w_out_s, w_ff1_s, w_ff2_s) = _attn_fwd(
        q, k, v, [(tsh("w_o_mla"), True), (tsh("w_pw_out"), True), (bf("w_out"), True), (tsh("w_ff1"), True),
                  (bf("w_ff2"), True)], n_b, seq)
    w_o_t = w_o_s.reshape(d, -1)
    w_pw_t = w_pw_s.reshape(d, -1)
    w_out_f = w_out_s.reshape(d, d)
    w_ff1_t = w_ff1_s.reshape(-1, d)
    w_ff2_f = w_ff2_s.reshape(-1, d)
    attn2 = attn.reshape(n_rows, N_HEADS * V_HEAD_DIM)
    u3, u1 = _conv_fwd(zglu.reshape(n_b, seq, 2 * CONV_CH), conv_w_f, vec["conv_b"], vec["conv_ln_g"], vec["conv_ln_b"], n_b, seq)
    u32 = u3.reshape(n_rows, CONV_CH)
    ya = _mm("mla_out", attn2, w_o_t, "nt", BF16)
    yb = _mm("conv_out", u32, w_pw_t, "nt", BF16)
    mmr = functools.partial(_mm_rows, n_rows=n_rows, seq=seq)

    def merge_fn(t):
        return _sigmoid(t[0]) * t[2] + _sigmoid(t[1]) * t[3]

    def mid_fn(acc, r, b, cc):
        x1_ = r[0] + b[0] * acc
        return [acc, x1_, _norm_mod(x1_, cc[0], b[1], b[2])], [], []

    mrg, mixed, x1, h2 = mmr("out_proj", [(zgate, d, 0), (zgate, d, 1), (ya, d, 0), (yb, d, 0)], merge_fn, w_out_f, "nn",
                             mid_fn, rows=[_full(x2)], bats=[gate1, scale2, shift2], consts=[vec["norm2_g"]],
                             outs=[(d, BF16), (d, F32), (d, BF16)], a_out=BF16)

    a = _mm("ff1", h2, w_ff1_t, "nt", BF16)

    def loss_fn(ff, r, b, cc):
        err = r[0] + b[0] * ff - r[1]
        dy_ = err * (1.0 / d)
        sq = jnp.broadcast_to(jnp.sum(err * err, keepdims=True), (1, LANES))
        return [dy_, b[0] * dy_], [jnp.sum(dy_ * ff, axis=0, keepdims=True)], [sq]

    dy, df, dgate2, sq_err = mmr("ff2_loss", [(a, a.shape[1], 0)], lambda t: _relu2(t[0]), w_ff2_f, "nn", loss_fn,
                                 rows=[_full(x1), _full(t2)], bats=[gate2], outs=[(d, F32), (d, BF16)], bat_outs=[d],
                                 tot_outs=[(1, LANES)], tk=a.shape[1])

    da = _mm("ff2_bwd", df, w_ff2_f, "nt", BF16, epi=lambda acc, av: acc * 2.0 * jnp.maximum(av, 0.0), epi_in=(a,))
    g_ff2 = _mm("ff2_dw", a, df, "tn", BF16, a_fn=_relu2)
    g_ff1_t = _mm("ff1_dw", da, h2, "tn", BF16)

    def mid_bwd(dh2_, r, b, cc):
        dx, dsc, dsh, dg = _norm_mod_bwd(r[0], cc[0], b[0], dh2_)
        dx1_ = r[1] + dx
        return [dx1_, b[1] * dx1_], [dsc, dsh, jnp.sum(dx1_ * r[2].astype(F32), axis=0, keepdims=True)], [dg]

    dx1, dmixed, dscale2, dshift2, dgate1, g_norm2 = mmr(
        "ff1_bwd", [(da, da.shape[1], 0)], None, w_ff1_t, "nn", mid_bwd, rows=[_full(x1), _full(dy), _full(mixed)],
        bats=[scale2, gate1], consts=[vec["norm2_g"]], outs=[(d, F32), (d, BF16)], bat_outs=[d, d, d],
        tot_outs=[(1, d)], tk=da.shape[1])

    g_out = _mm("out_proj_dw", mrg, dmixed, "tn", BF16)

    def merge_bwd(dm, r, b, cc):
        ya_, yb_ = r[2].astype(F32), r[3].astype(F32)
        sa, sb = _sigmoid(r[0].astype(F32)), _sigmoid(r[1].astype(F32))
        return [dm * ya_ * sa * (1.0 - sa), dm * yb_ * sb * (1.0 - sb), dm * sa, dm * sb], [], []

    dzga, dzgb, dya, dyb = mmr("out_proj_bwd", [(dmixed, d, 0)], None, w_out_f, "nt", merge_bwd,
                               rows=[(zgate, d, 0), (zgate, d, 1), _full(ya), _full(yb)], outs=[(d, BF16)] * 4)
    dattn = _mm("mla_out_bwd", dya, w_o_t, "nn", BF16)
    g_o_t = _mm("mla_out_dw", dya, attn2, "tn", BF16)
    du3 = _mm("conv_out_bwd", dyb, w_pw_t, "nn", BF16)
    g_pw_t = _mm("conv_out_dw", dyb, u32, "tn", BF16)

    dzglu, g_conv_w, g_conv_b, g_ln_g, g_ln_b = _conv_bwd(
        zglu.reshape(n_b, seq, 2 * CONV_CH), u1, du3.reshape(n_b, seq, CONV_CH), conv_w_f, vec["conv_ln_g"],
        vec["conv_ln_b"], n_b, seq)
    dzglu = dzglu.reshape(n_rows, 2 * CONV_CH)

    dq, dk, dv, (p_ff2, p_ff1, p_out, p_pw, p_o) = _attn_bwd(
        q, k, v, kt, dattn.reshape(n_b, seq, N_HEADS * V_HEAD_DIM), attn, lse,
        [(g_ff2.reshape(N_DEV, -1, d), False), (g_ff1_t.reshape(N_DEV, -1, d), False), (g_out.reshape(N_DEV, -1, d), False),
         (g_pw_t.reshape(N_DEV, -1, CONV_CH), False), (g_o_t.reshape(N_DEV, -1, N_HEADS * V_HEAD_DIM), False)], n_b, seq)
    dzsm, g_wuq, g_wukv, g_gq, g_gkv, g_gqn, g_gkn = _mla_prep_bwd(
        zsm, dq, dk, dv, wuq, wukv, vec["q_latent_g"], vec["kv_latent_g"], gqn, gkn, rope, n_b, seq)

    g_gate_a_t = _mm("in_proj_gate_dw_a", dzga, h, "tn", BF16)
    g_gate_b_t = _mm("in_proj_gate_dw_b", dzgb, h, "tn", BF16)
    g_glu_t = _mm("in_proj_glu_dw", dzglu, h, "tn", BF16)
    g_sm_t = _mm("in_proj_sm_dw", dzsm, h, "tn", BF16)
    g_in_t = jnp.concatenate([g_sm_t[:OFF_KV + rot_half], g_sm_t[OFF_KV + HEAD_PAD // 2:OFF_KV + HEAD_PAD // 2 + rot_half],
                              g_glu_t, g_gate_a_t, g_gate_b_t], axis=0)
    g_uq = _head_dims(g_wuq.reshape(Q_LORA, N_HEADS, HEAD_PAD)).reshape(Q_LORA, N_HEADS * QK_HEAD_DIM)
    g_wukv = g_wukv.reshape(KV_LORA, 2, N_HEADS, HEAD_PAD)
    g_v = jnp.where(odd, g_wukv[:, 1, :, V_HEAD_DIM:], g_wukv[:, 1, :, :V_HEAD_DIM])
    g_ukv = jnp.concatenate([_head_dims(g_wukv[:, 0])[:, :, :QK_NOPE_DIM], g_v], axis=2).reshape(KV_LORA, -1)

    grad_x, dscale1, dshift1, g_norm1, (p_in, p_uq, p_ukv, p_conv_w) = _in_proj_bwd(
        [(dzga, w_gate_t[:d]), (dzgb, w_gate_t[d:]), (dzglu, w_glu_t), (dzsm, w_sm_t)], x2, dx1, scale1, vec["norm1_g"],
        [(g_in_t.reshape(N_DEV, -1, d), False), (_stack_cols(g_uq, BF16), False), (_stack_cols(g_ukv, BF16), False),
         (_stack_cols(g_conv_w[:CONV_WIDTH], F32), False)], seq)

    dmod = jnp.concatenate([dshift1, dscale1, dgate1, dshift2, dscale2, dgate2], axis=1).reshape(n_b, N_DEV, ada_cols)
    (dmod_s,) = _exchange("scatter_dmod", [(dmod.transpose(1, 0, 2), False)])
    dmod_rows = jnp.pad(dmod_s.reshape(all_rows, ada_cols), ((0, pad_rows), (0, 0)))
    g_ada = _mm("ada_dw", c_rows, dmod_rows, "tn", F32, a_fn=_silu)
    (g_b_cols,) = _rowwise("ada_db", lambda r, b, cc: ([], [], [jnp.sum(r[0], axis=0, keepdims=True)]),
                           all_rows + pad_rows, all_rows + pad_rows, rows=[_full(dmod_rows)], tot_outs=[(1, ada_cols)])

    partial_of = {"norm1_g": g_norm1, "q_latent_g": g_gq, "kv_latent_g": g_gkv, "qk_norm_q_g": _head_dims(g_gqn),
                  "qk_norm_k_g": _head_dims(g_gkn), "conv_b": g_conv_b, "conv_ln_g": g_ln_g, "conv_ln_b": g_ln_b, "norm2_g": g_norm2}
    names = [n for n in REPLICATED if n != "b_ada"]
    pieces = [_pad_lanes(partial_of[n], -(-partial_of[n].shape[1] // LANES) * LANES) for n in names] + [g_b_cols, sq_err]
    widths = [p.shape[1] for p in pieces]
    small = jnp.concatenate(pieces, axis=1)
    small = _pad_lanes(small, -(-small.shape[1] // (8 * LANES)) * 8 * LANES).reshape(-1, LANES)
    (small_s,) = _exchange("gather_small_grads", [(small, True)])
    small_s = small_s.reshape(N_DEV, 1, -1)
    parts = {}
    off = 0
    for n, wd in zip(names, widths):
        parts[n] = small_s[:, :, off:off + vec[n].shape[1]]
        off += wd
    parts["b_ada"] = small_s[:, 0, off:off + ada_cols].reshape(1, 1, N_DEV * ada_cols)
    loss = jnp.sum(small_s[:, 0, off + ada_cols]) * (0.5 / d)
    g_in_mine = _sum_parts("sum_w_in", p_in).T
    parts.update({"w_ada": g_ada[None], "w_in": g_in_mine[None], "w_uq": p_uq, "w_ukv": p_ukv, "w_o_mla": p_o,
                  "conv_w": p_conv_w, "w_pw_out": p_pw, "w_out": p_out, "w_ff1": p_ff1, "w_ff2": p_ff2})
    transposed = ("w_o_mla", "w_pw_out", "w_ff1")

    grad_out, delta_out, m_out, v_out = [], [], [], []
    for n in WEIGHTS:
        shape2 = local[n].shape if local[n].ndim == 2 else (1, local[n].shape[0])
        g_w, d_w, n_m, n_v = _adamw("adamw_" + n, local[n].reshape(shape2), parts[n], given["m_" + n].reshape(shape2),
                                    given["v_" + n].reshape(shape2), transposed=n in transposed)
        full_shape = given[n].shape
        grad_out.append(g_w.reshape(full_shape))
        delta_out.append(d_w.reshape(full_shape))
        m_out.append(n_m.reshape(full_shape))
        v_out.append(n_v.reshape(full_shape))
    return (loss, grad_x.reshape(n_b, seq, d), *grad_out, *delta_out, *m_out, *v_out)
```

```python
import functools

import jax
import jax.numpy as jnp
from jax import lax
from jax.experimental import pallas as pl
from jax.experimental.pallas import tpu as pltpu

F32 = jnp.float32
BF16 = jnp.bfloat16

N_DEV = 8
EPS = 1e-6
N_HEADS = 8
QK_HEAD_DIM = 96
QK_NOPE_DIM = 64
QK_ROPE_DIM = 32
V_HEAD_DIM = 64
HEAD_PAD = 128
Q_LORA = 256
KV_LORA = 128
CONV_CH = 512
CONV_WIDTH = 31
CONV_HALO = 32
CONV_TAIL = 8
CHUNK = 64
ROPE_THETA = 10000.0
OFF_Q = Q_LORA
OFF_KV = OFF_Q + KV_LORA
OFF_KR = OFF_KV + QK_ROPE_DIM
OFF_GLU = OFF_KR + 2 * CONV_CH
ADA_CHUNKS = 6
ADAM_LR = 0.001
ADAM_B1 = 0.9
ADAM_B2 = 0.999
ADAM_EPS = 1e-08
ADAM_WD = 0.01
ADAM_STEP = 10
LANES = 128
VMEM_LIMIT = 56 * 1024 * 1024
NEG_BIG = -1e30
ATT_HEADS = 4
ATT_TILE = 512
PREP_TILE = 512
ATT_SCALE = QK_HEAD_DIM ** -0.5
LOG2E = 1.4426950408889634
LN2 = 0.6931471805599453
QK_SCALE = ATT_SCALE * LOG2E
ADAM_ROWS = 256
ROWS_PAD = 16

REPLICATED = ("b_ada", "norm1_g", "q_latent_g", "kv_latent_g", "qk_norm_q_g", "qk_norm_k_g", "conv_b", "conv_ln_g",
              "conv_ln_b", "norm2_g")
WEIGHTS = ("w_ada", "b_ada", "norm1_g", "w_in", "q_latent_g", "w_uq", "kv_latent_g", "w_ukv", "qk_norm_q_g",
           "qk_norm_k_g", "w_o_mla", "conv_w", "conv_b", "conv_ln_g", "conv_ln_b", "w_pw_out", "w_out", "norm2_g",
           "w_ff1", "w_ff2")


def _tile(dim, pref):
    if dim <= pref:
        return dim
    t = (pref // LANES) * LANES
    while dim % t:
        t -= LANES
    return t


def _params(semantics):
    return pltpu.CompilerParams(dimension_semantics=semantics, vmem_limit_bytes=VMEM_LIMIT)


def _sigmoid(v):
    return 1.0 / (1.0 + jnp.exp(-v))


def _silu(v):
    return v * _sigmoid(v)


def _relu2(v):
    return jnp.square(jnp.maximum(v, 0.0))


_DIMS = {"nn": (((1,), (0,)), ((), ())), "nt": (((1,), (1,)), ((), ())), "tn": (((0,), (0,)), ((), ()))}


def _mm(name, a, b, mode, out_dtype, *, a_fn=None, epi=None, epi_in=(), tm=1024, tn=1024, tk=1024):
    if mode == "nn":
        (m, k), n = a.shape, b.shape[1]
    elif mode == "nt":
        (m, k), n = a.shape, b.shape[0]
    else:
        (k, m), n = a.shape, b.shape[1]
    tm, tn, tk = _tile(m, tm), _tile(n, tn), _tile(k, tk)
    nk = k // tk
    a_spec = (pl.BlockSpec((tk, tm), lambda i, j, kk: (kk, i)) if mode == "tn"
              else pl.BlockSpec((tm, tk), lambda i, j, kk: (i, kk)))
    b_spec = (pl.BlockSpec((tn, tk), lambda i, j, kk: (j, kk)) if mode == "nt"
              else pl.BlockSpec((tk, tn), lambda i, j, kk: (kk, j)))
    o_spec = e_spec = pl.BlockSpec((tm, tn), lambda i, j, kk: (i, j))
    out_shape = jax.ShapeDtypeStruct((m, n), out_dtype)
    n_epi = len(epi_in)

    def body(a_ref, b_ref, *rest):
        epi_refs, o_ref, acc_ref = rest[:n_epi], rest[n_epi], rest[n_epi + 1]
        kk = pl.program_id(2)

        @pl.when(kk == 0)
        def _():
            acc_ref[...] = jnp.zeros_like(acc_ref)

        av = a_ref[...]
        if a_fn is not None:
            av = a_fn(av.astype(F32))
        acc_ref[...] += lax.dot_general(av.astype(BF16), b_ref[...].astype(BF16), _DIMS[mode],
                                        preferred_element_type=F32)

        @pl.when(kk == nk - 1)
        def _():
            acc = acc_ref[...]
            if epi is not None:
                acc = epi(acc, *[r[...].astype(F32) for r in epi_refs])
            o_ref[...] = acc.astype(out_dtype)

    return pl.pallas_call(
        body, name=name, grid=(m // tm, n // tn, nk),
        in_specs=[a_spec, b_spec] + [e_spec] * n_epi, out_specs=o_spec, out_shape=out_shape,
        scratch_shapes=[pltpu.VMEM((tm, tn), F32)],
        compiler_params=_params(("parallel", "parallel", "arbitrary")),
    )(a, b, *epi_in)


def _rowwise(name, fn, n_rows, seq, rows, bats=(), consts=(), outs=(), bat_outs=(), tot_outs=(), tm=256):
    tm = min(tm, seq)
    per_seq = seq // tm
    n_b = n_rows // seq
    nr, nb, nc, no, nbo, nto = len(rows), len(bats), len(consts), len(outs), len(bat_outs), len(tot_outs)

    def body(*refs):
        i = pl.program_id(0)
        r_in = [r[...] for r in refs[:nr]]
        b_in = [r[0] for r in refs[nr:nr + nb]]
        c_in = [r[...] for r in refs[nr + nb:nr + nb + nc]]
        o_refs = refs[nr + nb + nc:nr + nb + nc + no]
        bo_refs = refs[nr + nb + nc + no:nr + nb + nc + no + nbo]
        to_refs = refs[nr + nb + nc + no + nbo:]
        o_val, bo_val, to_val = fn(r_in, b_in, c_in)
        for r, v in zip(o_refs, o_val):
            r[...] = v.astype(r.dtype)
        if nbo:
            @pl.when(i % per_seq == 0)
            def _():
                for r in bo_refs:
                    r[...] = jnp.zeros_like(r)

            for r, v in zip(bo_refs, bo_val):
                r[0] += v
        if nto:
            @pl.when(i == 0)
            def _():
                for r in to_refs:
                    r[...] = jnp.zeros_like(r)

            for r, v in zip(to_refs, to_val):
                r[...] += v

    in_specs = [pl.BlockSpec((tm, w), functools.partial(lambda cb, i: (i, cb), cb)) for (_, w, cb) in rows]
    in_specs += [pl.BlockSpec((1, 1, bt.shape[2]), lambda i: (i // per_seq, 0, 0)) for bt in bats]
    in_specs += [pl.BlockSpec(ct.shape, lambda i: (0, 0)) for ct in consts]
    out_specs = [pl.BlockSpec((tm, w), lambda i: (i, 0)) for (w, _) in outs]
    out_specs += [pl.BlockSpec((1, 1, w), lambda i: (i // per_seq, 0, 0)) for w in bat_outs]
    out_specs += [pl.BlockSpec(shp, lambda i: (0, 0)) for shp in tot_outs]
    out_shape = [jax.ShapeDtypeStruct((n_rows, w), dt) for (w, dt) in outs]
    out_shape += [jax.ShapeDtypeStruct((n_b, 1, w), F32) for w in bat_outs]
    out_shape += [jax.ShapeDtypeStruct(shp, F32) for shp in tot_outs]
    res = pl.pallas_call(
        body, name=name, grid=(n_rows // tm,), in_specs=in_specs, out_specs=out_specs, out_shape=out_shape,
        compiler_params=_params(("arbitrary",)),
    )(*[r[0] for r in rows], *bats, *consts)
    return res


def _full(arr):
    return (arr, arr.shape[1], 0)


def _mm_rows(name, a_rows, a_fn, w, mode, fn, n_rows, seq, rows=(), bats=(), consts=(), outs=(), bat_outs=(),
             tot_outs=(), a_out=None, tm=512, tk=1024):
    tm = min(tm, seq)
    per_seq = seq // tm
    n_b = n_rows // seq
    k = a_rows[0][1]
    if mode == "nt":
        n_out, tk = w.shape[0], _tile(k, tk)
        w_spec = pl.BlockSpec((n_out, tk), lambda i, kk: (0, kk))
    else:
        n_out, tk = w.shape[1], _tile(k, tk)
        w_spec = pl.BlockSpec((tk, n_out), lambda i, kk: (kk, 0))
    nk = k // tk
    na, nr, nb, nc = len(a_rows), len(rows), len(bats), len(consts)
    n_extra = 0 if a_out is None else 1
    no, nbo, nto = len(outs), len(bat_outs), len(tot_outs)

    def body(*refs):
        i, kk = pl.program_id(0), pl.program_id(1)
        a_refs, w_ref = refs[:na], refs[na]
        pos = na + 1
        r_refs, b_refs, c_refs = refs[pos:pos + nr], refs[pos + nr:pos + nr + nb], refs[pos + nr + nb:pos + nr + nb + nc]
        pos += nr + nb + nc
        ao_refs = refs[pos:pos + n_extra]
        pos += n_extra
        o_refs, bo_refs, to_refs = refs[pos:pos + no], refs[pos + no:pos + no + nbo], refs[pos + no + nbo:pos + no + nbo + nto]
        acc_ref = refs[pos + no + nbo + nto]

        @pl.when(kk == 0)
        def _():
            acc_ref[...] = jnp.zeros_like(acc_ref)

        tiles = [r[...] for r in a_refs]
        av = a_fn([t.astype(F32) for t in tiles]) if a_fn is not None else tiles[0]
        av = av.astype(BF16)
        if n_extra:
            ao_refs[0][...] = av.astype(ao_refs[0].dtype)
        acc_ref[...] += lax.dot_general(av, w_ref[...].astype(BF16), _DIMS[mode], preferred_element_type=F32)

        @pl.when(kk == nk - 1)
        def _():
            o_val, bo_val, to_val = fn(acc_ref[...], [r[...] for r in r_refs], [r[0] for r in b_refs],
                                       [r[...] for r in c_refs])
            for r, v in zip(o_refs, o_val):
                r[...] = v.astype(r.dtype)
            if nbo:
                @pl.when(i % per_seq == 0)
                def _():
                    for r in bo_refs:
                        r[...] = jnp.zeros_like(r)

                for r, v in zip(bo_refs, bo_val):
                    r[0] += v
            if nto:
                @pl.when(i == 0)
                def _():
                    for r in to_refs:
                        r[...] = jnp.zeros_like(r)

                for r, v in zip(to_refs, to_val):
                    r[...] += v

    in_specs = [pl.BlockSpec((tm, tk), functools.partial(lambda cb, i, kk: (i, kk + cb), cb)) for (_, _, cb) in a_rows]
    in_specs += [w_spec]
    in_specs += [pl.BlockSpec((tm, wd), functools.partial(lambda cb, i, kk: (i, cb), cb)) for (_, wd, cb) in rows]
    in_specs += [pl.BlockSpec((1, 1, bt.shape[2]), lambda i, kk: (i // per_seq, 0, 0)) for bt in bats]
    in_specs += [pl.BlockSpec(ct.shape, lambda i, kk: (0, 0)) for ct in consts]
    out_specs = [pl.BlockSpec((tm, tk), lambda i, kk: (i, kk))] * n_extra
    out_specs += [pl.BlockSpec((tm, wd), lambda i, kk: (i, 0)) for (wd, _) in outs]
    out_specs += [pl.BlockSpec((1, 1, wd), lambda i, kk: (i // per_seq, 0, 0)) for wd in bat_outs]
    out_specs += [pl.BlockSpec(shp, lambda i, kk: (0, 0)) for shp in tot_outs]
    out_shape = [jax.ShapeDtypeStruct((n_rows, k), a_out)] if n_extra else []
    out_shape += [jax.ShapeDtypeStruct((n_rows, wd), dt) for (wd, dt) in outs]
    out_shape += [jax.ShapeDtypeStruct((n_b, 1, wd), F32) for wd in bat_outs]
    out_shape += [jax.ShapeDtypeStruct(shp, F32) for shp in tot_outs]
    return pl.pallas_call(
        body, name=name, grid=(n_rows // tm, nk), in_specs=in_specs, out_specs=out_specs, out_shape=out_shape,
        scratch_shapes=[pltpu.VMEM((tm, n_out), F32)],
        compiler_params=_params(("arbitrary", "arbitrary")),
    )(*[a for a, _, _ in a_rows], w, *[r[0] for r in rows], *bats, *consts)


def _norm_mod(x, g, scale, shift):
    r = lax.rsqrt(jnp.mean(x * x, axis=-1, keepdims=True) + EPS)
    xh = x * r
    return xh * g * (1.0 + scale) + shift


def _norm_mod_bwd(x, g, scale, dh):
    r = lax.rsqrt(jnp.mean(x * x, axis=-1, keepdims=True) + EPS)
    xh = x * r
    dn = dh * (1.0 + scale)
    dxh = dn * g
    dx = r * (dxh - xh * jnp.mean(dxh * xh, axis=-1, keepdims=True))
    dscale = jnp.sum(dh * xh * g, axis=0, keepdims=True)
    dshift = jnp.sum(dh, axis=0, keepdims=True)
    dg = jnp.sum(dn * xh, axis=0, keepdims=True)
    return dx, dscale, dshift, dg


def _rms(v, g):
    r = lax.rsqrt(jnp.mean(v * v, axis=-1, keepdims=True) + EPS)
    return v * r * g


def _rms_bwd(v, g, dy):
    r = lax.rsqrt(jnp.mean(v * v, axis=-1, keepdims=True) + EPS)
    vh = v * r
    dvh = dy * g
    dv = r * (dvh - vh * jnp.mean(dvh * vh, axis=-1, keepdims=True))
    return dv, jnp.sum(dy * vh, axis=0, keepdims=True)


def _lane_sum(t):
    return jnp.dot(t.astype(BF16), jnp.ones((HEAD_PAD, HEAD_PAD), BF16), preferred_element_type=F32)


def _head_norm(v, g):
    r = lax.rsqrt(_lane_sum(v * v) * (1.0 / QK_HEAD_DIM) + EPS)
    return v * r * g


def _head_norm_bwd(v, g, dy):
    r = lax.rsqrt(_lane_sum(v * v) * (1.0 / QK_HEAD_DIM) + EPS)
    vh = v * r
    dvh = dy * g
    dv = r * (dvh - vh * (_lane_sum(dvh * vh) * (1.0 / QK_HEAD_DIM)))
    return dv, jnp.sum(dy * vh, axis=0, keepdims=True)


def _rope(v, cos, sin):
    return v * cos + pltpu.roll(v, HEAD_PAD // 2, 1) * sin


def _rope_bwd(g, cos, sin):
    return g * cos + pltpu.roll(g * sin, HEAD_PAD // 2, 1)


def _mla_prep_fwd(zsm, wuq, wukv, gq, gkv, gqn, gkn, rope, n_b, seq):
    n_rows = n_b * seq
    tm = min(PREP_TILE, seq)
    per_seq = seq // tm
    att_tile = min(ATT_TILE, seq)
    k_cols = N_HEADS * HEAD_PAD

    def body(z_ref, wuq_ref, wukv_ref, gq_ref, gkv_ref, gqn_ref, gkn_ref, c_ref, s_ref, q_ref, k_ref, v_ref, kt_ref):
        z = z_ref[...]
        qn = _rms(z[:, :Q_LORA], gq_ref[...]).astype(BF16)
        kvn = _rms(z[:, Q_LORA:Q_LORA + KV_LORA], gkv_ref[...]).astype(BF16)
        krp = z[:, Q_LORA + KV_LORA:]
        cos, sin = c_ref[...], s_ref[...]
        q_all = jnp.dot(qn, wuq_ref[...], preferred_element_type=F32)
        kv_all = jnp.dot(kvn, wukv_ref[...], preferred_element_type=F32)
        for h in range(N_HEADS):
            cols = slice(h * HEAD_PAD, (h + 1) * HEAD_PAD)
            q_ref[0, h] = (_rope(_head_norm(q_all[:, cols], gqn_ref[...]), cos, sin) * QK_SCALE).astype(BF16)
            kh = _rope(_head_norm(kv_all[:, cols] + krp, gkn_ref[...]), cos, sin)
            k_ref[0, h] = kh.astype(BF16)
            for part in range(tm // att_tile):
                kt_ref[0, h, part] = kh[part * att_tile:(part + 1) * att_tile].T.astype(BF16)
            v_ref[0, h] = kv_all[:, k_cols + h * HEAD_PAD:k_cols + (h + 1) * HEAD_PAD].astype(BF16)

    whole2 = lambda arr: pl.BlockSpec(arr.shape, lambda i: (0, 0))
    rope_spec = pl.BlockSpec((tm, HEAD_PAD), lambda i: (i % per_seq, 0))
    head_spec = pl.BlockSpec((1, N_HEADS, tm, HEAD_PAD), lambda i: (i // per_seq, 0, i % per_seq, 0))
    head_shape = jax.ShapeDtypeStruct((n_b, N_HEADS, seq, HEAD_PAD), BF16)
    t_spec = pl.BlockSpec((1, N_HEADS, tm // att_tile, HEAD_PAD, att_tile), lambda i: (i // per_seq, 0, i % per_seq, 0, 0))
    t_shape = jax.ShapeDtypeStruct((n_b, N_HEADS, seq // att_tile, HEAD_PAD, att_tile), BF16)
    return pl.pallas_call(
        body, name="mla_prep_fwd", grid=(n_rows // tm,),
        in_specs=[pl.BlockSpec((tm, 512), lambda i: (i, 0)), whole2(wuq), whole2(wukv),
                  whole2(gq), whole2(gkv), whole2(gqn), whole2(gkn), rope_spec, rope_spec],
        out_specs=[head_spec] * 3 + [t_spec], out_shape=[head_shape] * 3 + [t_shape],
        compiler_params=_params(("parallel",)),
    )(zsm, wuq, wukv, gq, gkv, gqn, gkn, *rope)


def _mla_prep_bwd(zsm, dq, dk, dv, wuq, wukv, gq, gkv, gqn, gkn, rope, n_b, seq):
    n_rows = n_b * seq
    tm = min(PREP_TILE, seq)
    per_seq = seq // tm
    tn_dims = _DIMS["tn"]
    nt_dims = _DIMS["nt"]
    k_cols = N_HEADS * HEAD_PAD

    def body(z_ref, dq_ref, dk_ref, dv_ref, wuq_ref, wukv_ref, gq_ref, gkv_ref, gqn_ref, gkn_ref,
             c_ref, s_ref, dz_ref, dwuq_ref, dwukv_ref, dgq_ref, dgkv_ref, dgqn_ref, dgkn_ref):
        @pl.when(pl.program_id(0) == 0)
        def _():
            for r in (dwuq_ref, dwukv_ref, dgq_ref, dgkv_ref, dgqn_ref, dgkn_ref):
                r[...] = jnp.zeros_like(r)

        z = z_ref[...]
        zq, zkv, krp = z[:, :Q_LORA], z[:, Q_LORA:Q_LORA + KV_LORA], z[:, Q_LORA + KV_LORA:]
        qn = _rms(zq, gq_ref[...]).astype(BF16)
        kvn = _rms(zkv, gkv_ref[...]).astype(BF16)
        cos, sin = c_ref[...], s_ref[...]
        lane = lax.broadcasted_iota(jnp.int32, (tm, HEAD_PAD), 1)
        rope_lanes = (lane % (HEAD_PAD // 2)) < QK_ROPE_DIM // 2
        q_all = jnp.dot(qn, wuq_ref[...], preferred_element_type=F32)
        k_all = jnp.dot(kvn, wukv_ref[:, :k_cols], preferred_element_type=F32)
        dkrp = jnp.zeros((tm, HEAD_PAD), F32)
        dgqn = jnp.zeros((1, HEAD_PAD), F32)
        dgkn = jnp.zeros((1, HEAD_PAD), F32)
        dq_heads, dk_heads = [], []
        for h in range(N_HEADS):
            cols = slice(h * HEAD_PAD, (h + 1) * HEAD_PAD)
            dqh, dg = _head_norm_bwd(q_all[:, cols], gqn_ref[...],
                                     _rope_bwd(dq_ref[0, h].astype(F32) * ATT_SCALE, cos, sin))
            dgqn += dg
            dq_heads.append(dqh.astype(BF16))
            dkh, dg = _head_norm_bwd(k_all[:, cols] + krp, gkn_ref[...], _rope_bwd(dk_ref[0, h].astype(F32), cos, sin))
            dgkn += dg
            dkrp += jnp.where(rope_lanes, dkh, 0.0)
            dk_heads.append(dkh.astype(BF16))
        dq_all = jnp.concatenate(dq_heads, axis=1)
        dkv_all = jnp.concatenate(dk_heads + [dv_ref[0, h] for h in range(N_HEADS)], axis=1)
        dwuq_ref[...] += lax.dot_general(qn, dq_all, tn_dims, preferred_element_type=F32)
        dqn = lax.dot_general(dq_all, wuq_ref[...], nt_dims, preferred_element_type=F32)
        dwukv_ref[...] += lax.dot_general(kvn, dkv_all, tn_dims, preferred_element_type=F32)
        dkvn = lax.dot_general(dkv_all, wukv_ref[...], nt_dims, preferred_element_type=F32)
        dzq, dg = _rms_bwd(zq, gq_ref[...], dqn)
        dgq_ref[...] += dg
        dzkv, dg = _rms_bwd(zkv, gkv_ref[...], dkvn)
        dgkv_ref[...] += dg
        dgqn_ref[...] += dgqn
        dgkn_ref[...] += dgkn
        dz_ref[:, :Q_LORA] = dzq.astype(dz_ref.dtype)
        dz_ref[:, Q_LORA:Q_LORA + KV_LORA] = dzkv.astype(dz_ref.dtype)
        dz_ref[:, Q_LORA + KV_LORA:] = dkrp.astype(dz_ref.dtype)

    whole2 = lambda arr: pl.BlockSpec(arr.shape, lambda i: (0, 0))
    rope_spec = pl.BlockSpec((tm, HEAD_PAD), lambda i: (i % per_seq, 0))
    head_spec = pl.BlockSpec((1, N_HEADS, tm, HEAD_PAD), lambda i: (i // per_seq, 0, i % per_seq, 0))
    row_spec = pl.BlockSpec((tm, 512), lambda i: (i, 0))
    return pl.pallas_call(
        body, name="mla_prep_bwd", grid=(n_rows // tm,),
        in_specs=[row_spec, head_spec, head_spec, head_spec, whole2(wuq), whole2(wukv),
                  whole2(gq), whole2(gkv), whole2(gqn), whole2(gkn), rope_spec, rope_spec],
        out_specs=[row_spec, whole2(wuq), whole2(wukv), whole2(gq), whole2(gkv), whole2(gqn), whole2(gkn)],
        out_shape=[jax.ShapeDtypeStruct((n_rows, 512), BF16),
                   jax.ShapeDtypeStruct(wuq.shape, F32), jax.ShapeDtypeStruct(wukv.shape, F32),
                   jax.ShapeDtypeStruct(gq.shape, F32), jax.ShapeDtypeStruct(gkv.shape, F32),
                   jax.ShapeDtypeStruct(gqn.shape, F32), jax.ShapeDtypeStruct(gkn.shape, F32)],
        compiler_params=_params(("arbitrary",)),
    )(zsm, dq, dk, dv, wuq, wukv, gq, gkv, gqn, gkn, *rope)


HBM_SPEC = pl.BlockSpec(memory_space=pltpu.HBM)


def _xchg_out_shapes(bufs):
    return [jax.ShapeDtypeStruct((N_DEV,) + (a.shape if gather else a.shape[1:]), a.dtype) for a, gather in bufs]


def _xchg_scratch(n_buf):
    return [pltpu.SemaphoreType.DMA((n_buf * (N_DEV - 1),)), pltpu.SemaphoreType.DMA((n_buf * (N_DEV - 1),)),
            pltpu.SemaphoreType.DMA((n_buf,))]


def _xchg_copies(src_refs, dst_refs, gathers, send_sems, recv_sems, local_sems):
    x, y, c = lax.axis_index("x"), lax.axis_index("y"), lax.axis_index("c")
    me = 4 * x + 2 * y + c
    local, starts, arrivals = [], [], []
    for bi, (src, dst, gather) in enumerate(zip(src_refs, dst_refs, gathers)):
        local.append(pltpu.make_async_copy(src if gather else src.at[me], dst.at[me], local_sems.at[bi]))
        for kk in range(1, N_DEV):
            px = 1 - x if kk & 4 else x
            py = 1 - y if kk & 2 else y
            pc = 1 - c if kk & 1 else c
            pid = 4 * px + 2 * py + pc
            sem = bi * (N_DEV - 1) + kk - 1
            starts.append(pltpu.make_async_remote_copy(
                src_ref=src if gather else src.at[pid], dst_ref=dst.at[me],
                send_sem=send_sems.at[sem], recv_sem=recv_sems.at[sem],
                device_id=(px, py, pc), device_id_type=pl.DeviceIdType.MESH))
            arrivals.append(pltpu.make_async_remote_copy(
                src_ref=src if gather else src.at[me], dst_ref=dst.at[pid],
                send_sem=send_sems.at[sem], recv_sem=recv_sems.at[sem],
                device_id=(px, py, pc), device_id_type=pl.DeviceIdType.MESH))
    return local, starts, arrivals


def _xchg_start(copies):
    local, sends, _ = copies
    for cp in local + sends:
        cp.start()


def _xchg_finish(copies):
    local, sends, arrivals = copies
    for cp in arrivals:
        cp.wait_recv()
    for cp in sends:
        cp.wait_send()
    for cp in local:
        cp.wait()


def _gather_by_chip(src_refs, dst_refs, send_sems, recv_sems, local_sems, start=True, finish=True):
    x, y, c = lax.axis_index("x"), lax.axis_index("y"), lax.axis_index("c")
    me = 4 * x + 2 * y + c
    sibling = (x, y, 1 - c)

    def place(kk):
        px = 1 - x if kk & 4 else x
        py = 1 - y if kk & 2 else y
        pc = 1 - c if kk & 1 else c
        return (px, py, pc), 4 * px + 2 * py + pc

    def copy(bi, kk, src, dst, to):
        sem = bi * (N_DEV - 1) + kk - 1
        return pltpu.make_async_remote_copy(src_ref=src, dst_ref=dst, send_sem=send_sems.at[sem],
                                            recv_sem=recv_sems.at[sem], device_id=to, device_id_type=pl.DeviceIdType.MESH)

    local, sends = [], []
    for bi, (src, dst) in enumerate(zip(src_refs, dst_refs)):
        local.append(pltpu.make_async_copy(src, dst.at[me], local_sems.at[bi]))
        sends += [copy(bi, kk, src, dst.at[me], place(kk)[0]) for kk in (1, 2, 4, 6)]
    if start:
        for cp in local + sends:
            cp.start()
    if not finish:
        return
    for kk in (2, 4, 6):
        for bi, (src, dst) in enumerate(zip(src_refs, dst_refs)):
            dev, pid = place(kk)
            copy(bi, kk, src, dst.at[pid], dev).wait_recv()
            passed = copy(bi, kk | 1, dst.at[pid], dst.at[pid], sibling)
            passed.start()
            sends.append(passed)
    for kk in (1, 3, 5, 7):
        for bi, (src, dst) in enumerate(zip(src_refs, dst_refs)):
            dev, pid = place(kk)
            copy(bi, kk, src, dst.at[pid], sibling).wait_recv()
    for cp in sends:
        cp.wait_send()
    for cp in local:
        cp.wait()


def _exchange(name, bufs, by_chip=False):
    n_buf = len(bufs)
    gathers = [g for _, g in bufs]
    assert not by_chip or all(gathers)

    def body(*refs):
        srcs, dsts = refs[:n_buf], refs[n_buf:2 * n_buf]
        if by_chip:
            _gather_by_chip(srcs, dsts, *refs[2 * n_buf:])
            return
        copies = _xchg_copies(srcs, dsts, gathers, *refs[2 * n_buf:])
        _xchg_start(copies)
        _xchg_finish(copies)

    return pl.pallas_call(
        body, name=name, out_shape=_xchg_out_shapes(bufs),
        in_specs=[HBM_SPEC] * n_buf, out_specs=[HBM_SPEC] * n_buf, scratch_shapes=_xchg_scratch(n_buf),
    )(*[a for a, _ in bufs])


def _chunk_mask(t, keys_first):
    key = lax.broadcasted_iota(jnp.int32, (t, t), 0 if keys_first else 1) // CHUNK
    query = lax.broadcasted_iota(jnp.int32, (t, t), 1 if keys_first else 0) // CHUNK
    return query >= key


def _grid_ends(grid):
    ids = [pl.program_id(ax) for ax in range(len(grid))]
    first = functools.reduce(jnp.logical_and, [i == 0 for i in ids])
    last = functools.reduce(jnp.logical_and, [i == g - 1 for i, g in zip(ids, grid)])
    return first, last


def _attn_fwd(q, k, v, bufs, n_b, seq):
    tq = min(ATT_TILE, seq)
    nq = seq // tq
    nt_dims = _DIMS["nt"]
    hpb = ATT_HEADS
    grid = (n_b, N_HEADS // hpb, nq)
    n_buf = len(bufs)
    gathers = [g for _, g in bufs]
    sum_lane = [HEAD_PAD - 1 if hh % 2 == 0 else 0 for hh in range(hpb)]

    def body(q_ref, k_ref, v_ref, *rest):
        srcs, (o_ref, lse_ref), dsts = rest[:n_buf], rest[n_buf:n_buf + 2], rest[n_buf + 2:2 * n_buf + 2]
        gather = functools.partial(_gather_by_chip, srcs, dsts, *rest[2 * n_buf + 2:])
        first, last = _grid_ends(grid)
        pl.when(first)(functools.partial(gather, start=True, finish=False))

        qi = pl.program_id(2)
        mask = _chunk_mask(tq, keys_first=False)
        lane_row = lax.broadcasted_iota(jnp.int32, (1, HEAD_PAD), 1)
        ones = [(lane_row == sum_lane[hh]).astype(BF16) for hh in range(hpb)]
        qs = [q_ref[0, hh] for hh in range(hpb)]

        def step(j, carry, masked):
            rows = pl.ds(pl.multiple_of(j * tq, tq), tq)
            out = []
            for hh in range(hpb):
                m, acc = carry[hh]
                s = lax.dot_general(qs[hh], k_ref[0, hh, rows, :], nt_dims, preferred_element_type=F32)
                if masked:
                    s = jnp.where(mask, s, NEG_BIG)
                m_new = jnp.maximum(m, jnp.max(s, axis=-1, keepdims=True))
                p = jnp.exp2(s - m_new).astype(BF16)
                acc = jnp.exp2(m - m_new) * acc + jnp.dot(p, v_ref[0, hh, rows, :] + ones[hh], preferred_element_type=F32)
                out.append((m_new, acc))
            return tuple(out)

        init = tuple((jnp.full((tq, 1), NEG_BIG, F32), jnp.zeros((tq, HEAD_PAD), F32)) for _ in range(hpb))
        online = lax.fori_loop(0, qi, functools.partial(step, masked=False), init)
        online = step(qi, online, True)
        ms, accs = [c[0] for c in online], [c[1] for c in online]
        carry = list(zip(ms, accs))
        lane = lax.broadcasted_iota(jnp.int32, (tq, HEAD_PAD), 1)
        for pair in range(hpb // 2):
            outs = []
            for hh in (2 * pair, 2 * pair + 1):
                m, acc = carry[hh]
                l = jnp.sum(jnp.where(lane == sum_lane[hh], acc, 0.0), axis=-1, keepdims=True)
                outs.append(acc * (1.0 / l))
                lse_ref[0, hh] = jnp.broadcast_to(m + jnp.log2(l), (tq, HEAD_PAD)).T[0:8, :]
            o_ref[0, :, pair * HEAD_PAD:(pair + 1) * HEAD_PAD] = jnp.where(lane < V_HEAD_DIM, outs[0], outs[1]).astype(BF16)

        pl.when(last)(functools.partial(gather, start=False, finish=True))

    kv_spec = pl.BlockSpec((1, hpb, seq, HEAD_PAD), lambda b, hb, i: (b, hb, 0, 0))
    q_spec = pl.BlockSpec((1, hpb, tq, HEAD_PAD), lambda b, hb, i: (b, hb, i, 0))
    res = pl.pallas_call(
        body, name="attn_fwd", grid=grid,
        in_specs=[q_spec, kv_spec, kv_spec] + [HBM_SPEC] * n_buf,
        out_specs=[pl.BlockSpec((1, tq, hpb * V_HEAD_DIM), lambda b, hb, i: (b, i, hb)),
                   pl.BlockSpec((1, hpb, 8, tq), lambda b, hb, i: (b, hb, 0, i))] + [HBM_SPEC] * n_buf,
        out_shape=[jax.ShapeDtypeStruct((n_b, seq, N_HEADS * V_HEAD_DIM), BF16),
                   jax.ShapeDtypeStruct((n_b, N_HEADS, 8, seq), F32)] + _xchg_out_shapes(bufs),
        scratch_shapes=_xchg_scratch(n_buf),
        compiler_params=_params(("arbitrary", "arbitrary", "arbitrary")),
    )(q, k, v, *[a for a, _ in bufs])
    return res[0], res[1], res[2:]


def _attn_bwd(q, k, v, kt, do, o, lse, bufs, n_b, seq):
    tq = min(ATT_TILE, seq)
    nq = seq // tq
    nt_dims = _DIMS["nt"]
    hpb = ATT_HEADS
    grid = (n_b, N_HEADS // hpb, nq)
    n_buf = len(bufs)
    gathers = [g for _, g in bufs]

    def body(q_ref, k_ref, v_ref, kt_ref, do_ref, o_ref, lse_ref, *rest):
        srcs, (dq_ref, dk_ref, dv_ref), dsts = rest[:n_buf], rest[n_buf:n_buf + 3], rest[n_buf + 3:2 * n_buf + 3]
        dk_acc, dv_acc = rest[2 * n_buf + 3:2 * n_buf + 5]
        copies = _xchg_copies(srcs, dsts, gathers, *rest[2 * n_buf + 5:])
        first, last = _grid_ends(grid)
        pl.when(first)(functools.partial(_xchg_start, copies))

        qi = pl.program_id(2)

        @pl.when(qi == 0)
        def _():
            dk_acc[...] = jnp.zeros_like(dk_acc)
            dv_acc[...] = jnp.zeros_like(dv_acc)

        mask = _chunk_mask(tq, keys_first=True)
        lane = lax.broadcasted_iota(jnp.int32, (tq, HEAD_PAD), 1)
        qs, dos, deltas, lses = [], [], [], []
        for hh in range(hpb):
            cols = slice((hh // 2) * HEAD_PAD, (hh // 2 + 1) * HEAD_PAD)
            do_pair = do_ref[0, :, cols]
            prod = do_pair.astype(F32) * o_ref[0, :, cols].astype(F32)
            delta = jnp.sum(jnp.where(lane // V_HEAD_DIM == hh % 2, prod, 0.0), axis=-1, keepdims=True)
            qs.append(q_ref[0, hh])
            dos.append(do_pair)
            deltas.append(jnp.broadcast_to(delta, (tq, HEAD_PAD)).T[0:1, :])
            lses.append(lse_ref[0, hh][0:1, :])

        def step(j, dqs, masked):
            rows = pl.ds(pl.multiple_of(j * tq, tq), tq)
            out = []
            for hh in range(hpb):
                s = lax.dot_general(k_ref[0, hh, rows, :], qs[hh], nt_dims, preferred_element_type=F32)
                p = jnp.exp2(s - lses[hh])
                if masked:
                    p = jnp.where(mask, p, 0.0)
                dv_acc[hh, rows, :] += jnp.dot(p.astype(BF16), dos[hh], preferred_element_type=F32)
                dp = lax.dot_general(v_ref[0, hh, rows, :], dos[hh], nt_dims, preferred_element_type=F32)
                ds = (p * (dp - deltas[hh])).astype(BF16)
                dk_acc[hh, rows, :] += jnp.dot(ds, qs[hh], preferred_element_type=F32)
                out.append(dqs[hh] + jnp.dot(kt_ref[0, hh, j], ds, preferred_element_type=F32))
            return tuple(out)

        dqs = tuple(jnp.zeros((HEAD_PAD, tq), F32) for _ in range(hpb))
        dqs = lax.fori_loop(0, qi, functools.partial(step, masked=False), dqs)
        dqs = step(qi, dqs, True)
        for hh in range(hpb):
            dq_ref[0, hh] = dqs[hh].T.astype(BF16)

        @pl.when(qi == nq - 1)
        def _():
            dk_ref[0] = (dk_acc[...] * LN2).astype(BF16)
            dv_ref[0] = dv_acc[...].astype(BF16)

        pl.when(last)(functools.partial(_xchg_finish, copies))

    full_spec = pl.BlockSpec((1, hpb, seq, HEAD_PAD), lambda b, hb, i: (b, hb, 0, 0))
    t_spec = pl.BlockSpec((1, hpb, nq, HEAD_PAD, tq), lambda b, hb, i: (b, hb, 0, 0, 0))
    q_spec = pl.BlockSpec((1, hpb, tq, HEAD_PAD), lambda b, hb, i: (b, hb, i, 0))
    o_spec = pl.BlockSpec((1, tq, hpb * V_HEAD_DIM), lambda b, hb, i: (b, i, hb))
    lse_spec = pl.BlockSpec((1, hpb, 8, tq), lambda b, hb, i: (b, hb, 0, i))
    head_shape = jax.ShapeDtypeStruct((n_b, N_HEADS, seq, HEAD_PAD), BF16)
    res = pl.pallas_call(
        body, name="attn_bwd", grid=grid,
        in_specs=[q_spec, full_spec, full_spec, t_spec, o_spec, o_spec, lse_spec] + [HBM_SPEC] * n_buf,
        out_specs=[q_spec, full_spec, full_spec] + [HBM_SPEC] * n_buf,
        out_shape=[head_shape] * 3 + _xchg_out_shapes(bufs),
        scratch_shapes=[pltpu.VMEM((hpb, seq, HEAD_PAD), F32), pltpu.VMEM((hpb, seq, HEAD_PAD), F32)]
        + _xchg_scratch(n_buf),
        compiler_params=_params(("arbitrary", "arbitrary", "arbitrary")),
    )(q, k, v, kt, do, o, lse, *[a for a, _ in bufs])
    return res[0], res[1], res[2], res[3:]


def _in_proj_fwd(x2, scale, shift, g, w_parts, z_dtypes, seq):
    n_rows, d = x2.shape
    tm = min(512, seq)
    per_seq = seq // tm
    n_part = len(w_parts)
    nt_dims = _DIMS["nt"]

    def body(x_ref, sc_ref, sh_ref, g_ref, *rest):
        w_refs, h_ref, z_refs = rest[:n_part], rest[n_part], rest[n_part + 1:]
        h = _norm_mod(x_ref[...], g_ref[...], sc_ref[0], sh_ref[0]).astype(BF16)
        h_ref[...] = h
        for w_ref, z_ref in zip(w_refs, z_refs):
            z_ref[...] = lax.dot_general(h, w_ref[...], nt_dims, preferred_element_type=F32).astype(z_ref.dtype)

    row = lambda width: pl.BlockSpec((tm, width), lambda i: (i, 0))
    bat = pl.BlockSpec((1, 1, d), lambda i: (i // per_seq, 0, 0))
    whole = lambda arr: pl.BlockSpec(arr.shape, lambda i: (0, 0))
    return pl.pallas_call(
        body, name="in_proj_fwd", grid=(n_rows // tm,),
        in_specs=[row(d), bat, bat, whole(g)] + [whole(w) for w in w_parts],
        out_specs=[row(d)] + [row(w.shape[0]) for w in w_parts],
        out_shape=[jax.ShapeDtypeStruct((n_rows, d), BF16)]
        + [jax.ShapeDtypeStruct((n_rows, w.shape[0]), dt) for w, dt in zip(w_parts, z_dtypes)],
        compiler_params=_params(("parallel",)),
    )(x2, scale, shift, g, *w_parts)
def _in_proj_bwd(parts, x2, dx1, scale, g, bufs, seq):
    n_rows, d = x2.shape
    tm = min(512, seq)
    per_seq = seq // tm
    grid = (n_rows // tm,)
    n_part, n_buf = len(parts), len(bufs)
    gathers = [gt for _, gt in bufs]

    def body(*refs):
        dz_refs, w_refs = refs[:n_part], refs[n_part:2 * n_part]
        x_ref, dx1_ref, sc_ref, g_ref = refs[2 * n_part:2 * n_part + 4]
        srcs = refs[2 * n_part + 4:2 * n_part + 4 + n_buf]
        gx_ref, dsc_ref, dsh_ref, dg_ref = refs[2 * n_part + 4 + n_buf:2 * n_part + 8 + n_buf]
        dsts = refs[2 * n_part + 8 + n_buf:2 * n_part + 8 + 2 * n_buf]
        copies = _xchg_copies(srcs, dsts, gathers, *refs[2 * n_part + 8 + 2 * n_buf:])
        first, last = _grid_ends(grid)
        pl.when(first)(functools.partial(_xchg_start, copies))

        i = pl.program_id(0)
        dh = None
        for dz_ref, w_ref in zip(dz_refs, w_refs):
            term = jnp.dot(dz_ref[...], w_ref[...], preferred_element_type=F32)
            dh = term if dh is None else dh + term
        dx, dsc, dsh, dg = _norm_mod_bwd(x_ref[...], g_ref[...], sc_ref[0], dh)
        gx_ref[...] = dx1_ref[...] + dx

        @pl.when(i % per_seq == 0)
        def _():
            dsc_ref[...] = jnp.zeros_like(dsc_ref)
            dsh_ref[...] = jnp.zeros_like(dsh_ref)

        @pl.when(i == 0)
        def _():
            dg_ref[...] = jnp.zeros_like(dg_ref)

        dsc_ref[0] += dsc
        dsh_ref[0] += dsh
        dg_ref[...] += dg
        pl.when(last)(functools.partial(_xchg_finish, copies))

    row = lambda width: pl.BlockSpec((tm, width), lambda i: (i, 0))
    bat = pl.BlockSpec((1, 1, d), lambda i: (i // per_seq, 0, 0))
    whole = lambda arr: pl.BlockSpec(arr.shape, lambda i: (0, 0))
    n_b = n_rows // seq
    res = pl.pallas_call(
        body, name="in_proj_bwd", grid=grid,
        in_specs=[row(dz.shape[1]) for dz, _ in parts] + [whole(w) for _, w in parts]
        + [row(d), row(d), bat, whole(g)] + [HBM_SPEC] * n_buf,
        out_specs=[row(d), bat, bat, whole(g)] + [HBM_SPEC] * n_buf,
        out_shape=[jax.ShapeDtypeStruct((n_rows, d), F32), jax.ShapeDtypeStruct((n_b, 1, d), F32),
                   jax.ShapeDtypeStruct((n_b, 1, d), F32), jax.ShapeDtypeStruct(g.shape, F32)] + _xchg_out_shapes(bufs),
        scratch_shapes=_xchg_scratch(n_buf),
        compiler_params=_params(("arbitrary",)),
    )(*[dz for dz, _ in parts], *[w for _, w in parts], x2, dx1, scale, g, *[a for a, _ in bufs])
    return res[0], res[1], res[2], res[3], res[4:]


def _ln_silu(u1, g, b):
    mu = jnp.mean(u1, axis=-1, keepdims=True)
    uc = u1 - mu
    r = lax.rsqrt(jnp.mean(uc * uc, axis=-1, keepdims=True) + EPS)
    y = uc * r * g + b
    return y * _sigmoid(y)


def _conv_fill_glu(z_ref, u0_ref, seq, tile):
    u0_ref[0:CONV_HALO, :] = jnp.zeros((CONV_HALO, CONV_CH), F32)
    u0_ref[CONV_HALO + seq:CONV_HALO + seq + CONV_TAIL, :] = jnp.zeros((CONV_TAIL, CONV_CH), F32)
    for t in range(seq // tile):
        zt = z_ref[0, t * tile:(t + 1) * tile, :].astype(F32)
        u0_ref[CONV_HALO + t * tile:CONV_HALO + (t + 1) * tile, :] = zt[:, :CONV_CH] * _sigmoid(zt[:, CONV_CH:])


def _conv_windows(ref, views_ref, t, tile):
    for b in range(8):
        views_ref[b] = ref[t * tile + b:t * tile + b + tile + CONV_HALO, :]


def _conv_tap(views_ref, offset, tile):
    return views_ref[offset % 8, 8 * (offset // 8):8 * (offset // 8) + tile, :]


def _conv_tile(u0_ref, views_ref, w_ref, b_ref, t, tile):
    _conv_windows(u0_ref, views_ref, t, tile)
    acc = jnp.broadcast_to(b_ref[...], (tile, CONV_CH))
    for kk in range(CONV_WIDTH):
        acc = acc + w_ref[kk:kk + 1, :] * _conv_tap(views_ref, kk + CONV_HALO - (CONV_WIDTH - 1), tile)
    return acc


def _conv_fwd(zglu, conv_w, conv_b, ln_g, ln_b, n_b, seq):
    tile = min(256, seq)

    def body(z_ref, w_ref, b_ref, g_ref, bb_ref, o_ref, u1_ref, u0_ref, views_ref):
        _conv_fill_glu(z_ref, u0_ref, seq, tile)
        for t in range(seq // tile):
            u1 = _conv_tile(u0_ref, views_ref, w_ref, b_ref, t, tile)
            u1_ref[0, t * tile:(t + 1) * tile, :] = u1
            o_ref[0, t * tile:(t + 1) * tile, :] = _ln_silu(u1, g_ref[...], bb_ref[...]).astype(BF16)

    whole2 = lambda arr: pl.BlockSpec(arr.shape, lambda b: (0, 0))
    seq_spec = pl.BlockSpec((1, seq, CONV_CH), lambda b: (b, 0, 0))
    return pl.pallas_call(
        body, name="conv_fwd", grid=(n_b,),
        in_specs=[pl.BlockSpec((1, seq, 2 * CONV_CH), lambda b: (b, 0, 0)), whole2(conv_w), whole2(conv_b),
                  whole2(ln_g), whole2(ln_b)],
        out_specs=[seq_spec, seq_spec],
        out_shape=[jax.ShapeDtypeStruct((n_b, seq, CONV_CH), BF16), jax.ShapeDtypeStruct((n_b, seq, CONV_CH), F32)],
        scratch_shapes=[pltpu.VMEM((seq + CONV_HALO + CONV_TAIL, CONV_CH), F32),
                        pltpu.VMEM((8, tile + CONV_HALO, CONV_CH), F32)],
        compiler_params=_params(("parallel",)),
    )(zglu, conv_w, conv_b, ln_g, ln_b)


def _conv_bwd(zglu, u1_saved, du3, conv_w, ln_g, ln_b, n_b, seq):
    tile = min(256, seq)
    n_t = seq // tile

    def body(z_ref, u1_ref, du3_ref, w_ref, g_ref, bb_ref, dz_ref, dw_ref, db_ref, dg_ref, dbb_ref, u0_ref, du1_ref,
             u0_views, du1_views):
        @pl.when(pl.program_id(0) == 0)
        def _():
            for r in (dw_ref, db_ref, dg_ref, dbb_ref):
                r[...] = jnp.zeros_like(r)

        _conv_fill_glu(z_ref, u0_ref, seq, tile)
        du1_ref[seq:seq + CONV_HALO + CONV_TAIL, :] = jnp.zeros((CONV_HALO + CONV_TAIL, CONV_CH), F32)
        g = g_ref[...]
        for t in range(n_t):
            u1 = u1_ref[0, t * tile:(t + 1) * tile, :]
            mu = jnp.mean(u1, axis=-1, keepdims=True)
            uc = u1 - mu
            r = lax.rsqrt(jnp.mean(uc * uc, axis=-1, keepdims=True) + EPS)
            xh = uc * r
            y = xh * g + bb_ref[...]
            sg = _sigmoid(y)
            dy = du3_ref[0, t * tile:(t + 1) * tile, :].astype(F32) * (sg * (1.0 + y * (1.0 - sg)))
            dg_ref[...] += jnp.sum(dy * xh, axis=0, keepdims=True)
            dbb_ref[...] += jnp.sum(dy, axis=0, keepdims=True)
            dxh = dy * g
            du1 = r * (dxh - jnp.mean(dxh, axis=-1, keepdims=True) - xh * jnp.mean(dxh * xh, axis=-1, keepdims=True))
            db_ref[...] += jnp.sum(du1, axis=0, keepdims=True)
            du1_ref[t * tile:(t + 1) * tile, :] = du1
        for t in range(n_t):
            du1 = du1_ref[t * tile:(t + 1) * tile, :]
            du0 = jnp.zeros((tile, CONV_CH), F32)
            _conv_windows(u0_ref, u0_views, t, tile)
            _conv_windows(du1_ref, du1_views, t, tile)
            for kk in range(CONV_WIDTH):
                du0 = du0 + w_ref[kk:kk + 1, :] * _conv_tap(du1_views, CONV_WIDTH - 1 - kk, tile)
                u0_tap = _conv_tap(u0_views, kk + CONV_HALO - (CONV_WIDTH - 1), tile)
                dw_ref[kk:kk + 1, :] += jnp.sum(du1 * u0_tap, axis=0, keepdims=True)
            zt = z_ref[0, t * tile:(t + 1) * tile, :].astype(F32)
            ga, sb = zt[:, :CONV_CH], _sigmoid(zt[:, CONV_CH:])
            dz_ref[0, t * tile:(t + 1) * tile, :CONV_CH] = (du0 * sb).astype(BF16)
            dz_ref[0, t * tile:(t + 1) * tile, CONV_CH:] = (du0 * ga * sb * (1.0 - sb)).astype(BF16)

    whole2 = lambda arr: pl.BlockSpec(arr.shape, lambda b: (0, 0))
    z_spec = pl.BlockSpec((1, seq, 2 * CONV_CH), lambda b: (b, 0, 0))
    seq_spec = pl.BlockSpec((1, seq, CONV_CH), lambda b: (b, 0, 0))
    return pl.pallas_call(
        body, name="conv_bwd", grid=(n_b,),
        in_specs=[z_spec, seq_spec, seq_spec, whole2(conv_w), whole2(ln_g), whole2(ln_b)],
        out_specs=[z_spec, whole2(conv_w), whole2(ln_g), whole2(ln_g), whole2(ln_b)],
        out_shape=[jax.ShapeDtypeStruct((n_b, seq, 2 * CONV_CH), BF16), jax.ShapeDtypeStruct(conv_w.shape, F32),
                   jax.ShapeDtypeStruct(ln_g.shape, F32), jax.ShapeDtypeStruct(ln_g.shape, F32),
                   jax.ShapeDtypeStruct(ln_b.shape, F32)],
        scratch_shapes=[pltpu.VMEM((seq + CONV_HALO + CONV_TAIL, CONV_CH), F32)] * 2
        + [pltpu.VMEM((8, tile + CONV_HALO, CONV_CH), F32)] * 2,
        compiler_params=_params(("arbitrary",)),
    )(zglu, u1_saved, du3, conv_w, ln_g, ln_b)


def _sum_parts(name, parts):
    n_parts = parts.shape[0]

    def body(p_ref, o_ref):
        gg = p_ref[0].astype(F32)
        for j in range(1, n_parts):
            gg = gg + p_ref[j].astype(F32)
        o_ref[...] = gg

    return pl.pallas_call(body, name=name, out_shape=jax.ShapeDtypeStruct(parts.shape[1:], F32),
                          compiler_params=_params(None))(parts)


def _adamw(name, w, parts, m, v, transposed=False):
    n_parts = parts.shape[0]
    rows, cols = w.shape
    tr = ADAM_ROWS if rows % ADAM_ROWS == 0 else rows

    def body(w_ref, p_ref, m_ref, v_ref, g_ref, d_ref, nm_ref, nv_ref):
        gg = p_ref[0].astype(F32)
        for j in range(1, n_parts):
            gg = gg + p_ref[j].astype(F32)
        if transposed:
            gg = gg.T
        nm = ADAM_B1 * m_ref[...] + (1.0 - ADAM_B1) * gg
        nv = ADAM_B2 * v_ref[...] + (1.0 - ADAM_B2) * jnp.square(gg)
        m_hat = nm / (1.0 - ADAM_B1 ** ADAM_STEP)
        v_hat = nv / (1.0 - ADAM_B2 ** ADAM_STEP)
        g_ref[...] = gg
        d_ref[...] = -ADAM_LR * (m_hat / (jnp.sqrt(v_hat) + ADAM_EPS) + ADAM_WD * w_ref[...])
        nm_ref[...] = nm
        nv_ref[...] = nv

    shape = jax.ShapeDtypeStruct(w.shape, F32)
    blk = pl.BlockSpec((tr, cols), lambda i: (i, 0))
    p_spec = (pl.BlockSpec((n_parts, cols, tr), lambda i: (0, 0, i)) if transposed
              else pl.BlockSpec((n_parts, tr, cols), lambda i: (0, i, 0)))
    return pl.pallas_call(body, name=name, grid=(rows // tr,), in_specs=[blk, p_spec, blk, blk], out_specs=[blk] * 4,
                          out_shape=[shape] * 4, compiler_params=_params(("parallel",)))(w, parts, m, v)


def _rope_tables(seq):
    inv_freq = ROPE_THETA ** (-jnp.arange(0, QK_ROPE_DIM, 2, dtype=F32) / QK_ROPE_DIM)
    ang = jnp.arange(seq, dtype=F32)[:, None] * inv_freq[None, :]
    cos, sin = jnp.cos(ang), jnp.sin(ang)
    half = QK_ROPE_DIM // 2
    lane_half = HEAD_PAD // 2
    one = lambda n: jnp.ones((seq, n), F32)
    z = lambda n: jnp.zeros((seq, n), F32)
    used_hi = QK_HEAD_DIM - lane_half - half
    cos_t = jnp.concatenate([cos, one(lane_half - half), cos, one(used_hi), z(lane_half - half - used_hi)], axis=1)
    sin_t = jnp.concatenate([-sin, z(lane_half - half), sin, z(lane_half - half)], axis=1)
    return cos_t, sin_t


def _pad_lanes(v, width=HEAD_PAD):
    return jnp.pad(v, [(0, 0)] * (v.ndim - 1) + [(0, width - v.shape[-1])])


_LANE_HALF_NOPE = HEAD_PAD // 2 - QK_ROPE_DIM // 2


def _head_lanes(v):
    rot = v[..., QK_NOPE_DIM:]
    half = QK_ROPE_DIM // 2
    return _pad_lanes(jnp.concatenate([rot[..., :half], v[..., :_LANE_HALF_NOPE], rot[..., half:],
                                       v[..., _LANE_HALF_NOPE:QK_NOPE_DIM]], axis=-1))


def _head_dims(g):
    half = QK_ROPE_DIM // 2
    lane_half = HEAD_PAD // 2
    return jnp.concatenate([g[..., half:lane_half], g[..., lane_half + half:QK_HEAD_DIM], g[..., :half],
                            g[..., lane_half:lane_half + half]], axis=-1)


def _unstack_cols(s):
    return s.transpose(1, 0, 2).reshape(s.shape[1], N_DEV * s.shape[2])


def _stack_cols(g, dtype):
    rows, cols = g.shape
    return g.reshape(rows, N_DEV, cols // N_DEV).transpose(1, 0, 2).astype(dtype)


def kernel(x, c, w_ada, b_ada, norm1_g, w_in, q_latent_g, w_uq, kv_latent_g, w_ukv, qk_norm_q_g, qk_norm_k_g, w_o_mla, conv_w, conv_b, conv_ln_g, conv_ln_b, w_pw_out, w_out, norm2_g, w_ff1, w_ff2, loss_target, m_w_ada, m_b_ada, m_norm1_g, m_w_in, m_q_latent_g, m_w_uq, m_kv_latent_g, m_w_ukv, m_qk_norm_q_g, m_qk_norm_k_g, m_w_o_mla, m_conv_w, m_conv_b, m_conv_ln_g, m_conv_ln_b, m_w_pw_out, m_w_out, m_norm2_g, m_w_ff1, m_w_ff2, v_w_ada, v_b_ada, v_norm1_g, v_w_in, v_q_latent_g, v_w_uq, v_kv_latent_g, v_w_ukv, v_qk_norm_q_g, v_qk_norm_k_g, v_w_o_mla, v_conv_w, v_conv_b, v_conv_ln_g, v_conv_ln_b, v_w_pw_out, v_w_out, v_norm2_g, v_w_ff1, v_w_ff2):
    given = dict(locals())
    local = {n: given[n][0] for n in WEIGHTS}
    vec = {n: local[n].reshape(1, -1) for n in REPLICATED}
    bf = lambda n: local[n].astype(BF16)
    n_b, seq, d = x.shape
    n_rows = n_b * seq
    x2 = x.reshape(n_rows, d)
    t2 = loss_target.reshape(n_rows, d)
    me = 4 * lax.axis_index("x") + 2 * lax.axis_index("y") + lax.axis_index("c")
    ada_cols = local["w_ada"].shape[1]

    tsh = lambda n: local[n].T.astype(BF16)
    c_all, w_in_s, w_uq_s, w_ukv_s, conv_w_s = _exchange(
        "gather_early", [(c, True), (tsh("w_in"), True), (bf("w_uq"), True), (bf("w_ukv"), True), (local["conv_w"], True)],
        by_chip=True)
    w_in_t = w_in_s.reshape(-1, d)
    zrows = lambda n: jnp.zeros((n, d), BF16)
    rot_half = QK_ROPE_DIM // 2
    w_sm_t = jnp.concatenate([w_in_t[:OFF_KV + rot_half], zrows(HEAD_PAD // 2 - rot_half), w_in_t[OFF_KV + rot_half:OFF_KR],
                              zrows(HEAD_PAD // 2 - rot_half)], axis=0)
    w_glu_t = w_in_t[OFF_KR:OFF_GLU]
    w_gate_t = w_in_t[OFF_GLU:]
    wuq = _head_lanes(_unstack_cols(w_uq_s).reshape(Q_LORA, N_HEADS, QK_HEAD_DIM)).reshape(Q_LORA, N_HEADS * HEAD_PAD)
    wukv_f = _unstack_cols(w_ukv_s).reshape(KV_LORA, N_HEADS, QK_NOPE_DIM + V_HEAD_DIM)
    wv = wukv_f[:, :, QK_NOPE_DIM:]
    odd = (jnp.arange(N_HEADS) % 2 == 1)[None, :, None]
    wuv = jnp.where(odd, jnp.pad(wv, ((0, 0), (0, 0), (V_HEAD_DIM, 0))), jnp.pad(wv, ((0, 0), (0, 0), (0, V_HEAD_DIM))))
    wuk = _head_lanes(_pad_lanes(wukv_f[:, :, :QK_NOPE_DIM], QK_HEAD_DIM))
    wukv = jnp.concatenate([wuk, wuv], axis=1).reshape(KV_LORA, 2 * N_HEADS * HEAD_PAD)
    gqn = _head_lanes(vec["qk_norm_q_g"])
    gkn = _head_lanes(vec["qk_norm_k_g"])
    conv_w_f = jnp.pad(_unstack_cols(conv_w_s), ((0, 1), (0, 0)))
    rope = _rope_tables(seq)

    all_rows = N_DEV * n_b
    pad_rows = (-all_rows) % ROWS_PAD
    c_rows = jnp.pad(c_all.reshape(all_rows, d), ((0, pad_rows), (0, 0)))
    b_cols = lax.dynamic_slice(local["b_ada"], (me * ada_cols,), (ada_cols,))
    mod_cols = _mm("ada_fwd", c_rows, local["w_ada"], "nn", F32, a_fn=_silu, epi=lambda acc, b: acc + b,
                   epi_in=(jnp.broadcast_to(b_cols, (all_rows + pad_rows, ada_cols)),))
    (mod_s,) = _exchange("scatter_mod", [(mod_cols[:all_rows].reshape(N_DEV, n_b, ada_cols), False)])
    mod = mod_s.transpose(1, 0, 2).reshape(n_b, ADA_CHUNKS, 1, d)
    shift1, scale1, gate1, shift2, scale2, gate2 = [mod[:, i] for i in range(ADA_CHUNKS)]

    h, zgate, zglu, zsm = _in_proj_fwd(x2, scale1, shift1, vec["norm1_g"], [w_gate_t, w_glu_t, w_sm_t],
                                       [BF16, BF16, F32], seq)
    q, k, v, kt = _mla_prep_fwd(zsm, wuq, wukv, vec["q_latent_g"], vec["kv_latent_g"], gqn, gkn, rope, n_b, seq)
    attn, lse, (w_o_s, w_pw_s, w_out_s, w_ff1_s, w_ff2_s) = _attn_fwd(
        q, k, v, [(tsh("w_o_mla"), True), (tsh("w_pw_out"), True), (bf("w_out"), True), (tsh("w_ff1"), True),
                  (bf("w_ff2"), True)], n_b, seq)
    w_o_t = w_o_s.reshape(d, -1)
    w_pw_t = w_pw_s.reshape(d, -1)
    w_out_f = w_out_s.reshape(d, d)
    w_ff1_t = w_ff1_s.reshape(-1, d)
    w_ff2_f = w_ff2_s.reshape(-1, d)
    attn2 = attn.reshape(n_rows, N_HEADS * V_HEAD_DIM)
    u3, u1 = _conv_fwd(zglu.reshape(n_b, seq, 2 * CONV_CH), conv_w_f, vec["conv_b"], vec["conv_ln_g"], vec["conv_ln_b"], n_b, seq)
    u32 = u3.reshape(n_rows, CONV_CH)
    ya = _mm("mla_out", attn2, w_o_t, "nt", BF16)
    yb = _mm("conv_out", u32, w_pw_t, "nt", BF16)
    mmr = functools.partial(_mm_rows, n_rows=n_rows, seq=seq)

    def merge_fn(t):
        return _sigmoid(t[0]) * t[2] + _sigmoid(t[1]) * t[3]

    def mid_fn(acc, r, b, cc):
        x1_ = r[0] + b[0] * acc
        return [acc, x1_, _norm_mod(x1_, cc[0], b[1], b[2])], [], []

    mrg, mixed, x1, h2 = mmr("out_proj", [(zgate, d, 0), (zgate, d, 1), (ya, d, 0), (yb, d, 0)], merge_fn, w_out_f, "nn",
                             mid_fn, rows=[_full(x2)], bats=[gate1, scale2, shift2], consts=[vec["norm2_g"]],
                             outs=[(d, BF16), (d, F32), (d, BF16)], a_out=BF16)

    a = _mm("ff1", h2, w_ff1_t, "nt", BF16)

    def loss_fn(ff, r, b, cc):
        err = r[0] + b[0] * ff - r[1]
        dy_ = err * (1.0 / d)
        sq = jnp.broadcast_to(jnp.sum(err * err, keepdims=True), (1, LANES))
        return [dy_, b[0] * dy_], [jnp.sum(dy_ * ff, axis=0, keepdims=True)], [sq]

    dy, df, dgate2, sq_err = mmr("ff2_loss", [(a, a.shape[1], 0)], lambda t: _relu2(t[0]), w_ff2_f, "nn", loss_fn,
                                 rows=[_full(x1), _full(t2)], bats=[gate2], outs=[(d, F32), (d, BF16)], bat_outs=[d],
                                 tot_outs=[(1, LANES)], tk=a.shape[1])

    da = _mm("ff2_bwd", df, w_ff2_f, "nt", BF16, epi=lambda acc, av: acc * 2.0 * jnp.maximum(av, 0.0), epi_in=(a,))
    g_ff2 = _mm("ff2_dw", a, df, "tn", BF16, a_fn=_relu2)
    g_ff1_t = _mm("ff1_dw", da, h2, "tn", BF16)

    def mid_bwd(dh2_, r, b, cc):
        dx, dsc, dsh, dg = _norm_mod_bwd(r[0], cc[0], b[0], dh2_)
        dx1_ = r[1] + dx
        return [dx1_, b[1] * dx1_], [dsc, dsh, jnp.sum(dx1_ * r[2].astype(F32), axis=0, keepdims=True)], [dg]

    dx1, dmixed, dscale2, dshift2, dgate1, g_norm2 = mmr(
        "ff1_bwd", [(da, da.shape[1], 0)], None, w_ff1_t, "nn", mid_bwd, rows=[_full(x1), _full(dy), _full(mixed)],
        bats=[scale2, gate1], consts=[vec["norm2_g"]], outs=[(d, F32), (d, BF16)], bat_outs=[d, d, d],
        tot_outs=[(1, d)], tk=da.shape[1])

    g_out = _mm("out_proj_dw", mrg, dmixed, "tn", BF16)

    def merge_bwd(dm, r, b, cc):
        ya_, yb_ = r[2].astype(F32), r[3].astype(F32)
        sa, sb = _sigmoid(r[0].astype(F32)), _sigmoid(r[1].astype(F32))
        return [dm * ya_ * sa * (1.0 - sa), dm * yb_ * sb * (1.0 - sb), dm * sa, dm * sb], [], []

    dzga, dzgb, dya, dyb = mmr("out_proj_bwd", [(dmixed, d, 0)], None, w_out_f, "nt", merge_bwd,
                               rows=[(zgate, d, 0), (zgate, d, 1), _full(ya), _full(yb)], outs=[(d, BF16)] * 4)
    dattn = _mm("mla_out_bwd", dya, w_o_t, "nn", BF16)
    g_o_t = _mm("mla_out_dw", dya, attn2, "tn", BF16)
    du3 = _mm("conv_out_bwd", dyb, w_pw_t, "nn", BF16)
    g_pw_t = _mm("conv_out_dw", dyb, u32, "tn", BF16)

    dzglu, g_conv_w, g_conv_b, g_ln_g, g_ln_b = _conv_bwd(
        zglu.reshape(n_b, seq, 2 * CONV_CH), u1, du3.reshape(n_b, seq, CONV_CH), conv_w_f, vec["conv_ln_g"],
        vec["conv_ln_b"], n_b, seq)
    dzglu = dzglu.reshape(n_rows, 2 * CONV_CH)

    dq, dk, dv, (p_ff2, p_ff1, p_out, p_pw, p_o) = _attn_bwd(
        q, k, v, kt, dattn.reshape(n_b, seq, N_HEADS * V_HEAD_DIM), attn, lse,
        [(g_ff2.reshape(N_DEV, -1, d), False), (g_ff1_t.reshape(N_DEV, -1, d), False), (g_out.reshape(N_DEV, -1, d), False),
         (g_pw_t.reshape(N_DEV, -1, CONV_CH), False), (g_o_t.reshape(N_DEV, -1, N_HEADS * V_HEAD_DIM), False)], n_b, seq)
    dzsm, g_wuq, g_wukv, g_gq, g_gkv, g_gqn, g_gkn = _mla_prep_bwd(
        zsm, dq, dk, dv, wuq, wukv, vec["q_latent_g"], vec["kv_latent_g"], gqn, gkn, rope, n_b, seq)

    g_gate_a_t = _mm("in_proj_gate_dw_a", dzga, h, "tn", BF16)
    g_gate_b_t = _mm("in_proj_gate_dw_b", dzgb, h, "tn", BF16)
    g_glu_t = _mm("in_proj_glu_dw", dzglu, h, "tn", BF16)
    g_sm_t = _mm("in_proj_sm_dw", dzsm, h, "tn", BF16)
    g_in_t = jnp.concatenate([g_sm_t[:OFF_KV + rot_half], g_sm_t[OFF_KV + HEAD_PAD // 2:OFF_KV + HEAD_PAD // 2 + rot_half],
                              g_glu_t, g_gate_a_t, g_gate_b_t], axis=0)
    g_uq = _head_dims(g_wuq.reshape(Q_LORA, N_HEADS, HEAD_PAD)).reshape(Q_LORA, N_HEADS * QK_HEAD_DIM)
    g_wukv = g_wukv.reshape(KV_LORA, 2, N_HEADS, HEAD_PAD)
    g_v = jnp.where(odd, g_wukv[:, 1, :, V_HEAD_DIM:], g_wukv[:, 1, :, :V_HEAD_DIM])
    g_ukv = jnp.concatenate([_head_dims(g_wukv[:, 0])[:, :, :QK_NOPE_DIM], g_v], axis=2).reshape(KV_LORA, -1)

    grad_x, dscale1, dshift1, g_norm1, (p_in, p_uq, p_ukv, p_conv_w) = _in_proj_bwd(
        [(dzga, w_gate_t[:d]), (dzgb, w_gate_t[d:]), (dzglu, w_glu_t), (dzsm, w_sm_t)], x2, dx1, scale1, vec["norm1_g"],
        [(g_in_t.reshape(N_DEV, -1, d), False), (_stack_cols(g_uq, BF16), False), (_stack_cols(g_ukv, BF16), False),
         (_stack_cols(g_conv_w[:CONV_WIDTH], F32), False)], seq)

    dmod = jnp.concatenate([dshift1, dscale1, dgate1, dshift2, dscale2, dgate2], axis=1).reshape(n_b, N_DEV, ada_cols)
    (dmod_s,) = _exchange("scatter_dmod", [(dmod.transpose(1, 0, 2), False)])
    dmod_rows = jnp.pad(dmod_s.reshape(all_rows, ada_cols), ((0, pad_rows), (0, 0)))
    g_ada = _mm("ada_dw", c_rows, dmod_rows, "tn", F32, a_fn=_silu)
    (g_b_cols,) = _rowwise("ada_db", lambda r, b, cc: ([], [], [jnp.sum(r[0], axis=0, keepdims=True)]),
                           all_rows + pad_rows, all_rows + pad_rows, rows=[_full(dmod_rows)], tot_outs=[(1, ada_cols)])

    partial_of = {"norm1_g": g_norm1, "q_latent_g": g_gq, "kv_latent_g": g_gkv, "qk_norm_q_g": _head_dims(g_gqn),
                  "qk_norm_k_g": _head_dims(g_gkn), "conv_b": g_conv_b, "conv_ln_g": g_ln_g, "conv_ln_b": g_ln_b, "norm2_g": g_norm2}
    names = [n for n in REPLICATED if n != "b_ada"]
    pieces = [_pad_lanes(partial_of[n], -(-partial_of[n].shape[1] // LANES) * LANES) for n in names] + [g_b_cols, sq_err]
    widths = [p.shape[1] for p in pieces]
    small = jnp.concatenate(pieces, axis=1)
    small = _pad_lanes(small, -(-small.shape[1] // (8 * LANES)) * 8 * LANES).reshape(-1, LANES)
    (small_s,) = _exchange("gather_small_grads", [(small, True)])
    small_s = small_s.reshape(N_DEV, 1, -1)
    parts = {}
    off = 0
    for n, wd in zip(names, widths):
        parts[n] = small_s[:, :, off:off + vec[n].shape[1]]
        off += wd
    parts["b_ada"] = small_s[:, 0, off:off + ada_cols].reshape(1, 1, N_DEV * ada_cols)
    loss = jnp.sum(small_s[:, 0, off + ada_cols]) * (0.5 / d)
    g_in_mine = _sum_parts("sum_w_in", p_in).T
    parts.update({"w_ada": g_ada[None], "w_in": g_in_mine[None], "w_uq": p_uq, "w_ukv": p_ukv, "w_o_mla": p_o,
                  "conv_w": p_conv_w, "w_pw_out": p_pw, "w_out": p_out, "w_ff1": p_ff1, "w_ff2": p_ff2})
    transposed = ("w_o_mla", "w_pw_out", "w_ff1")

    grad_out, delta_out, m_out, v_out = [], [], [], []
    for n in WEIGHTS:
        shape2 = local[n].shape if local[n].ndim == 2 else (1, local[n].shape[0])
        g_w, d_w, n_m, n_v = _adamw("adamw_" + n, local[n].reshape(shape2), parts[n], given["m_" + n].reshape(shape2),
                                    given["v_" + n].reshape(shape2), transposed=n in transposed)
        full_shape = given[n].shape
        grad_out.append(g_w.reshape(full_shape))
        delta_out.append(d_w.reshape(full_shape))
        m_out.append(n_m.reshape(full_shape))
        v_out.append(n_v.reshape(full_shape))
    return (loss, grad_x.reshape(n_b, seq, d), *grad_out, *delta_out, *m_out, *v_out)
```

```python
import functools

import jax
import jax.numpy as jnp
from jax import lax
from jax.experimental import pallas as pl
from jax.experimental.pallas import tpu as pltpu

F32 = jnp.float32
BF16 = jnp.bfloat16

N_DEV = 8
EPS = 1e-6
N_HEADS = 8
QK_HEAD_DIM = 96
QK_NOPE_DIM = 64
QK_ROPE_DIM = 32
V_HEAD_DIM = 64
HEAD_PAD = 128
Q_LORA = 256
KV_LORA = 128
CONV_CH = 512
CONV_WIDTH = 31
CONV_HALO = 32
CONV_TAIL = 8
CHUNK = 64
ROPE_THETA = 10000.0
OFF_Q = Q_LORA
OFF_KV = OFF_Q + KV_LORA
OFF_KR = OFF_KV + QK_ROPE_DIM
OFF_GLU = OFF_KR + 2 * CONV_CH
ADA_CHUNKS = 6
ADAM_LR = 0.001
ADAM_B1 = 0.9
ADAM_B2 = 0.999
ADAM_EPS = 1e-08
ADAM_WD = 0.01
ADAM_STEP = 10
LANES = 128
VMEM_LIMIT = 56 * 1024 * 1024
NEG_BIG = -1e30
ATT_HEADS = 4
ATT_TILE = 512
PREP_TILE = 512
ATT_SCALE = QK_HEAD_DIM ** -0.5
LOG2E = 1.4426950408889634
LN2 = 0.6931471805599453
QK_SCALE = ATT_SCALE * LOG2E
ADAM_ROWS = 256
ROWS_PAD = 16

REPLICATED = ("b_ada", "norm1_g", "q_latent_g", "kv_latent_g", "qk_norm_q_g", "qk_norm_k_g", "conv_b", "conv_ln_g",
              "conv_ln_b", "norm2_g")
WEIGHTS = ("w_ada", "b_ada", "norm1_g", "w_in", "q_latent_g", "w_uq", "kv_latent_g", "w_ukv", "qk_norm_q_g",
           "qk_norm_k_g", "w_o_mla", "conv_w", "conv_b", "conv_ln_g", "conv_ln_b", "w_pw_out", "w_out", "norm2_g",
           "w_ff1", "w_ff2")


def _tile(dim, pref):
    if dim <= pref:
        return dim
    t = (pref // LANES) * LANES
    while dim % t:
        t -= LANES
    return t


def _params(semantics):
    return pltpu.CompilerParams(dimension_semantics=semantics, vmem_limit_bytes=VMEM_LIMIT)


def _sigmoid(v):
    return 1.0 / (1.0 + jnp.exp(-v))


def _silu(v):
    return v * _sigmoid(v)


def _relu2(v):
    return jnp.square(jnp.maximum(v, 0.0))


_DIMS = {"nn": (((1,), (0,)), ((), ())), "nt": (((1,), (1,)), ((), ())), "tn": (((0,), (0,)), ((), ()))}


def _mm(name, a, b, mode, out_dtype, *, a_fn=None, epi=None, epi_in=(), tm=1024, tn=1024, tk=1024):
    if mode == "nn":
        (m, k), n = a.shape, b.shape[1]
    elif mode == "nt":
        (m, k), n = a.shape, b.shape[0]
    else:
        (k, m), n = a.shape, b.shape[1]
    tm, tn, tk = _tile(m, tm), _tile(n, tn), _tile(k, tk)
    nk = k // tk
    a_spec = (pl.BlockSpec((tk, tm), lambda i, j, kk: (kk, i)) if mode == "tn"
              else pl.BlockSpec((tm, tk), lambda i, j, kk: (i, kk)))
    b_spec = (pl.BlockSpec((tn, tk), lambda i, j, kk: (j, kk)) if mode == "nt"
              else pl.BlockSpec((tk, tn), lambda i, j, kk: (kk, j)))
    o_spec = e_spec = pl.BlockSpec((tm, tn), lambda i, j, kk: (i, j))
    out_shape = jax.ShapeDtypeStruct((m, n), out_dtype)
    n_epi = len(epi_in)

    def body(a_ref, b_ref, *rest):
        epi_refs, o_ref, acc_ref = rest[:n_epi], rest[n_epi], rest[n_epi + 1]
        kk = pl.program_id(2)

        @pl.when(kk == 0)
        def _():
            acc_ref[...] = jnp.zeros_like(acc_ref)

        av = a_ref[...]
        if a_fn is not None:
            av = a_fn(av.astype(F32))
        acc_ref[...] += lax.dot_general(av.astype(BF16), b_ref[...].astype(BF16), _DIMS[mode],
                                        preferred_element_type=F32)

        @pl.when(kk == nk - 1)
        def _():
            acc = acc_ref[...]
            if epi is not None:
                acc = epi(acc, *[r[...].astype(F32) for r in epi_refs])
            o_ref[...] = acc.astype(out_dtype)

    return pl.pallas_call(
        body, name=name, grid=(m // tm, n // tn, nk),
        in_specs=[a_spec, b_spec] + [e_spec] * n_epi, out_specs=o_spec, out_shape=out_shape,
        scratch_shapes=[pltpu.VMEM((tm, tn), F32)],
        compiler_params=_params(("parallel", "parallel", "arbitrary")),
    )(a, b, *epi_in)


def _rowwise(name, fn, n_rows, seq, rows, bats=(), consts=(), outs=(), bat_outs=(), tot_outs=(), tm=256):
    tm = min(tm, seq)
    per_seq = seq // tm
    n_b = n_rows // seq
    nr, nb, nc, no, nbo, nto = len(rows), len(bats), len(consts), len(outs), len(bat_outs), len(tot_outs)

    def body(*refs):
        i = pl.program_id(0)
        r_in = [r[...] for r in refs[:nr]]
        b_in = [r[0] for r in refs[nr:nr + nb]]
        c_in = [r[...] for r in refs[nr + nb:nr + nb + nc]]
        o_refs = refs[nr + nb + nc:nr + nb + nc + no]
        bo_refs = refs[nr + nb + nc + no:nr + nb + nc + no + nbo]
        to_refs = refs[nr + nb + nc + no + nbo:]
        o_val, bo_val, to_val = fn(r_in, b_in, c_in)
        for r, v in zip(o_refs, o_val):
            r[...] = v.astype(r.dtype)
        if nbo:
            @pl.when(i % per_seq == 0)
            def _():
                for r in bo_refs:
                    r[...] = jnp.zeros_like(r)

            for r, v in zip(bo_refs, bo_val):
                r[0] += v
        if nto:
            @pl.when(i == 0)
            def _():
                for r in to_refs:
                    r[...] = jnp.zeros_like(r)

            for r, v in zip(to_refs, to_val):
                r[...] += v

    in_specs = [pl.BlockSpec((tm, w), functools.partial(lambda cb, i: (i, cb), cb)) for (_, w, cb) in rows]
    in_specs += [pl.BlockSpec((1, 1, bt.shape[2]), lambda i: (i // per_seq, 0, 0)) for bt in bats]
    in_specs += [pl.BlockSpec(ct.shape, lambda i: (0, 0)) for ct in consts]
    out_specs = [pl.BlockSpec((tm, w), lambda i: (i, 0)) for (w, _) in outs]
    out_specs += [pl.BlockSpec((1, 1, w), lambda i: (i // per_seq, 0, 0)) for w in bat_outs]
    out_specs += [pl.BlockSpec(shp, lambda i: (0, 0)) for shp in tot_outs]
    out_shape = [jax.ShapeDtypeStruct((n_rows, w), dt) for (w, dt) in outs]
    out_shape += [jax.ShapeDtypeStruct((n_b, 1, w), F32) for w in bat_outs]
    out_shape += [jax.ShapeDtypeStruct(shp, F32) for shp in tot_outs]
    res = pl.pallas_call(
        body, name=name, grid=(n_rows // tm,), in_specs=in_specs, out_specs=out_specs, out_shape=out_shape,
        compiler_params=_params(("arbitrary",)),
    )(*[r[0] for r in rows], *bats, *consts)
    return res


def _full(arr):
    return (arr, arr.shape[1], 0)


def _mm_rows(name, a_rows, a_fn, w, mode, fn, n_rows, seq, rows=(), bats=(), consts=(), outs=(), bat_outs=(),
             tot_outs=(), a_out=None, tm=512, tk=1024):
    tm = min(tm, seq)
    per_seq = seq // tm
    n_b = n_rows // seq
    k = a_rows[0][1]
    if mode == "nt":
        n_out, tk = w.shape[0], _tile(k, tk)
        w_spec = pl.BlockSpec((n_out, tk), lambda i, kk: (0, kk))
    else:
        n_out, tk = w.shape[1], _tile(k, tk)
        w_spec = pl.BlockSpec((tk, n_out), lambda i, kk: (kk, 0))
    nk = k // tk
    na, nr, nb, nc = len(a_rows), len(rows), len(bats), len(consts)
    n_extra = 0 if a_out is None else 1
    no, nbo, nto = len(outs), len(bat_outs), len(tot_outs)

    def body(*refs):
        i, kk = pl.program_id(0), pl.program_id(1)
        a_refs, w_ref = refs[:na], refs[na]
        pos = na + 1
        r_refs, b_refs, c_refs = refs[pos:pos + nr], refs[pos + nr:pos + nr + nb], refs[pos + nr + nb:pos + nr + nb + nc]
        pos += nr + nb + nc
        ao_refs = refs[pos:pos + n_extra]
        pos += n_extra
        o_refs, bo_refs, to_refs = refs[pos:pos + no], refs[pos + no:pos + no + nbo], refs[pos + no + nbo:pos + no + nbo + nto]
        acc_ref = refs[pos + no + nbo + nto]

        @pl.when(kk == 0)
        def _():
            acc_ref[...] = jnp.zeros_like(acc_ref)

        tiles = [r[...] for r in a_refs]
        av = a_fn([t.astype(F32) for t in tiles]) if a_fn is not None else tiles[0]
        av = av.astype(BF16)
        if n_extra:
            ao_refs[0][...] = av.astype(ao_refs[0].dtype)
        acc_ref[...] += lax.dot_general(av, w_ref[...].astype(BF16), _DIMS[mode], preferred_element_type=F32)

        @pl.when(kk == nk - 1)
        def _():
            o_val, bo_val, to_val = fn(acc_ref[...], [r[...] for r in r_refs], [r[0] for r in b_refs],
                                       [r[...] for r in c_refs])
            for r, v in zip(o_refs, o_val):
                r[...] = v.astype(r.dtype)
            if nbo:
                @pl.when(i % per_seq == 0)
                def _():
                    for r in bo_refs:
                        r[...] = jnp.zeros_like(r)

                for r, v in zip(bo_refs, bo_val):
                    r[0] += v
            if nto:
                @pl.when(i == 0)
                def _():
                    for r in to_refs:
                        r[...] = jnp.zeros_like(r)

                for r, v in zip(to_refs, to_val):
                    r[...] += v

    in_specs = [pl.BlockSpec((tm, tk), functools.partial(lambda cb, i, kk: (i, kk + cb), cb)) for (_, _, cb) in a_rows]
    in_specs += [w_spec]
    in_specs += [pl.BlockSpec((tm, wd), functools.partial(lambda cb, i, kk: (i, cb), cb)) for (_, wd, cb) in rows]
    in_specs += [pl.BlockSpec((1, 1, bt.shape[2]), lambda i, kk: (i // per_seq, 0, 0)) for bt in bats]
    in_specs += [pl.BlockSpec(ct.shape, lambda i, kk: (0, 0)) for ct in consts]
    out_specs = [pl.BlockSpec((tm, tk), lambda i, kk: (i, kk))] * n_extra
    out_specs += [pl.BlockSpec((tm, wd), lambda i, kk: (i, 0)) for (wd, _) in outs]
    out_specs += [pl.BlockSpec((1, 1, wd), lambda i, kk: (i // per_seq, 0, 0)) for wd in bat_outs]
    out_specs += [pl.BlockSpec(shp, lambda i, kk: (0, 0)) for shp in tot_outs]
    out_shape = [jax.ShapeDtypeStruct((n_rows, k), a_out)] if n_extra else []
    out_shape += [jax.ShapeDtypeStruct((n_rows, wd), dt) for (wd, dt) in outs]
    out_shape += [jax.ShapeDtypeStruct((n_b, 1, wd), F32) for wd in bat_outs]
    out_shape += [jax.ShapeDtypeStruct(shp, F32) for shp in tot_outs]
    return pl.pallas_call(
        body, name=name, grid=(n_rows // tm, nk), in_specs=in_specs, out_specs=out_specs, out_shape=out_shape,
        scratch_shapes=[pltpu.VMEM((tm, n_out), F32)],
        compiler_params=_params(("arbitrary", "arbitrary")),
    )(*[a for a, _, _ in a_rows], w, *[r[0] for r in rows], *bats, *consts)


def _norm_mod(x, g, scale, shift):
    r = lax.rsqrt(jnp.mean(x * x, axis=-1, keepdims=True) + EPS)
    xh = x * r
    return xh * g * (1.0 + scale) + shift


def _norm_mod_bwd(x, g, scale, dh):
    r = lax.rsqrt(jnp.mean(x * x, axis=-1, keepdims=True) + EPS)
    xh = x * r
    dn = dh * (1.0 + scale)
    dxh = dn * g
    dx = r * (dxh - xh * jnp.mean(dxh * xh, axis=-1, keepdims=True))
    dscale = jnp.sum(dh * xh * g, axis=0, keepdims=True)
    dshift = jnp.sum(dh, axis=0, keepdims=True)
    dg = jnp.sum(dn * xh, axis=0, keepdims=True)
    return dx, dscale, dshift, dg


def _rms(v, g):
    r = lax.rsqrt(jnp.mean(v * v, axis=-1, keepdims=True) + EPS)
    return v * r * g


def _rms_bwd(v, g, dy):
    r = lax.rsqrt(jnp.mean(v * v, axis=-1, keepdims=True) + EPS)
    vh = v * r
    dvh = dy * g
    dv = r * (dvh - vh * jnp.mean(dvh * vh, axis=-1, keepdims=True))
    return dv, jnp.sum(dy * vh, axis=0, keepdims=True)


def _lane_sum(t):
    return jnp.dot(t.astype(BF16), jnp.ones((HEAD_PAD, HEAD_PAD), BF16), preferred_element_type=F32)


def _head_norm(v, g):
    r = lax.rsqrt(_lane_sum(v * v) * (1.0 / QK_HEAD_DIM) + EPS)
    return v * r * g


def _head_norm_bwd(v, g, dy):
    r = lax.rsqrt(_lane_sum(v * v) * (1.0 / QK_HEAD_DIM) + EPS)
    vh = v * r
    dvh = dy * g
    dv = r * (dvh - vh * (_lane_sum(dvh * vh) * (1.0 / QK_HEAD_DIM)))
    return dv, jnp.sum(dy * vh, axis=0, keepdims=True)


def _swap_halves(t):
    row = lax.broadcasted_iota(jnp.int32, (HEAD_PAD, HEAD_PAD), 0)
    col = lax.broadcasted_iota(jnp.int32, (HEAD_PAD, HEAD_PAD), 1)
    perm = ((row + HEAD_PAD // 2) % HEAD_PAD == col).astype(BF16)
    return jnp.dot(t.astype(BF16), perm, preferred_element_type=F32)


def _rope(v, cos, sin):
    return v * cos + _swap_halves(v) * sin


def _rope_bwd(g, cos, sin):
    return g * cos + _swap_halves(g * sin)


def _mla_prep_fwd(zsm, wuq, wukv, gq, gkv, gqn, gkn, rope, n_b, seq):
    n_rows = n_b * seq
    tm = min(PREP_TILE, seq)
    per_seq = seq // tm
    att_tile = min(ATT_TILE, seq)
    k_cols = N_HEADS * HEAD_PAD

    def body(z_ref, wuq_ref, wukv_ref, gq_ref, gkv_ref, gqn_ref, gkn_ref, c_ref, s_ref, q_ref, k_ref, v_ref, kt_ref):
        z = z_ref[...]
        qn = _rms(z[:, :Q_LORA], gq_ref[...]).astype(BF16)
        kvn = _rms(z[:, Q_LORA:Q_LORA + KV_LORA], gkv_ref[...]).astype(BF16)
        krp = z[:, Q_LORA + KV_LORA:]
        cos, sin = c_ref[...], s_ref[...]
        q_all = jnp.dot(qn, wuq_ref[...], preferred_element_type=F32)
        kv_all = jnp.dot(kvn, wukv_ref[...], preferred_element_type=F32)
        for h in range(N_HEADS):
            cols = slice(h * HEAD_PAD, (h + 1) * HEAD_PAD)
            q_ref[0, h] = (_rope(_head_norm(q_all[:, cols], gqn_ref[...]), cos, sin) * QK_SCALE).astype(BF16)
            kh = _rope(_head_norm(kv_all[:, cols] + krp, gkn_ref[...]), cos, sin)
            k_ref[0, h] = kh.astype(BF16)
            for part in range(tm // att_tile):
                kt_ref[0, h, part] = kh[part * att_tile:(part + 1) * att_tile].T.astype(BF16)
            v_ref[0, h] = kv_all[:, k_cols + h * HEAD_PAD:k_cols + (h + 1) * HEAD_PAD].astype(BF16)

    whole2 = lambda arr: pl.BlockSpec(arr.shape, lambda i: (0, 0))
    rope_spec = pl.BlockSpec((tm, HEAD_PAD), lambda i: (i % per_seq, 0))
    head_spec = pl.BlockSpec((1, N_HEADS, tm, HEAD_PAD), lambda i: (i // per_seq, 0, i % per_seq, 0))
    head_shape = jax.ShapeDtypeStruct((n_b, N_HEADS, seq, HEAD_PAD), BF16)
    t_spec = pl.BlockSpec((1, N_HEADS, tm // att_tile, HEAD_PAD, att_tile), lambda i: (i // per_seq, 0, i % per_seq, 0, 0))
    t_shape = jax.ShapeDtypeStruct((n_b, N_HEADS, seq // att_tile, HEAD_PAD, att_tile), BF16)
    return pl.pallas_call(
        body, name="mla_prep_fwd", grid=(n_rows // tm,),
        in_specs=[pl.BlockSpec((tm, 512), lambda i: (i, 0)), whole2(wuq), whole2(wukv),
                  whole2(gq), whole2(gkv), whole2(gqn), whole2(gkn), rope_spec, rope_spec],
        out_specs=[head_spec] * 3 + [t_spec], out_shape=[head_shape] * 3 + [t_shape],
        compiler_params=_params(("parallel",)),
    )(zsm, wuq, wukv, gq, gkv, gqn, gkn, *rope)


def _mla_prep_bwd(zsm, dq, dk, dv, wuq, wukv, gq, gkv, gqn, gkn, rope, n_b, seq):
    n_rows = n_b * seq
    tm = min(PREP_TILE, seq)
    per_seq = seq // tm
    tn_dims = _DIMS["tn"]
    nt_dims = _DIMS["nt"]
    k_cols = N_HEADS * HEAD_PAD

    def body(z_ref, dq_ref, dk_ref, dv_ref, wuq_ref, wukv_ref, gq_ref, gkv_ref, gqn_ref, gkn_ref,
             c_ref, s_ref, dz_ref, dwuq_ref, dwukv_ref, dgq_ref, dgkv_ref, dgqn_ref, dgkn_ref):
        @pl.when(pl.program_id(0) == 0)
        def _():
            for r in (dwuq_ref, dwukv_ref, dgq_ref, dgkv_ref, dgqn_ref, dgkn_ref):
                r[...] = jnp.zeros_like(r)

        z = z_ref[...]
        zq, zkv, krp = z[:, :Q_LORA], z[:, Q_LORA:Q_LORA + KV_LORA], z[:, Q_LORA + KV_LORA:]
        qn = _rms(zq, gq_ref[...]).astype(BF16)
        kvn = _rms(zkv, gkv_ref[...]).astype(BF16)
        cos, sin = c_ref[...], s_ref[...]
        lane = lax.broadcasted_iota(jnp.int32, (tm, HEAD_PAD), 1)
        rope_lanes = (lane % (HEAD_PAD // 2)) < QK_ROPE_DIM // 2
        q_all = jnp.dot(qn, wuq_ref[...], preferred_element_type=F32)
        k_all = jnp.dot(kvn, wukv_ref[:, :k_cols], preferred_element_type=F32)
        dkrp = jnp.zeros((tm, HEAD_PAD), F32)
        dgqn = jnp.zeros((1, HEAD_PAD), F32)
        dgkn = jnp.zeros((1, HEAD_PAD), F32)
        dq_heads, dk_heads = [], []
        for h in range(N_HEADS):
            cols = slice(h * HEAD_PAD, (h + 1) * HEAD_PAD)
            dqh, dg = _head_norm_bwd(q_all[:, cols], gqn_ref[...],
                                     _rope_bwd(dq_ref[0, h].astype(F32) * ATT_SCALE, cos, sin))
            dgqn += dg
            dq_heads.append(dqh.astype(BF16))
            dkh, dg = _head_norm_bwd(k_all[:, cols] + krp, gkn_ref[...], _rope_bwd(dk_ref[0, h].astype(F32), cos, sin))
            dgkn += dg
            dkrp += jnp.where(rope_lanes, dkh, 0.0)
            dk_heads.append(dkh.astype(BF16))
        dq_all = jnp.concatenate(dq_heads, axis=1)
        dkv_all = jnp.concatenate(dk_heads + [dv_ref[0, h] for h in range(N_HEADS)], axis=1)
        dwuq_ref[...] += lax.dot_general(qn, dq_all, tn_dims, preferred_element_type=F32)
        dqn = lax.dot_general(dq_all, wuq_ref[...], nt_dims, preferred_element_type=F32)
        dwukv_ref[...] += lax.dot_general(kvn, dkv_all, tn_dims, preferred_element_type=F32)
        dkvn = lax.dot_general(dkv_all, wukv_ref[...], nt_dims, preferred_element_type=F32)
        dzq, dg = _rms_bwd(zq, gq_ref[...], dqn)
        dgq_ref[...] += dg
        dzkv, dg = _rms_bwd(zkv, gkv_ref[...], dkvn)
        dgkv_ref[...] += dg
        dgqn_ref[...] += dgqn
        dgkn_ref[...] += dgkn
        dz_ref[:, :Q_LORA] = dzq.astype(dz_ref.dtype)
        dz_ref[:, Q_LORA:Q_LORA + KV_LORA] = dzkv.astype(dz_ref.dtype)
        dz_ref[:, Q_LORA + KV_LORA:] = dkrp.astype(dz_ref.dtype)

    whole2 = lambda arr: pl.BlockSpec(arr.shape, lambda i: (0, 0))
    rope_spec = pl.BlockSpec((tm, HEAD_PAD), lambda i: (i % per_seq, 0))
    head_spec = pl.BlockSpec((1, N_HEADS, tm, HEAD_PAD), lambda i: (i // per_seq, 0, i % per_seq, 0))
    row_spec = pl.BlockSpec((tm, 512), lambda i: (i, 0))
    return pl.pallas_call(
        body, name="mla_prep_bwd", grid=(n_rows // tm,),
        in_specs=[row_spec, head_spec, head_spec, head_spec, whole2(wuq), whole2(wukv),
                  whole2(gq), whole2(gkv), whole2(gqn), whole2(gkn), rope_spec, rope_spec],
        out_specs=[row_spec, whole2(wuq), whole2(wukv), whole2(gq), whole2(gkv), whole2(gqn), whole2(gkn)],
        out_shape=[jax.ShapeDtypeStruct((n_rows, 512), BF16),
                   jax.ShapeDtypeStruct(wuq.shape, F32), jax.ShapeDtypeStruct(wukv.shape, F32),
                   jax.ShapeDtypeStruct(gq.shape, F32), jax.ShapeDtypeStruct(gkv.shape, F32),
                   jax.ShapeDtypeStruct(gqn.shape, F32), jax.ShapeDtypeStruct(gkn.shape, F32)],
        compiler_params=_params(("arbitrary",)),
    )(zsm, dq, dk, dv, wuq, wukv, gq, gkv, gqn, gkn, *rope)


HBM_SPEC = pl.BlockSpec(memory_space=pltpu.HBM)


def _xchg_out_shapes(bufs):
    return [jax.ShapeDtypeStruct((N_DEV,) + (a.shape if gather else a.shape[1:]), a.dtype) for a, gather in bufs]


def _xchg_scratch(n_buf):
    return [pltpu.SemaphoreType.DMA((n_buf * (N_DEV - 1),)), pltpu.SemaphoreType.DMA((n_buf * (N_DEV - 1),)),
            pltpu.SemaphoreType.DMA((n_buf,))]


def _xchg_copies(src_refs, dst_refs, gathers, send_sems, recv_sems, local_sems):
    x, y, c = lax.axis_index("x"), lax.axis_index("y"), lax.axis_index("c")
    me = 4 * x + 2 * y + c
    local, starts, arrivals = [], [], []
    for bi, (src, dst, gather) in enumerate(zip(src_refs, dst_refs, gathers)):
        local.append(pltpu.make_async_copy(src if gather else src.at[me], dst.at[me], local_sems.at[bi]))
        for kk in range(1, N_DEV):
            px = 1 - x if kk & 4 else x
            py = 1 - y if kk & 2 else y
            pc = 1 - c if kk & 1 else c
            pid = 4 * px + 2 * py + pc
            sem = bi * (N_DEV - 1) + kk - 1
            starts.append(pltpu.make_async_remote_copy(
                src_ref=src if gather else src.at[pid], dst_ref=dst.at[me],
                send_sem=send_sems.at[sem], recv_sem=recv_sems.at[sem],
                device_id=(px, py, pc), device_id_type=pl.DeviceIdType.MESH))
            arrivals.append(pltpu.make_async_remote_copy(
                src_ref=src if gather else src.at[me], dst_ref=dst.at[pid],
                send_sem=send_sems.at[sem], recv_sem=recv_sems.at[sem],
                device_id=(px, py, pc), device_id_type=pl.DeviceIdType.MESH))
    return local, starts, arrivals


def _xchg_start(copies):
    local, sends, _ = copies
    for cp in local + sends:
        cp.start()


def _xchg_finish(copies):
    local, sends, arrivals = copies
    for cp in arrivals:
        cp.wait_recv()
    for cp in sends:
        cp.wait_send()
    for cp in local:
        cp.wait()


def _gather_by_chip(src_refs, dst_refs, send_sems, recv_sems, local_sems, start=True, finish=True):
    x, y, c = lax.axis_index("x"), lax.axis_index("y"), lax.axis_index("c")
    me = 4 * x + 2 * y + c
    sibling = (x, y, 1 - c)

    def place(kk):
        px = 1 - x if kk & 4 else x
        py = 1 - y if kk & 2 else y
        pc = 1 - c if kk & 1 else c
        return (px, py, pc), 4 * px + 2 * py + pc

    def copy(bi, kk, src, dst, to):
        sem = bi * (N_DEV - 1) + kk - 1
        return pltpu.make_async_remote_copy(src_ref=src, dst_ref=dst, send_sem=send_sems.at[sem],
                                            recv_sem=recv_sems.at[sem], device_id=to, device_id_type=pl.DeviceIdType.MESH)

    local, sends = [], []
    for bi, (src, dst) in enumerate(zip(src_refs, dst_refs)):
        local.append(pltpu.make_async_copy(src, dst.at[me], local_sems.at[bi]))
        sends += [copy(bi, kk, src, dst.at[me], place(kk)[0]) for kk in (1, 2, 4, 6)]
    if start:
        for cp in local + sends:
            cp.start()
    if not finish:
        return
    for kk in (2, 4, 6):
        for bi, (src, dst) in enumerate(zip(src_refs, dst_refs)):
            dev, pid = place(kk)
            copy(bi, kk, src, dst.at[pid], dev).wait_recv()
            passed = copy(bi, kk | 1, dst.at[pid], dst.at[pid], sibling)
            passed.start()
            sends.append(passed)
    for kk in (1, 3, 5, 7):
        for bi, (src, dst) in enumerate(zip(src_refs, dst_refs)):
            dev, pid = place(kk)
            copy(bi, kk, src, dst.at[pid], sibling).wait_recv()
    for cp in sends:
        cp.wait_send()
    for cp in local:
        cp.wait()


def _exchange(name, bufs, by_chip=False):
    n_buf = len(bufs)
    gathers = [g for _, g in bufs]
    assert not by_chip or all(gathers)

    def body(*refs):
        srcs, dsts = refs[:n_buf], refs[n_buf:2 * n_buf]
        if by_chip:
            _gather_by_chip(srcs, dsts, *refs[2 * n_buf:])
            return
        copies = _xchg_copies(srcs, dsts, gathers, *refs[2 * n_buf:])
        _xchg_start(copies)
        _xchg_finish(copies)

    return pl.pallas_call(
        body, name=name, out_shape=_xchg_out_shapes(bufs),
        in_specs=[HBM_SPEC] * n_buf, out_specs=[HBM_SPEC] * n_buf, scratch_shapes=_xchg_scratch(n_buf),
    )(*[a for a, _ in bufs])


def _chunk_mask(t, keys_first):
    key = lax.broadcasted_iota(jnp.int32, (t, t), 0 if keys_first else 1) // CHUNK
    query = lax.broadcasted_iota(jnp.int32, (t, t), 1 if keys_first else 0) // CHUNK
    return query >= key


def _grid_ends(grid):
    ids = [pl.program_id(ax) for ax in range(len(grid))]
    first = functools.reduce(jnp.logical_and, [i == 0 for i in ids])
    last = functools.reduce(jnp.logical_and, [i == g - 1 for i, g in zip(ids, grid)])
    return first, last


def _attn_fwd(q, k, v, bufs, n_b, seq):
    tq = min(ATT_TILE, seq)
    nq = seq // tq
    nt_dims = _DIMS["nt"]
    hpb = ATT_HEADS
    grid = (n_b, N_HEADS // hpb, nq)
    n_buf = len(bufs)
    gathers = [g for _, g in bufs]
    sum_lane = [HEAD_PAD - 1 if hh % 2 == 0 else 0 for hh in range(hpb)]

    def body(q_ref, k_ref, v_ref, *rest):
        srcs, (o_ref, lse_ref), dsts = rest[:n_buf], rest[n_buf:n_buf + 2], rest[n_buf + 2:2 * n_buf + 2]
        gather = functools.partial(_gather_by_chip, srcs, dsts, *rest[2 * n_buf + 2:])
        first, last = _grid_ends(grid)
        pl.when(first)(functools.partial(gather, start=True, finish=False))

        qi = pl.program_id(2)
        mask = _chunk_mask(tq, keys_first=False)
        lane_row = lax.broadcasted_iota(jnp.int32, (1, HEAD_PAD), 1)
        ones = [(lane_row == sum_lane[hh]).astype(BF16) for hh in range(hpb)]
        qs = [q_ref[0, hh] for hh in range(hpb)]

        def step(j, carry, masked):
            rows = pl.ds(pl.multiple_of(j * tq, tq), tq)
            out = []
            for hh in range(hpb):
                m, acc = carry[hh]
                s = lax.dot_general(qs[hh], k_ref[0, hh, rows, :], nt_dims, preferred_element_type=F32)
                if masked:
                    s = jnp.where(mask, s, NEG_BIG)
                m_new = jnp.maximum(m, jnp.max(s, axis=-1, keepdims=True))
                p = jnp.exp2(s - m_new).astype(BF16)
                acc = jnp.exp2(m - m_new) * acc + jnp.dot(p, v_ref[0, hh, rows, :] + ones[hh], preferred_element_type=F32)
                out.append((m_new, acc))
            return tuple(out)

        init = tuple((jnp.full((tq, 1), NEG_BIG, F32), jnp.zeros((tq, HEAD_PAD), F32)) for _ in range(hpb))
        online = lax.fori_loop(0, qi, functools.partial(step, masked=False), init)
        online = step(qi, online, True)
        ms, accs = [c[0] for c in online], [c[1] for c in online]
        carry = list(zip(ms, accs))
        lane = lax.broadcasted_iota(jnp.int32, (tq, HEAD_PAD), 1)
        for pair in range(hpb // 2):
            outs = []
            for hh in (2 * pair, 2 * pair + 1):
                m, acc = carry[hh]
                l = jnp.sum(jnp.where(lane == sum_lane[hh], acc, 0.0), axis=-1, keepdims=True)
                outs.append(acc * (1.0 / l))
                lse_ref[0, hh] = jnp.broadcast_to(m + jnp.log2(l), (tq, HEAD_PAD)).T[0:8, :]
            o_ref[0, :, pair * HEAD_PAD:(pair + 1) * HEAD_PAD] = jnp.where(lane < V_HEAD_DIM, outs[0], outs[1]).astype(BF16)

        pl.when(last)(functools.partial(gather, start=False, finish=True))

    kv_spec = pl.BlockSpec((1, hpb, seq, HEAD_PAD), lambda b, hb, i: (b, hb, 0, 0))
    q_spec = pl.BlockSpec((1, hpb, tq, HEAD_PAD), lambda b, hb, i: (b, hb, i, 0))
    res = pl.pallas_call(
        body, name="attn_fwd", grid=grid,
        in_specs=[q_spec, kv_spec, kv_spec] + [HBM_SPEC] * n_buf,
        out_specs=[pl.BlockSpec((1, tq, hpb * V_HEAD_DIM), lambda b, hb, i: (b, i, hb)),
                   pl.BlockSpec((1, hpb, 8, tq), lambda b, hb, i: (b, hb, 0, i))] + [HBM_SPEC] * n_buf,
        out_shape=[jax.ShapeDtypeStruct((n_b, seq, N_HEADS * V_HEAD_DIM), BF16),
                   jax.ShapeDtypeStruct((n_b, N_HEADS, 8, seq), F32)] + _xchg_out_shapes(bufs),
        scratch_shapes=_xchg_scratch(n_buf),
        compiler_params=_params(("arbitrary", "arbitrary", "arbitrary")),
    )(q, k, v, *[a for a, _ in bufs])
    return res[0], res[1], res[2:]


def _attn_bwd(q, k, v, kt, do, o, lse, bufs, n_b, seq):
    tq = min(ATT_TILE, seq)
    nq = seq // tq
    nt_dims = _DIMS["nt"]
    hpb = ATT_HEADS
    grid = (n_b, N_HEADS // hpb, nq)
    n_buf = len(bufs)
    gathers = [g for _, g in bufs]

    def body(q_ref, k_ref, v_ref, kt_ref, do_ref, o_ref, lse_ref, *rest):
        srcs, (dq_ref, dk_ref, dv_ref), dsts = rest[:n_buf], rest[n_buf:n_buf + 3], rest[n_buf + 3:2 * n_buf + 3]
        dk_acc, dv_acc = rest[2 * n_buf + 3:2 * n_buf + 5]
        copies = _xchg_copies(srcs, dsts, gathers, *rest[2 * n_buf + 5:])
        first, last = _grid_ends(grid)
        pl.when(first)(functools.partial(_xchg_start, copies))

        qi = pl.program_id(2)

        @pl.when(qi == 0)
        def _():
            dk_acc[...] = jnp.zeros_like(dk_acc)
            dv_acc[...] = jnp.zeros_like(dv_acc)

        mask = _chunk_mask(tq, keys_first=True)
        lane = lax.broadcasted_iota(jnp.int32, (tq, HEAD_PAD), 1)
        qs, dos, deltas, lses = [], [], [], []
        for hh in range(hpb):
            cols = slice((hh // 2) * HEAD_PAD, (hh // 2 + 1) * HEAD_PAD)
            do_pair = do_ref[0, :, cols]
            prod = do_pair.astype(F32) * o_ref[0, :, cols].astype(F32)
            delta = jnp.sum(jnp.where(lane // V_HEAD_DIM == hh % 2, prod, 0.0), axis=-1, keepdims=True)
            qs.append(q_ref[0, hh])
            dos.append(do_pair)
            deltas.append(jnp.broadcast_to(delta, (tq, HEAD_PAD)).T[0:1, :])
            lses.append(lse_ref[0, hh][0:1, :])

        def step(j, dqs, masked):
            rows = pl.ds(pl.multiple_of(j * tq, tq), tq)
            out = []
            for hh in range(hpb):
                s = lax.dot_general(k_ref[0, hh, rows, :], qs[hh], nt_dims, preferred_element_type=F32)
                p = jnp.exp2(s - lses[hh])
                if masked:
                    p = jnp.where(mask, p, 0.0)
                dv_acc[hh, rows, :] += jnp.dot(p.astype(BF16), dos[hh], preferred_element_type=F32)
                dp = lax.dot_general(v_ref[0, hh, rows, :], dos[hh], nt_dims, preferred_element_type=F32)
                ds = (p * (dp - deltas[hh])).astype(BF16)
                dk_acc[hh, rows, :] += jnp.dot(ds, qs[hh], preferred_element_type=F32)
                out.append(dqs[hh] + jnp.dot(kt_ref[0, hh, j], ds, preferred_element_type=F32))
            return tuple(out)

        dqs = tuple(jnp.zeros((HEAD_PAD, tq), F32) for _ in range(hpb))
        dqs = lax.fori_loop(0, qi, functools.partial(step, masked=False), dqs)
        dqs = step(qi, dqs, True)
        for hh in range(hpb):
            dq_ref[0, hh] = dqs[hh].T.astype(BF16)

        @pl.when(qi == nq - 1)
        def _():
            dk_ref[0] = (dk_acc[...] * LN2).astype(BF16)
            dv_ref[0] = dv_acc[...].astype(BF16)

        pl.when(last)(functools.partial(_xchg_finish, copies))

    full_spec = pl.BlockSpec((1, hpb, seq, HEAD_PAD), lambda b, hb, i: (b, hb, 0, 0))
    t_spec = pl.BlockSpec((1, hpb, nq, HEAD_PAD, tq), lambda b, hb, i: (b, hb, 0, 0, 0))
    q_spec = pl.BlockSpec((1, hpb, tq, HEAD_PAD), lambda b, hb, i: (b, hb, i, 0))
    o_spec = pl.BlockSpec((1, tq, hpb * V_HEAD_DIM), lambda b, hb, i: (b, i, hb))
    lse_spec = pl.BlockSpec((1, hpb, 8, tq), lambda b, hb, i: (b, hb, 0, i))
    head_shape = jax.ShapeDtypeStruct((n_b, N_HEADS, seq, HEAD_PAD), BF16)
    res = pl.pallas_call(
        body, name="attn_bwd", grid=grid,
        in_specs=[q_spec, full_spec, full_spec, t_spec, o_spec, o_spec, lse_spec] + [HBM_SPEC] * n_buf,
        out_specs=[q_spec, full_spec, full_spec] + [HBM_SPEC] * n_buf,
        out_shape=[head_shape] * 3 + _xchg_out_shapes(bufs),
        scratch_shapes=[pltpu.VMEM((hpb, seq, HEAD_PAD), F32), pltpu.VMEM((hpb, seq, HEAD_PAD), F32)]
        + _xchg_scratch(n_buf),
        compiler_params=_params(("arbitrary", "arbitrary", "arbitrary")),
    )(q, k, v, kt, do, o, lse, *[a for a, _ in bufs])
    return res[0], res[1], res[2], res[3:]


def _in_proj_fwd(x2, scale, shift, g, w_parts, z_dtypes, seq):
    n_rows, d = x2.shape
    tm = min(512, seq)
    per_seq = seq // tm
    n_part = len(w_parts)
    nt_dims = _DIMS["nt"]

    def body(x_ref, sc_ref, sh_ref, g_ref, *rest):
        w_refs, h_ref, z_refs = rest[:n_part], rest[n_part], rest[n_part + 1:]
        h = _norm_mod(x_ref[...], g_ref[...], sc_ref[0], sh_ref[0]).astype(BF16)
        h_ref[...] = h
        for w_ref, z_ref in zip(w_refs, z_refs):
            z_ref[...] = lax.dot_general(h, w_ref[...], nt_dims, preferred_element_type=F32).astype(z_ref.dtype)

    row = lambda width: pl.BlockSpec((tm, width), lambda i: (i, 0))
    bat = pl.BlockSpec((1, 1, d), lambda i: (i // per_seq, 0, 0))
    whole = lambda arr: pl.BlockSpec(arr.shape, lambda i: (0, 0))
    return pl.pallas_call(
        body, name="in_proj_fwd", grid=(n_rows // tm,),
        in_specs=[row(d), bat, bat, whole(g)] + [whole(w) for w in w_parts],
        out_specs=[row(d)] + [row(w.shape[0]) for w in w_parts],
        out_shape=[jax.ShapeDtypeStruct((n_rows, d), BF16)]
        + [jax.ShapeDtypeStruct((n_rows, w.shape[0]), dt) for w, dt in zip(w_parts, z_dtypes)],
        compiler_params=_params(("parallel",)),
    )(x2, scale, shift, g, *w_parts)
def _in_proj_bwd(parts, x2, dx1, scale, g, bufs, seq):
    n_rows, d = x2.shape
    tm = min(512, seq)
    per_seq = seq // tm
    grid = (n_rows // tm,)
    n_part, n_buf = len(parts), len(bufs)
    gathers = [gt for _, gt in bufs]

    def body(*refs):
        dz_refs, w_refs = refs[:n_part], refs[n_part:2 * n_part]
        x_ref, dx1_ref, sc_ref, g_ref = refs[2 * n_part:2 * n_part + 4]
        srcs = refs[2 * n_part + 4:2 * n_part + 4 + n_buf]
        gx_ref, dsc_ref, dsh_ref, dg_ref = refs[2 * n_part + 4 + n_buf:2 * n_part + 8 + n_buf]
        dsts = refs[2 * n_part + 8 + n_buf:2 * n_part + 8 + 2 * n_buf]
        copies = _xchg_copies(srcs, dsts, gathers, *refs[2 * n_part + 8 + 2 * n_buf:])
        first, last = _grid_ends(grid)
        pl.when(first)(functools.partial(_xchg_start, copies))

        i = pl.program_id(0)
        dh = None
        for dz_ref, w_ref in zip(dz_refs, w_refs):
            term = jnp.dot(dz_ref[...], w_ref[...], preferred_element_type=F32)
            dh = term if dh is None else dh + term
        dx, dsc, dsh, dg = _norm_mod_bwd(x_ref[...], g_ref[...], sc_ref[0], dh)
        gx_ref[...] = dx1_ref[...] + dx

        @pl.when(i % per_seq == 0)
        def _():
            dsc_ref[...] = jnp.zeros_like(dsc_ref)
            dsh_ref[...] = jnp.zeros_like(dsh_ref)

        @pl.when(i == 0)
        def _():
            dg_ref[...] = jnp.zeros_like(dg_ref)

        dsc_ref[0] += dsc
        dsh_ref[0] += dsh
        dg_ref[...] += dg
        pl.when(last)(functools.partial(_xchg_finish, copies))

    row = lambda width: pl.BlockSpec((tm, width), lambda i: (i, 0))
    bat = pl.BlockSpec((1, 1, d), lambda i: (i // per_seq, 0, 0))
    whole = lambda arr: pl.BlockSpec(arr.shape, lambda i: (0, 0))
    n_b = n_rows // seq
    res = pl.pallas_call(
        body, name="in_proj_bwd", grid=grid,
        in_specs=[row(dz.shape[1]) for dz, _ in parts] + [whole(w) for _, w in parts]
        + [row(d), row(d), bat, whole(g)] + [HBM_SPEC] * n_buf,
        out_specs=[row(d), bat, bat, whole(g)] + [HBM_SPEC] * n_buf,
        out_shape=[jax.ShapeDtypeStruct((n_rows, d), F32), jax.ShapeDtypeStruct((n_b, 1, d), F32),
                   jax.ShapeDtypeStruct((n_b, 1, d), F32), jax.ShapeDtypeStruct(g.shape, F32)] + _xchg_out_shapes(bufs),
        scratch_shapes=_xchg_scratch(n_buf),
        compiler_params=_params(("arbitrary",)),
    )(*[dz for dz, _ in parts], *[w for _, w in parts], x2, dx1, scale, g, *[a for a, _ in bufs])
    return res[0], res[1], res[2], res[3], res[4:]


def _ln_silu(u1, g, b):
    mu = jnp.mean(u1, axis=-1, keepdims=True)
    uc = u1 - mu
    r = lax.rsqrt(jnp.mean(uc * uc, axis=-1, keepdims=True) + EPS)
    y = uc * r * g + b
    return y * _sigmoid(y)


def _conv_fill_glu(z_ref, u0_ref, seq, tile):
    u0_ref[0:CONV_HALO, :] = jnp.zeros((CONV_HALO, CONV_CH), F32)
    u0_ref[CONV_HALO + seq:CONV_HALO + seq + CONV_TAIL, :] = jnp.zeros((CONV_TAIL, CONV_CH), F32)
    for t in range(seq // tile):
        zt = z_ref[0, t * tile:(t + 1) * tile, :].astype(F32)
        u0_ref[CONV_HALO + t * tile:CONV_HALO + (t + 1) * tile, :] = zt[:, :CONV_CH] * _sigmoid(zt[:, CONV_CH:])


def _conv_windows(ref, views_ref, t, tile):
    for b in range(8):
        views_ref[b] = ref[t * tile + b:t * tile + b + tile + CONV_HALO, :]


def _conv_tap(views_ref, offset, tile):
    return views_ref[offset % 8, 8 * (offset // 8):8 * (offset // 8) + tile, :]


def _conv_tile(u0_ref, views_ref, w_ref, b_ref, t, tile):
    _conv_windows(u0_ref, views_ref, t, tile)
    acc = jnp.broadcast_to(b_ref[...], (tile, CONV_CH))
    for kk in range(CONV_WIDTH):
        acc = acc + w_ref[kk:kk + 1, :] * _conv_tap(views_ref, kk + CONV_HALO - (CONV_WIDTH - 1), tile)
    return acc


def _conv_fwd(zglu, conv_w, conv_b, ln_g, ln_b, n_b, seq):
    tile = min(256, seq)

    def body(z_ref, w_ref, b_ref, g_ref, bb_ref, o_ref, u1_ref, u0_ref, views_ref):
        _conv_fill_glu(z_ref, u0_ref, seq, tile)
        for t in range(seq // tile):
            u1 = _conv_tile(u0_ref, views_ref, w_ref, b_ref, t, tile)
            u1_ref[0, t * tile:(t + 1) * tile, :] = u1
            o_ref[0, t * tile:(t + 1) * tile, :] = _ln_silu(u1, g_ref[...], bb_ref[...]).astype(BF16)

    whole2 = lambda arr: pl.BlockSpec(arr.shape, lambda b: (0, 0))
    seq_spec = pl.BlockSpec((1, seq, CONV_CH), lambda b: (b, 0, 0))
    return pl.pallas_call(
        body, name="conv_fwd", grid=(n_b,),
        in_specs=[pl.BlockSpec((1, seq, 2 * CONV_CH), lambda b: (b, 0, 0)), whole2(conv_w), whole2(conv_b),
                  whole2(ln_g), whole2(ln_b)],
        out_specs=[seq_spec, seq_spec],
        out_shape=[jax.ShapeDtypeStruct((n_b, seq, CONV_CH), BF16), jax.ShapeDtypeStruct((n_b, seq, CONV_CH), F32)],
        scratch_shapes=[pltpu.VMEM((seq + CONV_HALO + CONV_TAIL, CONV_CH), F32),
                        pltpu.VMEM((8, tile + CONV_HALO, CONV_CH), F32)],
        compiler_params=_params(("parallel",)),
    )(zglu, conv_w, conv_b, ln_g, ln_b)


def _conv_bwd(zglu, u1_saved, du3, conv_w, ln_g, ln_b, n_b, seq):
    tile = min(256, seq)
    n_t = seq // tile

    def body(z_ref, u1_ref, du3_ref, w_ref, g_ref, bb_ref, dz_ref, dw_ref, db_ref, dg_ref, dbb_ref, u0_ref, du1_ref,
             u0_views, du1_views):
        @pl.when(pl.program_id(0) == 0)
        def _():
            for r in (dw_ref, db_ref, dg_ref, dbb_ref):
                r[...] = jnp.zeros_like(r)

        _conv_fill_glu(z_ref, u0_ref, seq, tile)
        du1_ref[seq:seq + CONV_HALO + CONV_TAIL, :] = jnp.zeros((CONV_HALO + CONV_TAIL, CONV_CH), F32)
        g = g_ref[...]
        for t in range(n_t):
            u1 = u1_ref[0, t * tile:(t + 1) * tile, :]
            mu = jnp.mean(u1, axis=-1, keepdims=True)
            uc = u1 - mu
            r = lax.rsqrt(jnp.mean(uc * uc, axis=-1, keepdims=True) + EPS)
            xh = uc * r
            y = xh * g + bb_ref[...]
            sg = _sigmoid(y)
            dy = du3_ref[0, t * tile:(t + 1) * tile, :].astype(F32) * (sg * (1.0 + y * (1.0 - sg)))
            dg_ref[...] += jnp.sum(dy * xh, axis=0, keepdims=True)
            dbb_ref[...] += jnp.sum(dy, axis=0, keepdims=True)
            dxh = dy * g
            du1 = r * (dxh - jnp.mean(dxh, axis=-1, keepdims=True) - xh * jnp.mean(dxh * xh, axis=-1, keepdims=True))
            db_ref[...] += jnp.sum(du1, axis=0, keepdims=True)
            du1_ref[t * tile:(t + 1) * tile, :] = du1
        for t in range(n_t):
            du1 = du1_ref[t * tile:(t + 1) * tile, :]
            du0 = jnp.zeros((tile, CONV_CH), F32)
            _conv_windows(u0_ref, u0_views, t, tile)
            _conv_windows(du1_ref, du1_views, t, tile)
            for kk in range(CONV_WIDTH):
                du0 = du0 + w_ref[kk:kk + 1, :] * _conv_tap(du1_views, CONV_WIDTH - 1 - kk, tile)
                u0_tap = _conv_tap(u0_views, kk + CONV_HALO - (CONV_WIDTH - 1), tile)
                dw_ref[kk:kk + 1, :] += jnp.sum(du1 * u0_tap, axis=0, keepdims=True)
            zt = z_ref[0, t * tile:(t + 1) * tile, :].astype(F32)
            ga, sb = zt[:, :CONV_CH], _sigmoid(zt[:, CONV_CH:])
            dz_ref[0, t * tile:(t + 1) * tile, :CONV_CH] = (du0 * sb).astype(BF16)
            dz_ref[0, t * tile:(t + 1) * tile, CONV_CH:] = (du0 * ga * sb * (1.0 - sb)).astype(BF16)

    whole2 = lambda arr: pl.BlockSpec(arr.shape, lambda b: (0, 0))
    z_spec = pl.BlockSpec((1, seq, 2 * CONV_CH), lambda b: (b, 0, 0))
    seq_spec = pl.BlockSpec((1, seq, CONV_CH), lambda b: (b, 0, 0))
    return pl.pallas_call(
        body, name="conv_bwd", grid=(n_b,),
        in_specs=[z_spec, seq_spec, seq_spec, whole2(conv_w), whole2(ln_g), whole2(ln_b)],
        out_specs=[z_spec, whole2(conv_w), whole2(ln_g), whole2(ln_g), whole2(ln_b)],
        out_shape=[jax.ShapeDtypeStruct((n_b, seq, 2 * CONV_CH), BF16), jax.ShapeDtypeStruct(conv_w.shape, F32),
                   jax.ShapeDtypeStruct(ln_g.shape, F32), jax.ShapeDtypeStruct(ln_g.shape, F32),
                   jax.ShapeDtypeStruct(ln_b.shape, F32)],
        scratch_shapes=[pltpu.VMEM((seq + CONV_HALO + CONV_TAIL, CONV_CH), F32)] * 2
        + [pltpu.VMEM((8, tile + CONV_HALO, CONV_CH), F32)] * 2,
        compiler_params=_params(("arbitrary",)),
    )(zglu, u1_saved, du3, conv_w, ln_g, ln_b)


def _sum_parts(name, parts):
    n_parts = parts.shape[0]

    def body(p_ref, o_ref):
        gg = p_ref[0].astype(F32)
        for j in range(1, n_parts):
            gg = gg + p_ref[j].astype(F32)
        o_ref[...] = gg

    return pl.pallas_call(body, name=name, out_shape=jax.ShapeDtypeStruct(parts.shape[1:], F32),
                          compiler_params=_params(None))(parts)


def _adamw(name, w, parts, m, v, transposed=False):
    n_parts = parts.shape[0]
    rows, cols = w.shape
    tr = ADAM_ROWS if rows % ADAM_ROWS == 0 else rows

    def body(w_ref, p_ref, m_ref, v_ref, g_ref, d_ref, nm_ref, nv_ref):
        gg = p_ref[0].astype(F32)
        for j in range(1, n_parts):
            gg = gg + p_ref[j].astype(F32)
        if transposed:
            gg = gg.T
        nm = ADAM_B1 * m_ref[...] + (1.0 - ADAM_B1) * gg
        nv = ADAM_B2 * v_ref[...] + (1.0 - ADAM_B2) * jnp.square(gg)
        m_hat = nm / (1.0 - ADAM_B1 ** ADAM_STEP)
        v_hat = nv / (1.0 - ADAM_B2 ** ADAM_STEP)
        g_ref[...] = gg
        d_ref[...] = -ADAM_LR * (m_hat / (jnp.sqrt(v_hat) + ADAM_EPS) + ADAM_WD * w_ref[...])
        nm_ref[...] = nm
        nv_ref[...] = nv

    shape = jax.ShapeDtypeStruct(w.shape, F32)
    blk = pl.BlockSpec((tr, cols), lambda i: (i, 0))
    p_spec = (pl.BlockSpec((n_parts, cols, tr), lambda i: (0, 0, i)) if transposed
              else pl.BlockSpec((n_parts, tr, cols), lambda i: (0, i, 0)))
    return pl.pallas_call(body, name=name, grid=(rows // tr,), in_specs=[blk, p_spec, blk, blk], out_specs=[blk] * 4,
                          out_shape=[shape] * 4, compiler_params=_params(("parallel",)))(w, parts, m, v)


def _rope_tables(seq):
    inv_freq = ROPE_THETA ** (-jnp.arange(0, QK_ROPE_DIM, 2, dtype=F32) / QK_ROPE_DIM)
    ang = jnp.arange(seq, dtype=F32)[:, None] * inv_freq[None, :]
    cos, sin = jnp.cos(ang), jnp.sin(ang)
    half = QK_ROPE_DIM // 2
    lane_half = HEAD_PAD // 2
    one = lambda n: jnp.ones((seq, n), F32)
    z = lambda n: jnp.zeros((seq, n), F32)
    used_hi = QK_HEAD_DIM - lane_half - half
    cos_t = jnp.concatenate([cos, one(lane_half - half), cos, one(used_hi), z(lane_half - half - used_hi)], axis=1)
    sin_t = jnp.concatenate([-sin, z(lane_half - half), sin, z(lane_half - half)], axis=1)
    return cos_t, sin_t


def _pad_lanes(v, width=HEAD_PAD):
    return jnp.pad(v, [(0, 0)] * (v.ndim - 1) + [(0, width - v.shape[-1])])


_LANE_HALF_NOPE = HEAD_PAD // 2 - QK_ROPE_DIM // 2


def _head_lanes(v):
    rot = v[..., QK_NOPE_DIM:]
    half = QK_ROPE_DIM // 2
    return _pad_lanes(jnp.concatenate([rot[..., :half], v[..., :_LANE_HALF_NOPE], rot[..., half:],
                                       v[..., _LANE_HALF_NOPE:QK_NOPE_DIM]], axis=-1))


def _head_dims(g):
    half = QK_ROPE_DIM // 2
    lane_half = HEAD_PAD // 2
    return jnp.concatenate([g[..., half:lane_half], g[..., lane_half + half:QK_HEAD_DIM], g[..., :half],
                            g[..., lane_half:lane_half + half]], axis=-1)


def _unstack_cols(s):
    return s.transpose(1, 0, 2).reshape(s.shape[1], N_DEV * s.shape[2])


def _stack_cols(g, dtype):
    rows, cols = g.shape
    return g.reshape(rows, N_DEV, cols // N_DEV).transpose(1, 0, 2).astype(dtype)


def kernel(x, c, w_ada, b_ada, norm1_g, w_in, q_latent_g, w_uq, kv_latent_g, w_ukv, qk_norm_q_g, qk_norm_k_g, w_o_mla, conv_w, conv_b, conv_ln_g, conv_ln_b, w_pw_out, w_out, norm2_g, w_ff1, w_ff2, loss_target, m_w_ada, m_b_ada, m_norm1_g, m_w_in, m_q_latent_g, m_w_uq, m_kv_latent_g, m_w_ukv, m_qk_norm_q_g, m_qk_norm_k_g, m_w_o_mla, m_conv_w, m_conv_b, m_conv_ln_g, m_conv_ln_b, m_w_pw_out, m_w_out, m_norm2_g, m_w_ff1, m_w_ff2, v_w_ada, v_b_ada, v_norm1_g, v_w_in, v_q_latent_g, v_w_uq, v_kv_latent_g, v_w_ukv, v_qk_norm_q_g, v_qk_norm_k_g, v_w_o_mla, v_conv_w, v_conv_b, v_conv_ln_g, v_conv_ln_b, v_w_pw_out, v_w_out, v_norm2_g, v_w_ff1, v_w_ff2):
    given = dict(locals())
    local = {n: given[n][0] for n in WEIGHTS}
    vec = {n: local[n].reshape(1, -1) for n in REPLICATED}
    bf = lambda n: local[n].astype(BF16)
    n_b, seq, d = x.shape
    n_rows = n_b * seq
    x2 = x.reshape(n_rows, d)
    t2 = loss_target.reshape(n_rows, d)
    me = 4 * lax.axis_index("x") + 2 * lax.axis_index("y") + lax.axis_index("c")
    ada_cols = local["w_ada"].shape[1]

    tsh = lambda n: local[n].T.astype(BF16)
    c_all, w_in_s, w_uq_s, w_ukv_s, conv_w_s = _exchange(
        "gather_early", [(c, True), (tsh("w_in"), True), (bf("w_uq"), True), (bf("w_ukv"), True), (local["conv_w"], True)],
        by_chip=True)
    w_in_t = w_in_s.reshape(-1, d)
    zrows = lambda n: jnp.zeros((n, d), BF16)
    rot_half = QK_ROPE_DIM // 2
    w_sm_t = jnp.concatenate([w_in_t[:OFF_KV + rot_half], zrows(HEAD_PAD // 2 - rot_half), w_in_t[OFF_KV + rot_half:OFF_KR],
                              zrows(HEAD_PAD // 2 - rot_half)], axis=0)
    w_glu_t = w_in_t[OFF_KR:OFF_GLU]
    w_gate_t = w_in_t[OFF_GLU:]
    wuq = _head_lanes(_unstack_cols(w_uq_s).reshape(Q_LORA, N_HEADS, QK_HEAD_DIM)).reshape(Q_LORA, N_HEADS * HEAD_PAD)
    wukv_f = _unstack_cols(w_ukv_s).reshape(KV_LORA, N_HEADS, QK_NOPE_DIM + V_HEAD_DIM)
    wv = wukv_f[:, :, QK_NOPE_DIM:]
    odd = (jnp.arange(N_HEADS) % 2 == 1)[None, :, None]
    wuv = jnp.where(odd, jnp.pad(wv, ((0, 0), (0, 0), (V_HEAD_DIM, 0))), jnp.pad(wv, ((0, 0), (0, 0), (0, V_HEAD_DIM))))
    wuk = _head_lanes(_pad_lanes(wukv_f[:, :, :QK_NOPE_DIM], QK_HEAD_DIM))
    wukv = jnp.concatenate([wuk, wuv], axis=1).reshape(KV_LORA, 2 * N_HEADS * HEAD_PAD)
    gqn = _head_lanes(vec["qk_norm_q_g"])
    gkn = _head_lanes(vec["qk_norm_k_g"])
    conv_w_f = jnp.pad(_unstack_cols(conv_w_s), ((0, 1), (0, 0)))
    rope = _rope_tables(seq)

    all_rows = N_DEV * n_b
    pad_rows = (-all_rows) % ROWS_PAD
    c_rows = jnp.pad(c_all.reshape(all_rows, d), ((0, pad_rows), (0, 0)))
    b_cols = lax.dynamic_slice(local["b_ada"], (me * ada_cols,), (ada_cols,))
    mod_cols = _mm("ada_fwd", c_rows, local["w_ada"], "nn", F32, a_fn=_silu, epi=lambda acc, b: acc + b,
                   epi_in=(jnp.broadcast_to(b_cols, (all_rows + pad_rows, ada_cols)),))
    (mod_s,) = _exchange("scatter_mod", [(mod_cols[:all_rows].reshape(N_DEV, n_b, ada_cols), False)])
    mod = mod_s.transpose(1, 0, 2).reshape(n_b, ADA_CHUNKS, 1, d)
    shift1, scale1, gate1, shift2, scale2, gate2 = [mod[:, i] for i in range(ADA_CHUNKS)]

    h, zgate, zglu, zsm = _in_proj_fwd(x2, scale1, shift1, vec["norm1_g"], [w_gate_t, w_glu_t, w_sm_t],
                                       [BF16, BF16, F32], seq)
    q, k, v, kt = _mla_prep_fwd(zsm, wuq, wukv, vec["q_latent_g"], vec["kv_latent_g"], gqn, gkn, rope, n_b, seq)
    attn, lse, (w_o_s, w_pw_s, w_out_s, w_ff1_s, w_ff2_s) = _attn_fwd(
        q, k, v, [(tsh("w_o_mla"), True), (tsh("w_pw_out"), True), (bf("w_out"), True), (tsh("w_ff1"), True),
                  (bf("w_ff2"), True)], n_b, seq)
    w_o_t = w_o_s.reshape(d, -1)
    w_pw_t = w_pw_s.reshape(d, -1)
    w_out_f = w_out_s.reshape(d, d)
    w_ff1_t = w_ff1_s.reshape(-1, d)
    w_ff2_f = w_ff2_s.reshape(-1, d)
    attn2 = attn.reshape(n_rows, N_HEADS * V_HEAD_DIM)
    u3, u1 = _conv_fwd(zglu.reshape(n_b, seq, 2 * CONV_CH), conv_w_f, vec["conv_b"], vec["conv_ln_g"], vec["conv_ln_b"], n_b, seq)
    u32 = u3.reshape(n_rows, CONV_CH)
    ya = _mm("mla_out", attn2, w_o_t, "nt", BF16)
    yb = _mm("conv_out", u32, w_pw_t, "nt", BF16)
    mmr = functools.partial(_mm_rows, n_rows=n_rows, seq=seq)

    def merge_fn(t):
        return _sigmoid(t[0]) * t[2] + _sigmoid(t[1]) * t[3]

    def mid_fn(acc, r, b, cc):
        x1_ = r[0] + b[0] * acc
        return [acc, x1_, _norm_mod(x1_, cc[0], b[1], b[2])], [], []

    mrg, mixed, x1, h2 = mmr("out_proj", [(zgate, d, 0), (zgate, d, 1), (ya, d, 0), (yb, d, 0)], merge_fn, w_out_f, "nn",
                             mid_fn, rows=[_full(x2)], bats=[gate1, scale2, shift2], consts=[vec["norm2_g"]],
                             outs=[(d, BF16), (d, F32), (d, BF16)], a_out=BF16)

    a = _mm("ff1", h2, w_ff1_t, "nt", BF16)

    def loss_fn(ff, r, b, cc):
        err = r[0] + b[0] * ff - r[1]
        dy_ = err * (1.0 / d)
        sq = jnp.broadcast_to(jnp.sum(err * err, keepdims=True), (1, LANES))
        return [dy_, b[0] * dy_], [jnp.sum(dy_ * ff, axis=0, keepdims=True)], [sq]

    dy, df, dgate2, sq_err = mmr("ff2_loss", [(a, a.shape[1], 0)], lambda t: _relu2(t[0]), w_ff2_f, "nn", loss_fn,
                                 rows=[_full(x1), _full(t2)], bats=[gate2], outs=[(d, F32), (d, BF16)], bat_outs=[d],
                                 tot_outs=[(1, LANES)], tk=a.shape[1])

    da = _mm("ff2_bwd", df, w_ff2_f, "nt", BF16, epi=lambda acc, av: acc * 2.0 * jnp.maximum(av, 0.0), epi_in=(a,))
    g_ff2 = _mm("ff2_dw", a, df, "tn", BF16, a_fn=_relu2)
    g_ff1_t = _mm("ff1_dw", da, h2, "tn", BF16)

    def mid_bwd(dh2_, r, b, cc):
        dx, dsc, dsh, dg = _norm_mod_bwd(r[0], cc[0], b[0], dh2_)
        dx1_ = r[1] + dx
        return [dx1_, b[1] * dx1_], [dsc, dsh, jnp.sum(dx1_ * r[2].astype(F32), axis=0, keepdims=True)], [dg]

    dx1, dmixed, dscale2, dshift2, dgate1, g_norm2 = mmr(
        "ff1_bwd", [(da, da.shape[1], 0)], None, w_ff1_t, "nn", mid_bwd, rows=[_full(x1), _full(dy), _full(mixed)],
        bats=[scale2, gate1], consts=[vec["norm2_g"]], outs=[(d, F32), (d, BF16)], bat_outs=[d, d, d],
        tot_outs=[(1, d)], tk=da.shape[1])

    g_out = _mm("out_proj_dw", mrg, dmixed, "tn", BF16)

    def merge_bwd(dm, r, b, cc):
        ya_, yb_ = r[2].astype(F32), r[3].astype(F32)
        sa, sb = _sigmoid(r[0].astype(F32)), _sigmoid(r[1].astype(F32))
        return [dm * ya_ * sa * (1.0 - sa), dm * yb_ * sb * (1.0 - sb), dm * sa, dm * sb], [], []

    dzga, dzgb, dya, dyb = mmr("out_proj_bwd", [(dmixed, d, 0)], None, w_out_f, "nt", merge_bwd,
                               rows=[(zgate, d, 0), (zgate, d, 1), _full(ya), _full(yb)], outs=[(d, BF16)] * 4)
    dattn = _mm("mla_out_bwd", dya, w_o_t, "nn", BF16)
    g_o_t = _mm("mla_out_dw", dya, attn2, "tn", BF16)
    du3 = _mm("conv_out_bwd", dyb, w_pw_t, "nn", BF16)
    g_pw_t = _mm("conv_out_dw", dyb, u32, "tn", BF16)

    dzglu, g_conv_w, g_conv_b, g_ln_g, g_ln_b = _conv_bwd(
        zglu.reshape(n_b, seq, 2 * CONV_CH), u1, du3.reshape(n_b, seq, CONV_CH), conv_w_f, vec["conv_ln_g"],
        vec["conv_ln_b"], n_b, seq)
    dzglu = dzglu.reshape(n_rows, 2 * CONV_CH)

    dq, dk, dv, (p_ff2, p_ff1, p_out, p_pw, p_o) = _attn_bwd(
        q, k, v, kt, dattn.reshape(n_b, seq, N_HEADS * V_HEAD_DIM), attn, lse,
        [(g_ff2.reshape(N_DEV, -1, d), False), (g_ff1_t.reshape(N_DEV, -1, d), False), (g_out.reshape(N_DEV, -1, d), False),
         (g_pw_t.reshape(N_DEV, -1, CONV_CH), False), (g_o_t.reshape(N_DEV, -1, N_HEADS * V_HEAD_DIM), False)], n_b, seq)
    dzsm, g_wuq, g_wukv, g_gq, g_gkv, g_gqn, g_gkn = _mla_prep_bwd(
        zsm, dq, dk, dv, wuq, wukv, vec["q_latent_g"], vec["kv_latent_g"], gqn, gkn, rope, n_b, seq)

    g_gate_a_t = _mm("in_proj_gate_dw_a", dzga, h, "tn", BF16)
    g_gate_b_t = _mm("in_proj_gate_dw_b", dzgb, h, "tn", BF16)
    g_glu_t = _mm("in_proj_glu_dw", dzglu, h, "tn", BF16)
    g_sm_t = _mm("in_proj_sm_dw", dzsm, h, "tn", BF16)
    g_in_t = jnp.concatenate([g_sm_t[:OFF_KV + rot_half], g_sm_t[OFF_KV + HEAD_PAD // 2:OFF_KV + HEAD_PAD // 2 + rot_half],
                              g_glu_t, g_gate_a_t, g_gate_b_t], axis=0)
    g_uq = _head_dims(g_wuq.reshape(Q_LORA, N_HEADS, HEAD_PAD)).reshape(Q_LORA, N_HEADS * QK_HEAD_DIM)
    g_wukv = g_wukv.reshape(KV_LORA, 2, N_HEADS, HEAD_PAD)
    g_v = jnp.where(odd, g_wukv[:, 1, :, V_HEAD_DIM:], g_wukv[:, 1, :, :V_HEAD_DIM])
    g_ukv = jnp.concatenate([_head_dims(g_wukv[:, 0])[:, :, :QK_NOPE_DIM], g_v], axis=2).reshape(KV_LORA, -1)

    grad_x, dscale1, dshift1, g_norm1, (p_in, p_uq, p_ukv, p_conv_w) = _in_proj_bwd(
        [(dzga, w_gate_t[:d]), (dzgb, w_gate_t[d:]), (dzglu, w_glu_t), (dzsm, w_sm_t)], x2, dx1, scale1, vec["norm1_g"],
        [(g_in_t.reshape(N_DEV, -1, d), False), (_stack_cols(g_uq, BF16), False), (_stack_cols(g_ukv, BF16), False),
         (_stack_cols(g_conv_w[:CONV_WIDTH], F32), False)], seq)

    dmod = jnp.concatenate([dshift1, dscale1, dgate1, dshift2, dscale2, dgate2], axis=1).reshape(n_b, N_DEV, ada_cols)
    (dmod_s,) = _exchange("scatter_dmod", [(dmod.transpose(1, 0, 2), False)])
    dmod_rows = jnp.pad(dmod_s.reshape(all_rows, ada_cols), ((0, pad_rows), (0, 0)))
    g_ada = _mm("ada_dw", c_rows, dmod_rows, "tn", F32, a_fn=_silu)
    (g_b_cols,) = _rowwise("ada_db", lambda r, b, cc: ([], [], [jnp.sum(r[0], axis=0, keepdims=True)]),
                           all_rows + pad_rows, all_rows + pad_rows, rows=[_full(dmod_rows)], tot_outs=[(1, ada_cols)])

    partial_of = {"norm1_g": g_norm1, "q_latent_g": g_gq, "kv_latent_g": g_gkv, "qk_norm_q_g": _head_dims(g_gqn),
                  "qk_norm_k_g": _head_dims(g_gkn), "conv_b": g_conv_b, "conv_ln_g": g_ln_g, "conv_ln_b": g_ln_b, "norm2_g": g_norm2}
    names = [n for n in REPLICATED if n != "b_ada"]
    pieces = [_pad_lanes(partial_of[n], -(-partial_of[n].shape[1] // LANES) * LANES) for n in names] + [g_b_cols, sq_err]
    widths = [p.shape[1] for p in pieces]
    small = jnp.concatenate(pieces, axis=1)
    small = _pad_lanes(small, -(-small.shape[1] // (8 * LANES)) * 8 * LANES).reshape(-1, LANES)
    (small_s,) = _exchange("gather_small_grads", [(small, True)])
    small_s = small_s.reshape(N_DEV, 1, -1)
    parts = {}
    off = 0
    for n, wd in zip(names, widths):
        parts[n] = small_s[:, :, off:off + vec[n].shape[1]]
        off += wd
    parts["b_ada"] = small_s[:, 0, off:off + ada_cols].reshape(1, 1, N_DEV * ada_cols)
    loss = jnp.sum(small_s[:, 0, off + ada_cols]) * (0.5 / d)
    g_in_mine = _sum_parts("sum_w_in", p_in).T
    parts.update({"w_ada": g_ada[None], "w_in": g_in_mine[None], "w_uq": p_uq, "w_ukv": p_ukv, "w_o_mla": p_o,
                  "conv_w": p_conv_w, "w_pw_out": p_pw, "w_out": p_out, "w_ff1": p_ff1, "w_ff2": p_ff2})
    transposed = ("w_o_mla", "w_pw_out", "w_ff1")

    grad_out, delta_out, m_out, v_out = [], [], [], []
    for n in WEIGHTS:
        shape2 = local[n].shape if local[n].ndim == 2 else (1, local[n].shape[0])
        g_w, d_w, n_m, n_v = _adamw("adamw_" + n, local[n].reshape(shape2), parts[n], given["m_" + n].reshape(shape2),
                                    given["v_" + n].reshape(shape2), transposed=n in transposed)
        full_shape = given[n].shape
        grad_out.append(g_w.reshape(full_shape))
        delta_out.append(d_w.reshape(full_shape))
        m_out.append(n_m.reshape(full_shape))
        v_out.append(n_v.reshape(full_shape))
    return (loss, grad_x.reshape(n_b, seq, d), *grad_out, *delta_out, *m_out, *v_out)
```

```python
import functools

import jax
import jax.numpy as jnp
from jax import lax
from jax.experimental import pallas as pl
from jax.experimental.pallas import tpu as pltpu

F32 = jnp.float32
BF16 = jnp.bfloat16

N_DEV = 8
EPS = 1e-6
N_HEADS = 8
QK_HEAD_DIM = 96
QK_NOPE_DIM = 64
QK_ROPE_DIM = 32
V_HEAD_DIM = 64
HEAD_PAD = 128
Q_LORA = 256
KV_LORA = 128
CONV_CH = 512
CONV_WIDTH = 31
CONV_HALO = 32
CONV_TAIL = 8
CHUNK = 64
ROPE_THETA = 10000.0
OFF_Q = Q_LORA
OFF_KV = OFF_Q + KV_LORA
OFF_KR = OFF_KV + QK_ROPE_DIM
OFF_GLU = OFF_KR + 2 * CONV_CH
ADA_CHUNKS = 6
ADAM_LR = 0.001
ADAM_B1 = 0.9
ADAM_B2 = 0.999
ADAM_EPS = 1e-08
ADAM_WD = 0.01
ADAM_STEP = 10
LANES = 128
VMEM_LIMIT = 56 * 1024 * 1024
NEG_BIG = -1e30
ATT_HEADS = 4
ATT_TILE = 512
PREP_TILE = 512
ATT_SCALE = QK_HEAD_DIM ** -0.5
LOG2E = 1.4426950408889634
LN2 = 0.6931471805599453
QK_SCALE = ATT_SCALE * LOG2E
ADAM_ROWS = 256
ROWS_PAD = 16

REPLICATED = ("b_ada", "norm1_g", "q_latent_g", "kv_latent_g", "qk_norm_q_g", "qk_norm_k_g", "conv_b", "conv_ln_g",
              "conv_ln_b", "norm2_g")
WEIGHTS = ("w_ada", "b_ada", "norm1_g", "w_in", "q_latent_g", "w_uq", "kv_latent_g", "w_ukv", "qk_norm_q_g",
           "qk_norm_k_g", "w_o_mla", "conv_w", "conv_b", "conv_ln_g", "conv_ln_b", "w_pw_out", "w_out", "norm2_g",
           "w_ff1", "w_ff2")


def _tile(dim, pref):
    if dim <= pref:
        return dim
    t = (pref // LANES) * LANES
    while dim % t:
        t -= LANES
    return t


def _params(semantics):
    return pltpu.CompilerParams(dimension_semantics=semantics, vmem_limit_bytes=VMEM_LIMIT)


def _sigmoid(v):
    return 1.0 / (1.0 + jnp.exp(-v))


def _silu(v):
    return v * _sigmoid(v)


def _relu2(v):
    return jnp.square(jnp.maximum(v, 0.0))


_DIMS = {"nn": (((1,), (0,)), ((), ())), "nt": (((1,), (1,)), ((), ())), "tn": (((0,), (0,)), ((), ()))}


def _mm(name, a, b, mode, out_dtype, *, a_fn=None, epi=None, epi_in=(), tm=1024, tn=1024, tk=1024):
    if mode == "nn":
        (m, k), n = a.shape, b.shape[1]
    elif mode == "nt":
        (m, k), n = a.shape, b.shape[0]
    else:
        (k, m), n = a.shape, b.shape[1]
    tm, tn, tk = _tile(m, tm), _tile(n, tn), _tile(k, tk)
    nk = k // tk
    a_spec = (pl.BlockSpec((tk, tm), lambda i, j, kk: (kk, i)) if mode == "tn"
              else pl.BlockSpec((tm, tk), lambda i, j, kk: (i, kk)))
    b_spec = (pl.BlockSpec((tn, tk), lambda i, j, kk: (j, kk)) if mode == "nt"
              else pl.BlockSpec((tk, tn), lambda i, j, kk: (kk, j)))
    o_spec = e_spec = pl.BlockSpec((tm, tn), lambda i, j, kk: (i, j))
    out_shape = jax.ShapeDtypeStruct((m, n), out_dtype)
    n_epi = len(epi_in)

    def body(a_ref, b_ref, *rest):
        epi_refs, o_ref, acc_ref = rest[:n_epi], rest[n_epi], rest[n_epi + 1]
        kk = pl.program_id(2)

        @pl.when(kk == 0)
        def _():
            acc_ref[...] = jnp.zeros_like(acc_ref)

        av = a_ref[...]
        if a_fn is not None:
            av = a_fn(av.astype(F32))
        acc_ref[...] += lax.dot_general(av.astype(BF16), b_ref[...].astype(BF16), _DIMS[mode],
                                        preferred_element_type=F32)

        @pl.when(kk == nk - 1)
        def _():
            acc = acc_ref[...]
            if epi is not None:
                acc = epi(acc, *[r[...].astype(F32) for r in epi_refs])
            o_ref[...] = acc.astype(out_dtype)

    return pl.pallas_call(
        body, name=name, grid=(m // tm, n // tn, nk),
        in_specs=[a_spec, b_spec] + [e_spec] * n_epi, out_specs=o_spec, out_shape=out_shape,
        scratch_shapes=[pltpu.VMEM((tm, tn), F32)],
        compiler_params=_params(("parallel", "parallel", "arbitrary")),
    )(a, b, *epi_in)


def _rowwise(name, fn, n_rows, seq, rows, bats=(), consts=(), outs=(), bat_outs=(), tot_outs=(), tm=256):
    tm = min(tm, seq)
    per_seq = seq // tm
    n_b = n_rows // seq
    nr, nb, nc, no, nbo, nto = len(rows), len(bats), len(consts), len(outs), len(bat_outs), len(tot_outs)

    def body(*refs):
        i = pl.program_id(0)
        r_in = [r[...] for r in refs[:nr]]
        b_in = [r[0] for r in refs[nr:nr + nb]]
        c_in = [r[...] for r in refs[nr + nb:nr + nb + nc]]
        o_refs = refs[nr + nb + nc:nr + nb + nc + no]
        bo_refs = refs[nr + nb + nc + no:nr + nb + nc + no + nbo]
        to_refs = refs[nr + nb + nc + no + nbo:]
        o_val, bo_val, to_val = fn(r_in, b_in, c_in)
        for r, v in zip(o_refs, o_val):
            r[...] = v.astype(r.dtype)
        if nbo:
            @pl.when(i % per_seq == 0)
            def _():
                for r in bo_refs:
                    r[...] = jnp.zeros_like(r)

            for r, v in zip(bo_refs, bo_val):
                r[0] += v
        if nto:
            @pl.when(i == 0)
            def _():
                for r in to_refs:
                    r[...] = jnp.zeros_like(r)

            for r, v in zip(to_refs, to_val):
                r[...] += v

    in_specs = [pl.BlockSpec((tm, w), functools.partial(lambda cb, i: (i, cb), cb)) for (_, w, cb) in rows]
    in_specs += [pl.BlockSpec((1, 1, bt.shape[2]), lambda i: (i // per_seq, 0, 0)) for bt in bats]
    in_specs += [pl.BlockSpec(ct.shape, lambda i: (0, 0)) for ct in consts]
    out_specs = [pl.BlockSpec((tm, w), lambda i: (i, 0)) for (w, _) in outs]
    out_specs += [pl.BlockSpec((1, 1, w), lambda i: (i // per_seq, 0, 0)) for w in bat_outs]
    out_specs += [pl.BlockSpec(shp, lambda i: (0, 0)) for shp in tot_outs]
    out_shape = [jax.ShapeDtypeStruct((n_rows, w), dt) for (w, dt) in outs]
    out_shape += [jax.ShapeDtypeStruct((n_b, 1, w), F32) for w in bat_outs]
    out_shape += [jax.ShapeDtypeStruct(shp, F32) for shp in tot_outs]
    res = pl.pallas_call(
        body, name=name, grid=(n_rows // tm,), in_specs=in_specs, out_specs=out_specs, out_shape=out_shape,
        compiler_params=_params(("arbitrary",)),
    )(*[r[0] for r in rows], *bats, *consts)
    return res


def _full(arr):
    return (arr, arr.shape[1], 0)


def _mm_rows(name, a_rows, a_fn, w, mode, fn, n_rows, seq, rows=(), bats=(), consts=(), outs=(), bat_outs=(),
             tot_outs=(), a_out=None, tm=512, tk=1024):
    tm = min(tm, seq)
    per_seq = seq // tm
    n_b = n_rows // seq
    k = a_rows[0][1]
    if mode == "nt":
        n_out, tk = w.shape[0], _tile(k, tk)
        w_spec = pl.BlockSpec((n_out, tk), lambda i, kk: (0, kk))
    else:
        n_out, tk = w.shape[1], _tile(k, tk)
        w_spec = pl.BlockSpec((tk, n_out), lambda i, kk: (kk, 0))
    nk = k // tk
    na, nr, nb, nc = len(a_rows), len(rows), len(bats), len(consts)
    n_extra = 0 if a_out is None else 1
    no, nbo, nto = len(outs), len(bat_outs), len(tot_outs)

    def body(*refs):
        i, kk = pl.program_id(0), pl.program_id(1)
        a_refs, w_ref = refs[:na], refs[na]
        pos = na + 1
        r_refs, b_refs, c_refs = refs[pos:pos + nr], refs[pos + nr:pos + nr + nb], refs[pos + nr + nb:pos + nr + nb + nc]
        pos += nr + nb + nc
        ao_refs = refs[pos:pos + n_extra]
        pos += n_extra
        o_refs, bo_refs, to_refs = refs[pos:pos + no], refs[pos + no:pos + no + nbo], refs[pos + no + nbo:pos + no + nbo + nto]
        acc_ref = refs[pos + no + nbo + nto]

        @pl.when(kk == 0)
        def _():
            acc_ref[...] = jnp.zeros_like(acc_ref)

        tiles = [r[...] for r in a_refs]
        av = a_fn([t.astype(F32) for t in tiles]) if a_fn is not None else tiles[0]
        av = av.astype(BF16)
        if n_extra:
            ao_refs[0][...] = av.astype(ao_refs[0].dtype)
        acc_ref[...] += lax.dot_general(av, w_ref[...].astype(BF16), _DIMS[mode], preferred_element_type=F32)

        @pl.when(kk == nk - 1)
        def _():
            o_val, bo_val, to_val = fn(acc_ref[...], [r[...] for r in r_refs], [r[0] for r in b_refs],
                                       [r[...] for r in c_refs])
            for r, v in zip(o_refs, o_val):
                r[...] = v.astype(r.dtype)
            if nbo:
                @pl.when(i % per_seq == 0)
                def _():
                    for r in bo_refs:
                        r[...] = jnp.zeros_like(r)

                for r, v in zip(bo_refs, bo_val):
                    r[0] += v
            if nto:
                @pl.when(i == 0)
                def _():
                    for r in to_refs:
                        r[...] = jnp.zeros_like(r)

                for r, v in zip(to_refs, to_val):
                    r[...] += v

    in_specs = [pl.BlockSpec((tm, tk), functools.partial(lambda cb, i, kk: (i, kk + cb), cb)) for (_, _, cb) in a_rows]
    in_specs += [w_spec]
    in_specs += [pl.BlockSpec((tm, wd), functools.partial(lambda cb, i, kk: (i, cb), cb)) for (_, wd, cb) in rows]
    in_specs += [pl.BlockSpec((1, 1, bt.shape[2]), lambda i, kk: (i // per_seq, 0, 0)) for bt in bats]
    in_specs += [pl.BlockSpec(ct.shape, lambda i, kk: (0, 0)) for ct in consts]
    out_specs = [pl.BlockSpec((tm, tk), lambda i, kk: (i, kk))] * n_extra
    out_specs += [pl.BlockSpec((tm, wd), lambda i, kk: (i, 0)) for (wd, _) in outs]
    out_specs += [pl.BlockSpec((1, 1, wd), lambda i, kk: (i // per_seq, 0, 0)) for wd in bat_outs]
    out_specs += [pl.BlockSpec(shp, lambda i, kk: (0, 0)) for shp in tot_outs]
    out_shape = [jax.ShapeDtypeStruct((n_rows, k), a_out)] if n_extra else []
    out_shape += [jax.ShapeDtypeStruct((n_rows, wd), dt) for (wd, dt) in outs]
    out_shape += [jax.ShapeDtypeStruct((n_b, 1, wd), F32) for wd in bat_outs]
    out_shape += [jax.ShapeDtypeStruct(shp, F32) for shp in tot_outs]
    return pl.pallas_call(
        body, name=name, grid=(n_rows // tm, nk), in_specs=in_specs, out_specs=out_specs, out_shape=out_shape,
        scratch_shapes=[pltpu.VMEM((tm, n_out), F32)],
        compiler_params=_params(("arbitrary", "arbitrary")),
    )(*[a for a, _, _ in a_rows], w, *[r[0] for r in rows], *bats, *consts)


def _norm_mod(x, g, scale, shift):
    r = lax.rsqrt(jnp.mean(x * x, axis=-1, keepdims=True) + EPS)
    xh = x * r
    return xh * g * (1.0 + scale) + shift


def _norm_mod_bwd(x, g, scale, dh):
    r = lax.rsqrt(jnp.mean(x * x, axis=-1, keepdims=True) + EPS)
    xh = x * r
    dn = dh * (1.0 + scale)
    dxh = dn * g
    dx = r * (dxh - xh * jnp.mean(dxh * xh, axis=-1, keepdims=True))
    dscale = jnp.sum(dh * xh * g, axis=0, keepdims=True)
    dshift = jnp.sum(dh, axis=0, keepdims=True)
    dg = jnp.sum(dn * xh, axis=0, keepdims=True)
    return dx, dscale, dshift, dg


def _rms(v, g):
    r = lax.rsqrt(jnp.mean(v * v, axis=-1, keepdims=True) + EPS)
    return v * r * g


def _rms_bwd(v, g, dy):
    r = lax.rsqrt(jnp.mean(v * v, axis=-1, keepdims=True) + EPS)
    vh = v * r
    dvh = dy * g
    dv = r * (dvh - vh * jnp.mean(dvh * vh, axis=-1, keepdims=True))
    return dv, jnp.sum(dy * vh, axis=0, keepdims=True)


def _lane_sum(t):
    return jnp.dot(t.astype(BF16), jnp.ones((HEAD_PAD, HEAD_PAD), BF16), preferred_element_type=F32)


def _head_norm(v, g):
    r = lax.rsqrt(_lane_sum(v * v) * (1.0 / QK_HEAD_DIM) + EPS)
    return v * r * g


def _head_norm_bwd(v, g, dy):
    r = lax.rsqrt(_lane_sum(v * v) * (1.0 / QK_HEAD_DIM) + EPS)
    vh = v * r
    dvh = dy * g
    dv = r * (dvh - vh * (_lane_sum(dvh * vh) * (1.0 / QK_HEAD_DIM)))
    return dv, jnp.sum(dy * vh, axis=0, keepdims=True)


def _rope(v, cos, sin):
    return v * cos + pltpu.roll(v, HEAD_PAD // 2, 1) * sin


def _rope_bwd(g, cos, sin):
    return g * cos + pltpu.roll(g * sin, HEAD_PAD // 2, 1)


def _mla_prep_fwd(zsm, wuq, wukv, gq, gkv, gqn, gkn, rope, n_b, seq):
    n_rows = n_b * seq
    tm = min(PREP_TILE, seq)
    per_seq = seq // tm
    att_tile = min(ATT_TILE, seq)
    k_cols = N_HEADS * HEAD_PAD

    def body(z_ref, wuq_ref, wukv_ref, gq_ref, gkv_ref, gqn_ref, gkn_ref, c_ref, s_ref, q_ref, k_ref, v_ref, kt_ref):
        z = z_ref[...]
        qn = _rms(z[:, :Q_LORA], gq_ref[...]).astype(BF16)
        kvn = _rms(z[:, Q_LORA:Q_LORA + KV_LORA], gkv_ref[...]).astype(BF16)
        krp = z[:, Q_LORA + KV_LORA:]
        cos, sin = c_ref[...], s_ref[...]
        q_all = jnp.dot(qn, wuq_ref[...], preferred_element_type=F32)
        kv_all = jnp.dot(kvn, wukv_ref[...], preferred_element_type=F32)
        for h in range(N_HEADS):
            cols = slice(h * HEAD_PAD, (h + 1) * HEAD_PAD)
            q_ref[0, h] = (_rope(_head_norm(q_all[:, cols], gqn_ref[...]), cos, sin) * QK_SCALE).astype(BF16)
            kh = _rope(_head_norm(kv_all[:, cols] + krp, gkn_ref[...]), cos, sin)
            k_ref[0, h] = kh.astype(BF16)
            for part in range(tm // att_tile):
                kt_ref[0, h, part] = kh[part * att_tile:(part + 1) * att_tile].T.astype(BF16)
            v_ref[0, h] = kv_all[:, k_cols + h * HEAD_PAD:k_cols + (h + 1) * HEAD_PAD].astype(BF16)

    whole2 = lambda arr: pl.BlockSpec(arr.shape, lambda i: (0, 0))
    rope_spec = pl.BlockSpec((tm, HEAD_PAD), lambda i: (i % per_seq, 0))
    head_spec = pl.BlockSpec((1, N_HEADS, tm, HEAD_PAD), lambda i: (i // per_seq, 0, i % per_seq, 0))
    head_shape = jax.ShapeDtypeStruct((n_b, N_HEADS, seq, HEAD_PAD), BF16)
    t_spec = pl.BlockSpec((1, N_HEADS, tm // att_tile, HEAD_PAD, att_tile), lambda i: (i // per_seq, 0, i % per_seq, 0, 0))
    t_shape = jax.ShapeDtypeStruct((n_b, N_HEADS, seq // att_tile, HEAD_PAD, att_tile), BF16)
    return pl.pallas_call(
        body, name="mla_prep_fwd", grid=(n_rows // tm,),
        in_specs=[pl.BlockSpec((tm, 512), lambda i: (i, 0)), whole2(wuq), whole2(wukv),
                  whole2(gq), whole2(gkv), whole2(gqn), whole2(gkn), rope_spec, rope_spec],
        out_specs=[head_spec] * 3 + [t_spec], out_shape=[head_shape] * 3 + [t_shape],
        compiler_params=_params(("parallel",)),
    )(zsm, wuq, wukv, gq, gkv, gqn, gkn, *rope)


def _mla_prep_bwd(zsm, dq, dk, dv, wuq, wukv, gq, gkv, gqn, gkn, rope, n_b, seq):
    n_rows = n_b * seq
    tm = min(PREP_TILE, seq)
    per_seq = seq // tm
    tn_dims = _DIMS["tn"]
    nt_dims = _DIMS["nt"]
    k_cols = N_HEADS * HEAD_PAD

    def body(z_ref, dq_ref, dk_ref, dv_ref, wuq_ref, wukv_ref, gq_ref, gkv_ref, gqn_ref, gkn_ref,
             c_ref, s_ref, dz_ref, dwuq_ref, dwukv_ref, dgq_ref, dgkv_ref, dgqn_ref, dgkn_ref):
        @pl.when(pl.program_id(0) == 0)
        def _():
            for r in (dwuq_ref, dwukv_ref, dgq_ref, dgkv_ref, dgqn_ref, dgkn_ref):
                r[...] = jnp.zeros_like(r)

        z = z_ref[...]
        zq, zkv, krp = z[:, :Q_LORA], z[:, Q_LORA:Q_LORA + KV_LORA], z[:, Q_LORA + KV_LORA:]
        qn = _rms(zq, gq_ref[...]).astype(BF16)
        kvn = _rms(zkv, gkv_ref[...]).astype(BF16)
        cos, sin = c_ref[...], s_ref[...]
        lane = lax.broadcasted_iota(jnp.int32, (tm, HEAD_PAD), 1)
        rope_lanes = (lane % (HEAD_PAD // 2)) < QK_ROPE_DIM // 2
        q_all = jnp.dot(qn, wuq_ref[...], preferred_element_type=F32)
        k_all = jnp.dot(kvn, wukv_ref[:, :k_cols], preferred_element_type=F32)
        dkrp = jnp.zeros((tm, HEAD_PAD), F32)
        dgqn = jnp.zeros((1, HEAD_PAD), F32)
        dgkn = jnp.zeros((1, HEAD_PAD), F32)
        dq_heads, dk_heads = [], []
        for h in range(N_HEADS):
            cols = slice(h * HEAD_PAD, (h + 1) * HEAD_PAD)
            dqh, dg = _head_norm_bwd(q_all[:, cols], gqn_ref[...],
                                     _rope_bwd(dq_ref[0, h].astype(F32) * ATT_SCALE, cos, sin))
            dgqn += dg
            dq_heads.append(dqh.astype(BF16))
            dkh, dg = _head_norm_bwd(k_all[:, cols] + krp, gkn_ref[...], _rope_bwd(dk_ref[0, h].astype(F32), cos, sin))
            dgkn += dg
            dkrp += jnp.where(rope_lanes, dkh, 0.0)
            dk_heads.append(dkh.astype(BF16))
        dq_all = jnp.concatenate(dq_heads, axis=1)
        dkv_all = jnp.concatenate(dk_heads + [dv_ref[0, h] for h in range(N_HEADS)], axis=1)
        dwuq_ref[...] += lax.dot_general(qn, dq_all, tn_dims, preferred_element_type=F32)
        dqn = lax.dot_general(dq_all, wuq_ref[...], nt_dims, preferred_element_type=F32)
        dwukv_ref[...] += lax.dot_general(kvn, dkv_all, tn_dims, preferred_element_type=F32)
        dkvn = lax.dot_general(dkv_all, wukv_ref[...], nt_dims, preferred_element_type=F32)
        dzq, dg = _rms_bwd(zq, gq_ref[...], dqn)
        dgq_ref[...] += dg
        dzkv, dg = _rms_bwd(zkv, gkv_ref[...], dkvn)
        dgkv_ref[...] += dg
        dgqn_ref[...] += dgqn
        dgkn_ref[...] += dgkn
        dz_ref[:, :Q_LORA] = dzq.astype(dz_ref.dtype)
        dz_ref[:, Q_LORA:Q_LORA + KV_LORA] = dzkv.astype(dz_ref.dtype)
        dz_ref[:, Q_LORA + KV_LORA:] = dkrp.astype(dz_ref.dtype)

    whole2 = lambda arr: pl.BlockSpec(arr.shape, lambda i: (0, 0))
    rope_spec = pl.BlockSpec((tm, HEAD_PAD), lambda i: (i % per_seq, 0))
    head_spec = pl.BlockSpec((1, N_HEADS, tm, HEAD_PAD), lambda i: (i // per_seq, 0, i % per_seq, 0))
    row_spec = pl.BlockSpec((tm, 512), lambda i: (i, 0))
    return pl.pallas_call(
        body, name="mla_prep_bwd", grid=(n_rows // tm,),
        in_specs=[row_spec, head_spec, head_spec, head_spec, whole2(wuq), whole2(wukv),
                  whole2(gq), whole2(gkv), whole2(gqn), whole2(gkn), rope_spec, rope_spec],
        out_specs=[row_spec, whole2(wuq), whole2(wukv), whole2(gq), whole2(gkv), whole2(gqn), whole2(gkn)],
        out_shape=[jax.ShapeDtypeStruct((n_rows, 512), BF16),
                   jax.ShapeDtypeStruct(wuq.shape, F32), jax.ShapeDtypeStruct(wukv.shape, F32),
                   jax.ShapeDtypeStruct(gq.shape, F32), jax.ShapeDtypeStruct(gkv.shape, F32),
                   jax.ShapeDtypeStruct(gqn.shape, F32), jax.ShapeDtypeStruct(gkn.shape, F32)],
        compiler_params=_params(("arbitrary",)),
    )(zsm, dq, dk, dv, wuq, wukv, gq, gkv, gqn, gkn, *rope)


HBM_SPEC = pl.BlockSpec(memory_space=pltpu.HBM)


def _xchg_out_shapes(bufs):
    return [jax.ShapeDtypeStruct((N_DEV,) + (a.shape if gather else a.shape[1:]), a.dtype) for a, gather in bufs]


def _xchg_scratch(n_buf):
    return [pltpu.SemaphoreType.DMA((n_buf * (N_DEV - 1),)), pltpu.SemaphoreType.DMA((n_buf * (N_DEV - 1),)),
            pltpu.SemaphoreType.DMA((n_buf,))]


def _xchg_copies(src_refs, dst_refs, gathers, send_sems, recv_sems, local_sems):
    x, y, c = lax.axis_index("x"), lax.axis_index("y"), lax.axis_index("c")
    me = 4 * x + 2 * y + c
    local, starts, arrivals = [], [], []
    for bi, (src, dst, gather) in enumerate(zip(src_refs, dst_refs, gathers)):
        local.append(pltpu.make_async_copy(src if gather else src.at[me], dst.at[me], local_sems.at[bi]))
        for kk in range(1, N_DEV):
            px = 1 - x if kk & 4 else x
            py = 1 - y if kk & 2 else y
            pc = 1 - c if kk & 1 else c
            pid = 4 * px + 2 * py + pc
            sem = bi * (N_DEV - 1) + kk - 1
            starts.append(pltpu.make_async_remote_copy(
                src_ref=src if gather else src.at[pid], dst_ref=dst.at[me],
                send_sem=send_sems.at[sem], recv_sem=recv_sems.at[sem],
                device_id=(px, py, pc), device_id_type=pl.DeviceIdType.MESH))
            arrivals.append(pltpu.make_async_remote_copy(
                src_ref=src if gather else src.at[me], dst_ref=dst.at[pid],
                send_sem=send_sems.at[sem], recv_sem=recv_sems.at[sem],
                device_id=(px, py, pc), device_id_type=pl.DeviceIdType.MESH))
    return local, starts, arrivals


def _xchg_start(copies):
    local, sends, _ = copies
    for cp in local + sends:
        cp.start()


def _xchg_finish(copies):
    local, sends, arrivals = copies
    for cp in arrivals:
        cp.wait_recv()
    for cp in sends:
        cp.wait_send()
    for cp in local:
        cp.wait()


def _gather_by_chip(src_refs, dst_refs, send_sems, recv_sems, local_sems, start=True, finish=True):
    x, y, c = lax.axis_index("x"), lax.axis_index("y"), lax.axis_index("c")
    me = 4 * x + 2 * y + c
    sibling = (x, y, 1 - c)

    def place(kk):
        px = 1 - x if kk & 4 else x
        py = 1 - y if kk & 2 else y
        pc = 1 - c if kk & 1 else c
        return (px, py, pc), 4 * px + 2 * py + pc

    def copy(bi, kk, src, dst, to):
        sem = bi * (N_DEV - 1) + kk - 1
        return pltpu.make_async_remote_copy(src_ref=src, dst_ref=dst, send_sem=send_sems.at[sem],
                                            recv_sem=recv_sems.at[sem], device_id=to, device_id_type=pl.DeviceIdType.MESH)

    local, sends = [], []
    for bi, (src, dst) in enumerate(zip(src_refs, dst_refs)):
        local.append(pltpu.make_async_copy(src, dst.at[me], local_sems.at[bi]))
        sends += [copy(bi, kk, src, dst.at[me], place(kk)[0]) for kk in (1, 2, 4, 6)]
    if start:
        for cp in local + sends:
            cp.start()
    if not finish:
        return
    for kk in (2, 4, 6):
        for bi, (src, dst) in enumerate(zip(src_refs, dst_refs)):
            dev, pid = place(kk)
            copy(bi, kk, src, dst.at[pid], dev).wait_recv()
            passed = copy(bi, kk | 1, dst.at[pid], dst.at[pid], sibling)
            passed.start()
            sends.append(passed)
    for kk in (1, 3, 5, 7):
        for bi, (src, dst) in enumerate(zip(src_refs, dst_refs)):
            dev, pid = place(kk)
            copy(bi, kk, src, dst.at[pid], sibling).wait_recv()
    for cp in sends:
        cp.wait_send()
    for cp in local:
        cp.wait()


def _exchange(name, bufs, by_chip=False):
    n_buf = len(bufs)
    gathers = [g for _, g in bufs]
    assert not by_chip or all(gathers)

    def body(*refs):
        srcs, dsts = refs[:n_buf], refs[n_buf:2 * n_buf]
        if by_chip:
            _gather_by_chip(srcs, dsts, *refs[2 * n_buf:])
            return
        copies = _xchg_copies(srcs, dsts, gathers, *refs[2 * n_buf:])
        _xchg_start(copies)
        _xchg_finish(copies)

    return pl.pallas_call(
        body, name=name, out_shape=_xchg_out_shapes(bufs),
        in_specs=[HBM_SPEC] * n_buf, out_specs=[HBM_SPEC] * n_buf, scratch_shapes=_xchg_scratch(n_buf),
    )(*[a for a, _ in bufs])


def _chunk_mask(t, keys_first):
    key = lax.broadcasted_iota(jnp.int32, (t, t), 0 if keys_first else 1) // CHUNK
    query = lax.broadcasted_iota(jnp.int32, (t, t), 1 if keys_first else 0) // CHUNK
    return query >= key


def _grid_ends(grid):
    ids = [pl.program_id(ax) for ax in range(len(grid))]
    first = functools.reduce(jnp.logical_and, [i == 0 for i in ids])
    last = functools.reduce(jnp.logical_and, [i == g - 1 for i, g in zip(ids, grid)])
    return first, last


def _attn_fwd(q, k, v, bufs, n_b, seq):
    tq = min(ATT_TILE, seq)
    nq = seq // tq
    nt_dims = _DIMS["nt"]
    hpb = ATT_HEADS
    grid = (n_b, N_HEADS // hpb, nq)
    n_buf = len(bufs)
    gathers = [g for _, g in bufs]
    sum_lane = [HEAD_PAD - 1 if hh % 2 == 0 else 0 for hh in range(hpb)]

    def body(q_ref, k_ref, v_ref, *rest):
        srcs, (o_ref, lse_ref), dsts = rest[:n_buf], rest[n_buf:n_buf + 2], rest[n_buf + 2:2 * n_buf + 2]
        s_ref = rest[2 * n_buf + 2]
        gather = functools.partial(_gather_by_chip, srcs, dsts, *rest[2 * n_buf + 3:])
        first, last = _grid_ends(grid)
        pl.when(first)(functools.partial(gather, start=True, finish=False))

        qi = pl.program_id(2)
        mask = _chunk_mask(tq, keys_first=False)
        lane_row = lax.broadcasted_iota(jnp.int32, (1, HEAD_PAD), 1)
        ones = [(lane_row == sum_lane[hh]).astype(BF16) for hh in range(hpb)]
        qs = [q_ref[0, hh] for hh in range(hpb)]

        def score_step(j, tops, masked):
            rows = pl.ds(pl.multiple_of(j * tq, tq), tq)
            out = []
            for hh in range(hpb):
                s = lax.dot_general(qs[hh], k_ref[0, hh, rows, :], nt_dims, preferred_element_type=F32)
                if masked:
                    s = jnp.where(mask, s, NEG_BIG)
                s_ref[hh, j] = s
                out.append(jnp.maximum(tops[hh], s))
            return tuple(out)

        tops = tuple(jnp.full((tq, tq), NEG_BIG, F32) for _ in range(hpb))
        tops = lax.fori_loop(0, qi, functools.partial(score_step, masked=False), tops)
        tops = score_step(qi, tops, True)
        ms = [jnp.max(top, axis=-1, keepdims=True) for top in tops]

        def value_step(j, accs):
            rows = pl.ds(pl.multiple_of(j * tq, tq), tq)
            out = []
            for hh in range(hpb):
                p = jnp.exp2(s_ref[hh, j] - ms[hh]).astype(BF16)
                out.append(accs[hh] + jnp.dot(p, v_ref[0, hh, rows, :] + ones[hh], preferred_element_type=F32))
            return tuple(out)

        accs = tuple(jnp.zeros((tq, HEAD_PAD), F32) for _ in range(hpb))
        accs = lax.fori_loop(0, qi + 1, value_step, accs)
        carry = list(zip(ms, accs))
        lane = lax.broadcasted_iota(jnp.int32, (tq, HEAD_PAD), 1)
        for pair in range(hpb // 2):
            outs = []
            for hh in (2 * pair, 2 * pair + 1):
                m, acc = carry[hh]
                l = jnp.sum(jnp.where(lane == sum_lane[hh], acc, 0.0), axis=-1, keepdims=True)
                outs.append(acc * (1.0 / l))
                lse_ref[0, hh] = jnp.broadcast_to(m + jnp.log2(l), (tq, HEAD_PAD)).T[0:8, :]
            o_ref[0, :, pair * HEAD_PAD:(pair + 1) * HEAD_PAD] = jnp.where(lane < V_HEAD_DIM, outs[0], outs[1]).astype(BF16)

        pl.when(last)(functools.partial(gather, start=False, finish=True))

    kv_spec = pl.BlockSpec((1, hpb, seq, HEAD_PAD), lambda b, hb, i: (b, hb, 0, 0))
    q_spec = pl.BlockSpec((1, hpb, tq, HEAD_PAD), lambda b, hb, i: (b, hb, i, 0))
    res = pl.pallas_call(
        body, name="attn_fwd", grid=grid,
        in_specs=[q_spec, kv_spec, kv_spec] + [HBM_SPEC] * n_buf,
        out_specs=[pl.BlockSpec((1, tq, hpb * V_HEAD_DIM), lambda b, hb, i: (b, i, hb)),
                   pl.BlockSpec((1, hpb, 8, tq), lambda b, hb, i: (b, hb, 0, i))] + [HBM_SPEC] * n_buf,
        out_shape=[jax.ShapeDtypeStruct((n_b, seq, N_HEADS * V_HEAD_DIM), BF16),
                   jax.ShapeDtypeStruct((n_b, N_HEADS, 8, seq), F32)] + _xchg_out_shapes(bufs),
        scratch_shapes=[pltpu.VMEM((hpb, nq, tq, tq), F32)] + _xchg_scratch(n_buf),
        compiler_params=_params(("arbitrary", "arbitrary", "arbitrary")),
    )(q, k, v, *[a for a, _ in bufs])
    return res[0], res[1], res[2:]


def _attn_bwd(q, k, v, kt, do, o, lse, bufs, n_b, seq):
    tq = min(ATT_TILE, seq)
    nq = seq // tq
    nt_dims = _DIMS["nt"]
    hpb = ATT_HEADS
    grid = (n_b, N_HEADS // hpb, nq)
    n_buf = len(bufs)
    gathers = [g for _, g in bufs]

    def body(q_ref, k_ref, v_ref, kt_ref, do_ref, o_ref, lse_ref, *rest):
        srcs, (dq_ref, dk_ref, dv_ref), dsts = rest[:n_buf], rest[n_buf:n_buf + 3], rest[n_buf + 3:2 * n_buf + 3]
        dk_acc, dv_acc = rest[2 * n_buf + 3:2 * n_buf + 5]
        copies = _xchg_copies(srcs, dsts, gathers, *rest[2 * n_buf + 5:])
        first, last = _grid_ends(grid)
        pl.when(first)(functools.partial(_xchg_start, copies))

        qi = pl.program_id(2)

        @pl.when(qi == 0)
        def _():
            dk_acc[...] = jnp.zeros_like(dk_acc)
            dv_acc[...] = jnp.zeros_like(dv_acc)

        mask = _chunk_mask(tq, keys_first=True)
        lane = lax.broadcasted_iota(jnp.int32, (tq, HEAD_PAD), 1)
        qs, dos, deltas, lses = [], [], [], []
        for hh in range(hpb):
            cols = slice((hh // 2) * HEAD_PAD, (hh // 2 + 1) * HEAD_PAD)
            do_pair = do_ref[0, :, cols]
            prod = do_pair.astype(F32) * o_ref[0, :, cols].astype(F32)
            delta = jnp.sum(jnp.where(lane // V_HEAD_DIM == hh % 2, prod, 0.0), axis=-1, keepdims=True)
            qs.append(q_ref[0, hh])
            dos.append(do_pair)
            deltas.append(jnp.broadcast_to(delta, (tq, HEAD_PAD)).T[0:1, :])
            lses.append(lse_ref[0, hh][0:1, :])

        def step(j, dqs, masked):
            rows = pl.ds(pl.multiple_of(j * tq, tq), tq)
            out = []
            for hh in range(hpb):
                s = lax.dot_general(k_ref[0, hh, rows, :], qs[hh], nt_dims, preferred_element_type=F32)
                p = jnp.exp2(s - lses[hh])
                if masked:
                    p = jnp.where(mask, p, 0.0)
                dv_acc[hh, rows, :] += jnp.dot(p.astype(BF16), dos[hh], preferred_element_type=F32)
                dp = lax.dot_general(v_ref[0, hh, rows, :], dos[hh], nt_dims, preferred_element_type=F32)
                ds = (p * (dp - deltas[hh])).astype(BF16)
                dk_acc[hh, rows, :] += jnp.dot(ds, qs[hh], preferred_element_type=F32)
                out.append(dqs[hh] + jnp.dot(kt_ref[0, hh, j], ds, preferred_element_type=F32))
            return tuple(out)

        dqs = tuple(jnp.zeros((HEAD_PAD, tq), F32) for _ in range(hpb))
        dqs = lax.fori_loop(0, qi, functools.partial(step, masked=False), dqs)
        dqs = step(qi, dqs, True)
        for hh in range(hpb):
            dq_ref[0, hh] = dqs[hh].T.astype(BF16)

        @pl.when(qi == nq - 1)
        def _():
            dk_ref[0] = (dk_acc[...] * LN2).astype(BF16)
            dv_ref[0] = dv_acc[...].astype(BF16)

        pl.when(last)(functools.partial(_xchg_finish, copies))

    full_spec = pl.BlockSpec((1, hpb, seq, HEAD_PAD), lambda b, hb, i: (b, hb, 0, 0))
    t_spec = pl.BlockSpec((1, hpb, nq, HEAD_PAD, tq), lambda b, hb, i: (b, hb, 0, 0, 0))
    q_spec = pl.BlockSpec((1, hpb, tq, HEAD_PAD), lambda b, hb, i: (b, hb, i, 0))
    o_spec = pl.BlockSpec((1, tq, hpb * V_HEAD_DIM), lambda b, hb, i: (b, i, hb))
    lse_spec = pl.BlockSpec((1, hpb, 8, tq), lambda b, hb, i: (b, hb, 0, i))
    head_shape = jax.ShapeDtypeStruct((n_b, N_HEADS, seq, HEAD_PAD), BF16)
    res = pl.pallas_call(
        body, name="attn_bwd", grid=grid,
        in_specs=[q_spec, full_spec, full_spec, t_spec, o_spec, o_spec, lse_spec] + [HBM_SPEC] * n_buf,
        out_specs=[q_spec, full_spec, full_spec] + [HBM_SPEC] * n_buf,
        out_shape=[head_shape] * 3 + _xchg_out_shapes(bufs),
        scratch_shapes=[pltpu.VMEM((hpb, seq, HEAD_PAD), F32), pltpu.VMEM((hpb, seq, HEAD_PAD), F32)]
        + _xchg_scratch(n_buf),
        compiler_params=_params(("arbitrary", "arbitrary", "arbitrary")),
    )(q, k, v, kt, do, o, lse, *[a for a, _ in bufs])
    return res[0], res[1], res[2], res[3:]


def _in_proj_fwd(x2, scale, shift, g, w_parts, z_dtypes, seq):
    n_rows, d = x2.shape
    tm = min(512, seq)
    per_seq = seq // tm
    n_part = len(w_parts)
    nt_dims = _DIMS["nt"]

    def body(x_ref, sc_ref, sh_ref, g_ref, *rest):
        w_refs, h_ref, z_refs = rest[:n_part], rest[n_part], rest[n_part + 1:]
        h = _norm_mod(x_ref[...], g_ref[...], sc_ref[0], sh_ref[0]).astype(BF16)
        h_ref[...] = h
        for w_ref, z_ref in zip(w_refs, z_refs):
            z_ref[...] = lax.dot_general(h, w_ref[...], nt_dims, preferred_element_type=F32).astype(z_ref.dtype)

    row = lambda width: pl.BlockSpec((tm, width), lambda i: (i, 0))
    bat = pl.BlockSpec((1, 1, d), lambda i: (i // per_seq, 0, 0))
    whole = lambda arr: pl.BlockSpec(arr.shape, lambda i: (0, 0))
    return pl.pallas_call(
        body, name="in_proj_fwd", grid=(n_rows // tm,),
        in_specs=[row(d), bat, bat, whole(g)] + [whole(w) for w in w_parts],
        out_specs=[row(d)] + [row(w.shape[0]) for w in w_parts],
        out_shape=[jax.ShapeDtypeStruct((n_rows, d), BF16)]
        + [jax.ShapeDtypeStruct((n_rows, w.shape[0]), dt) for w, dt in zip(w_parts, z_dtypes)],
        compiler_params=_params(("parallel",)),
    )(x2, scale, shift, g, *w_parts)
def _in_proj_bwd(parts, x2, dx1, scale, g, bufs, seq):
    n_rows, d = x2.shape
    tm = min(512, seq)
    per_seq = seq // tm
    grid = (n_rows // tm,)
    n_part, n_buf = len(parts), len(bufs)
    gathers = [gt for _, gt in bufs]

    def body(*refs):
        dz_refs, w_refs = refs[:n_part], refs[n_part:2 * n_part]
        x_ref, dx1_ref, sc_ref, g_ref = refs[2 * n_part:2 * n_part + 4]
        srcs = refs[2 * n_part + 4:2 * n_part + 4 + n_buf]
        gx_ref, dsc_ref, dsh_ref, dg_ref = refs[2 * n_part + 4 + n_buf:2 * n_part + 8 + n_buf]
        dsts = refs[2 * n_part + 8 + n_buf:2 * n_part + 8 + 2 * n_buf]
        copies = _xchg_copies(srcs, dsts, gathers, *refs[2 * n_part + 8 + 2 * n_buf:])
        first, last = _grid_ends(grid)
        pl.when(first)(functools.partial(_xchg_start, copies))

        i = pl.program_id(0)
        dh = None
        for dz_ref, w_ref in zip(dz_refs, w_refs):
            term = jnp.dot(dz_ref[...], w_ref[...], preferred_element_type=F32)
            dh = term if dh is None else dh + term
        dx, dsc, dsh, dg = _norm_mod_bwd(x_ref[...], g_ref[...], sc_ref[0], dh)
        gx_ref[...] = dx1_ref[...] + dx

        @pl.when(i % per_seq == 0)
        def _():
            dsc_ref[...] = jnp.zeros_like(dsc_ref)
            dsh_ref[...] = jnp.zeros_like(dsh_ref)

        @pl.when(i == 0)
        def _():
            dg_ref[...] = jnp.zeros_like(dg_ref)

        dsc_ref[0] += dsc
        dsh_ref[0] += dsh
        dg_ref[...] += dg
        pl.when(last)(functools.partial(_xchg_finish, copies))

    row = lambda width: pl.BlockSpec((tm, width), lambda i: (i, 0))
    bat = pl.BlockSpec((1, 1, d), lambda i: (i // per_seq, 0, 0))
    whole = lambda arr: pl.BlockSpec(arr.shape, lambda i: (0, 0))
    n_b = n_rows // seq
    res = pl.pallas_call(
        body, name="in_proj_bwd", grid=grid,
        in_specs=[row(dz.shape[1]) for dz, _ in parts] + [whole(w) for _, w in parts]
        + [row(d), row(d), bat, whole(g)] + [HBM_SPEC] * n_buf,
        out_specs=[row(d), bat, bat, whole(g)] + [HBM_SPEC] * n_buf,
        out_shape=[jax.ShapeDtypeStruct((n_rows, d), F32), jax.ShapeDtypeStruct((n_b, 1, d), F32),
                   jax.ShapeDtypeStruct((n_b, 1, d), F32), jax.ShapeDtypeStruct(g.shape, F32)] + _xchg_out_shapes(bufs),
        scratch_shapes=_xchg_scratch(n_buf),
        compiler_params=_params(("arbitrary",)),
    )(*[dz for dz, _ in parts], *[w for _, w in parts], x2, dx1, scale, g, *[a for a, _ in bufs])
    return res[0], res[1], res[2], res[3], res[4:]


def _ln_silu(u1, g, b):
    mu = jnp.mean(u1, axis=-1, keepdims=True)
    uc = u1 - mu
    r = lax.rsqrt(jnp.mean(uc * uc, axis=-1, keepdims=True) + EPS)
    y = uc * r * g + b
    return y * _sigmoid(y)


def _conv_fill_glu(z_ref, u0_ref, seq, tile):
    u0_ref[0:CONV_HALO, :] = jnp.zeros((CONV_HALO, CONV_CH), F32)
    u0_ref[CONV_HALO + seq:CONV_HALO + seq + CONV_TAIL, :] = jnp.zeros((CONV_TAIL, CONV_CH), F32)
    for t in range(seq // tile):
        zt = z_ref[0, t * tile:(t + 1) * tile, :].astype(F32)
        u0_ref[CONV_HALO + t * tile:CONV_HALO + (t + 1) * tile, :] = zt[:, :CONV_CH] * _sigmoid(zt[:, CONV_CH:])


def _conv_windows(ref, views_ref, t, tile):
    for b in range(8):
        views_ref[b] = ref[t * tile + b:t * tile + b + tile + CONV_HALO, :]


def _conv_tap(views_ref, offset, tile):
    return views_ref[offset % 8, 8 * (offset // 8):8 * (offset // 8) + tile, :]


def _conv_tile(u0_ref, views_ref, w_ref, b_ref, t, tile):
    _conv_windows(u0_ref, views_ref, t, tile)
    acc = jnp.broadcast_to(b_ref[...], (tile, CONV_CH))
    for kk in range(CONV_WIDTH):
        acc = acc + w_ref[kk:kk + 1, :] * _conv_tap(views_ref, kk + CONV_HALO - (CONV_WIDTH - 1), tile)
    return acc


def _conv_fwd(zglu, conv_w, conv_b, ln_g, ln_b, n_b, seq):
    tile = min(256, seq)

    def body(z_ref, w_ref, b_ref, g_ref, bb_ref, o_ref, u1_ref, u0_ref, views_ref):
        _conv_fill_glu(z_ref, u0_ref, seq, tile)
        for t in range(seq // tile):
            u1 = _conv_tile(u0_ref, views_ref, w_ref, b_ref, t, tile)
            u1_ref[0, t * tile:(t + 1) * tile, :] = u1
            o_ref[0, t * tile:(t + 1) * tile, :] = _ln_silu(u1, g_ref[...], bb_ref[...]).astype(BF16)

    whole2 = lambda arr: pl.BlockSpec(arr.shape, lambda b: (0, 0))
    seq_spec = pl.BlockSpec((1, seq, CONV_CH), lambda b: (b, 0, 0))
    return pl.pallas_call(
        body, name="conv_fwd", grid=(n_b,),
        in_specs=[pl.BlockSpec((1, seq, 2 * CONV_CH), lambda b: (b, 0, 0)), whole2(conv_w), whole2(conv_b),
                  whole2(ln_g), whole2(ln_b)],
        out_specs=[seq_spec, seq_spec],
        out_shape=[jax.ShapeDtypeStruct((n_b, seq, CONV_CH), BF16), jax.ShapeDtypeStruct((n_b, seq, CONV_CH), F32)],
        scratch_shapes=[pltpu.VMEM((seq + CONV_HALO + CONV_TAIL, CONV_CH), F32),
                        pltpu.VMEM((8, tile + CONV_HALO, CONV_CH), F32)],
        compiler_params=_params(("parallel",)),
    )(zglu, conv_w, conv_b, ln_g, ln_b)


def _conv_bwd(zglu, u1_saved, du3, conv_w, ln_g, ln_b, n_b, seq):
    tile = min(256, seq)
    n_t = seq // tile

    def body(z_ref, u1_ref, du3_ref, w_ref, g_ref, bb_ref, dz_ref, dw_ref, db_ref, dg_ref, dbb_ref, u0_ref, du1_ref,
             u0_views, du1_views):
        @pl.when(pl.program_id(0) == 0)
        def _():
            for r in (dw_ref, db_ref, dg_ref, dbb_ref):
                r[...] = jnp.zeros_like(r)

        _conv_fill_glu(z_ref, u0_ref, seq, tile)
        du1_ref[seq:seq + CONV_HALO + CONV_TAIL, :] = jnp.zeros((CONV_HALO + CONV_TAIL, CONV_CH), F32)
        g = g_ref[...]
        for t in range(n_t):
            u1 = u1_ref[0, t * tile:(t + 1) * tile, :]
            mu = jnp.mean(u1, axis=-1, keepdims=True)
            uc = u1 - mu
            r = lax.rsqrt(jnp.mean(uc * uc, axis=-1, keepdims=True) + EPS)
            xh = uc * r
            y = xh * g + bb_ref[...]
            sg = _sigmoid(y)
            dy = du3_ref[0, t * tile:(t + 1) * tile, :].astype(F32) * (sg * (1.0 + y * (1.0 - sg)))
            dg_ref[...] += jnp.sum(dy * xh, axis=0, keepdims=True)
            dbb_ref[...] += jnp.sum(dy, axis=0, keepdims=True)
            dxh = dy * g
            du1 = r * (dxh - jnp.mean(dxh, axis=-1, keepdims=True) - xh * jnp.mean(dxh * xh, axis=-1, keepdims=True))
            db_ref[...] += jnp.sum(du1, axis=0, keepdims=True)
            du1_ref[t * tile:(t + 1) * tile, :] = du1
        for t in range(n_t):
            du1 = du1_ref[t * tile:(t + 1) * tile, :]
            du0 = jnp.zeros((tile, CONV_CH), F32)
            _conv_windows(u0_ref, u0_views, t, tile)
            _conv_windows(du1_ref, du1_views, t, tile)
            for kk in range(CONV_WIDTH):
                du0 = du0 + w_ref[kk:kk + 1, :] * _conv_tap(du1_views, CONV_WIDTH - 1 - kk, tile)
                u0_tap = _conv_tap(u0_views, kk + CONV_HALO - (CONV_WIDTH - 1), tile)
                dw_ref[kk:kk + 1, :] += jnp.sum(du1 * u0_tap, axis=0, keepdims=True)
            zt = z_ref[0, t * tile:(t + 1) * tile, :].astype(F32)
            ga, sb = zt[:, :CONV_CH], _sigmoid(zt[:, CONV_CH:])
            dz_ref[0, t * tile:(t + 1) * tile, :CONV_CH] = (du0 * sb).astype(BF16)
            dz_ref[0, t * tile:(t + 1) * tile, CONV_CH:] = (du0 * ga * sb * (1.0 - sb)).astype(BF16)

    whole2 = lambda arr: pl.BlockSpec(arr.shape, lambda b: (0, 0))
    z_spec = pl.BlockSpec((1, seq, 2 * CONV_CH), lambda b: (b, 0, 0))
    seq_spec = pl.BlockSpec((1, seq, CONV_CH), lambda b: (b, 0, 0))
    return pl.pallas_call(
        body, name="conv_bwd", grid=(n_b,),
        in_specs=[z_spec, seq_spec, seq_spec, whole2(conv_w), whole2(ln_g), whole2(ln_b)],
        out_specs=[z_spec, whole2(conv_w), whole2(ln_g), whole2(ln_g), whole2(ln_b)],
        out_shape=[jax.ShapeDtypeStruct((n_b, seq, 2 * CONV_CH), BF16), jax.ShapeDtypeStruct(conv_w.shape, F32),
                   jax.ShapeDtypeStruct(ln_g.shape, F32), jax.ShapeDtypeStruct(ln_g.shape, F32),
                   jax.ShapeDtypeStruct(ln_b.shape, F32)],
        scratch_shapes=[pltpu.VMEM((seq + CONV_HALO + CONV_TAIL, CONV_CH), F32)] * 2
        + [pltpu.VMEM((8, tile + CONV_HALO, CONV_CH), F32)] * 2,
        compiler_params=_params(("arbitrary",)),
    )(zglu, u1_saved, du3, conv_w, ln_g, ln_b)


def _sum_parts(name, parts):
    n_parts = parts.shape[0]

    def body(p_ref, o_ref):
        gg = p_ref[0].astype(F32)
        for j in range(1, n_parts):
            gg = gg + p_ref[j].astype(F32)
        o_ref[...] = gg

    return pl.pallas_call(body, name=name, out_shape=jax.ShapeDtypeStruct(parts.shape[1:], F32),
                          compiler_params=_params(None))(parts)


def _adamw(name, w, parts, m, v, transposed=False):
    n_parts = parts.shape[0]
    rows, cols = w.shape
    tr = ADAM_ROWS if rows % ADAM_ROWS == 0 else rows

    def body(w_ref, p_ref, m_ref, v_ref, g_ref, d_ref, nm_ref, nv_ref):
        gg = p_ref[0].astype(F32)
        for j in range(1, n_parts):
            gg = gg + p_ref[j].astype(F32)
        if transposed:
            gg = gg.T
        nm = ADAM_B1 * m_ref[...] + (1.0 - ADAM_B1) * gg
        nv = ADAM_B2 * v_ref[...] + (1.0 - ADAM_B2) * jnp.square(gg)
        m_hat = nm / (1.0 - ADAM_B1 ** ADAM_STEP)
        v_hat = nv / (1.0 - ADAM_B2 ** ADAM_STEP)
        g_ref[...] = gg
        d_ref[...] = -ADAM_LR * (m_hat / (jnp.sqrt(v_hat) + ADAM_EPS) + ADAM_WD * w_ref[...])
        nm_ref[...] = nm
        nv_ref[...] = nv

    shape = jax.ShapeDtypeStruct(w.shape, F32)
    blk = pl.BlockSpec((tr, cols), lambda i: (i, 0))
    p_spec = (pl.BlockSpec((n_parts, cols, tr), lambda i: (0, 0, i)) if transposed
              else pl.BlockSpec((n_parts, tr, cols), lambda i: (0, i, 0)))
    return pl.pallas_call(body, name=name, grid=(rows // tr,), in_specs=[blk, p_spec, blk, blk], out_specs=[blk] * 4,
                          out_shape=[shape] * 4, compiler_params=_params(("parallel",)))(w, parts, m, v)


def _rope_tables(seq):
    inv_freq = ROPE_THETA ** (-jnp.arange(0, QK_ROPE_DIM, 2, dtype=F32) / QK_ROPE_DIM)
    ang = jnp.arange(seq, dtype=F32)[:, None] * inv_freq[None, :]
    cos, sin = jnp.cos(ang), jnp.sin(ang)
    half = QK_ROPE_DIM // 2
    lane_half = HEAD_PAD // 2
    one = lambda n: jnp.ones((seq, n), F32)
    z = lambda n: jnp.zeros((seq, n), F32)
    used_hi = QK_HEAD_DIM - lane_half - half
    cos_t = jnp.concatenate([cos, one(lane_half - half), cos, one(used_hi), z(lane_half - half - used_hi)], axis=1)
    sin_t = jnp.concatenate([-sin, z(lane_half - half), sin, z(lane_half - half)], axis=1)
    return cos_t, sin_t


def _pad_lanes(v, width=HEAD_PAD):
    return jnp.pad(v, [(0, 0)] * (v.ndim - 1) + [(0, width - v.shape[-1])])


_LANE_HALF_NOPE = HEAD_PAD // 2 - QK_ROPE_DIM // 2


def _head_lanes(v):
    rot = v[..., QK_NOPE_DIM:]
    half = QK_ROPE_DIM // 2
    return _pad_lanes(jnp.concatenate([rot[..., :half], v[..., :_LANE_HALF_NOPE], rot[..., half:],
                                       v[..., _LANE_HALF_NOPE:QK_NOPE_DIM]], axis=-1))


def _head_dims(g):
    half = QK_ROPE_DIM // 2
    lane_half = HEAD_PAD // 2
    return jnp.concatenate([g[..., half:lane_half], g[..., lane_half + half:QK_HEAD_DIM], g[..., :half],
                            g[..., lane_half:lane_half + half]], axis=-1)


def _unstack_cols(s):
    return s.transpose(1, 0, 2).reshape(s.shape[1], N_DEV * s.shape[2])


def _stack_cols(g, dtype):
    rows, cols = g.shape
    return g.reshape(rows, N_DEV, cols // N_DEV).transpose(1, 0, 2).astype(dtype)


def kernel(x, c, w_ada, b_ada, norm1_g, w_in, q_latent_g, w_uq, kv_latent_g, w_ukv, qk_norm_q_g, qk_norm_k_g, w_o_mla, conv_w, conv_b, conv_ln_g, conv_ln_b, w_pw_out, w_out, norm2_g, w_ff1, w_ff2, loss_target, m_w_ada, m_b_ada, m_norm1_g, m_w_in, m_q_latent_g, m_w_uq, m_kv_latent_g, m_w_ukv, m_qk_norm_q_g, m_qk_norm_k_g, m_w_o_mla, m_conv_w, m_conv_b, m_conv_ln_g, m_conv_ln_b, m_w_pw_out, m_w_out, m_norm2_g, m_w_ff1, m_w_ff2, v_w_ada, v_b_ada, v_norm1_g, v_w_in, v_q_latent_g, v_w_uq, v_kv_latent_g, v_w_ukv, v_qk_norm_q_g, v_qk_norm_k_g, v_w_o_mla, v_conv_w, v_conv_b, v_conv_ln_g, v_conv_ln_b, v_w_pw_out, v_w_out, v_norm2_g, v_w_ff1, v_w_ff2):
    given = dict(locals())
    local = {n: given[n][0] for n in WEIGHTS}
    vec = {n: local[n].reshape(1, -1) for n in REPLICATED}
    bf = lambda n: local[n].astype(BF16)
    n_b, seq, d = x.shape
    n_rows = n_b * seq
    x2 = x.reshape(n_rows, d)
    t2 = loss_target.reshape(n_rows, d)
    me = 4 * lax.axis_index("x") + 2 * lax.axis_index("y") + lax.axis_index("c")
    ada_cols = local["w_ada"].shape[1]

    tsh = lambda n: local[n].T.astype(BF16)
    c_all, w_in_s, w_uq_s, w_ukv_s, conv_w_s = _exchange(
        "gather_early", [(c, True), (tsh("w_in"), True), (bf("w_uq"), True), (bf("w_ukv"), True), (local["conv_w"], True)],
        by_chip=True)
    w_in_t = w_in_s.reshape(-1, d)
    zrows = lambda n: jnp.zeros((n, d), BF16)
    rot_half = QK_ROPE_DIM // 2
    w_sm_t = jnp.concatenate([w_in_t[:OFF_KV + rot_half], zrows(HEAD_PAD // 2 - rot_half), w_in_t[OFF_KV + rot_half:OFF_KR],
                              zrows(HEAD_PAD // 2 - rot_half)], axis=0)
    w_glu_t = w_in_t[OFF_KR:OFF_GLU]
    w_gate_t = w_in_t[OFF_GLU:]
    wuq = _head_lanes(_unstack_cols(w_uq_s).reshape(Q_LORA, N_HEADS, QK_HEAD_DIM)).reshape(Q_LORA, N_HEADS * HEAD_PAD)
    wukv_f = _unstack_cols(w_ukv_s).reshape(KV_LORA, N_HEADS, QK_NOPE_DIM + V_HEAD_DIM)
    wv = wukv_f[:, :, QK_NOPE_DIM:]
    odd = (jnp.arange(N_HEADS) % 2 == 1)[None, :, None]
    wuv = jnp.where(odd, jnp.pad(wv, ((0, 0), (0, 0), (V_HEAD_DIM, 0))), jnp.pad(wv, ((0, 0), (0, 0), (0, V_HEAD_DIM))))
    wuk = _head_lanes(_pad_lanes(wukv_f[:, :, :QK_NOPE_DIM], QK_HEAD_DIM))
    wukv = jnp.concatenate([wuk, wuv], axis=1).reshape(KV_LORA, 2 * N_HEADS * HEAD_PAD)
    gqn = _head_lanes(vec["qk_norm_q_g"])
    gkn = _head_lanes(vec["qk_norm_k_g"])
    conv_w_f = jnp.pad(_unstack_cols(conv_w_s), ((0, 1), (0, 0)))
    rope = _rope_tables(seq)

    all_rows = N_DEV * n_b
    pad_rows = (-all_rows) % ROWS_PAD
    c_rows = jnp.pad(c_all.reshape(all_rows, d), ((0, pad_rows), (0, 0)))
    b_cols = lax.dynamic_slice(local["b_ada"], (me * ada_cols,), (ada_cols,))
    mod_cols = _mm("ada_fwd", c_rows, local["w_ada"], "nn", F32, a_fn=_silu, epi=lambda acc, b: acc + b,
                   epi_in=(jnp.broadcast_to(b_cols, (all_rows + pad_rows, ada_cols)),))
    (mod_s,) = _exchange("scatter_mod", [(mod_cols[:all_rows].reshape(N_DEV, n_b, ada_cols), False)])
    mod = mod_s.transpose(1, 0, 2).reshape(n_b, ADA_CHUNKS, 1, d)
    shift1, scale1, gate1, shift2, scale2, gate2 = [mod[:, i] for i in range(ADA_CHUNKS)]

    h, zgate, zglu, zsm = _in_proj_fwd(x2, scale1, shift1, vec["norm1_g"], [w_gate_t, w_glu_t, w_sm_t],
                                       [BF16, BF16, F32], seq)
    q, k, v, kt = _mla_prep_fwd(zsm, wuq, wukv, vec["q_latent_g"], vec["kv_latent_g"], gqn, gkn, rope, n_b, seq)
    attn, lse, (w_o_s, w_pw_s, w_out_s, w_ff1_s, w_ff2_s) = _attn_fwd(
        q, k, v, [(tsh("w_o_mla"), True), (tsh("w_pw_out"), True), (bf("w_out"), True), (tsh("w_ff1"), True),
                  (bf("w_ff2"), True)], n_b, seq)
    w_o_t = w_o_s.reshape(d, -1)
    w_pw_t = w_pw_s.reshape(d, -1)
    w_out_f = w_out_s.reshape(d, d)
    w_ff1_t = w_ff1_s.reshape(-1, d)
    w_ff2_f = w_ff2_s.reshape(-1, d)
    attn2 = attn.reshape(n_rows, N_HEADS * V_HEAD_DIM)
    u3, u1 = _conv_fwd(zglu.reshape(n_b, seq, 2 * CONV_CH), conv_w_f, vec["conv_b"], vec["conv_ln_g"], vec["conv_ln_b"], n_b, seq)
    u32 = u3.reshape(n_rows, CONV_CH)
    ya = _mm("mla_out", attn2, w_o_t, "nt", BF16)
    yb = _mm("conv_out", u32, w_pw_t, "nt", BF16)
    mmr = functools.partial(_mm_rows, n_rows=n_rows, seq=seq)

    def merge_fn(t):
        return _sigmoid(t[0]) * t[2] + _sigmoid(t[1]) * t[3]

    def mid_fn(acc, r, b, cc):
        x1_ = r[0] + b[0] * acc
        return [acc, x1_, _norm_mod(x1_, cc[0], b[1], b[2])], [], []

    mrg, mixed, x1, h2 = mmr("out_proj", [(zgate, d, 0), (zgate, d, 1), (ya, d, 0), (yb, d, 0)], merge_fn, w_out_f, "nn",
                             mid_fn, rows=[_full(x2)], bats=[gate1, scale2, shift2], consts=[vec["norm2_g"]],
                             outs=[(d, BF16), (d, F32), (d, BF16)], a_out=BF16)

    a = _mm("ff1", h2, w_ff1_t, "nt", BF16)

    def loss_fn(ff, r, b, cc):
        err = r[0] + b[0] * ff - r[1]
        dy_ = err * (1.0 / d)
        sq = jnp.broadcast_to(jnp.sum(err * err, keepdims=True), (1, LANES))
        return [dy_, b[0] * dy_], [jnp.sum(dy_ * ff, axis=0, keepdims=True)], [sq]

    dy, df, dgate2, sq_err = mmr("ff2_loss", [(a, a.shape[1], 0)], lambda t: _relu2(t[0]), w_ff2_f, "nn", loss_fn,
                                 rows=[_full(x1), _full(t2)], bats=[gate2], outs=[(d, F32), (d, BF16)], bat_outs=[d],
                                 tot_outs=[(1, LANES)], tk=a.shape[1])

    da = _mm("ff2_bwd", df, w_ff2_f, "nt", BF16, epi=lambda acc, av: acc * 2.0 * jnp.maximum(av, 0.0), epi_in=(a,))
    g_ff2 = _mm("ff2_dw", a, df, "tn", BF16, a_fn=_relu2)
    g_ff1_t = _mm("ff1_dw", da, h2, "tn", BF16)

    def mid_bwd(dh2_, r, b, cc):
        dx, dsc, dsh, dg = _norm_mod_bwd(r[0], cc[0], b[0], dh2_)
        dx1_ = r[1] + dx
        return [dx1_, b[1] * dx1_], [dsc, dsh, jnp.sum(dx1_ * r[2].astype(F32), axis=0, keepdims=True)], [dg]

    dx1, dmixed, dscale2, dshift2, dgate1, g_norm2 = mmr(
        "ff1_bwd", [(da, da.shape[1], 0)], None, w_ff1_t, "nn", mid_bwd, rows=[_full(x1), _full(dy), _full(mixed)],
        bats=[scale2, gate1], consts=[vec["norm2_g"]], outs=[(d, F32), (d, BF16)], bat_outs=[d, d, d],
        tot_outs=[(1, d)], tk=da.shape[1])

    g_out = _mm("out_proj_dw", mrg, dmixed, "tn", BF16)

    def merge_bwd(dm, r, b, cc):
        ya_, yb_ = r[2].astype(F32), r[3].astype(F32)
        sa, sb = _sigmoid(r[0].astype(F32)), _sigmoid(r[1].astype(F32))
        return [dm * ya_ * sa * (1.0 - sa), dm * yb_ * sb * (1.0 - sb), dm * sa, dm * sb], [], []

    dzga, dzgb, dya, dyb = mmr("out_proj_bwd", [(dmixed, d, 0)], None, w_out_f, "nt", merge_bwd,
                               rows=[(zgate, d, 0), (zgate, d, 1), _full(ya), _full(yb)], outs=[(d, BF16)] * 4)
    dattn = _mm("mla_out_bwd", dya, w_o_t, "nn", BF16)
    g_o_t = _mm("mla_out_dw", dya, attn2, "tn", BF16)
    du3 = _mm("conv_out_bwd", dyb, w_pw_t, "nn", BF16)
    g_pw_t = _mm("conv_out_dw", dyb, u32, "tn", BF16)

    dzglu, g_conv_w, g_conv_b, g_ln_g, g_ln_b = _conv_bwd(
        zglu.reshape(n_b, seq, 2 * CONV_CH), u1, du3.reshape(n_b, seq, CONV_CH), conv_w_f, vec["conv_ln_g"],
        vec["conv_ln_b"], n_b, seq)
    dzglu = dzglu.reshape(n_rows, 2 * CONV_CH)

    dq, dk, dv, (p_ff2, p_ff1, p_out, p_pw, p_o) = _attn_bwd(
        q, k, v, kt, dattn.reshape(n_b, seq, N_HEADS * V_HEAD_DIM), attn, lse,
        [(g_ff2.reshape(N_DEV, -1, d), False), (g_ff1_t.reshape(N_DEV, -1, d), False), (g_out.reshape(N_DEV, -1, d), False),
         (g_pw_t.reshape(N_DEV, -1, CONV_CH), False), (g_o_t.reshape(N_DEV, -1, N_HEADS * V_HEAD_DIM), False)], n_b, seq)
    dzsm, g_wuq, g_wukv, g_gq, g_gkv, g_gqn, g_gkn = _mla_prep_bwd(
        zsm, dq, dk, dv, wuq, wukv, vec["q_latent_g"], vec["kv_latent_g"], gqn, gkn, rope, n_b, seq)

    g_gate_a_t = _mm("in_proj_gate_dw_a", dzga, h, "tn", BF16)
    g_gate_b_t = _mm("in_proj_gate_dw_b", dzgb, h, "tn", BF16)
    g_glu_t = _mm("in_proj_glu_dw", dzglu, h, "tn", BF16)
    g_sm_t = _mm("in_proj_sm_dw", dzsm, h, "tn", BF16)
    g_in_t = jnp.concatenate([g_sm_t[:OFF_KV + rot_half], g_sm_t[OFF_KV + HEAD_PAD // 2:OFF_KV + HEAD_PAD // 2 + rot_half],
                              g_glu_t, g_gate_a_t, g_gate_b_t], axis=0)
    g_uq = _head_dims(g_wuq.reshape(Q_LORA, N_HEADS, HEAD_PAD)).reshape(Q_LORA, N_HEADS * QK_HEAD_DIM)
    g_wukv = g_wukv.reshape(KV_LORA, 2, N_HEADS, HEAD_PAD)
    g_v = jnp.where(odd, g_wukv[:, 1, :, V_HEAD_DIM:], g_wukv[:, 1, :, :V_HEAD_DIM])
    g_ukv = jnp.concatenate([_head_dims(g_wukv[:, 0])[:, :, :QK_NOPE_DIM], g_v], axis=2).reshape(KV_LORA, -1)

    grad_x, dscale1, dshift1, g_norm1, (p_in, p_uq, p_ukv, p_conv_w) = _in_proj_bwd(
        [(dzga, w_gate_t[:d]), (dzgb, w_gate_t[d:]), (dzglu, w_glu_t), (dzsm, w_sm_t)], x2, dx1, scale1, vec["norm1_g"],
        [(g_in_t.reshape(N_DEV, -1, d), False), (_stack_cols(g_uq, BF16), False), (_stack_cols(g_ukv, BF16), False),
         (_stack_cols(g_conv_w[:CONV_WIDTH], F32), False)], seq)

    dmod = jnp.concatenate([dshift1, dscale1, dgate1, dshift2, dscale2, dgate2], axis=1).reshape(n_b, N_DEV, ada_cols)
    (dmod_s,) = _exchange("scatter_dmod", [(dmod.transpose(1, 0, 2), False)])
    dmod_rows = jnp.pad(dmod_s.reshape(all_rows, ada_cols), ((0, pad_rows), (0, 0)))
    g_ada = _mm("ada_dw", c_rows, dmod_rows, "tn", F32, a_fn=_silu)
    (g_b_cols,) = _rowwise("ada_db", lambda r, b, cc: ([], [], [jnp.sum(r[0], axis=0, keepdims=True)]),
                           all_rows + pad_rows, all_rows + pad_rows, rows=[_full(dmod_rows)], tot_outs=[(1, ada_cols)])

    partial_of = {"norm1_g": g_norm1, "q_latent_g": g_gq, "kv_latent_g": g_gkv, "qk_norm_q_g": _head_dims(g_gqn),
                  "qk_norm_k_g": _head_dims(g_gkn), "conv_b": g_conv_b, "conv_ln_g": g_ln_g, "conv_ln_b": g_ln_b, "norm2_g": g_norm2}
    names = [n for n in REPLICATED if n != "b_ada"]
    pieces = [_pad_lanes(partial_of[n], -(-partial_of[n].shape[1] // LANES) * LANES) for n in names] + [g_b_cols, sq_err]
    widths = [p.shape[1] for p in pieces]
    small = jnp.concatenate(pieces, axis=1)
    small = _pad_lanes(small, -(-small.shape[1] // (8 * LANES)) * 8 * LANES).reshape(-1, LANES)
    (small_s,) = _exchange("gather_small_grads", [(small, True)])
    small_s = small_s.reshape(N_DEV, 1, -1)
    parts = {}
    off = 0
    for n, wd in zip(names, widths):
        parts[n] = small_s[:, :, off:off + vec[n].shape[1]]
        off += wd
    parts["b_ada"] = small_s[:, 0, off:off + ada_cols].reshape(1, 1, N_DEV * ada_cols)
    loss = jnp.sum(small_s[:, 0, off + ada_cols]) * (0.5 / d)
    g_in_mine = _sum_parts("sum_w_in", p_in).T
    parts.update({"w_ada": g_ada[None], "w_in": g_in_mine[None], "w_uq": p_uq, "w_ukv": p_ukv, "w_o_mla": p_o,
                  "conv_w": p_conv_w, "w_pw_out": p_pw, "w_out": p_out, "w_ff1": p_ff1, "w_ff2": p_ff2})
    transposed = ("w_o_mla", "w_pw_out", "w_ff1")

    grad_out, delta_out, m_out, v_out = [], [], [], []
    for n in WEIGHTS:
        shape2 = local[n].shape if local[n].ndim == 2 else (1, local[n].shape[0])
        g_w, d_w, n_m, n_v = _adamw("adamw_" + n, local[n].reshape(shape2), parts[n], given["m_" + n].reshape(shape2),
                                    given["v_" + n].reshape(shape2), transposed=n in transposed)
        full_shape = given[n].shape
        grad_out.append(g_w.reshape(full_shape))
        delta_out.append(d_w.reshape(full_shape))
        m_out.append(n_m.reshape(full_shape))
        v_out.append(n_v.reshape(full_shape))
    return (loss, grad_x.reshape(n_b, seq, d), *grad_out, *delta_out, *m_out, *v_out)
```

```python
import functools

import jax
import jax.numpy as jnp
from jax import lax
from jax.experimental import pallas as pl
from jax.experimental.pallas import tpu as pltpu

F32 = jnp.float32
BF16 = jnp.bfloat16

N_DEV = 8
EPS = 1e-6
N_HEADS = 8
QK_HEAD_DIM = 96
QK_NOPE_DIM = 64
QK_ROPE_DIM = 32
V_HEAD_DIM = 64
HEAD_PAD = 128
Q_LORA = 256
KV_LORA = 128
CONV_CH = 512
CONV_WIDTH = 31
CONV_HALO = 32
CONV_TAIL = 8
CHUNK = 64
ROPE_THETA = 10000.0
OFF_Q = Q_LORA
OFF_KV = OFF_Q + KV_LORA
OFF_KR = OFF_KV + QK_ROPE_DIM
OFF_GLU = OFF_KR + 2 * CONV_CH
ADA_CHUNKS = 6
ADAM_LR = 0.001
ADAM_B1 = 0.9
ADAM_B2 = 0.999
ADAM_EPS = 1e-08
ADAM_WD = 0.01
ADAM_STEP = 10
LANES = 128
VMEM_LIMIT = 56 * 1024 * 1024
NEG_BIG = -1e30
ATT_HEADS = 4
ATT_HEADS_FWD = 8
ATT_TILE = 512
PREP_TILE = 512
ATT_SCALE = QK_HEAD_DIM ** -0.5
LOG2E = 1.4426950408889634
LN2 = 0.6931471805599453
QK_SCALE = ATT_SCALE * LOG2E
ADAM_ROWS = 256
ROWS_PAD = 16

REPLICATED = ("b_ada", "norm1_g", "q_latent_g", "kv_latent_g", "qk_norm_q_g", "qk_norm_k_g", "conv_b", "conv_ln_g",
              "conv_ln_b", "norm2_g")
WEIGHTS = ("w_ada", "b_ada", "norm1_g", "w_in", "q_latent_g", "w_uq", "kv_latent_g", "w_ukv", "qk_norm_q_g",
           "qk_norm_k_g", "w_o_mla", "conv_w", "conv_b", "conv_ln_g", "conv_ln_b", "w_pw_out", "w_out", "norm2_g",
           "w_ff1", "w_ff2")


def _tile(dim, pref):
    if dim <= pref:
        return dim
    t = (pref // LANES) * LANES
    while dim % t:
        t -= LANES
    return t


def _params(semantics):
    return pltpu.CompilerParams(dimension_semantics=semantics, vmem_limit_bytes=VMEM_LIMIT)


def _sigmoid(v):
    return 1.0 / (1.0 + jnp.exp(-v))


def _silu(v):
    return v * _sigmoid(v)


def _relu2(v):
    return jnp.square(jnp.maximum(v, 0.0))


_DIMS = {"nn": (((1,), (0,)), ((), ())), "nt": (((1,), (1,)), ((), ())), "tn": (((0,), (0,)), ((), ()))}


def _mm(name, a, b, mode, out_dtype, *, a_fn=None, epi=None, epi_in=(), tm=1024, tn=1024, tk=1024):
    if mode == "nn":
        (m, k), n = a.shape, b.shape[1]
    elif mode == "nt":
        (m, k), n = a.shape, b.shape[0]
    else:
        (k, m), n = a.shape, b.shape[1]
    tm, tn, tk = _tile(m, tm), _tile(n, tn), _tile(k, tk)
    nk = k // tk
    a_spec = (pl.BlockSpec((tk, tm), lambda i, j, kk: (kk, i)) if mode == "tn"
              else pl.BlockSpec((tm, tk), lambda i, j, kk: (i, kk)))
    b_spec = (pl.BlockSpec((tn, tk), lambda i, j, kk: (j, kk)) if mode == "nt"
              else pl.BlockSpec((tk, tn), lambda i, j, kk: (kk, j)))
    o_spec = e_spec = pl.BlockSpec((tm, tn), lambda i, j, kk: (i, j))
    out_shape = jax.ShapeDtypeStruct((m, n), out_dtype)
    n_epi = len(epi_in)

    def body(a_ref, b_ref, *rest):
        epi_refs, o_ref, acc_ref = rest[:n_epi], rest[n_epi], rest[n_epi + 1]
        kk = pl.program_id(2)

        @pl.when(kk == 0)
        def _():
            acc_ref[...] = jnp.zeros_like(acc_ref)

        av = a_ref[...]
        if a_fn is not None:
            av = a_fn(av.astype(F32))
        acc_ref[...] += lax.dot_general(av.astype(BF16), b_ref[...].astype(BF16), _DIMS[mode],
                                        preferred_element_type=F32)

        @pl.when(kk == nk - 1)
        def _():
            acc = acc_ref[...]
            if epi is not None:
                acc = epi(acc, *[r[...].astype(F32) for r in epi_refs])
            o_ref[...] = acc.astype(out_dtype)

    return pl.pallas_call(
        body, name=name, grid=(m // tm, n // tn, nk),
        in_specs=[a_spec, b_spec] + [e_spec] * n_epi, out_specs=o_spec, out_shape=out_shape,
        scratch_shapes=[pltpu.VMEM((tm, tn), F32)],
        compiler_params=_params(("parallel", "parallel", "arbitrary")),
    )(a, b, *epi_in)


def _rowwise(name, fn, n_rows, seq, rows, bats=(), consts=(), outs=(), bat_outs=(), tot_outs=(), tm=256):
    tm = min(tm, seq)
    per_seq = seq // tm
    n_b = n_rows // seq
    nr, nb, nc, no, nbo, nto = len(rows), len(bats), len(consts), len(outs), len(bat_outs), len(tot_outs)

    def body(*refs):
        i = pl.program_id(0)
        r_in = [r[...] for r in refs[:nr]]
        b_in = [r[0] for r in refs[nr:nr + nb]]
        c_in = [r[...] for r in refs[nr + nb:nr + nb + nc]]
        o_refs = refs[nr + nb + nc:nr + nb + nc + no]
        bo_refs = refs[nr + nb + nc + no:nr + nb + nc + no + nbo]
        to_refs = refs[nr + nb + nc + no + nbo:]
        o_val, bo_val, to_val = fn(r_in, b_in, c_in)
        for r, v in zip(o_refs, o_val):
            r[...] = v.astype(r.dtype)
        if nbo:
            @pl.when(i % per_seq == 0)
            def _():
                for r in bo_refs:
                    r[...] = jnp.zeros_like(r)

            for r, v in zip(bo_refs, bo_val):
                r[0] += v
        if nto:
            @pl.when(i == 0)
            def _():
                for r in to_refs:
                    r[...] = jnp.zeros_like(r)

            for r, v in zip(to_refs, to_val):
                r[...] += v

    in_specs = [pl.BlockSpec((tm, w), functools.partial(lambda cb, i: (i, cb), cb)) for (_, w, cb) in rows]
    in_specs += [pl.BlockSpec((1, 1, bt.shape[2]), lambda i: (i // per_seq, 0, 0)) for bt in bats]
    in_specs += [pl.BlockSpec(ct.shape, lambda i: (0, 0)) for ct in consts]
    out_specs = [pl.BlockSpec((tm, w), lambda i: (i, 0)) for (w, _) in outs]
    out_specs += [pl.BlockSpec((1, 1, w), lambda i: (i // per_seq, 0, 0)) for w in bat_outs]
    out_specs += [pl.BlockSpec(shp, lambda i: (0, 0)) for shp in tot_outs]
    out_shape = [jax.ShapeDtypeStruct((n_rows, w), dt) for (w, dt) in outs]
    out_shape += [jax.ShapeDtypeStruct((n_b, 1, w), F32) for w in bat_outs]
    out_shape += [jax.ShapeDtypeStruct(shp, F32) for shp in tot_outs]
    res = pl.pallas_call(
        body, name=name, grid=(n_rows // tm,), in_specs=in_specs, out_specs=out_specs, out_shape=out_shape,
        compiler_params=_params(("arbitrary",)),
    )(*[r[0] for r in rows], *bats, *consts)
    return res


def _full(arr):
    return (arr, arr.shape[1], 0)


def _mm_rows(name, a_rows, a_fn, w, mode, fn, n_rows, seq, rows=(), bats=(), consts=(), outs=(), bat_outs=(),
             tot_outs=(), a_out=None, tm=512, tk=1024):
    tm = min(tm, seq)
    per_seq = seq // tm
    n_b = n_rows // seq
    k = a_rows[0][1]
    if mode == "nt":
        n_out, tk = w.shape[0], _tile(k, tk)
        w_spec = pl.BlockSpec((n_out, tk), lambda i, kk: (0, kk))
    else:
        n_out, tk = w.shape[1], _tile(k, tk)
        w_spec = pl.BlockSpec((tk, n_out), lambda i, kk: (kk, 0))
    nk = k // tk
    na, nr, nb, nc = len(a_rows), len(rows), len(bats), len(consts)
    n_extra = 0 if a_out is None else 1
    no, nbo, nto = len(outs), len(bat_outs), len(tot_outs)

    def body(*refs):
        i, kk = pl.program_id(0), pl.program_id(1)
        a_refs, w_ref = refs[:na], refs[na]
        pos = na + 1
        r_refs, b_refs, c_refs = refs[pos:pos + nr], refs[pos + nr:pos + nr + nb], refs[pos + nr + nb:pos + nr + nb + nc]
        pos += nr + nb + nc
        ao_refs = refs[pos:pos + n_extra]
        pos += n_extra
        o_refs, bo_refs, to_refs = refs[pos:pos + no], refs[pos + no:pos + no + nbo], refs[pos + no + nbo:pos + no + nbo + nto]
        acc_ref = refs[pos + no + nbo + nto]

        @pl.when(kk == 0)
        def _():
            acc_ref[...] = jnp.zeros_like(acc_ref)

        tiles = [r[...] for r in a_refs]
        av = a_fn([t.astype(F32) for t in tiles]) if a_fn is not None else tiles[0]
        av = av.astype(BF16)
        if n_extra:
            ao_refs[0][...] = av.astype(ao_refs[0].dtype)
        acc_ref[...] += lax.dot_general(av, w_ref[...].astype(BF16), _DIMS[mode], preferred_element_type=F32)

        @pl.when(kk == nk - 1)
        def _():
            o_val, bo_val, to_val = fn(acc_ref[...], [r[...] for r in r_refs], [r[0] for r in b_refs],
                                       [r[...] for r in c_refs])
            for r, v in zip(o_refs, o_val):
                r[...] = v.astype(r.dtype)
            if nbo:
                @pl.when(i % per_seq == 0)
                def _():
                    for r in bo_refs:
                        r[...] = jnp.zeros_like(r)

                for r, v in zip(bo_refs, bo_val):
                    r[0] += v
            if nto:
                @pl.when(i == 0)
                def _():
                    for r in to_refs:
                        r[...] = jnp.zeros_like(r)

                for r, v in zip(to_refs, to_val):
                    r[...] += v

    in_specs = [pl.BlockSpec((tm, tk), functools.partial(lambda cb, i, kk: (i, kk + cb), cb)) for (_, _, cb) in a_rows]
    in_specs += [w_spec]
    in_specs += [pl.BlockSpec((tm, wd), functools.partial(lambda cb, i, kk: (i, cb), cb)) for (_, wd, cb) in rows]
    in_specs += [pl.BlockSpec((1, 1, bt.shape[2]), lambda i, kk: (i // per_seq, 0, 0)) for bt in bats]
    in_specs += [pl.BlockSpec(ct.shape, lambda i, kk: (0, 0)) for ct in consts]
    out_specs = [pl.BlockSpec((tm, tk), lambda i, kk: (i, kk))] * n_extra
    out_specs += [pl.BlockSpec((tm, wd), lambda i, kk: (i, 0)) for (wd, _) in outs]
    out_specs += [pl.BlockSpec((1, 1, wd), lambda i, kk: (i // per_seq, 0, 0)) for wd in bat_outs]
    out_specs += [pl.BlockSpec(shp, lambda i, kk: (0, 0)) for shp in tot_outs]
    out_shape = [jax.ShapeDtypeStruct((n_rows, k), a_out)] if n_extra else []
    out_shape += [jax.ShapeDtypeStruct((n_rows, wd), dt) for (wd, dt) in outs]
    out_shape += [jax.ShapeDtypeStruct((n_b, 1, wd), F32) for wd in bat_outs]
    out_shape += [jax.ShapeDtypeStruct(shp, F32) for shp in tot_outs]
    return pl.pallas_call(
        body, name=name, grid=(n_rows // tm, nk), in_specs=in_specs, out_specs=out_specs, out_shape=out_shape,
        scratch_shapes=[pltpu.VMEM((tm, n_out), F32)],
        compiler_params=_params(("arbitrary", "arbitrary")),
    )(*[a for a, _, _ in a_rows], w, *[r[0] for r in rows], *bats, *consts)


def _norm_mod(x, g, scale, shift):
    r = lax.rsqrt(jnp.mean(x * x, axis=-1, keepdims=True) + EPS)
    xh = x * r
    return xh * g * (1.0 + scale) + shift


def _norm_mod_bwd(x, g, scale, dh):
    r = lax.rsqrt(jnp.mean(x * x, axis=-1, keepdims=True) + EPS)
    xh = x * r
    dn = dh * (1.0 + scale)
    dxh = dn * g
    dx = r * (dxh - xh * jnp.mean(dxh * xh, axis=-1, keepdims=True))
    dscale = jnp.sum(dh * xh * g, axis=0, keepdims=True)
    dshift = jnp.sum(dh, axis=0, keepdims=True)
    dg = jnp.sum(dn * xh, axis=0, keepdims=True)
    return dx, dscale, dshift, dg


def _rms(v, g):
    r = lax.rsqrt(jnp.mean(v * v, axis=-1, keepdims=True) + EPS)
    return v * r * g


def _rms_bwd(v, g, dy):
    r = lax.rsqrt(jnp.mean(v * v, axis=-1, keepdims=True) + EPS)
    vh = v * r
    dvh = dy * g
    dv = r * (dvh - vh * jnp.mean(dvh * vh, axis=-1, keepdims=True))
    return dv, jnp.sum(dy * vh, axis=0, keepdims=True)


def _lane_sum(t):
    return jnp.dot(t.astype(BF16), jnp.ones((HEAD_PAD, HEAD_PAD), BF16), preferred_element_type=F32)


def _head_norm(v, g):
    r = lax.rsqrt(_lane_sum(v * v) * (1.0 / QK_HEAD_DIM) + EPS)
    return v * r * g


def _head_norm_bwd(v, g, dy):
    r = lax.rsqrt(_lane_sum(v * v) * (1.0 / QK_HEAD_DIM) + EPS)
    vh = v * r
    dvh = dy * g
    dv = r * (dvh - vh * (_lane_sum(dvh * vh) * (1.0 / QK_HEAD_DIM)))
    return dv, jnp.sum(dy * vh, axis=0, keepdims=True)


def _rope(v, cos, sin):
    return v * cos + pltpu.roll(v, HEAD_PAD // 2, 1) * sin


def _rope_bwd(g, cos, sin):
    return g * cos + pltpu.roll(g * sin, HEAD_PAD // 2, 1)


def _mla_prep_fwd(zsm, wuq, wukv, gq, gkv, gqn, gkn, rope, n_b, seq):
    n_rows = n_b * seq
    tm = min(PREP_TILE, seq)
    per_seq = seq // tm
    att_tile = min(ATT_TILE, seq)
    k_cols = N_HEADS * HEAD_PAD

    def body(z_ref, wuq_ref, wukv_ref, gq_ref, gkv_ref, gqn_ref, gkn_ref, c_ref, s_ref, q_ref, k_ref, v_ref, kt_ref):
        z = z_ref[...]
        qn = _rms(z[:, :Q_LORA], gq_ref[...]).astype(BF16)
        kvn = _rms(z[:, Q_LORA:Q_LORA + KV_LORA], gkv_ref[...]).astype(BF16)
        krp = z[:, Q_LORA + KV_LORA:]
        cos, sin = c_ref[...], s_ref[...]
        q_all = jnp.dot(qn, wuq_ref[...], preferred_element_type=F32)
        kv_all = jnp.dot(kvn, wukv_ref[...], preferred_element_type=F32)
        for h in range(N_HEADS):
            cols = slice(h * HEAD_PAD, (h + 1) * HEAD_PAD)
            q_ref[0, h] = (_rope(_head_norm(q_all[:, cols], gqn_ref[...]), cos, sin) * QK_SCALE).astype(BF16)
            kh = _rope(_head_norm(kv_all[:, cols] + krp, gkn_ref[...]), cos, sin)
            k_ref[0, h] = kh.astype(BF16)
            for part in range(tm // att_tile):
                kt_ref[0, h, part] = kh[part * att_tile:(part + 1) * att_tile].T.astype(BF16)
            v_ref[0, h] = kv_all[:, k_cols + h * HEAD_PAD:k_cols + (h + 1) * HEAD_PAD].astype(BF16)

    whole2 = lambda arr: pl.BlockSpec(arr.shape, lambda i: (0, 0))
    rope_spec = pl.BlockSpec((tm, HEAD_PAD), lambda i: (i % per_seq, 0))
    head_spec = pl.BlockSpec((1, N_HEADS, tm, HEAD_PAD), lambda i: (i // per_seq, 0, i % per_seq, 0))
    head_shape = jax.ShapeDtypeStruct((n_b, N_HEADS, seq, HEAD_PAD), BF16)
    t_spec = pl.BlockSpec((1, N_HEADS, tm // att_tile, HEAD_PAD, att_tile), lambda i: (i // per_seq, 0, i % per_seq, 0, 0))
    t_shape = jax.ShapeDtypeStruct((n_b, N_HEADS, seq // att_tile, HEAD_PAD, att_tile), BF16)
    return pl.pallas_call(
        body, name="mla_prep_fwd", grid=(n_rows // tm,),
        in_specs=[pl.BlockSpec((tm, 512), lambda i: (i, 0)), whole2(wuq), whole2(wukv),
                  whole2(gq), whole2(gkv), whole2(gqn), whole2(gkn), rope_spec, rope_spec],
        out_specs=[head_spec] * 3 + [t_spec], out_shape=[head_shape] * 3 + [t_shape],
        compiler_params=_params(("parallel",)),
    )(zsm, wuq, wukv, gq, gkv, gqn, gkn, *rope)


def _mla_prep_bwd(zsm, dq, dk, dv, wuq, wukv, gq, gkv, gqn, gkn, rope, n_b, seq):
    n_rows = n_b * seq
    tm = min(PREP_TILE, seq)
    per_seq = seq // tm
    tn_dims = _DIMS["tn"]
    nt_dims = _DIMS["nt"]
    k_cols = N_HEADS * HEAD_PAD

    def body(z_ref, dq_ref, dk_ref, dv_ref, wuq_ref, wukv_ref, gq_ref, gkv_ref, gqn_ref, gkn_ref,
             c_ref, s_ref, dz_ref, dwuq_ref, dwukv_ref, dgq_ref, dgkv_ref, dgqn_ref, dgkn_ref):
        @pl.when(pl.program_id(0) == 0)
        def _():
            for r in (dwuq_ref, dwukv_ref, dgq_ref, dgkv_ref, dgqn_ref, dgkn_ref):
                r[...] = jnp.zeros_like(r)

        z = z_ref[...]
        zq, zkv, krp = z[:, :Q_LORA], z[:, Q_LORA:Q_LORA + KV_LORA], z[:, Q_LORA + KV_LORA:]
        qn = _rms(zq, gq_ref[...]).astype(BF16)
        kvn = _rms(zkv, gkv_ref[...]).astype(BF16)
        cos, sin = c_ref[...], s_ref[...]
        lane = lax.broadcasted_iota(jnp.int32, (tm, HEAD_PAD), 1)
        rope_lanes = (lane % (HEAD_PAD // 2)) < QK_ROPE_DIM // 2
        q_all = jnp.dot(qn, wuq_ref[...], preferred_element_type=F32)
        k_all = jnp.dot(kvn, wukv_ref[:, :k_cols], preferred_element_type=F32)
        dkrp = jnp.zeros((tm, HEAD_PAD), F32)
        dgqn = jnp.zeros((1, HEAD_PAD), F32)
        dgkn = jnp.zeros((1, HEAD_PAD), F32)
        dq_heads, dk_heads = [], []
        for h in range(N_HEADS):
            cols = slice(h * HEAD_PAD, (h + 1) * HEAD_PAD)
            dqh, dg = _head_norm_bwd(q_all[:, cols], gqn_ref[...],
                                     _rope_bwd(dq_ref[0, h].astype(F32) * ATT_SCALE, cos, sin))
            dgqn += dg
            dq_heads.append(dqh.astype(BF16))
            dkh, dg = _head_norm_bwd(k_all[:, cols] + krp, gkn_ref[...], _rope_bwd(dk_ref[0, h].astype(F32), cos, sin))
            dgkn += dg
            dkrp += jnp.where(rope_lanes, dkh, 0.0)
            dk_heads.append(dkh.astype(BF16))
        dq_all = jnp.concatenate(dq_heads, axis=1)
        dkv_all = jnp.concatenate(dk_heads + [dv_ref[0, h] for h in range(N_HEADS)], axis=1)
        dwuq_ref[...] += lax.dot_general(qn, dq_all, tn_dims, preferred_element_type=F32)
        dqn = lax.dot_general(dq_all, wuq_ref[...], nt_dims, preferred_element_type=F32)
        dwukv_ref[...] += lax.dot_general(kvn, dkv_all, tn_dims, preferred_element_type=F32)
        dkvn = lax.dot_general(dkv_all, wukv_ref[...], nt_dims, preferred_element_type=F32)
        dzq, dg = _rms_bwd(zq, gq_ref[...], dqn)
        dgq_ref[...] += dg
        dzkv, dg = _rms_bwd(zkv, gkv_ref[...], dkvn)
        dgkv_ref[...] += dg
        dgqn_ref[...] += dgqn
        dgkn_ref[...] += dgkn
        dz_ref[:, :Q_LORA] = dzq.astype(dz_ref.dtype)
        dz_ref[:, Q_LORA:Q_LORA + KV_LORA] = dzkv.astype(dz_ref.dtype)
        dz_ref[:, Q_LORA + KV_LORA:] = dkrp.astype(dz_ref.dtype)

    whole2 = lambda arr: pl.BlockSpec(arr.shape, lambda i: (0, 0))
    rope_spec = pl.BlockSpec((tm, HEAD_PAD), lambda i: (i % per_seq, 0))
    head_spec = pl.BlockSpec((1, N_HEADS, tm, HEAD_PAD), lambda i: (i // per_seq, 0, i % per_seq, 0))
    row_spec = pl.BlockSpec((tm, 512), lambda i: (i, 0))
    return pl.pallas_call(
        body, name="mla_prep_bwd", grid=(n_rows // tm,),
        in_specs=[row_spec, head_spec, head_spec, head_spec, whole2(wuq), whole2(wukv),
                  whole2(gq), whole2(gkv), whole2(gqn), whole2(gkn), rope_spec, rope_spec],
        out_specs=[row_spec, whole2(wuq), whole2(wukv), whole2(gq), whole2(gkv), whole2(gqn), whole2(gkn)],
        out_shape=[jax.ShapeDtypeStruct((n_rows, 512), BF16),
                   jax.ShapeDtypeStruct(wuq.shape, F32), jax.ShapeDtypeStruct(wukv.shape, F32),
                   jax.ShapeDtypeStruct(gq.shape, F32), jax.ShapeDtypeStruct(gkv.shape, F32),
                   jax.ShapeDtypeStruct(gqn.shape, F32), jax.ShapeDtypeStruct(gkn.shape, F32)],
        compiler_params=_params(("arbitrary",)),
    )(zsm, dq, dk, dv, wuq, wukv, gq, gkv, gqn, gkn, *rope)


HBM_SPEC = pl.BlockSpec(memory_space=pltpu.HBM)


def _xchg_out_shapes(bufs):
    return [jax.ShapeDtypeStruct((N_DEV,) + (a.shape if gather else a.shape[1:]), a.dtype) for a, gather in bufs]


def _xchg_scratch(n_buf):
    return [pltpu.SemaphoreType.DMA((n_buf * (N_DEV - 1),)), pltpu.SemaphoreType.DMA((n_buf * (N_DEV - 1),)),
            pltpu.SemaphoreType.DMA((n_buf,))]


def _xchg_copies(src_refs, dst_refs, gathers, send_sems, recv_sems, local_sems):
    x, y, c = lax.axis_index("x"), lax.axis_index("y"), lax.axis_index("c")
    me = 4 * x + 2 * y + c
    local, starts, arrivals = [], [], []
    for bi, (src, dst, gather) in enumerate(zip(src_refs, dst_refs, gathers)):
        local.append(pltpu.make_async_copy(src if gather else src.at[me], dst.at[me], local_sems.at[bi]))
        for kk in range(1, N_DEV):
            px = 1 - x if kk & 4 else x
            py = 1 - y if kk & 2 else y
            pc = 1 - c if kk & 1 else c
            pid = 4 * px + 2 * py + pc
            sem = bi * (N_DEV - 1) + kk - 1
            starts.append(pltpu.make_async_remote_copy(
                src_ref=src if gather else src.at[pid], dst_ref=dst.at[me],
                send_sem=send_sems.at[sem], recv_sem=recv_sems.at[sem],
                device_id=(px, py, pc), device_id_type=pl.DeviceIdType.MESH))
            arrivals.append(pltpu.make_async_remote_copy(
                src_ref=src if gather else src.at[me], dst_ref=dst.at[pid],
                send_sem=send_sems.at[sem], recv_sem=recv_sems.at[sem],
                device_id=(px, py, pc), device_id_type=pl.DeviceIdType.MESH))
    return local, starts, arrivals


def _xchg_start(copies):
    local, sends, _ = copies
    for cp in local + sends:
        cp.start()


def _xchg_finish(copies):
    local, sends, arrivals = copies
    for cp in arrivals:
        cp.wait_recv()
    for cp in sends:
        cp.wait_send()
    for cp in local:
        cp.wait()


def _gather_by_chip(src_refs, dst_refs, send_sems, recv_sems, local_sems, start=True, finish=True):
    x, y, c = lax.axis_index("x"), lax.axis_index("y"), lax.axis_index("c")
    me = 4 * x + 2 * y + c
    sibling = (x, y, 1 - c)

    def place(kk):
        px = 1 - x if kk & 4 else x
        py = 1 - y if kk & 2 else y
        pc = 1 - c if kk & 1 else c
        return (px, py, pc), 4 * px + 2 * py + pc

    def copy(bi, kk, src, dst, to):
        sem = bi * (N_DEV - 1) + kk - 1
        return pltpu.make_async_remote_copy(src_ref=src, dst_ref=dst, send_sem=send_sems.at[sem],
                                            recv_sem=recv_sems.at[sem], device_id=to, device_id_type=pl.DeviceIdType.MESH)

    local, sends = [], []
    for bi, (src, dst) in enumerate(zip(src_refs, dst_refs)):
        local.append(pltpu.make_async_copy(src, dst.at[me], local_sems.at[bi]))
        sends += [copy(bi, kk, src, dst.at[me], place(kk)[0]) for kk in (1, 2, 4, 6)]
    if start:
        for cp in local + sends:
            cp.start()
    if not finish:
        return
    for kk in (2, 4, 6):
        for bi, (src, dst) in enumerate(zip(src_refs, dst_refs)):
            dev, pid = place(kk)
            copy(bi, kk, src, dst.at[pid], dev).wait_recv()
            passed = copy(bi, kk | 1, dst.at[pid], dst.at[pid], sibling)
            passed.start()
            sends.append(passed)
    for kk in (1, 3, 5, 7):
        for bi, (src, dst) in enumerate(zip(src_refs, dst_refs)):
            dev, pid = place(kk)
            copy(bi, kk, src, dst.at[pid], sibling).wait_recv()
    for cp in sends:
        cp.wait_send()
    for cp in local:
        cp.wait()


def _exchange(name, bufs, by_chip=False):
    n_buf = len(bufs)
    gathers = [g for _, g in bufs]
    assert not by_chip or all(gathers)

    def body(*refs):
        srcs, dsts = refs[:n_buf], refs[n_buf:2 * n_buf]
        if by_chip:
            _gather_by_chip(srcs, dsts, *refs[2 * n_buf:])
            return
        copies = _xchg_copies(srcs, dsts, gathers, *refs[2 * n_buf:])
        _xchg_start(copies)
        _xchg_finish(copies)

    return pl.pallas_call(
        body, name=name, out_shape=_xchg_out_shapes(bufs),
        in_specs=[HBM_SPEC] * n_buf, out_specs=[HBM_SPEC] * n_buf, scratch_shapes=_xchg_scratch(n_buf),
    )(*[a for a, _ in bufs])


def _chunk_mask(t, keys_first):
    key = lax.broadcasted_iota(jnp.int32, (t, t), 0 if keys_first else 1) // CHUNK
    query = lax.broadcasted_iota(jnp.int32, (t, t), 1 if keys_first else 0) // CHUNK
    return query >= key


def _grid_ends(grid):
    ids = [pl.program_id(ax) for ax in range(len(grid))]
    first = functools.reduce(jnp.logical_and, [i == 0 for i in ids])
    last = functools.reduce(jnp.logical_and, [i == g - 1 for i, g in zip(ids, grid)])
    return first, last


def _attn_fwd(q, k, v, bufs, n_b, seq):
    tq = min(ATT_TILE, seq)
    nq = seq // tq
    nt_dims = _DIMS["nt"]
    hpb = ATT_HEADS_FWD
    grid = (n_b, N_HEADS // hpb, nq)
    n_buf = len(bufs)
    gathers = [g for _, g in bufs]
    sum_lane = [HEAD_PAD - 1 if hh % 2 == 0 else 0 for hh in range(hpb)]

    def body(q_ref, k_ref, v_ref, *rest):
        srcs, (o_ref, lse_ref), dsts = rest[:n_buf], rest[n_buf:n_buf + 2], rest[n_buf + 2:2 * n_buf + 2]
        gather = functools.partial(_gather_by_chip, srcs, dsts, *rest[2 * n_buf + 2:])
        first, last = _grid_ends(grid)
        pl.when(first)(functools.partial(gather, start=True, finish=False))

        qi = pl.program_id(2)
        mask = _chunk_mask(tq, keys_first=False)
        lane_row = lax.broadcasted_iota(jnp.int32, (1, HEAD_PAD), 1)
        ones = [(lane_row == sum_lane[hh]).astype(BF16) for hh in range(hpb)]
        qs = [q_ref[0, hh] for hh in range(hpb)]

        def step(j, carry, masked):
            rows = pl.ds(pl.multiple_of(j * tq, tq), tq)
            out = []
            for hh in range(hpb):
                m, acc = carry[hh]
                s = lax.dot_general(qs[hh], k_ref[0, hh, rows, :], nt_dims, preferred_element_type=F32)
                if masked:
                    s = jnp.where(mask, s, NEG_BIG)
                m_new = jnp.maximum(m, jnp.max(s, axis=-1, keepdims=True))
                p = jnp.exp2(s - m_new).astype(BF16)
                acc = jnp.exp2(m - m_new) * acc + jnp.dot(p, v_ref[0, hh, rows, :] + ones[hh], preferred_element_type=F32)
                out.append((m_new, acc))
            return tuple(out)

        init = tuple((jnp.full((tq, 1), NEG_BIG, F32), jnp.zeros((tq, HEAD_PAD), F32)) for _ in range(hpb))
        online = lax.fori_loop(0, qi, functools.partial(step, masked=False), init)
        online = step(qi, online, True)
        ms, accs = [c[0] for c in online], [c[1] for c in online]
        carry = list(zip(ms, accs))
        lane = lax.broadcasted_iota(jnp.int32, (tq, HEAD_PAD), 1)
        for pair in range(hpb // 2):
            outs = []
            for hh in (2 * pair, 2 * pair + 1):
                m, acc = carry[hh]
                l = jnp.sum(jnp.where(lane == sum_lane[hh], acc, 0.0), axis=-1, keepdims=True)
                outs.append(acc * (1.0 / l))
                lse_ref[0, hh] = jnp.broadcast_to(m + jnp.log2(l), (tq, HEAD_PAD)).T[0:8, :]
            o_ref[0, :, pair * HEAD_PAD:(pair + 1) * HEAD_PAD] = jnp.where(lane < V_HEAD_DIM, outs[0], outs[1]).astype(BF16)

        pl.when(last)(functools.partial(gather, start=False, finish=True))

    kv_spec = pl.BlockSpec((1, hpb, seq, HEAD_PAD), lambda b, hb, i: (b, hb, 0, 0))
    q_spec = pl.BlockSpec((1, hpb, tq, HEAD_PAD), lambda b, hb, i: (b, hb, i, 0))
    res = pl.pallas_call(
        body, name="attn_fwd", grid=grid,
        in_specs=[q_spec, kv_spec, kv_spec] + [HBM_SPEC] * n_buf,
        out_specs=[pl.BlockSpec((1, tq, hpb * V_HEAD_DIM), lambda b, hb, i: (b, i, hb)),
                   pl.BlockSpec((1, hpb, 8, tq), lambda b, hb, i: (b, hb, 0, i))] + [HBM_SPEC] * n_buf,
        out_shape=[jax.ShapeDtypeStruct((n_b, seq, N_HEADS * V_HEAD_DIM), BF16),
                   jax.ShapeDtypeStruct((n_b, N_HEADS, 8, seq), F32)] + _xchg_out_shapes(bufs),
        scratch_shapes=_xchg_scratch(n_buf),
        compiler_params=_params(("arbitrary", "arbitrary", "arbitrary")),
    )(q, k, v, *[a for a, _ in bufs])
    return res[0], res[1], res[2:]


def _attn_bwd(q, k, v, kt, do, o, lse, bufs, n_b, seq):
    tq = min(ATT_TILE, seq)
    nq = seq // tq
    nt_dims = _DIMS["nt"]
    hpb = ATT_HEADS
    grid = (n_b, N_HEADS // hpb, nq)
    n_buf = len(bufs)
    gathers = [g for _, g in bufs]

    def body(q_ref, k_ref, v_ref, kt_ref, do_ref, o_ref, lse_ref, *rest):
        srcs, (dq_ref, dk_ref, dv_ref), dsts = rest[:n_buf], rest[n_buf:n_buf + 3], rest[n_buf + 3:2 * n_buf + 3]
        dk_acc, dv_acc = rest[2 * n_buf + 3:2 * n_buf + 5]
        copies = _xchg_copies(srcs, dsts, gathers, *rest[2 * n_buf + 5:])
        first, last = _grid_ends(grid)
        pl.when(first)(functools.partial(_xchg_start, copies))

        qi = pl.program_id(2)

        @pl.when(qi == 0)
        def _():
            dk_acc[...] = jnp.zeros_like(dk_acc)
            dv_acc[...] = jnp.zeros_like(dv_acc)

        mask = _chunk_mask(tq, keys_first=True)
        lane = lax.broadcasted_iota(jnp.int32, (tq, HEAD_PAD), 1)
        qs, dos, deltas, lses = [], [], [], []
        for hh in range(hpb):
            cols = slice((hh // 2) * HEAD_PAD, (hh // 2 + 1) * HEAD_PAD)
            do_pair = do_ref[0, :, cols]
            prod = do_pair.astype(F32) * o_ref[0, :, cols].astype(F32)
            delta = jnp.sum(jnp.where(lane // V_HEAD_DIM == hh % 2, prod, 0.0), axis=-1, keepdims=True)
            qs.append(q_ref[0, hh])
            dos.append(do_pair)
            deltas.append(jnp.broadcast_to(delta, (tq, HEAD_PAD)).T[0:1, :])
            lses.append(lse_ref[0, hh][0:1, :])

        def step(j, dqs, masked):
            rows = pl.ds(pl.multiple_of(j * tq, tq), tq)
            out = []
            for hh in range(hpb):
                s = lax.dot_general(k_ref[0, hh, rows, :], qs[hh], nt_dims, preferred_element_type=F32)
                p = jnp.exp2(s - lses[hh])
                if masked:
                    p = jnp.where(mask, p, 0.0)
                dv_acc[hh, rows, :] += jnp.dot(p.astype(BF16), dos[hh], preferred_element_type=F32)
                dp = lax.dot_general(v_ref[0, hh, rows, :], dos[hh], nt_dims, preferred_element_type=F32)
                ds = (p * (dp - deltas[hh])).astype(BF16)
                dk_acc[hh, rows, :] += jnp.dot(ds, qs[hh], preferred_element_type=F32)
                out.append(dqs[hh] + jnp.dot(kt_ref[0, hh, j], ds, preferred_element_type=F32))
            return tuple(out)

        dqs = tuple(jnp.zeros((HEAD_PAD, tq), F32) for _ in range(hpb))
        dqs = lax.fori_loop(0, qi, functools.partial(step, masked=False), dqs)
        dqs = step(qi, dqs, True)
        for hh in range(hpb):
            dq_ref[0, hh] = dqs[hh].T.astype(BF16)

        @pl.when(qi == nq - 1)
        def _():
            dk_ref[0] = (dk_acc[...] * LN2).astype(BF16)
            dv_ref[0] = dv_acc[...].astype(BF16)

        pl.when(last)(functools.partial(_xchg_finish, copies))

    full_spec = pl.BlockSpec((1, hpb, seq, HEAD_PAD), lambda b, hb, i: (b, hb, 0, 0))
    t_spec = pl.BlockSpec((1, hpb, nq, HEAD_PAD, tq), lambda b, hb, i: (b, hb, 0, 0, 0))
    q_spec = pl.BlockSpec((1, hpb, tq, HEAD_PAD), lambda b, hb, i: (b, hb, i, 0))
    o_spec = pl.BlockSpec((1, tq, hpb * V_HEAD_DIM), lambda b, hb, i: (b, i, hb))
    lse_spec = pl.BlockSpec((1, hpb, 8, tq), lambda b, hb, i: (b, hb, 0, i))
    head_shape = jax.ShapeDtypeStruct((n_b, N_HEADS, seq, HEAD_PAD), BF16)
    res = pl.pallas_call(
        body, name="attn_bwd", grid=grid,
        in_specs=[q_spec, full_spec, full_spec, t_spec, o_spec, o_spec, lse_spec] + [HBM_SPEC] * n_buf,
        out_specs=[q_spec, full_spec, full_spec] + [HBM_SPEC] * n_buf,
        out_shape=[head_shape] * 3 + _xchg_out_shapes(bufs),
        scratch_shapes=[pltpu.VMEM((hpb, seq, HEAD_PAD), F32), pltpu.VMEM((hpb, seq, HEAD_PAD), F32)]
        + _xchg_scratch(n_buf),
        compiler_params=_params(("arbitrary", "arbitrary", "arbitrary")),
    )(q, k, v, kt, do, o, lse, *[a for a, _ in bufs])
    return res[0], res[1], res[2], res[3:]


def _in_proj_fwd(x2, scale, shift, g, w_parts, z_dtypes, seq):
    n_rows, d = x2.shape
    tm = min(512, seq)
    per_seq = seq // tm
    n_part = len(w_parts)
    nt_dims = _DIMS["nt"]

    def body(x_ref, sc_ref, sh_ref, g_ref, *rest):
        w_refs, h_ref, z_refs = rest[:n_part], rest[n_part], rest[n_part + 1:]
        h = _norm_mod(x_ref[...], g_ref[...], sc_ref[0], sh_ref[0]).astype(BF16)
        h_ref[...] = h
        for w_ref, z_ref in zip(w_refs, z_refs):
            z_ref[...] = lax.dot_general(h, w_ref[...], nt_dims, preferred_element_type=F32).astype(z_ref.dtype)

    row = lambda width: pl.BlockSpec((tm, width), lambda i: (i, 0))
    bat = pl.BlockSpec((1, 1, d), lambda i: (i // per_seq, 0, 0))
    whole = lambda arr: pl.BlockSpec(arr.shape, lambda i: (0, 0))
    return pl.pallas_call(
        body, name="in_proj_fwd", grid=(n_rows // tm,),
        in_specs=[row(d), bat, bat, whole(g)] + [whole(w) for w in w_parts],
        out_specs=[row(d)] + [row(w.shape[0]) for w in w_parts],
        out_shape=[jax.ShapeDtypeStruct((n_rows, d), BF16)]
        + [jax.ShapeDtypeStruct((n_rows, w.shape[0]), dt) for w, dt in zip(w_parts, z_dtypes)],
        compiler_params=_params(("parallel",)),
    )(x2, scale, shift, g, *w_parts)
def _in_proj_bwd(parts, x2, dx1, scale, g, bufs, seq):
    n_rows, d = x2.shape
    tm = min(512, seq)
    per_seq = seq // tm
    grid = (n_rows // tm,)
    n_part, n_buf = len(parts), len(bufs)
    gathers = [gt for _, gt in bufs]

    def body(*refs):
        dz_refs, w_refs = refs[:n_part], refs[n_part:2 * n_part]
        x_ref, dx1_ref, sc_ref, g_ref = refs[2 * n_part:2 * n_part + 4]
        srcs = refs[2 * n_part + 4:2 * n_part + 4 + n_buf]
        gx_ref, dsc_ref, dsh_ref, dg_ref = refs[2 * n_part + 4 + n_buf:2 * n_part + 8 + n_buf]
        dsts = refs[2 * n_part + 8 + n_buf:2 * n_part + 8 + 2 * n_buf]
        copies = _xchg_copies(srcs, dsts, gathers, *refs[2 * n_part + 8 + 2 * n_buf:])
        first, last = _grid_ends(grid)
        pl.when(first)(functools.partial(_xchg_start, copies))

        i = pl.program_id(0)
        dh = None
        for dz_ref, w_ref in zip(dz_refs, w_refs):
            term = jnp.dot(dz_ref[...], w_ref[...], preferred_element_type=F32)
            dh = term if dh is None else dh + term
        dx, dsc, dsh, dg = _norm_mod_bwd(x_ref[...], g_ref[...], sc_ref[0], dh)
        gx_ref[...] = dx1_ref[...] + dx

        @pl.when(i % per_seq == 0)
        def _():
            dsc_ref[...] = jnp.zeros_like(dsc_ref)
            dsh_ref[...] = jnp.zeros_like(dsh_ref)

        @pl.when(i == 0)
        def _():
            dg_ref[...] = jnp.zeros_like(dg_ref)

        dsc_ref[0] += dsc
        dsh_ref[0] += dsh
        dg_ref[...] += dg
        pl.when(last)(functools.partial(_xchg_finish, copies))

    row = lambda width: pl.BlockSpec((tm, width), lambda i: (i, 0))
    bat = pl.BlockSpec((1, 1, d), lambda i: (i // per_seq, 0, 0))
    whole = lambda arr: pl.BlockSpec(arr.shape, lambda i: (0, 0))
    n_b = n_rows // seq
    res = pl.pallas_call(
        body, name="in_proj_bwd", grid=grid,
        in_specs=[row(dz.shape[1]) for dz, _ in parts] + [whole(w) for _, w in parts]
        + [row(d), row(d), bat, whole(g)] + [HBM_SPEC] * n_buf,
        out_specs=[row(d), bat, bat, whole(g)] + [HBM_SPEC] * n_buf,
        out_shape=[jax.ShapeDtypeStruct((n_rows, d), F32), jax.ShapeDtypeStruct((n_b, 1, d), F32),
                   jax.ShapeDtypeStruct((n_b, 1, d), F32), jax.ShapeDtypeStruct(g.shape, F32)] + _xchg_out_shapes(bufs),
        scratch_shapes=_xchg_scratch(n_buf),
        compiler_params=_params(("arbitrary",)),
    )(*[dz for dz, _ in parts], *[w for _, w in parts], x2, dx1, scale, g, *[a for a, _ in bufs])
    return res[0], res[1], res[2], res[3], res[4:]


def _ln_silu(u1, g, b):
    mu = jnp.mean(u1, axis=-1, keepdims=True)
    uc = u1 - mu
    r = lax.rsqrt(jnp.mean(uc * uc, axis=-1, keepdims=True) + EPS)
    y = uc * r * g + b
    return y * _sigmoid(y)


def _conv_fill_glu(z_ref, u0_ref, seq, tile):
    u0_ref[0:CONV_HALO, :] = jnp.zeros((CONV_HALO, CONV_CH), F32)
    u0_ref[CONV_HALO + seq:CONV_HALO + seq + CONV_TAIL, :] = jnp.zeros((CONV_TAIL, CONV_CH), F32)
    for t in range(seq // tile):
        zt = z_ref[0, t * tile:(t + 1) * tile, :].astype(F32)
        u0_ref[CONV_HALO + t * tile:CONV_HALO + (t + 1) * tile, :] = zt[:, :CONV_CH] * _sigmoid(zt[:, CONV_CH:])


def _conv_windows(ref, views_ref, t, tile):
    for b in range(8):
        views_ref[b] = ref[t * tile + b:t * tile + b + tile + CONV_HALO, :]


def _conv_tap(views_ref, offset, tile):
    return views_ref[offset % 8, 8 * (offset // 8):8 * (offset // 8) + tile, :]


def _conv_tile(u0_ref, views_ref, w_ref, b_ref, t, tile):
    _conv_windows(u0_ref, views_ref, t, tile)
    acc = jnp.broadcast_to(b_ref[...], (tile, CONV_CH))
    for kk in range(CONV_WIDTH):
        acc = acc + w_ref[kk:kk + 1, :] * _conv_tap(views_ref, kk + CONV_HALO - (CONV_WIDTH - 1), tile)
    return acc


def _conv_fwd(zglu, conv_w, conv_b, ln_g, ln_b, n_b, seq):
    tile = min(256, seq)

    def body(z_ref, w_ref, b_ref, g_ref, bb_ref, o_ref, u1_ref, u0_ref, views_ref):
        _conv_fill_glu(z_ref, u0_ref, seq, tile)
        for t in range(seq // tile):
            u1 = _conv_tile(u0_ref, views_ref, w_ref, b_ref, t, tile)
            u1_ref[0, t * tile:(t + 1) * tile, :] = u1
            o_ref[0, t * tile:(t + 1) * tile, :] = _ln_silu(u1, g_ref[...], bb_ref[...]).astype(BF16)

    whole2 = lambda arr: pl.BlockSpec(arr.shape, lambda b: (0, 0))
    seq_spec = pl.BlockSpec((1, seq, CONV_CH), lambda b: (b, 0, 0))
    return pl.pallas_call(
        body, name="conv_fwd", grid=(n_b,),
        in_specs=[pl.BlockSpec((1, seq, 2 * CONV_CH), lambda b: (b, 0, 0)), whole2(conv_w), whole2(conv_b),
                  whole2(ln_g), whole2(ln_b)],
        out_specs=[seq_spec, seq_spec],
        out_shape=[jax.ShapeDtypeStruct((n_b, seq, CONV_CH), BF16), jax.ShapeDtypeStruct((n_b, seq, CONV_CH), F32)],
        scratch_shapes=[pltpu.VMEM((seq + CONV_HALO + CONV_TAIL, CONV_CH), F32),
                        pltpu.VMEM((8, tile + CONV_HALO, CONV_CH), F32)],
        compiler_params=_params(("parallel",)),
    )(zglu, conv_w, conv_b, ln_g, ln_b)


def _conv_bwd(zglu, u1_saved, du3, conv_w, ln_g, ln_b, n_b, seq):
    tile = min(256, seq)
    n_t = seq // tile

    def body(z_ref, u1_ref, du3_ref, w_ref, g_ref, bb_ref, dz_ref, dw_ref, db_ref, dg_ref, dbb_ref, u0_ref, du1_ref,
             u0_views, du1_views):
        @pl.when(pl.program_id(0) == 0)
        def _():
            for r in (dw_ref, db_ref, dg_ref, dbb_ref):
                r[...] = jnp.zeros_like(r)

        _conv_fill_glu(z_ref, u0_ref, seq, tile)
        du1_ref[seq:seq + CONV_HALO + CONV_TAIL, :] = jnp.zeros((CONV_HALO + CONV_TAIL, CONV_CH), F32)
        g = g_ref[...]
        for t in range(n_t):
            u1 = u1_ref[0, t * tile:(t + 1) * tile, :]
            mu = jnp.mean(u1, axis=-1, keepdims=True)
            uc = u1 - mu
            r = lax.rsqrt(jnp.mean(uc * uc, axis=-1, keepdims=True) + EPS)
            xh = uc * r
            y = xh * g + bb_ref[...]
            sg = _sigmoid(y)
            dy = du3_ref[0, t * tile:(t + 1) * tile, :].astype(F32) * (sg * (1.0 + y * (1.0 - sg)))
            dg_ref[...] += jnp.sum(dy * xh, axis=0, keepdims=True)
            dbb_ref[...] += jnp.sum(dy, axis=0, keepdims=True)
            dxh = dy * g
            du1 = r * (dxh - jnp.mean(dxh, axis=-1, keepdims=True) - xh * jnp.mean(dxh * xh, axis=-1, keepdims=True))
            db_ref[...] += jnp.sum(du1, axis=0, keepdims=True)
            du1_ref[t * tile:(t + 1) * tile, :] = du1
        for t in range(n_t):
            du1 = du1_ref[t * tile:(t + 1) * tile, :]
            du0 = jnp.zeros((tile, CONV_CH), F32)
            _conv_windows(u0_ref, u0_views, t, tile)
            _conv_windows(du1_ref, du1_views, t, tile)
            for kk in range(CONV_WIDTH):
                du0 = du0 + w_ref[kk:kk + 1, :] * _conv_tap(du1_views, CONV_WIDTH - 1 - kk, tile)
                u0_tap = _conv_tap(u0_views, kk + CONV_HALO - (CONV_WIDTH - 1), tile)
                dw_ref[kk:kk + 1, :] += jnp.sum(du1 * u0_tap, axis=0, keepdims=True)
            zt = z_ref[0, t * tile:(t + 1) * tile, :].astype(F32)
            ga, sb = zt[:, :CONV_CH], _sigmoid(zt[:, CONV_CH:])
            dz_ref[0, t * tile:(t + 1) * tile, :CONV_CH] = (du0 * sb).astype(BF16)
            dz_ref[0, t * tile:(t + 1) * tile, CONV_CH:] = (du0 * ga * sb * (1.0 - sb)).astype(BF16)

    whole2 = lambda arr: pl.BlockSpec(arr.shape, lambda b: (0, 0))
    z_spec = pl.BlockSpec((1, seq, 2 * CONV_CH), lambda b: (b, 0, 0))
    seq_spec = pl.BlockSpec((1, seq, CONV_CH), lambda b: (b, 0, 0))
    return pl.pallas_call(
        body, name="conv_bwd", grid=(n_b,),
        in_specs=[z_spec, seq_spec, seq_spec, whole2(conv_w), whole2(ln_g), whole2(ln_b)],
        out_specs=[z_spec, whole2(conv_w), whole2(ln_g), whole2(ln_g), whole2(ln_b)],
        out_shape=[jax.ShapeDtypeStruct((n_b, seq, 2 * CONV_CH), BF16), jax.ShapeDtypeStruct(conv_w.shape, F32),
                   jax.ShapeDtypeStruct(ln_g.shape, F32), jax.ShapeDtypeStruct(ln_g.shape, F32),
                   jax.ShapeDtypeStruct(ln_b.shape, F32)],
        scratch_shapes=[pltpu.VMEM((seq + CONV_HALO + CONV_TAIL, CONV_CH), F32)] * 2
        + [pltpu.VMEM((8, tile + CONV_HALO, CONV_CH), F32)] * 2,
        compiler_params=_params(("arbitrary",)),
    )(zglu, u1_saved, du3, conv_w, ln_g, ln_b)


def _sum_parts(name, parts):
    n_parts = parts.shape[0]

    def body(p_ref, o_ref):
        gg = p_ref[0].astype(F32)
        for j in range(1, n_parts):
            gg = gg + p_ref[j].astype(F32)
        o_ref[...] = gg

    return pl.pallas_call(body, name=name, out_shape=jax.ShapeDtypeStruct(parts.shape[1:], F32),
                          compiler_params=_params(None))(parts)


def _adamw(name, w, parts, m, v, transposed=False):
    n_parts = parts.shape[0]
    rows, cols = w.shape
    tr = ADAM_ROWS if rows % ADAM_ROWS == 0 else rows

    def body(w_ref, p_ref, m_ref, v_ref, g_ref, d_ref, nm_ref, nv_ref):
        gg = p_ref[0].astype(F32)
        for j in range(1, n_parts):
            gg = gg + p_ref[j].astype(F32)
        if transposed:
            gg = gg.T
        nm = ADAM_B1 * m_ref[...] + (1.0 - ADAM_B1) * gg
        nv = ADAM_B2 * v_ref[...] + (1.0 - ADAM_B2) * jnp.square(gg)
        m_hat = nm / (1.0 - ADAM_B1 ** ADAM_STEP)
        v_hat = nv / (1.0 - ADAM_B2 ** ADAM_STEP)
        g_ref[...] = gg
        d_ref[...] = -ADAM_LR * (m_hat / (jnp.sqrt(v_hat) + ADAM_EPS) + ADAM_WD * w_ref[...])
        nm_ref[...] = nm
        nv_ref[...] = nv

    shape = jax.ShapeDtypeStruct(w.shape, F32)
    blk = pl.BlockSpec((tr, cols), lambda i: (i, 0))
    p_spec = (pl.BlockSpec((n_parts, cols, tr), lambda i: (0, 0, i)) if transposed
              else pl.BlockSpec((n_parts, tr, cols), lambda i: (0, i, 0)))
    return pl.pallas_call(body, name=name, grid=(rows // tr,), in_specs=[blk, p_spec, blk, blk], out_specs=[blk] * 4,
                          out_shape=[shape] * 4, compiler_params=_params(("parallel",)))(w, parts, m, v)


def _rope_tables(seq):
    inv_freq = ROPE_THETA ** (-jnp.arange(0, QK_ROPE_DIM, 2, dtype=F32) / QK_ROPE_DIM)
    ang = jnp.arange(seq, dtype=F32)[:, None] * inv_freq[None, :]
    cos, sin = jnp.cos(ang), jnp.sin(ang)
    half = QK_ROPE_DIM // 2
    lane_half = HEAD_PAD // 2
    one = lambda n: jnp.ones((seq, n), F32)
    z = lambda n: jnp.zeros((seq, n), F32)
    used_hi = QK_HEAD_DIM - lane_half - half
    cos_t = jnp.concatenate([cos, one(lane_half - half), cos, one(used_hi), z(lane_half - half - used_hi)], axis=1)
    sin_t = jnp.concatenate([-sin, z(lane_half - half), sin, z(lane_half - half)], axis=1)
    return cos_t, sin_t


def _pad_lanes(v, width=HEAD_PAD):
    return jnp.pad(v, [(0, 0)] * (v.ndim - 1) + [(0, width - v.shape[-1])])


_LANE_HALF_NOPE = HEAD_PAD // 2 - QK_ROPE_DIM // 2


def _head_lanes(v):
    rot = v[..., QK_NOPE_DIM:]
    half = QK_ROPE_DIM // 2
    return _pad_lanes(jnp.concatenate([rot[..., :half], v[..., :_LANE_HALF_NOPE], rot[..., half:],
                                       v[..., _LANE_HALF_NOPE:QK_NOPE_DIM]], axis=-1))


def _head_dims(g):
    half = QK_ROPE_DIM // 2
    lane_half = HEAD_PAD // 2
    return jnp.concatenate([g[..., half:lane_half], g[..., lane_half + half:QK_HEAD_DIM], g[..., :half],
                            g[..., lane_half:lane_half + half]], axis=-1)


def _unstack_cols(s):
    return s.transpose(1, 0, 2).reshape(s.shape[1], N_DEV * s.shape[2])


def _stack_cols(g, dtype):
    rows, cols = g.shape
    return g.reshape(rows, N_DEV, cols // N_DEV).transpose(1, 0, 2).astype(dtype)


def kernel(x, c, w_ada, b_ada, norm1_g, w_in, q_latent_g, w_uq, kv_latent_g, w_ukv, qk_norm_q_g, qk_norm_k_g, w_o_mla, conv_w, conv_b, conv_ln_g, conv_ln_b, w_pw_out, w_out, norm2_g, w_ff1, w_ff2, loss_target, m_w_ada, m_b_ada, m_norm1_g, m_w_in, m_q_latent_g, m_w_uq, m_kv_latent_g, m_w_ukv, m_qk_norm_q_g, m_qk_norm_k_g, m_w_o_mla, m_conv_w, m_conv_b, m_conv_ln_g, m_conv_ln_b, m_w_pw_out, m_w_out, m_norm2_g, m_w_ff1, m_w_ff2, v_w_ada, v_b_ada, v_norm1_g, v_w_in, v_q_latent_g, v_w_uq, v_kv_latent_g, v_w_ukv, v_qk_norm_q_g, v_qk_norm_k_g, v_w_o_mla, v_conv_w, v_conv_b, v_conv_ln_g, v_conv_ln_b, v_w_pw_out, v_w_out, v_norm2_g, v_w_ff1, v_w_ff2):
    given = dict(locals())
    local = {n: given[n][0] for n in WEIGHTS}
    vec = {n: local[n].reshape(1, -1) for n in REPLICATED}
    bf = lambda n: local[n].astype(BF16)
    n_b, seq, d = x.shape
    n_rows = n_b * seq
    x2 = x.reshape(n_rows, d)
    t2 = loss_target.reshape(n_rows, d)
    me = 4 * lax.axis_index("x") + 2 * lax.axis_index("y") + lax.axis_index("c")
    ada_cols = local["w_ada"].shape[1]

    tsh = lambda n: local[n].T.astype(BF16)
    c_all, w_in_s, w_uq_s, w_ukv_s, conv_w_s = _exchange(
        "gather_early", [(c, True), (tsh("w_in"), True), (bf("w_uq"), True), (bf("w_ukv"), True), (local["conv_w"], True)],
        by_chip=True)
    w_in_t = w_in_s.reshape(-1, d)
    zrows = lambda n: jnp.zeros((n, d), BF16)
    rot_half = QK_ROPE_DIM // 2
    w_sm_t = jnp.concatenate([w_in_t[:OFF_KV + rot_half], zrows(HEAD_PAD // 2 - rot_half), w_in_t[OFF_KV + rot_half:OFF_KR],
                              zrows(HEAD_PAD // 2 - rot_half)], axis=0)
    w_glu_t = w_in_t[OFF_KR:OFF_GLU]
    w_gate_t = w_in_t[OFF_GLU:]
    wuq = _head_lanes(_unstack_cols(w_uq_s).reshape(Q_LORA, N_HEADS, QK_HEAD_DIM)).reshape(Q_LORA, N_HEADS * HEAD_PAD)
    wukv_f = _unstack_cols(w_ukv_s).reshape(KV_LORA, N_HEADS, QK_NOPE_DIM + V_HEAD_DIM)
    wv = wukv_f[:, :, QK_NOPE_DIM:]
    odd = (jnp.arange(N_HEADS) % 2 == 1)[None, :, None]
    wuv = jnp.where(odd, jnp.pad(wv, ((0, 0), (0, 0), (V_HEAD_DIM, 0))), jnp.pad(wv, ((0, 0), (0, 0), (0, V_HEAD_DIM))))
    wuk = _head_lanes(_pad_lanes(wukv_f[:, :, :QK_NOPE_DIM], QK_HEAD_DIM))
    wukv = jnp.concatenate([wuk, wuv], axis=1).reshape(KV_LORA, 2 * N_HEADS * HEAD_PAD)
    gqn = _head_lanes(vec["qk_norm_q_g"])
    gkn = _head_lanes(vec["qk_norm_k_g"])
    conv_w_f = jnp.pad(_unstack_cols(conv_w_s), ((0, 1), (0, 0)))
    rope = _rope_tables(seq)

    all_rows = N_DEV * n_b
    pad_rows = (-all_rows) % ROWS_PAD
    c_rows = jnp.pad(c_all.reshape(all_rows, d), ((0, pad_rows), (0, 0)))
    b_cols = lax.dynamic_slice(local["b_ada"], (me * ada_cols,), (ada_cols,))
    mod_cols = _mm("ada_fwd", c_rows, local["w_ada"], "nn", F32, a_fn=_silu, epi=lambda acc, b: acc + b,
                   epi_in=(jnp.broadcast_to(b_cols, (all_rows + pad_rows, ada_cols)),))
    (mod_s,) = _exchange("scatter_mod", [(mod_cols[:all_rows].reshape(N_DEV, n_b, ada_cols), False)])
    mod = mod_s.transpose(1, 0, 2).reshape(n_b, ADA_CHUNKS, 1, d)
    shift1, scale1, gate1, shift2, scale2, gate2 = [mod[:, i] for i in range(ADA_CHUNKS)]

    h, zgate, zglu, zsm = _in_proj_fwd(x2, scale1, shift1, vec["norm1_g"], [w_gate_t, w_glu_t, w_sm_t],
                                       [BF16, BF16, F32], seq)
    q, k, v, kt = _mla_prep_fwd(zsm, wuq, wukv, vec["q_latent_g"], vec["kv_latent_g"], gqn, gkn, rope, n_b, seq)
    attn, lse, (w_o_s, w_pw_s, w_out_s, w_ff1_s, w_ff2_s) = _attn_fwd(
        q, k, v, [(tsh("w_o_mla"), True), (tsh("w_pw_out"), True), (bf("w_out"), True), (tsh("w_ff1"), True),
                  (bf("w_ff2"), True)], n_b, seq)
    w_o_t = w_o_s.reshape(d, -1)
    w_pw_t = w_pw_s.reshape(d, -1)
    w_out_f = w_out_s.reshape(d, d)
    w_ff1_t = w_ff1_s.reshape(-1, d)
    w_ff2_f = w_ff2_s.reshape(-1, d)
    attn2 = attn.reshape(n_rows, N_HEADS * V_HEAD_DIM)
    u3, u1 = _conv_fwd(zglu.reshape(n_b, seq, 2 * CONV_CH), conv_w_f, vec["conv_b"], vec["conv_ln_g"], vec["conv_ln_b"], n_b, seq)
    u32 = u3.reshape(n_rows, CONV_CH)
    ya = _mm("mla_out", attn2, w_o_t, "nt", BF16)
    yb = _mm("conv_out", u32, w_pw_t, "nt", BF16)
    mmr = functools.partial(_mm_rows, n_rows=n_rows, seq=seq)

    def merge_fn(t):
        return _sigmoid(t[0]) * t[2] + _sigmoid(t[1]) * t[3]

    def mid_fn(acc, r, b, cc):
        x1_ = r[0] + b[0] * acc
        return [acc, x1_, _norm_mod(x1_, cc[0], b[1], b[2])], [], []

    mrg, mixed, x1, h2 = mmr("out_proj", [(zgate, d, 0), (zgate, d, 1), (ya, d, 0), (yb, d, 0)], merge_fn, w_out_f, "nn",
                             mid_fn, rows=[_full(x2)], bats=[gate1, scale2, shift2], consts=[vec["norm2_g"]],
                             outs=[(d, BF16), (d, F32), (d, BF16)], a_out=BF16)

    a = _mm("ff1", h2, w_ff1_t, "nt", BF16)

    def loss_fn(ff, r, b, cc):
        err = r[0] + b[0] * ff - r[1]
        dy_ = err * (1.0 / d)
        sq = jnp.broadcast_to(jnp.sum(err * err, keepdims=True), (1, LANES))
        return [dy_, b[0] * dy_], [jnp.sum(dy_ * ff, axis=0, keepdims=True)], [sq]

    dy, df, dgate2, sq_err = mmr("ff2_loss", [(a, a.shape[1], 0)], lambda t: _relu2(t[0]), w_ff2_f, "nn", loss_fn,
                                 rows=[_full(x1), _full(t2)], bats=[gate2], outs=[(d, F32), (d, BF16)], bat_outs=[d],
                                 tot_outs=[(1, LANES)], tk=a.shape[1])

    da = _mm("ff2_bwd", df, w_ff2_f, "nt", BF16, epi=lambda acc, av: acc * 2.0 * jnp.maximum(av, 0.0), epi_in=(a,))
    g_ff2 = _mm("ff2_dw", a, df, "tn", BF16, a_fn=_relu2)
    g_ff1_t = _mm("ff1_dw", da, h2, "tn", BF16)

    def mid_bwd(dh2_, r, b, cc):
        dx, dsc, dsh, dg = _norm_mod_bwd(r[0], cc[0], b[0], dh2_)
        dx1_ = r[1] + dx
        return [dx1_, b[1] * dx1_], [dsc, dsh, jnp.sum(dx1_ * r[2].astype(F32), axis=0, keepdims=True)], [dg]

    dx1, dmixed, dscale2, dshift2, dgate1, g_norm2 = mmr(
        "ff1_bwd", [(da, da.shape[1], 0)], None, w_ff1_t, "nn", mid_bwd, rows=[_full(x1), _full(dy), _full(mixed)],
        bats=[scale2, gate1], consts=[vec["norm2_g"]], outs=[(d, F32), (d, BF16)], bat_outs=[d, d, d],
        tot_outs=[(1, d)], tk=da.shape[1])

    g_out = _mm("out_proj_dw", mrg, dmixed, "tn", BF16)

    def merge_bwd(dm, r, b, cc):
        ya_, yb_ = r[2].astype(F32), r[3].astype(F32)
        sa, sb = _sigmoid(r[0].astype(F32)), _sigmoid(r[1].astype(F32))
        return [dm * ya_ * sa * (1.0 - sa), dm * yb_ * sb * (1.0 - sb), dm * sa, dm * sb], [], []

    dzga, dzgb, dya, dyb = mmr("out_proj_bwd", [(dmixed, d, 0)], None, w_out_f, "nt", merge_bwd,
                               rows=[(zgate, d, 0), (zgate, d, 1), _full(ya), _full(yb)], outs=[(d, BF16)] * 4)
    dattn = _mm("mla_out_bwd", dya, w_o_t, "nn", BF16)
    g_o_t = _mm("mla_out_dw", dya, attn2, "tn", BF16)
    du3 = _mm("conv_out_bwd", dyb, w_pw_t, "nn", BF16)
    g_pw_t = _mm("conv_out_dw", dyb, u32, "tn", BF16)

    dzglu, g_conv_w, g_conv_b, g_ln_g, g_ln_b = _conv_bwd(
        zglu.reshape(n_b, seq, 2 * CONV_CH), u1, du3.reshape(n_b, seq, CONV_CH), conv_w_f, vec["conv_ln_g"],
        vec["conv_ln_b"], n_b, seq)
    dzglu = dzglu.reshape(n_rows, 2 * CONV_CH)

    dq, dk, dv, (p_ff2, p_ff1, p_out, p_pw, p_o) = _attn_bwd(
        q, k, v, kt, dattn.reshape(n_b, seq, N_HEADS * V_HEAD_DIM), attn, lse,
        [(g_ff2.reshape(N_DEV, -1, d), False), (g_ff1_t.reshape(N_DEV, -1, d), False), (g_out.reshape(N_DEV, -1, d), False),
         (g_pw_t.reshape(N_DEV, -1, CONV_CH), False), (g_o_t.reshape(N_DEV, -1, N_HEADS * V_HEAD_DIM), False)], n_b, seq)
    dzsm, g_wuq, g_wukv, g_gq, g_gkv, g_gqn, g_gkn = _mla_prep_bwd(
        zsm, dq, dk, dv, wuq, wukv, vec["q_latent_g"], vec["kv_latent_g"], gqn, gkn, rope, n_b, seq)

    g_gate_a_t = _mm("in_proj_gate_dw_a", dzga, h, "tn", BF16)
    g_gate_b_t = _mm("in_proj_gate_dw_b", dzgb, h, "tn", BF16)
    g_glu_t = _mm("in_proj_glu_dw", dzglu, h, "tn", BF16)
    g_sm_t = _mm("in_proj_sm_dw", dzsm, h, "tn", BF16)
    g_in_t = jnp.concatenate([g_sm_t[:OFF_KV + rot_half], g_sm_t[OFF_KV + HEAD_PAD // 2:OFF_KV + HEAD_PAD // 2 + rot_half],
                              g_glu_t, g_gate_a_t, g_gate_b_t], axis=0)
    g_uq = _head_dims(g_wuq.reshape(Q_LORA, N_HEADS, HEAD_PAD)).reshape(Q_LORA, N_HEADS * QK_HEAD_DIM)
    g_wukv = g_wukv.reshape(KV_LORA, 2, N_HEADS, HEAD_PAD)
    g_v = jnp.where(odd, g_wukv[:, 1, :, V_HEAD_DIM:], g_wukv[:, 1, :, :V_HEAD_DIM])
    g_ukv = jnp.concatenate([_head_dims(g_wukv[:, 0])[:, :, :QK_NOPE_DIM], g_v], axis=2).reshape(KV_LORA, -1)

    grad_x, dscale1, dshift1, g_norm1, (p_in, p_uq, p_ukv, p_conv_w) = _in_proj_bwd(
        [(dzga, w_gate_t[:d]), (dzgb, w_gate_t[d:]), (dzglu, w_glu_t), (dzsm, w_sm_t)], x2, dx1, scale1, vec["norm1_g"],
        [(g_in_t.reshape(N_DEV, -1, d), False), (_stack_cols(g_uq, BF16), False), (_stack_cols(g_ukv, BF16), False),
         (_stack_cols(g_conv_w[:CONV_WIDTH], F32), False)], seq)

    dmod = jnp.concatenate([dshift1, dscale1, dgate1, dshift2, dscale2, dgate2], axis=1).reshape(n_b, N_DEV, ada_cols)
    (dmod_s,) = _exchange("scatter_dmod", [(dmod.transpose(1, 0, 2), False)])
    dmod_rows = jnp.pad(dmod_s.reshape(all_rows, ada_cols), ((0, pad_rows), (0, 0)))
    g_ada = _mm("ada_dw", c_rows, dmod_rows, "tn", F32, a_fn=_silu)
    (g_b_cols,) = _rowwise("ada_db", lambda r, b, cc: ([], [], [jnp.sum(r[0], axis=0, keepdims=True)]),
                           all_rows + pad_rows, all_rows + pad_rows, rows=[_full(dmod_rows)], tot_outs=[(1, ada_cols)])

    partial_of = {"norm1_g": g_norm1, "q_latent_g": g_gq, "kv_latent_g": g_gkv, "qk_norm_q_g": _head_dims(g_gqn),
                  "qk_norm_k_g": _head_dims(g_gkn), "conv_b": g_conv_b, "conv_ln_g": g_ln_g, "conv_ln_b": g_ln_b, "norm2_g": g_norm2}
    names = [n for n in REPLICATED if n != "b_ada"]
    pieces = [_pad_lanes(partial_of[n], -(-partial_of[n].shape[1] // LANES) * LANES) for n in names] + [g_b_cols, sq_err]
    widths = [p.shape[1] for p in pieces]
    small = jnp.concatenate(pieces, axis=1)
    small = _pad_lanes(small, -(-small.shape[1] // (8 * LANES)) * 8 * LANES).reshape(-1, LANES)
    (small_s,) = _exchange("gather_small_grads", [(small, True)])
    small_s = small_s.reshape(N_DEV, 1, -1)
    parts = {}
    off = 0
    for n, wd in zip(names, widths):
        parts[n] = small_s[:, :, off:off + vec[n].shape[1]]
        off += wd
    parts["b_ada"] = small_s[:, 0, off:off + ada_cols].reshape(1, 1, N_DEV * ada_cols)
    loss = jnp.sum(small_s[:, 0, off + ada_cols]) * (0.5 / d)
    g_in_mine = _sum_parts("sum_w_in", p_in).T
    parts.update({"w_ada": g_ada[None], "w_in": g_in_mine[None], "w_uq": p_uq, "w_ukv": p_ukv, "w_o_mla": p_o,
                  "conv_w": p_conv_w, "w_pw_out": p_pw, "w_out": p_out, "w_ff1": p_ff1, "w_ff2": p_ff2})
    transposed = ("w_o_mla", "w_pw_out", "w_ff1")

    grad_out, delta_out, m_out, v_out = [], [], [], []
    for n in WEIGHTS:
        shape2 = local[n].shape if local[n].ndim == 2 else (1, local[n].shape[0])
        g_w, d_w, n_m, n_v = _adamw("adamw_" + n, local[n].reshape(shape2), parts[n], given["m_" + n].reshape(shape2),
                                    given["v_" + n].reshape(shape2), transposed=n in transposed)
        full_shape = given[n].shape
        grad_out.append(g_w.reshape(full_shape))
        delta_out.append(d_w.reshape(full_shape))
        m_out.append(n_m.reshape(full_shape))
        v_out.append(n_v.reshape(full_shape))
    return (loss, grad_x.reshape(n_b, seq, d), *grad_out, *delta_out, *m_out, *v_out)
```

```python
import functools

import jax
import jax.numpy as jnp
from jax import lax
from jax.experimental import pallas as pl
from jax.experimental.pallas import tpu as pltpu

F32 = jnp.float32
BF16 = jnp.bfloat16

N_DEV = 8
EPS = 1e-6
N_HEADS = 8
QK_HEAD_DIM = 96
QK_NOPE_DIM = 64
QK_ROPE_DIM = 32
V_HEAD_DIM = 64
HEAD_PAD = 128
Q_LORA = 256
KV_LORA = 128
CONV_CH = 512
CONV_WIDTH = 31
CONV_HALO = 32
CONV_TAIL = 8
CHUNK = 64
ROPE_THETA = 10000.0
OFF_Q = Q_LORA
OFF_KV = OFF_Q + KV_LORA
OFF_KR = OFF_KV + QK_ROPE_DIM
OFF_GLU = OFF_KR + 2 * CONV_CH
ADA_CHUNKS = 6
ADAM_LR = 0.001
ADAM_B1 = 0.9
ADAM_B2 = 0.999
ADAM_EPS = 1e-08
ADAM_WD = 0.01
ADAM_STEP = 10
LANES = 128
VMEM_LIMIT = 56 * 1024 * 1024
NEG_BIG = -1e30
ATT_HEADS = 4
ATT_HEADS_FWD = 8
ATT_TILE = 512
PREP_TILE = 512
ATT_SCALE = QK_HEAD_DIM ** -0.5
LOG2E = 1.4426950408889634
LN2 = 0.6931471805599453
QK_SCALE = ATT_SCALE * LOG2E
ADAM_ROWS = 256
ROWS_PAD = 16

REPLICATED = ("b_ada", "norm1_g", "q_latent_g", "kv_latent_g", "qk_norm_q_g", "qk_norm_k_g", "conv_b", "conv_ln_g",
              "conv_ln_b", "norm2_g")
WEIGHTS = ("w_ada", "b_ada", "norm1_g", "w_in", "q_latent_g", "w_uq", "kv_latent_g", "w_ukv", "qk_norm_q_g",
           "qk_norm_k_g", "w_o_mla", "conv_w", "conv_b", "conv_ln_g", "conv_ln_b", "w_pw_out", "w_out", "norm2_g",
           "w_ff1", "w_ff2")


def _tile(dim, pref):
    if dim <= pref:
        return dim
    t = (pref // LANES) * LANES
    while dim % t:
        t -= LANES
    return t


def _params(semantics):
    return pltpu.CompilerParams(dimension_semantics=semantics, vmem_limit_bytes=VMEM_LIMIT)


def _sigmoid(v):
    return 1.0 / (1.0 + jnp.exp(-v))


def _silu(v):
    return v * _sigmoid(v)


def _relu2(v):
    return jnp.square(jnp.maximum(v, 0.0))


_DIMS = {"nn": (((1,), (0,)), ((), ())), "nt": (((1,), (1,)), ((), ())), "tn": (((0,), (0,)), ((), ()))}


def _mm(name, a, b, mode, out_dtype, *, a_fn=None, epi=None, epi_in=(), tm=2048, tn=1024, tk=1024):
    if mode == "nn":
        (m, k), n = a.shape, b.shape[1]
    elif mode == "nt":
        (m, k), n = a.shape, b.shape[0]
    else:
        (k, m), n = a.shape, b.shape[1]
    tm, tn, tk = _tile(m, tm), _tile(n, tn), _tile(k, tk)
    nk = k // tk
    a_spec = (pl.BlockSpec((tk, tm), lambda i, j, kk: (kk, i)) if mode == "tn"
              else pl.BlockSpec((tm, tk), lambda i, j, kk: (i, kk)))
    b_spec = (pl.BlockSpec((tn, tk), lambda i, j, kk: (j, kk)) if mode == "nt"
              else pl.BlockSpec((tk, tn), lambda i, j, kk: (kk, j)))
    o_spec = e_spec = pl.BlockSpec((tm, tn), lambda i, j, kk: (i, j))
    out_shape = jax.ShapeDtypeStruct((m, n), out_dtype)
    n_epi = len(epi_in)

    def body(a_ref, b_ref, *rest):
        epi_refs, o_ref, acc_ref = rest[:n_epi], rest[n_epi], rest[n_epi + 1]
        kk = pl.program_id(2)

        @pl.when(kk == 0)
        def _():
            acc_ref[...] = jnp.zeros_like(acc_ref)

        av = a_ref[...]
        if a_fn is not None:
            av = a_fn(av.astype(F32))
        acc_ref[...] += lax.dot_general(av.astype(BF16), b_ref[...].astype(BF16), _DIMS[mode],
                                        preferred_element_type=F32)

        @pl.when(kk == nk - 1)
        def _():
            acc = acc_ref[...]
            if epi is not None:
                acc = epi(acc, *[r[...].astype(F32) for r in epi_refs])
            o_ref[...] = acc.astype(out_dtype)

    return pl.pallas_call(
        body, name=name, grid=(m // tm, n // tn, nk),
        in_specs=[a_spec, b_spec] + [e_spec] * n_epi, out_specs=o_spec, out_shape=out_shape,
        scratch_shapes=[pltpu.VMEM((tm, tn), F32)],
        compiler_params=_params(("parallel", "parallel", "arbitrary")),
    )(a, b, *epi_in)


def _rowwise(name, fn, n_rows, seq, rows, bats=(), consts=(), outs=(), bat_outs=(), tot_outs=(), tm=256):
    tm = min(tm, seq)
    per_seq = seq // tm
    n_b = n_rows // seq
    nr, nb, nc, no, nbo, nto = len(rows), len(bats), len(consts), len(outs), len(bat_outs), len(tot_outs)

    def body(*refs):
        i = pl.program_id(0)
        r_in = [r[...] for r in refs[:nr]]
        b_in = [r[0] for r in refs[nr:nr + nb]]
        c_in = [r[...] for r in refs[nr + nb:nr + nb + nc]]
        o_refs = refs[nr + nb + nc:nr + nb + nc + no]
        bo_refs = refs[nr + nb + nc + no:nr + nb + nc + no + nbo]
        to_refs = refs[nr + nb + nc + no + nbo:]
        o_val, bo_val, to_val = fn(r_in, b_in, c_in)
        for r, v in zip(o_refs, o_val):
            r[...] = v.astype(r.dtype)
        if nbo:
            @pl.when(i % per_seq == 0)
            def _():
                for r in bo_refs:
                    r[...] = jnp.zeros_like(r)

            for r, v in zip(bo_refs, bo_val):
                r[0] += v
        if nto:
            @pl.when(i == 0)
            def _():
                for r in to_refs:
                    r[...] = jnp.zeros_like(r)

            for r, v in zip(to_refs, to_val):
                r[...] += v

    in_specs = [pl.BlockSpec((tm, w), functools.partial(lambda cb, i: (i, cb), cb)) for (_, w, cb) in rows]
    in_specs += [pl.BlockSpec((1, 1, bt.shape[2]), lambda i: (i // per_seq, 0, 0)) for bt in bats]
    in_specs += [pl.BlockSpec(ct.shape, lambda i: (0, 0)) for ct in consts]
    out_specs = [pl.BlockSpec((tm, w), lambda i: (i, 0)) for (w, _) in outs]
    out_specs += [pl.BlockSpec((1, 1, w), lambda i: (i // per_seq, 0, 0)) for w in bat_outs]
    out_specs += [pl.BlockSpec(shp, lambda i: (0, 0)) for shp in tot_outs]
    out_shape = [jax.ShapeDtypeStruct((n_rows, w), dt) for (w, dt) in outs]
    out_shape += [jax.ShapeDtypeStruct((n_b, 1, w), F32) for w in bat_outs]
    out_shape += [jax.ShapeDtypeStruct(shp, F32) for shp in tot_outs]
    res = pl.pallas_call(
        body, name=name, grid=(n_rows // tm,), in_specs=in_specs, out_specs=out_specs, out_shape=out_shape,
        compiler_params=_params(("arbitrary",)),
    )(*[r[0] for r in rows], *bats, *consts)
    return res


def _full(arr):
    return (arr, arr.shape[1], 0)


def _mm_rows(name, a_rows, a_fn, w, mode, fn, n_rows, seq, rows=(), bats=(), consts=(), outs=(), bat_outs=(),
             tot_outs=(), a_out=None, tm=512, tk=1024):
    tm = min(tm, seq)
    per_seq = seq // tm
    n_b = n_rows // seq
    k = a_rows[0][1]
    if mode == "nt":
        n_out, tk = w.shape[0], _tile(k, tk)
        w_spec = pl.BlockSpec((n_out, tk), lambda i, kk: (0, kk))
    else:
        n_out, tk = w.shape[1], _tile(k, tk)
        w_spec = pl.BlockSpec((tk, n_out), lambda i, kk: (kk, 0))
    nk = k // tk
    na, nr, nb, nc = len(a_rows), len(rows), len(bats), len(consts)
    n_extra = 0 if a_out is None else 1
    no, nbo, nto = len(outs), len(bat_outs), len(tot_outs)

    def body(*refs):
        i, kk = pl.program_id(0), pl.program_id(1)
        a_refs, w_ref = refs[:na], refs[na]
        pos = na + 1
        r_refs, b_refs, c_refs = refs[pos:pos + nr], refs[pos + nr:pos + nr + nb], refs[pos + nr + nb:pos + nr + nb + nc]
        pos += nr + nb + nc
        ao_refs = refs[pos:pos + n_extra]
        pos += n_extra
        o_refs, bo_refs, to_refs = refs[pos:pos + no], refs[pos + no:pos + no + nbo], refs[pos + no + nbo:pos + no + nbo + nto]
        acc_ref = refs[pos + no + nbo + nto]

        @pl.when(kk == 0)
        def _():
            acc_ref[...] = jnp.zeros_like(acc_ref)

        tiles = [r[...] for r in a_refs]
        av = a_fn([t.astype(F32) for t in tiles]) if a_fn is not None else tiles[0]
        av = av.astype(BF16)
        if n_extra:
            ao_refs[0][...] = av.astype(ao_refs[0].dtype)
        acc_ref[...] += lax.dot_general(av, w_ref[...].astype(BF16), _DIMS[mode], preferred_element_type=F32)

        @pl.when(kk == nk - 1)
        def _():
            o_val, bo_val, to_val = fn(acc_ref[...], [r[...] for r in r_refs], [r[0] for r in b_refs],
                                       [r[...] for r in c_refs])
            for r, v in zip(o_refs, o_val):
                r[...] = v.astype(r.dtype)
            if nbo:
                @pl.when(i % per_seq == 0)
                def _():
                    for r in bo_refs:
                        r[...] = jnp.zeros_like(r)

                for r, v in zip(bo_refs, bo_val):
                    r[0] += v
            if nto:
                @pl.when(i == 0)
                def _():
                    for r in to_refs:
                        r[...] = jnp.zeros_like(r)

                for r, v in zip(to_refs, to_val):
                    r[...] += v

    in_specs = [pl.BlockSpec((tm, tk), functools.partial(lambda cb, i, kk: (i, kk + cb), cb)) for (_, _, cb) in a_rows]
    in_specs += [w_spec]
    in_specs += [pl.BlockSpec((tm, wd), functools.partial(lambda cb, i, kk: (i, cb), cb)) for (_, wd, cb) in rows]
    in_specs += [pl.BlockSpec((1, 1, bt.shape[2]), lambda i, kk: (i // per_seq, 0, 0)) for bt in bats]
    in_specs += [pl.BlockSpec(ct.shape, lambda i, kk: (0, 0)) for ct in consts]
    out_specs = [pl.BlockSpec((tm, tk), lambda i, kk: (i, kk))] * n_extra
    out_specs += [pl.BlockSpec((tm, wd), lambda i, kk: (i, 0)) for (wd, _) in outs]
    out_specs += [pl.BlockSpec((1, 1, wd), lambda i, kk: (i // per_seq, 0, 0)) for wd in bat_outs]
    out_specs += [pl.BlockSpec(shp, lambda i, kk: (0, 0)) for shp in tot_outs]
    out_shape = [jax.ShapeDtypeStruct((n_rows, k), a_out)] if n_extra else []
    out_shape += [jax.ShapeDtypeStruct((n_rows, wd), dt) for (wd, dt) in outs]
    out_shape += [jax.ShapeDtypeStruct((n_b, 1, wd), F32) for wd in bat_outs]
    out_shape += [jax.ShapeDtypeStruct(shp, F32) for shp in tot_outs]
    return pl.pallas_call(
        body, name=name, grid=(n_rows // tm, nk), in_specs=in_specs, out_specs=out_specs, out_shape=out_shape,
        scratch_shapes=[pltpu.VMEM((tm, n_out), F32)],
        compiler_params=_params(("arbitrary", "arbitrary")),
    )(*[a for a, _, _ in a_rows], w, *[r[0] for r in rows], *bats, *consts)


def _norm_mod(x, g, scale, shift):
    r = lax.rsqrt(jnp.mean(x * x, axis=-1, keepdims=True) + EPS)
    xh = x * r
    return xh * g * (1.0 + scale) + shift


def _norm_mod_bwd(x, g, scale, dh):
    r = lax.rsqrt(jnp.mean(x * x, axis=-1, keepdims=True) + EPS)
    xh = x * r
    dn = dh * (1.0 + scale)
    dxh = dn * g
    dx = r * (dxh - xh * jnp.mean(dxh * xh, axis=-1, keepdims=True))
    dscale = jnp.sum(dh * xh * g, axis=0, keepdims=True)
    dshift = jnp.sum(dh, axis=0, keepdims=True)
    dg = jnp.sum(dn * xh, axis=0, keepdims=True)
    return dx, dscale, dshift, dg


def _rms(v, g):
    r = lax.rsqrt(jnp.mean(v * v, axis=-1, keepdims=True) + EPS)
    return v * r * g


def _rms_bwd(v, g, dy):
    r = lax.rsqrt(jnp.mean(v * v, axis=-1, keepdims=True) + EPS)
    vh = v * r
    dvh = dy * g
    dv = r * (dvh - vh * jnp.mean(dvh * vh, axis=-1, keepdims=True))
    return dv, jnp.sum(dy * vh, axis=0, keepdims=True)


def _lane_sum(t):
    return jnp.dot(t.astype(BF16), jnp.ones((HEAD_PAD, HEAD_PAD), BF16), preferred_element_type=F32)


def _head_norm(v, g):
    r = lax.rsqrt(_lane_sum(v * v) * (1.0 / QK_HEAD_DIM) + EPS)
    return v * r * g


def _head_norm_bwd(v, g, dy):
    r = lax.rsqrt(_lane_sum(v * v) * (1.0 / QK_HEAD_DIM) + EPS)
    vh = v * r
    dvh = dy * g
    dv = r * (dvh - vh * (_lane_sum(dvh * vh) * (1.0 / QK_HEAD_DIM)))
    return dv, jnp.sum(dy * vh, axis=0, keepdims=True)


def _rope(v, cos, sin):
    return v * cos + pltpu.roll(v, HEAD_PAD // 2, 1) * sin


def _rope_bwd(g, cos, sin):
    return g * cos + pltpu.roll(g * sin, HEAD_PAD // 2, 1)


def _mla_prep_fwd(zsm, wuq, wukv, gq, gkv, gqn, gkn, rope, n_b, seq):
    n_rows = n_b * seq
    tm = min(PREP_TILE, seq)
    per_seq = seq // tm
    att_tile = min(ATT_TILE, seq)
    k_cols = N_HEADS * HEAD_PAD

    def body(z_ref, wuq_ref, wukv_ref, gq_ref, gkv_ref, gqn_ref, gkn_ref, c_ref, s_ref, q_ref, k_ref, v_ref, kt_ref):
        z = z_ref[...]
        qn = _rms(z[:, :Q_LORA], gq_ref[...]).astype(BF16)
        kvn = _rms(z[:, Q_LORA:Q_LORA + KV_LORA], gkv_ref[...]).astype(BF16)
        krp = z[:, Q_LORA + KV_LORA:]
        cos, sin = c_ref[...], s_ref[...]
        q_all = jnp.dot(qn, wuq_ref[...], preferred_element_type=F32)
        kv_all = jnp.dot(kvn, wukv_ref[...], preferred_element_type=F32)
        for h in range(N_HEADS):
            cols = slice(h * HEAD_PAD, (h + 1) * HEAD_PAD)
            q_ref[0, h] = (_rope(_head_norm(q_all[:, cols], gqn_ref[...]), cos, sin) * QK_SCALE).astype(BF16)
            kh = _rope(_head_norm(kv_all[:, cols] + krp, gkn_ref[...]), cos, sin)
            k_ref[0, h] = kh.astype(BF16)
            for part in range(tm // att_tile):
                kt_ref[0, h, part] = kh[part * att_tile:(part + 1) * att_tile].T.astype(BF16)
            v_ref[0, h] = kv_all[:, k_cols + h * HEAD_PAD:k_cols + (h + 1) * HEAD_PAD].astype(BF16)

    whole2 = lambda arr: pl.BlockSpec(arr.shape, lambda i: (0, 0))
    rope_spec = pl.BlockSpec((tm, HEAD_PAD), lambda i: (i % per_seq, 0))
    head_spec = pl.BlockSpec((1, N_HEADS, tm, HEAD_PAD), lambda i: (i // per_seq, 0, i % per_seq, 0))
    head_shape = jax.ShapeDtypeStruct((n_b, N_HEADS, seq, HEAD_PAD), BF16)
    t_spec = pl.BlockSpec((1, N_HEADS, tm // att_tile, HEAD_PAD, att_tile), lambda i: (i // per_seq, 0, i % per_seq, 0, 0))
    t_shape = jax.ShapeDtypeStruct((n_b, N_HEADS, seq // att_tile, HEAD_PAD, att_tile), BF16)
    return pl.pallas_call(
        body, name="mla_prep_fwd", grid=(n_rows // tm,),
        in_specs=[pl.BlockSpec((tm, 512), lambda i: (i, 0)), whole2(wuq), whole2(wukv),
                  whole2(gq), whole2(gkv), whole2(gqn), whole2(gkn), rope_spec, rope_spec],
        out_specs=[head_spec] * 3 + [t_spec], out_shape=[head_shape] * 3 + [t_shape],
        compiler_params=_params(("parallel",)),
    )(zsm, wuq, wukv, gq, gkv, gqn, gkn, *rope)


def _mla_prep_bwd(zsm, dq, dk, dv, wuq, wukv, gq, gkv, gqn, gkn, rope, n_b, seq):
    n_rows = n_b * seq
    tm = min(PREP_TILE, seq)
    per_seq = seq // tm
    tn_dims = _DIMS["tn"]
    nt_dims = _DIMS["nt"]
    k_cols = N_HEADS * HEAD_PAD

    def body(z_ref, dq_ref, dk_ref, dv_ref, wuq_ref, wukv_ref, gq_ref, gkv_ref, gqn_ref, gkn_ref,
             c_ref, s_ref, dz_ref, dwuq_ref, dwukv_ref, dgq_ref, dgkv_ref, dgqn_ref, dgkn_ref):
        @pl.when(pl.program_id(0) == 0)
        def _():
            for r in (dwuq_ref, dwukv_ref, dgq_ref, dgkv_ref, dgqn_ref, dgkn_ref):
                r[...] = jnp.zeros_like(r)

        z = z_ref[...]
        zq, zkv, krp = z[:, :Q_LORA], z[:, Q_LORA:Q_LORA + KV_LORA], z[:, Q_LORA + KV_LORA:]
        qn = _rms(zq, gq_ref[...]).astype(BF16)
        kvn = _rms(zkv, gkv_ref[...]).astype(BF16)
        cos, sin = c_ref[...], s_ref[...]
        lane = lax.broadcasted_iota(jnp.int32, (tm, HEAD_PAD), 1)
        rope_lanes = (lane % (HEAD_PAD // 2)) < QK_ROPE_DIM // 2
        q_all = jnp.dot(qn, wuq_ref[...], preferred_element_type=F32)
        k_all = jnp.dot(kvn, wukv_ref[:, :k_cols], preferred_element_type=F32)
        dkrp = jnp.zeros((tm, HEAD_PAD), F32)
        dgqn = jnp.zeros((1, HEAD_PAD), F32)
        dgkn = jnp.zeros((1, HEAD_PAD), F32)
        dq_heads, dk_heads = [], []
        for h in range(N_HEADS):
            cols = slice(h * HEAD_PAD, (h + 1) * HEAD_PAD)
            dqh, dg = _head_norm_bwd(q_all[:, cols], gqn_ref[...],
                                     _rope_bwd(dq_ref[0, h].astype(F32) * ATT_SCALE, cos, sin))
            dgqn += dg
            dq_heads.append(dqh.astype(BF16))
            dkh, dg = _head_norm_bwd(k_all[:, cols] + krp, gkn_ref[...], _rope_bwd(dk_ref[0, h].astype(F32), cos, sin))
            dgkn += dg
            dkrp += jnp.where(rope_lanes, dkh, 0.0)
            dk_heads.append(dkh.astype(BF16))
        dq_all = jnp.concatenate(dq_heads, axis=1)
        dkv_all = jnp.concatenate(dk_heads + [dv_ref[0, h] for h in range(N_HEADS)], axis=1)
        dwuq_ref[...] += lax.dot_general(qn, dq_all, tn_dims, preferred_element_type=F32)
        dqn = lax.dot_general(dq_all, wuq_ref[...], nt_dims, preferred_element_type=F32)
        dwukv_ref[...] += lax.dot_general(kvn, dkv_all, tn_dims, preferred_element_type=F32)
        dkvn = lax.dot_general(dkv_all, wukv_ref[...], nt_dims, preferred_element_type=F32)
        dzq, dg = _rms_bwd(zq, gq_ref[...], dqn)
        dgq_ref[...] += dg
        dzkv, dg = _rms_bwd(zkv, gkv_ref[...], dkvn)
        dgkv_ref[...] += dg
        dgqn_ref[...] += dgqn
        dgkn_ref[...] += dgkn
        dz_ref[:, :Q_LORA] = dzq.astype(dz_ref.dtype)
        dz_ref[:, Q_LORA:Q_LORA + KV_LORA] = dzkv.astype(dz_ref.dtype)
        dz_ref[:, Q_LORA + KV_LORA:] = dkrp.astype(dz_ref.dtype)

    whole2 = lambda arr: pl.BlockSpec(arr.shape, lambda i: (0, 0))
    rope_spec = pl.BlockSpec((tm, HEAD_PAD), lambda i: (i % per_seq, 0))
    head_spec = pl.BlockSpec((1, N_HEADS, tm, HEAD_PAD), lambda i: (i // per_seq, 0, i % per_seq, 0))
    row_spec = pl.BlockSpec((tm, 512), lambda i: (i, 0))
    return pl.pallas_call(
        body, name="mla_prep_bwd", grid=(n_rows // tm,),
        in_specs=[row_spec, head_spec, head_spec, head_spec, whole2(wuq), whole2(wukv),
                  whole2(gq), whole2(gkv), whole2(gqn), whole2(gkn), rope_spec, rope_spec],
        out_specs=[row_spec, whole2(wuq), whole2(wukv), whole2(gq), whole2(gkv), whole2(gqn), whole2(gkn)],
        out_shape=[jax.ShapeDtypeStruct((n_rows, 512), BF16),
                   jax.ShapeDtypeStruct(wuq.shape, F32), jax.ShapeDtypeStruct(wukv.shape, F32),
                   jax.ShapeDtypeStruct(gq.shape, F32), jax.ShapeDtypeStruct(gkv.shape, F32),
                   jax.ShapeDtypeStruct(gqn.shape, F32), jax.ShapeDtypeStruct(gkn.shape, F32)],
        compiler_params=_params(("arbitrary",)),
    )(zsm, dq, dk, dv, wuq, wukv, gq, gkv, gqn, gkn, *rope)


HBM_SPEC = pl.BlockSpec(memory_space=pltpu.HBM)


def _xchg_out_shapes(bufs):
    return [jax.ShapeDtypeStruct((N_DEV,) + (a.shape if gather else a.shape[1:]), a.dtype) for a, gather in bufs]


def _xchg_scratch(n_buf):
    return [pltpu.SemaphoreType.DMA((n_buf * (N_DEV - 1),)), pltpu.SemaphoreType.DMA((n_buf * (N_DEV - 1),)),
            pltpu.SemaphoreType.DMA((n_buf,))]


def _xchg_copies(src_refs, dst_refs, gathers, send_sems, recv_sems, local_sems):
    x, y, c = lax.axis_index("x"), lax.axis_index("y"), lax.axis_index("c")
    me = 4 * x + 2 * y + c
    local, starts, arrivals = [], [], []
    for bi, (src, dst, gather) in enumerate(zip(src_refs, dst_refs, gathers)):
        local.append(pltpu.make_async_copy(src if gather else src.at[me], dst.at[me], local_sems.at[bi]))
        for kk in range(1, N_DEV):
            px = 1 - x if kk & 4 else x
            py = 1 - y if kk & 2 else y
            pc = 1 - c if kk & 1 else c
            pid = 4 * px + 2 * py + pc
            sem = bi * (N_DEV - 1) + kk - 1
            starts.append(pltpu.make_async_remote_copy(
                src_ref=src if gather else src.at[pid], dst_ref=dst.at[me],
                send_sem=send_sems.at[sem], recv_sem=recv_sems.at[sem],
                device_id=(px, py, pc), device_id_type=pl.DeviceIdType.MESH))
            arrivals.append(pltpu.make_async_remote_copy(
                src_ref=src if gather else src.at[me], dst_ref=dst.at[pid],
                send_sem=send_sems.at[sem], recv_sem=recv_sems.at[sem],
                device_id=(px, py, pc), device_id_type=pl.DeviceIdType.MESH))
    return local, starts, arrivals


def _xchg_start(copies):
    local, sends, _ = copies
    for cp in local + sends:
        cp.start()


def _xchg_finish(copies):
    local, sends, arrivals = copies
    for cp in arrivals:
        cp.wait_recv()
    for cp in sends:
        cp.wait_send()
    for cp in local:
        cp.wait()


def _gather_by_chip(src_refs, dst_refs, send_sems, recv_sems, local_sems, start=True, finish=True):
    x, y, c = lax.axis_index("x"), lax.axis_index("y"), lax.axis_index("c")
    me = 4 * x + 2 * y + c
    sibling = (x, y, 1 - c)

    def place(kk):
        px = 1 - x if kk & 4 else x
        py = 1 - y if kk & 2 else y
        pc = 1 - c if kk & 1 else c
        return (px, py, pc), 4 * px + 2 * py + pc

    def copy(bi, kk, src, dst, to):
        sem = bi * (N_DEV - 1) + kk - 1
        return pltpu.make_async_remote_copy(src_ref=src, dst_ref=dst, send_sem=send_sems.at[sem],
                                            recv_sem=recv_sems.at[sem], device_id=to, device_id_type=pl.DeviceIdType.MESH)

    local, sends = [], []
    for bi, (src, dst) in enumerate(zip(src_refs, dst_refs)):
        local.append(pltpu.make_async_copy(src, dst.at[me], local_sems.at[bi]))
        sends += [copy(bi, kk, src, dst.at[me], place(kk)[0]) for kk in (1, 2, 4, 6)]
    if start:
        for cp in local + sends:
            cp.start()
    if not finish:
        return
    for kk in (2, 4, 6):
        for bi, (src, dst) in enumerate(zip(src_refs, dst_refs)):
            dev, pid = place(kk)
            copy(bi, kk, src, dst.at[pid], dev).wait_recv()
            passed = copy(bi, kk | 1, dst.at[pid], dst.at[pid], sibling)
            passed.start()
            sends.append(passed)
    for kk in (1, 3, 5, 7):
        for bi, (src, dst) in enumerate(zip(src_refs, dst_refs)):
            dev, pid = place(kk)
            copy(bi, kk, src, dst.at[pid], sibling).wait_recv()
    for cp in sends:
        cp.wait_send()
    for cp in local:
        cp.wait()


def _exchange(name, bufs, by_chip=False):
    n_buf = len(bufs)
    gathers = [g for _, g in bufs]
    assert not by_chip or all(gathers)

    def body(*refs):
        srcs, dsts = refs[:n_buf], refs[n_buf:2 * n_buf]
        if by_chip:
            _gather_by_chip(srcs, dsts, *refs[2 * n_buf:])
            return
        copies = _xchg_copies(srcs, dsts, gathers, *refs[2 * n_buf:])
        _xchg_start(copies)
        _xchg_finish(copies)

    return pl.pallas_call(
        body, name=name, out_shape=_xchg_out_shapes(bufs),
        in_specs=[HBM_SPEC] * n_buf, out_specs=[HBM_SPEC] * n_buf, scratch_shapes=_xchg_scratch(n_buf),
    )(*[a for a, _ in bufs])


def _chunk_mask(t, keys_first):
    key = lax.broadcasted_iota(jnp.int32, (t, t), 0 if keys_first else 1) // CHUNK
    query = lax.broadcasted_iota(jnp.int32, (t, t), 1 if keys_first else 0) // CHUNK
    return query >= key


def _grid_ends(grid):
    ids = [pl.program_id(ax) for ax in range(len(grid))]
    first = functools.reduce(jnp.logical_and, [i == 0 for i in ids])
    last = functools.reduce(jnp.logical_and, [i == g - 1 for i, g in zip(ids, grid)])
    return first, last


def _attn_fwd(q, k, v, bufs, n_b, seq):
    tq = min(ATT_TILE, seq)
    nq = seq // tq
    nt_dims = _DIMS["nt"]
    hpb = ATT_HEADS_FWD
    grid = (n_b, N_HEADS // hpb, nq)
    n_buf = len(bufs)
    gathers = [g for _, g in bufs]
    sum_lane = [HEAD_PAD - 1 if hh % 2 == 0 else 0 for hh in range(hpb)]

    def body(q_ref, k_ref, v_ref, *rest):
        srcs, (o_ref, lse_ref), dsts = rest[:n_buf], rest[n_buf:n_buf + 2], rest[n_buf + 2:2 * n_buf + 2]
        gather = functools.partial(_gather_by_chip, srcs, dsts, *rest[2 * n_buf + 2:])
        first, last = _grid_ends(grid)
        pl.when(first)(functools.partial(gather, start=True, finish=False))

        qi = pl.program_id(2)
        mask = _chunk_mask(tq, keys_first=False)
        lane_row = lax.broadcasted_iota(jnp.int32, (1, HEAD_PAD), 1)
        ones = [(lane_row == sum_lane[hh]).astype(BF16) for hh in range(hpb)]
        qs = [q_ref[0, hh] for hh in range(hpb)]

        def step(j, carry, masked):
            rows = pl.ds(pl.multiple_of(j * tq, tq), tq)
            out = []
            for hh in range(hpb):
                m, acc = carry[hh]
                s = lax.dot_general(qs[hh], k_ref[0, hh, rows, :], nt_dims, preferred_element_type=F32)
                if masked:
                    s = jnp.where(mask, s, NEG_BIG)
                m_new = jnp.maximum(m, jnp.max(s, axis=-1, keepdims=True))
                p = jnp.exp2(s - m_new).astype(BF16)
                acc = jnp.exp2(m - m_new) * acc + jnp.dot(p, v_ref[0, hh, rows, :] + ones[hh], preferred_element_type=F32)
                out.append((m_new, acc))
            return tuple(out)

        init = tuple((jnp.full((tq, 1), NEG_BIG, F32), jnp.zeros((tq, HEAD_PAD), F32)) for _ in range(hpb))
        online = lax.fori_loop(0, qi, functools.partial(step, masked=False), init)
        online = step(qi, online, True)
        ms, accs = [c[0] for c in online], [c[1] for c in online]
        carry = list(zip(ms, accs))
        lane = lax.broadcasted_iota(jnp.int32, (tq, HEAD_PAD), 1)
        for pair in range(hpb // 2):
            outs = []
            for hh in (2 * pair, 2 * pair + 1):
                m, acc = carry[hh]
                l = jnp.sum(jnp.where(lane == sum_lane[hh], acc, 0.0), axis=-1, keepdims=True)
                outs.append(acc * (1.0 / l))
                lse_ref[0, hh] = jnp.broadcast_to(m + jnp.log2(l), (tq, HEAD_PAD)).T[0:8, :]
            o_ref[0, :, pair * HEAD_PAD:(pair + 1) * HEAD_PAD] = jnp.where(lane < V_HEAD_DIM, outs[0], outs[1]).astype(BF16)

        pl.when(last)(functools.partial(gather, start=False, finish=True))

    kv_spec = pl.BlockSpec((1, hpb, seq, HEAD_PAD), lambda b, hb, i: (b, hb, 0, 0))
    q_spec = pl.BlockSpec((1, hpb, tq, HEAD_PAD), lambda b, hb, i: (b, hb, i, 0))
    res = pl.pallas_call(
        body, name="attn_fwd", grid=grid,
        in_specs=[q_spec, kv_spec, kv_spec] + [HBM_SPEC] * n_buf,
        out_specs=[pl.BlockSpec((1, tq, hpb * V_HEAD_DIM), lambda b, hb, i: (b, i, hb)),
                   pl.BlockSpec((1, hpb, 8, tq), lambda b, hb, i: (b, hb, 0, i))] + [HBM_SPEC] * n_buf,
        out_shape=[jax.ShapeDtypeStruct((n_b, seq, N_HEADS * V_HEAD_DIM), BF16),
                   jax.ShapeDtypeStruct((n_b, N_HEADS, 8, seq), F32)] + _xchg_out_shapes(bufs),
        scratch_shapes=_xchg_scratch(n_buf),
        compiler_params=_params(("arbitrary", "arbitrary", "arbitrary")),
    )(q, k, v, *[a for a, _ in bufs])
    return res[0], res[1], res[2:]


def _attn_bwd(q, k, v, kt, do, o, lse, bufs, n_b, seq):
    tq = min(ATT_TILE, seq)
    nq = seq // tq
    nt_dims = _DIMS["nt"]
    hpb = ATT_HEADS
    grid = (n_b, N_HEADS // hpb, nq)
    n_buf = len(bufs)
    gathers = [g for _, g in bufs]

    def body(q_ref, k_ref, v_ref, kt_ref, do_ref, o_ref, lse_ref, *rest):
        srcs, (dq_ref, dk_ref, dv_ref), dsts = rest[:n_buf], rest[n_buf:n_buf + 3], rest[n_buf + 3:2 * n_buf + 3]
        dk_acc, dv_acc = rest[2 * n_buf + 3:2 * n_buf + 5]
        copies = _xchg_copies(srcs, dsts, gathers, *rest[2 * n_buf + 5:])
        first, last = _grid_ends(grid)
        pl.when(first)(functools.partial(_xchg_start, copies))

        qi = pl.program_id(2)

        @pl.when(qi == 0)
        def _():
            dk_acc[...] = jnp.zeros_like(dk_acc)
            dv_acc[...] = jnp.zeros_like(dv_acc)

        mask = _chunk_mask(tq, keys_first=True)
        lane = lax.broadcasted_iota(jnp.int32, (tq, HEAD_PAD), 1)
        qs, dos, deltas, lses = [], [], [], []
        for hh in range(hpb):
            cols = slice((hh // 2) * HEAD_PAD, (hh // 2 + 1) * HEAD_PAD)
            do_pair = do_ref[0, :, cols]
            prod = do_pair.astype(F32) * o_ref[0, :, cols].astype(F32)
            delta = jnp.sum(jnp.where(lane // V_HEAD_DIM == hh % 2, prod, 0.0), axis=-1, keepdims=True)
            qs.append(q_ref[0, hh])
            dos.append(do_pair)
            deltas.append(jnp.broadcast_to(delta, (tq, HEAD_PAD)).T[0:1, :])
            lses.append(lse_ref[0, hh][0:1, :])

        def step(j, dqs, masked):
            rows = pl.ds(pl.multiple_of(j * tq, tq), tq)
            out = []
            for hh in range(hpb):
                s = lax.dot_general(k_ref[0, hh, rows, :], qs[hh], nt_dims, preferred_element_type=F32)
                p = jnp.exp2(s - lses[hh])
                if masked:
                    p = jnp.where(mask, p, 0.0)
                dv_acc[hh, rows, :] += jnp.dot(p.astype(BF16), dos[hh], preferred_element_type=F32)
                dp = lax.dot_general(v_ref[0, hh, rows, :], dos[hh], nt_dims, preferred_element_type=F32)
                ds = (p * (dp - deltas[hh])).astype(BF16)
                dk_acc[hh, rows, :] += jnp.dot(ds, qs[hh], preferred_element_type=F32)
                out.append(dqs[hh] + jnp.dot(kt_ref[0, hh, j], ds, preferred_element_type=F32))
            return tuple(out)

        dqs = tuple(jnp.zeros((HEAD_PAD, tq), F32) for _ in range(hpb))
        dqs = lax.fori_loop(0, qi, functools.partial(step, masked=False), dqs)
        dqs = step(qi, dqs, True)
        for hh in range(hpb):
            dq_ref[0, hh] = dqs[hh].T.astype(BF16)

        @pl.when(qi == nq - 1)
        def _():
            dk_ref[0] = (dk_acc[...] * LN2).astype(BF16)
            dv_ref[0] = dv_acc[...].astype(BF16)

        pl.when(last)(functools.partial(_xchg_finish, copies))

    full_spec = pl.BlockSpec((1, hpb, seq, HEAD_PAD), lambda b, hb, i: (b, hb, 0, 0))
    t_spec = pl.BlockSpec((1, hpb, nq, HEAD_PAD, tq), lambda b, hb, i: (b, hb, 0, 0, 0))
    q_spec = pl.BlockSpec((1, hpb, tq, HEAD_PAD), lambda b, hb, i: (b, hb, i, 0))
    o_spec = pl.BlockSpec((1, tq, hpb * V_HEAD_DIM), lambda b, hb, i: (b, i, hb))
    lse_spec = pl.BlockSpec((1, hpb, 8, tq), lambda b, hb, i: (b, hb, 0, i))
    head_shape = jax.ShapeDtypeStruct((n_b, N_HEADS, seq, HEAD_PAD), BF16)
    res = pl.pallas_call(
        body, name="attn_bwd", grid=grid,
        in_specs=[q_spec, full_spec, full_spec, t_spec, o_spec, o_spec, lse_spec] + [HBM_SPEC] * n_buf,
        out_specs=[q_spec, full_spec, full_spec] + [HBM_SPEC] * n_buf,
        out_shape=[head_shape] * 3 + _xchg_out_shapes(bufs),
        scratch_shapes=[pltpu.VMEM((hpb, seq, HEAD_PAD), F32), pltpu.VMEM((hpb, seq, HEAD_PAD), F32)]
        + _xchg_scratch(n_buf),
        compiler_params=_params(("arbitrary", "arbitrary", "arbitrary")),
    )(q, k, v, kt, do, o, lse, *[a for a, _ in bufs])
    return res[0], res[1], res[2], res[3:]


def _in_proj_fwd(x2, scale, shift, g, w_parts, z_dtypes, seq):
    n_rows, d = x2.shape
    tm = min(512, seq)
    per_seq = seq // tm
    n_part = len(w_parts)
    nt_dims = _DIMS["nt"]

    def body(x_ref, sc_ref, sh_ref, g_ref, *rest):
        w_refs, h_ref, z_refs = rest[:n_part], rest[n_part], rest[n_part + 1:]
        h = _norm_mod(x_ref[...], g_ref[...], sc_ref[0], sh_ref[0]).astype(BF16)
        h_ref[...] = h
        for w_ref, z_ref in zip(w_refs, z_refs):
            z_ref[...] = lax.dot_general(h, w_ref[...], nt_dims, preferred_element_type=F32).astype(z_ref.dtype)

    row = lambda width: pl.BlockSpec((tm, width), lambda i: (i, 0))
    bat = pl.BlockSpec((1, 1, d), lambda i: (i // per_seq, 0, 0))
    whole = lambda arr: pl.BlockSpec(arr.shape, lambda i: (0, 0))
    return pl.pallas_call(
        body, name="in_proj_fwd", grid=(n_rows // tm,),
        in_specs=[row(d), bat, bat, whole(g)] + [whole(w) for w in w_parts],
        out_specs=[row(d)] + [row(w.shape[0]) for w in w_parts],
        out_shape=[jax.ShapeDtypeStruct((n_rows, d), BF16)]
        + [jax.ShapeDtypeStruct((n_rows, w.shape[0]), dt) for w, dt in zip(w_parts, z_dtypes)],
        compiler_params=_params(("parallel",)),
    )(x2, scale, shift, g, *w_parts)
def _in_proj_bwd(parts, x2, dx1, scale, g, bufs, seq):
    n_rows, d = x2.shape
    tm = min(512, seq)
    per_seq = seq // tm
    grid = (n_rows // tm,)
    n_part, n_buf = len(parts), len(bufs)
    gathers = [gt for _, gt in bufs]

    def body(*refs):
        dz_refs, w_refs = refs[:n_part], refs[n_part:2 * n_part]
        x_ref, dx1_ref, sc_ref, g_ref = refs[2 * n_part:2 * n_part + 4]
        srcs = refs[2 * n_part + 4:2 * n_part + 4 + n_buf]
        gx_ref, dsc_ref, dsh_ref, dg_ref = refs[2 * n_part + 4 + n_buf:2 * n_part + 8 + n_buf]
        dsts = refs[2 * n_part + 8 + n_buf:2 * n_part + 8 + 2 * n_buf]
        copies = _xchg_copies(srcs, dsts, gathers, *refs[2 * n_part + 8 + 2 * n_buf:])
        first, last = _grid_ends(grid)
        pl.when(first)(functools.partial(_xchg_start, copies))

        i = pl.program_id(0)
        dh = None
        for dz_ref, w_ref in zip(dz_refs, w_refs):
            term = jnp.dot(dz_ref[...], w_ref[...], preferred_element_type=F32)
            dh = term if dh is None else dh + term
        dx, dsc, dsh, dg = _norm_mod_bwd(x_ref[...], g_ref[...], sc_ref[0], dh)
        gx_ref[...] = dx1_ref[...] + dx

        @pl.when(i % per_seq == 0)
        def _():
            dsc_ref[...] = jnp.zeros_like(dsc_ref)
            dsh_ref[...] = jnp.zeros_like(dsh_ref)

        @pl.when(i == 0)
        def _():
            dg_ref[...] = jnp.zeros_like(dg_ref)

        dsc_ref[0] += dsc
        dsh_ref[0] += dsh
        dg_ref[...] += dg
        pl.when(last)(functools.partial(_xchg_finish, copies))

    row = lambda width: pl.BlockSpec((tm, width), lambda i: (i, 0))
    bat = pl.BlockSpec((1, 1, d), lambda i: (i // per_seq, 0, 0))
    whole = lambda arr: pl.BlockSpec(arr.shape, lambda i: (0, 0))
    n_b = n_rows // seq
    res = pl.pallas_call(
        body, name="in_proj_bwd", grid=grid,
        in_specs=[row(dz.shape[1]) for dz, _ in parts] + [whole(w) for _, w in parts]
        + [row(d), row(d), bat, whole(g)] + [HBM_SPEC] * n_buf,
        out_specs=[row(d), bat, bat, whole(g)] + [HBM_SPEC] * n_buf,
        out_shape=[jax.ShapeDtypeStruct((n_rows, d), F32), jax.ShapeDtypeStruct((n_b, 1, d), F32),
                   jax.ShapeDtypeStruct((n_b, 1, d), F32), jax.ShapeDtypeStruct(g.shape, F32)] + _xchg_out_shapes(bufs),
        scratch_shapes=_xchg_scratch(n_buf),
        compiler_params=_params(("arbitrary",)),
    )(*[dz for dz, _ in parts], *[w for _, w in parts], x2, dx1, scale, g, *[a for a, _ in bufs])
    return res[0], res[1], res[2], res[3], res[4:]


def _ln_silu(u1, g, b):
    mu = jnp.mean(u1, axis=-1, keepdims=True)
    uc = u1 - mu
    r = lax.rsqrt(jnp.mean(uc * uc, axis=-1, keepdims=True) + EPS)
    y = uc * r * g + b
    return y * _sigmoid(y)


def _conv_fill_glu(z_ref, u0_ref, seq, tile):
    u0_ref[0:CONV_HALO, :] = jnp.zeros((CONV_HALO, CONV_CH), F32)
    u0_ref[CONV_HALO + seq:CONV_HALO + seq + CONV_TAIL, :] = jnp.zeros((CONV_TAIL, CONV_CH), F32)
    for t in range(seq // tile):
        zt = z_ref[0, t * tile:(t + 1) * tile, :].astype(F32)
        u0_ref[CONV_HALO + t * tile:CONV_HALO + (t + 1) * tile, :] = zt[:, :CONV_CH] * _sigmoid(zt[:, CONV_CH:])


def _conv_windows(ref, views_ref, t, tile):
    for b in range(8):
        views_ref[b] = ref[t * tile + b:t * tile + b + tile + CONV_HALO, :]


def _conv_tap(views_ref, offset, tile):
    return views_ref[offset % 8, 8 * (offset // 8):8 * (offset // 8) + tile, :]


def _conv_tile(u0_ref, views_ref, w_ref, b_ref, t, tile):
    _conv_windows(u0_ref, views_ref, t, tile)
    acc = jnp.broadcast_to(b_ref[...], (tile, CONV_CH))
    for kk in range(CONV_WIDTH):
        acc = acc + w_ref[kk:kk + 1, :] * _conv_tap(views_ref, kk + CONV_HALO - (CONV_WIDTH - 1), tile)
    return acc


def _conv_fwd(zglu, conv_w, conv_b, ln_g, ln_b, n_b, seq):
    tile = min(256, seq)

    def body(z_ref, w_ref, b_ref, g_ref, bb_ref, o_ref, u1_ref, u0_ref, views_ref):
        _conv_fill_glu(z_ref, u0_ref, seq, tile)
        for t in range(seq // tile):
            u1 = _conv_tile(u0_ref, views_ref, w_ref, b_ref, t, tile)
            u1_ref[0, t * tile:(t + 1) * tile, :] = u1
            o_ref[0, t * tile:(t + 1) * tile, :] = _ln_silu(u1, g_ref[...], bb_ref[...]).astype(BF16)

    whole2 = lambda arr: pl.BlockSpec(arr.shape, lambda b: (0, 0))
    seq_spec = pl.BlockSpec((1, seq, CONV_CH), lambda b: (b, 0, 0))
    return pl.pallas_call(
        body, name="conv_fwd", grid=(n_b,),
        in_specs=[pl.BlockSpec((1, seq, 2 * CONV_CH), lambda b: (b, 0, 0)), whole2(conv_w), whole2(conv_b),
                  whole2(ln_g), whole2(ln_b)],
        out_specs=[seq_spec, seq_spec],
        out_shape=[jax.ShapeDtypeStruct((n_b, seq, CONV_CH), BF16), jax.ShapeDtypeStruct((n_b, seq, CONV_CH), F32)],
        scratch_shapes=[pltpu.VMEM((seq + CONV_HALO + CONV_TAIL, CONV_CH), F32),
                        pltpu.VMEM((8, tile + CONV_HALO, CONV_CH), F32)],
        compiler_params=_params(("parallel",)),
    )(zglu, conv_w, conv_b, ln_g, ln_b)


def _conv_bwd(zglu, u1_saved, du3, conv_w, ln_g, ln_b, n_b, seq):
    tile = min(256, seq)
    n_t = seq // tile

    def body(z_ref, u1_ref, du3_ref, w_ref, g_ref, bb_ref, dz_ref, dw_ref, db_ref, dg_ref, dbb_ref, u0_ref, du1_ref,
             u0_views, du1_views):
        @pl.when(pl.program_id(0) == 0)
        def _():
            for r in (dw_ref, db_ref, dg_ref, dbb_ref):
                r[...] = jnp.zeros_like(r)

        _conv_fill_glu(z_ref, u0_ref, seq, tile)
        du1_ref[seq:seq + CONV_HALO + CONV_TAIL, :] = jnp.zeros((CONV_HALO + CONV_TAIL, CONV_CH), F32)
        g = g_ref[...]
        for t in range(n_t):
            u1 = u1_ref[0, t * tile:(t + 1) * tile, :]
            mu = jnp.mean(u1, axis=-1, keepdims=True)
            uc = u1 - mu
            r = lax.rsqrt(jnp.mean(uc * uc, axis=-1, keepdims=True) + EPS)
            xh = uc * r
            y = xh * g + bb_ref[...]
            sg = _sigmoid(y)
            dy = du3_ref[0, t * tile:(t + 1) * tile, :].astype(F32) * (sg * (1.0 + y * (1.0 - sg)))
            dg_ref[...] += jnp.sum(dy * xh, axis=0, keepdims=True)
            dbb_ref[...] += jnp.sum(dy, axis=0, keepdims=True)
            dxh = dy * g
            du1 = r * (dxh - jnp.mean(dxh, axis=-1, keepdims=True) - xh * jnp.mean(dxh * xh, axis=-1, keepdims=True))
            db_ref[...] += jnp.sum(du1, axis=0, keepdims=True)
            du1_ref[t * tile:(t + 1) * tile, :] = du1
        for t in range(n_t):
            du1 = du1_ref[t * tile:(t + 1) * tile, :]
            du0 = jnp.zeros((tile, CONV_CH), F32)
            _conv_windows(u0_ref, u0_views, t, tile)
            _conv_windows(du1_ref, du1_views, t, tile)
            for kk in range(CONV_WIDTH):
                du0 = du0 + w_ref[kk:kk + 1, :] * _conv_tap(du1_views, CONV_WIDTH - 1 - kk, tile)
                u0_tap = _conv_tap(u0_views, kk + CONV_HALO - (CONV_WIDTH - 1), tile)
                dw_ref[kk:kk + 1, :] += jnp.sum(du1 * u0_tap, axis=0, keepdims=True)
            zt = z_ref[0, t * tile:(t + 1) * tile, :].astype(F32)
            ga, sb = zt[:, :CONV_CH], _sigmoid(zt[:, CONV_CH:])
            dz_ref[0, t * tile:(t + 1) * tile, :CONV_CH] = (du0 * sb).astype(BF16)
            dz_ref[0, t * tile:(t + 1) * tile, CONV_CH:] = (du0 * ga * sb * (1.0 - sb)).astype(BF16)

    whole2 = lambda arr: pl.BlockSpec(arr.shape, lambda b: (0, 0))
    z_spec = pl.BlockSpec((1, seq, 2 * CONV_CH), lambda b: (b, 0, 0))
    seq_spec = pl.BlockSpec((1, seq, CONV_CH), lambda b: (b, 0, 0))
    return pl.pallas_call(
        body, name="conv_bwd", grid=(n_b,),
        in_specs=[z_spec, seq_spec, seq_spec, whole2(conv_w), whole2(ln_g), whole2(ln_b)],
        out_specs=[z_spec, whole2(conv_w), whole2(ln_g), whole2(ln_g), whole2(ln_b)],
        out_shape=[jax.ShapeDtypeStruct((n_b, seq, 2 * CONV_CH), BF16), jax.ShapeDtypeStruct(conv_w.shape, F32),
                   jax.ShapeDtypeStruct(ln_g.shape, F32), jax.ShapeDtypeStruct(ln_g.shape, F32),
                   jax.ShapeDtypeStruct(ln_b.shape, F32)],
        scratch_shapes=[pltpu.VMEM((seq + CONV_HALO + CONV_TAIL, CONV_CH), F32)] * 2
        + [pltpu.VMEM((8, tile + CONV_HALO, CONV_CH), F32)] * 2,
        compiler_params=_params(("arbitrary",)),
    )(zglu, u1_saved, du3, conv_w, ln_g, ln_b)


def _sum_parts(name, parts):
    n_parts = parts.shape[0]

    def body(p_ref, o_ref):
        gg = p_ref[0].astype(F32)
        for j in range(1, n_parts):
            gg = gg + p_ref[j].astype(F32)
        o_ref[...] = gg

    return pl.pallas_call(body, name=name, out_shape=jax.ShapeDtypeStruct(parts.shape[1:], F32),
                          compiler_params=_params(None))(parts)


def _adamw(name, w, parts, m, v, transposed=False):
    n_parts = parts.shape[0]
    rows, cols = w.shape
    tr = ADAM_ROWS if rows % ADAM_ROWS == 0 else rows

    def body(w_ref, p_ref, m_ref, v_ref, g_ref, d_ref, nm_ref, nv_ref):
        gg = p_ref[0].astype(F32)
        for j in range(1, n_parts):
            gg = gg + p_ref[j].astype(F32)
        if transposed:
            gg = gg.T
        nm = ADAM_B1 * m_ref[...] + (1.0 - ADAM_B1) * gg
        nv = ADAM_B2 * v_ref[...] + (1.0 - ADAM_B2) * jnp.square(gg)
        m_hat = nm / (1.0 - ADAM_B1 ** ADAM_STEP)
        v_hat = nv / (1.0 - ADAM_B2 ** ADAM_STEP)
        g_ref[...] = gg
        d_ref[...] = -ADAM_LR * (m_hat / (jnp.sqrt(v_hat) + ADAM_EPS) + ADAM_WD * w_ref[...])
        nm_ref[...] = nm
        nv_ref[...] = nv

    shape = jax.ShapeDtypeStruct(w.shape, F32)
    blk = pl.BlockSpec((tr, cols), lambda i: (i, 0))
    p_spec = (pl.BlockSpec((n_parts, cols, tr), lambda i: (0, 0, i)) if transposed
              else pl.BlockSpec((n_parts, tr, cols), lambda i: (0, i, 0)))
    return pl.pallas_call(body, name=name, grid=(rows // tr,), in_specs=[blk, p_spec, blk, blk], out_specs=[blk] * 4,
                          out_shape=[shape] * 4, compiler_params=_params(("parallel",)))(w, parts, m, v)


def _rope_tables(seq):
    inv_freq = ROPE_THETA ** (-jnp.arange(0, QK_ROPE_DIM, 2, dtype=F32) / QK_ROPE_DIM)
    ang = jnp.arange(seq, dtype=F32)[:, None] * inv_freq[None, :]
    cos, sin = jnp.cos(ang), jnp.sin(ang)
    half = QK_ROPE_DIM // 2
    lane_half = HEAD_PAD // 2
    one = lambda n: jnp.ones((seq, n), F32)
    z = lambda n: jnp.zeros((seq, n), F32)
    used_hi = QK_HEAD_DIM - lane_half - half
    cos_t = jnp.concatenate([cos, one(lane_half - half), cos, one(used_hi), z(lane_half - half - used_hi)], axis=1)
    sin_t = jnp.concatenate([-sin, z(lane_half - half), sin, z(lane_half - half)], axis=1)
    return cos_t, sin_t


def _pad_lanes(v, width=HEAD_PAD):
    return jnp.pad(v, [(0, 0)] * (v.ndim - 1) + [(0, width - v.shape[-1])])


_LANE_HALF_NOPE = HEAD_PAD // 2 - QK_ROPE_DIM // 2


def _head_lanes(v):
    rot = v[..., QK_NOPE_DIM:]
    half = QK_ROPE_DIM // 2
    return _pad_lanes(jnp.concatenate([rot[..., :half], v[..., :_LANE_HALF_NOPE], rot[..., half:],
                                       v[..., _LANE_HALF_NOPE:QK_NOPE_DIM]], axis=-1))


def _head_dims(g):
    half = QK_ROPE_DIM // 2
    lane_half = HEAD_PAD // 2
    return jnp.concatenate([g[..., half:lane_half], g[..., lane_half + half:QK_HEAD_DIM], g[..., :half],
                            g[..., lane_half:lane_half + half]], axis=-1)


def _unstack_cols(s):
    return s.transpose(1, 0, 2).reshape(s.shape[1], N_DEV * s.shape[2])


def _stack_cols(g, dtype):
    rows, cols = g.shape
    return g.reshape(rows, N_DEV, cols // N_DEV).transpose(1, 0, 2).astype(dtype)


def kernel(x, c, w_ada, b_ada, norm1_g, w_in, q_latent_g, w_uq, kv_latent_g, w_ukv, qk_norm_q_g, qk_norm_k_g, w_o_mla, conv_w, conv_b, conv_ln_g, conv_ln_b, w_pw_out, w_out, norm2_g, w_ff1, w_ff2, loss_target, m_w_ada, m_b_ada, m_norm1_g, m_w_in, m_q_latent_g, m_w_uq, m_kv_latent_g, m_w_ukv, m_qk_norm_q_g, m_qk_norm_k_g, m_w_o_mla, m_conv_w, m_conv_b, m_conv_ln_g, m_conv_ln_b, m_w_pw_out, m_w_out, m_norm2_g, m_w_ff1, m_w_ff2, v_w_ada, v_b_ada, v_norm1_g, v_w_in, v_q_latent_g, v_w_uq, v_kv_latent_g, v_w_ukv, v_qk_norm_q_g, v_qk_norm_k_g, v_w_o_mla, v_conv_w, v_conv_b, v_conv_ln_g, v_conv_ln_b, v_w_pw_out, v_w_out, v_norm2_g, v_w_ff1, v_w_ff2):
    given = dict(locals())
    local = {n: given[n][0] for n in WEIGHTS}
    vec = {n: local[n].reshape(1, -1) for n in REPLICATED}
    bf = lambda n: local[n].astype(BF16)
    n_b, seq, d = x.shape
    n_rows = n_b * seq
    x2 = x.reshape(n_rows, d)
    t2 = loss_target.reshape(n_rows, d)
    me = 4 * lax.axis_index("x") + 2 * lax.axis_index("y") + lax.axis_index("c")
    ada_cols = local["w_ada"].shape[1]

    tsh = lambda n: local[n].T.astype(BF16)
    c_all, w_in_s, w_uq_s, w_ukv_s, conv_w_s = _exchange(
        "gather_early", [(c, True), (tsh("w_in"), True), (bf("w_uq"), True), (bf("w_ukv"), True), (local["conv_w"], True)],
        by_chip=True)
    w_in_t = w_in_s.reshape(-1, d)
    zrows = lambda n: jnp.zeros((n, d), BF16)
    rot_half = QK_ROPE_DIM // 2
    w_sm_t = jnp.concatenate([w_in_t[:OFF_KV + rot_half], zrows(HEAD_PAD // 2 - rot_half), w_in_t[OFF_KV + rot_half:OFF_KR],
                              zrows(HEAD_PAD // 2 - rot_half)], axis=0)
    w_glu_t = w_in_t[OFF_KR:OFF_GLU]
    w_gate_t = w_in_t[OFF_GLU:]
    wuq = _head_lanes(_unstack_cols(w_uq_s).reshape(Q_LORA, N_HEADS, QK_HEAD_DIM)).reshape(Q_LORA, N_HEADS * HEAD_PAD)
    wukv_f = _unstack_cols(w_ukv_s).reshape(KV_LORA, N_HEADS, QK_NOPE_DIM + V_HEAD_DIM)
    wv = wukv_f[:, :, QK_NOPE_DIM:]
    odd = (jnp.arange(N_HEADS) % 2 == 1)[None, :, None]
    wuv = jnp.where(odd, jnp.pad(wv, ((0, 0), (0, 0), (V_HEAD_DIM, 0))), jnp.pad(wv, ((0, 0), (0, 0), (0, V_HEAD_DIM))))
    wuk = _head_lanes(_pad_lanes(wukv_f[:, :, :QK_NOPE_DIM], QK_HEAD_DIM))
    wukv = jnp.concatenate([wuk, wuv], axis=1).reshape(KV_LORA, 2 * N_HEADS * HEAD_PAD)
    gqn = _head_lanes(vec["qk_norm_q_g"])
    gkn = _head_lanes(vec["qk_norm_k_g"])
    conv_w_f = jnp.pad(_unstack_cols(conv_w_s), ((0, 1), (0, 0)))
    rope = _rope_tables(seq)

    all_rows = N_DEV * n_b
    pad_rows = (-all_rows) % ROWS_PAD
    c_rows = jnp.pad(c_all.reshape(all_rows, d), ((0, pad_rows), (0, 0)))
    b_cols = lax.dynamic_slice(local["b_ada"], (me * ada_cols,), (ada_cols,))
    mod_cols = _mm("ada_fwd", c_rows, local["w_ada"], "nn", F32, a_fn=_silu, epi=lambda acc, b: acc + b,
                   epi_in=(jnp.broadcast_to(b_cols, (all_rows + pad_rows, ada_cols)),))
    (mod_s,) = _exchange("scatter_mod", [(mod_cols[:all_rows].reshape(N_DEV, n_b, ada_cols), False)])
    mod = mod_s.transpose(1, 0, 2).reshape(n_b, ADA_CHUNKS, 1, d)
    shift1, scale1, gate1, shift2, scale2, gate2 = [mod[:, i] for i in range(ADA_CHUNKS)]

    h, zgate, zglu, zsm = _in_proj_fwd(x2, scale1, shift1, vec["norm1_g"], [w_gate_t, w_glu_t, w_sm_t],
                                       [BF16, BF16, F32], seq)
    q, k, v, kt = _mla_prep_fwd(zsm, wuq, wukv, vec["q_latent_g"], vec["kv_latent_g"], gqn, gkn, rope, n_b, seq)
    attn, lse, (w_o_s, w_pw_s, w_out_s, w_ff1_s, w_ff2_s) = _attn_fwd(
        q, k, v, [(tsh("w_o_mla"), True), (tsh("w_pw_out"), True), (bf("w_out"), True), (tsh("w_ff1"), True),
                  (bf("w_ff2"), True)], n_b, seq)
    w_o_t = w_o_s.reshape(d, -1)
    w_pw_t = w_pw_s.reshape(d, -1)
    w_out_f = w_out_s.reshape(d, d)
    w_ff1_t = w_ff1_s.reshape(-1, d)
    w_ff2_f = w_ff2_s.reshape(-1, d)
    attn2 = attn.reshape(n_rows, N_HEADS * V_HEAD_DIM)
    u3, u1 = _conv_fwd(zglu.reshape(n_b, seq, 2 * CONV_CH), conv_w_f, vec["conv_b"], vec["conv_ln_g"], vec["conv_ln_b"], n_b, seq)
    u32 = u3.reshape(n_rows, CONV_CH)
    ya = _mm("mla_out", attn2, w_o_t, "nt", BF16)
    yb = _mm("conv_out", u32, w_pw_t, "nt", BF16)
    mmr = functools.partial(_mm_rows, n_rows=n_rows, seq=seq)

    def merge_fn(t):
        return _sigmoid(t[0]) * t[2] + _sigmoid(t[1]) * t[3]

    def mid_fn(acc, r, b, cc):
        x1_ = r[0] + b[0] * acc
        return [acc, x1_, _norm_mod(x1_, cc[0], b[1], b[2])], [], []

    mrg, mixed, x1, h2 = mmr("out_proj", [(zgate, d, 0), (zgate, d, 1), (ya, d, 0), (yb, d, 0)], merge_fn, w_out_f, "nn",
                             mid_fn, rows=[_full(x2)], bats=[gate1, scale2, shift2], consts=[vec["norm2_g"]],
                             outs=[(d, BF16), (d, F32), (d, BF16)], a_out=BF16)

    a = _mm("ff1", h2, w_ff1_t, "nt", BF16)

    def loss_fn(ff, r, b, cc):
        err = r[0] + b[0] * ff - r[1]
        dy_ = err * (1.0 / d)
        sq = jnp.broadcast_to(jnp.sum(err * err, keepdims=True), (1, LANES))
        return [dy_, b[0] * dy_], [jnp.sum(dy_ * ff, axis=0, keepdims=True)], [sq]

    dy, df, dgate2, sq_err = mmr("ff2_loss", [(a, a.shape[1], 0)], lambda t: _relu2(t[0]), w_ff2_f, "nn", loss_fn,
                                 rows=[_full(x1), _full(t2)], bats=[gate2], outs=[(d, F32), (d, BF16)], bat_outs=[d],
                                 tot_outs=[(1, LANES)], tk=a.shape[1])

    da = _mm("ff2_bwd", df, w_ff2_f, "nt", BF16, epi=lambda acc, av: acc * 2.0 * jnp.maximum(av, 0.0), epi_in=(a,))
    g_ff2 = _mm("ff2_dw", a, df, "tn", BF16, a_fn=_relu2)
    g_ff1_t = _mm("ff1_dw", da, h2, "tn", BF16)

    def mid_bwd(dh2_, r, b, cc):
        dx, dsc, dsh, dg = _norm_mod_bwd(r[0], cc[0], b[0], dh2_)
        dx1_ = r[1] + dx
        return [dx1_, b[1] * dx1_], [dsc, dsh, jnp.sum(dx1_ * r[2].astype(F32), axis=0, keepdims=True)], [dg]

    dx1, dmixed, dscale2, dshift2, dgate1, g_norm2 = mmr(
        "ff1_bwd", [(da, da.shape[1], 0)], None, w_ff1_t, "nn", mid_bwd, rows=[_full(x1), _full(dy), _full(mixed)],
        bats=[scale2, gate1], consts=[vec["norm2_g"]], outs=[(d, F32), (d, BF16)], bat_outs=[d, d, d],
        tot_outs=[(1, d)], tk=da.shape[1])

    g_out = _mm("out_proj_dw", mrg, dmixed, "tn", BF16)

    def merge_bwd(dm, r, b, cc):
        ya_, yb_ = r[2].astype(F32), r[3].astype(F32)
        sa, sb = _sigmoid(r[0].astype(F32)), _sigmoid(r[1].astype(F32))
        return [dm * ya_ * sa * (1.0 - sa), dm * yb_ * sb * (1.0 - sb), dm * sa, dm * sb], [], []

    dzga, dzgb, dya, dyb = mmr("out_proj_bwd", [(dmixed, d, 0)], None, w_out_f, "nt", merge_bwd,
                               rows=[(zgate, d, 0), (zgate, d, 1), _full(ya), _full(yb)], outs=[(d, BF16)] * 4)
    dattn = _mm("mla_out_bwd", dya, w_o_t, "nn", BF16)
    g_o_t = _mm("mla_out_dw", dya, attn2, "tn", BF16)
    du3 = _mm("conv_out_bwd", dyb, w_pw_t, "nn", BF16)
    g_pw_t = _mm("conv_out_dw", dyb, u32, "tn", BF16)

    dzglu, g_conv_w, g_conv_b, g_ln_g, g_ln_b = _conv_bwd(
        zglu.reshape(n_b, seq, 2 * CONV_CH), u1, du3.reshape(n_b, seq, CONV_CH), conv_w_f, vec["conv_ln_g"],
        vec["conv_ln_b"], n_b, seq)
    dzglu = dzglu.reshape(n_rows, 2 * CONV_CH)

    dq, dk, dv, (p_ff2, p_ff1, p_out, p_pw, p_o) = _attn_bwd(
        q, k, v, kt, dattn.reshape(n_b, seq, N_HEADS * V_HEAD_DIM), attn, lse,
        [(g_ff2.reshape(N_DEV, -1, d), False), (g_ff1_t.reshape(N_DEV, -1, d), False), (g_out.reshape(N_DEV, -1, d), False),
         (g_pw_t.reshape(N_DEV, -1, CONV_CH), False), (g_o_t.reshape(N_DEV, -1, N_HEADS * V_HEAD_DIM), False)], n_b, seq)
    dzsm, g_wuq, g_wukv, g_gq, g_gkv, g_gqn, g_gkn = _mla_prep_bwd(
        zsm, dq, dk, dv, wuq, wukv, vec["q_latent_g"], vec["kv_latent_g"], gqn, gkn, rope, n_b, seq)

    g_gate_a_t = _mm("in_proj_gate_dw_a", dzga, h, "tn", BF16)
    g_gate_b_t = _mm("in_proj_gate_dw_b", dzgb, h, "tn", BF16)
    g_glu_t = _mm("in_proj_glu_dw", dzglu, h, "tn", BF16)
    g_sm_t = _mm("in_proj_sm_dw", dzsm, h, "tn", BF16)
    g_in_t = jnp.concatenate([g_sm_t[:OFF_KV + rot_half], g_sm_t[OFF_KV + HEAD_PAD // 2:OFF_KV + HEAD_PAD // 2 + rot_half],
                              g_glu_t, g_gate_a_t, g_gate_b_t], axis=0)
    g_uq = _head_dims(g_wuq.reshape(Q_LORA, N_HEADS, HEAD_PAD)).reshape(Q_LORA, N_HEADS * QK_HEAD_DIM)
    g_wukv = g_wukv.reshape(KV_LORA, 2, N_HEADS, HEAD_PAD)
    g_v = jnp.where(odd, g_wukv[:, 1, :, V_HEAD_DIM:], g_wukv[:, 1, :, :V_HEAD_DIM])
    g_ukv = jnp.concatenate([_head_dims(g_wukv[:, 0])[:, :, :QK_NOPE_DIM], g_v], axis=2).reshape(KV_LORA, -1)

    grad_x, dscale1, dshift1, g_norm1, (p_in, p_uq, p_ukv, p_conv_w) = _in_proj_bwd(
        [(dzga, w_gate_t[:d]), (dzgb, w_gate_t[d:]), (dzglu, w_glu_t), (dzsm, w_sm_t)], x2, dx1, scale1, vec["norm1_g"],
        [(g_in_t.reshape(N_DEV, -1, d), False), (_stack_cols(g_uq, BF16), False), (_stack_cols(g_ukv, BF16), False),
         (_stack_cols(g_conv_w[:CONV_WIDTH], F32), False)], seq)

    dmod = jnp.concatenate([dshift1, dscale1, dgate1, dshift2, dscale2, dgate2], axis=1).reshape(n_b, N_DEV, ada_cols)
    (dmod_s,) = _exchange("scatter_dmod", [(dmod.transpose(1, 0, 2), False)])
    dmod_rows = jnp.pad(dmod_s.reshape(all_rows, ada_cols), ((0, pad_rows), (0, 0)))
    g_ada = _mm("ada_dw", c_rows, dmod_rows, "tn", F32, a_fn=_silu)
    (g_b_cols,) = _rowwise("ada_db", lambda r, b, cc: ([], [], [jnp.sum(r[0], axis=0, keepdims=True)]),
                           all_rows + pad_rows, all_rows + pad_rows, rows=[_full(dmod_rows)], tot_outs=[(1, ada_cols)])

    partial_of = {"norm1_g": g_norm1, "q_latent_g": g_gq, "kv_latent_g": g_gkv, "qk_norm_q_g": _head_dims(g_gqn),
                  "qk_norm_k_g": _head_dims(g_gkn), "conv_b": g_conv_b, "conv_ln_g": g_ln_g, "conv_ln_b": g_ln_b, "norm2_g": g_norm2}
    names = [n for n in REPLICATED if n != "b_ada"]
    pieces = [_pad_lanes(partial_of[n], -(-partial_of[n].shape[1] // LANES) * LANES) for n in names] + [g_b_cols, sq_err]
    widths = [p.shape[1] for p in pieces]
    small = jnp.concatenate(pieces, axis=1)
    small = _pad_lanes(small, -(-small.shape[1] // (8 * LANES)) * 8 * LANES).reshape(-1, LANES)
    (small_s,) = _exchange("gather_small_grads", [(small, True)])
    small_s = small_s.reshape(N_DEV, 1, -1)
    parts = {}
    off = 0
    for n, wd in zip(names, widths):
        parts[n] = small_s[:, :, off:off + vec[n].shape[1]]
        off += wd
    parts["b_ada"] = small_s[:, 0, off:off + ada_cols].reshape(1, 1, N_DEV * ada_cols)
    loss = jnp.sum(small_s[:, 0, off + ada_cols]) * (0.5 / d)
    g_in_mine = _sum_parts("sum_w_in", p_in).T
    parts.update({"w_ada": g_ada[None], "w_in": g_in_mine[None], "w_uq": p_uq, "w_ukv": p_ukv, "w_o_mla": p_o,
                  "conv_w": p_conv_w, "w_pw_out": p_pw, "w_out": p_out, "w_ff1": p_ff1, "w_ff2": p_ff2})
    transposed = ("w_o_mla", "w_pw_out", "w_ff1")

    grad_out, delta_out, m_out, v_out = [], [], [], []
    for n in WEIGHTS:
        shape2 = local[n].shape if local[n].ndim == 2 else (1, local[n].shape[0])
        g_w, d_w, n_m, n_v = _adamw("adamw_" + n, local[n].reshape(shape2), parts[n], given["m_" + n].reshape(shape2),
                                    given["v_" + n].reshape(shape2), transposed=n in transposed)
        full_shape = given[n].shape
        grad_out.append(g_w.reshape(full_shape))
        delta_out.append(d_w.reshape(full_shape))
        m_out.append(n_m.reshape(full_shape))
        v_out.append(n_v.reshape(full_shape))
    return (loss, grad_x.reshape(n_b, seq, d), *grad_out, *delta_out, *m_out, *v_out)
```
